```python
import math
import jax, jax.numpy as jnp
from jax import lax
import numpy as np

D_MODEL = 1024
BATCH = 8
SEQ = 4096
DEPTH = 1

MLA_HEADS = 8
MLA_NOPE_DIM = 64
MLA_ROPE_DIM = 32
MLA_V_DIM = 64
MLA_Q_RANK = 384
MLA_KV_RANK = 128
MLA_WIDTH = MLA_HEADS * MLA_V_DIM
MLA_QK_DIM = MLA_NOPE_DIM + MLA_ROPE_DIM
ROPE_THETA = 10000.0
Q_BLOCK = 128
HGRN_HEADS = 4
HGRN_HEAD_DIM = 128
HGRN_WIDTH = HGRN_HEADS * HGRN_HEAD_DIM
HGRN_CHUNK = 64
D_MIX = MLA_WIDTH + HGRN_WIDTH
D_IN_PROJ = MLA_Q_RANK + MLA_KV_RANK + MLA_ROPE_DIM + 4 * HGRN_WIDTH
D_FF = -(-8 * D_MODEL // (3 * 256)) * 256
EPS = 1e-6

kernel_name = 'hymba_mla_hgrn2_sandwich_block'


def rmsnorm(x, w):
    x32 = x.astype(jnp.float32)
    inv = lax.rsqrt(jnp.mean(x32 * x32, axis=-1, keepdims=True) + EPS)
    return (x32 * inv).astype(x.dtype) * w


def rope_cos_sin(positions, dtype):
    inv_freq = 1.0 / (ROPE_THETA ** (jnp.arange(0, MLA_ROPE_DIM, 2, dtype=jnp.float32) / MLA_ROPE_DIM))
    ang = positions.astype(jnp.float32)[..., None] * inv_freq
    return jnp.cos(ang).astype(dtype), jnp.sin(ang).astype(dtype)


def apply_rope(x, cos, sin):
    x1, x2 = jnp.split(x, 2, axis=-1)
    return jnp.concatenate([x1 * cos - x2 * sin, x1 * sin + x2 * cos], axis=-1)


def mla_mixer(c_q, c_kv, k_rope_raw, positions, q_norm_w, w_uq, kv_norm_w, w_ukv, out_norm_w):
    B, S, _ = c_q.shape
    nb = S // Q_BLOCK
    q = jnp.einsum('bsr,rhd->bshd', rmsnorm(c_q, q_norm_w), w_uq)
    q_nope, q_rope = q[..., :MLA_NOPE_DIM], q[..., MLA_NOPE_DIM:]
    kv = jnp.einsum('bsr,rhd->bshd', rmsnorm(c_kv, kv_norm_w), w_ukv)
    k_nope, v = kv[..., :MLA_NOPE_DIM], kv[..., MLA_NOPE_DIM:]
    cos, sin = rope_cos_sin(positions, q.dtype)
    q_rope = apply_rope(q_rope, cos[:, :, None, :], sin[:, :, None, :])
    k_rope = apply_rope(k_rope_raw, cos, sin)
    scale = MLA_QK_DIM ** -0.5
    qn_b = q_nope.reshape(B, nb, Q_BLOCK, MLA_HEADS, MLA_NOPE_DIM).transpose(1, 0, 2, 3, 4)
    qr_b = q_rope.reshape(B, nb, Q_BLOCK, MLA_HEADS, MLA_ROPE_DIM).transpose(1, 0, 2, 3, 4)
    key_idx = jnp.arange(S)

    def block(args):
        qn, qr, blk = args
        s = (jnp.einsum('bqhd,bkhd->bhqk', qn, k_nope)
             + jnp.einsum('bqhd,bkd->bhqk', qr, k_rope)).astype(jnp.float32) * scale
        q_idx = blk * Q_BLOCK + jnp.arange(Q_BLOCK)
        mask = key_idx[None, :] <= q_idx[:, None]
        p = jax.nn.softmax(jnp.where(mask, s, -jnp.inf), axis=-1).astype(v.dtype)
        return jnp.einsum('bhqk,bkhd->bqhd', p, v)

    o = lax.map(block, (qn_b, qr_b, jnp.arange(nb)))
    o = o.transpose(1, 0, 2, 3, 4).reshape(B, S, MLA_HEADS, MLA_V_DIM)
    o = rmsnorm(o, out_norm_w.reshape(MLA_HEADS, MLA_V_DIM))
    return o.reshape(B, S, MLA_WIDTH)


def hgrn2_mixer(q_raw, f_raw, i_raw, g_raw, lb, out_norm_w):
    B, S, _ = q_raw.shape
    H, D, C = HGRN_HEADS, HGRN_HEAD_DIM, HGRN_CHUNK
    nc = S // C
    lb32 = lb.astype(jnp.float32)
    f = lb32 + (1.0 - lb32) * jax.nn.sigmoid(f_raw.astype(jnp.float32))
    log_f = jnp.log(f)
    k = 1.0 - f
    q = jax.nn.silu(q_raw.astype(jnp.float32))
    v = i_raw.astype(jnp.float32)

    def to_chunks(t):
        return t.reshape(B, nc, C, H, D).transpose(1, 0, 3, 2, 4)

    causal = jnp.tril(jnp.ones((C, C), dtype=bool))[:, :, None]

    def step(state, inp):
        qc, kc, vc, lfc = inp
        b = jnp.cumsum(lfc, axis=2)
        o_inter = jnp.einsum('bhtk,bhkv->bhtv', qc * jnp.exp(b), state)
        diff = b[:, :, :, None, :] - b[:, :, None, :, :]
        decay = jnp.exp(jnp.where(causal, diff, -jnp.inf))
        a = jnp.einsum('bhtk,bhtsk,bhsk->bhts', qc, decay, kc)
        o_intra = jnp.einsum('bhts,bhsv->bhtv', a, vc)
        b_last = b[:, :, -1:, :]
        k_dec = kc * jnp.exp(b_last - b)
        state = jnp.exp(b_last[:, :, 0, :])[..., None] * state + jnp.einsum('bhsk,bhsv->bhkv', k_dec, vc)
        return state, o_inter + o_intra

    s0 = jnp.zeros((B, H, D, D), jnp.float32)
    _, o = lax.scan(step, s0, (to_chunks(q), to_chunks(k), to_chunks(v), to_chunks(log_f)))
    o = o.transpose(1, 0, 3, 2, 4).reshape(B, S, H, D)
    o = rmsnorm(o, out_norm_w.reshape(H, D).astype(jnp.float32))
    o = o.reshape(B, S, HGRN_WIDTH) * jax.nn.silu(g_raw.astype(jnp.float32))
    return o.astype(q_raw.dtype)


def _fwd_setup_inputs(seed: int = 0) -> dict:
    key = jax.random.key(seed)
    ks = jax.random.split(key, 24)
    nrm = lambda k, shape, fan_in: jax.random.normal(k, shape, jnp.float32) * fan_in ** -0.5
    gain = lambda k, shape: 1.0 + 0.02 * jax.random.normal(k, shape, jnp.float32)
    x = jax.random.normal(ks[0], (BATCH, SEQ, D_MODEL), jnp.float32)
    offset = jax.random.randint(ks[1], (BATCH, 1), 0, 2048, dtype=jnp.int32)
    positions = (offset + jnp.arange(SEQ, dtype=jnp.int32)[None, :]).astype(jnp.int32)
    lb_base = jnp.concatenate([-jnp.ones((1, HGRN_WIDTH), jnp.float32),
                               jnp.ones((DEPTH, HGRN_WIDTH), jnp.float32)], axis=0)
    hgrn_lb_logits = lb_base + 0.1 * jax.random.normal(ks[2], (DEPTH + 1, HGRN_WIDTH), jnp.float32)
    return {
        'x': x,
        'positions': positions,
        'attn_pre_norm': gain(ks[3], (DEPTH, D_MODEL)),
        'w_in': nrm(ks[4], (DEPTH, D_MODEL, D_IN_PROJ), D_MODEL),
        'mla_q_norm': gain(ks[5], (DEPTH, MLA_Q_RANK)),
        'mla_w_uq': nrm(ks[6], (DEPTH, MLA_Q_RANK, MLA_HEADS, MLA_QK_DIM), MLA_Q_RANK),
        'mla_kv_norm': gain(ks[7], (DEPTH, MLA_KV_RANK)),
        'mla_w_ukv': nrm(ks[8], (DEPTH, MLA_KV_RANK, MLA_HEADS, MLA_NOPE_DIM + MLA_V_DIM), MLA_KV_RANK),
        'mla_out_norm': gain(ks[9], (DEPTH, MLA_WIDTH)),
        'hgrn_lb_logits': hgrn_lb_logits,
        'hgrn_out_norm': gain(ks[10], (DEPTH, HGRN_WIDTH)),
        'w_out': nrm(ks[11], (DEPTH, D_MIX, D_MODEL), D_MIX),
        'attn_post_norm': gain(ks[12], (DEPTH, D_MODEL)),
        'ffn_pre_norm': gain(ks[13], (DEPTH, D_MODEL)),
        'w_gate': nrm(ks[14], (DEPTH, D_MODEL, D_FF), D_MODEL),
        'w_up': nrm(ks[15], (DEPTH, D_MODEL, D_FF), D_MODEL),
        'w_down': nrm(ks[16], (DEPTH, D_FF, D_MODEL), D_FF),
        'ffn_post_norm': gain(ks[17], (DEPTH, D_MODEL)),
    }


def _fwd_reference(x, positions, attn_pre_norm, w_in, mla_q_norm, mla_w_uq, mla_kv_norm, mla_w_ukv,
              mla_out_norm, hgrn_lb_logits, hgrn_out_norm, w_out, attn_post_norm, ffn_pre_norm,
              w_gate, w_up, w_down, ffn_post_norm):
    lb_all = jnp.cumsum(jax.nn.softmax(hgrn_lb_logits.astype(jnp.float32), axis=0), axis=0)[:DEPTH]
    s1 = MLA_Q_RANK
    s2 = s1 + MLA_KV_RANK
    s3 = s2 + MLA_ROPE_DIM
    s4 = s3 + HGRN_WIDTH
    s5 = s4 + HGRN_WIDTH
    s6 = s5 + HGRN_WIDTH
    h = x
    for l in range(DEPTH):
        u = rmsnorm(h, attn_pre_norm[l])
        xp = jnp.einsum('bsd,de->bse', u, w_in[l])
        c_q, c_kv, k_rope_raw = xp[..., :s1], xp[..., s1:s2], xp[..., s2:s3]
        hq, hf, hi, hg = xp[..., s3:s4], xp[..., s4:s5], xp[..., s5:s6], xp[..., s6:]
        o_mla = mla_mixer(c_q, c_kv, k_rope_raw, positions, mla_q_norm[l], mla_w_uq[l],
                          mla_kv_norm[l], mla_w_ukv[l], mla_out_norm[l])
        o_hgrn = hgrn2_mixer(hq, hf, hi, hg, lb_all[l], hgrn_out_norm[l])
        mix = jnp.concatenate([o_mla, o_hgrn.astype(o_mla.dtype)], axis=-1)
        h = h + rmsnorm(jnp.einsum('bse,ed->bsd', mix, w_out[l]), attn_post_norm[l])
        z = rmsnorm(h, ffn_pre_norm[l])
        ff = jax.nn.silu(jnp.einsum('bsd,df->bsf', z, w_gate[l])) * jnp.einsum('bsd,df->bsf', z, w_up[l])
        h = h + rmsnorm(jnp.einsum('bsf,fd->bsd', ff, w_down[l]), ffn_post_norm[l])
    return h


import jax as _jax
import jax.numpy as _jnp

TWIN_FORMAT = 'train_step'
FWD_PARAMS = ['x', 'positions', 'attn_pre_norm', 'w_in', 'mla_q_norm', 'mla_w_uq', 'mla_kv_norm', 'mla_w_ukv', 'mla_out_norm', 'hgrn_lb_logits', 'hgrn_out_norm', 'w_out', 'attn_post_norm', 'ffn_pre_norm', 'w_gate', 'w_up', 'w_down', 'ffn_post_norm']
TWIN_WEIGHTS = ['attn_pre_norm', 'w_in', 'mla_q_norm', 'mla_w_uq', 'mla_kv_norm', 'mla_w_ukv', 'mla_out_norm', 'hgrn_lb_logits', 'hgrn_out_norm', 'w_out', 'attn_post_norm', 'ffn_pre_norm', 'w_gate', 'w_up', 'w_down', 'ffn_post_norm']
TWIN_DIFF_INPUT = 'x'
TWIN_INPUTS = ['x', 'positions', 'attn_pre_norm', 'w_in', 'mla_q_norm', 'mla_w_uq', 'mla_kv_norm', 'mla_w_ukv', 'mla_out_norm', 'hgrn_lb_logits', 'hgrn_out_norm', 'w_out', 'attn_post_norm', 'ffn_pre_norm', 'w_gate', 'w_up', 'w_down', 'ffn_post_norm', 'loss_target', 'm_attn_pre_norm', 'm_w_in', 'm_mla_q_norm', 'm_mla_w_uq', 'm_mla_kv_norm', 'm_mla_w_ukv', 'm_mla_out_norm', 'm_hgrn_lb_logits', 'm_hgrn_out_norm', 'm_w_out', 'm_attn_post_norm', 'm_ffn_pre_norm', 'm_w_gate', 'm_w_up', 'm_w_down', 'm_ffn_post_norm', 'v_attn_pre_norm', 'v_w_in', 'v_mla_q_norm', 'v_mla_w_uq', 'v_mla_kv_norm', 'v_mla_w_ukv', 'v_mla_out_norm', 'v_hgrn_lb_logits', 'v_hgrn_out_norm', 'v_w_out', 'v_attn_post_norm', 'v_ffn_pre_norm', 'v_w_gate', 'v_w_up', 'v_w_down', 'v_ffn_post_norm']
TWIN_OUTPUTS = ['loss', 'grad_x', 'grad_attn_pre_norm', 'grad_w_in', 'grad_mla_q_norm', 'grad_mla_w_uq', 'grad_mla_kv_norm', 'grad_mla_w_ukv', 'grad_mla_out_norm', 'grad_hgrn_lb_logits', 'grad_hgrn_out_norm', 'grad_w_out', 'grad_attn_post_norm', 'grad_ffn_pre_norm', 'grad_w_gate', 'grad_w_up', 'grad_w_down', 'grad_ffn_post_norm', 'delta_attn_pre_norm', 'delta_w_in', 'delta_mla_q_norm', 'delta_mla_w_uq', 'delta_mla_kv_norm', 'delta_mla_w_ukv', 'delta_mla_out_norm', 'delta_hgrn_lb_logits', 'delta_hgrn_out_norm', 'delta_w_out', 'delta_attn_post_norm', 'delta_ffn_pre_norm', 'delta_w_gate', 'delta_w_up', 'delta_w_down', 'delta_ffn_post_norm', 'new_m_attn_pre_norm', 'new_m_w_in', 'new_m_mla_q_norm', 'new_m_mla_w_uq', 'new_m_mla_kv_norm', 'new_m_mla_w_ukv', 'new_m_mla_out_norm', 'new_m_hgrn_lb_logits', 'new_m_hgrn_out_norm', 'new_m_w_out', 'new_m_attn_post_norm', 'new_m_ffn_pre_norm', 'new_m_w_gate', 'new_m_w_up', 'new_m_w_down', 'new_m_ffn_post_norm', 'new_v_attn_pre_norm', 'new_v_w_in', 'new_v_mla_q_norm', 'new_v_mla_w_uq', 'new_v_mla_kv_norm', 'new_v_mla_w_ukv', 'new_v_mla_out_norm', 'new_v_hgrn_lb_logits', 'new_v_hgrn_out_norm', 'new_v_w_out', 'new_v_attn_post_norm', 'new_v_ffn_pre_norm', 'new_v_w_gate', 'new_v_w_up', 'new_v_w_down', 'new_v_ffn_post_norm']
TWIN_LEAF_KINDS = {'loss': 'loss', 'grad_x': 'grad_x', 'grad_attn_pre_norm': 'grad_w', 'grad_w_in': 'grad_w', 'grad_mla_q_norm': 'grad_w', 'grad_mla_w_uq': 'grad_w', 'grad_mla_kv_norm': 'grad_w', 'grad_mla_w_ukv': 'grad_w', 'grad_mla_out_norm': 'grad_w', 'grad_hgrn_lb_logits': 'grad_w', 'grad_hgrn_out_norm': 'grad_w', 'grad_w_out': 'grad_w', 'grad_attn_post_norm': 'grad_w', 'grad_ffn_pre_norm': 'grad_w', 'grad_w_gate': 'grad_w', 'grad_w_up': 'grad_w', 'grad_w_down': 'grad_w', 'grad_ffn_post_norm': 'grad_w', 'delta_attn_pre_norm': 'delta_w', 'delta_w_in': 'delta_w', 'delta_mla_q_norm': 'delta_w', 'delta_mla_w_uq': 'delta_w', 'delta_mla_kv_norm': 'delta_w', 'delta_mla_w_ukv': 'delta_w', 'delta_mla_out_norm': 'delta_w', 'delta_hgrn_lb_logits': 'delta_w', 'delta_hgrn_out_norm': 'delta_w', 'delta_w_out': 'delta_w', 'delta_attn_post_norm': 'delta_w', 'delta_ffn_pre_norm': 'delta_w', 'delta_w_gate': 'delta_w', 'delta_w_up': 'delta_w', 'delta_w_down': 'delta_w', 'delta_ffn_post_norm': 'delta_w', 'new_m_attn_pre_norm': 'new_m', 'new_m_w_in': 'new_m', 'new_m_mla_q_norm': 'new_m', 'new_m_mla_w_uq': 'new_m', 'new_m_mla_kv_norm': 'new_m', 'new_m_mla_w_ukv': 'new_m', 'new_m_mla_out_norm': 'new_m', 'new_m_hgrn_lb_logits': 'new_m', 'new_m_hgrn_out_norm': 'new_m', 'new_m_w_out': 'new_m', 'new_m_attn_post_norm': 'new_m', 'new_m_ffn_pre_norm': 'new_m', 'new_m_w_gate': 'new_m', 'new_m_w_up': 'new_m', 'new_m_w_down': 'new_m', 'new_m_ffn_post_norm': 'new_m', 'new_v_attn_pre_norm': 'new_v', 'new_v_w_in': 'new_v', 'new_v_mla_q_norm': 'new_v', 'new_v_mla_w_uq': 'new_v', 'new_v_mla_kv_norm': 'new_v', 'new_v_mla_w_ukv': 'new_v', 'new_v_mla_out_norm': 'new_v', 'new_v_hgrn_lb_logits': 'new_v', 'new_v_hgrn_out_norm': 'new_v', 'new_v_w_out': 'new_v', 'new_v_attn_post_norm': 'new_v', 'new_v_ffn_pre_norm': 'new_v', 'new_v_w_gate': 'new_v', 'new_v_w_up': 'new_v', 'new_v_w_down': 'new_v', 'new_v_ffn_post_norm': 'new_v'}


def _forward(args):
    return _fwd_reference(*[args[k] for k in FWD_PARAMS])


def _output_shape():
    out = _jax.eval_shape(lambda: _forward(_fwd_setup_inputs(0)))
    return out.shape, out.dtype

N_MICROBATCH = 1
ADAM_LR = 0.001
ADAM_B1 = 0.9
ADAM_B2 = 0.999
ADAM_EPS = 1e-08
ADAM_WD = 0.01
ADAM_STEP = 10
PER_EXAMPLE_BATCH_AXIS = {'x': 0, 'positions': 0, 'loss_target': 0}
SHARED_INPUTS = []
_WEIGHT_DTYPES = {'attn_pre_norm': _jnp.float32, 'w_in': _jnp.float32, 'mla_q_norm': _jnp.float32, 'mla_w_uq': _jnp.float32, 'mla_kv_norm': _jnp.float32, 'mla_w_ukv': _jnp.float32, 'mla_out_norm': _jnp.float32, 'hgrn_lb_logits': _jnp.float32, 'hgrn_out_norm': _jnp.float32, 'w_out': _jnp.float32, 'attn_post_norm': _jnp.float32, 'ffn_pre_norm': _jnp.float32, 'w_gate': _jnp.float32, 'w_up': _jnp.float32, 'w_down': _jnp.float32, 'ffn_post_norm': _jnp.float32}
MOMENT_SCALE = {'attn_pre_norm': 8.244043e-01, 'w_in': 5.075470e-01, 'mla_q_norm': 1.129345e+00, 'mla_w_uq': 7.082488e-01, 'mla_kv_norm': 3.712269e+00, 'mla_w_ukv': 9.424115e-01, 'mla_out_norm': 1.022415e+00, 'hgrn_lb_logits': 8.685582e-03, 'hgrn_out_norm': 4.331149e-01, 'w_out': 7.760781e-01, 'attn_post_norm': 3.187725e+01, 'ffn_pre_norm': 6.919823e-01, 'w_gate': 2.212100e-01, 'w_up': 3.428690e-01, 'w_down': 5.617646e-01, 'ffn_post_norm': 3.183107e+01}


def _to_microbatches(a, axis):
    t = _jnp.moveaxis(a, axis, 0)
    t = t.reshape((N_MICROBATCH, t.shape[0] // N_MICROBATCH) + t.shape[1:])
    return _jnp.moveaxis(t, 1, axis + 1)


def setup_inputs(seed: int = 0) -> dict:
    inp = _fwd_setup_inputs(seed)
    key = _jax.random.fold_in(_jax.random.key(seed), 7919)
    shape, _ = _output_shape()
    out = dict(inp)
    out["loss_target"] = _jax.random.normal(_jax.random.fold_in(key, 0), shape, _jnp.float32)
    for i, name in enumerate(TWIN_WEIGHTS):
        w = inp[name].astype(_jnp.float32)
        if MOMENT_SCALE is None:
            s = _jnp.sqrt(_jnp.mean(_jnp.square(w)) + 1e-30)
        else:
            s = MOMENT_SCALE[name]
        km, kv = _jax.random.split(_jax.random.fold_in(key, i + 1))
        out[name] = w
        out["m_" + name] = s * _jax.random.normal(km, w.shape, _jnp.float32)
        out["v_" + name] = (s * s) * _jax.random.uniform(kv, w.shape, _jnp.float32, 0.5, 1.5)
    if N_MICROBATCH > 1:
        for name, axis in PER_EXAMPLE_BATCH_AXIS.items():
            out[name] = _to_microbatches(out[name], axis)
    return {'x': out['x'], 'positions': out['positions'], 'attn_pre_norm': out['attn_pre_norm'], 'w_in': out['w_in'], 'mla_q_norm': out['mla_q_norm'], 'mla_w_uq': out['mla_w_uq'], 'mla_kv_norm': out['mla_kv_norm'], 'mla_w_ukv': out['mla_w_ukv'], 'mla_out_norm': out['mla_out_norm'], 'hgrn_lb_logits': out['hgrn_lb_logits'], 'hgrn_out_norm': out['hgrn_out_norm'], 'w_out': out['w_out'], 'attn_post_norm': out['attn_post_norm'], 'ffn_pre_norm': out['ffn_pre_norm'], 'w_gate': out['w_gate'], 'w_up': out['w_up'], 'w_down': out['w_down'], 'ffn_post_norm': out['ffn_post_norm'], 'loss_target': out['loss_target'], 'm_attn_pre_norm': out['m_attn_pre_norm'], 'm_w_in': out['m_w_in'], 'm_mla_q_norm': out['m_mla_q_norm'], 'm_mla_w_uq': out['m_mla_w_uq'], 'm_mla_kv_norm': out['m_mla_kv_norm'], 'm_mla_w_ukv': out['m_mla_w_ukv'], 'm_mla_out_norm': out['m_mla_out_norm'], 'm_hgrn_lb_logits': out['m_hgrn_lb_logits'], 'm_hgrn_out_norm': out['m_hgrn_out_norm'], 'm_w_out': out['m_w_out'], 'm_attn_post_norm': out['m_attn_post_norm'], 'm_ffn_pre_norm': out['m_ffn_pre_norm'], 'm_w_gate': out['m_w_gate'], 'm_w_up': out['m_w_up'], 'm_w_down': out['m_w_down'], 'm_ffn_post_norm': out['m_ffn_post_norm'], 'v_attn_pre_norm': out['v_attn_pre_norm'], 'v_w_in': out['v_w_in'], 'v_mla_q_norm': out['v_mla_q_norm'], 'v_mla_w_uq': out['v_mla_w_uq'], 'v_mla_kv_norm': out['v_mla_kv_norm'], 'v_mla_w_ukv': out['v_mla_w_ukv'], 'v_mla_out_norm': out['v_mla_out_norm'], 'v_hgrn_lb_logits': out['v_hgrn_lb_logits'], 'v_hgrn_out_norm': out['v_hgrn_out_norm'], 'v_w_out': out['v_w_out'], 'v_attn_post_norm': out['v_attn_post_norm'], 'v_ffn_pre_norm': out['v_ffn_pre_norm'], 'v_w_gate': out['v_w_gate'], 'v_w_up': out['v_w_up'], 'v_w_down': out['v_w_down'], 'v_ffn_post_norm': out['v_ffn_post_norm']}


def _loss(weights, diff, rest, loss_target):
    with _jax.named_scope("forward"):
        args = {**rest, TWIN_DIFF_INPUT: diff, **{k: w.astype(_WEIGHT_DTYPES[k]) for k, w in weights.items()}}
        y = _forward(args)
    with _jax.named_scope("loss_head"):
        err = _jnp.square(y.astype(_jnp.float32) - loss_target)
        return 0.5 * _jnp.sum(_jnp.mean(err, axis=-1)) if err.ndim else 0.5 * err


def _adamw(w, g, m, v):
    m = ADAM_B1 * m + (1.0 - ADAM_B1) * g
    v = ADAM_B2 * v + (1.0 - ADAM_B2) * _jnp.square(g)
    m_hat = m / (1.0 - ADAM_B1 ** ADAM_STEP)
    v_hat = v / (1.0 - ADAM_B2 ** ADAM_STEP)
    delta = -ADAM_LR * (m_hat / (_jnp.sqrt(v_hat) + ADAM_EPS) + ADAM_WD * w)
    return delta, m, v


def reference(x, positions, attn_pre_norm, w_in, mla_q_norm, mla_w_uq, mla_kv_norm, mla_w_ukv, mla_out_norm, hgrn_lb_logits, hgrn_out_norm, w_out, attn_post_norm, ffn_pre_norm, w_gate, w_up, w_down, ffn_post_norm, loss_target, m_attn_pre_norm, m_w_in, m_mla_q_norm, m_mla_w_uq, m_mla_kv_norm, m_mla_w_ukv, m_mla_out_norm, m_hgrn_lb_logits, m_hgrn_out_norm, m_w_out, m_attn_post_norm, m_ffn_pre_norm, m_w_gate, m_w_up, m_w_down, m_ffn_post_norm, v_attn_pre_norm, v_w_in, v_mla_q_norm, v_mla_w_uq, v_mla_kv_norm, v_mla_w_ukv, v_mla_out_norm, v_hgrn_lb_logits, v_hgrn_out_norm, v_w_out, v_attn_post_norm, v_ffn_pre_norm, v_w_gate, v_w_up, v_w_down, v_ffn_post_norm):
    given = dict(x=x, positions=positions, attn_pre_norm=attn_pre_norm, w_in=w_in, mla_q_norm=mla_q_norm, mla_w_uq=mla_w_uq, mla_kv_norm=mla_kv_norm, mla_w_ukv=mla_w_ukv, mla_out_norm=mla_out_norm, hgrn_lb_logits=hgrn_lb_logits, hgrn_out_norm=hgrn_out_norm, w_out=w_out, attn_post_norm=attn_post_norm, ffn_pre_norm=ffn_pre_norm, w_gate=w_gate, w_up=w_up, w_down=w_down, ffn_post_norm=ffn_post_norm, loss_target=loss_target, m_attn_pre_norm=m_attn_pre_norm, m_w_in=m_w_in, m_mla_q_norm=m_mla_q_norm, m_mla_w_uq=m_mla_w_uq, m_mla_kv_norm=m_mla_kv_norm, m_mla_w_ukv=m_mla_w_ukv, m_mla_out_norm=m_mla_out_norm, m_hgrn_lb_logits=m_hgrn_lb_logits, m_hgrn_out_norm=m_hgrn_out_norm, m_w_out=m_w_out, m_attn_post_norm=m_attn_post_norm, m_ffn_pre_norm=m_ffn_pre_norm, m_w_gate=m_w_gate, m_w_up=m_w_up, m_w_down=m_w_down, m_ffn_post_norm=m_ffn_post_norm, v_attn_pre_norm=v_attn_pre_norm, v_w_in=v_w_in, v_mla_q_norm=v_mla_q_norm, v_mla_w_uq=v_mla_w_uq, v_mla_kv_norm=v_mla_kv_norm, v_mla_w_ukv=v_mla_w_ukv, v_mla_out_norm=v_mla_out_norm, v_hgrn_lb_logits=v_hgrn_lb_logits, v_hgrn_out_norm=v_hgrn_out_norm, v_w_out=v_w_out, v_attn_post_norm=v_attn_post_norm, v_ffn_pre_norm=v_ffn_pre_norm, v_w_gate=v_w_gate, v_w_up=v_w_up, v_w_down=v_w_down, v_ffn_post_norm=v_ffn_post_norm)
    weights = {n: given[n] for n in TWIN_WEIGHTS}
    shared = {n: given[n] for n in SHARED_INPUTS}
    per_example = {n: given[n] for n in ['x', 'positions']}
    grad_fn = _jax.value_and_grad(_loss, argnums=(0, 1))

    def one_microbatch(ex, loss_target):
        ex = dict(ex)
        diff = ex.pop(TWIN_DIFF_INPUT)
        return grad_fn(weights, diff, {**shared, **ex}, loss_target)

    if N_MICROBATCH == 1:
        loss, (grad_w, grad_x) = one_microbatch(per_example, given["loss_target"])
    else:
        def body(carry, xs):
            loss_sum, grad_sum = carry
            l_k, (gw_k, gx_k) = one_microbatch(xs[0], xs[1])
            with _jax.named_scope("update"):
                return (loss_sum + l_k, _jax.tree.map(_jnp.add, grad_sum, gw_k)), gx_k

        init = (_jnp.zeros((), _jnp.float32), _jax.tree.map(_jnp.zeros_like, weights))
        (loss, grad_w), grad_x = _jax.lax.scan(body, init, (per_example, given["loss_target"]))
    with _jax.named_scope("update"):
        delta_w, new_m, new_v = {}, {}, {}
        for n in TWIN_WEIGHTS:
            delta_w[n], new_m[n], new_v[n] = _adamw(weights[n], grad_w[n], given["m_" + n], given["v_" + n])
    return (loss, grad_x, *[grad_w[n] for n in TWIN_WEIGHTS], *[delta_w[n] for n in TWIN_WEIGHTS],
            *[new_m[n] for n in TWIN_WEIGHTS], *[new_v[n] for n in TWIN_WEIGHTS])
```

```python
import jax
import jax.numpy as jnp
from jax import lax
from jax.experimental import pallas as pl
from jax.experimental.pallas import tpu as pltpu

BF = jnp.bfloat16
F32 = jnp.float32
MESH = pl.DeviceIdType.MESH

N_DEV = 8
D = 1024
EPS = 1e-6
ROPE_THETA = 10000.0
N_HEADS = 8
HB = 128
NOPE = 64
ROPE = 32
V_DIM = 64
QK_DIM = NOPE + ROPE
Q_RANK = 384
KV_RANK = 128
KR_PAD = 128
MLA_IN = Q_RANK + KV_RANK + KR_PAD
G_HEADS = 4
G_DIM = 128
G_W = G_HEADS * G_DIM
CHUNK = 64
SUB = 16
XP_W = MLA_IN + 4 * G_W
IN_SH = 324
IN_W = N_DEV * IN_SH
FF_SH = 352
FF_PAD = 384
MIX_W = N_HEADS * HB + G_W

ADAM_LR = 0.001
ADAM_B1 = 0.9
ADAM_B2 = 0.999
ADAM_EPS = 1e-08
ADAM_WD = 0.01
ADAM_STEP = 10

_TM = 512
_TQ = 512
_VMEM_LIMIT = 56 * 1024 * 1024
NEG = -1e30


def _dot(a, b):
    return jnp.dot(a.astype(BF), b.astype(BF), preferred_element_type=F32)


def _dot_nt(a, b):
    return lax.dot_general(a.astype(BF), b.astype(BF), (((1,), (1,)), ((), ())), preferred_element_type=F32)


def _dot_tn(a, b):
    return lax.dot_general(a.astype(BF), b.astype(BF), (((0,), (0,)), ((), ())), preferred_element_type=F32)


def _sigmoid(x):
    return 1.0 / (1.0 + jnp.exp(-x))


def _rms(x, n):
    r = lax.rsqrt(jnp.sum(x * x, -1, keepdims=True) * (1.0 / n) + EPS)
    return x * r, r


def _rms_bwd(nx, r, g, dy, n):
    dg = jnp.sum(dy * nx, 0, keepdims=True)
    dn = dy * g
    dx = r * (dn - nx * (jnp.sum(dn * nx, -1, keepdims=True) * (1.0 / n)))
    return dx, dg


def _adamw(w, g, m, v):
    m2 = ADAM_B1 * m + (1.0 - ADAM_B1) * g
    v2 = ADAM_B2 * v + (1.0 - ADAM_B2) * (g * g)
    m_hat = m2 / (1.0 - ADAM_B1 ** ADAM_STEP)
    v_hat = v2 / (1.0 - ADAM_B2 ** ADAM_STEP)
    delta = -ADAM_LR * (m_hat / (jnp.sqrt(v_hat) + ADAM_EPS) + ADAM_WD * w)
    return delta, m2, v2


def _pcall(body, name, grid, in_specs, out_specs, out_shape, scratch=()):
    return pl.pallas_call(
        body, name=name, grid=grid, in_specs=in_specs, out_specs=out_specs, out_shape=out_shape,
        scratch_shapes=list(scratch),
        compiler_params=pltpu.CompilerParams(
            dimension_semantics=("arbitrary",) * len(grid), vmem_limit_bytes=_VMEM_LIMIT))


def _full(shape):
    return pl.BlockSpec(shape, lambda *_: (0,) * len(shape))


def _rows(tm, n):
    return pl.BlockSpec((tm, n), lambda i, *_: (i, 0))


def _sds(shape, dtype=F32):
    return jax.ShapeDtypeStruct(shape, dtype)


def _peer(k, x, y, c):
    px = 1 - x if (k >> 2) & 1 else x
    py = 1 - y if (k >> 1) & 1 else y
    pc = 1 - c if k & 1 else c
    return px, py, pc


def _all_gather_weights(w_in, w_uq, w_out, w_gate, w_up, w_down):
    shapes = [(D, IN_SH), (Q_RANK // N_DEV, N_HEADS * QK_DIM), (D // N_DEV, D), (D, FF_PAD), (D, FF_PAD), (FF_PAD, D)]
    n_w = len(shapes)

    def body(win, wuq, wout, wg, wu, wd, gin, guq, gout, gg, gu, gd,
             sin_, suq, sout, sg, su, sd, send_sems, recv_sems, loc_sems):
        x, y, c = lax.axis_index("x"), lax.axis_index("y"), lax.axis_index("c")
        me = 4 * x + 2 * y + c
        sin_[...] = win[...].astype(BF)
        suq[...] = wuq[...].astype(BF)
        sout[...] = wout[...].astype(BF)
        sg[...] = jnp.zeros(sg.shape, BF)
        sg[:, 0:FF_SH] = wg[...].astype(BF)
        su[...] = jnp.zeros(su.shape, BF)
        su[:, 0:FF_SH] = wu[...].astype(BF)
        sd[...] = jnp.zeros(sd.shape, BF)
        sd[0:FF_SH, :] = wd[...].astype(BF)
        pairs = [(sin_, gin), (suq, guq), (sout, gout), (sg, gg), (su, gu), (sd, gd)]

        def remote(w, k):
            src, dst = pairs[w]
            return pltpu.make_async_remote_copy(
                src_ref=src, dst_ref=dst.at[me], send_sem=send_sems.at[w, k - 1], recv_sem=recv_sems.at[w, k - 1],
                device_id=_peer(k, x, y, c), device_id_type=MESH)

        def arrival(w, k):
            src, dst = pairs[w]
            px, py, pc = _peer(k, x, y, c)
            return pltpu.make_async_remote_copy(
                src_ref=src, dst_ref=dst.at[4 * px + 2 * py + pc], send_sem=send_sems.at[w, k - 1],
                recv_sem=recv_sems.at[w, k - 1], device_id=(px, py, pc), device_id_type=MESH)

        local = [pltpu.make_async_copy(pairs[w][0], pairs[w][1].at[me], loc_sems.at[w]) for w in range(n_w)]
        for w in range(n_w):
            local[w].start()
            for k in range(1, N_DEV):
                remote(w, k).start()
        for w in range(n_w):
            local[w].wait()
            for k in range(1, N_DEV):
                arrival(w, k).wait_recv()
        for w in range(n_w):
            for k in range(1, N_DEV):
                remote(w, k).wait_send()

    vm = pl.BlockSpec(memory_space=pltpu.VMEM)
    hbm = pl.BlockSpec(memory_space=pl.ANY)
    return pl.pallas_call(
        body, name="ag_weights",
        in_specs=[vm] * n_w, out_specs=[hbm] * n_w,
        out_shape=[_sds((N_DEV,) + s, BF) for s in shapes],
        scratch_shapes=[pltpu.VMEM(s, BF) for s in shapes] + [
            pltpu.SemaphoreType.DMA((n_w, N_DEV - 1)), pltpu.SemaphoreType.DMA((n_w, N_DEV - 1)),
            pltpu.SemaphoreType.DMA((n_w,))],
        compiler_params=pltpu.CompilerParams(vmem_limit_bytes=_VMEM_LIMIT),
    )(w_in, w_uq, w_out, w_gate, w_up, w_down)


def _reduce_scatter_grads(parts):
    n_w = len(parts)

    def body(*refs):
        srcs, dsts = refs[:n_w], refs[n_w:2 * n_w]
        send_sems, recv_sems, loc_sems = refs[2 * n_w:]
        x, y, c = lax.axis_index("x"), lax.axis_index("y"), lax.axis_index("c")
        me = 4 * x + 2 * y + c

        def remote(w, k):
            px, py, pc = _peer(k, x, y, c)
            return pltpu.make_async_remote_copy(
                src_ref=srcs[w].at[4 * px + 2 * py + pc], dst_ref=dsts[w].at[me],
                send_sem=send_sems.at[w, k - 1], recv_sem=recv_sems.at[w, k - 1],
                device_id=(px, py, pc), device_id_type=MESH)

        def arrival(w, k):
            px, py, pc = _peer(k, x, y, c)
            return pltpu.make_async_remote_copy(
                src_ref=srcs[w].at[me], dst_ref=dsts[w].at[4 * px + 2 * py + pc],
                send_sem=send_sems.at[w, k - 1], recv_sem=recv_sems.at[w, k - 1],
                device_id=(px, py, pc), device_id_type=MESH)

        local = [pltpu.make_async_copy(srcs[w].at[me], dsts[w].at[me], loc_sems.at[w]) for w in range(n_w)]
        for w in range(n_w):
            local[w].start()
            for k in range(1, N_DEV):
                remote(w, k).start()
        for w in range(n_w):
            local[w].wait()
            for k in range(1, N_DEV):
                arrival(w, k).wait_recv()
        for w in range(n_w):
            for k in range(1, N_DEV):
                remote(w, k).wait_send()

    hbm = pl.BlockSpec(memory_space=pl.ANY)
    return pl.pallas_call(
        body, name="rs_grads",
        in_specs=[hbm] * n_w, out_specs=[hbm] * n_w,
        out_shape=[_sds(p.shape, p.dtype) for p in parts],
        scratch_shapes=[pltpu.SemaphoreType.DMA((n_w, N_DEV - 1)), pltpu.SemaphoreType.DMA((n_w, N_DEV - 1)),
                        pltpu.SemaphoreType.DMA((n_w,))],
    )(*parts)


def _small_allreduce_adam(grads, ws, ms, vs):
    n_p = len(grads)
    offs, rows = [], 0
    for g in grads:
        offs.append(rows)
        rows += -(-g.shape[0] // 8) * 8

    def body(*refs):
        g_refs, w_refs = refs[:n_p], refs[n_p:2 * n_p]
        m_refs, v_refs = refs[2 * n_p:3 * n_p], refs[3 * n_p:4 * n_p]
        outs = refs[4 * n_p:8 * n_p]
        pk, rall, send_sems, recv_sems, loc_sem = refs[8 * n_p:]
        x, y, c = lax.axis_index("x"), lax.axis_index("y"), lax.axis_index("c")
        me = 4 * x + 2 * y + c
        pk[...] = jnp.zeros(pk.shape, F32)
        for p in range(n_p):
            r, n = g_refs[p].shape
            pk[offs[p]:offs[p] + r, 0:n] = g_refs[p][...]

        def remote(k):
            return pltpu.make_async_remote_copy(
                src_ref=pk, dst_ref=rall.at[me], send_sem=send_sems.at[k - 1], recv_sem=recv_sems.at[k - 1],
                device_id=_peer(k, x, y, c), device_id_type=MESH)

        def arrival(k):
            px, py, pc = _peer(k, x, y, c)
            return pltpu.make_async_remote_copy(
                src_ref=pk, dst_ref=rall.at[4 * px + 2 * py + pc], send_sem=send_sems.at[k - 1],
                recv_sem=recv_sems.at[k - 1], device_id=(px, py, pc), device_id_type=MESH)

        local = pltpu.make_async_copy(pk, rall.at[me], loc_sem)
        local.start()
        for k in range(1, N_DEV):
            remote(k).start()
        local.wait()
        for k in range(1, N_DEV):
            arrival(k).wait_recv()
        for k in range(1, N_DEV):
            remote(k).wait_send()
        for p in range(n_p):
            r, n = g_refs[p].shape
            sl = (slice(offs[p], offs[p] + r), slice(0, n))
            g = rall[(0,) + sl]
            for j in range(1, N_DEV):
                g = g + rall[(j,) + sl]
            delta, m2, v2 = _adamw(w_refs[p][...], g, m_refs[p][...], v_refs[p][...])
            outs[p][...] = g
            outs[n_p + p][...] = delta
            outs[2 * n_p + p][...] = m2
            outs[3 * n_p + p][...] = v2

    vm = pl.BlockSpec(memory_space=pltpu.VMEM)
    res = pl.pallas_call(
        body, name="ar_small_adam",
        in_specs=[vm] * (4 * n_p), out_specs=[vm] * (4 * n_p),
        out_shape=[_sds(g.shape) for g in grads] * 4,
        scratch_shapes=[pltpu.VMEM((rows, D), F32), pltpu.VMEM((N_DEV, rows, D), F32),
                        pltpu.SemaphoreType.DMA((N_DEV - 1,)), pltpu.SemaphoreType.DMA((N_DEV - 1,)),
                        pltpu.SemaphoreType.DMA],
        compiler_params=pltpu.CompilerParams(vmem_limit_bytes=_VMEM_LIMIT),
    )(*grads, *ws, *ms, *vs)
    return res[:n_p], res[n_p:2 * n_p], res[2 * n_p:3 * n_p], res[3 * n_p:]


def _shard_adam(name, parts, w, m, v, tr):
    a0, b0 = w.shape
    b = parts.shape[2]

    def body(p_ref, w_ref, m_ref, v_ref, g_out, d_out, m_out, v_out):
        g = p_ref[0].astype(F32)
        for j in range(1, N_DEV):
            g = g + p_ref[j].astype(F32)
        g = g[:, 0:b0]
        delta, m2, v2 = _adamw(w_ref[...], g, m_ref[...], v_ref[...])
        g_out[...] = g
        d_out[...] = delta
        m_out[...] = m2
        v_out[...] = v2

    blk = pl.BlockSpec((tr, b0), lambda i: (i, 0))
    return _pcall(
        body, name, (a0 // tr,),
        [pl.BlockSpec((N_DEV, tr, b), lambda i: (0, i, 0)), blk, blk, blk],
        [blk] * 4, [_sds((a0, b0))] * 4)(parts, w, m, v)


def _fwd_in(x, g_pre, w_in_al, tm):
    T = x.shape[0]

    def body(x_ref, g_ref, w_ref, xp_ref, u_ref):
        nx, _ = _rms(x_ref[...], D)
        u = (nx * g_ref[...]).astype(BF)
        u_ref[...] = u
        xp_ref[...] = jnp.dot(u, w_ref[...], preferred_element_type=F32)

    return _pcall(body, "fwd_in", (T // tm,),
                  [_rows(tm, D), _full((1, D)), _full((D, XP_W))],
                  [_rows(tm, XP_W), _rows(tm, D)],
                  [_sds((T, XP_W)), _sds((T, D), BF)])(x, g_pre, w_in_al)


def _rope(blk, ta, tb1, tb2):
    return blk * ta + pltpu.roll(blk, HB - ROPE // 2, 1) * tb1 + pltpu.roll(blk, ROPE // 2, 1) * tb2


def _unrope(d, ta, tb1, tb2):
    return d * ta + pltpu.roll(d * tb1, ROPE // 2, 1) + pltpu.roll(d * tb2, HB - ROPE // 2, 1)


def _mla_prep(xp, tabs, g_q, g_kv, w_uq, w_uk, w_uv, tm):
    T = xp.shape[0]
    W = N_HEADS * HB

    def body(xp_ref, ta_ref, tb1_ref, tb2_ref, gq_ref, gkv_ref, wuq_ref, wuk_ref, wuv_ref, q_ref, k_ref, v_ref):
        ta, tb1, tb2 = ta_ref[...], tb1_ref[...], tb2_ref[...]
        nq, _ = _rms(xp_ref[:, 0:Q_RANK], Q_RANK)
        nkv, _ = _rms(xp_ref[:, Q_RANK:Q_RANK + KV_RANK], KV_RANK)
        nkv = (nkv * gkv_ref[...]).astype(BF)
        qpre = _dot(nq * gq_ref[...], wuq_ref[...])
        kpre = jnp.dot(nkv, wuk_ref[...], preferred_element_type=F32)
        v_ref[...] = jnp.dot(nkv, wuv_ref[...], preferred_element_type=F32).astype(BF)
        kr = _rope(pltpu.roll(xp_ref[:, Q_RANK + KV_RANK:MLA_IN], NOPE, 1), ta, tb1, tb2)
        for h in range(N_HEADS):
            sl = slice(h * HB, (h + 1) * HB)
            q_ref[:, sl] = _rope(qpre[:, sl], ta, tb1, tb2).astype(BF)
            k_ref[:, sl] = (kpre[:, sl] + kr).astype(BF)

    tab = _rows(tm, HB)
    return _pcall(body, "mla_prep", (T // tm,),
                  [_rows(tm, MLA_IN), tab, tab, tab, _full((1, Q_RANK)), _full((1, KV_RANK)),
                   _full((Q_RANK, W)), _full((KV_RANK, W)), _full((KV_RANK, W))],
                  [_rows(tm, W)] * 3, [_sds((T, W), BF)] * 3)(xp, *tabs, g_q, g_kv, w_uq, w_uk, w_uv)


def _flash_fwd(q, k, v, tq):
    T = q.shape[0]
    scale = QK_DIM ** -0.5

    def body(q_ref, k_ref, v_ref, o_ref, lse_ref):
        i = pl.program_id(1)
        qv = q_ref[...]

        def blk(j, carry, masked):
            m, l, acc = carry
            st = pl.multiple_of(j * tq, tq)
            kv = k_ref[pl.ds(st, tq), :]
            vv = v_ref[pl.ds(st, tq), :]
            s = _dot_nt(qv, kv) * scale
            if masked:
                r = lax.broadcasted_iota(jnp.int32, (tq, tq), 0)
                c = lax.broadcasted_iota(jnp.int32, (tq, tq), 1)
                s = jnp.where(c <= r, s, NEG)
            m2 = jnp.maximum(m, jnp.max(s, -1, keepdims=True))
            p = jnp.exp(s - m2)
            a = jnp.exp(m - m2)
            return m2, a * l + jnp.sum(p, -1, keepdims=True), a * acc + _dot(p, vv)

        init = (jnp.full((tq, 1), NEG, F32), jnp.zeros((tq, 1), F32), jnp.zeros((tq, HB), F32))
        carry = lax.fori_loop(0, i, lambda j, cr: blk(j, cr, False), init)
        m, l, acc = blk(i, carry, True)
        o_ref[...] = acc / l
        lse_ref[...] = jnp.broadcast_to(m + jnp.log(l), (tq, HB))

    qs = pl.BlockSpec((tq, HB), lambda h, i: (i, h))
    kvs = pl.BlockSpec((T, HB), lambda h, i: (0, h))
    return _pcall(body, "flash_fwd", (N_HEADS, T // tq), [qs, kvs, kvs], [qs, qs],
                  [_sds((T, N_HEADS * HB))] * 2)(q, k, v)


def _gates(hq, hf, lb):
    sig = _sigmoid(hf)
    f = lb + (1.0 - lb) * sig
    sq = _sigmoid(hq)
    return hq * sq, 1.0 - f, f, jnp.log(f), sig, sq


def _lower_bound(lbl_ref):
    l0, l1 = lbl_ref[0:1, :], lbl_ref[1:2, :]
    mx = jnp.maximum(l0, l1)
    e0, e1 = jnp.exp(l0 - mx), jnp.exp(l1 - mx)
    return e0 / (e0 + e1)


def _split3(x):
    hi = x.astype(BF)
    r1 = x - hi.astype(F32)
    mid = r1.astype(BF)
    lo = (r1 - mid.astype(F32)).astype(BF)
    return hi, mid, lo


def _tri_mm(tri, x):
    hi, mid, lo = _split3(x)
    mm = lambda t: jnp.dot(tri, t, preferred_element_type=F32)
    return mm(hi) + mm(mid) + mm(lo)


def _anchor_mask(i, row, col):
    return ((row >> 4) == i) & (col < SUB * i)


def _diag_mask(d, row, col):
    return (col == row - d) & ((row & (SUB - 1)) >= d)


def _intra(q, k, b, b_s, row, col):
    a = jnp.zeros((CHUNK, CHUNK), F32)
    for i in range(1, CHUNK // SUB):
        b0 = b_s[SUB * i - 1:SUB * i, :]
        qs = q * jnp.exp(jnp.minimum(b - b0, 0.0))
        ks = k * jnp.exp(jnp.minimum(b0 - b, 0.0))
        a = a + jnp.where(_anchor_mask(i, row, col), _dot_nt(qs, ks), 0.0)
    for d in range(SUB):
        ksh = pltpu.roll(k, d, 0) if d else k
        bsh = pltpu.roll(b, d, 0) if d else b
        e = jnp.exp(jnp.minimum(b - bsh, 0.0))
        val = jnp.sum(q * ksh * e, -1, keepdims=True)
        a = a + jnp.where(_diag_mask(d, row, col), val, 0.0)
    return a


def _intra_bwd(q, k, b, b_s, da, row, col):
    dq = jnp.zeros((CHUNK, G_DIM), F32)
    dk = jnp.zeros((CHUNK, G_DIM), F32)
    for i in range(1, CHUNK // SUB):
        b0 = b_s[SUB * i - 1:SUB * i, :]
        eq = jnp.exp(jnp.minimum(b - b0, 0.0))
        ek = jnp.exp(jnp.minimum(b0 - b, 0.0))
        dai = jnp.where(_anchor_mask(i, row, col), da, 0.0)
        dq = dq + _dot(dai, k * ek) * eq
        dk = dk + _dot_tn(dai, q * eq) * ek
    for d in range(SUB):
        ksh = pltpu.roll(k, d, 0) if d else k
        bsh = pltpu.roll(b, d, 0) if d else b
        e = jnp.exp(jnp.minimum(b - bsh, 0.0))
        g = jnp.sum(jnp.where(_diag_mask(d, row, col), da, 0.0), -1, keepdims=True) * e
        dq = dq + g * ksh
        cb = g * q
        dk = dk + (pltpu.roll(cb, CHUNK - d, 0) if d else cb)
    return dq, dk


def _hgrn_fwd(xp, lb_logits, g_hn):
    T = xp.shape[0]
    nc = T // CHUNK
    off = MLA_IN // G_DIM

    def body(hq_ref, hf_ref, hi_ref, hg_ref, lbl_ref, ghn_ref, out_ref, oraw_ref, sall_ref, st_ref, b_s):
        lb = _lower_bound(lbl_ref)
        st_ref[...] = jnp.zeros(st_ref.shape, F32)
        row = lax.broadcasted_iota(jnp.int32, (CHUNK, CHUNK), 0)
        col = lax.broadcasted_iota(jnp.int32, (CHUNK, CHUNK), 1)
        tri = (col <= row).astype(BF)

        def chunk(c, carry):
            sl = pl.ds(pl.multiple_of(c * CHUNK, CHUNK), CHUNK)
            q, k, _, lf, _, _ = _gates(hq_ref[sl, :], hf_ref[sl, :], lb)
            v = hi_ref[sl, :]
            b = _tri_mm(tri, lf)
            b_s[...] = b
            st = st_ref[...]
            sall_ref[c, 0] = st
            o = _dot_nt(q * jnp.exp(b), st) + _dot(_intra(q, k, b, b_s, row, col), v)
            bl = b_s[CHUNK - 1:CHUNK, :]
            st_ref[...] = st * jnp.exp(bl) + _dot_tn(v, k * jnp.exp(bl - b))
            oraw_ref[sl, :] = o
            n, _ = _rms(o, G_DIM)
            hg = hg_ref[sl, :]
            out_ref[sl, :] = n * ghn_ref[...] * (hg * _sigmoid(hg))
            return carry

        lax.fori_loop(0, nc, chunk, 0)

    col_blk = lambda j: pl.BlockSpec((T, G_DIM), lambda h: (0, off + j * G_HEADS + h))
    head = pl.BlockSpec((T, G_DIM), lambda h: (0, h))
    return _pcall(
        body, "hgrn_fwd", (G_HEADS,),
        [col_blk(0), col_blk(1), col_blk(2), col_blk(3),
         pl.BlockSpec((2, G_DIM), lambda h: (0, h)), pl.BlockSpec((1, G_DIM), lambda h: (0, h))],
        [head, head, pl.BlockSpec((nc, 1, G_DIM, G_DIM), lambda h: (0, h, 0, 0))],
        [_sds((T, G_W)), _sds((T, G_W)), _sds((nc, G_HEADS, G_DIM, G_DIM))],
        scratch=[pltpu.VMEM((G_DIM, G_DIM), F32), pltpu.VMEM((CHUNK, G_DIM), F32)],
    )(xp, xp, xp, xp, lb_logits, g_hn)


def _fwd_out(o_pad, o_hgrn, x, g_on, w_out, g_post, g_fpre, tm):
    T = x.shape[0]

    def body(o_ref, oh_ref, x_ref, gon_ref, w_ref, gpost_ref, gfpre_ref, h1_ref, y1_ref, z_ref, mix_ref):
        for h in range(N_HEADS):
            sl = slice(h * HB, (h + 1) * HB)
            n, _ = _rms(o_ref[:, sl], V_DIM)
            mix_ref[:, sl] = (n * gon_ref[:, sl]).astype(BF)
        mix_ref[:, N_HEADS * HB:MIX_W] = oh_ref[...].astype(BF)
        y1 = jnp.dot(mix_ref[...], w_ref[...], preferred_element_type=F32)
        y1_ref[...] = y1
        ny, _ = _rms(y1, D)
        h1 = x_ref[...] + ny * gpost_ref[...]
        h1_ref[...] = h1
        nh, _ = _rms(h1, D)
        z_ref[...] = (nh * gfpre_ref[...]).astype(BF)

    return _pcall(body, "fwd_out", (T // tm,),
                  [_rows(tm, N_HEADS * HB), _rows(tm, G_W), _rows(tm, D), _full((1, N_HEADS * HB)),
                   _full((MIX_W, D)), _full((1, D)), _full((1, D))],
                  [_rows(tm, D), _rows(tm, D), _rows(tm, D), _rows(tm, MIX_W)],
                  [_sds((T, D)), _sds((T, D)), _sds((T, D), BF), _sds((T, MIX_W), BF)],
                  )(o_pad, o_hgrn, x, g_on, w_out, g_post, g_fpre)


def _ffn_fwd(z, wg, wu, wd, h1, tgt, g_fpost, tm):
    T = z.shape[0]
    nf = wg.shape[0]

    def body(z_ref, wg_ref, wu_ref, wd_ref, h1_ref, t_ref, gp_ref,
             gs_ref, us_ref, dh2_ref, dy2_ref, dgp_ref, loss_ref, acc):
        i, j = pl.program_id(0), pl.program_id(1)
        zv = z_ref[...]
        g = jnp.dot(zv, wg_ref[0], preferred_element_type=F32)
        u = jnp.dot(zv, wu_ref[0], preferred_element_type=F32)
        gs_ref[...] = g.astype(BF)
        us_ref[...] = u.astype(BF)
        part = _dot(g * _sigmoid(g) * u, wd_ref[0])

        @pl.when(j == 0)
        def _():
            acc[...] = part

        @pl.when(j > 0)
        def _():
            acc[...] += part

        @pl.when((i == 0) & (j == 0))
        def _():
            dgp_ref[...] = jnp.zeros(dgp_ref.shape, F32)
            loss_ref[...] = jnp.zeros(loss_ref.shape, F32)

        @pl.when(j == nf - 1)
        def _():
            ny, r = _rms(acc[...], D)
            err = h1_ref[...] + ny * gp_ref[...] - t_ref[...]
            loss_ref[...] += 0.5 * jnp.sum(jnp.sum(err * err, -1, keepdims=True) * (1.0 / D), 0, keepdims=True)
            dh2 = err * (1.0 / D)
            dh2_ref[...] = dh2
            dy2, dgp = _rms_bwd(ny, r, gp_ref[...], dh2, D)
            dy2_ref[...] = dy2.astype(BF)
            dgp_ref[...] += dgp

    tok = lambda n: pl.BlockSpec((tm, n), lambda i, j: (i, 0))
    col = pl.BlockSpec((tm, FF_PAD), lambda i, j: (i, j))
    return _pcall(
        body, "ffn_fwd", (T // tm, nf),
        [tok(D), pl.BlockSpec((1, D, FF_PAD), lambda i, j: (j, 0, 0)), pl.BlockSpec((1, D, FF_PAD), lambda i, j: (j, 0, 0)),
         pl.BlockSpec((1, FF_PAD, D), lambda i, j: (j, 0, 0)), tok(D), tok(D), _full((1, D))],
        [col, col, tok(D), tok(D), _full((1, D)), _full((1, HB))],
        [_sds((T, nf * FF_PAD), BF), _sds((T, nf * FF_PAD), BF), _sds((T, D)), _sds((T, D), BF),
         _sds((1, D)), _sds((1, HB))],
        scratch=[pltpu.VMEM((tm, D), F32)],
    )(z, wg, wu, wd, h1, tgt, g_fpost)


def _dsilu(x, s):
    return s * (1.0 + x * (1.0 - s))


def _ffn_bwd_x(dy2, gs, us, wg, wu, wd, h1, y1, dh2, g_fpre, g_post, tm):
    T = dy2.shape[0]
    nf = wg.shape[0]

    def body(dy2_ref, gs_ref, us_ref, wg_ref, wu_ref, wd_ref, h1_ref, y1_ref, dh2_ref, gf_ref, gp_ref,
             dgs_ref, dus_ref, dh1_ref, dy1_ref, dgf_ref, dgp_ref, acc):
        i, j = pl.program_id(0), pl.program_id(1)
        dff = _dot_nt(dy2_ref[...], wd_ref[0])
        g = gs_ref[...].astype(F32)
        u = us_ref[...].astype(F32)
        s = _sigmoid(g)
        dg = (dff * u * _dsilu(g, s)).astype(BF)
        du = (dff * g * s).astype(BF)
        dgs_ref[...] = dg
        dus_ref[...] = du
        part = _dot_nt(dg, wg_ref[0]) + _dot_nt(du, wu_ref[0])

        @pl.when(j == 0)
        def _():
            acc[...] = part

        @pl.when(j > 0)
        def _():
            acc[...] += part

        @pl.when((i == 0) & (j == 0))
        def _():
            dgf_ref[...] = jnp.zeros(dgf_ref.shape, F32)
            dgp_ref[...] = jnp.zeros(dgp_ref.shape, F32)

        @pl.when(j == nf - 1)
        def _():
            nh, rh = _rms(h1_ref[...], D)
            dh, dgf = _rms_bwd(nh, rh, gf_ref[...], acc[...], D)
            dh1 = dh2_ref[...] + dh
            dh1_ref[...] = dh1
            dgf_ref[...] += dgf
            ny, ry = _rms(y1_ref[...], D)
            dy1, dgp = _rms_bwd(ny, ry, gp_ref[...], dh1, D)
            dy1_ref[...] = dy1.astype(BF)
            dgp_ref[...] += dgp

    tok = lambda n: pl.BlockSpec((tm, n), lambda i, j: (i, 0))
    col = pl.BlockSpec((tm, FF_PAD), lambda i, j: (i, j))
    wcol = pl.BlockSpec((1, D, FF_PAD), lambda i, j: (j, 0, 0))
    return _pcall(
        body, "ffn_bwd_x", (T // tm, nf),
        [tok(D), col, col, wcol, wcol, pl.BlockSpec((1, FF_PAD, D), lambda i, j: (j, 0, 0)),
         tok(D), tok(D), tok(D), _full((1, D)), _full((1, D))],
        [col, col, tok(D), tok(D), _full((1, D)), _full((1, D))],
        [_sds((T, nf * FF_PAD), BF), _sds((T, nf * FF_PAD), BF), _sds((T, D)), _sds((T, D), BF),
         _sds((1, D)), _sds((1, D))],
        scratch=[pltpu.VMEM((tm, D), F32)],
    )(dy2, gs, us, wg, wu, wd, h1, y1, dh2, g_fpre, g_post)


def _ffn_bwd_w(z, gs, us, dgs, dus, dy2, tm):
    T = z.shape[0]
    nf = gs.shape[1] // FF_PAD
    nt = T // tm

    def body(z_ref, gs_ref, us_ref, dgs_ref, dus_ref, dy2_ref, dwg_ref, dwu_ref, dwd_ref, ag, au, ad):
        i = pl.program_id(1)
        g = gs_ref[...].astype(F32)
        ff = g * _sigmoid(g) * us_ref[...].astype(F32)
        pg = _dot_tn(z_ref[...], dgs_ref[...])
        pu = _dot_tn(z_ref[...], dus_ref[...])
        pd = _dot_tn(ff, dy2_ref[...])

        @pl.when(i == 0)
        def _():
            ag[...] = pg
            au[...] = pu
            ad[...] = pd

        @pl.when(i > 0)
        def _():
            ag[...] += pg
            au[...] += pu
            ad[...] += pd

        @pl.when(i == nt - 1)
        def _():
            dwg_ref[0] = ag[...].astype(BF)
            dwu_ref[0] = au[...].astype(BF)
            dwd_ref[0] = ad[...].astype(BF)

    tok = lambda n: pl.BlockSpec((tm, n), lambda j, i: (i, 0))
    col = pl.BlockSpec((tm, FF_PAD), lambda j, i: (i, j))
    wcol = pl.BlockSpec((1, D, FF_PAD), lambda j, i: (j, 0, 0))
    wrow = pl.BlockSpec((1, FF_PAD, D), lambda j, i: (j, 0, 0))
    return _pcall(
        body, "ffn_bwd_w", (nf, nt), [tok(D), col, col, col, col, tok(D)], [wcol, wcol, wrow],
        [_sds((nf, D, FF_PAD), BF), _sds((nf, D, FF_PAD), BF), _sds((nf, FF_PAD, D), BF)],
        scratch=[pltpu.VMEM((D, FF_PAD), F32), pltpu.VMEM((D, FF_PAD), F32), pltpu.VMEM((FF_PAD, D), F32)],
    )(z, gs, us, dgs, dus, dy2)


def _out_bwd(dy1, mix, o_pad, w_out, g_on, tm):
    T = dy1.shape[0]
    W = N_HEADS * HB

    def body(dy1_ref, mix_ref, o_ref, w_ref, gon_ref, do_ref, dl_ref, dohg_ref, dw_ref, dgon_ref):
        i = pl.program_id(0)
        dy1v = dy1_ref[...]
        dmix = _dot_nt(dy1v, w_ref[...])
        pw = _dot_tn(mix_ref[...], dy1v)

        @pl.when(i == 0)
        def _():
            dw_ref[...] = pw
            dgon_ref[...] = jnp.zeros(dgon_ref.shape, F32)

        @pl.when(i > 0)
        def _():
            dw_ref[...] += pw

        for h in range(N_HEADS):
            sl = slice(h * HB, (h + 1) * HB)
            ov = o_ref[:, sl]
            n, r = _rms(ov, V_DIM)
            do, dg = _rms_bwd(n, r, gon_ref[:, sl], dmix[:, sl], V_DIM)
            dgon_ref[:, sl] += dg
            do_ref[:, sl] = do.astype(BF)
            dl_ref[:, sl] = jnp.broadcast_to(jnp.sum(do * ov, -1, keepdims=True), (tm, HB))
        dohg_ref[...] = dmix[:, W:MIX_W]

    return _pcall(body, "out_bwd", (T // tm,),
                  [_rows(tm, D), _rows(tm, MIX_W), _rows(tm, W), _full((MIX_W, D)), _full((1, W))],
                  [_rows(tm, W), _rows(tm, W), _rows(tm, G_W), _full((MIX_W, D)), _full((1, W))],
                  [_sds((T, W), BF), _sds((T, W)), _sds((T, G_W)), _sds((MIX_W, D)), _sds((1, W))],
                  )(dy1, mix, o_pad, w_out, g_on)


def _flash_bwd(q, k, v, do, lse, dl, tq):
    T = q.shape[0]
    nq = T // tq
    scale = QK_DIM ** -0.5

    def body(k_ref, v_ref, q_ref, do_ref, lse_ref, dl_ref, dk_ref, dv_ref, dq_ref):
        j = pl.program_id(1)

        @pl.when(j == 0)
        def _():
            dq_ref[...] = jnp.zeros(dq_ref.shape, F32)

        kv, vv = k_ref[...], v_ref[...]

        def blk(i, carry, masked):
            dk, dv = carry
            sl = pl.ds(pl.multiple_of(i * tq, tq), tq)
            qv, dov = q_ref[sl, :], do_ref[sl, :]
            s = _dot_nt(qv, kv) * scale
            if masked:
                r = lax.broadcasted_iota(jnp.int32, (tq, tq), 0)
                c = lax.broadcasted_iota(jnp.int32, (tq, tq), 1)
                s = jnp.where(c <= r, s, NEG)
            p = jnp.exp(s - lse_ref[sl, 0:1])
            ds = p * (_dot_nt(dov, vv) - dl_ref[sl, 0:1]) * scale
            dq_ref[sl, :] += _dot(ds, kv)
            return dk + _dot_tn(ds, qv), dv + _dot_tn(p, dov)

        zero = jnp.zeros((tq, HB), F32)
        carry = blk(j, (zero, zero), True)
        dk, dv = lax.fori_loop(j + 1, nq, lambda i, cr: blk(i, cr, False), carry)
        dk_ref[...] = dk
        dv_ref[...] = dv

    tile = pl.BlockSpec((tq, HB), lambda h, j: (j, h))
    whole = pl.BlockSpec((T, HB), lambda h, j: (0, h))
    return _pcall(body, "flash_bwd", (N_HEADS, nq), [tile, tile, whole, whole, whole, whole],
                  [tile, tile, whole], [_sds((T, N_HEADS * HB))] * 3)(k, v, q, do, lse, dl)


def _mla_prep_bwd(xp, tabs, dq, dk, dv, g_q, g_kv, w_uq, w_uk, w_uv, tm):
    T = xp.shape[0]
    W = N_HEADS * HB

    def body(xp_ref, ta_ref, tb1_ref, tb2_ref, dq_ref, dk_ref, dv_ref, gq_ref, gkv_ref, wuq_ref, wuk_ref, wuv_ref,
             dxp_ref, dwuq_ref, dwuk_ref, dwuv_ref, dgq_ref, dgkv_ref, dqp):
        i = pl.program_id(0)
        ta, tb1, tb2 = ta_ref[...], tb1_ref[...], tb2_ref[...]
        nq, rq = _rms(xp_ref[:, 0:Q_RANK], Q_RANK)
        nkv, rkv = _rms(xp_ref[:, Q_RANK:Q_RANK + KV_RANK], KV_RANK)
        dkr = jnp.zeros((tm, HB), F32)
        for h in range(N_HEADS):
            sl = slice(h * HB, (h + 1) * HB)
            dqp[:, sl] = _unrope(dq_ref[:, sl], ta, tb1, tb2).astype(BF)
            dkr = dkr + dk_ref[:, sl]
        dkr = pltpu.roll(_unrope(dkr, ta, tb1, tb2), HB - NOPE, 1)
        lane = lax.broadcasted_iota(jnp.int32, (tm, HB), 1)
        dxp_ref[:, Q_RANK + KV_RANK:MLA_IN] = jnp.where(lane < ROPE, dkr, 0.0)
        dqpv = dqp[...]
        dkv, dvv = dk_ref[...].astype(BF), dv_ref[...].astype(BF)
        nqs = (nq * gq_ref[...]).astype(BF)
        nkvs = (nkv * gkv_ref[...]).astype(BF)
        pq, pk, pv = _dot_tn(nqs, dqpv), _dot_tn(nkvs, dkv), _dot_tn(nkvs, dvv)
        dcq, dgq = _rms_bwd(nq, rq, gq_ref[...], _dot_nt(dqpv, wuq_ref[...]), Q_RANK)
        dckv, dgkv = _rms_bwd(nkv, rkv, gkv_ref[...], _dot_nt(dkv, wuk_ref[...]) + _dot_nt(dvv, wuv_ref[...]), KV_RANK)
        dxp_ref[:, 0:Q_RANK] = dcq
        dxp_ref[:, Q_RANK:Q_RANK + KV_RANK] = dckv

        @pl.when(i == 0)
        def _():
            dwuq_ref[...] = pq
            dwuk_ref[...] = pk
            dwuv_ref[...] = pv
            dgq_ref[...] = dgq
            dgkv_ref[...] = dgkv

        @pl.when(i > 0)
        def _():
            dwuq_ref[...] += pq
            dwuk_ref[...] += pk
            dwuv_ref[...] += pv
            dgq_ref[...] += dgq
            dgkv_ref[...] += dgkv

    tab = _rows(tm, HB)
    return _pcall(
        body, "mla_prep_bwd", (T // tm,),
        [_rows(tm, MLA_IN), tab, tab, tab, _rows(tm, W), _rows(tm, W), _rows(tm, W), _full((1, Q_RANK)),
         _full((1, KV_RANK)), _full((Q_RANK, W)), _full((KV_RANK, W)), _full((KV_RANK, W))],
        [_rows(tm, MLA_IN), _full((Q_RANK, W)), _full((KV_RANK, W)), _full((KV_RANK, W)), _full((1, Q_RANK)),
         _full((1, KV_RANK))],
        [_sds((T, MLA_IN)), _sds((Q_RANK, W)), _sds((KV_RANK, W)), _sds((KV_RANK, W)), _sds((1, Q_RANK)),
         _sds((1, KV_RANK))],
        scratch=[pltpu.VMEM((tm, W), BF)],
    )(xp, *tabs, dq, dk, dv, g_q, g_kv, w_uq, w_uk, w_uv)


def _hgrn_bwd(xp, o_raw, s_all, d_out, lb_logits, g_hn):
    T = xp.shape[0]
    nc = T // CHUNK
    off = MLA_IN // G_DIM

    def body(hq_ref, hf_ref, hi_ref, hg_ref, o_ref, sall_ref, dout_ref, lbl_ref, ghn_ref,
             dhq_ref, dhf_ref, dhi_ref, dhg_ref, dlbl_ref, dghn_ref, dst_ref, b_s, acc_lb, acc_g):
        lb = _lower_bound(lbl_ref)
        dst_ref[...] = jnp.zeros(dst_ref.shape, F32)
        acc_lb[...] = jnp.zeros(acc_lb.shape, F32)
        acc_g[...] = jnp.zeros(acc_g.shape, F32)
        row = lax.broadcasted_iota(jnp.int32, (CHUNK, CHUNK), 0)
        col = lax.broadcasted_iota(jnp.int32, (CHUNK, CHUNK), 1)
        tri = (col <= row).astype(BF)
        tri_t = (col >= row).astype(BF)
        last = lax.broadcasted_iota(jnp.int32, (CHUNK, G_DIM), 0) == CHUNK - 1
        ghn = ghn_ref[...]

        def chunk(cc, carry):
            c = nc - 1 - cc
            sl = pl.ds(pl.multiple_of(c * CHUNK, CHUNK), CHUNK)
            hq, hg = hq_ref[sl, :], hg_ref[sl, :]
            q, k, f, lf, sig, sq = _gates(hq, hf_ref[sl, :], lb)
            v = hi_ref[sl, :]
            b = _tri_mm(tri, lf)
            b_s[...] = b
            st = sall_ref[c, 0]
            dstn = dst_ref[...]
            o = o_ref[sl, :]
            dout = dout_ref[sl, :]
            n, r = _rms(o, G_DIM)
            sg = _sigmoid(hg)
            dhg_ref[sl, :] = dout * (n * ghn) * _dsilu(hg, sg)
            do, dg = _rms_bwd(n, r, ghn, dout * (hg * sg), G_DIM)
            acc_g[...] += dg
            eb = jnp.exp(b)
            bl = b_s[CHUNK - 1:CHUNK, :]
            ebl = jnp.exp(bl)
            ekd = jnp.exp(bl - b)
            kd = k * ekd
            a = _intra(q, k, b, b_s, row, col)
            dq_i, dk_i = _intra_bwd(q, k, b, b_s, _dot_nt(do, v), row, col)
            dhi_ref[sl, :] = _dot_tn(a, do) + _dot_nt(kd, dstn)
            dk_state = _dot(v, dstn) * ekd
            dq = dq_i + _dot(do, st) * eb
            dk = dk_i + dk_state
            dbl = jnp.sum(k * dk_state, 0, keepdims=True) + ebl * jnp.sum(dstn * st, 0, keepdims=True)
            db = q * dq - k * dk + jnp.where(last, dbl, 0.0)
            df = _tri_mm(tri_t, db) / f - dk
            dhf_ref[sl, :] = df * (1.0 - lb) * sig * (1.0 - sig)
            acc_lb[...] += jnp.sum(df * (1.0 - sig), 0, keepdims=True)
            dhq_ref[sl, :] = dq * _dsilu(hq, sq)
            dst_ref[...] = dstn * ebl + _dot_tn(do, q * eb)
            return carry

        lax.fori_loop(0, nc, chunk, 0)
        dl0 = acc_lb[...] * lb * (1.0 - lb)
        dlbl_ref[0:1, :] = dl0
        dlbl_ref[1:2, :] = -dl0
        dghn_ref[...] = acc_g[...]

    col_blk = lambda j: pl.BlockSpec((T, G_DIM), lambda h: (0, off + j * G_HEADS + h))
    head = pl.BlockSpec((T, G_DIM), lambda h: (0, h))
    two = pl.BlockSpec((2, G_DIM), lambda h: (0, h))
    one = pl.BlockSpec((1, G_DIM), lambda h: (0, h))
    res = _pcall(
        body, "hgrn_bwd", (G_HEADS,),
        [col_blk(0), col_blk(1), col_blk(2), col_blk(3), head,
         pl.BlockSpec((nc, 1, G_DIM, G_DIM), lambda h: (0, h, 0, 0)), head, two, one],
        [head, head, head, head, two, one],
        [_sds((T, G_W))] * 4 + [_sds((2, G_W)), _sds((1, G_W))],
        scratch=[pltpu.VMEM((G_DIM, G_DIM), F32), pltpu.VMEM((CHUNK, G_DIM), F32),
                 pltpu.VMEM((1, G_DIM), F32), pltpu.VMEM((1, G_DIM), F32)],
    )(xp, xp, xp, xp, o_raw, s_all, d_out, lb_logits, g_hn)
    return res


def _in_bwd_x(x, dxp_m, dxp_h, dh1, w_in_al, g_pre, tm):
    T = x.shape[0]

    def body(x_ref, dm_ref, d0_ref, d1_ref, d2_ref, d3_ref, dh1_ref, w_ref, g_ref, dx_ref, dg_ref):
        i = pl.program_id(0)
        du = _dot_nt(dm_ref[...], w_ref[:, 0:MLA_IN])
        for j, d_ref in enumerate((d0_ref, d1_ref, d2_ref, d3_ref)):
            du = du + _dot_nt(d_ref[...], w_ref[:, MLA_IN + j * G_W:MLA_IN + (j + 1) * G_W])
        nx, r = _rms(x_ref[...], D)
        dx, dg = _rms_bwd(nx, r, g_ref[...], du, D)
        dx_ref[...] = dh1_ref[...] + dx

        @pl.when(i == 0)
        def _():
            dg_ref[...] = dg

        @pl.when(i > 0)
        def _():
            dg_ref[...] += dg

    return _pcall(body, "in_bwd_x", (T // tm,),
                  [_rows(tm, D), _rows(tm, MLA_IN)] + [_rows(tm, G_W)] * 4 + [_rows(tm, D), _full((D, XP_W)), _full((1, D))],
                  [_rows(tm, D), _full((1, D))], [_sds((T, D)), _sds((1, D))],
                  )(x, dxp_m, *dxp_h, dh1, w_in_al, g_pre)


def _mm_tn(name, a, b, tn, tt):
    T, M = a.shape
    N = b.shape[1]

    def body(a_ref, b_ref, o_ref):
        t = pl.program_id(1)
        part = _dot_tn(a_ref[...], b_ref[...])

        @pl.when(t == 0)
        def _():
            o_ref[...] = part

        @pl.when(t > 0)
        def _():
            o_ref[...] += part

    return _pcall(body, name, (N // tn, T // tt),
                  [pl.BlockSpec((tt, M), lambda n, t: (t, 0)), pl.BlockSpec((tt, tn), lambda n, t: (t, n))],
                  pl.BlockSpec((M, tn), lambda n, t: (0, n)), _sds((M, N)))(a, b)


def _pad_heads(w, width, real):
    lead = w.shape[:-1]
    w = w.reshape(lead + (N_HEADS, real))
    w = jnp.pad(w, [(0, 0)] * len(lead) + [(0, 0), (0, width - real)])
    return w.reshape(lead + (N_HEADS * width,))


def _unpad_heads(w, width, real):
    lead = w.shape[:-1]
    return w.reshape(lead + (N_HEADS, width))[..., :real].reshape(lead + (N_HEADS * real,))


def _rope_tables(positions):
    half = ROPE // 2
    inv_freq = 1.0 / (ROPE_THETA ** (jnp.arange(0, ROPE, 2, dtype=F32) / ROPE))
    ang = positions.astype(F32)[:, None] * inv_freq
    cos, sin = jnp.cos(ang), jnp.sin(ang)
    T = positions.shape[0]
    z = lambda n: jnp.zeros((T, n), F32)
    ta = jnp.concatenate([jnp.ones((T, NOPE), F32), cos, cos, z(HB - QK_DIM)], 1)
    tb1 = jnp.concatenate([z(NOPE), -sin, z(half), z(HB - QK_DIM)], 1)
    tb2 = jnp.concatenate([z(NOPE), z(half), sin, z(HB - QK_DIM)], 1)
    return ta, tb1, tb2


def kernel(x, positions, attn_pre_norm, w_in, mla_q_norm, mla_w_uq, mla_kv_norm, mla_w_ukv, mla_out_norm, hgrn_lb_logits, hgrn_out_norm, w_out, attn_post_norm, ffn_pre_norm, w_gate, w_up, w_down, ffn_post_norm, loss_target, m_attn_pre_norm, m_w_in, m_mla_q_norm, m_mla_w_uq, m_mla_kv_norm, m_mla_w_ukv, m_mla_out_norm, m_hgrn_lb_logits, m_hgrn_out_norm, m_w_out, m_attn_post_norm, m_ffn_pre_norm, m_w_gate, m_w_up, m_w_down, m_ffn_post_norm, v_attn_pre_norm, v_w_in, v_mla_q_norm, v_mla_w_uq, v_mla_kv_norm, v_mla_w_ukv, v_mla_out_norm, v_hgrn_lb_logits, v_hgrn_out_norm, v_w_out, v_attn_post_norm, v_ffn_pre_norm, v_w_gate, v_w_up, v_w_down, v_ffn_post_norm):
    T = x.shape[1]
    tm = min(_TM, T)
    tq = min(_TQ, T)
    xs, tgt = x[0], loss_target[0]
    uq_sh = (Q_RANK // N_DEV, N_HEADS * QK_DIM)

    g_in, g_uq, g_out, wg, wu, wd = _all_gather_weights(
        w_in[0], mla_w_uq[0].reshape(uq_sh), w_out[0], w_gate[0], w_up[0], w_down[0])
    w_in_full = g_in.transpose(1, 0, 2).reshape(D, IN_W)
    kr_end = Q_RANK + KV_RANK + ROPE
    w_in_al = jnp.concatenate([w_in_full[:, :kr_end], jnp.zeros((D, KR_PAD - ROPE), BF), w_in_full[:, kr_end:]], 1)
    w_uq_p = _pad_heads(g_uq.reshape(Q_RANK, N_HEADS * QK_DIM), HB, QK_DIM)
    w_ukv = mla_w_ukv[0].astype(BF)
    w_uk_p = _pad_heads(w_ukv[..., :NOPE].reshape(KV_RANK, N_HEADS * NOPE), HB, NOPE)
    w_uv_p = _pad_heads(w_ukv[..., NOPE:].reshape(KV_RANK, N_HEADS * V_DIM), HB, V_DIM)
    w_out_full = g_out.reshape(D, D)
    w_out_mla = jnp.pad(w_out_full[:N_HEADS * V_DIM].reshape(N_HEADS, V_DIM, D), ((0, 0), (0, HB - V_DIM), (0, 0)))
    w_out_p = jnp.concatenate([w_out_mla.reshape(N_HEADS * HB, D), w_out_full[N_HEADS * V_DIM:]], 0)
    g_on_p = _pad_heads(mla_out_norm, HB, V_DIM)
    tabs = _rope_tables(positions[0])

    xp, u = _fwd_in(xs, attn_pre_norm, w_in_al, tm)
    q_att, k_att, v_att = _mla_prep(xp, tabs, mla_q_norm, mla_kv_norm, w_uq_p, w_uk_p, w_uv_p, tm)
    o_pad, lse = _flash_fwd(q_att, k_att, v_att, tq)
    o_hgrn, o_raw, s_all = _hgrn_fwd(xp, hgrn_lb_logits, hgrn_out_norm)
    h1, y1, z, mix = _fwd_out(o_pad, o_hgrn, xs, g_on_p, w_out_p, attn_post_norm, ffn_pre_norm, tm)
    gs, us, dh2, dy2, d_fpost, loss_row = _ffn_fwd(z, wg, wu, wd, h1, tgt, ffn_post_norm, tm)

    dgs, dus, dh1, dy1, d_fpre, d_post = _ffn_bwd_x(dy2, gs, us, wg, wu, wd, h1, y1, dh2, ffn_pre_norm, attn_post_norm, tm)
    dwg, dwu, dwd = _ffn_bwd_w(z, gs, us, dgs, dus, dy2, tm)
    do_pad, dl, d_ohg, dw_out_p, d_on_p = _out_bwd(dy1, mix, o_pad, w_out_p, g_on_p, tm)
    dk_att, dv_att, dq_att = _flash_bwd(q_att, k_att, v_att, do_pad, lse, dl, tq)
    dxp_m, dw_uq_p, dw_uk_p, dw_uv_p, d_gq, d_gkv = _mla_prep_bwd(
        xp, tabs, dq_att, dk_att, dv_att, mla_q_norm, mla_kv_norm, w_uq_p, w_uk_p, w_uv_p, tm)
    *dxp_h, d_lbl, d_ghn = _hgrn_bwd(xp, o_raw, s_all, d_ohg, hgrn_lb_logits, hgrn_out_norm)
    grad_x, d_pre = _in_bwd_x(xs, dxp_m, dxp_h, dh1, w_in_al, attn_pre_norm, tm)
    dw_in_m = _mm_tn("in_bwd_w_mla", u, dxp_m, MLA_IN, tm)
    dw_in_h = [_mm_tn("in_bwd_w_hgrn%d" % j, u, dxp_h[j], G_W, tm) for j in range(4)]

    dw_in = jnp.concatenate([dw_in_m[:, :kr_end]] + dw_in_h, 1).reshape(D, N_DEV, IN_SH).transpose(1, 0, 2).astype(BF)
    dw_uq = _unpad_heads(dw_uq_p, HB, QK_DIM).reshape((N_DEV,) + uq_sh).astype(BF)
    dw_out_mla = dw_out_p[:N_HEADS * HB].reshape(N_HEADS, HB, D)[:, :V_DIM].reshape(N_HEADS * V_DIM, D)
    dw_out = jnp.concatenate([dw_out_mla, dw_out_p[N_HEADS * HB:]], 0).reshape(N_DEV, D // N_DEV, D).astype(BF)
    dw_ukv = jnp.concatenate([_unpad_heads(dw_uk_p, HB, NOPE).reshape(KV_RANK, N_HEADS, NOPE),
                              _unpad_heads(dw_uv_p, HB, V_DIM).reshape(KV_RANK, N_HEADS, V_DIM)], -1)
    d_on = _unpad_heads(d_on_p, HB, V_DIM)

    p_in, p_uq, p_out, p_g, p_u, p_d = _reduce_scatter_grads([dw_in, dw_uq, dw_out, dwg, dwu, dwd])
    r_in = _shard_adam("adam_w_in", p_in, w_in[0], m_w_in[0], v_w_in[0], 256)
    r_uq = _shard_adam("adam_w_uq", p_uq, mla_w_uq[0].reshape(uq_sh), m_mla_w_uq[0].reshape(uq_sh),
                       v_mla_w_uq[0].reshape(uq_sh), uq_sh[0])
    r_out = _shard_adam("adam_w_out", p_out, w_out[0], m_w_out[0], v_w_out[0], D // N_DEV)
    r_g = _shard_adam("adam_w_gate", p_g, w_gate[0], m_w_gate[0], v_w_gate[0], 256)
    r_u = _shard_adam("adam_w_up", p_u, w_up[0], m_w_up[0], v_w_up[0], 256)
    r_d = _shard_adam("adam_w_down", p_d, w_down[0], m_w_down[0], v_w_down[0], FF_SH // 2)

    ukv2 = lambda a: a.reshape(KV_RANK, N_HEADS * HB)
    small_g = [d_pre, d_gq, d_gkv, ukv2(dw_ukv), d_on, d_lbl, d_ghn, d_post, d_fpre, d_fpost]
    small_w = [attn_pre_norm, mla_q_norm, mla_kv_norm, ukv2(mla_w_ukv), mla_out_norm, hgrn_lb_logits, hgrn_out_norm,
               attn_post_norm, ffn_pre_norm, ffn_post_norm]
    small_m = [m_attn_pre_norm, m_mla_q_norm, m_mla_kv_norm, ukv2(m_mla_w_ukv), m_mla_out_norm, m_hgrn_lb_logits,
               m_hgrn_out_norm, m_attn_post_norm, m_ffn_pre_norm, m_ffn_post_norm]
    small_v = [v_attn_pre_norm, v_mla_q_norm, v_mla_kv_norm, ukv2(v_mla_w_ukv), v_mla_out_norm, v_hgrn_lb_logits,
               v_hgrn_out_norm, v_attn_post_norm, v_ffn_pre_norm, v_ffn_post_norm]
    s_g, s_d, s_m, s_v = _small_allreduce_adam(small_g, small_w, small_m, small_v)

    loss = lax.psum(loss_row[0, 0], ("x", "y", "c"))

    def assemble(big, small):
        b_in, b_uq, b_out, b_g, b_u, b_d = big
        return [small[0], b_in[None], small[1], b_uq.reshape(mla_w_uq.shape), small[2],
                small[3].reshape(mla_w_ukv.shape), small[4], small[5], small[6], b_out[None], small[7], small[8],
                b_g[None], b_u[None], b_d[None], small[9]]

    outs = [loss, grad_x[None]]
    for idx, small in enumerate((s_g, s_d, s_m, s_v)):
        outs += assemble([r[idx] for r in (r_in, r_uq, r_out, r_g, r_u, r_d)], small)
    return tuple(outs)
```

```python
import jax
import jax.numpy as jnp
from jax import lax
from jax.experimental import pallas as pl
from jax.experimental.pallas import tpu as pltpu

BF = jnp.bfloat16
F32 = jnp.float32
MESH = pl.DeviceIdType.MESH

N_DEV = 8
D = 1024
EPS = 1e-6
ROPE_THETA = 10000.0
N_HEADS = 8
HB = 128
NOPE = 64
ROPE = 32
V_DIM = 64
QK_DIM = NOPE + ROPE
Q_RANK = 384
KV_RANK = 128
KR_PAD = 128
MLA_IN = Q_RANK + KV_RANK + KR_PAD
G_HEADS = 4
G_DIM = 128
G_W = G_HEADS * G_DIM
CHUNK = 64
SUB = 16
XP_W = MLA_IN + 4 * G_W
IN_SH = 324
IN_W = N_DEV * IN_SH
FF_SH = 352
FF_PAD = 384
MIX_W = N_HEADS * HB + G_W

ADAM_LR = 0.001
ADAM_B1 = 0.9
ADAM_B2 = 0.999
ADAM_EPS = 1e-08
ADAM_WD = 0.01
ADAM_STEP = 10

_TM = 512
_TQ = 512
_VMEM_LIMIT = 56 * 1024 * 1024
NEG = -1e30


def _dot(a, b):
    return jnp.dot(a.astype(BF), b.astype(BF), preferred_element_type=F32)


def _dot_nt(a, b):
    return lax.dot_general(a.astype(BF), b.astype(BF), (((1,), (1,)), ((), ())), preferred_element_type=F32)


def _dot_tn(a, b):
    return lax.dot_general(a.astype(BF), b.astype(BF), (((0,), (0,)), ((), ())), preferred_element_type=F32)


def _sigmoid(x):
    return 1.0 / (1.0 + jnp.exp(-x))


def _rms(x, n):
    r = lax.rsqrt(jnp.sum(x * x, -1, keepdims=True) * (1.0 / n) + EPS)
    return x * r, r


def _rms_bwd(nx, r, g, dy, n):
    dg = jnp.sum(dy * nx, 0, keepdims=True)
    dn = dy * g
    dx = r * (dn - nx * (jnp.sum(dn * nx, -1, keepdims=True) * (1.0 / n)))
    return dx, dg


def _adamw(w, g, m, v):
    m2 = ADAM_B1 * m + (1.0 - ADAM_B1) * g
    v2 = ADAM_B2 * v + (1.0 - ADAM_B2) * (g * g)
    m_hat = m2 / (1.0 - ADAM_B1 ** ADAM_STEP)
    v_hat = v2 / (1.0 - ADAM_B2 ** ADAM_STEP)
    delta = -ADAM_LR * (m_hat / (jnp.sqrt(v_hat) + ADAM_EPS) + ADAM_WD * w)
    return delta, m2, v2


def _pcall(body, name, grid, in_specs, out_specs, out_shape, scratch=(), exchange=None):
    scratch = list(scratch)
    extra = ()
    if exchange is not None:
        kinds, extra = exchange
        in_specs, out_specs, out_shape = list(in_specs), list(out_specs), list(out_shape)
        n_in, n_out, n_scr, n_x = len(in_specs), len(out_specs), len(scratch), len(extra)
        inner = body

        def body(*refs):
            ins, rest = refs[:n_in], refs[n_in:]
            x_src, rest = rest[:n_x], rest[n_x:]
            outs, rest = rest[:n_out], rest[n_out:]
            x_dst, rest = rest[:n_x], rest[n_x:]
            ex = _Exchange(kinds, x_src, x_dst, *rest[n_scr:])
            first = pl.program_id(0) == 0
            last = pl.program_id(0) == grid[0] - 1
            for a in range(1, len(grid)):
                first = first & (pl.program_id(a) == 0)
                last = last & (pl.program_id(a) == grid[a] - 1)
            pl.when(first)(ex.start)
            inner(*ins, *outs, *rest[:n_scr])
            pl.when(last)(ex.wait)

        in_specs += [_HBM] * n_x
        out_specs += [_HBM] * n_x
        out_shape += _exchange_shapes(kinds, extra)
        scratch += _exchange_sems(n_x)
    call = pl.pallas_call(
        body, name=name, grid=grid, in_specs=in_specs, out_specs=out_specs, out_shape=out_shape,
        scratch_shapes=scratch,
        compiler_params=pltpu.CompilerParams(
            dimension_semantics=("arbitrary",) * len(grid), vmem_limit_bytes=_VMEM_LIMIT))
    return lambda *operands: call(*operands, *extra)


def _full(shape):
    return pl.BlockSpec(shape, lambda *_: (0,) * len(shape))


def _rows(tm, n):
    return pl.BlockSpec((tm, n), lambda i, *_: (i, 0))


def _sds(shape, dtype=F32):
    return jax.ShapeDtypeStruct(shape, dtype)


def _peer(k, x, y, c):
    px = 1 - x if (k >> 2) & 1 else x
    py = 1 - y if (k >> 1) & 1 else y
    pc = 1 - c if k & 1 else c
    return px, py, pc


GATHER, SCATTER = "gather", "scatter"


class _Exchange:
    def __init__(self, kinds, srcs, dsts, send_sems, recv_sems, loc_sems):
        self.kinds, self.srcs, self.dsts = kinds, srcs, dsts
        self.send_sems, self.recv_sems, self.loc_sems = send_sems, recv_sems, loc_sems
        self.x, self.y, self.c = lax.axis_index("x"), lax.axis_index("y"), lax.axis_index("c")
        self.me = 4 * self.x + 2 * self.y + self.c

    def _src(self, w, slot):
        return self.srcs[w] if self.kinds[w] == GATHER else self.srcs[w].at[slot]

    def _copy(self, w, k, outgoing):
        px, py, pc = _peer(k, self.x, self.y, self.c)
        pid = 4 * px + 2 * py + pc
        return pltpu.make_async_remote_copy(
            src_ref=self._src(w, pid if outgoing else self.me),
            dst_ref=self.dsts[w].at[self.me if outgoing else pid],
            send_sem=self.send_sems.at[w, k - 1], recv_sem=self.recv_sems.at[w, k - 1],
            device_id=(px, py, pc), device_id_type=MESH)

    def _local(self, w):
        return pltpu.make_async_copy(self._src(w, self.me), self.dsts[w].at[self.me], self.loc_sems.at[w])

    def start(self):
        for w in range(len(self.srcs)):
            self._local(w).start()
            for k in range(1, N_DEV):
                self._copy(w, k, True).start()

    def wait(self):
        for w in range(len(self.srcs)):
            self._local(w).wait()
            for k in range(1, N_DEV):
                self._copy(w, k, False).wait_recv()
        for w in range(len(self.srcs)):
            for k in range(1, N_DEV):
                self._copy(w, k, True).wait_send()


def _exchange_sems(n_w):
    return [pltpu.SemaphoreType.DMA((n_w, N_DEV - 1)), pltpu.SemaphoreType.DMA((n_w, N_DEV - 1)),
            pltpu.SemaphoreType.DMA((n_w,))]


def _exchange_shapes(kinds, srcs):
    return [_sds(((N_DEV,) if kd == GATHER else ()) + tuple(s.shape), s.dtype) for kd, s in zip(kinds, srcs)]


_HBM = pl.BlockSpec(memory_space=pl.ANY)


def _cast_shards(w_in, w_uq, w_out, w_gate, w_up, w_down):
    shapes = [(D, IN_SH), (Q_RANK // N_DEV, N_HEADS * QK_DIM), (D // N_DEV, D), (D, FF_PAD), (D, FF_PAD), (FF_PAD, D)]

    def body(win, wuq, wout, wg, wu, wd, sin_, suq, sout, sg, su, sd):
        sin_[...] = win[...].astype(BF)
        suq[...] = wuq[...].astype(BF)
        sout[...] = wout[...].astype(BF)
        sg[...] = jnp.zeros(sg.shape, BF)
        sg[:, 0:FF_SH] = wg[...].astype(BF)
        su[...] = jnp.zeros(su.shape, BF)
        su[:, 0:FF_SH] = wu[...].astype(BF)
        sd[...] = jnp.zeros(sd.shape, BF)
        sd[0:FF_SH, :] = wd[...].astype(BF)

    vm = pl.BlockSpec(memory_space=pltpu.VMEM)
    return pl.pallas_call(
        body, name="cast_shards", in_specs=[vm] * 6, out_specs=[vm] * 6,
        out_shape=[_sds(s, BF) for s in shapes],
        compiler_params=pltpu.CompilerParams(vmem_limit_bytes=_VMEM_LIMIT),
    )(w_in, w_uq, w_out, w_gate, w_up, w_down)


def _exchange_call(name, kinds, srcs):
    n_w = len(srcs)

    def body(*refs):
        ex = _Exchange(kinds, refs[:n_w], refs[n_w:2 * n_w], *refs[2 * n_w:])
        ex.start()
        ex.wait()

    return pl.pallas_call(
        body, name=name, in_specs=[_HBM] * n_w, out_specs=[_HBM] * n_w,
        out_shape=_exchange_shapes(kinds, srcs), scratch_shapes=_exchange_sems(n_w))(*srcs)


def _final_exchange_adam(grads, big, dw_in, ws, ms, vs):
    n_p = len(grads)
    offs, rows = [], 0
    for p, g in enumerate(grads):
        offs.append(rows)
        rows += 0 if p == big else g.shape[0]
    rows = -(-rows // 8) * 8

    def body(*refs):
        g_refs, w_refs = refs[:n_p], refs[n_p:2 * n_p]
        m_refs, v_refs = refs[2 * n_p:3 * n_p], refs[3 * n_p:4 * n_p]
        din = refs[4 * n_p]
        outs = refs[4 * n_p + 1:8 * n_p + 1]
        pin = refs[8 * n_p + 1]
        pk, rall, send_sems, recv_sems, loc_sem, xs_send, xs_recv, xs_loc = refs[8 * n_p + 2:]
        ex = _Exchange([SCATTER], [din], [pin], xs_send, xs_recv, xs_loc)
        ex.start()
        x, y, c, me = ex.x, ex.y, ex.c, ex.me
        pk[...] = jnp.zeros(pk.shape, F32)
        for p in range(n_p):
            if p != big:
                r, n = g_refs[p].shape
                pk[offs[p]:offs[p] + r, 0:n] = g_refs[p][...]

        def remote(k):
            return pltpu.make_async_remote_copy(
                src_ref=pk, dst_ref=rall.at[me], send_sem=send_sems.at[k - 1], recv_sem=recv_sems.at[k - 1],
                device_id=_peer(k, x, y, c), device_id_type=MESH)

        def arrival(k):
            px, py, pc = _peer(k, x, y, c)
            return pltpu.make_async_remote_copy(
                src_ref=pk, dst_ref=rall.at[4 * px + 2 * py + pc], send_sem=send_sems.at[k - 1],
                recv_sem=recv_sems.at[k - 1], device_id=(px, py, pc), device_id_type=MESH)

        local = pltpu.make_async_copy(pk, rall.at[me], loc_sem)
        local.start()
        for k in range(1, N_DEV):
            remote(k).start()
        local.wait()
        for k in range(1, N_DEV):
            arrival(k).wait_recv()
        for k in range(1, N_DEV):
            remote(k).wait_send()
        for p in range(n_p):
            if p == big:
                g = g_refs[p][0]
                for j in range(1, N_DEV):
                    g = g + g_refs[p][j]
            else:
                r, n = g_refs[p].shape
                sl = (slice(offs[p], offs[p] + r), slice(0, n))
                g = rall[(0,) + sl]
                for j in range(1, N_DEV):
                    g = g + rall[(j,) + sl]
            delta, m2, v2 = _adamw(w_refs[p][...], g, m_refs[p][...], v_refs[p][...])
            outs[p][...] = g
            outs[n_p + p][...] = delta
            outs[2 * n_p + p][...] = m2
            outs[3 * n_p + p][...] = v2
        ex.wait()

    vm = pl.BlockSpec(memory_space=pltpu.VMEM)
    res = pl.pallas_call(
        body, name="final_exchange_adam",
        in_specs=[vm] * (4 * n_p) + [_HBM], out_specs=[vm] * (4 * n_p) + [_HBM],
        out_shape=[_sds(w.shape) for w in ws] * 4 + [_sds(dw_in.shape, dw_in.dtype)],
        scratch_shapes=[pltpu.VMEM((rows, D), F32), pltpu.VMEM((N_DEV, rows, D), F32),
                        pltpu.SemaphoreType.DMA((N_DEV - 1,)), pltpu.SemaphoreType.DMA((N_DEV - 1,)),
                        pltpu.SemaphoreType.DMA] + _exchange_sems(1),
        compiler_params=pltpu.CompilerParams(vmem_limit_bytes=_VMEM_LIMIT),
    )(*grads, *ws, *ms, *vs, dw_in)
    return res[:n_p], res[n_p:2 * n_p], res[2 * n_p:3 * n_p], res[3 * n_p:4 * n_p], res[4 * n_p]


def _shard_adam(name, parts, w, m, v, tr):
    a0, b0 = w.shape
    b = parts.shape[2]

    def body(p_ref, w_ref, m_ref, v_ref, g_out, d_out, m_out, v_out):
        g = p_ref[0].astype(F32)
        for j in range(1, N_DEV):
            g = g + p_ref[j].astype(F32)
        g = g[:, 0:b0]
        delta, m2, v2 = _adamw(w_ref[...], g, m_ref[...], v_ref[...])
        g_out[...] = g
        d_out[...] = delta
        m_out[...] = m2
        v_out[...] = v2

    blk = pl.BlockSpec((tr, b0), lambda i: (i, 0))
    return _pcall(
        body, name, (a0 // tr,),
        [pl.BlockSpec((N_DEV, tr, b), lambda i: (0, i, 0)), blk, blk, blk],
        [blk] * 4, [_sds((a0, b0))] * 4)(parts, w, m, v)


def _fwd_in(x, g_pre, w_in_al, tm):
    T = x.shape[0]

    def body(x_ref, g_ref, w_ref, xp_ref, u_ref):
        nx, _ = _rms(x_ref[...], D)
        u = (nx * g_ref[...]).astype(BF)
        u_ref[...] = u
        xp_ref[...] = jnp.dot(u, w_ref[...], preferred_element_type=F32)

    return _pcall(body, "fwd_in", (T // tm,),
                  [_rows(tm, D), _full((1, D)), _full((D, XP_W))],
                  [_rows(tm, XP_W), _rows(tm, D)],
                  [_sds((T, XP_W)), _sds((T, D), BF)])(x, g_pre, w_in_al)


def _rope(blk, ta, tb1, tb2):
    return blk * ta + pltpu.roll(blk, HB - ROPE // 2, 1) * tb1 + pltpu.roll(blk, ROPE // 2, 1) * tb2


def _unrope(d, ta, tb1, tb2):
    return d * ta + pltpu.roll(d * tb1, ROPE // 2, 1) + pltpu.roll(d * tb2, HB - ROPE // 2, 1)


def _mla_prep(xp, tabs, g_q, g_kv, w_uq, w_uk, w_uv, tm):
    T = xp.shape[0]
    W = N_HEADS * HB

    def body(xp_ref, ta_ref, tb1_ref, tb2_ref, gq_ref, gkv_ref, wuq_ref, wuk_ref, wuv_ref, q_ref, k_ref, v_ref):
        ta, tb1, tb2 = ta_ref[...], tb1_ref[...], tb2_ref[...]
        nq, _ = _rms(xp_ref[:, 0:Q_RANK], Q_RANK)
        nkv, _ = _rms(xp_ref[:, Q_RANK:Q_RANK + KV_RANK], KV_RANK)
        nkv = (nkv * gkv_ref[...]).astype(BF)
        qpre = _dot(nq * gq_ref[...], wuq_ref[...])
        kpre = jnp.dot(nkv, wuk_ref[...], preferred_element_type=F32)
        v_ref[...] = jnp.dot(nkv, wuv_ref[...], preferred_element_type=F32).astype(BF)
        kr = _rope(pltpu.roll(xp_ref[:, Q_RANK + KV_RANK:MLA_IN], NOPE, 1), ta, tb1, tb2)
        for h in range(N_HEADS):
            sl = slice(h * HB, (h + 1) * HB)
            q_ref[:, sl] = _rope(qpre[:, sl], ta, tb1, tb2).astype(BF)
            k_ref[:, sl] = (kpre[:, sl] + kr).astype(BF)

    tab = _rows(tm, HB)
    return _pcall(body, "mla_prep", (T // tm,),
                  [_rows(tm, MLA_IN), tab, tab, tab, _full((1, Q_RANK)), _full((1, KV_RANK)),
                   _full((Q_RANK, W)), _full((KV_RANK, W)), _full((KV_RANK, W))],
                  [_rows(tm, W)] * 3, [_sds((T, W), BF)] * 3)(xp, *tabs, g_q, g_kv, w_uq, w_uk, w_uv)


def _flash_fwd(q, k, v, tq, exchange=None):
    T = q.shape[0]
    scale = QK_DIM ** -0.5

    def body(q_ref, k_ref, v_ref, o_ref, lse_ref):
        i = pl.program_id(1)
        qv = q_ref[...]

        def blk(j, carry, masked):
            m, l, acc = carry
            st = pl.multiple_of(j * tq, tq)
            kv = k_ref[pl.ds(st, tq), :]
            vv = v_ref[pl.ds(st, tq), :]
            s = _dot_nt(qv, kv) * scale
            if masked:
                r = lax.broadcasted_iota(jnp.int32, (tq, tq), 0)
                c = lax.broadcasted_iota(jnp.int32, (tq, tq), 1)
                s = jnp.where(c <= r, s, NEG)
            m2 = jnp.maximum(m, jnp.max(s, -1, keepdims=True))
            p = jnp.exp(s - m2)
            a = jnp.exp(m - m2)
            return m2, a * l + jnp.sum(p, -1, keepdims=True), a * acc + _dot(p, vv)

        init = (jnp.full((tq, 1), NEG, F32), jnp.zeros((tq, 1), F32), jnp.zeros((tq, HB), F32))
        carry = lax.fori_loop(0, i, lambda j, cr: blk(j, cr, False), init)
        m, l, acc = blk(i, carry, True)
        o_ref[...] = acc / l
        lse_ref[...] = jnp.broadcast_to(m + jnp.log(l), (tq, HB))

    qs = pl.BlockSpec((tq, HB), lambda h, i: (i, h))
    kvs = pl.BlockSpec((T, HB), lambda h, i: (0, h))
    return _pcall(body, "flash_fwd", (N_HEADS, T // tq), [qs, kvs, kvs], [qs, qs],
                  [_sds((T, N_HEADS * HB))] * 2, exchange=exchange)(q, k, v)


def _gates(hq, hf, lb):
    sig = _sigmoid(hf)
    f = lb + (1.0 - lb) * sig
    sq = _sigmoid(hq)
    return hq * sq, 1.0 - f, f, jnp.log(f), sig, sq


def _lower_bound(lbl_ref):
    l0, l1 = lbl_ref[0:1, :], lbl_ref[1:2, :]
    mx = jnp.maximum(l0, l1)
    e0, e1 = jnp.exp(l0 - mx), jnp.exp(l1 - mx)
    return e0 / (e0 + e1)


def _split3(x):
    hi = x.astype(BF)
    r1 = x - hi.astype(F32)
    mid = r1.astype(BF)
    lo = (r1 - mid.astype(F32)).astype(BF)
    return hi, mid, lo


def _tri_mm(tri, x):
    hi, mid, lo = _split3(x)
    mm = lambda t: jnp.dot(tri, t, preferred_element_type=F32)
    return mm(hi) + mm(mid) + mm(lo)


def _anchor_mask(i, row, col):
    return ((row >> 4) == i) & (col < SUB * i)


def _diag_mask(d, row, col):
    return (col == row - d) & ((row & (SUB - 1)) >= d)


def _intra(q, k, b, b_s, row, col):
    a = jnp.zeros((CHUNK, CHUNK), F32)
    for i in range(1, CHUNK // SUB):
        b0 = b_s[SUB * i - 1:SUB * i, :]
        qs = q * jnp.exp(jnp.minimum(b - b0, 0.0))
        ks = k * jnp.exp(jnp.minimum(b0 - b, 0.0))
        a = a + jnp.where(_anchor_mask(i, row, col), _dot_nt(qs, ks), 0.0)
    for d in range(SUB):
        ksh = pltpu.roll(k, d, 0) if d else k
        bsh = pltpu.roll(b, d, 0) if d else b
        e = jnp.exp(jnp.minimum(b - bsh, 0.0))
        val = jnp.sum(q * ksh * e, -1, keepdims=True)
        a = a + jnp.where(_diag_mask(d, row, col), val, 0.0)
    return a


def _intra_bwd(q, k, b, b_s, da, row, col):
    dq = jnp.zeros((CHUNK, G_DIM), F32)
    dk = jnp.zeros((CHUNK, G_DIM), F32)
    for i in range(1, CHUNK // SUB):
        b0 = b_s[SUB * i - 1:SUB * i, :]
        eq = jnp.exp(jnp.minimum(b - b0, 0.0))
        ek = jnp.exp(jnp.minimum(b0 - b, 0.0))
        dai = jnp.where(_anchor_mask(i, row, col), da, 0.0)
        dq = dq + _dot(dai, k * ek) * eq
        dk = dk + _dot_tn(dai, q * eq) * ek
    for d in range(SUB):
        ksh = pltpu.roll(k, d, 0) if d else k
        bsh = pltpu.roll(b, d, 0) if d else b
        e = jnp.exp(jnp.minimum(b - bsh, 0.0))
        g = jnp.sum(jnp.where(_diag_mask(d, row, col), da, 0.0), -1, keepdims=True) * e
        dq = dq + g * ksh
        cb = g * q
        dk = dk + (pltpu.roll(cb, CHUNK - d, 0) if d else cb)
    return dq, dk


def _hgrn_fwd(xp, lb_logits, g_hn, exchange=None):
    T = xp.shape[0]
    nc = T // CHUNK
    off = MLA_IN // G_DIM

    def body(hq_ref, hf_ref, hi_ref, hg_ref, lbl_ref, ghn_ref, out_ref, oraw_ref, sall_ref, st_ref, b_s):
        lb = _lower_bound(lbl_ref)
        st_ref[...] = jnp.zeros(st_ref.shape, F32)
        row = lax.broadcasted_iota(jnp.int32, (CHUNK, CHUNK), 0)
        col = lax.broadcasted_iota(jnp.int32, (CHUNK, CHUNK), 1)
        tri = (col <= row).astype(BF)

        def chunk(c, carry):
            sl = pl.ds(pl.multiple_of(c * CHUNK, CHUNK), CHUNK)
            q, k, _, lf, _, _ = _gates(hq_ref[sl, :], hf_ref[sl, :], lb)
            v = hi_ref[sl, :]
            b = _tri_mm(tri, lf)
            b_s[...] = b
            st = st_ref[...]
            sall_ref[c, 0] = st
            o = _dot_nt(q * jnp.exp(b), st) + _dot(_intra(q, k, b, b_s, row, col), v)
            bl = b_s[CHUNK - 1:CHUNK, :]
            st_ref[...] = st * jnp.exp(bl) + _dot_tn(v, k * jnp.exp(bl - b))
            oraw_ref[sl, :] = o
            n, _ = _rms(o, G_DIM)
            hg = hg_ref[sl, :]
            out_ref[sl, :] = n * ghn_ref[...] * (hg * _sigmoid(hg))
            return carry

        lax.fori_loop(0, nc, chunk, 0)

    col_blk = lambda j: pl.BlockSpec((T, G_DIM), lambda h: (0, off + j * G_HEADS + h))
    head = pl.BlockSpec((T, G_DIM), lambda h: (0, h))
    return _pcall(
        body, "hgrn_fwd", (G_HEADS,),
        [col_blk(0), col_blk(1), col_blk(2), col_blk(3),
         pl.BlockSpec((2, G_DIM), lambda h: (0, h)), pl.BlockSpec((1, G_DIM), lambda h: (0, h))],
        [head, head, pl.BlockSpec((nc, 1, G_DIM, G_DIM), lambda h: (0, h, 0, 0))],
        [_sds((T, G_W)), _sds((T, G_W)), _sds((nc, G_HEADS, G_DIM, G_DIM))],
        scratch=[pltpu.VMEM((G_DIM, G_DIM), F32), pltpu.VMEM((CHUNK, G_DIM), F32)], exchange=exchange,
    )(xp, xp, xp, xp, lb_logits, g_hn)


def _fwd_out(o_pad, o_hgrn, x, g_on, w_out, g_post, g_fpre, tm):
    T = x.shape[0]

    def body(o_ref, oh_ref, x_ref, gon_ref, w_ref, gpost_ref, gfpre_ref, h1_ref, y1_ref, z_ref, mix_ref):
        for h in range(N_HEADS):
            sl = slice(h * HB, (h + 1) * HB)
            n, _ = _rms(o_ref[:, sl], V_DIM)
            mix_ref[:, sl] = (n * gon_ref[:, sl]).astype(BF)
        mix_ref[:, N_HEADS * HB:MIX_W] = oh_ref[...].astype(BF)
        y1 = jnp.dot(mix_ref[...], w_ref[...], preferred_element_type=F32)
        y1_ref[...] = y1
        ny, _ = _rms(y1, D)
        h1 = x_ref[...] + ny * gpost_ref[...]
        h1_ref[...] = h1
        nh, _ = _rms(h1, D)
        z_ref[...] = (nh * gfpre_ref[...]).astype(BF)

    return _pcall(body, "fwd_out", (T // tm,),
                  [_rows(tm, N_HEADS * HB), _rows(tm, G_W), _rows(tm, D), _full((1, N_HEADS * HB)),
                   _full((MIX_W, D)), _full((1, D)), _full((1, D))],
                  [_rows(tm, D), _rows(tm, D), _rows(tm, D), _rows(tm, MIX_W)],
                  [_sds((T, D)), _sds((T, D)), _sds((T, D), BF), _sds((T, MIX_W), BF)],
                  )(o_pad, o_hgrn, x, g_on, w_out, g_post, g_fpre)


def _ffn_fwd(z, wg, wu, wd, h1, tgt, g_fpost, tm):
    T = z.shape[0]
    nf = wg.shape[0]

    def body(z_ref, wg_ref, wu_ref, wd_ref, h1_ref, t_ref, gp_ref,
             gs_ref, us_ref, dh2_ref, dy2_ref, dgp_ref, loss_ref, acc):
        i, j = pl.program_id(0), pl.program_id(1)
        zv = z_ref[...]
        g = jnp.dot(zv, wg_ref[0], preferred_element_type=F32)
        u = jnp.dot(zv, wu_ref[0], preferred_element_type=F32)
        gs_ref[...] = g.astype(BF)
        us_ref[...] = u.astype(BF)
        part = _dot(g * _sigmoid(g) * u, wd_ref[0])

        @pl.when(j == 0)
        def _():
            acc[...] = part

        @pl.when(j > 0)
        def _():
            acc[...] += part

        @pl.when((i == 0) & (j == 0))
        def _():
            dgp_ref[...] = jnp.zeros(dgp_ref.shape, F32)
            loss_ref[...] = jnp.zeros(loss_ref.shape, F32)

        @pl.when(j == nf - 1)
        def _():
            ny, r = _rms(acc[...], D)
            err = h1_ref[...] + ny * gp_ref[...] - t_ref[...]
            loss_ref[...] += 0.5 * jnp.sum(jnp.sum(err * err, -1, keepdims=True) * (1.0 / D), 0, keepdims=True)
            dh2 = err * (1.0 / D)
            dh2_ref[...] = dh2
            dy2, dgp = _rms_bwd(ny, r, gp_ref[...], dh2, D)
            dy2_ref[...] = dy2.astype(BF)
            dgp_ref[...] += dgp

    tok = lambda n: pl.BlockSpec((tm, n), lambda i, j: (i, 0))
    col = pl.BlockSpec((tm, FF_PAD), lambda i, j: (i, j))
    return _pcall(
        body, "ffn_fwd", (T // tm, nf),
        [tok(D), pl.BlockSpec((1, D, FF_PAD), lambda i, j: (j, 0, 0)), pl.BlockSpec((1, D, FF_PAD), lambda i, j: (j, 0, 0)),
         pl.BlockSpec((1, FF_PAD, D), lambda i, j: (j, 0, 0)), tok(D), tok(D), _full((1, D))],
        [col, col, tok(D), tok(D), _full((1, D)), _full((1, HB))],
        [_sds((T, nf * FF_PAD), BF), _sds((T, nf * FF_PAD), BF), _sds((T, D)), _sds((T, D), BF),
         _sds((1, D)), _sds((1, HB))],
        scratch=[pltpu.VMEM((tm, D), F32)],
    )(z, wg, wu, wd, h1, tgt, g_fpost)


def _dsilu(x, s):
    return s * (1.0 + x * (1.0 - s))


def _ffn_bwd_x(dy2, gs, us, wg, wu, wd, h1, y1, dh2, g_fpre, g_post, tm):
    T = dy2.shape[0]
    nf = wg.shape[0]

    def body(dy2_ref, gs_ref, us_ref, wg_ref, wu_ref, wd_ref, h1_ref, y1_ref, dh2_ref, gf_ref, gp_ref,
             dgs_ref, dus_ref, dh1_ref, dy1_ref, dgf_ref, dgp_ref, acc):
        i, j = pl.program_id(0), pl.program_id(1)
        dff = _dot_nt(dy2_ref[...], wd_ref[0])
        g = gs_ref[...].astype(F32)
        u = us_ref[...].astype(F32)
        s = _sigmoid(g)
        dg = (dff * u * _dsilu(g, s)).astype(BF)
        du = (dff * g * s).astype(BF)
        dgs_ref[...] = dg
        dus_ref[...] = du
        part = _dot_nt(dg, wg_ref[0]) + _dot_nt(du, wu_ref[0])

        @pl.when(j == 0)
        def _():
            acc[...] = part

        @pl.when(j > 0)
        def _():
            acc[...] += part

        @pl.when((i == 0) & (j == 0))
        def _():
            dgf_ref[...] = jnp.zeros(dgf_ref.shape, F32)
            dgp_ref[...] = jnp.zeros(dgp_ref.shape, F32)

        @pl.when(j == nf - 1)
        def _():
            nh, rh = _rms(h1_ref[...], D)
            dh, dgf = _rms_bwd(nh, rh, gf_ref[...], acc[...], D)
            dh1 = dh2_ref[...] + dh
            dh1_ref[...] = dh1
            dgf_ref[...] += dgf
            ny, ry = _rms(y1_ref[...], D)
            dy1, dgp = _rms_bwd(ny, ry, gp_ref[...], dh1, D)
            dy1_ref[...] = dy1.astype(BF)
            dgp_ref[...] += dgp

    tok = lambda n: pl.BlockSpec((tm, n), lambda i, j: (i, 0))
    col = pl.BlockSpec((tm, FF_PAD), lambda i, j: (i, j))
    wcol = pl.BlockSpec((1, D, FF_PAD), lambda i, j: (j, 0, 0))
    return _pcall(
        body, "ffn_bwd_x", (T // tm, nf),
        [tok(D), col, col, wcol, wcol, pl.BlockSpec((1, FF_PAD, D), lambda i, j: (j, 0, 0)),
         tok(D), tok(D), tok(D), _full((1, D)), _full((1, D))],
        [col, col, tok(D), tok(D), _full((1, D)), _full((1, D))],
        [_sds((T, nf * FF_PAD), BF), _sds((T, nf * FF_PAD), BF), _sds((T, D)), _sds((T, D), BF),
         _sds((1, D)), _sds((1, D))],
        scratch=[pltpu.VMEM((tm, D), F32)],
    )(dy2, gs, us, wg, wu, wd, h1, y1, dh2, g_fpre, g_post)


def _ffn_bwd_w(z, gs, us, dgs, dus, dy2, tm):
    T = z.shape[0]
    nf = gs.shape[1] // FF_PAD
    nt = T // tm

    def body(z_ref, gs_ref, us_ref, dgs_ref, dus_ref, dy2_ref, dwg_ref, dwu_ref, dwd_ref, ag, au, ad):
        i = pl.program_id(1)
        g = gs_ref[...].astype(F32)
        ff = g * _sigmoid(g) * us_ref[...].astype(F32)
        pg = _dot_tn(z_ref[...], dgs_ref[...])
        pu = _dot_tn(z_ref[...], dus_ref[...])
        pd = _dot_tn(ff, dy2_ref[...])

        @pl.when(i == 0)
        def _():
            ag[...] = pg
            au[...] = pu
            ad[...] = pd

        @pl.when(i > 0)
        def _():
            ag[...] += pg
            au[...] += pu
            ad[...] += pd

        @pl.when(i == nt - 1)
        def _():
            dwg_ref[0] = ag[...].astype(BF)
            dwu_ref[0] = au[...].astype(BF)
            dwd_ref[0] = ad[...].astype(BF)

    tok = lambda n: pl.BlockSpec((tm, n), lambda j, i: (i, 0))
    col = pl.BlockSpec((tm, FF_PAD), lambda j, i: (i, j))
    wcol = pl.BlockSpec((1, D, FF_PAD), lambda j, i: (j, 0, 0))
    wrow = pl.BlockSpec((1, FF_PAD, D), lambda j, i: (j, 0, 0))
    return _pcall(
        body, "ffn_bwd_w", (nf, nt), [tok(D), col, col, col, col, tok(D)], [wcol, wcol, wrow],
        [_sds((nf, D, FF_PAD), BF), _sds((nf, D, FF_PAD), BF), _sds((nf, FF_PAD, D), BF)],
        scratch=[pltpu.VMEM((D, FF_PAD), F32), pltpu.VMEM((D, FF_PAD), F32), pltpu.VMEM((FF_PAD, D), F32)],
    )(z, gs, us, dgs, dus, dy2)


def _out_bwd(dy1, mix, o_pad, w_out, g_on, tm):
    T = dy1.shape[0]
    W = N_HEADS * HB

    def body(dy1_ref, mix_ref, o_ref, w_ref, gon_ref, do_ref, dl_ref, dohg_ref, dw_ref, dgon_ref):
        i = pl.program_id(0)
        dy1v = dy1_ref[...]
        dmix = _dot_nt(dy1v, w_ref[...])
        pw = _dot_tn(mix_ref[...], dy1v)

        @pl.when(i == 0)
        def _():
            dw_ref[...] = pw
            dgon_ref[...] = jnp.zeros(dgon_ref.shape, F32)

        @pl.when(i > 0)
        def _():
            dw_ref[...] += pw

        for h in range(N_HEADS):
            sl = slice(h * HB, (h + 1) * HB)
            ov = o_ref[:, sl]
            n, r = _rms(ov, V_DIM)
            do, dg = _rms_bwd(n, r, gon_ref[:, sl], dmix[:, sl], V_DIM)
            dgon_ref[:, sl] += dg
            do_ref[:, sl] = do.astype(BF)
            dl_ref[:, sl] = jnp.broadcast_to(jnp.sum(do * ov, -1, keepdims=True), (tm, HB))
        dohg_ref[...] = dmix[:, W:MIX_W]

    return _pcall(body, "out_bwd", (T // tm,),
                  [_rows(tm, D), _rows(tm, MIX_W), _rows(tm, W), _full((MIX_W, D)), _full((1, W))],
                  [_rows(tm, W), _rows(tm, W), _rows(tm, G_W), _full((MIX_W, D)), _full((1, W))],
                  [_sds((T, W), BF), _sds((T, W)), _sds((T, G_W)), _sds((MIX_W, D)), _sds((1, W))],
                  )(dy1, mix, o_pad, w_out, g_on)


def _flash_bwd(q, k, v, do, lse, dl, tq, exchange=None):
    T = q.shape[0]
    nq = T // tq
    scale = QK_DIM ** -0.5

    def body(k_ref, v_ref, q_ref, do_ref, lse_ref, dl_ref, dk_ref, dv_ref, dq_ref):
        j = pl.program_id(1)

        @pl.when(j == 0)
        def _():
            dq_ref[...] = jnp.zeros(dq_ref.shape, F32)

        kv, vv = k_ref[...], v_ref[...]

        def blk(i, carry, masked):
            dk, dv = carry
            sl = pl.ds(pl.multiple_of(i * tq, tq), tq)
            qv, dov = q_ref[sl, :], do_ref[sl, :]
            s = _dot_nt(qv, kv) * scale
            if masked:
                r = lax.broadcasted_iota(jnp.int32, (tq, tq), 0)
                c = lax.broadcasted_iota(jnp.int32, (tq, tq), 1)
                s = jnp.where(c <= r, s, NEG)
            p = jnp.exp(s - lse_ref[sl, 0:1])
            ds = p * (_dot_nt(dov, vv) - dl_ref[sl, 0:1]) * scale
            dq_ref[sl, :] += _dot(ds, kv)
            return dk + _dot_tn(ds, qv), dv + _dot_tn(p, dov)

        zero = jnp.zeros((tq, HB), F32)
        carry = blk(j, (zero, zero), True)
        dk, dv = lax.fori_loop(j + 1, nq, lambda i, cr: blk(i, cr, False), carry)
        dk_ref[...] = dk
        dv_ref[...] = dv

    tile = pl.BlockSpec((tq, HB), lambda h, j: (j, h))
    whole = pl.BlockSpec((T, HB), lambda h, j: (0, h))
    return _pcall(body, "flash_bwd", (N_HEADS, nq), [tile, tile, whole, whole, whole, whole],
                  [tile, tile, whole], [_sds((T, N_HEADS * HB))] * 3, exchange=exchange)(k, v, q, do, lse, dl)


def _mla_prep_bwd(xp, tabs, dq, dk, dv, g_q, g_kv, w_uq, w_uk, w_uv, tm):
    T = xp.shape[0]
    W = N_HEADS * HB

    def body(xp_ref, ta_ref, tb1_ref, tb2_ref, dq_ref, dk_ref, dv_ref, gq_ref, gkv_ref, wuq_ref, wuk_ref, wuv_ref,
             dxp_ref, dwuq_ref, dwuk_ref, dwuv_ref, dgq_ref, dgkv_ref, dqp):
        i = pl.program_id(0)
        ta, tb1, tb2 = ta_ref[...], tb1_ref[...], tb2_ref[...]
        nq, rq = _rms(xp_ref[:, 0:Q_RANK], Q_RANK)
        nkv, rkv = _rms(xp_ref[:, Q_RANK:Q_RANK + KV_RANK], KV_RANK)
        dkr = jnp.zeros((tm, HB), F32)
        for h in range(N_HEADS):
            sl = slice(h * HB, (h + 1) * HB)
            dqp[:, sl] = _unrope(dq_ref[:, sl], ta, tb1, tb2).astype(BF)
            dkr = dkr + dk_ref[:, sl]
        dkr = pltpu.roll(_unrope(dkr, ta, tb1, tb2), HB - NOPE, 1)
        lane = lax.broadcasted_iota(jnp.int32, (tm, HB), 1)
        dxp_ref[:, Q_RANK + KV_RANK:MLA_IN] = jnp.where(lane < ROPE, dkr, 0.0)
        dqpv = dqp[...]
        dkv, dvv = dk_ref[...].astype(BF), dv_ref[...].astype(BF)
        nqs = (nq * gq_ref[...]).astype(BF)
        nkvs = (nkv * gkv_ref[...]).astype(BF)
        pq, pk, pv = _dot_tn(nqs, dqpv), _dot_tn(nkvs, dkv), _dot_tn(nkvs, dvv)
        dcq, dgq = _rms_bwd(nq, rq, gq_ref[...], _dot_nt(dqpv, wuq_ref[...]), Q_RANK)
        dckv, dgkv = _rms_bwd(nkv, rkv, gkv_ref[...], _dot_nt(dkv, wuk_ref[...]) + _dot_nt(dvv, wuv_ref[...]), KV_RANK)
        dxp_ref[:, 0:Q_RANK] = dcq
        dxp_ref[:, Q_RANK:Q_RANK + KV_RANK] = dckv

        @pl.when(i == 0)
        def _():
            dwuq_ref[...] = pq
            dwuk_ref[...] = pk
            dwuv_ref[...] = pv
            dgq_ref[...] = dgq
            dgkv_ref[...] = dgkv

        @pl.when(i > 0)
        def _():
            dwuq_ref[...] += pq
            dwuk_ref[...] += pk
            dwuv_ref[...] += pv
            dgq_ref[...] += dgq
            dgkv_ref[...] += dgkv

    tab = _rows(tm, HB)
    return _pcall(
        body, "mla_prep_bwd", (T // tm,),
        [_rows(tm, MLA_IN), tab, tab, tab, _rows(tm, W), _rows(tm, W), _rows(tm, W), _full((1, Q_RANK)),
         _full((1, KV_RANK)), _full((Q_RANK, W)), _full((KV_RANK, W)), _full((KV_RANK, W))],
        [_rows(tm, MLA_IN), _full((Q_RANK, W)), _full((KV_RANK, W)), _full((KV_RANK, W)), _full((1, Q_RANK)),
         _full((1, KV_RANK))],
        [_sds((T, MLA_IN)), _sds((Q_RANK, W)), _sds((KV_RANK, W)), _sds((KV_RANK, W)), _sds((1, Q_RANK)),
         _sds((1, KV_RANK))],
        scratch=[pltpu.VMEM((tm, W), BF)],
    )(xp, *tabs, dq, dk, dv, g_q, g_kv, w_uq, w_uk, w_uv)


def _hgrn_bwd(xp, o_raw, s_all, d_out, lb_logits, g_hn, exchange=None):
    T = xp.shape[0]
    nc = T // CHUNK
    off = MLA_IN // G_DIM

    def body(hq_ref, hf_ref, hi_ref, hg_ref, o_ref, sall_ref, dout_ref, lbl_ref, ghn_ref,
             dhq_ref, dhf_ref, dhi_ref, dhg_ref, dlbl_ref, dghn_ref, dst_ref, b_s, acc_lb, acc_g):
        lb = _lower_bound(lbl_ref)
        dst_ref[...] = jnp.zeros(dst_ref.shape, F32)
        acc_lb[...] = jnp.zeros(acc_lb.shape, F32)
        acc_g[...] = jnp.zeros(acc_g.shape, F32)
        row = lax.broadcasted_iota(jnp.int32, (CHUNK, CHUNK), 0)
        col = lax.broadcasted_iota(jnp.int32, (CHUNK, CHUNK), 1)
        tri = (col <= row).astype(BF)
        tri_t = (col >= row).astype(BF)
        last = lax.broadcasted_iota(jnp.int32, (CHUNK, G_DIM), 0) == CHUNK - 1
        ghn = ghn_ref[...]

        def chunk(cc, carry):
            c = nc - 1 - cc
            sl = pl.ds(pl.multiple_of(c * CHUNK, CHUNK), CHUNK)
            hq, hg = hq_ref[sl, :], hg_ref[sl, :]
            q, k, f, lf, sig, sq = _gates(hq, hf_ref[sl, :], lb)
            v = hi_ref[sl, :]
            b = _tri_mm(tri, lf)
            b_s[...] = b
            st = sall_ref[c, 0]
            dstn = dst_ref[...]
            o = o_ref[sl, :]
            dout = dout_ref[sl, :]
            n, r = _rms(o, G_DIM)
            sg = _sigmoid(hg)
            dhg_ref[sl, :] = dout * (n * ghn) * _dsilu(hg, sg)
            do, dg = _rms_bwd(n, r, ghn, dout * (hg * sg), G_DIM)
            acc_g[...] += dg
            eb = jnp.exp(b)
            bl = b_s[CHUNK - 1:CHUNK, :]
            ebl = jnp.exp(bl)
            ekd = jnp.exp(bl - b)
            kd = k * ekd
            a = _intra(q, k, b, b_s, row, col)
            dq_i, dk_i = _intra_bwd(q, k, b, b_s, _dot_nt(do, v), row, col)
            dhi_ref[sl, :] = _dot_tn(a, do) + _dot_nt(kd, dstn)
            dk_state = _dot(v, dstn) * ekd
            dq = dq_i + _dot(do, st) * eb
            dk = dk_i + dk_state
            dbl = jnp.sum(k * dk_state, 0, keepdims=True) + ebl * jnp.sum(dstn * st, 0, keepdims=True)
            db = q * dq - k * dk + jnp.where(last, dbl, 0.0)
            df = _tri_mm(tri_t, db) / f - dk
            dhf_ref[sl, :] = df * (1.0 - lb) * sig * (1.0 - sig)
            acc_lb[...] += jnp.sum(df * (1.0 - sig), 0, keepdims=True)
            dhq_ref[sl, :] = dq * _dsilu(hq, sq)
            dst_ref[...] = dstn * ebl + _dot_tn(do, q * eb)
            return carry

        lax.fori_loop(0, nc, chunk, 0)
        dl0 = acc_lb[...] * lb * (1.0 - lb)
        dlbl_ref[0:1, :] = dl0
        dlbl_ref[1:2, :] = -dl0
        dghn_ref[...] = acc_g[...]

    col_blk = lambda j: pl.BlockSpec((T, G_DIM), lambda h: (0, off + j * G_HEADS + h))
    head = pl.BlockSpec((T, G_DIM), lambda h: (0, h))
    two = pl.BlockSpec((2, G_DIM), lambda h: (0, h))
    one = pl.BlockSpec((1, G_DIM), lambda h: (0, h))
    res = _pcall(
        body, "hgrn_bwd", (G_HEADS,),
        [col_blk(0), col_blk(1), col_blk(2), col_blk(3), head,
         pl.BlockSpec((nc, 1, G_DIM, G_DIM), lambda h: (0, h, 0, 0)), head, two, one],
        [head, head, head, head, two, one],
        [_sds((T, G_W))] * 4 + [_sds((2, G_W)), _sds((1, G_W))],
        scratch=[pltpu.VMEM((G_DIM, G_DIM), F32), pltpu.VMEM((CHUNK, G_DIM), F32),
                 pltpu.VMEM((1, G_DIM), F32), pltpu.VMEM((1, G_DIM), F32)], exchange=exchange,
    )(xp, xp, xp, xp, o_raw, s_all, d_out, lb_logits, g_hn)
    return res


def _in_bwd_x(x, dxp_m, dxp_h, dh1, w_in_al, g_pre, tm):
    T = x.shape[0]

    def body(x_ref, dm_ref, d0_ref, d1_ref, d2_ref, d3_ref, dh1_ref, w_ref, g_ref, dx_ref, dg_ref):
        i = pl.program_id(0)
        du = _dot_nt(dm_ref[...], w_ref[:, 0:MLA_IN])
        for j, d_ref in enumerate((d0_ref, d1_ref, d2_ref, d3_ref)):
            du = du + _dot_nt(d_ref[...], w_ref[:, MLA_IN + j * G_W:MLA_IN + (j + 1) * G_W])
        nx, r = _rms(x_ref[...], D)
        dx, dg = _rms_bwd(nx, r, g_ref[...], du, D)
        dx_ref[...] = dh1_ref[...] + dx

        @pl.when(i == 0)
        def _():
            dg_ref[...] = dg

        @pl.when(i > 0)
        def _():
            dg_ref[...] += dg

    return _pcall(body, "in_bwd_x", (T // tm,),
                  [_rows(tm, D), _rows(tm, MLA_IN)] + [_rows(tm, G_W)] * 4 + [_rows(tm, D), _full((D, XP_W)), _full((1, D))],
                  [_rows(tm, D), _full((1, D))], [_sds((T, D)), _sds((1, D))],
                  )(x, dxp_m, *dxp_h, dh1, w_in_al, g_pre)


def _mm_tn(name, a, b, tn, tt):
    T, M = a.shape
    N = b.shape[1]

    def body(a_ref, b_ref, o_ref):
        t = pl.program_id(1)
        part = _dot_tn(a_ref[...], b_ref[...])

        @pl.when(t == 0)
        def _():
            o_ref[...] = part

        @pl.when(t > 0)
        def _():
            o_ref[...] += part

    return _pcall(body, name, (N // tn, T // tt),
                  [pl.BlockSpec((tt, M), lambda n, t: (t, 0)), pl.BlockSpec((tt, tn), lambda n, t: (t, n))],
                  pl.BlockSpec((M, tn), lambda n, t: (0, n)), _sds((M, N)))(a, b)


def _pad_heads(w, width, real):
    lead = w.shape[:-1]
    w = w.reshape(lead + (N_HEADS, real))
    w = jnp.pad(w, [(0, 0)] * len(lead) + [(0, 0), (0, width - real)])
    return w.reshape(lead + (N_HEADS * width,))


def _unpad_heads(w, width, real):
    lead = w.shape[:-1]
    return w.reshape(lead + (N_HEADS, width))[..., :real].reshape(lead + (N_HEADS * real,))


def _rope_tables(positions):
    half = ROPE // 2
    inv_freq = 1.0 / (ROPE_THETA ** (jnp.arange(0, ROPE, 2, dtype=F32) / ROPE))
    ang = positions.astype(F32)[:, None] * inv_freq
    cos, sin = jnp.cos(ang), jnp.sin(ang)
    T = positions.shape[0]
    z = lambda n: jnp.zeros((T, n), F32)
    ta = jnp.concatenate([jnp.ones((T, NOPE), F32), cos, cos, z(HB - QK_DIM)], 1)
    tb1 = jnp.concatenate([z(NOPE), -sin, z(half), z(HB - QK_DIM)], 1)
    tb2 = jnp.concatenate([z(NOPE), z(half), sin, z(HB - QK_DIM)], 1)
    return ta, tb1, tb2


def kernel(x, positions, attn_pre_norm, w_in, mla_q_norm, mla_w_uq, mla_kv_norm, mla_w_ukv, mla_out_norm, hgrn_lb_logits, hgrn_out_norm, w_out, attn_post_norm, ffn_pre_norm, w_gate, w_up, w_down, ffn_post_norm, loss_target, m_attn_pre_norm, m_w_in, m_mla_q_norm, m_mla_w_uq, m_mla_kv_norm, m_mla_w_ukv, m_mla_out_norm, m_hgrn_lb_logits, m_hgrn_out_norm, m_w_out, m_attn_post_norm, m_ffn_pre_norm, m_w_gate, m_w_up, m_w_down, m_ffn_post_norm, v_attn_pre_norm, v_w_in, v_mla_q_norm, v_mla_w_uq, v_mla_kv_norm, v_mla_w_ukv, v_mla_out_norm, v_hgrn_lb_logits, v_hgrn_out_norm, v_w_out, v_attn_post_norm, v_ffn_pre_norm, v_w_gate, v_w_up, v_w_down, v_ffn_post_norm):
    T = x.shape[1]
    tm = min(_TM, T)
    tq = min(_TQ, T)
    xs, tgt = x[0], loss_target[0]
    uq_sh = (Q_RANK // N_DEV, N_HEADS * QK_DIM)

    b_in, b_uq, b_out, b_g, b_u, b_d = _cast_shards(
        w_in[0], mla_w_uq[0].reshape(uq_sh), w_out[0], w_gate[0], w_up[0], w_down[0])
    g_in, g_uq = _exchange_call("ag_first", [GATHER, GATHER], [b_in, b_uq])
    w_in_full = g_in.transpose(1, 0, 2).reshape(D, IN_W)
    kr_end = Q_RANK + KV_RANK + ROPE
    w_in_al = jnp.concatenate([w_in_full[:, :kr_end], jnp.zeros((D, KR_PAD - ROPE), BF), w_in_full[:, kr_end:]], 1)
    w_uq_p = _pad_heads(g_uq.reshape(Q_RANK, N_HEADS * QK_DIM), HB, QK_DIM)
    w_ukv = mla_w_ukv[0].astype(BF)
    w_uk_p = _pad_heads(w_ukv[..., :NOPE].reshape(KV_RANK, N_HEADS * NOPE), HB, NOPE)
    w_uv_p = _pad_heads(w_ukv[..., NOPE:].reshape(KV_RANK, N_HEADS * V_DIM), HB, V_DIM)
    g_on_p = _pad_heads(mla_out_norm, HB, V_DIM)
    tabs = _rope_tables(positions[0])

    xp, u = _fwd_in(xs, attn_pre_norm, w_in_al, tm)
    q_att, k_att, v_att = _mla_prep(xp, tabs, mla_q_norm, mla_kv_norm, w_uq_p, w_uk_p, w_uv_p, tm)
    o_hgrn, o_raw, s_all, g_out, wd = _hgrn_fwd(xp, hgrn_lb_logits, hgrn_out_norm, ([GATHER, GATHER], [b_out, b_d]))
    o_pad, lse, wg, wu = _flash_fwd(q_att, k_att, v_att, tq, ([GATHER, GATHER], [b_g, b_u]))
    w_out_full = g_out.reshape(D, D)
    w_out_mla = jnp.pad(w_out_full[:N_HEADS * V_DIM].reshape(N_HEADS, V_DIM, D), ((0, 0), (0, HB - V_DIM), (0, 0)))
    w_out_p = jnp.concatenate([w_out_mla.reshape(N_HEADS * HB, D), w_out_full[N_HEADS * V_DIM:]], 0)
    h1, y1, z, mix = _fwd_out(o_pad, o_hgrn, xs, g_on_p, w_out_p, attn_post_norm, ffn_pre_norm, tm)
    gs, us, dh2, dy2, d_fpost, loss_row = _ffn_fwd(z, wg, wu, wd, h1, tgt, ffn_post_norm, tm)

    dgs, dus, dh1, dy1, d_fpre, d_post = _ffn_bwd_x(dy2, gs, us, wg, wu, wd, h1, y1, dh2, ffn_pre_norm, attn_post_norm, tm)
    dwg, dwu, dwd = _ffn_bwd_w(z, gs, us, dgs, dus, dy2, tm)
    do_pad, dl, d_ohg, dw_out_p, d_on_p = _out_bwd(dy1, mix, o_pad, w_out_p, g_on_p, tm)
    dw_out_mla = dw_out_p[:N_HEADS * HB].reshape(N_HEADS, HB, D)[:, :V_DIM].reshape(N_HEADS * V_DIM, D)
    dw_out = jnp.concatenate([dw_out_mla, dw_out_p[N_HEADS * HB:]], 0).reshape(N_DEV, D // N_DEV, D).astype(BF)
    dk_att, dv_att, dq_att, p_g, p_u, p_d, p_out = _flash_bwd(
        q_att, k_att, v_att, do_pad, lse, dl, tq, ([SCATTER] * 4, [dwg, dwu, dwd, dw_out]))
    dxp_m, dw_uq_p, dw_uk_p, dw_uv_p, d_gq, d_gkv = _mla_prep_bwd(
        xp, tabs, dq_att, dk_att, dv_att, mla_q_norm, mla_kv_norm, w_uq_p, w_uk_p, w_uv_p, tm)
    dw_uq = _unpad_heads(dw_uq_p, HB, QK_DIM).reshape((N_DEV,) + uq_sh).astype(BF)
    dw_ukv = jnp.concatenate([_unpad_heads(dw_uk_p, HB, NOPE).reshape(KV_RANK, N_HEADS, NOPE),
                              _unpad_heads(dw_uv_p, HB, V_DIM).reshape(KV_RANK, N_HEADS, V_DIM)], -1)
    *dxp_h, d_lbl, d_ghn, p_uq, dw_ukv_all = _hgrn_bwd(
        xp, o_raw, s_all, d_ohg, hgrn_lb_logits, hgrn_out_norm,
        ([SCATTER, GATHER], [dw_uq, dw_ukv.reshape(KV_RANK, N_HEADS * HB)]))
    grad_x, d_pre = _in_bwd_x(xs, dxp_m, dxp_h, dh1, w_in_al, attn_pre_norm, tm)
    dw_in_m = _mm_tn("in_bwd_w_mla", u, dxp_m, MLA_IN, tm)
    dw_in_h = [_mm_tn("in_bwd_w_hgrn%d" % j, u, dxp_h[j], G_W, tm) for j in range(4)]
    dw_in = jnp.concatenate([dw_in_m[:, :kr_end]] + dw_in_h, 1).reshape(D, N_DEV, IN_SH).transpose(1, 0, 2).astype(BF)
    d_on = _unpad_heads(d_on_p, HB, V_DIM)

    ukv2 = lambda a: a.reshape(KV_RANK, N_HEADS * HB)
    small_g = [d_pre, d_gq, d_gkv, dw_ukv_all, d_on, d_lbl, d_ghn, d_post, d_fpre, d_fpost]
    small_w = [attn_pre_norm, mla_q_norm, mla_kv_norm, ukv2(mla_w_ukv), mla_out_norm, hgrn_lb_logits, hgrn_out_norm,
               attn_post_norm, ffn_pre_norm, ffn_post_norm]
    small_m = [m_attn_pre_norm, m_mla_q_norm, m_mla_kv_norm, ukv2(m_mla_w_ukv), m_mla_out_norm, m_hgrn_lb_logits,
               m_hgrn_out_norm, m_attn_post_norm, m_ffn_pre_norm, m_ffn_post_norm]
    small_v = [v_attn_pre_norm, v_mla_q_norm, v_mla_kv_norm, ukv2(v_mla_w_ukv), v_mla_out_norm, v_hgrn_lb_logits,
               v_hgrn_out_norm, v_attn_post_norm, v_ffn_pre_norm, v_ffn_post_norm]
    s_g, s_d, s_m, s_v, p_in = _final_exchange_adam(small_g, 3, dw_in, small_w, small_m, small_v)
    r_in = _shard_adam("adam_w_in", p_in, w_in[0], m_w_in[0], v_w_in[0], 256)
    r_uq = _shard_adam("adam_w_uq", p_uq, mla_w_uq[0].reshape(uq_sh), m_mla_w_uq[0].reshape(uq_sh),
                       v_mla_w_uq[0].reshape(uq_sh), uq_sh[0])
    r_out = _shard_adam("adam_w_out", p_out, w_out[0], m_w_out[0], v_w_out[0], D // N_DEV)
    r_g = _shard_adam("adam_w_gate", p_g, w_gate[0], m_w_gate[0], v_w_gate[0], 256)
    r_u = _shard_adam("adam_w_up", p_u, w_up[0], m_w_up[0], v_w_up[0], 256)
    r_d = _shard_adam("adam_w_down", p_d, w_down[0], m_w_down[0], v_w_down[0], FF_SH // 2)

    loss = lax.psum(loss_row[0, 0], ("x", "y", "c"))

    def assemble(big, small):
        b_in, b_uq, b_out, b_g, b_u, b_d = big
        return [small[0], b_in[None], small[1], b_uq.reshape(mla_w_uq.shape), small[2],
                small[3].reshape(mla_w_ukv.shape), small[4], small[5], small[6], b_out[None], small[7], small[8],
                b_g[None], b_u[None], b_d[None], small[9]]

    outs = [loss, grad_x[None]]
    for idx, small in enumerate((s_g, s_d, s_m, s_v)):
        outs += assemble([r[idx] for r in (r_in, r_uq, r_out, r_g, r_u, r_d)], small)
    return tuple(outs)
```

```python
import jax
import jax.numpy as jnp
from jax import lax
from jax.experimental import pallas as pl
from jax.experimental.pallas import tpu as pltpu

BF = jnp.bfloat16
F32 = jnp.float32
MESH = pl.DeviceIdType.MESH

N_DEV = 8
D = 1024
EPS = 1e-6
ROPE_THETA = 10000.0
N_HEADS = 8
HB = 128
NOPE = 64
ROPE = 32
V_DIM = 64
QK_DIM = NOPE + ROPE
Q_RANK = 384
KV_RANK = 128
KR_PAD = 128
MLA_IN = Q_RANK + KV_RANK + KR_PAD
G_HEADS = 4
G_DIM = 128
G_W = G_HEADS * G_DIM
CHUNK = 64
SUB = 16
XP_W = MLA_IN + 4 * G_W
IN_SH = 324
IN_W = N_DEV * IN_SH
FF_SH = 352
FF_PAD = 384
MIX_W = N_HEADS * HB + G_W

ADAM_LR = 0.001
ADAM_B1 = 0.9
ADAM_B2 = 0.999
ADAM_EPS = 1e-08
ADAM_WD = 0.01
ADAM_STEP = 10

_TM = 512
_TQ = 512
_FB = 768
_TB = 1024
_HP = 4
_VMEM_LIMIT = 56 * 1024 * 1024
NEG = -1e30


def _dot(a, b):
    return jnp.dot(a.astype(BF), b.astype(BF), preferred_element_type=F32)


def _dot_nt(a, b):
    return lax.dot_general(a.astype(BF), b.astype(BF), (((1,), (1,)), ((), ())), preferred_element_type=F32)


def _dot_tn(a, b):
    return lax.dot_general(a.astype(BF), b.astype(BF), (((0,), (0,)), ((), ())), preferred_element_type=F32)


def _sigmoid(x):
    return 1.0 / (1.0 + jnp.exp(-x))


def _rms(x, n):
    r = lax.rsqrt(jnp.sum(x * x, -1, keepdims=True) * (1.0 / n) + EPS)
    return x * r, r


def _rms_bwd(nx, r, g, dy, n):
    dg = jnp.sum(dy * nx, 0, keepdims=True)
    dn = dy * g
    dx = r * (dn - nx * (jnp.sum(dn * nx, -1, keepdims=True) * (1.0 / n)))
    return dx, dg


def _adamw(w, g, m, v):
    m2 = ADAM_B1 * m + (1.0 - ADAM_B1) * g
    v2 = ADAM_B2 * v + (1.0 - ADAM_B2) * (g * g)
    m_hat = m2 / (1.0 - ADAM_B1 ** ADAM_STEP)
    v_hat = v2 / (1.0 - ADAM_B2 ** ADAM_STEP)
    delta = -ADAM_LR * (m_hat / (jnp.sqrt(v_hat) + ADAM_EPS) + ADAM_WD * w)
    return delta, m2, v2


def _pcall(body, name, grid, in_specs, out_specs, out_shape, scratch=(), exchange=None):
    scratch = list(scratch)
    extra = ()
    if exchange is not None:
        kinds, extra = exchange
        in_specs, out_specs, out_shape = list(in_specs), list(out_specs), list(out_shape)
        n_in, n_out, n_scr, n_x = len(in_specs), len(out_specs), len(scratch), len(extra)
        inner = body

        def body(*refs):
            ins, rest = refs[:n_in], refs[n_in:]
            x_src, rest = rest[:n_x], rest[n_x:]
            outs, rest = rest[:n_out], rest[n_out:]
            x_dst, rest = rest[:n_x], rest[n_x:]
            ex = _Exchange(kinds, x_src, x_dst, *rest[n_scr:])
            first = pl.program_id(0) == 0
            last = pl.program_id(0) == grid[0] - 1
            for a in range(1, len(grid)):
                first = first & (pl.program_id(a) == 0)
                last = last & (pl.program_id(a) == grid[a] - 1)
            pl.when(first)(ex.start)
            inner(*ins, *outs, *rest[:n_scr])
            pl.when(last)(ex.wait)

        in_specs += [_HBM] * n_x
        out_specs += [_HBM] * n_x
        out_shape += _exchange_shapes(kinds, extra)
        scratch += _exchange_sems(n_x)
    call = pl.pallas_call(
        body, name=name, grid=grid, in_specs=in_specs, out_specs=out_specs, out_shape=out_shape,
        scratch_shapes=scratch,
        compiler_params=pltpu.CompilerParams(
            dimension_semantics=("arbitrary",) * len(grid), vmem_limit_bytes=_VMEM_LIMIT))
    return lambda *operands: call(*operands, *extra)


def _full(shape):
    return pl.BlockSpec(shape, lambda *_: (0,) * len(shape))


def _rows(tm, n):
    return pl.BlockSpec((tm, n), lambda i, *_: (i, 0))


def _sds(shape, dtype=F32):
    return jax.ShapeDtypeStruct(shape, dtype)


def _peer(k, x, y, c):
    px = 1 - x if (k >> 2) & 1 else x
    py = 1 - y if (k >> 1) & 1 else y
    pc = 1 - c if k & 1 else c
    return px, py, pc


GATHER, SCATTER, GATHER_COLS, SCATTER_COLS = "gather", "scatter", "gather_cols", "scatter_cols"


class _Exchange:
    def __init__(self, kinds, srcs, dsts, send_sems, recv_sems, loc_sems):
        self.kinds, self.srcs, self.dsts = kinds, srcs, dsts
        self.send_sems, self.recv_sems, self.loc_sems = send_sems, recv_sems, loc_sems
        self.x, self.y, self.c = lax.axis_index("x"), lax.axis_index("y"), lax.axis_index("c")
        self.me = 4 * self.x + 2 * self.y + self.c

    @staticmethod
    def _cols(ref, slot):
        n = ref.shape[1] // N_DEV
        return ref.at[:, pl.ds(pl.multiple_of(slot * n, 128), n)]

    def _src(self, w, slot):
        kind = self.kinds[w]
        if kind in (GATHER, GATHER_COLS):
            return self.srcs[w]
        return self.srcs[w].at[slot] if kind == SCATTER else self._cols(self.srcs[w], slot)

    def _dst(self, w, slot):
        return self._cols(self.dsts[w], slot) if self.kinds[w] == GATHER_COLS else self.dsts[w].at[slot]

    def _copy(self, w, k, outgoing):
        px, py, pc = _peer(k, self.x, self.y, self.c)
        pid = 4 * px + 2 * py + pc
        return pltpu.make_async_remote_copy(
            src_ref=self._src(w, pid if outgoing else self.me),
            dst_ref=self._dst(w, self.me if outgoing else pid),
            send_sem=self.send_sems.at[w, k - 1], recv_sem=self.recv_sems.at[w, k - 1],
            device_id=(px, py, pc), device_id_type=MESH)

    def _local(self, w):
        return pltpu.make_async_copy(self._src(w, self.me), self._dst(w, self.me), self.loc_sems.at[w])

    def start(self):
        for w in range(len(self.srcs)):
            self._local(w).start()
            for k in range(1, N_DEV):
                self._copy(w, k, True).start()

    def wait(self):
        for w in range(len(self.srcs)):
            self._local(w).wait()
            for k in range(1, N_DEV):
                self._copy(w, k, False).wait_recv()
        for w in range(len(self.srcs)):
            for k in range(1, N_DEV):
                self._copy(w, k, True).wait_send()


def _exchange_sems(n_w):
    return [pltpu.SemaphoreType.DMA((n_w, N_DEV - 1)), pltpu.SemaphoreType.DMA((n_w, N_DEV - 1)),
            pltpu.SemaphoreType.DMA((n_w,))]


def _exchange_shapes(kinds, srcs):
    def shape(kind, s):
        if kind == GATHER:
            return (N_DEV,) + tuple(s.shape)
        if kind == GATHER_COLS:
            return (s.shape[0], N_DEV * s.shape[1])
        if kind == SCATTER_COLS:
            return (N_DEV, s.shape[0], s.shape[1] // N_DEV)
        return tuple(s.shape)
    return [_sds(shape(kd, s), s.dtype) for kd, s in zip(kinds, srcs)]


_HBM = pl.BlockSpec(memory_space=pl.ANY)


def _cast_shards(w_in, w_uq, w_out, w_gate, w_up, w_down):
    shapes = [(D, IN_SH), (Q_RANK // N_DEV, N_HEADS * QK_DIM), (D // N_DEV, D), (D, FF_PAD), (D, FF_PAD), (FF_PAD, D)]

    def body(win, wuq, wout, wg, wu, wd, sin_, suq, sout, sg, su, sd):
        sin_[...] = win[...].astype(BF)
        suq[...] = wuq[...].astype(BF)
        sout[...] = wout[...].astype(BF)
        sg[...] = jnp.zeros(sg.shape, BF)
        sg[:, 0:FF_SH] = wg[...].astype(BF)
        su[...] = jnp.zeros(su.shape, BF)
        su[:, 0:FF_SH] = wu[...].astype(BF)
        sd[...] = jnp.zeros(sd.shape, BF)
        sd[0:FF_SH, :] = wd[...].astype(BF)

    vm = pl.BlockSpec(memory_space=pltpu.VMEM)
    return pl.pallas_call(
        body, name="cast_shards", in_specs=[vm] * 6, out_specs=[vm] * 6,
        out_shape=[_sds(s, BF) for s in shapes],
        compiler_params=pltpu.CompilerParams(vmem_limit_bytes=_VMEM_LIMIT),
    )(w_in, w_uq, w_out, w_gate, w_up, w_down)


def _exchange_call(name, kinds, srcs):
    n_w = len(srcs)

    def body(*refs):
        ex = _Exchange(kinds, refs[:n_w], refs[n_w:2 * n_w], *refs[2 * n_w:])
        ex.start()
        ex.wait()

    return pl.pallas_call(
        body, name=name, in_specs=[_HBM] * n_w, out_specs=[_HBM] * n_w,
        out_shape=_exchange_shapes(kinds, srcs), scratch_shapes=_exchange_sems(n_w))(*srcs)


def _final_exchange_adam(grads, big, dw_in, ws, ms, vs):
    n_p = len(grads)
    offs, rows = [], 0
    for p, g in enumerate(grads):
        offs.append(rows)
        rows += 0 if p == big else g.shape[0]
    rows = -(-rows // 8) * 8

    def body(*refs):
        g_refs, w_refs = refs[:n_p], refs[n_p:2 * n_p]
        m_refs, v_refs = refs[2 * n_p:3 * n_p], refs[3 * n_p:4 * n_p]
        din = refs[4 * n_p]
        outs = refs[4 * n_p + 1:8 * n_p + 1]
        pin = refs[8 * n_p + 1]
        pk, rall, send_sems, recv_sems, loc_sem, xs_send, xs_recv, xs_loc = refs[8 * n_p + 2:]
        ex = _Exchange([SCATTER], [din], [pin], xs_send, xs_recv, xs_loc)
        ex.start()
        x, y, c, me = ex.x, ex.y, ex.c, ex.me
        pk[...] = jnp.zeros(pk.shape, F32)
        for p in range(n_p):
            if p != big:
                r, n = g_refs[p].shape
                pk[offs[p]:offs[p] + r, 0:n] = g_refs[p][...]

        def remote(k):
            return pltpu.make_async_remote_copy(
                src_ref=pk, dst_ref=rall.at[me], send_sem=send_sems.at[k - 1], recv_sem=recv_sems.at[k - 1],
                device_id=_peer(k, x, y, c), device_id_type=MESH)

        def arrival(k):
            px, py, pc = _peer(k, x, y, c)
            return pltpu.make_async_remote_copy(
                src_ref=pk, dst_ref=rall.at[4 * px + 2 * py + pc], send_sem=send_sems.at[k - 1],
                recv_sem=recv_sems.at[k - 1], device_id=(px, py, pc), device_id_type=MESH)

        local = pltpu.make_async_copy(pk, rall.at[me], loc_sem)
        local.start()
        for k in range(1, N_DEV):
            remote(k).start()
        local.wait()
        for k in range(1, N_DEV):
            arrival(k).wait_recv()
        for k in range(1, N_DEV):
            remote(k).wait_send()
        for p in range(n_p):
            if p == big:
                g = g_refs[p][0]
                for j in range(1, N_DEV):
                    g = g + g_refs[p][j]
            else:
                r, n = g_refs[p].shape
                sl = (slice(offs[p], offs[p] + r), slice(0, n))
                g = rall[(0,) + sl]
                for j in range(1, N_DEV):
                    g = g + rall[(j,) + sl]
            delta, m2, v2 = _adamw(w_refs[p][...], g, m_refs[p][...], v_refs[p][...])
            outs[p][...] = g
            outs[n_p + p][...] = delta
            outs[2 * n_p + p][...] = m2
            outs[3 * n_p + p][...] = v2
        ex.wait()

    vm = pl.BlockSpec(memory_space=pltpu.VMEM)
    res = pl.pallas_call(
        body, name="final_exchange_adam",
        in_specs=[vm] * (4 * n_p) + [_HBM], out_specs=[vm] * (4 * n_p) + [_HBM],
        out_shape=[_sds(w.shape) for w in ws] * 4 + [_sds(dw_in.shape, dw_in.dtype)],
        scratch_shapes=[pltpu.VMEM((rows, D), F32), pltpu.VMEM((N_DEV, rows, D), F32),
                        pltpu.SemaphoreType.DMA((N_DEV - 1,)), pltpu.SemaphoreType.DMA((N_DEV - 1,)),
                        pltpu.SemaphoreType.DMA] + _exchange_sems(1),
        compiler_params=pltpu.CompilerParams(vmem_limit_bytes=_VMEM_LIMIT),
    )(*grads, *ws, *ms, *vs, dw_in)
    return res[:n_p], res[n_p:2 * n_p], res[2 * n_p:3 * n_p], res[3 * n_p:4 * n_p], res[4 * n_p]


def _shard_adam(name, parts, w, m, v, tr):
    a0, b0 = w.shape
    b = parts.shape[2]

    def body(p_ref, w_ref, m_ref, v_ref, g_out, d_out, m_out, v_out):
        g = p_ref[0].astype(F32)
        for j in range(1, N_DEV):
            g = g + p_ref[j].astype(F32)
        g = g[:, 0:b0]
        delta, m2, v2 = _adamw(w_ref[...], g, m_ref[...], v_ref[...])
        g_out[...] = g
        d_out[...] = delta
        m_out[...] = m2
        v_out[...] = v2

    blk = pl.BlockSpec((tr, b0), lambda i: (i, 0))
    return _pcall(
        body, name, (a0 // tr,),
        [pl.BlockSpec((N_DEV, tr, b), lambda i: (0, i, 0)), blk, blk, blk],
        [blk] * 4, [_sds((a0, b0))] * 4)(parts, w, m, v)


def _fwd_in(x, g_pre, w_in_al, tm):
    T = x.shape[0]

    def body(x_ref, g_ref, w_ref, xm_ref, xh_ref, u_ref):
        nx, _ = _rms(x_ref[...], D)
        u = (nx * g_ref[...]).astype(BF)
        u_ref[...] = u
        xm_ref[...] = jnp.dot(u, w_ref[:, 0:MLA_IN], preferred_element_type=F32)
        xh_ref[...] = jnp.dot(u, w_ref[:, MLA_IN:XP_W], preferred_element_type=F32)

    return _pcall(body, "fwd_in", (T // tm,),
                  [_rows(tm, D), _full((1, D)), _full((D, XP_W))],
                  [_rows(tm, MLA_IN), _rows(tm, 4 * G_W), _rows(tm, D)],
                  [_sds((T, MLA_IN)), _sds((T, 4 * G_W)), _sds((T, D), BF)])(x, g_pre, w_in_al)


def _rope(blk, ta, tb1, tb2):
    return blk * ta + pltpu.roll(blk, HB - ROPE // 2, 1) * tb1 + pltpu.roll(blk, ROPE // 2, 1) * tb2


def _unrope(d, ta, tb1, tb2):
    return d * ta + pltpu.roll(d * tb1, ROPE // 2, 1) + pltpu.roll(d * tb2, HB - ROPE // 2, 1)


def _mla_prep(xp, tabs, g_q, g_kv, w_uq, w_uk, w_uv, tm):
    T = xp.shape[0]
    W = N_HEADS * HB

    def body(xp_ref, ta_ref, tb1_ref, tb2_ref, gq_ref, gkv_ref, wuq_ref, wuk_ref, wuv_ref, q_ref, k_ref, v_ref):
        ta, tb1, tb2 = ta_ref[...], tb1_ref[...], tb2_ref[...]
        nq, _ = _rms(xp_ref[:, 0:Q_RANK], Q_RANK)
        nkv, _ = _rms(xp_ref[:, Q_RANK:Q_RANK + KV_RANK], KV_RANK)
        nkv = (nkv * gkv_ref[...]).astype(BF)
        qpre = _dot(nq * gq_ref[...], wuq_ref[...])
        kpre = jnp.dot(nkv, wuk_ref[...], preferred_element_type=F32)
        v_ref[...] = jnp.dot(nkv, wuv_ref[...], preferred_element_type=F32).astype(BF)
        kr = _rope(pltpu.roll(xp_ref[:, Q_RANK + KV_RANK:MLA_IN], NOPE, 1), ta, tb1, tb2)
        for h in range(N_HEADS):
            sl = slice(h * HB, (h + 1) * HB)
            q_ref[:, sl] = _rope(qpre[:, sl], ta, tb1, tb2).astype(BF)
            k_ref[:, sl] = (kpre[:, sl] + kr).astype(BF)

    tab = _rows(tm, HB)
    return _pcall(body, "mla_prep", (T // tm,),
                  [_rows(tm, MLA_IN), tab, tab, tab, _full((1, Q_RANK)), _full((1, KV_RANK)),
                   _full((Q_RANK, W)), _full((KV_RANK, W)), _full((KV_RANK, W))],
                  [_rows(tm, W)] * 3, [_sds((T, W), BF)] * 3)(xp, *tabs, g_q, g_kv, w_uq, w_uk, w_uv)


def _flash_fwd(q, k, v, tq, exchange=None):
    T = q.shape[0]
    scale = QK_DIM ** -0.5

    def body(q_ref, k_ref, v_ref, o_ref, lse_ref):
        i = pl.program_id(1)
        qv = q_ref[...]

        def blk(j, carry, masked):
            m, l, acc = carry
            st = pl.multiple_of(j * tq, tq)
            kv = k_ref[pl.ds(st, tq), :]
            vv = v_ref[pl.ds(st, tq), :]
            s = _dot_nt(qv, kv) * scale
            if masked:
                r = lax.broadcasted_iota(jnp.int32, (tq, tq), 0)
                c = lax.broadcasted_iota(jnp.int32, (tq, tq), 1)
                s = jnp.where(c <= r, s, NEG)
            m2 = jnp.maximum(m, jnp.max(s, -1, keepdims=True))
            p = jnp.exp(s - m2)
            a = jnp.exp(m - m2)
            return m2, a * l + jnp.sum(p, -1, keepdims=True), a * acc + _dot(p, vv)

        init = (jnp.full((tq, 1), NEG, F32), jnp.zeros((tq, 1), F32), jnp.zeros((tq, HB), F32))
        carry = lax.fori_loop(0, i, lambda j, cr: blk(j, cr, False), init)
        m, l, acc = blk(i, carry, True)
        o_ref[...] = acc / l
        lse_ref[...] = jnp.broadcast_to(m + jnp.log(l), (tq, HB))

    qs = pl.BlockSpec((tq, HB), lambda h, i: (i, h))
    kvs = pl.BlockSpec((T, HB), lambda h, i: (0, h))
    return _pcall(body, "flash_fwd", (N_HEADS, T // tq), [qs, kvs, kvs], [qs, qs],
                  [_sds((T, N_HEADS * HB))] * 2, exchange=exchange)(q, k, v)


def _gates(hq, hf, lb):
    sig = _sigmoid(hf)
    f = lb + (1.0 - lb) * sig
    sq = _sigmoid(hq)
    return hq * sq, 1.0 - f, f, jnp.log(f), sig, sq


def _lower_bound(lbl_ref):
    l0, l1 = lbl_ref[0:1, :], lbl_ref[1:2, :]
    mx = jnp.maximum(l0, l1)
    e0, e1 = jnp.exp(l0 - mx), jnp.exp(l1 - mx)
    return e0 / (e0 + e1)


def _split3(x):
    hi = x.astype(BF)
    r1 = x - hi.astype(F32)
    mid = r1.astype(BF)
    lo = (r1 - mid.astype(F32)).astype(BF)
    return hi, mid, lo


def _tri_mm(tri, x):
    hi, mid, lo = _split3(x)
    mm = lambda t: jnp.dot(tri, t, preferred_element_type=F32)
    return mm(hi) + mm(mid) + mm(lo)


def _anchor_mask(i, row, col):
    return ((row >> 4) == i) & (col < SUB * i)


def _diag_mask(d, row, col):
    return (col == row - d) & ((row & (SUB - 1)) >= d)


def _intra(q, k, b, b_s, row, col):
    a = jnp.zeros((CHUNK, CHUNK), F32)
    for i in range(1, CHUNK // SUB):
        b0 = b_s[SUB * i - 1:SUB * i, :]
        qs = q * jnp.exp(jnp.minimum(b - b0, 0.0))
        ks = k * jnp.exp(jnp.minimum(b0 - b, 0.0))
        a = a + jnp.where(_anchor_mask(i, row, col), _dot_nt(qs, ks), 0.0)
    for d in range(SUB):
        ksh = pltpu.roll(k, d, 0) if d else k
        bsh = pltpu.roll(b, d, 0) if d else b
        e = jnp.exp(jnp.minimum(b - bsh, 0.0))
        val = jnp.sum(q * ksh * e, -1, keepdims=True)
        a = a + jnp.where(_diag_mask(d, row, col), val, 0.0)
    return a


def _intra_bwd(q, k, b, b_s, da, row, col):
    dq = jnp.zeros((CHUNK, G_DIM), F32)
    dk = jnp.zeros((CHUNK, G_DIM), F32)
    for i in range(1, CHUNK // SUB):
        b0 = b_s[SUB * i - 1:SUB * i, :]
        eq = jnp.exp(jnp.minimum(b - b0, 0.0))
        ek = jnp.exp(jnp.minimum(b0 - b, 0.0))
        dai = jnp.where(_anchor_mask(i, row, col), da, 0.0)
        dq = dq + _dot(dai, k * ek) * eq
        dk = dk + _dot_tn(dai, q * eq) * ek
    for d in range(SUB):
        ksh = pltpu.roll(k, d, 0) if d else k
        bsh = pltpu.roll(b, d, 0) if d else b
        e = jnp.exp(jnp.minimum(b - bsh, 0.0))
        g = jnp.sum(jnp.where(_diag_mask(d, row, col), da, 0.0), -1, keepdims=True) * e
        dq = dq + g * ksh
        cb = g * q
        dk = dk + (pltpu.roll(cb, CHUNK - d, 0) if d else cb)
    return dq, dk


def _hgrn_fwd(xp, lb_logits, g_hn, exchange=None):
    T = xp.shape[0]
    tb = min(_TB, T)
    ncb = tb // CHUNK
    hp = _HP
    W = hp * G_DIM

    def body(hq_ref, hf_ref, hi_ref, hg_ref, lbl_ref, ghn_ref, out_ref, oraw_ref, sall_ref, st_ref, b_s):
        lb_all = _lower_bound(lbl_ref)

        @pl.when(pl.program_id(1) == 0)
        def _():
            st_ref[...] = jnp.zeros(st_ref.shape, F32)

        row = lax.broadcasted_iota(jnp.int32, (CHUNK, CHUNK), 0)
        col = lax.broadcasted_iota(jnp.int32, (CHUNK, CHUNK), 1)
        tri = (col <= row).astype(BF)

        def chunk(c, carry):
            sl = pl.ds(pl.multiple_of(c * CHUNK, CHUNK), CHUNK)
            for h in range(hp):
                ls = slice(h * G_DIM, (h + 1) * G_DIM)
                q, k, _, lf, _, _ = _gates(hq_ref[sl, ls], hf_ref[sl, ls], lb_all[:, ls])
                v = hi_ref[sl, ls]
                b = _tri_mm(tri, lf)
                b_s[h] = b
                st = st_ref[h]
                sall_ref[c, h] = st
                o = _dot_nt(q * jnp.exp(b), st) + _dot(_intra(q, k, b, b_s.at[h], row, col), v)
                bl = b_s[h, CHUNK - 1:CHUNK, :]
                st_ref[h] = st * jnp.exp(bl) + _dot_tn(v, k * jnp.exp(bl - b))
                oraw_ref[sl, ls] = o
                n, _ = _rms(o, G_DIM)
                hg = hg_ref[sl, ls]
                out_ref[sl, ls] = n * ghn_ref[:, ls] * (hg * _sigmoid(hg))
            return carry

        lax.fori_loop(0, ncb, chunk, 0)

    col_blk = lambda j: pl.BlockSpec((tb, W), lambda p, t: (t, j * (G_HEADS // hp) + p))
    head = pl.BlockSpec((tb, W), lambda p, t: (t, p))
    return _pcall(
        body, "hgrn_fwd", (G_HEADS // hp, T // tb),
        [col_blk(0), col_blk(1), col_blk(2), col_blk(3),
         pl.BlockSpec((2, W), lambda p, t: (0, p)), pl.BlockSpec((1, W), lambda p, t: (0, p))],
        [head, head, pl.BlockSpec((ncb, hp, G_DIM, G_DIM), lambda p, t: (t, p, 0, 0))],
        [_sds((T, G_W)), _sds((T, G_W)), _sds((T // CHUNK, G_HEADS, G_DIM, G_DIM))],
        scratch=[pltpu.VMEM((hp, G_DIM, G_DIM), F32), pltpu.VMEM((hp, CHUNK, G_DIM), F32)], exchange=exchange,
    )(xp, xp, xp, xp, lb_logits, g_hn)


def _fwd_out(o_pad, o_hgrn, x, g_on, w_out, g_post, g_fpre, tm):
    T = x.shape[0]

    def body(o_ref, oh_ref, x_ref, gon_ref, w_ref, gpost_ref, gfpre_ref, h1_ref, y1_ref, z_ref, mix_ref):
        for h in range(N_HEADS):
            sl = slice(h * HB, (h + 1) * HB)
            n, _ = _rms(o_ref[:, sl], V_DIM)
            mix_ref[:, sl] = (n * gon_ref[:, sl]).astype(BF)
        mix_ref[:, N_HEADS * HB:MIX_W] = oh_ref[...].astype(BF)
        y1 = jnp.dot(mix_ref[...], w_ref[...], preferred_element_type=F32)
        y1_ref[...] = y1
        ny, _ = _rms(y1, D)
        h1 = x_ref[...] + ny * gpost_ref[...]
        h1_ref[...] = h1
        nh, _ = _rms(h1, D)
        z_ref[...] = (nh * gfpre_ref[...]).astype(BF)

    return _pcall(body, "fwd_out", (T // tm,),
                  [_rows(tm, N_HEADS * HB), _rows(tm, G_W), _rows(tm, D), _full((1, N_HEADS * HB)),
                   _full((MIX_W, D)), _full((1, D)), _full((1, D))],
                  [_rows(tm, D), _rows(tm, D), _rows(tm, D), _rows(tm, MIX_W)],
                  [_sds((T, D)), _sds((T, D)), _sds((T, D), BF), _sds((T, MIX_W), BF)],
                  )(o_pad, o_hgrn, x, g_on, w_out, g_post, g_fpre)


def _ffn_fwd(z, wg, wu, wd, h1, tgt, g_fpost, tm):
    T = z.shape[0]
    nf = wg.shape[1] // _FB

    def body(z_ref, wg_ref, wu_ref, wd_ref, h1_ref, t_ref, gp_ref,
             gs_ref, us_ref, dh2_ref, dy2_ref, dgp_ref, loss_ref, acc):
        i, j = pl.program_id(0), pl.program_id(1)
        zv = z_ref[...]
        g = jnp.dot(zv, wg_ref[...], preferred_element_type=F32)
        u = jnp.dot(zv, wu_ref[...], preferred_element_type=F32)
        gs_ref[...] = g.astype(BF)
        us_ref[...] = u.astype(BF)
        part = _dot(g * _sigmoid(g) * u, wd_ref[...])

        @pl.when(j == 0)
        def _():
            acc[...] = part

        @pl.when(j > 0)
        def _():
            acc[...] += part

        @pl.when((i == 0) & (j == 0))
        def _():
            dgp_ref[...] = jnp.zeros(dgp_ref.shape, F32)
            loss_ref[...] = jnp.zeros(loss_ref.shape, F32)

        @pl.when(j == nf - 1)
        def _():
            ny, r = _rms(acc[...], D)
            err = h1_ref[...] + ny * gp_ref[...] - t_ref[...]
            loss_ref[...] += 0.5 * jnp.sum(jnp.sum(err * err, -1, keepdims=True) * (1.0 / D), 0, keepdims=True)
            dh2 = err * (1.0 / D)
            dh2_ref[...] = dh2
            dy2, dgp = _rms_bwd(ny, r, gp_ref[...], dh2, D)
            dy2_ref[...] = dy2.astype(BF)
            dgp_ref[...] += dgp

    tok = lambda n: pl.BlockSpec((tm, n), lambda i, j: (i, 0))
    col = pl.BlockSpec((tm, _FB), lambda i, j: (i, j))
    return _pcall(
        body, "ffn_fwd", (T // tm, nf),
        [tok(D), pl.BlockSpec((D, _FB), lambda i, j: (0, j)), pl.BlockSpec((D, _FB), lambda i, j: (0, j)),
         pl.BlockSpec((_FB, D), lambda i, j: (j, 0)), tok(D), tok(D), _full((1, D))],
        [col, col, tok(D), tok(D), _full((1, D)), _full((1, HB))],
        [_sds((T, nf * _FB), BF), _sds((T, nf * _FB), BF), _sds((T, D)), _sds((T, D), BF),
         _sds((1, D)), _sds((1, HB))],
        scratch=[pltpu.VMEM((tm, D), F32)],
    )(z, wg, wu, wd, h1, tgt, g_fpost)


def _dsilu(x, s):
    return s * (1.0 + x * (1.0 - s))


def _ffn_bwd_x(dy2, gs, us, wg, wu, wd, h1, y1, dh2, g_fpre, g_post, tm):
    T = dy2.shape[0]
    nf = wg.shape[1] // _FB

    def body(dy2_ref, gs_ref, us_ref, wg_ref, wu_ref, wd_ref, h1_ref, y1_ref, dh2_ref, gf_ref, gp_ref,
             dgs_ref, dus_ref, dh1_ref, dy1_ref, dgf_ref, dgp_ref, acc):
        i, j = pl.program_id(0), pl.program_id(1)
        dff = _dot_nt(dy2_ref[...], wd_ref[...])
        g = gs_ref[...].astype(F32)
        u = us_ref[...].astype(F32)
        s = _sigmoid(g)
        dg = (dff * u * _dsilu(g, s)).astype(BF)
        du = (dff * g * s).astype(BF)
        dgs_ref[...] = dg
        dus_ref[...] = du
        part = _dot_nt(dg, wg_ref[...]) + _dot_nt(du, wu_ref[...])

        @pl.when(j == 0)
        def _():
            acc[...] = part

        @pl.when(j > 0)
        def _():
            acc[...] += part

        @pl.when((i == 0) & (j == 0))
        def _():
            dgf_ref[...] = jnp.zeros(dgf_ref.shape, F32)
            dgp_ref[...] = jnp.zeros(dgp_ref.shape, F32)

        @pl.when(j == nf - 1)
        def _():
            nh, rh = _rms(h1_ref[...], D)
            dh, dgf = _rms_bwd(nh, rh, gf_ref[...], acc[...], D)
            dh1 = dh2_ref[...] + dh
            dh1_ref[...] = dh1
            dgf_ref[...] += dgf
            ny, ry = _rms(y1_ref[...], D)
            dy1, dgp = _rms_bwd(ny, ry, gp_ref[...], dh1, D)
            dy1_ref[...] = dy1.astype(BF)
            dgp_ref[...] += dgp

    tok = lambda n: pl.BlockSpec((tm, n), lambda i, j: (i, 0))
    col = pl.BlockSpec((tm, _FB), lambda i, j: (i, j))
    wcol = pl.BlockSpec((D, _FB), lambda i, j: (0, j))
    return _pcall(
        body, "ffn_bwd_x", (T // tm, nf),
        [tok(D), col, col, wcol, wcol, pl.BlockSpec((_FB, D), lambda i, j: (j, 0)),
         tok(D), tok(D), tok(D), _full((1, D)), _full((1, D))],
        [col, col, tok(D), tok(D), _full((1, D)), _full((1, D))],
        [_sds((T, nf * _FB), BF), _sds((T, nf * _FB), BF), _sds((T, D)), _sds((T, D), BF),
         _sds((1, D)), _sds((1, D))],
        scratch=[pltpu.VMEM((tm, D), F32)],
    )(dy2, gs, us, wg, wu, wd, h1, y1, dh2, g_fpre, g_post)


def _ffn_bwd_w(z, gs, us, dgs, dus, dy2, tm):
    T = z.shape[0]
    nf = gs.shape[1] // _FB
    nt = T // tm

    def body(z_ref, gs_ref, us_ref, dgs_ref, dus_ref, dy2_ref, dwg_ref, dwu_ref, dwd_ref, ag, au, ad):
        i = pl.program_id(1)
        g = gs_ref[...].astype(F32)
        ff = g * _sigmoid(g) * us_ref[...].astype(F32)
        pg = _dot_tn(z_ref[...], dgs_ref[...])
        pu = _dot_tn(z_ref[...], dus_ref[...])
        pd = _dot_tn(ff, dy2_ref[...])

        @pl.when(i == 0)
        def _():
            ag[...] = pg
            au[...] = pu
            ad[...] = pd

        @pl.when(i > 0)
        def _():
            ag[...] += pg
            au[...] += pu
            ad[...] += pd

        @pl.when(i == nt - 1)
        def _():
            dwg_ref[...] = ag[...].astype(BF)
            dwu_ref[...] = au[...].astype(BF)
            dwd_ref[...] = ad[...].astype(BF)

    F = nf * _FB
    tok = lambda n: pl.BlockSpec((tm, n), lambda j, i: (i, 0))
    col = pl.BlockSpec((tm, _FB), lambda j, i: (i, j))
    wcol = pl.BlockSpec((D, _FB), lambda j, i: (0, j))
    wrow = pl.BlockSpec((_FB, D), lambda j, i: (j, 0))
    return _pcall(
        body, "ffn_bwd_w", (nf, nt), [tok(D), col, col, col, col, tok(D)], [wcol, wcol, wrow],
        [_sds((D, F), BF), _sds((D, F), BF), _sds((F, D), BF)],
        scratch=[pltpu.VMEM((D, _FB), F32), pltpu.VMEM((D, _FB), F32), pltpu.VMEM((_FB, D), F32)],
    )(z, gs, us, dgs, dus, dy2)


def _out_bwd(dy1, mix, o_pad, w_out, g_on, tm):
    T = dy1.shape[0]
    W = N_HEADS * HB

    def body(dy1_ref, mix_ref, o_ref, w_ref, gon_ref, do_ref, dl_ref, dohg_ref, dw_ref, dgon_ref):
        i = pl.program_id(0)
        dy1v = dy1_ref[...]
        dmix = _dot_nt(dy1v, w_ref[...])
        pw = _dot_tn(mix_ref[...], dy1v)

        @pl.when(i == 0)
        def _():
            dw_ref[...] = pw
            dgon_ref[...] = jnp.zeros(dgon_ref.shape, F32)

        @pl.when(i > 0)
        def _():
            dw_ref[...] += pw

        for h in range(N_HEADS):
            sl = slice(h * HB, (h + 1) * HB)
            ov = o_ref[:, sl]
            n, r = _rms(ov, V_DIM)
            do, dg = _rms_bwd(n, r, gon_ref[:, sl], dmix[:, sl], V_DIM)
            dgon_ref[:, sl] += dg
            do_ref[:, sl] = do.astype(BF)
            dl_ref[:, sl] = jnp.broadcast_to(jnp.sum(do * ov, -1, keepdims=True), (tm, HB))
        dohg_ref[...] = dmix[:, W:MIX_W]

    return _pcall(body, "out_bwd", (T // tm,),
                  [_rows(tm, D), _rows(tm, MIX_W), _rows(tm, W), _full((MIX_W, D)), _full((1, W))],
                  [_rows(tm, W), _rows(tm, W), _rows(tm, G_W), _full((MIX_W, D)), _full((1, W))],
                  [_sds((T, W), BF), _sds((T, W)), _sds((T, G_W)), _sds((MIX_W, D)), _sds((1, W))],
                  )(dy1, mix, o_pad, w_out, g_on)


def _flash_bwd(q, k, v, do, lse, dl, tq, exchange=None):
    T = q.shape[0]
    nq = T // tq
    scale = QK_DIM ** -0.5

    def body(k_ref, v_ref, q_ref, do_ref, lse_ref, dl_ref, dk_ref, dv_ref, dq_ref):
        j = pl.program_id(1)

        @pl.when(j == 0)
        def _():
            dq_ref[...] = jnp.zeros(dq_ref.shape, F32)

        kv, vv = k_ref[...], v_ref[...]

        def blk(i, carry, masked):
            dk, dv = carry
            sl = pl.ds(pl.multiple_of(i * tq, tq), tq)
            qv, dov = q_ref[sl, :], do_ref[sl, :]
            s = _dot_nt(qv, kv) * scale
            if masked:
                r = lax.broadcasted_iota(jnp.int32, (tq, tq), 0)
                c = lax.broadcasted_iota(jnp.int32, (tq, tq), 1)
                s = jnp.where(c <= r, s, NEG)
            p = jnp.exp(s - lse_ref[sl, 0:1])
            ds = p * (_dot_nt(dov, vv) - dl_ref[sl, 0:1]) * scale
            dq_ref[sl, :] += _dot(ds, kv)
            return dk + _dot_tn(ds, qv), dv + _dot_tn(p, dov)

        zero = jnp.zeros((tq, HB), F32)
        carry = blk(j, (zero, zero), True)
        dk, dv = lax.fori_loop(j + 1, nq, lambda i, cr: blk(i, cr, False), carry)
        dk_ref[...] = dk
        dv_ref[...] = dv

    tile = pl.BlockSpec((tq, HB), lambda h, j: (j, h))
    whole = pl.BlockSpec((T, HB), lambda h, j: (0, h))
    return _pcall(body, "flash_bwd", (N_HEADS, nq), [tile, tile, whole, whole, whole, whole],
                  [tile, tile, whole], [_sds((T, N_HEADS * HB))] * 3, exchange=exchange)(k, v, q, do, lse, dl)


def _mla_prep_bwd(xp, tabs, dq, dk, dv, g_q, g_kv, w_uq, w_uk, w_uv, tm):
    T = xp.shape[0]
    W = N_HEADS * HB

    def body(xp_ref, ta_ref, tb1_ref, tb2_ref, dq_ref, dk_ref, dv_ref, gq_ref, gkv_ref, wuq_ref, wuk_ref, wuv_ref,
             dxp_ref, dwuq_ref, dwuk_ref, dwuv_ref, dgq_ref, dgkv_ref, dqp):
        i = pl.program_id(0)
        ta, tb1, tb2 = ta_ref[...], tb1_ref[...], tb2_ref[...]
        nq, rq = _rms(xp_ref[:, 0:Q_RANK], Q_RANK)
        nkv, rkv = _rms(xp_ref[:, Q_RANK:Q_RANK + KV_RANK], KV_RANK)
        dkr = jnp.zeros((tm, HB), F32)
        for h in range(N_HEADS):
            sl = slice(h * HB, (h + 1) * HB)
            dqp[:, sl] = _unrope(dq_ref[:, sl], ta, tb1, tb2).astype(BF)
            dkr = dkr + dk_ref[:, sl]
        dkr = pltpu.roll(_unrope(dkr, ta, tb1, tb2), HB - NOPE, 1)
        lane = lax.broadcasted_iota(jnp.int32, (tm, HB), 1)
        dxp_ref[:, Q_RANK + KV_RANK:MLA_IN] = jnp.where(lane < ROPE, dkr, 0.0)
        dqpv = dqp[...]
        dkv, dvv = dk_ref[...].astype(BF), dv_ref[...].astype(BF)
        nqs = (nq * gq_ref[...]).astype(BF)
        nkvs = (nkv * gkv_ref[...]).astype(BF)
        pq, pk, pv = _dot_tn(nqs, dqpv), _dot_tn(nkvs, dkv), _dot_tn(nkvs, dvv)
        dcq, dgq = _rms_bwd(nq, rq, gq_ref[...], _dot_nt(dqpv, wuq_ref[...]), Q_RANK)
        dckv, dgkv = _rms_bwd(nkv, rkv, gkv_ref[...], _dot_nt(dkv, wuk_ref[...]) + _dot_nt(dvv, wuv_ref[...]), KV_RANK)
        dxp_ref[:, 0:Q_RANK] = dcq
        dxp_ref[:, Q_RANK:Q_RANK + KV_RANK] = dckv

        @pl.when(i == 0)
        def _():
            dwuq_ref[...] = pq
            dwuk_ref[...] = pk
            dwuv_ref[...] = pv
            dgq_ref[...] = dgq
            dgkv_ref[...] = dgkv

        @pl.when(i > 0)
        def _():
            dwuq_ref[...] += pq
            dwuk_ref[...] += pk
            dwuv_ref[...] += pv
            dgq_ref[...] += dgq
            dgkv_ref[...] += dgkv

    tab = _rows(tm, HB)
    return _pcall(
        body, "mla_prep_bwd", (T // tm,),
        [_rows(tm, MLA_IN), tab, tab, tab, _rows(tm, W), _rows(tm, W), _rows(tm, W), _full((1, Q_RANK)),
         _full((1, KV_RANK)), _full((Q_RANK, W)), _full((KV_RANK, W)), _full((KV_RANK, W))],
        [_rows(tm, MLA_IN), _full((Q_RANK, W)), _full((KV_RANK, W)), _full((KV_RANK, W)), _full((1, Q_RANK)),
         _full((1, KV_RANK))],
        [_sds((T, MLA_IN)), _sds((Q_RANK, W)), _sds((KV_RANK, W)), _sds((KV_RANK, W)), _sds((1, Q_RANK)),
         _sds((1, KV_RANK))],
        scratch=[pltpu.VMEM((tm, W), BF)],
    )(xp, *tabs, dq, dk, dv, g_q, g_kv, w_uq, w_uk, w_uv)


def _hgrn_bwd(xp, o_raw, s_all, d_out, lb_logits, g_hn, exchange=None):
    T = xp.shape[0]
    tb = min(_TB, T)
    ncb = tb // CHUNK
    nb = T // tb
    hp = _HP
    W = hp * G_DIM

    def body(hq_ref, hf_ref, hi_ref, hg_ref, o_ref, sall_ref, dout_ref, lbl_ref, ghn_ref,
             dhq_ref, dhf_ref, dhi_ref, dhg_ref, dlbl_ref, dghn_ref, dst_ref, b_s, acc_lb, acc_g):
        t = pl.program_id(1)
        lb_all = _lower_bound(lbl_ref)

        @pl.when(t == 0)
        def _():
            dst_ref[...] = jnp.zeros(dst_ref.shape, F32)
            acc_lb[...] = jnp.zeros(acc_lb.shape, F32)
            acc_g[...] = jnp.zeros(acc_g.shape, F32)

        row = lax.broadcasted_iota(jnp.int32, (CHUNK, CHUNK), 0)
        col = lax.broadcasted_iota(jnp.int32, (CHUNK, CHUNK), 1)
        tri = (col <= row).astype(BF)
        tri_t = (col >= row).astype(BF)
        last = lax.broadcasted_iota(jnp.int32, (CHUNK, G_DIM), 0) == CHUNK - 1

        def chunk(cc, carry):
            c = ncb - 1 - cc
            sl = pl.ds(pl.multiple_of(c * CHUNK, CHUNK), CHUNK)
            for h in range(hp):
                ls = slice(h * G_DIM, (h + 1) * G_DIM)
                lb, ghn = lb_all[:, ls], ghn_ref[:, ls]
                hq, hg = hq_ref[sl, ls], hg_ref[sl, ls]
                q, k, f, lf, sig, sq = _gates(hq, hf_ref[sl, ls], lb)
                v = hi_ref[sl, ls]
                b = _tri_mm(tri, lf)
                b_s[h] = b
                st = sall_ref[c, h]
                dstn = dst_ref[h]
                o = o_ref[sl, ls]
                dout = dout_ref[sl, ls]
                n, r = _rms(o, G_DIM)
                sg = _sigmoid(hg)
                dhg_ref[sl, ls] = dout * (n * ghn) * _dsilu(hg, sg)
                do, dg = _rms_bwd(n, r, ghn, dout * (hg * sg), G_DIM)
                acc_g[:, ls] += dg
                eb = jnp.exp(b)
                bl = b_s[h, CHUNK - 1:CHUNK, :]
                ebl = jnp.exp(bl)
                ekd = jnp.exp(bl - b)
                kd = k * ekd
                a = _intra(q, k, b, b_s.at[h], row, col)
                dq_i, dk_i = _intra_bwd(q, k, b, b_s.at[h], _dot_nt(do, v), row, col)
                dhi_ref[sl, ls] = _dot_tn(a, do) + _dot_nt(kd, dstn)
                dk_state = _dot(v, dstn) * ekd
                dq = dq_i + _dot(do, st) * eb
                dk = dk_i + dk_state
                dbl = jnp.sum(k * dk_state, 0, keepdims=True) + ebl * jnp.sum(dstn * st, 0, keepdims=True)
                db = q * dq - k * dk + jnp.where(last, dbl, 0.0)
                df = _tri_mm(tri_t, db) / f - dk
                dhf_ref[sl, ls] = df * (1.0 - lb) * sig * (1.0 - sig)
                acc_lb[:, ls] += jnp.sum(df * (1.0 - sig), 0, keepdims=True)
                dhq_ref[sl, ls] = dq * _dsilu(hq, sq)
                dst_ref[h] = dstn * ebl + _dot_tn(do, q * eb)
            return carry

        lax.fori_loop(0, ncb, chunk, 0)

        @pl.when(t == nb - 1)
        def _():
            dl0 = acc_lb[...] * lb_all * (1.0 - lb_all)
            dlbl_ref[0:1, :] = dl0
            dlbl_ref[1:2, :] = -dl0
            dghn_ref[...] = acc_g[...]

    col_blk = lambda j: pl.BlockSpec((tb, W), lambda p, t: (nb - 1 - t, j * (G_HEADS // hp) + p))
    head = pl.BlockSpec((tb, W), lambda p, t: (nb - 1 - t, p))
    two = pl.BlockSpec((2, W), lambda p, t: (0, p))
    one = pl.BlockSpec((1, W), lambda p, t: (0, p))
    res = _pcall(
        body, "hgrn_bwd", (G_HEADS // hp, nb),
        [col_blk(0), col_blk(1), col_blk(2), col_blk(3), head,
         pl.BlockSpec((ncb, hp, G_DIM, G_DIM), lambda p, t: (nb - 1 - t, p, 0, 0)), head, two, one],
        [head, head, head, head, two, one],
        [_sds((T, G_W))] * 4 + [_sds((2, G_W)), _sds((1, G_W))],
        scratch=[pltpu.VMEM((hp, G_DIM, G_DIM), F32), pltpu.VMEM((hp, CHUNK, G_DIM), F32),
                 pltpu.VMEM((1, W), F32), pltpu.VMEM((1, W), F32)], exchange=exchange,
    )(xp, xp, xp, xp, o_raw, s_all, d_out, lb_logits, g_hn)
    return res


def _in_bwd_x(x, dxp_m, dxp_h, dh1, w_in_al, g_pre, tm):
    T = x.shape[0]

    def body(x_ref, dm_ref, d0_ref, d1_ref, d2_ref, d3_ref, dh1_ref, w_ref, g_ref, dx_ref, dg_ref):
        i = pl.program_id(0)
        du = _dot_nt(dm_ref[...], w_ref[:, 0:MLA_IN])
        for j, d_ref in enumerate((d0_ref, d1_ref, d2_ref, d3_ref)):
            du = du + _dot_nt(d_ref[...], w_ref[:, MLA_IN + j * G_W:MLA_IN + (j + 1) * G_W])
        nx, r = _rms(x_ref[...], D)
        dx, dg = _rms_bwd(nx, r, g_ref[...], du, D)
        dx_ref[...] = dh1_ref[...] + dx

        @pl.when(i == 0)
        def _():
            dg_ref[...] = dg

        @pl.when(i > 0)
        def _():
            dg_ref[...] += dg

    return _pcall(body, "in_bwd_x", (T // tm,),
                  [_rows(tm, D), _rows(tm, MLA_IN)] + [_rows(tm, G_W)] * 4 + [_rows(tm, D), _full((D, XP_W)), _full((1, D))],
                  [_rows(tm, D), _full((1, D))], [_sds((T, D)), _sds((1, D))],
                  )(x, dxp_m, *dxp_h, dh1, w_in_al, g_pre)


def _mm_tn(name, a, b, tn, tt):
    T, M = a.shape
    N = b.shape[1]

    def body(a_ref, b_ref, o_ref):
        t = pl.program_id(1)
        part = _dot_tn(a_ref[...], b_ref[...])

        @pl.when(t == 0)
        def _():
            o_ref[...] = part

        @pl.when(t > 0)
        def _():
            o_ref[...] += part

    return _pcall(body, name, (N // tn, T // tt),
                  [pl.BlockSpec((tt, M), lambda n, t: (t, 0)), pl.BlockSpec((tt, tn), lambda n, t: (t, n))],
                  pl.BlockSpec((M, tn), lambda n, t: (0, n)), _sds((M, N)))(a, b)


def _pad_heads(w, width, real):
    lead = w.shape[:-1]
    w = w.reshape(lead + (N_HEADS, real))
    w = jnp.pad(w, [(0, 0)] * len(lead) + [(0, 0), (0, width - real)])
    return w.reshape(lead + (N_HEADS * width,))


def _unpad_heads(w, width, real):
    lead = w.shape[:-1]
    return w.reshape(lead + (N_HEADS, width))[..., :real].reshape(lead + (N_HEADS * real,))


def _rope_tables(positions):
    half = ROPE // 2
    inv_freq = 1.0 / (ROPE_THETA ** (jnp.arange(0, ROPE, 2, dtype=F32) / ROPE))
    ang = positions.astype(F32)[:, None] * inv_freq
    cos, sin = jnp.cos(ang), jnp.sin(ang)
    T = positions.shape[0]
    z = lambda n: jnp.zeros((T, n), F32)
    ta = jnp.concatenate([jnp.ones((T, NOPE), F32), cos, cos, z(HB - QK_DIM)], 1)
    tb1 = jnp.concatenate([z(NOPE), -sin, z(half), z(HB - QK_DIM)], 1)
    tb2 = jnp.concatenate([z(NOPE), z(half), sin, z(HB - QK_DIM)], 1)
    return ta, tb1, tb2


def kernel(x, positions, attn_pre_norm, w_in, mla_q_norm, mla_w_uq, mla_kv_norm, mla_w_ukv, mla_out_norm, hgrn_lb_logits, hgrn_out_norm, w_out, attn_post_norm, ffn_pre_norm, w_gate, w_up, w_down, ffn_post_norm, loss_target, m_attn_pre_norm, m_w_in, m_mla_q_norm, m_mla_w_uq, m_mla_kv_norm, m_mla_w_ukv, m_mla_out_norm, m_hgrn_lb_logits, m_hgrn_out_norm, m_w_out, m_attn_post_norm, m_ffn_pre_norm, m_w_gate, m_w_up, m_w_down, m_ffn_post_norm, v_attn_pre_norm, v_w_in, v_mla_q_norm, v_mla_w_uq, v_mla_kv_norm, v_mla_w_ukv, v_mla_out_norm, v_hgrn_lb_logits, v_hgrn_out_norm, v_w_out, v_attn_post_norm, v_ffn_pre_norm, v_w_gate, v_w_up, v_w_down, v_ffn_post_norm):
    T = x.shape[1]
    tm = min(_TM, T)
    tq = min(_TQ, T)
    xs, tgt = x[0], loss_target[0]
    uq_sh = (Q_RANK // N_DEV, N_HEADS * QK_DIM)

    b_in, b_uq, b_out, b_g, b_u, b_d = _cast_shards(
        w_in[0], mla_w_uq[0].reshape(uq_sh), w_out[0], w_gate[0], w_up[0], w_down[0])
    g_in, g_uq = _exchange_call("ag_first", [GATHER, GATHER], [b_in, b_uq])
    w_in_full = g_in.transpose(1, 0, 2).reshape(D, IN_W)
    kr_end = Q_RANK + KV_RANK + ROPE
    w_in_al = jnp.concatenate([w_in_full[:, :kr_end], jnp.zeros((D, KR_PAD - ROPE), BF), w_in_full[:, kr_end:]], 1)
    w_uq_p = _pad_heads(g_uq.reshape(Q_RANK, N_HEADS * QK_DIM), HB, QK_DIM)
    w_ukv = mla_w_ukv[0].astype(BF)
    w_uk_p = _pad_heads(w_ukv[..., :NOPE].reshape(KV_RANK, N_HEADS * NOPE), HB, NOPE)
    w_uv_p = _pad_heads(w_ukv[..., NOPE:].reshape(KV_RANK, N_HEADS * V_DIM), HB, V_DIM)
    g_on_p = _pad_heads(mla_out_norm, HB, V_DIM)
    tabs = _rope_tables(positions[0])

    xp_m, xp_h, u = _fwd_in(xs, attn_pre_norm, w_in_al, tm)
    q_att, k_att, v_att = _mla_prep(xp_m, tabs, mla_q_norm, mla_kv_norm, w_uq_p, w_uk_p, w_uv_p, tm)
    o_hgrn, o_raw, s_all, g_out, wd = _hgrn_fwd(xp_h, hgrn_lb_logits, hgrn_out_norm, ([GATHER, GATHER], [b_out, b_d]))
    wd = wd.reshape(N_DEV * FF_PAD, D)
    o_pad, lse, wg, wu = _flash_fwd(q_att, k_att, v_att, tq, ([GATHER_COLS, GATHER_COLS], [b_g, b_u]))
    w_out_full = g_out.reshape(D, D)
    w_out_mla = jnp.pad(w_out_full[:N_HEADS * V_DIM].reshape(N_HEADS, V_DIM, D), ((0, 0), (0, HB - V_DIM), (0, 0)))
    w_out_p = jnp.concatenate([w_out_mla.reshape(N_HEADS * HB, D), w_out_full[N_HEADS * V_DIM:]], 0)
    h1, y1, z, mix = _fwd_out(o_pad, o_hgrn, xs, g_on_p, w_out_p, attn_post_norm, ffn_pre_norm, tm)
    gs, us, dh2, dy2, d_fpost, loss_row = _ffn_fwd(z, wg, wu, wd, h1, tgt, ffn_post_norm, tm)

    dgs, dus, dh1, dy1, d_fpre, d_post = _ffn_bwd_x(dy2, gs, us, wg, wu, wd, h1, y1, dh2, ffn_pre_norm, attn_post_norm, tm)
    dwg, dwu, dwd = _ffn_bwd_w(z, gs, us, dgs, dus, dy2, tm)
    do_pad, dl, d_ohg, dw_out_p, d_on_p = _out_bwd(dy1, mix, o_pad, w_out_p, g_on_p, tm)
    dw_out_mla = dw_out_p[:N_HEADS * HB].reshape(N_HEADS, HB, D)[:, :V_DIM].reshape(N_HEADS * V_DIM, D)
    dw_out = jnp.concatenate([dw_out_mla, dw_out_p[N_HEADS * HB:]], 0).reshape(N_DEV, D // N_DEV, D).astype(BF)
    dk_att, dv_att, dq_att, p_g, p_u, p_d, p_out = _flash_bwd(
        q_att, k_att, v_att, do_pad, lse, dl, tq,
        ([SCATTER_COLS, SCATTER_COLS, SCATTER, SCATTER], [dwg, dwu, dwd.reshape(N_DEV, FF_PAD, D), dw_out]))
    dxp_m, dw_uq_p, dw_uk_p, dw_uv_p, d_gq, d_gkv = _mla_prep_bwd(
        xp_m, tabs, dq_att, dk_att, dv_att, mla_q_norm, mla_kv_norm, w_uq_p, w_uk_p, w_uv_p, tm)
    dw_uq = _unpad_heads(dw_uq_p, HB, QK_DIM).reshape((N_DEV,) + uq_sh).astype(BF)
    dw_ukv = jnp.concatenate([_unpad_heads(dw_uk_p, HB, NOPE).reshape(KV_RANK, N_HEADS, NOPE),
                              _unpad_heads(dw_uv_p, HB, V_DIM).reshape(KV_RANK, N_HEADS, V_DIM)], -1)
    *dxp_h, d_lbl, d_ghn, p_uq, dw_ukv_all = _hgrn_bwd(
        xp_h, o_raw, s_all, d_ohg, hgrn_lb_logits, hgrn_out_norm,
        ([SCATTER, GATHER], [dw_uq, dw_ukv.reshape(KV_RANK, N_HEADS * HB)]))
    grad_x, d_pre = _in_bwd_x(xs, dxp_m, dxp_h, dh1, w_in_al, attn_pre_norm, tm)
    dw_in_m = _mm_tn("in_bwd_w_mla", u, dxp_m, MLA_IN, tm)
    dw_in_h = [_mm_tn("in_bwd_w_hgrn%d" % j, u, dxp_h[j], G_W, tm) for j in range(4)]
    dw_in = jnp.concatenate([dw_in_m[:, :kr_end]] + dw_in_h, 1).reshape(D, N_DEV, IN_SH).transpose(1, 0, 2).astype(BF)
    d_on = _unpad_heads(d_on_p, HB, V_DIM)

    ukv2 = lambda a: a.reshape(KV_RANK, N_HEADS * HB)
    small_g = [d_pre, d_gq, d_gkv, dw_ukv_all, d_on, d_lbl, d_ghn, d_post, d_fpre, d_fpost]
    small_w = [attn_pre_norm, mla_q_norm, mla_kv_norm, ukv2(mla_w_ukv), mla_out_norm, hgrn_lb_logits, hgrn_out_norm,
               attn_post_norm, ffn_pre_norm, ffn_post_norm]
    small_m = [m_attn_pre_norm, m_mla_q_norm, m_mla_kv_norm, ukv2(m_mla_w_ukv), m_mla_out_norm, m_hgrn_lb_logits,
               m_hgrn_out_norm, m_attn_post_norm, m_ffn_pre_norm, m_ffn_post_norm]
    small_v = [v_attn_pre_norm, v_mla_q_norm, v_mla_kv_norm, ukv2(v_mla_w_ukv), v_mla_out_norm, v_hgrn_lb_logits,
               v_hgrn_out_norm, v_attn_post_norm, v_ffn_pre_norm, v_ffn_post_norm]
    s_g, s_d, s_m, s_v, p_in = _final_exchange_adam(small_g, 3, dw_in, small_w, small_m, small_v)
    r_in = _shard_adam("adam_w_in", p_in, w_in[0], m_w_in[0], v_w_in[0], 256)
    r_uq = _shard_adam("adam_w_uq", p_uq, mla_w_uq[0].reshape(uq_sh), m_mla_w_uq[0].reshape(uq_sh),
                       v_mla_w_uq[0].reshape(uq_sh), uq_sh[0])
    r_out = _shard_adam("adam_w_out", p_out, w_out[0], m_w_out[0], v_w_out[0], D // N_DEV)
    r_g = _shard_adam("adam_w_gate", p_g, w_gate[0], m_w_gate[0], v_w_gate[0], 256)
    r_u = _shard_adam("adam_w_up", p_u, w_up[0], m_w_up[0], v_w_up[0], 256)
    r_d = _shard_adam("adam_w_down", p_d, w_down[0], m_w_down[0], v_w_down[0], FF_SH // 2)

    loss = lax.psum(loss_row[0, 0], ("x", "y", "c"))

    def assemble(big, small):
        b_in, b_uq, b_out, b_g, b_u, b_d = big
        return [small[0], b_in[None], small[1], b_uq.reshape(mla_w_uq.shape), small[2],
                small[3].reshape(mla_w_ukv.shape), small[4], small[5], small[6], b_out[None], small[7], small[8],
                b_g[None], b_u[None], b_d[None], small[9]]

    outs = [loss, grad_x[None]]
    for idx, small in enumerate((s_g, s_d, s_m, s_v)):
        outs += assemble([r[idx] for r in (r_in, r_uq, r_out, r_g, r_u, r_d)], small)
    return tuple(outs)
```

```python
import jax
import jax.numpy as jnp
from jax import lax
from jax.experimental import pallas as pl
from jax.experimental.pallas import tpu as pltpu

BF = jnp.bfloat16
F32 = jnp.float32
MESH = pl.DeviceIdType.MESH

N_DEV = 8
D = 1024
EPS = 1e-6
ROPE_THETA = 10000.0
N_HEADS = 8
HB = 128
NOPE = 64
ROPE = 32
V_DIM = 64
QK_DIM = NOPE + ROPE
Q_RANK = 384
KV_RANK = 128
KR_PAD = 128
MLA_IN = Q_RANK + KV_RANK + KR_PAD
G_HEADS = 4
G_DIM = 128
G_W = G_HEADS * G_DIM
CHUNK = 64
SUB = 16
XP_W = MLA_IN + 4 * G_W
IN_SH = 324
IN_W = N_DEV * IN_SH
FF_SH = 352
FF_PAD = 384
MIX_W = N_HEADS * HB + G_W

ADAM_LR = 0.001
ADAM_B1 = 0.9
ADAM_B2 = 0.999
ADAM_EPS = 1e-08
ADAM_WD = 0.01
ADAM_STEP = 10

_TM = 512
_TQ = 512
_AH = 2
_FB = 768
_TB = 1024
_HP = 4
_VMEM_LIMIT = 56 * 1024 * 1024
NEG = -1e30


def _dot(a, b):
    return jnp.dot(a.astype(BF), b.astype(BF), preferred_element_type=F32)


def _dot_nt(a, b):
    return lax.dot_general(a.astype(BF), b.astype(BF), (((1,), (1,)), ((), ())), preferred_element_type=F32)


def _dot_tn(a, b):
    return lax.dot_general(a.astype(BF), b.astype(BF), (((0,), (0,)), ((), ())), preferred_element_type=F32)


def _sigmoid(x):
    return 1.0 / (1.0 + jnp.exp(-x))


def _rms(x, n):
    r = lax.rsqrt(jnp.sum(x * x, -1, keepdims=True) * (1.0 / n) + EPS)
    return x * r, r


def _rms_bwd(nx, r, g, dy, n):
    dg = jnp.sum(dy * nx, 0, keepdims=True)
    dn = dy * g
    dx = r * (dn - nx * (jnp.sum(dn * nx, -1, keepdims=True) * (1.0 / n)))
    return dx, dg


def _adamw(w, g, m, v):
    m2 = ADAM_B1 * m + (1.0 - ADAM_B1) * g
    v2 = ADAM_B2 * v + (1.0 - ADAM_B2) * (g * g)
    m_hat = m2 / (1.0 - ADAM_B1 ** ADAM_STEP)
    v_hat = v2 / (1.0 - ADAM_B2 ** ADAM_STEP)
    delta = -ADAM_LR * (m_hat / (jnp.sqrt(v_hat) + ADAM_EPS) + ADAM_WD * w)
    return delta, m2, v2


def _pcall(body, name, grid, in_specs, out_specs, out_shape, scratch=(), exchange=None):
    scratch = list(scratch)
    extra = ()
    if exchange is not None:
        kinds, extra = exchange
        in_specs, out_specs, out_shape = list(in_specs), list(out_specs), list(out_shape)
        n_in, n_out, n_scr, n_x = len(in_specs), len(out_specs), len(scratch), len(extra)
        inner = body

        def body(*refs):
            ins, rest = refs[:n_in], refs[n_in:]
            x_src, rest = rest[:n_x], rest[n_x:]
            outs, rest = rest[:n_out], rest[n_out:]
            x_dst, rest = rest[:n_x], rest[n_x:]
            ex = _Exchange(kinds, x_src, x_dst, *rest[n_scr:])
            first = pl.program_id(0) == 0
            last = pl.program_id(0) == grid[0] - 1
            for a in range(1, len(grid)):
                first = first & (pl.program_id(a) == 0)
                last = last & (pl.program_id(a) == grid[a] - 1)
            pl.when(first)(ex.start)
            inner(*ins, *outs, *rest[:n_scr])
            pl.when(last)(ex.wait)

        in_specs += [_HBM] * n_x
        out_specs += [_HBM] * n_x
        out_shape += _exchange_shapes(kinds, extra)
        scratch += _exchange_sems(n_x)
    call = pl.pallas_call(
        body, name=name, grid=grid, in_specs=in_specs, out_specs=out_specs, out_shape=out_shape,
        scratch_shapes=scratch,
        compiler_params=pltpu.CompilerParams(
            dimension_semantics=("arbitrary",) * len(grid), vmem_limit_bytes=_VMEM_LIMIT))
    return lambda *operands: call(*operands, *extra)


def _full(shape):
    return pl.BlockSpec(shape, lambda *_: (0,) * len(shape))


def _rows(tm, n):
    return pl.BlockSpec((tm, n), lambda i, *_: (i, 0))


def _sds(shape, dtype=F32):
    return jax.ShapeDtypeStruct(shape, dtype)


def _peer(k, x, y, c):
    px = 1 - x if (k >> 2) & 1 else x
    py = 1 - y if (k >> 1) & 1 else y
    pc = 1 - c if k & 1 else c
    return px, py, pc


GATHER, SCATTER, GATHER_COLS, SCATTER_COLS = "gather", "scatter", "gather_cols", "scatter_cols"


class _Exchange:
    def __init__(self, kinds, srcs, dsts, send_sems, recv_sems, loc_sems):
        self.kinds, self.srcs, self.dsts = kinds, srcs, dsts
        self.send_sems, self.recv_sems, self.loc_sems = send_sems, recv_sems, loc_sems
        self.x, self.y, self.c = lax.axis_index("x"), lax.axis_index("y"), lax.axis_index("c")
        self.me = 4 * self.x + 2 * self.y + self.c

    @staticmethod
    def _cols(ref, slot):
        n = ref.shape[1] // N_DEV
        return ref.at[:, pl.ds(pl.multiple_of(slot * n, 128), n)]

    def _src(self, w, slot):
        kind = self.kinds[w]
        if kind in (GATHER, GATHER_COLS):
            return self.srcs[w]
        return self.srcs[w].at[slot] if kind == SCATTER else self._cols(self.srcs[w], slot)

    def _dst(self, w, slot):
        return self._cols(self.dsts[w], slot) if self.kinds[w] == GATHER_COLS else self.dsts[w].at[slot]

    def _copy(self, w, k, outgoing):
        px, py, pc = _peer(k, self.x, self.y, self.c)
        pid = 4 * px + 2 * py + pc
        return pltpu.make_async_remote_copy(
            src_ref=self._src(w, pid if outgoing else self.me),
            dst_ref=self._dst(w, self.me if outgoing else pid),
            send_sem=self.send_sems.at[w, k - 1], recv_sem=self.recv_sems.at[w, k - 1],
            device_id=(px, py, pc), device_id_type=MESH)

    def _local(self, w):
        return pltpu.make_async_copy(self._src(w, self.me), self._dst(w, self.me), self.loc_sems.at[w])

    def start(self):
        for w in range(len(self.srcs)):
            self._local(w).start()
            for k in range(1, N_DEV):
                self._copy(w, k, True).start()

    def wait(self):
        for w in range(len(self.srcs)):
            self._local(w).wait()
            for k in range(1, N_DEV):
                self._copy(w, k, False).wait_recv()
        for w in range(len(self.srcs)):
            for k in range(1, N_DEV):
                self._copy(w, k, True).wait_send()


def _exchange_sems(n_w):
    return [pltpu.SemaphoreType.DMA((n_w, N_DEV - 1)), pltpu.SemaphoreType.DMA((n_w, N_DEV - 1)),
            pltpu.SemaphoreType.DMA((n_w,))]


def _exchange_shapes(kinds, srcs):
    def shape(kind, s):
        if kind == GATHER:
            return (N_DEV,) + tuple(s.shape)
        if kind == GATHER_COLS:
            return (s.shape[0], N_DEV * s.shape[1])
        if kind == SCATTER_COLS:
            return (N_DEV, s.shape[0], s.shape[1] // N_DEV)
        return tuple(s.shape)
    return [_sds(shape(kd, s), s.dtype) for kd, s in zip(kinds, srcs)]


_HBM = pl.BlockSpec(memory_space=pl.ANY)


def _cast_shards(w_in, w_uq, w_out, w_gate, w_up, w_down):
    shapes = [(D, IN_SH), (Q_RANK // N_DEV, N_HEADS * QK_DIM), (D // N_DEV, D), (D, FF_PAD), (D, FF_PAD), (FF_PAD, D)]

    def body(win, wuq, wout, wg, wu, wd, sin_, suq, sout, sg, su, sd):
        sin_[...] = win[...].astype(BF)
        suq[...] = wuq[...].astype(BF)
        sout[...] = wout[...].astype(BF)
        sg[...] = jnp.zeros(sg.shape, BF)
        sg[:, 0:FF_SH] = wg[...].astype(BF)
        su[...] = jnp.zeros(su.shape, BF)
        su[:, 0:FF_SH] = wu[...].astype(BF)
        sd[...] = jnp.zeros(sd.shape, BF)
        sd[0:FF_SH, :] = wd[...].astype(BF)

    vm = pl.BlockSpec(memory_space=pltpu.VMEM)
    return pl.pallas_call(
        body, name="cast_shards", in_specs=[vm] * 6, out_specs=[vm] * 6,
        out_shape=[_sds(s, BF) for s in shapes],
        compiler_params=pltpu.CompilerParams(vmem_limit_bytes=_VMEM_LIMIT),
    )(w_in, w_uq, w_out, w_gate, w_up, w_down)


def _exchange_call(name, kinds, srcs):
    n_w = len(srcs)

    def body(*refs):
        ex = _Exchange(kinds, refs[:n_w], refs[n_w:2 * n_w], *refs[2 * n_w:])
        ex.start()
        ex.wait()

    return pl.pallas_call(
        body, name=name, in_specs=[_HBM] * n_w, out_specs=[_HBM] * n_w,
        out_shape=_exchange_shapes(kinds, srcs), scratch_shapes=_exchange_sems(n_w))(*srcs)


def _row_offsets(arrays):
    offs, rows = [], 0
    for a in arrays:
        offs.append(rows)
        rows += a.shape[0]
    return offs, -(-rows // 8) * 8


def _final_exchange(vecs, dw_in):
    n_p = len(vecs)
    offs, rows = _row_offsets(vecs)

    def body(*refs):
        g_refs = refs[:n_p]
        din, rall, pin = refs[n_p:n_p + 3]
        pk, send_sems, recv_sems, loc_sem, xs_send, xs_recv, xs_loc = refs[n_p + 3:]
        ex = _Exchange([SCATTER], [din], [pin], xs_send, xs_recv, xs_loc)
        ex.start()
        x, y, c, me = ex.x, ex.y, ex.c, ex.me
        pk[...] = jnp.zeros(pk.shape, F32)
        for p in range(n_p):
            r, n = g_refs[p].shape
            pk[offs[p]:offs[p] + r, 0:n] = g_refs[p][...]

        def remote(k):
            return pltpu.make_async_remote_copy(
                src_ref=pk, dst_ref=rall.at[me], send_sem=send_sems.at[k - 1], recv_sem=recv_sems.at[k - 1],
                device_id=_peer(k, x, y, c), device_id_type=MESH)

        def arrival(k):
            px, py, pc = _peer(k, x, y, c)
            return pltpu.make_async_remote_copy(
                src_ref=pk, dst_ref=rall.at[4 * px + 2 * py + pc], send_sem=send_sems.at[k - 1],
                recv_sem=recv_sems.at[k - 1], device_id=(px, py, pc), device_id_type=MESH)

        local = pltpu.make_async_copy(pk, rall.at[me], loc_sem)
        local.start()
        for k in range(1, N_DEV):
            remote(k).start()
        local.wait()
        for k in range(1, N_DEV):
            arrival(k).wait_recv()
        for k in range(1, N_DEV):
            remote(k).wait_send()
        ex.wait()

    vm = pl.BlockSpec(memory_space=pltpu.VMEM)
    return pl.pallas_call(
        body, name="final_exchange",
        in_specs=[vm] * n_p + [_HBM], out_specs=[vm, _HBM],
        out_shape=[_sds((N_DEV, rows, D)), _sds(dw_in.shape, dw_in.dtype)],
        scratch_shapes=[pltpu.VMEM((rows, D), F32),
                        pltpu.SemaphoreType.DMA((N_DEV - 1,)), pltpu.SemaphoreType.DMA((N_DEV - 1,)),
                        pltpu.SemaphoreType.DMA] + _exchange_sems(1),
    )(*vecs, dw_in)


def _small_adam(rall, big_parts, big, ws, ms, vs):
    n_p = len(ws)
    packed = [w for p, w in enumerate(ws) if p != big] + [jax.ShapeDtypeStruct((1, HB), F32)]
    offs, _ = _row_offsets(packed)
    offs = offs[:big] + [None] + offs[big:]

    def total(ref, sl):
        g = ref[(0,) + sl]
        for j in range(1, N_DEV):
            g = g + ref[(j,) + sl]
        return g

    def body(*refs):
        rall_ref, big_ref = refs[:2]
        w_refs, m_refs, v_refs = refs[2:2 + n_p], refs[2 + n_p:2 + 2 * n_p], refs[2 + 2 * n_p:2 + 3 * n_p]
        outs = refs[2 + 3 * n_p:]
        for p in range(n_p):
            r, n = w_refs[p].shape
            if p == big:
                g = total(big_ref, (slice(0, r), slice(0, n)))
            else:
                g = total(rall_ref, (slice(offs[p], offs[p] + r), slice(0, n)))
            delta, m2, v2 = _adamw(w_refs[p][...], g, m_refs[p][...], v_refs[p][...])
            outs[p][...] = g
            outs[n_p + p][...] = delta
            outs[2 * n_p + p][...] = m2
            outs[3 * n_p + p][...] = v2
        outs[4 * n_p][...] = total(rall_ref, (slice(offs[n_p], offs[n_p] + 1), slice(0, HB)))

    vm = pl.BlockSpec(memory_space=pltpu.VMEM)
    res = pl.pallas_call(
        body, name="small_adam", in_specs=[vm] * (2 + 3 * n_p), out_specs=[vm] * (4 * n_p + 1),
        out_shape=[_sds(w.shape) for w in ws] * 4 + [_sds((1, HB))],
        compiler_params=pltpu.CompilerParams(vmem_limit_bytes=_VMEM_LIMIT),
    )(rall, big_parts, *ws, *ms, *vs)
    return res[:n_p], res[n_p:2 * n_p], res[2 * n_p:3 * n_p], res[3 * n_p:4 * n_p], res[4 * n_p]


def _shard_adam(name, parts, w, m, v, tr):
    a0, b0 = w.shape
    b = parts.shape[2]

    def body(p_ref, w_ref, m_ref, v_ref, g_out, d_out, m_out, v_out):
        g = p_ref[0].astype(F32)
        for j in range(1, N_DEV):
            g = g + p_ref[j].astype(F32)
        g = g[:, 0:b0]
        delta, m2, v2 = _adamw(w_ref[...], g, m_ref[...], v_ref[...])
        g_out[...] = g
        d_out[...] = delta
        m_out[...] = m2
        v_out[...] = v2

    blk = pl.BlockSpec((tr, b0), lambda i: (i, 0))
    return _pcall(
        body, name, (a0 // tr,),
        [pl.BlockSpec((N_DEV, tr, b), lambda i: (0, i, 0)), blk, blk, blk],
        [blk] * 4, [_sds((a0, b0))] * 4)(parts, w, m, v)


def _fwd_in(x, g_pre, w_in_al, tm):
    T = x.shape[0]

    def body(x_ref, g_ref, w_ref, xm_ref, xh_ref, u_ref):
        nx, _ = _rms(x_ref[...], D)
        u = (nx * g_ref[...]).astype(BF)
        u_ref[...] = u
        xm_ref[...] = jnp.dot(u, w_ref[:, 0:MLA_IN], preferred_element_type=F32)
        xh_ref[...] = jnp.dot(u, w_ref[:, MLA_IN:XP_W], preferred_element_type=F32)

    return _pcall(body, "fwd_in", (T // tm,),
                  [_rows(tm, D), _full((1, D)), _full((D, XP_W))],
                  [_rows(tm, MLA_IN), _rows(tm, 4 * G_W), _rows(tm, D)],
                  [_sds((T, MLA_IN)), _sds((T, 4 * G_W)), _sds((T, D), BF)])(x, g_pre, w_in_al)


def _rope(blk, ta, tb1, tb2):
    return blk * ta + pltpu.roll(blk, HB - ROPE // 2, 1) * tb1 + pltpu.roll(blk, ROPE // 2, 1) * tb2


def _unrope(d, ta, tb1, tb2):
    return d * ta + pltpu.roll(d * tb1, ROPE // 2, 1) + pltpu.roll(d * tb2, HB - ROPE // 2, 1)


def _mla_prep(xp, tabs, g_q, g_kv, w_uq, w_uk, w_uv, tm):
    T = xp.shape[0]
    W = N_HEADS * HB

    def body(xp_ref, ta_ref, tb1_ref, tb2_ref, gq_ref, gkv_ref, wuq_ref, wuk_ref, wuv_ref, q_ref, k_ref, v_ref):
        ta, tb1, tb2 = ta_ref[...], tb1_ref[...], tb2_ref[...]
        nq, _ = _rms(xp_ref[:, 0:Q_RANK], Q_RANK)
        nkv, _ = _rms(xp_ref[:, Q_RANK:Q_RANK + KV_RANK], KV_RANK)
        nkv = (nkv * gkv_ref[...]).astype(BF)
        qpre = _dot(nq * gq_ref[...], wuq_ref[...])
        kpre = jnp.dot(nkv, wuk_ref[...], preferred_element_type=F32)
        v_ref[...] = jnp.dot(nkv, wuv_ref[...], preferred_element_type=F32).astype(BF)
        kr = _rope(pltpu.roll(xp_ref[:, Q_RANK + KV_RANK:MLA_IN], NOPE, 1), ta, tb1, tb2)
        for h in range(N_HEADS):
            sl = slice(h * HB, (h + 1) * HB)
            q_ref[:, sl] = _rope(qpre[:, sl], ta, tb1, tb2).astype(BF)
            k_ref[:, sl] = (kpre[:, sl] + kr).astype(BF)

    tab = _rows(tm, HB)
    return _pcall(body, "mla_prep", (T // tm,),
                  [_rows(tm, MLA_IN), tab, tab, tab, _full((1, Q_RANK)), _full((1, KV_RANK)),
                   _full((Q_RANK, W)), _full((KV_RANK, W)), _full((KV_RANK, W))],
                  [_rows(tm, W)] * 3, [_sds((T, W), BF)] * 3)(xp, *tabs, g_q, g_kv, w_uq, w_uk, w_uv)


def _flash_fwd(q, k, v, tq, exchange=None):
    T = q.shape[0]
    scale = QK_DIM ** -0.5

    hp = _AH
    W = hp * HB

    def body(q_ref, k_ref, v_ref, o_ref, lse_ref):
        i = pl.program_id(1)

        def blk(j, carry, masked):
            st = pl.multiple_of(j * tq, tq)
            out = []
            for h in range(hp):
                ls = slice(h * HB, (h + 1) * HB)
                m, l, acc = carry[h]
                s = _dot_nt(q_ref[:, ls], k_ref[pl.ds(st, tq), ls]) * scale
                if masked:
                    r = lax.broadcasted_iota(jnp.int32, (tq, tq), 0)
                    c = lax.broadcasted_iota(jnp.int32, (tq, tq), 1)
                    s = jnp.where(c <= r, s, NEG)
                m2 = jnp.maximum(m, jnp.max(s, -1, keepdims=True))
                p = jnp.exp(s - m2)
                a = jnp.exp(m - m2)
                out.append((m2, a * l + jnp.sum(p, -1, keepdims=True), a * acc + _dot(p, v_ref[pl.ds(st, tq), ls])))
            return tuple(out)

        init = tuple((jnp.full((tq, 1), NEG, F32), jnp.zeros((tq, 1), F32), jnp.zeros((tq, HB), F32))
                     for _ in range(hp))
        carry = lax.fori_loop(0, i, lambda j, cr: blk(j, cr, False), init)
        res = blk(i, carry, True)
        for h in range(hp):
            ls = slice(h * HB, (h + 1) * HB)
            m, l, acc = res[h]
            o_ref[:, ls] = acc / l
            lse_ref[:, ls] = jnp.broadcast_to(m + jnp.log(l), (tq, HB))

    qs = pl.BlockSpec((tq, W), lambda h, i: (i, h))
    kvs = pl.BlockSpec((T, W), lambda h, i: (0, h))
    return _pcall(body, "flash_fwd", (N_HEADS // hp, T // tq), [qs, kvs, kvs], [qs, qs],
                  [_sds((T, N_HEADS * HB))] * 2, exchange=exchange)(q, k, v)


def _gates(hq, hf, lb):
    sig = _sigmoid(hf)
    f = lb + (1.0 - lb) * sig
    sq = _sigmoid(hq)
    return hq * sq, 1.0 - f, f, jnp.log(f), sig, sq


def _lower_bound(lbl_ref):
    l0, l1 = lbl_ref[0:1, :], lbl_ref[1:2, :]
    mx = jnp.maximum(l0, l1)
    e0, e1 = jnp.exp(l0 - mx), jnp.exp(l1 - mx)
    return e0 / (e0 + e1)


def _split3(x):
    hi = x.astype(BF)
    r1 = x - hi.astype(F32)
    mid = r1.astype(BF)
    lo = (r1 - mid.astype(F32)).astype(BF)
    return hi, mid, lo


def _tri_mm(tri, x):
    hi, mid, lo = _split3(x)
    mm = lambda t: jnp.dot(tri, t, preferred_element_type=F32)
    return mm(hi) + mm(mid) + mm(lo)


def _anchor_mask(i, row, col):
    return ((row >> 4) == i) & (col < SUB * i)


def _diag_mask(d, row, col):
    return (col == row - d) & ((row & (SUB - 1)) >= d)


def _intra(q, k, b, b_s, row, col):
    a = jnp.zeros((CHUNK, CHUNK), F32)
    for i in range(1, CHUNK // SUB):
        b0 = b_s[SUB * i - 1:SUB * i, :]
        qs = q * jnp.exp(jnp.minimum(b - b0, 0.0))
        ks = k * jnp.exp(jnp.minimum(b0 - b, 0.0))
        a = a + jnp.where(_anchor_mask(i, row, col), _dot_nt(qs, ks), 0.0)
    for d in range(SUB):
        ksh = pltpu.roll(k, d, 0) if d else k
        bsh = pltpu.roll(b, d, 0) if d else b
        e = jnp.exp(jnp.minimum(b - bsh, 0.0))
        val = jnp.sum(q * ksh * e, -1, keepdims=True)
        a = a + jnp.where(_diag_mask(d, row, col), val, 0.0)
    return a


def _intra_bwd(q, k, b, b_s, da, row, col):
    dq = jnp.zeros((CHUNK, G_DIM), F32)
    dk = jnp.zeros((CHUNK, G_DIM), F32)
    for i in range(1, CHUNK // SUB):
        b0 = b_s[SUB * i - 1:SUB * i, :]
        eq = jnp.exp(jnp.minimum(b - b0, 0.0))
        ek = jnp.exp(jnp.minimum(b0 - b, 0.0))
        dai = jnp.where(_anchor_mask(i, row, col), da, 0.0)
        dq = dq + _dot(dai, k * ek) * eq
        dk = dk + _dot_tn(dai, q * eq) * ek
    for d in range(SUB):
        ksh = pltpu.roll(k, d, 0) if d else k
        bsh = pltpu.roll(b, d, 0) if d else b
        e = jnp.exp(jnp.minimum(b - bsh, 0.0))
        g = jnp.sum(jnp.where(_diag_mask(d, row, col), da, 0.0), -1, keepdims=True) * e
        dq = dq + g * ksh
        cb = g * q
        dk = dk + (pltpu.roll(cb, CHUNK - d, 0) if d else cb)
    return dq, dk


def _hgrn_fwd(xp, lb_logits, g_hn, exchange=None):
    T = xp.shape[0]
    tb = min(_TB, T)
    ncb = tb // CHUNK
    hp = _HP
    W = hp * G_DIM

    def body(hq_ref, hf_ref, hi_ref, hg_ref, lbl_ref, ghn_ref, out_ref, oraw_ref, sall_ref, st_ref, b_s):
        lb_all = _lower_bound(lbl_ref)

        @pl.when(pl.program_id(1) == 0)
        def _():
            st_ref[...] = jnp.zeros(st_ref.shape, F32)

        row = lax.broadcasted_iota(jnp.int32, (CHUNK, CHUNK), 0)
        col = lax.broadcasted_iota(jnp.int32, (CHUNK, CHUNK), 1)
        tri = (col <= row).astype(BF)

        def chunk(c, carry):
            sl = pl.ds(pl.multiple_of(c * CHUNK, CHUNK), CHUNK)
            for h in range(hp):
                ls = slice(h * G_DIM, (h + 1) * G_DIM)
                q, k, _, lf, _, _ = _gates(hq_ref[sl, ls], hf_ref[sl, ls], lb_all[:, ls])
                v = hi_ref[sl, ls]
                b = _tri_mm(tri, lf)
                b_s[h] = b
                st = st_ref[h]
                sall_ref[c, h] = st
                o = _dot_nt(q * jnp.exp(b), st) + _dot(_intra(q, k, b, b_s.at[h], row, col), v)
                bl = b_s[h, CHUNK - 1:CHUNK, :]
                st_ref[h] = st * jnp.exp(bl) + _dot_tn(v, k * jnp.exp(bl - b))
                oraw_ref[sl, ls] = o
                n, _ = _rms(o, G_DIM)
                hg = hg_ref[sl, ls]
                out_ref[sl, ls] = n * ghn_ref[:, ls] * (hg * _sigmoid(hg))
            return carry

        lax.fori_loop(0, ncb, chunk, 0)

    col_blk = lambda j: pl.BlockSpec((tb, W), lambda p, t: (t, j * (G_HEADS // hp) + p))
    head = pl.BlockSpec((tb, W), lambda p, t: (t, p))
    return _pcall(
        body, "hgrn_fwd", (G_HEADS // hp, T // tb),
        [col_blk(0), col_blk(1), col_blk(2), col_blk(3),
         pl.BlockSpec((2, W), lambda p, t: (0, p)), pl.BlockSpec((1, W), lambda p, t: (0, p))],
        [head, head, pl.BlockSpec((ncb, hp, G_DIM, G_DIM), lambda p, t: (t, p, 0, 0))],
        [_sds((T, G_W)), _sds((T, G_W)), _sds((T // CHUNK, G_HEADS, G_DIM, G_DIM))],
        scratch=[pltpu.VMEM((hp, G_DIM, G_DIM), F32), pltpu.VMEM((hp, CHUNK, G_DIM), F32)], exchange=exchange,
    )(xp, xp, xp, xp, lb_logits, g_hn)


def _fwd_out(o_pad, o_hgrn, x, g_on, w_out, g_post, g_fpre, tm):
    T = x.shape[0]

    def body(o_ref, oh_ref, x_ref, gon_ref, w_ref, gpost_ref, gfpre_ref, h1_ref, y1_ref, z_ref, mix_ref):
        for h in range(N_HEADS):
            sl = slice(h * HB, (h + 1) * HB)
            n, _ = _rms(o_ref[:, sl], V_DIM)
            mix_ref[:, sl] = (n * gon_ref[:, sl]).astype(BF)
        mix_ref[:, N_HEADS * HB:MIX_W] = oh_ref[...].astype(BF)
        y1 = jnp.dot(mix_ref[...], w_ref[...], preferred_element_type=F32)
        y1_ref[...] = y1
        ny, _ = _rms(y1, D)
        h1 = x_ref[...] + ny * gpost_ref[...]
        h1_ref[...] = h1
        nh, _ = _rms(h1, D)
        z_ref[...] = (nh * gfpre_ref[...]).astype(BF)

    return _pcall(body, "fwd_out", (T // tm,),
                  [_rows(tm, N_HEADS * HB), _rows(tm, G_W), _rows(tm, D), _full((1, N_HEADS * HB)),
                   _full((MIX_W, D)), _full((1, D)), _full((1, D))],
                  [_rows(tm, D), _rows(tm, D), _rows(tm, D), _rows(tm, MIX_W)],
                  [_sds((T, D)), _sds((T, D)), _sds((T, D), BF), _sds((T, MIX_W), BF)],
                  )(o_pad, o_hgrn, x, g_on, w_out, g_post, g_fpre)


def _ffn_fwd(z, wg, wu, wd, h1, tgt, g_fpost, tm):
    T = z.shape[0]
    nf = wg.shape[1] // _FB

    def body(z_ref, wg_ref, wu_ref, wd_ref, h1_ref, t_ref, gp_ref,
             gs_ref, us_ref, dh2_ref, dy2_ref, dgp_ref, loss_ref, acc):
        i, j = pl.program_id(0), pl.program_id(1)
        zv = z_ref[...]
        g = jnp.dot(zv, wg_ref[...], preferred_element_type=F32)
        u = jnp.dot(zv, wu_ref[...], preferred_element_type=F32)
        gs_ref[...] = g.astype(BF)
        us_ref[...] = u.astype(BF)
        part = _dot(g * _sigmoid(g) * u, wd_ref[...])

        @pl.when(j == 0)
        def _():
            acc[...] = part

        @pl.when(j > 0)
        def _():
            acc[...] += part

        @pl.when((i == 0) & (j == 0))
        def _():
            dgp_ref[...] = jnp.zeros(dgp_ref.shape, F32)
            loss_ref[...] = jnp.zeros(loss_ref.shape, F32)

        @pl.when(j == nf - 1)
        def _():
            ny, r = _rms(acc[...], D)
            err = h1_ref[...] + ny * gp_ref[...] - t_ref[...]
            loss_ref[...] += 0.5 * jnp.sum(jnp.sum(err * err, -1, keepdims=True) * (1.0 / D), 0, keepdims=True)
            dh2 = err * (1.0 / D)
            dh2_ref[...] = dh2
            dy2, dgp = _rms_bwd(ny, r, gp_ref[...], dh2, D)
            dy2_ref[...] = dy2.astype(BF)
            dgp_ref[...] += dgp

    tok = lambda n: pl.BlockSpec((tm, n), lambda i, j: (i, 0))
    col = pl.BlockSpec((tm, _FB), lambda i, j: (i, j))
    return _pcall(
        body, "ffn_fwd", (T // tm, nf),
        [tok(D), pl.BlockSpec((D, _FB), lambda i, j: (0, j)), pl.BlockSpec((D, _FB), lambda i, j: (0, j)),
         pl.BlockSpec((_FB, D), lambda i, j: (j, 0)), tok(D), tok(D), _full((1, D))],
        [col, col, tok(D), tok(D), _full((1, D)), _full((1, HB))],
        [_sds((T, nf * _FB), BF), _sds((T, nf * _FB), BF), _sds((T, D)), _sds((T, D), BF),
         _sds((1, D)), _sds((1, HB))],
        scratch=[pltpu.VMEM((tm, D), F32)],
    )(z, wg, wu, wd, h1, tgt, g_fpost)


def _dsilu(x, s):
    return s * (1.0 + x * (1.0 - s))


def _ffn_bwd_x(dy2, gs, us, wg, wu, wd, h1, y1, dh2, g_fpre, g_post, tm):
    T = dy2.shape[0]
    nf = wg.shape[1] // _FB

    def body(dy2_ref, gs_ref, us_ref, wg_ref, wu_ref, wd_ref, h1_ref, y1_ref, dh2_ref, gf_ref, gp_ref,
             dgs_ref, dus_ref, dh1_ref, dy1_ref, dgf_ref, dgp_ref, acc):
        i, j = pl.program_id(0), pl.program_id(1)
        dff = _dot_nt(dy2_ref[...], wd_ref[...])
        g = gs_ref[...].astype(F32)
        u = us_ref[...].astype(F32)
        s = _sigmoid(g)
        dg = (dff * u * _dsilu(g, s)).astype(BF)
        du = (dff * g * s).astype(BF)
        dgs_ref[...] = dg
        dus_ref[...] = du
        part = _dot_nt(dg, wg_ref[...]) + _dot_nt(du, wu_ref[...])

        @pl.when(j == 0)
        def _():
            acc[...] = part

        @pl.when(j > 0)
        def _():
            acc[...] += part

        @pl.when((i == 0) & (j == 0))
        def _():
            dgf_ref[...] = jnp.zeros(dgf_ref.shape, F32)
            dgp_ref[...] = jnp.zeros(dgp_ref.shape, F32)

        @pl.when(j == nf - 1)
        def _():
            nh, rh = _rms(h1_ref[...], D)
            dh, dgf = _rms_bwd(nh, rh, gf_ref[...], acc[...], D)
            dh1 = dh2_ref[...] + dh
            dh1_ref[...] = dh1
            dgf_ref[...] += dgf
            ny, ry = _rms(y1_ref[...], D)
            dy1, dgp = _rms_bwd(ny, ry, gp_ref[...], dh1, D)
            dy1_ref[...] = dy1.astype(BF)
            dgp_ref[...] += dgp

    tok = lambda n: pl.BlockSpec((tm, n), lambda i, j: (i, 0))
    col = pl.BlockSpec((tm, _FB), lambda i, j: (i, j))
    wcol = pl.BlockSpec((D, _FB), lambda i, j: (0, j))
    return _pcall(
        body, "ffn_bwd_x", (T // tm, nf),
        [tok(D), col, col, wcol, wcol, pl.BlockSpec((_FB, D), lambda i, j: (j, 0)),
         tok(D), tok(D), tok(D), _full((1, D)), _full((1, D))],
        [col, col, tok(D), tok(D), _full((1, D)), _full((1, D))],
        [_sds((T, nf * _FB), BF), _sds((T, nf * _FB), BF), _sds((T, D)), _sds((T, D), BF),
         _sds((1, D)), _sds((1, D))],
        scratch=[pltpu.VMEM((tm, D), F32)],
    )(dy2, gs, us, wg, wu, wd, h1, y1, dh2, g_fpre, g_post)


def _ffn_bwd_w(z, gs, us, dgs, dus, dy2, tm):
    T = z.shape[0]
    nf = gs.shape[1] // _FB
    nt = T // tm

    def body(z_ref, gs_ref, us_ref, dgs_ref, dus_ref, dy2_ref, dwg_ref, dwu_ref, dwd_ref, ag, au, ad):
        i = pl.program_id(1)
        g = gs_ref[...].astype(F32)
        ff = g * _sigmoid(g) * us_ref[...].astype(F32)
        pg = _dot_tn(z_ref[...], dgs_ref[...])
        pu = _dot_tn(z_ref[...], dus_ref[...])
        pd = _dot_tn(ff, dy2_ref[...])

        @pl.when(i == 0)
        def _():
            ag[...] = pg
            au[...] = pu
            ad[...] = pd

        @pl.when(i > 0)
        def _():
            ag[...] += pg
            au[...] += pu
            ad[...] += pd

        @pl.when(i == nt - 1)
        def _():
            dwg_ref[...] = ag[...].astype(BF)
            dwu_ref[...] = au[...].astype(BF)
            dwd_ref[...] = ad[...].astype(BF)

    F = nf * _FB
    tok = lambda n: pl.BlockSpec((tm, n), lambda j, i: (i, 0))
    col = pl.BlockSpec((tm, _FB), lambda j, i: (i, j))
    wcol = pl.BlockSpec((D, _FB), lambda j, i: (0, j))
    wrow = pl.BlockSpec((_FB, D), lambda j, i: (j, 0))
    return _pcall(
        body, "ffn_bwd_w", (nf, nt), [tok(D), col, col, col, col, tok(D)], [wcol, wcol, wrow],
        [_sds((D, F), BF), _sds((D, F), BF), _sds((F, D), BF)],
        scratch=[pltpu.VMEM((D, _FB), F32), pltpu.VMEM((D, _FB), F32), pltpu.VMEM((_FB, D), F32)],
    )(z, gs, us, dgs, dus, dy2)


def _out_bwd(dy1, mix, o_pad, w_out, g_on, tm):
    T = dy1.shape[0]
    W = N_HEADS * HB

    def body(dy1_ref, mix_ref, o_ref, w_ref, gon_ref, do_ref, dl_ref, dohg_ref, dw_ref, dgon_ref):
        i = pl.program_id(0)
        dy1v = dy1_ref[...]
        dmix = _dot_nt(dy1v, w_ref[...])
        pw = _dot_tn(mix_ref[...], dy1v)

        @pl.when(i == 0)
        def _():
            dw_ref[...] = pw
            dgon_ref[...] = jnp.zeros(dgon_ref.shape, F32)

        @pl.when(i > 0)
        def _():
            dw_ref[...] += pw

        for h in range(N_HEADS):
            sl = slice(h * HB, (h + 1) * HB)
            ov = o_ref[:, sl]
            n, r = _rms(ov, V_DIM)
            do, dg = _rms_bwd(n, r, gon_ref[:, sl], dmix[:, sl], V_DIM)
            dgon_ref[:, sl] += dg
            do_ref[:, sl] = do.astype(BF)
            dl_ref[:, sl] = jnp.broadcast_to(jnp.sum(do * ov, -1, keepdims=True), (tm, HB))
        dohg_ref[...] = dmix[:, W:MIX_W]

    return _pcall(body, "out_bwd", (T // tm,),
                  [_rows(tm, D), _rows(tm, MIX_W), _rows(tm, W), _full((MIX_W, D)), _full((1, W))],
                  [_rows(tm, W), _rows(tm, W), _rows(tm, G_W), _full((MIX_W, D)), _full((1, W))],
                  [_sds((T, W), BF), _sds((T, W)), _sds((T, G_W)), _sds((MIX_W, D)), _sds((1, W))],
                  )(dy1, mix, o_pad, w_out, g_on)


def _flash_bwd(q, k, v, do, lse, dl, tq, exchange=None):
    T = q.shape[0]
    nq = T // tq
    scale = QK_DIM ** -0.5
    hp = _AH
    W = hp * HB

    def body(k_ref, v_ref, q_ref, do_ref, lse_ref, dl_ref, dk_ref, dv_ref, dq_ref):
        j = pl.program_id(1)

        @pl.when(j == 0)
        def _():
            dq_ref[...] = jnp.zeros(dq_ref.shape, F32)

        def blk(i, carry, masked):
            sl = pl.ds(pl.multiple_of(i * tq, tq), tq)
            out = []
            for h in range(hp):
                ls = slice(h * HB, (h + 1) * HB)
                dk, dv = carry[h]
                kv, vv = k_ref[:, ls], v_ref[:, ls]
                qv, dov = q_ref[sl, ls], do_ref[sl, ls]
                s = _dot_nt(qv, kv) * scale
                if masked:
                    r = lax.broadcasted_iota(jnp.int32, (tq, tq), 0)
                    c = lax.broadcasted_iota(jnp.int32, (tq, tq), 1)
                    s = jnp.where(c <= r, s, NEG)
                p = jnp.exp(s - lse_ref[sl, h * HB:h * HB + 1])
                ds = p * (_dot_nt(dov, vv) - dl_ref[sl, h * HB:h * HB + 1]) * scale
                dq_ref[sl, ls] += _dot(ds, kv)
                out.append((dk + _dot_tn(ds, qv), dv + _dot_tn(p, dov)))
            return tuple(out)

        zero = jnp.zeros((tq, HB), F32)
        carry = blk(j, tuple((zero, zero) for _ in range(hp)), True)
        res = lax.fori_loop(j + 1, nq, lambda i, cr: blk(i, cr, False), carry)
        for h in range(hp):
            ls = slice(h * HB, (h + 1) * HB)
            dk_ref[:, ls] = res[h][0]
            dv_ref[:, ls] = res[h][1]

    tile = pl.BlockSpec((tq, W), lambda h, j: (j, h))
    whole = pl.BlockSpec((T, W), lambda h, j: (0, h))
    return _pcall(body, "flash_bwd", (N_HEADS // hp, nq), [tile, tile, whole, whole, whole, whole],
                  [tile, tile, whole], [_sds((T, N_HEADS * HB))] * 3, exchange=exchange)(k, v, q, do, lse, dl)


def _mla_prep_bwd(xp, tabs, dq, dk, dv, g_q, g_kv, w_uq, w_uk, w_uv, tm):
    T = xp.shape[0]
    W = N_HEADS * HB

    def body(xp_ref, ta_ref, tb1_ref, tb2_ref, dq_ref, dk_ref, dv_ref, gq_ref, gkv_ref, wuq_ref, wuk_ref, wuv_ref,
             dxp_ref, dwuq_ref, dwuk_ref, dwuv_ref, dgq_ref, dgkv_ref, dqp):
        i = pl.program_id(0)
        ta, tb1, tb2 = ta_ref[...], tb1_ref[...], tb2_ref[...]
        nq, rq = _rms(xp_ref[:, 0:Q_RANK], Q_RANK)
        nkv, rkv = _rms(xp_ref[:, Q_RANK:Q_RANK + KV_RANK], KV_RANK)
        dkr = jnp.zeros((tm, HB), F32)
        for h in range(N_HEADS):
            sl = slice(h * HB, (h + 1) * HB)
            dqp[:, sl] = _unrope(dq_ref[:, sl], ta, tb1, tb2).astype(BF)
            dkr = dkr + dk_ref[:, sl]
        dkr = pltpu.roll(_unrope(dkr, ta, tb1, tb2), HB - NOPE, 1)
        lane = lax.broadcasted_iota(jnp.int32, (tm, HB), 1)
        dxp_ref[:, Q_RANK + KV_RANK:MLA_IN] = jnp.where(lane < ROPE, dkr, 0.0)
        dqpv = dqp[...]
        dkv, dvv = dk_ref[...].astype(BF), dv_ref[...].astype(BF)
        nqs = (nq * gq_ref[...]).astype(BF)
        nkvs = (nkv * gkv_ref[...]).astype(BF)
        pq, pk, pv = _dot_tn(nqs, dqpv), _dot_tn(nkvs, dkv), _dot_tn(nkvs, dvv)
        dcq, dgq = _rms_bwd(nq, rq, gq_ref[...], _dot_nt(dqpv, wuq_ref[...]), Q_RANK)
        dckv, dgkv = _rms_bwd(nkv, rkv, gkv_ref[...], _dot_nt(dkv, wuk_ref[...]) + _dot_nt(dvv, wuv_ref[...]), KV_RANK)
        dxp_ref[:, 0:Q_RANK] = dcq
        dxp_ref[:, Q_RANK:Q_RANK + KV_RANK] = dckv

        @pl.when(i == 0)
        def _():
            dwuq_ref[...] = pq
            dwuk_ref[...] = pk
            dwuv_ref[...] = pv
            dgq_ref[...] = dgq
            dgkv_ref[...] = dgkv

        @pl.when(i > 0)
        def _():
            dwuq_ref[...] += pq
            dwuk_ref[...] += pk
            dwuv_ref[...] += pv
            dgq_ref[...] += dgq
            dgkv_ref[...] += dgkv

    tab = _rows(tm, HB)
    return _pcall(
        body, "mla_prep_bwd", (T // tm,),
        [_rows(tm, MLA_IN), tab, tab, tab, _rows(tm, W), _rows(tm, W), _rows(tm, W), _full((1, Q_RANK)),
         _full((1, KV_RANK)), _full((Q_RANK, W)), _full((KV_RANK, W)), _full((KV_RANK, W))],
        [_rows(tm, MLA_IN), _full((Q_RANK, W)), _full((KV_RANK, W)), _full((KV_RANK, W)), _full((1, Q_RANK)),
         _full((1, KV_RANK))],
        [_sds((T, MLA_IN)), _sds((Q_RANK, W)), _sds((KV_RANK, W)), _sds((KV_RANK, W)), _sds((1, Q_RANK)),
         _sds((1, KV_RANK))],
        scratch=[pltpu.VMEM((tm, W), BF)],
    )(xp, *tabs, dq, dk, dv, g_q, g_kv, w_uq, w_uk, w_uv)


def _hgrn_bwd(xp, o_raw, s_all, d_out, lb_logits, g_hn, exchange=None):
    T = xp.shape[0]
    tb = min(_TB, T)
    ncb = tb // CHUNK
    nb = T // tb
    hp = _HP
    W = hp * G_DIM

    def body(hq_ref, hf_ref, hi_ref, hg_ref, o_ref, sall_ref, dout_ref, lbl_ref, ghn_ref,
             dhq_ref, dhf_ref, dhi_ref, dhg_ref, dlbl_ref, dghn_ref, dst_ref, b_s, acc_lb, acc_g):
        t = pl.program_id(1)
        lb_all = _lower_bound(lbl_ref)

        @pl.when(t == 0)
        def _():
            dst_ref[...] = jnp.zeros(dst_ref.shape, F32)
            acc_lb[...] = jnp.zeros(acc_lb.shape, F32)
            acc_g[...] = jnp.zeros(acc_g.shape, F32)

        row = lax.broadcasted_iota(jnp.int32, (CHUNK, CHUNK), 0)
        col = lax.broadcasted_iota(jnp.int32, (CHUNK, CHUNK), 1)
        tri = (col <= row).astype(BF)
        tri_t = (col >= row).astype(BF)
        last = lax.broadcasted_iota(jnp.int32, (CHUNK, G_DIM), 0) == CHUNK - 1

        def chunk(cc, carry):
            c = ncb - 1 - cc
            sl = pl.ds(pl.multiple_of(c * CHUNK, CHUNK), CHUNK)
            for h in range(hp):
                ls = slice(h * G_DIM, (h + 1) * G_DIM)
                lb, ghn = lb_all[:, ls], ghn_ref[:, ls]
                hq, hg = hq_ref[sl, ls], hg_ref[sl, ls]
                q, k, f, lf, sig, sq = _gates(hq, hf_ref[sl, ls], lb)
                v = hi_ref[sl, ls]
                b = _tri_mm(tri, lf)
                b_s[h] = b
                st = sall_ref[c, h]
                dstn = dst_ref[h]
                o = o_ref[sl, ls]
                dout = dout_ref[sl, ls]
                n, r = _rms(o, G_DIM)
                sg = _sigmoid(hg)
                dhg_ref[sl, ls] = dout * (n * ghn) * _dsilu(hg, sg)
                do, dg = _rms_bwd(n, r, ghn, dout * (hg * sg), G_DIM)
                acc_g[:, ls] += dg
                eb = jnp.exp(b)
                bl = b_s[h, CHUNK - 1:CHUNK, :]
                ebl = jnp.exp(bl)
                ekd = jnp.exp(bl - b)
                kd = k * ekd
                a = _intra(q, k, b, b_s.at[h], row, col)
                dq_i, dk_i = _intra_bwd(q, k, b, b_s.at[h], _dot_nt(do, v), row, col)
                dhi_ref[sl, ls] = _dot_tn(a, do) + _dot_nt(kd, dstn)
                dk_state = _dot(v, dstn) * ekd
                dq = dq_i + _dot(do, st) * eb
                dk = dk_i + dk_state
                dbl = jnp.sum(k * dk_state, 0, keepdims=True) + ebl * jnp.sum(dstn * st, 0, keepdims=True)
                db = q * dq - k * dk + jnp.where(last, dbl, 0.0)
                df = _tri_mm(tri_t, db) / f - dk
                dhf_ref[sl, ls] = df * (1.0 - lb) * sig * (1.0 - sig)
                acc_lb[:, ls] += jnp.sum(df * (1.0 - sig), 0, keepdims=True)
                dhq_ref[sl, ls] = dq * _dsilu(hq, sq)
                dst_ref[h] = dstn * ebl + _dot_tn(do, q * eb)
            return carry

        lax.fori_loop(0, ncb, chunk, 0)

        @pl.when(t == nb - 1)
        def _():
            dl0 = acc_lb[...] * lb_all * (1.0 - lb_all)
            dlbl_ref[0:1, :] = dl0
            dlbl_ref[1:2, :] = -dl0
            dghn_ref[...] = acc_g[...]

    col_blk = lambda j: pl.BlockSpec((tb, W), lambda p, t: (nb - 1 - t, j * (G_HEADS // hp) + p))
    head = pl.BlockSpec((tb, W), lambda p, t: (nb - 1 - t, p))
    two = pl.BlockSpec((2, W), lambda p, t: (0, p))
    one = pl.BlockSpec((1, W), lambda p, t: (0, p))
    res = _pcall(
        body, "hgrn_bwd", (G_HEADS // hp, nb),
        [col_blk(0), col_blk(1), col_blk(2), col_blk(3), head,
         pl.BlockSpec((ncb, hp, G_DIM, G_DIM), lambda p, t: (nb - 1 - t, p, 0, 0)), head, two, one],
        [head, head, head, head, two, one],
        [_sds((T, G_W))] * 4 + [_sds((2, G_W)), _sds((1, G_W))],
        scratch=[pltpu.VMEM((hp, G_DIM, G_DIM), F32), pltpu.VMEM((hp, CHUNK, G_DIM), F32),
                 pltpu.VMEM((1, W), F32), pltpu.VMEM((1, W), F32)], exchange=exchange,
    )(xp, xp, xp, xp, o_raw, s_all, d_out, lb_logits, g_hn)
    return res


def _in_bwd_x(x, dxp_m, dxp_h, dh1, w_in_al, g_pre, tm):
    T = x.shape[0]

    def body(x_ref, dm_ref, d0_ref, d1_ref, d2_ref, d3_ref, dh1_ref, w_ref, g_ref, dx_ref, dg_ref):
        i = pl.program_id(0)
        du = _dot_nt(dm_ref[...], w_ref[:, 0:MLA_IN])
        for j, d_ref in enumerate((d0_ref, d1_ref, d2_ref, d3_ref)):
            du = du + _dot_nt(d_ref[...], w_ref[:, MLA_IN + j * G_W:MLA_IN + (j + 1) * G_W])
        nx, r = _rms(x_ref[...], D)
        dx, dg = _rms_bwd(nx, r, g_ref[...], du, D)
        dx_ref[...] = dh1_ref[...] + dx

        @pl.when(i == 0)
        def _():
            dg_ref[...] = dg

        @pl.when(i > 0)
        def _():
            dg_ref[...] += dg

    return _pcall(body, "in_bwd_x", (T // tm,),
                  [_rows(tm, D), _rows(tm, MLA_IN)] + [_rows(tm, G_W)] * 4 + [_rows(tm, D), _full((D, XP_W)), _full((1, D))],
                  [_rows(tm, D), _full((1, D))], [_sds((T, D)), _sds((1, D))],
                  )(x, dxp_m, *dxp_h, dh1, w_in_al, g_pre)


def _mm_tn(name, a, b, tn, tt):
    T, M = a.shape
    N = b.shape[1]

    def body(a_ref, b_ref, o_ref):
        t = pl.program_id(1)
        part = _dot_tn(a_ref[...], b_ref[...])

        @pl.when(t == 0)
        def _():
            o_ref[...] = part

        @pl.when(t > 0)
        def _():
            o_ref[...] += part

    return _pcall(body, name, (N // tn, T // tt),
                  [pl.BlockSpec((tt, M), lambda n, t: (t, 0)), pl.BlockSpec((tt, tn), lambda n, t: (t, n))],
                  pl.BlockSpec((M, tn), lambda n, t: (0, n)), _sds((M, N)))(a, b)


def _pad_heads(w, width, real):
    lead = w.shape[:-1]
    w = w.reshape(lead + (N_HEADS, real))
    w = jnp.pad(w, [(0, 0)] * len(lead) + [(0, 0), (0, width - real)])
    return w.reshape(lead + (N_HEADS * width,))


def _unpad_heads(w, width, real):
    lead = w.shape[:-1]
    return w.reshape(lead + (N_HEADS, width))[..., :real].reshape(lead + (N_HEADS * real,))


def _rope_tables(positions):
    half = ROPE // 2
    inv_freq = 1.0 / (ROPE_THETA ** (jnp.arange(0, ROPE, 2, dtype=F32) / ROPE))
    ang = positions.astype(F32)[:, None] * inv_freq
    cos, sin = jnp.cos(ang), jnp.sin(ang)
    T = positions.shape[0]
    z = lambda n: jnp.zeros((T, n), F32)
    ta = jnp.concatenate([jnp.ones((T, NOPE), F32), cos, cos, z(HB - QK_DIM)], 1)
    tb1 = jnp.concatenate([z(NOPE), -sin, z(half), z(HB - QK_DIM)], 1)
    tb2 = jnp.concatenate([z(NOPE), z(half), sin, z(HB - QK_DIM)], 1)
    return ta, tb1, tb2


def kernel(x, positions, attn_pre_norm, w_in, mla_q_norm, mla_w_uq, mla_kv_norm, mla_w_ukv, mla_out_norm, hgrn_lb_logits, hgrn_out_norm, w_out, attn_post_norm, ffn_pre_norm, w_gate, w_up, w_down, ffn_post_norm, loss_target, m_attn_pre_norm, m_w_in, m_mla_q_norm, m_mla_w_uq, m_mla_kv_norm, m_mla_w_ukv, m_mla_out_norm, m_hgrn_lb_logits, m_hgrn_out_norm, m_w_out, m_attn_post_norm, m_ffn_pre_norm, m_w_gate, m_w_up, m_w_down, m_ffn_post_norm, v_attn_pre_norm, v_w_in, v_mla_q_norm, v_mla_w_uq, v_mla_kv_norm, v_mla_w_ukv, v_mla_out_norm, v_hgrn_lb_logits, v_hgrn_out_norm, v_w_out, v_attn_post_norm, v_ffn_pre_norm, v_w_gate, v_w_up, v_w_down, v_ffn_post_norm):
    T = x.shape[1]
    tm = min(_TM, T)
    tq = min(_TQ, T)
    xs, tgt = x[0], loss_target[0]
    uq_sh = (Q_RANK // N_DEV, N_HEADS * QK_DIM)

    b_in, b_uq, b_out, b_g, b_u, b_d = _cast_shards(
        w_in[0], mla_w_uq[0].reshape(uq_sh), w_out[0], w_gate[0], w_up[0], w_down[0])
    g_in, g_uq = _exchange_call("ag_first", [GATHER, GATHER], [b_in, b_uq])
    w_in_full = g_in.transpose(1, 0, 2).reshape(D, IN_W)
    kr_end = Q_RANK + KV_RANK + ROPE
    w_in_al = jnp.concatenate([w_in_full[:, :kr_end], jnp.zeros((D, KR_PAD - ROPE), BF), w_in_full[:, kr_end:]], 1)
    w_uq_p = _pad_heads(g_uq.reshape(Q_RANK, N_HEADS * QK_DIM), HB, QK_DIM)
    w_ukv = mla_w_ukv[0].astype(BF)
    w_uk_p = _pad_heads(w_ukv[..., :NOPE].reshape(KV_RANK, N_HEADS * NOPE), HB, NOPE)
    w_uv_p = _pad_heads(w_ukv[..., NOPE:].reshape(KV_RANK, N_HEADS * V_DIM), HB, V_DIM)
    g_on_p = _pad_heads(mla_out_norm, HB, V_DIM)
    tabs = _rope_tables(positions[0])

    xp_m, xp_h, u = _fwd_in(xs, attn_pre_norm, w_in_al, tm)
    q_att, k_att, v_att = _mla_prep(xp_m, tabs, mla_q_norm, mla_kv_norm, w_uq_p, w_uk_p, w_uv_p, tm)
    o_hgrn, o_raw, s_all, g_out, wd = _hgrn_fwd(xp_h, hgrn_lb_logits, hgrn_out_norm, ([GATHER, GATHER], [b_out, b_d]))
    wd = wd.reshape(N_DEV * FF_PAD, D)
    o_pad, lse, wg, wu = _flash_fwd(q_att, k_att, v_att, tq, ([GATHER_COLS, GATHER_COLS], [b_g, b_u]))
    w_out_full = g_out.reshape(D, D)
    w_out_mla = jnp.pad(w_out_full[:N_HEADS * V_DIM].reshape(N_HEADS, V_DIM, D), ((0, 0), (0, HB - V_DIM), (0, 0)))
    w_out_p = jnp.concatenate([w_out_mla.reshape(N_HEADS * HB, D), w_out_full[N_HEADS * V_DIM:]], 0)
    h1, y1, z, mix = _fwd_out(o_pad, o_hgrn, xs, g_on_p, w_out_p, attn_post_norm, ffn_pre_norm, tm)
    gs, us, dh2, dy2, d_fpost, loss_row = _ffn_fwd(z, wg, wu, wd, h1, tgt, ffn_post_norm, tm)

    dgs, dus, dh1, dy1, d_fpre, d_post = _ffn_bwd_x(dy2, gs, us, wg, wu, wd, h1, y1, dh2, ffn_pre_norm, attn_post_norm, tm)
    dwg, dwu, dwd = _ffn_bwd_w(z, gs, us, dgs, dus, dy2, tm)
    do_pad, dl, d_ohg, dw_out_p, d_on_p = _out_bwd(dy1, mix, o_pad, w_out_p, g_on_p, tm)
    dw_out_mla = dw_out_p[:N_HEADS * HB].reshape(N_HEADS, HB, D)[:, :V_DIM].reshape(N_HEADS * V_DIM, D)
    dw_out = jnp.concatenate([dw_out_mla, dw_out_p[N_HEADS * HB:]], 0).reshape(N_DEV, D // N_DEV, D).astype(BF)
    dk_att, dv_att, dq_att, p_g, p_u, p_d, p_out = _flash_bwd(
        q_att, k_att, v_att, do_pad, lse, dl, tq,
        ([SCATTER_COLS, SCATTER_COLS, SCATTER, SCATTER], [dwg, dwu, dwd.reshape(N_DEV, FF_PAD, D), dw_out]))
    dxp_m, dw_uq_p, dw_uk_p, dw_uv_p, d_gq, d_gkv = _mla_prep_bwd(
        xp_m, tabs, dq_att, dk_att, dv_att, mla_q_norm, mla_kv_norm, w_uq_p, w_uk_p, w_uv_p, tm)
    dw_uq = _unpad_heads(dw_uq_p, HB, QK_DIM).reshape((N_DEV,) + uq_sh).astype(BF)
    dw_ukv = jnp.concatenate([_unpad_heads(dw_uk_p, HB, NOPE).reshape(KV_RANK, N_HEADS, NOPE),
                              _unpad_heads(dw_uv_p, HB, V_DIM).reshape(KV_RANK, N_HEADS, V_DIM)], -1)
    *dxp_h, d_lbl, d_ghn, p_uq, dw_ukv_all = _hgrn_bwd(
        xp_h, o_raw, s_all, d_ohg, hgrn_lb_logits, hgrn_out_norm,
        ([SCATTER, GATHER], [dw_uq, dw_ukv.reshape(KV_RANK, N_HEADS * HB)]))
    grad_x, d_pre = _in_bwd_x(xs, dxp_m, dxp_h, dh1, w_in_al, attn_pre_norm, tm)
    dw_in_m = _mm_tn("in_bwd_w_mla", u, dxp_m, MLA_IN, tm)
    dw_in_h = [_mm_tn("in_bwd_w_hgrn%d" % j, u, dxp_h[j], G_W, tm) for j in range(4)]
    dw_in = jnp.concatenate([dw_in_m[:, :kr_end]] + dw_in_h, 1).reshape(D, N_DEV, IN_SH).transpose(1, 0, 2).astype(BF)
    d_on = _unpad_heads(d_on_p, HB, V_DIM)

    ukv2 = lambda a: a.reshape(KV_RANK, N_HEADS * HB)
    vecs = [d_pre, d_gq, d_gkv, d_on, d_lbl, d_ghn, d_post, d_fpre, d_fpost, loss_row]
    small_w = [attn_pre_norm, mla_q_norm, mla_kv_norm, ukv2(mla_w_ukv), mla_out_norm, hgrn_lb_logits, hgrn_out_norm,
               attn_post_norm, ffn_pre_norm, ffn_post_norm]
    small_m = [m_attn_pre_norm, m_mla_q_norm, m_mla_kv_norm, ukv2(m_mla_w_ukv), m_mla_out_norm, m_hgrn_lb_logits,
               m_hgrn_out_norm, m_attn_post_norm, m_ffn_pre_norm, m_ffn_post_norm]
    small_v = [v_attn_pre_norm, v_mla_q_norm, v_mla_kv_norm, ukv2(v_mla_w_ukv), v_mla_out_norm, v_hgrn_lb_logits,
               v_hgrn_out_norm, v_attn_post_norm, v_ffn_pre_norm, v_ffn_post_norm]
    rall, p_in = _final_exchange(vecs, dw_in)
    s_g, s_d, s_m, s_v, loss_all = _small_adam(rall, dw_ukv_all, 3, small_w, small_m, small_v)
    r_in = _shard_adam("adam_w_in", p_in, w_in[0], m_w_in[0], v_w_in[0], 256)
    r_uq = _shard_adam("adam_w_uq", p_uq, mla_w_uq[0].reshape(uq_sh), m_mla_w_uq[0].reshape(uq_sh),
                       v_mla_w_uq[0].reshape(uq_sh), uq_sh[0])
    r_out = _shard_adam("adam_w_out", p_out, w_out[0], m_w_out[0], v_w_out[0], D // N_DEV)
    r_g = _shard_adam("adam_w_gate", p_g, w_gate[0], m_w_gate[0], v_w_gate[0], 256)
    r_u = _shard_adam("adam_w_up", p_u, w_up[0], m_w_up[0], v_w_up[0], 256)
    r_d = _shard_adam("adam_w_down", p_d, w_down[0], m_w_down[0], v_w_down[0], FF_SH // 2)

    loss = loss_all[0, 0]

    def assemble(big, small):
        b_in, b_uq, b_out, b_g, b_u, b_d = big
        return [small[0], b_in[None], small[1], b_uq.reshape(mla_w_uq.shape), small[2],
                small[3].reshape(mla_w_ukv.shape), small[4], small[5], small[6], b_out[None], small[7], small[8],
                b_g[None], b_u[None], b_d[None], small[9]]

    outs = [loss, grad_x[None]]
    for idx, small in enumerate((s_g, s_d, s_m, s_v)):
        outs += assemble([r[idx] for r in (r_in, r_uq, r_out, r_g, r_u, r_d)], small)
    return tuple(outs)
```

```python
import jax
import jax.numpy as jnp
from jax import lax
from jax.experimental import pallas as pl
from jax.experimental.pallas import tpu as pltpu

BF = jnp.bfloat16
F32 = jnp.float32
MESH = pl.DeviceIdType.MESH

N_DEV = 8
D = 1024
EPS = 1e-6
ROPE_THETA = 10000.0
N_HEADS = 8
HB = 128
NOPE = 64
ROPE = 32
V_DIM = 64
QK_DIM = NOPE + ROPE
Q_RANK = 384
KV_RANK = 128
KR_PAD = 128
MLA_IN = Q_RANK + KV_RANK + KR_PAD
G_HEADS = 4
G_DIM = 128
G_W = G_HEADS * G_DIM
CHUNK = 64
SUB = 16
XP_W = MLA_IN + 4 * G_W
IN_SH = 324
IN_W = N_DEV * IN_SH
FF_SH = 352
FF_PAD = 384
MIX_W = N_HEADS * HB + G_W

ADAM_LR = 0.001
ADAM_B1 = 0.9
ADAM_B2 = 0.999
ADAM_EPS = 1e-08
ADAM_WD = 0.01
ADAM_STEP = 10

_TM = 512
_TQ = 512
_AH = 2
_FB = 768
_TB = 1024
_HP = 4
_VMEM_LIMIT = 56 * 1024 * 1024
NEG = -1e30


def _dot(a, b):
    return jnp.dot(a.astype(BF), b.astype(BF), preferred_element_type=F32)


def _dot_nt(a, b):
    return lax.dot_general(a.astype(BF), b.astype(BF), (((1,), (1,)), ((), ())), preferred_element_type=F32)


def _dot_tn(a, b):
    return lax.dot_general(a.astype(BF), b.astype(BF), (((0,), (0,)), ((), ())), preferred_element_type=F32)


def _sigmoid(x):
    return 1.0 / (1.0 + jnp.exp(-x))


def _rms(x, n):
    r = lax.rsqrt(jnp.sum(x * x, -1, keepdims=True) * (1.0 / n) + EPS)
    return x * r, r


def _rms_bwd(nx, r, g, dy, n):
    dg = jnp.sum(dy * nx, 0, keepdims=True)
    dn = dy * g
    dx = r * (dn - nx * (jnp.sum(dn * nx, -1, keepdims=True) * (1.0 / n)))
    return dx, dg


def _adamw(w, g, m, v):
    m2 = ADAM_B1 * m + (1.0 - ADAM_B1) * g
    v2 = ADAM_B2 * v + (1.0 - ADAM_B2) * (g * g)
    m_hat = m2 / (1.0 - ADAM_B1 ** ADAM_STEP)
    v_hat = v2 / (1.0 - ADAM_B2 ** ADAM_STEP)
    delta = -ADAM_LR * (m_hat / (jnp.sqrt(v_hat) + ADAM_EPS) + ADAM_WD * w)
    return delta, m2, v2


def _pcall(body, name, grid, in_specs, out_specs, out_shape, scratch=(), exchange=None):
    scratch = list(scratch)
    extra = ()
    if exchange is not None:
        kinds, extra = exchange
        in_specs, out_specs, out_shape = list(in_specs), list(out_specs), list(out_shape)
        n_in, n_out, n_scr, n_x = len(in_specs), len(out_specs), len(scratch), len(extra)
        inner = body

        def body(*refs):
            ins, rest = refs[:n_in], refs[n_in:]
            x_src, rest = rest[:n_x], rest[n_x:]
            outs, rest = rest[:n_out], rest[n_out:]
            x_dst, rest = rest[:n_x], rest[n_x:]
            ex = _Exchange(kinds, x_src, x_dst, *rest[n_scr:])
            first = pl.program_id(0) == 0
            last = pl.program_id(0) == grid[0] - 1
            for a in range(1, len(grid)):
                first = first & (pl.program_id(a) == 0)
                last = last & (pl.program_id(a) == grid[a] - 1)
            pl.when(first)(ex.start)
            inner(*ins, *outs, *rest[:n_scr])
            pl.when(last)(ex.wait)

        in_specs += [_HBM] * n_x
        out_specs += [_HBM] * n_x
        out_shape += _exchange_shapes(kinds, extra)
        scratch += _exchange_sems(n_x)
    call = pl.pallas_call(
        body, name=name, grid=grid, in_specs=in_specs, out_specs=out_specs, out_shape=out_shape,
        scratch_shapes=scratch,
        compiler_params=pltpu.CompilerParams(
            dimension_semantics=("arbitrary",) * len(grid), vmem_limit_bytes=_VMEM_LIMIT))
    return lambda *operands: call(*operands, *extra)


def _full(shape):
    return pl.BlockSpec(shape, lambda *_: (0,) * len(shape))


def _rows(tm, n):
    return pl.BlockSpec((tm, n), lambda i, *_: (i, 0))


def _sds(shape, dtype=F32):
    return jax.ShapeDtypeStruct(shape, dtype)


def _peer(k, x, y, c):
    px = 1 - x if (k >> 2) & 1 else x
    py = 1 - y if (k >> 1) & 1 else y
    pc = 1 - c if k & 1 else c
    return px, py, pc


GATHER, SCATTER, GATHER_COLS, SCATTER_COLS = "gather", "scatter", "gather_cols", "scatter_cols"


class _Exchange:
    def __init__(self, kinds, srcs, dsts, send_sems, recv_sems, loc_sems):
        self.kinds, self.srcs, self.dsts = kinds, srcs, dsts
        self.send_sems, self.recv_sems, self.loc_sems = send_sems, recv_sems, loc_sems
        self.x, self.y, self.c = lax.axis_index("x"), lax.axis_index("y"), lax.axis_index("c")
        self.me = 4 * self.x + 2 * self.y + self.c

    @staticmethod
    def _cols(ref, slot):
        n = ref.shape[1] // N_DEV
        return ref.at[:, pl.ds(pl.multiple_of(slot * n, 128), n)]

    def _src(self, w, slot):
        kind = self.kinds[w]
        if kind in (GATHER, GATHER_COLS):
            return self.srcs[w]
        return self.srcs[w].at[slot] if kind == SCATTER else self._cols(self.srcs[w], slot)

    def _dst(self, w, slot):
        return self._cols(self.dsts[w], slot) if self.kinds[w] == GATHER_COLS else self.dsts[w].at[slot]

    def _copy(self, w, k, outgoing):
        px, py, pc = _peer(k, self.x, self.y, self.c)
        pid = 4 * px + 2 * py + pc
        return pltpu.make_async_remote_copy(
            src_ref=self._src(w, pid if outgoing else self.me),
            dst_ref=self._dst(w, self.me if outgoing else pid),
            send_sem=self.send_sems.at[w, k - 1], recv_sem=self.recv_sems.at[w, k - 1],
            device_id=(px, py, pc), device_id_type=MESH)

    def _local(self, w):
        return pltpu.make_async_copy(self._src(w, self.me), self._dst(w, self.me), self.loc_sems.at[w])

    def start(self):
        for w in range(len(self.srcs)):
            self._local(w).start()
            for k in range(1, N_DEV):
                self._copy(w, k, True).start()

    def wait(self):
        for w in range(len(self.srcs)):
            self._local(w).wait()
            for k in range(1, N_DEV):
                self._copy(w, k, False).wait_recv()
        for w in range(len(self.srcs)):
            for k in range(1, N_DEV):
                self._copy(w, k, True).wait_send()


def _exchange_sems(n_w):
    return [pltpu.SemaphoreType.DMA((n_w, N_DEV - 1)), pltpu.SemaphoreType.DMA((n_w, N_DEV - 1)),
            pltpu.SemaphoreType.DMA((n_w,))]


def _exchange_shapes(kinds, srcs):
    def shape(kind, s):
        if kind == GATHER:
            return (N_DEV,) + tuple(s.shape)
        if kind == GATHER_COLS:
            return (s.shape[0], N_DEV * s.shape[1])
        if kind == SCATTER_COLS:
            return (N_DEV, s.shape[0], s.shape[1] // N_DEV)
        return tuple(s.shape)
    return [_sds(shape(kd, s), s.dtype) for kd, s in zip(kinds, srcs)]


_HBM = pl.BlockSpec(memory_space=pl.ANY)


def _cast_shards(w_in, w_uq, w_out, w_gate, w_up, w_down):
    shapes = [(D, IN_SH), (Q_RANK // N_DEV, N_HEADS * QK_DIM), (D // N_DEV, D), (D, FF_PAD), (D, FF_PAD), (FF_PAD, D)]

    def body(win, wuq, wout, wg, wu, wd, sin_, suq, sout, sg, su, sd):
        sin_[...] = win[...].astype(BF)
        suq[...] = wuq[...].astype(BF)
        sout[...] = wout[...].astype(BF)
        sg[...] = jnp.zeros(sg.shape, BF)
        sg[:, 0:FF_SH] = wg[...].astype(BF)
        su[...] = jnp.zeros(su.shape, BF)
        su[:, 0:FF_SH] = wu[...].astype(BF)
        sd[...] = jnp.zeros(sd.shape, BF)
        sd[0:FF_SH, :] = wd[...].astype(BF)

    vm = pl.BlockSpec(memory_space=pltpu.VMEM)
    return pl.pallas_call(
        body, name="cast_shards", in_specs=[vm] * 6, out_specs=[vm] * 6,
        out_shape=[_sds(s, BF) for s in shapes],
        compiler_params=pltpu.CompilerParams(vmem_limit_bytes=_VMEM_LIMIT),
    )(w_in, w_uq, w_out, w_gate, w_up, w_down)


def _exchange_call(name, kinds, srcs):
    n_w = len(srcs)

    def body(*refs):
        ex = _Exchange(kinds, refs[:n_w], refs[n_w:2 * n_w], *refs[2 * n_w:])
        ex.start()
        ex.wait()

    return pl.pallas_call(
        body, name=name, in_specs=[_HBM] * n_w, out_specs=[_HBM] * n_w,
        out_shape=_exchange_shapes(kinds, srcs), scratch_shapes=_exchange_sems(n_w))(*srcs)


def _row_offsets(arrays):
    offs, rows = [], 0
    for a in arrays:
        offs.append(rows)
        rows += a.shape[0]
    return offs, -(-rows // 8) * 8


def _final_exchange(vecs, dw_in):
    n_p = len(vecs)
    offs, rows = _row_offsets(vecs)

    def body(*refs):
        g_refs = refs[:n_p]
        din, rall, pin = refs[n_p:n_p + 3]
        pk, send_sems, recv_sems, loc_sem, xs_send, xs_recv, xs_loc = refs[n_p + 3:]
        ex = _Exchange([SCATTER], [din], [pin], xs_send, xs_recv, xs_loc)
        ex.start()
        x, y, c, me = ex.x, ex.y, ex.c, ex.me
        pk[...] = jnp.zeros(pk.shape, F32)
        for p in range(n_p):
            r, n = g_refs[p].shape
            pk[offs[p]:offs[p] + r, 0:n] = g_refs[p][...]

        def remote(k):
            return pltpu.make_async_remote_copy(
                src_ref=pk, dst_ref=rall.at[me], send_sem=send_sems.at[k - 1], recv_sem=recv_sems.at[k - 1],
                device_id=_peer(k, x, y, c), device_id_type=MESH)

        def arrival(k):
            px, py, pc = _peer(k, x, y, c)
            return pltpu.make_async_remote_copy(
                src_ref=pk, dst_ref=rall.at[4 * px + 2 * py + pc], send_sem=send_sems.at[k - 1],
                recv_sem=recv_sems.at[k - 1], device_id=(px, py, pc), device_id_type=MESH)

        local = pltpu.make_async_copy(pk, rall.at[me], loc_sem)
        local.start()
        for k in range(1, N_DEV):
            remote(k).start()
        local.wait()
        for k in range(1, N_DEV):
            arrival(k).wait_recv()
        for k in range(1, N_DEV):
            remote(k).wait_send()
        ex.wait()

    vm = pl.BlockSpec(memory_space=pltpu.VMEM)
    return pl.pallas_call(
        body, name="final_exchange",
        in_specs=[vm] * n_p + [_HBM], out_specs=[vm, _HBM],
        out_shape=[_sds((N_DEV, rows, D)), _sds(dw_in.shape, dw_in.dtype)],
        scratch_shapes=[pltpu.VMEM((rows, D), F32),
                        pltpu.SemaphoreType.DMA((N_DEV - 1,)), pltpu.SemaphoreType.DMA((N_DEV - 1,)),
                        pltpu.SemaphoreType.DMA] + _exchange_sems(1),
    )(*vecs, dw_in)


def _small_adam(rall, big_parts, big, ws, ms, vs):
    n_p = len(ws)
    packed = [w for p, w in enumerate(ws) if p != big] + [jax.ShapeDtypeStruct((1, HB), F32)]
    offs, _ = _row_offsets(packed)
    offs = offs[:big] + [None] + offs[big:]

    def total(ref, sl):
        g = ref[(0,) + sl]
        for j in range(1, N_DEV):
            g = g + ref[(j,) + sl]
        return g

    def body(*refs):
        rall_ref, big_ref = refs[:2]
        w_refs, m_refs, v_refs = refs[2:2 + n_p], refs[2 + n_p:2 + 2 * n_p], refs[2 + 2 * n_p:2 + 3 * n_p]
        outs = refs[2 + 3 * n_p:]
        for p in range(n_p):
            r, n = w_refs[p].shape
            if p == big:
                g = total(big_ref, (slice(0, r), slice(0, n)))
            else:
                g = total(rall_ref, (slice(offs[p], offs[p] + r), slice(0, n)))
            delta, m2, v2 = _adamw(w_refs[p][...], g, m_refs[p][...], v_refs[p][...])
            outs[p][...] = g
            outs[n_p + p][...] = delta
            outs[2 * n_p + p][...] = m2
            outs[3 * n_p + p][...] = v2
        outs[4 * n_p][...] = total(rall_ref, (slice(offs[n_p], offs[n_p] + 1), slice(0, HB)))

    vm = pl.BlockSpec(memory_space=pltpu.VMEM)
    res = pl.pallas_call(
        body, name="small_adam", in_specs=[vm] * (2 + 3 * n_p), out_specs=[vm] * (4 * n_p + 1),
        out_shape=[_sds(w.shape) for w in ws] * 4 + [_sds((1, HB))],
        compiler_params=pltpu.CompilerParams(vmem_limit_bytes=_VMEM_LIMIT),
    )(rall, big_parts, *ws, *ms, *vs)
    return res[:n_p], res[n_p:2 * n_p], res[2 * n_p:3 * n_p], res[3 * n_p:4 * n_p], res[4 * n_p]


def _shard_adam(name, parts, w, m, v, tr):
    a0, b0 = w.shape
    b = parts.shape[2]

    def body(p_ref, w_ref, m_ref, v_ref, g_out, d_out, m_out, v_out):
        g = p_ref[0].astype(F32)
        for j in range(1, N_DEV):
            g = g + p_ref[j].astype(F32)
        g = g[:, 0:b0]
        delta, m2, v2 = _adamw(w_ref[...], g, m_ref[...], v_ref[...])
        g_out[...] = g
        d_out[...] = delta
        m_out[...] = m2
        v_out[...] = v2

    blk = pl.BlockSpec((tr, b0), lambda i: (i, 0))
    return _pcall(
        body, name, (a0 // tr,),
        [pl.BlockSpec((N_DEV, tr, b), lambda i: (0, i, 0)), blk, blk, blk],
        [blk] * 4, [_sds((a0, b0))] * 4)(parts, w, m, v)


def _fwd_in(x, g_pre, w_in_al, tm):
    T = x.shape[0]

    def body(x_ref, g_ref, w_ref, xm_ref, xh_ref, u_ref):
        nx, _ = _rms(x_ref[...], D)
        u = (nx * g_ref[...]).astype(BF)
        u_ref[...] = u
        xm_ref[...] = jnp.dot(u, w_ref[:, 0:MLA_IN], preferred_element_type=F32)
        xh_ref[...] = jnp.dot(u, w_ref[:, MLA_IN:XP_W], preferred_element_type=F32)

    return _pcall(body, "fwd_in", (T // tm,),
                  [_rows(tm, D), _full((1, D)), _full((D, XP_W))],
                  [_rows(tm, MLA_IN), _rows(tm, 4 * G_W), _rows(tm, D)],
                  [_sds((T, MLA_IN)), _sds((T, 4 * G_W)), _sds((T, D), BF)])(x, g_pre, w_in_al)


def _rope(blk, ta, tb1, tb2):
    return blk * ta + pltpu.roll(blk, HB - ROPE // 2, 1) * tb1 + pltpu.roll(blk, ROPE // 2, 1) * tb2


def _unrope(d, ta, tb1, tb2):
    return d * ta + pltpu.roll(d * tb1, ROPE // 2, 1) + pltpu.roll(d * tb2, HB - ROPE // 2, 1)


def _mla_prep(xp, tabs, g_q, g_kv, w_uq, w_uk, w_uv, tm):
    T = xp.shape[0]
    W = N_HEADS * HB

    def body(xp_ref, ta_ref, tb1_ref, tb2_ref, gq_ref, gkv_ref, wuq_ref, wuk_ref, wuv_ref, q_ref, k_ref, v_ref):
        ta, tb1, tb2 = ta_ref[...], tb1_ref[...], tb2_ref[...]
        nq, _ = _rms(xp_ref[:, 0:Q_RANK], Q_RANK)
        nkv, _ = _rms(xp_ref[:, Q_RANK:Q_RANK + KV_RANK], KV_RANK)
        nkv = (nkv * gkv_ref[...]).astype(BF)
        qpre = _dot(nq * gq_ref[...], wuq_ref[...])
        kpre = jnp.dot(nkv, wuk_ref[...], preferred_element_type=F32)
        v_ref[...] = jnp.dot(nkv, wuv_ref[...], preferred_element_type=F32).astype(BF)
        kr = _rope(pltpu.roll(xp_ref[:, Q_RANK + KV_RANK:MLA_IN], NOPE, 1), ta, tb1, tb2)
        for h in range(N_HEADS):
            sl = slice(h * HB, (h + 1) * HB)
            q_ref[:, sl] = _rope(qpre[:, sl], ta, tb1, tb2).astype(BF)
            k_ref[:, sl] = (kpre[:, sl] + kr).astype(BF)

    tab = _rows(tm, HB)
    return _pcall(body, "mla_prep", (T // tm,),
                  [_rows(tm, MLA_IN), tab, tab, tab, _full((1, Q_RANK)), _full((1, KV_RANK)),
                   _full((Q_RANK, W)), _full((KV_RANK, W)), _full((KV_RANK, W))],
                  [_rows(tm, W)] * 3, [_sds((T, W), BF)] * 3)(xp, *tabs, g_q, g_kv, w_uq, w_uk, w_uv)


def _flash_fwd(q, k, v, tq, exchange=None):
    T = q.shape[0]
    scale = QK_DIM ** -0.5

    hp = _AH
    W = hp * HB

    def body(q_ref, k_ref, v_ref, o_ref, lse_ref):
        i = pl.program_id(1)

        def blk(j, carry, masked):
            st = pl.multiple_of(j * tq, tq)
            out = []
            for h in range(hp):
                ls = slice(h * HB, (h + 1) * HB)
                m, l, acc = carry[h]
                s = _dot_nt(q_ref[:, ls], k_ref[pl.ds(st, tq), ls]) * scale
                if masked:
                    r = lax.broadcasted_iota(jnp.int32, (tq, tq), 0)
                    c = lax.broadcasted_iota(jnp.int32, (tq, tq), 1)
                    s = jnp.where(c <= r, s, NEG)
                m2 = jnp.maximum(m, jnp.max(s, -1, keepdims=True))
                p = jnp.exp(s - m2)
                a = jnp.exp(m - m2)
                out.append((m2, a * l + jnp.sum(p, -1, keepdims=True), a * acc + _dot(p, v_ref[pl.ds(st, tq), ls])))
            return tuple(out)

        init = tuple((jnp.full((tq, 1), NEG, F32), jnp.zeros((tq, 1), F32), jnp.zeros((tq, HB), F32))
                     for _ in range(hp))
        carry = lax.fori_loop(0, i, lambda j, cr: blk(j, cr, False), init)
        res = blk(i, carry, True)
        for h in range(hp):
            ls = slice(h * HB, (h + 1) * HB)
            m, l, acc = res[h]
            o_ref[:, ls] = acc / l
            lse_ref[:, ls] = jnp.broadcast_to(m + jnp.log(l), (tq, HB))

    qs = pl.BlockSpec((tq, W), lambda h, i: (i, h))
    kvs = pl.BlockSpec((T, W), lambda h, i: (0, h))
    return _pcall(body, "flash_fwd", (N_HEADS // hp, T // tq), [qs, kvs, kvs], [qs, qs],
                  [_sds((T, N_HEADS * HB))] * 2, exchange=exchange)(q, k, v)


def _gates(hq, hf, lb):
    sig = _sigmoid(hf)
    f = lb + (1.0 - lb) * sig
    sq = _sigmoid(hq)
    return hq * sq, 1.0 - f, f, jnp.log(f), sig, sq


def _lower_bound(lbl_ref):
    l0, l1 = lbl_ref[0:1, :], lbl_ref[1:2, :]
    mx = jnp.maximum(l0, l1)
    e0, e1 = jnp.exp(l0 - mx), jnp.exp(l1 - mx)
    return e0 / (e0 + e1)


def _split3(x):
    hi = x.astype(BF)
    r1 = x - hi.astype(F32)
    mid = r1.astype(BF)
    lo = (r1 - mid.astype(F32)).astype(BF)
    return hi, mid, lo


def _tri_mm(tri, x):
    hi, mid, lo = _split3(x)
    mm = lambda t: jnp.dot(tri, t, preferred_element_type=F32)
    return mm(hi) + mm(mid) + mm(lo)


def _anchor_mask(i, row, col):
    return ((row >> 4) == i) & (col < SUB * i)


def _diag_mask(d, row, col):
    return (col == row - d) & ((row & (SUB - 1)) >= d)


def _intra(q, k, b, b_s, row, col):
    a = jnp.zeros((CHUNK, CHUNK), F32)
    for i in range(1, CHUNK // SUB):
        b0 = b_s[SUB * i - 1:SUB * i, :]
        qs = q * jnp.exp(jnp.minimum(b - b0, 0.0))
        ks = k * jnp.exp(jnp.minimum(b0 - b, 0.0))
        a = a + jnp.where(_anchor_mask(i, row, col), _dot_nt(qs, ks), 0.0)
    for d in range(SUB):
        ksh = pltpu.roll(k, d, 0) if d else k
        bsh = pltpu.roll(b, d, 0) if d else b
        e = jnp.exp(jnp.minimum(b - bsh, 0.0))
        val = jnp.sum(q * ksh * e, -1, keepdims=True)
        a = a + jnp.where(_diag_mask(d, row, col), val, 0.0)
    return a


def _intra_bwd(q, k, b, b_s, da, row, col):
    dq = jnp.zeros((CHUNK, G_DIM), F32)
    dk = jnp.zeros((CHUNK, G_DIM), F32)
    for i in range(1, CHUNK // SUB):
        b0 = b_s[SUB * i - 1:SUB * i, :]
        eq = jnp.exp(jnp.minimum(b - b0, 0.0))
        ek = jnp.exp(jnp.minimum(b0 - b, 0.0))
        dai = jnp.where(_anchor_mask(i, row, col), da, 0.0)
        dq = dq + _dot(dai, k * ek) * eq
        dk = dk + _dot_tn(dai, q * eq) * ek
    for d in range(SUB):
        ksh = pltpu.roll(k, d, 0) if d else k
        bsh = pltpu.roll(b, d, 0) if d else b
        e = jnp.exp(jnp.minimum(b - bsh, 0.0))
        g = jnp.sum(jnp.where(_diag_mask(d, row, col), da, 0.0), -1, keepdims=True) * e
        dq = dq + g * ksh
        cb = g * q
        dk = dk + (pltpu.roll(cb, CHUNK - d, 0) if d else cb)
    return dq, dk


def _hgrn_fwd(xp, lb_logits, g_hn, exchange=None):
    T = xp.shape[0]
    tb = min(_TB, T)
    ncb = tb // CHUNK
    hp = _HP
    W = hp * G_DIM

    def body(hq_ref, hf_ref, hi_ref, hg_ref, lbl_ref, ghn_ref, out_ref, oraw_ref, sall_ref, st_ref, b_s):
        lb_all = _lower_bound(lbl_ref)

        @pl.when(pl.program_id(1) == 0)
        def _():
            st_ref[...] = jnp.zeros(st_ref.shape, F32)

        row = lax.broadcasted_iota(jnp.int32, (CHUNK, CHUNK), 0)
        col = lax.broadcasted_iota(jnp.int32, (CHUNK, CHUNK), 1)
        tri = (col <= row).astype(BF)

        def chunk(c, carry):
            sl = pl.ds(pl.multiple_of(c * CHUNK, CHUNK), CHUNK)
            for h in range(hp):
                ls = slice(h * G_DIM, (h + 1) * G_DIM)
                q, k, _, lf, _, _ = _gates(hq_ref[sl, ls], hf_ref[sl, ls], lb_all[:, ls])
                v = hi_ref[sl, ls]
                b = _tri_mm(tri, lf)
                b_s[h] = b
                st = st_ref[h]
                sall_ref[c, h] = st
                o = _dot_nt(q * jnp.exp(b), st) + _dot(_intra(q, k, b, b_s.at[h], row, col), v)
                bl = b_s[h, CHUNK - 1:CHUNK, :]
                st_ref[h] = st * jnp.exp(bl) + _dot_tn(v, k * jnp.exp(bl - b))
                oraw_ref[sl, ls] = o
                n, _ = _rms(o, G_DIM)
                hg = hg_ref[sl, ls]
                out_ref[sl, ls] = n * ghn_ref[:, ls] * (hg * _sigmoid(hg))
            return carry

        lax.fori_loop(0, ncb, chunk, 0)

    col_blk = lambda j: pl.BlockSpec((tb, W), lambda p, t: (t, j * (G_HEADS // hp) + p))
    head = pl.BlockSpec((tb, W), lambda p, t: (t, p))
    return _pcall(
        body, "hgrn_fwd", (G_HEADS // hp, T // tb),
        [col_blk(0), col_blk(1), col_blk(2), col_blk(3),
         pl.BlockSpec((2, W), lambda p, t: (0, p)), pl.BlockSpec((1, W), lambda p, t: (0, p))],
        [head, head, pl.BlockSpec((ncb, hp, G_DIM, G_DIM), lambda p, t: (t, p, 0, 0))],
        [_sds((T, G_W)), _sds((T, G_W)), _sds((T // CHUNK, G_HEADS, G_DIM, G_DIM))],
        scratch=[pltpu.VMEM((hp, G_DIM, G_DIM), F32), pltpu.VMEM((hp, CHUNK, G_DIM), F32)], exchange=exchange,
    )(xp, xp, xp, xp, lb_logits, g_hn)


def _fwd_out(o_pad, o_hgrn, x, g_on, w_out, g_post, g_fpre, tm):
    T = x.shape[0]

    def body(o_ref, oh_ref, x_ref, gon_ref, w_ref, gpost_ref, gfpre_ref, h1_ref, y1_ref, z_ref, mix_ref):
        for h in range(N_HEADS):
            sl = slice(h * HB, (h + 1) * HB)
            n, _ = _rms(o_ref[:, sl], V_DIM)
            mix_ref[:, sl] = (n * gon_ref[:, sl]).astype(BF)
        mix_ref[:, N_HEADS * HB:MIX_W] = oh_ref[...].astype(BF)
        y1 = jnp.dot(mix_ref[...], w_ref[...], preferred_element_type=F32)
        y1_ref[...] = y1
        ny, _ = _rms(y1, D)
        h1 = x_ref[...] + ny * gpost_ref[...]
        h1_ref[...] = h1
        nh, _ = _rms(h1, D)
        z_ref[...] = (nh * gfpre_ref[...]).astype(BF)

    return _pcall(body, "fwd_out", (T // tm,),
                  [_rows(tm, N_HEADS * HB), _rows(tm, G_W), _rows(tm, D), _full((1, N_HEADS * HB)),
                   _full((MIX_W, D)), _full((1, D)), _full((1, D))],
                  [_rows(tm, D), _rows(tm, D), _rows(tm, D), _rows(tm, MIX_W)],
                  [_sds((T, D)), _sds((T, D)), _sds((T, D), BF), _sds((T, MIX_W), BF)],
                  )(o_pad, o_hgrn, x, g_on, w_out, g_post, g_fpre)


def _ffn_fwd(z, wg, wu, wd, h1, tgt, g_fpost, tm):
    T = z.shape[0]
    nf = wg.shape[1] // _FB

    def body(z_ref, wg_ref, wu_ref, wd_ref, h1_ref, t_ref, gp_ref,
             as_ref, bs_ref, ff_ref, dh2_ref, dy2_ref, dgp_ref, loss_ref, acc):
        i, j = pl.program_id(0), pl.program_id(1)
        zv = z_ref[...]
        g = jnp.dot(zv, wg_ref[...], preferred_element_type=F32)
        u = jnp.dot(zv, wu_ref[...], preferred_element_type=F32)
        s = _sigmoid(g)
        b = g * s
        ff = (b * u).astype(BF)
        as_ref[...] = (u * _dsilu(g, s)).astype(BF)
        bs_ref[...] = b.astype(BF)
        ff_ref[...] = ff
        part = jnp.dot(ff, wd_ref[...], preferred_element_type=F32)

        @pl.when(j == 0)
        def _():
            acc[...] = part

        @pl.when(j > 0)
        def _():
            acc[...] += part

        @pl.when((i == 0) & (j == 0))
        def _():
            dgp_ref[...] = jnp.zeros(dgp_ref.shape, F32)
            loss_ref[...] = jnp.zeros(loss_ref.shape, F32)

        @pl.when(j == nf - 1)
        def _():
            ny, r = _rms(acc[...], D)
            err = h1_ref[...] + ny * gp_ref[...] - t_ref[...]
            loss_ref[...] += 0.5 * jnp.sum(jnp.sum(err * err, -1, keepdims=True) * (1.0 / D), 0, keepdims=True)
            dh2 = err * (1.0 / D)
            dh2_ref[...] = dh2
            dy2, dgp = _rms_bwd(ny, r, gp_ref[...], dh2, D)
            dy2_ref[...] = dy2.astype(BF)
            dgp_ref[...] += dgp

    tok = lambda n: pl.BlockSpec((tm, n), lambda i, j: (i, 0))
    col = pl.BlockSpec((tm, _FB), lambda i, j: (i, j))
    return _pcall(
        body, "ffn_fwd", (T // tm, nf),
        [tok(D), pl.BlockSpec((D, _FB), lambda i, j: (0, j)), pl.BlockSpec((D, _FB), lambda i, j: (0, j)),
         pl.BlockSpec((_FB, D), lambda i, j: (j, 0)), tok(D), tok(D), _full((1, D))],
        [col, col, col, tok(D), tok(D), _full((1, D)), _full((1, HB))],
        [_sds((T, nf * _FB), BF)] * 3 + [_sds((T, D)), _sds((T, D), BF), _sds((1, D)), _sds((1, HB))],
        scratch=[pltpu.VMEM((tm, D), F32)],
    )(z, wg, wu, wd, h1, tgt, g_fpost)


def _dsilu(x, s):
    return s * (1.0 + x * (1.0 - s))


def _ffn_bwd_x(dy2, gs, us, wg, wu, wd, h1, y1, dh2, g_fpre, g_post, tm):
    T = dy2.shape[0]
    nf = wg.shape[1] // _FB

    def body(dy2_ref, gs_ref, us_ref, wg_ref, wu_ref, wd_ref, h1_ref, y1_ref, dh2_ref, gf_ref, gp_ref,
             dgs_ref, dus_ref, dh1_ref, dy1_ref, dgf_ref, dgp_ref, acc):
        i, j = pl.program_id(0), pl.program_id(1)
        dff = _dot_nt(dy2_ref[...], wd_ref[...])
        dg = (dff * gs_ref[...].astype(F32)).astype(BF)
        du = (dff * us_ref[...].astype(F32)).astype(BF)
        dgs_ref[...] = dg
        dus_ref[...] = du
        part = _dot_nt(dg, wg_ref[...]) + _dot_nt(du, wu_ref[...])

        @pl.when(j == 0)
        def _():
            acc[...] = part

        @pl.when(j > 0)
        def _():
            acc[...] += part

        @pl.when((i == 0) & (j == 0))
        def _():
            dgf_ref[...] = jnp.zeros(dgf_ref.shape, F32)
            dgp_ref[...] = jnp.zeros(dgp_ref.shape, F32)

        @pl.when(j == nf - 1)
        def _():
            nh, rh = _rms(h1_ref[...], D)
            dh, dgf = _rms_bwd(nh, rh, gf_ref[...], acc[...], D)
            dh1 = dh2_ref[...] + dh
            dh1_ref[...] = dh1
            dgf_ref[...] += dgf
            ny, ry = _rms(y1_ref[...], D)
            dy1, dgp = _rms_bwd(ny, ry, gp_ref[...], dh1, D)
            dy1_ref[...] = dy1.astype(BF)
            dgp_ref[...] += dgp

    tok = lambda n: pl.BlockSpec((tm, n), lambda i, j: (i, 0))
    col = pl.BlockSpec((tm, _FB), lambda i, j: (i, j))
    wcol = pl.BlockSpec((D, _FB), lambda i, j: (0, j))
    return _pcall(
        body, "ffn_bwd_x", (T // tm, nf),
        [tok(D), col, col, wcol, wcol, pl.BlockSpec((_FB, D), lambda i, j: (j, 0)),
         tok(D), tok(D), tok(D), _full((1, D)), _full((1, D))],
        [col, col, tok(D), tok(D), _full((1, D)), _full((1, D))],
        [_sds((T, nf * _FB), BF), _sds((T, nf * _FB), BF), _sds((T, D)), _sds((T, D), BF),
         _sds((1, D)), _sds((1, D))],
        scratch=[pltpu.VMEM((tm, D), F32)],
    )(dy2, gs, us, wg, wu, wd, h1, y1, dh2, g_fpre, g_post)


def _ffn_bwd_w(z, ffs, dgs, dus, dy2, tm):
    T = z.shape[0]
    nf = ffs.shape[1] // _FB
    nt = T // tm

    def body(z_ref, ff_ref, dgs_ref, dus_ref, dy2_ref, dwg_ref, dwu_ref, dwd_ref, ag, au, ad):
        i = pl.program_id(1)
        zt = z_ref[...].T
        pg = jnp.dot(zt, dgs_ref[...], preferred_element_type=F32)
        pu = jnp.dot(zt, dus_ref[...], preferred_element_type=F32)
        pd = _dot_tn(ff_ref[...], dy2_ref[...])

        @pl.when(i == 0)
        def _():
            ag[...] = pg
            au[...] = pu
            ad[...] = pd

        @pl.when(i > 0)
        def _():
            ag[...] += pg
            au[...] += pu
            ad[...] += pd

        @pl.when(i == nt - 1)
        def _():
            dwg_ref[...] = ag[...].astype(BF)
            dwu_ref[...] = au[...].astype(BF)
            dwd_ref[...] = ad[...].astype(BF)

    F = nf * _FB
    tok = lambda n: pl.BlockSpec((tm, n), lambda j, i: (i, 0))
    col = pl.BlockSpec((tm, _FB), lambda j, i: (i, j))
    wcol = pl.BlockSpec((D, _FB), lambda j, i: (0, j))
    wrow = pl.BlockSpec((_FB, D), lambda j, i: (j, 0))
    return _pcall(
        body, "ffn_bwd_w", (nf, nt), [tok(D), col, col, col, tok(D)], [wcol, wcol, wrow],
        [_sds((D, F), BF), _sds((D, F), BF), _sds((F, D), BF)],
        scratch=[pltpu.VMEM((D, _FB), F32), pltpu.VMEM((D, _FB), F32), pltpu.VMEM((_FB, D), F32)],
    )(z, ffs, dgs, dus, dy2)


def _out_bwd(dy1, mix, o_pad, w_out, g_on, tm):
    T = dy1.shape[0]
    W = N_HEADS * HB

    def body(dy1_ref, mix_ref, o_ref, w_ref, gon_ref, do_ref, dl_ref, dohg_ref, dw_ref, dgon_ref):
        i = pl.program_id(0)
        dy1v = dy1_ref[...]
        dmix = _dot_nt(dy1v, w_ref[...])
        pw = _dot_tn(mix_ref[...], dy1v)

        @pl.when(i == 0)
        def _():
            dw_ref[...] = pw
            dgon_ref[...] = jnp.zeros(dgon_ref.shape, F32)

        @pl.when(i > 0)
        def _():
            dw_ref[...] += pw

        for h in range(N_HEADS):
            sl = slice(h * HB, (h + 1) * HB)
            ov = o_ref[:, sl]
            n, r = _rms(ov, V_DIM)
            do, dg = _rms_bwd(n, r, gon_ref[:, sl], dmix[:, sl], V_DIM)
            dgon_ref[:, sl] += dg
            do_ref[:, sl] = do.astype(BF)
            dl_ref[:, sl] = jnp.broadcast_to(jnp.sum(do * ov, -1, keepdims=True), (tm, HB))
        dohg_ref[...] = dmix[:, W:MIX_W]

    return _pcall(body, "out_bwd", (T // tm,),
                  [_rows(tm, D), _rows(tm, MIX_W), _rows(tm, W), _full((MIX_W, D)), _full((1, W))],
                  [_rows(tm, W), _rows(tm, W), _rows(tm, G_W), _full((MIX_W, D)), _full((1, W))],
                  [_sds((T, W), BF), _sds((T, W)), _sds((T, G_W)), _sds((MIX_W, D)), _sds((1, W))],
                  )(dy1, mix, o_pad, w_out, g_on)


def _flash_bwd(q, k, v, do, lse, dl, tq, exchange=None):
    T = q.shape[0]
    nq = T // tq
    scale = QK_DIM ** -0.5
    hp = _AH
    W = hp * HB

    def body(k_ref, v_ref, q_ref, do_ref, lse_ref, dl_ref, dk_ref, dv_ref, dq_ref):
        j = pl.program_id(1)

        @pl.when(j == 0)
        def _():
            dq_ref[...] = jnp.zeros(dq_ref.shape, F32)

        def blk(i, carry, masked):
            sl = pl.ds(pl.multiple_of(i * tq, tq), tq)
            out = []
            for h in range(hp):
                ls = slice(h * HB, (h + 1) * HB)
                dk, dv = carry[h]
                kv, vv = k_ref[:, ls], v_ref[:, ls]
                qv, dov = q_ref[sl, ls], do_ref[sl, ls]
                s = _dot_nt(qv, kv) * scale
                if masked:
                    r = lax.broadcasted_iota(jnp.int32, (tq, tq), 0)
                    c = lax.broadcasted_iota(jnp.int32, (tq, tq), 1)
                    s = jnp.where(c <= r, s, NEG)
                p = jnp.exp(s - lse_ref[sl, h * HB:h * HB + 1])
                ds = p * (_dot_nt(dov, vv) - dl_ref[sl, h * HB:h * HB + 1]) * scale
                dq_ref[sl, ls] += _dot(ds, kv)
                out.append((dk + _dot_tn(ds, qv), dv + _dot_tn(p, dov)))
            return tuple(out)

        zero = jnp.zeros((tq, HB), F32)
        carry = blk(j, tuple((zero, zero) for _ in range(hp)), True)
        res = lax.fori_loop(j + 1, nq, lambda i, cr: blk(i, cr, False), carry)
        for h in range(hp):
            ls = slice(h * HB, (h + 1) * HB)
            dk_ref[:, ls] = res[h][0]
            dv_ref[:, ls] = res[h][1]

    tile = pl.BlockSpec((tq, W), lambda h, j: (j, h))
    whole = pl.BlockSpec((T, W), lambda h, j: (0, h))
    return _pcall(body, "flash_bwd", (N_HEADS // hp, nq), [tile, tile, whole, whole, whole, whole],
                  [tile, tile, whole], [_sds((T, N_HEADS * HB))] * 3, exchange=exchange)(k, v, q, do, lse, dl)


def _mla_prep_bwd(xp, tabs, dq, dk, dv, g_q, g_kv, w_uq, w_uk, w_uv, tm):
    T = xp.shape[0]
    W = N_HEADS * HB

    def body(xp_ref, ta_ref, tb1_ref, tb2_ref, dq_ref, dk_ref, dv_ref, gq_ref, gkv_ref, wuq_ref, wuk_ref, wuv_ref,
             dxp_ref, dwuq_ref, dwuk_ref, dwuv_ref, dgq_ref, dgkv_ref, dqp):
        i = pl.program_id(0)
        ta, tb1, tb2 = ta_ref[...], tb1_ref[...], tb2_ref[...]
        nq, rq = _rms(xp_ref[:, 0:Q_RANK], Q_RANK)
        nkv, rkv = _rms(xp_ref[:, Q_RANK:Q_RANK + KV_RANK], KV_RANK)
        dkr = jnp.zeros((tm, HB), F32)
        for h in range(N_HEADS):
            sl = slice(h * HB, (h + 1) * HB)
            dqp[:, sl] = _unrope(dq_ref[:, sl], ta, tb1, tb2).astype(BF)
            dkr = dkr + dk_ref[:, sl]
        dkr = pltpu.roll(_unrope(dkr, ta, tb1, tb2), HB - NOPE, 1)
        lane = lax.broadcasted_iota(jnp.int32, (tm, HB), 1)
        dxp_ref[:, Q_RANK + KV_RANK:MLA_IN] = jnp.where(lane < ROPE, dkr, 0.0)
        dqpv = dqp[...]
        dkv, dvv = dk_ref[...].astype(BF), dv_ref[...].astype(BF)
        nqs = (nq * gq_ref[...]).astype(BF)
        nkvs = (nkv * gkv_ref[...]).astype(BF)
        pq, pk, pv = _dot_tn(nqs, dqpv), _dot_tn(nkvs, dkv), _dot_tn(nkvs, dvv)
        dcq, dgq = _rms_bwd(nq, rq, gq_ref[...], _dot_nt(dqpv, wuq_ref[...]), Q_RANK)
        dckv, dgkv = _rms_bwd(nkv, rkv, gkv_ref[...], _dot_nt(dkv, wuk_ref[...]) + _dot_nt(dvv, wuv_ref[...]), KV_RANK)
        dxp_ref[:, 0:Q_RANK] = dcq
        dxp_ref[:, Q_RANK:Q_RANK + KV_RANK] = dckv

        @pl.when(i == 0)
        def _():
            dwuq_ref[...] = pq
            dwuk_ref[...] = pk
            dwuv_ref[...] = pv
            dgq_ref[...] = dgq
            dgkv_ref[...] = dgkv

        @pl.when(i > 0)
        def _():
            dwuq_ref[...] += pq
            dwuk_ref[...] += pk
            dwuv_ref[...] += pv
            dgq_ref[...] += dgq
            dgkv_ref[...] += dgkv

    tab = _rows(tm, HB)
    return _pcall(
        body, "mla_prep_bwd", (T // tm,),
        [_rows(tm, MLA_IN), tab, tab, tab, _rows(tm, W), _rows(tm, W), _rows(tm, W), _full((1, Q_RANK)),
         _full((1, KV_RANK)), _full((Q_RANK, W)), _full((KV_RANK, W)), _full((KV_RANK, W))],
        [_rows(tm, MLA_IN), _full((Q_RANK, W)), _full((KV_RANK, W)), _full((KV_RANK, W)), _full((1, Q_RANK)),
         _full((1, KV_RANK))],
        [_sds((T, MLA_IN)), _sds((Q_RANK, W)), _sds((KV_RANK, W)), _sds((KV_RANK, W)), _sds((1, Q_RANK)),
         _sds((1, KV_RANK))],
        scratch=[pltpu.VMEM((tm, W), BF)],
    )(xp, *tabs, dq, dk, dv, g_q, g_kv, w_uq, w_uk, w_uv)


def _hgrn_bwd(xp, o_raw, s_all, d_out, lb_logits, g_hn, exchange=None):
    T = xp.shape[0]
    tb = min(_TB, T)
    ncb = tb // CHUNK
    nb = T // tb
    hp = _HP
    W = hp * G_DIM

    def body(hq_ref, hf_ref, hi_ref, hg_ref, o_ref, sall_ref, dout_ref, lbl_ref, ghn_ref,
             dhq_ref, dhf_ref, dhi_ref, dhg_ref, dlbl_ref, dghn_ref, dst_ref, b_s, acc_lb, acc_g):
        t = pl.program_id(1)
        lb_all = _lower_bound(lbl_ref)

        @pl.when(t == 0)
        def _():
            dst_ref[...] = jnp.zeros(dst_ref.shape, F32)
            acc_lb[...] = jnp.zeros(acc_lb.shape, F32)
            acc_g[...] = jnp.zeros(acc_g.shape, F32)

        row = lax.broadcasted_iota(jnp.int32, (CHUNK, CHUNK), 0)
        col = lax.broadcasted_iota(jnp.int32, (CHUNK, CHUNK), 1)
        tri = (col <= row).astype(BF)
        tri_t = (col >= row).astype(BF)
        last = lax.broadcasted_iota(jnp.int32, (CHUNK, G_DIM), 0) == CHUNK - 1

        def chunk(cc, carry):
            c = ncb - 1 - cc
            sl = pl.ds(pl.multiple_of(c * CHUNK, CHUNK), CHUNK)
            for h in range(hp):
                ls = slice(h * G_DIM, (h + 1) * G_DIM)
                lb, ghn = lb_all[:, ls], ghn_ref[:, ls]
                hq, hg = hq_ref[sl, ls], hg_ref[sl, ls]
                q, k, f, lf, sig, sq = _gates(hq, hf_ref[sl, ls], lb)
                v = hi_ref[sl, ls]
                b = _tri_mm(tri, lf)
                b_s[h] = b
                st = sall_ref[c, h]
                dstn = dst_ref[h]
                o = o_ref[sl, ls]
                dout = dout_ref[sl, ls]
                n, r = _rms(o, G_DIM)
                sg = _sigmoid(hg)
                dhg_ref[sl, ls] = dout * (n * ghn) * _dsilu(hg, sg)
                do, dg = _rms_bwd(n, r, ghn, dout * (hg * sg), G_DIM)
                acc_g[:, ls] += dg
                eb = jnp.exp(b)
                bl = b_s[h, CHUNK - 1:CHUNK, :]
                ebl = jnp.exp(bl)
                ekd = jnp.exp(bl - b)
                kd = k * ekd
                a = _intra(q, k, b, b_s.at[h], row, col)
                dq_i, dk_i = _intra_bwd(q, k, b, b_s.at[h], _dot_nt(do, v), row, col)
                dhi_ref[sl, ls] = _dot_tn(a, do) + _dot_nt(kd, dstn)
                dk_state = _dot(v, dstn) * ekd
                dq = dq_i + _dot(do, st) * eb
                dk = dk_i + dk_state
                dbl = jnp.sum(k * dk_state, 0, keepdims=True) + ebl * jnp.sum(dstn * st, 0, keepdims=True)
                db = q * dq - k * dk + jnp.where(last, dbl, 0.0)
                df = _tri_mm(tri_t, db) / f - dk
                dhf_ref[sl, ls] = df * (1.0 - lb) * sig * (1.0 - sig)
                acc_lb[:, ls] += jnp.sum(df * (1.0 - sig), 0, keepdims=True)
                dhq_ref[sl, ls] = dq * _dsilu(hq, sq)
                dst_ref[h] = dstn * ebl + _dot_tn(do, q * eb)
            return carry

        lax.fori_loop(0, ncb, chunk, 0)

        @pl.when(t == nb - 1)
        def _():
            dl0 = acc_lb[...] * lb_all * (1.0 - lb_all)
            dlbl_ref[0:1, :] = dl0
            dlbl_ref[1:2, :] = -dl0
            dghn_ref[...] = acc_g[...]

    col_blk = lambda j: pl.BlockSpec((tb, W), lambda p, t: (nb - 1 - t, j * (G_HEADS // hp) + p))
    head = pl.BlockSpec((tb, W), lambda p, t: (nb - 1 - t, p))
    two = pl.BlockSpec((2, W), lambda p, t: (0, p))
    one = pl.BlockSpec((1, W), lambda p, t: (0, p))
    res = _pcall(
        body, "hgrn_bwd", (G_HEADS // hp, nb),
        [col_blk(0), col_blk(1), col_blk(2), col_blk(3), head,
         pl.BlockSpec((ncb, hp, G_DIM, G_DIM), lambda p, t: (nb - 1 - t, p, 0, 0)), head, two, one],
        [head, head, head, head, two, one],
        [_sds((T, G_W))] * 4 + [_sds((2, G_W)), _sds((1, G_W))],
        scratch=[pltpu.VMEM((hp, G_DIM, G_DIM), F32), pltpu.VMEM((hp, CHUNK, G_DIM), F32),
                 pltpu.VMEM((1, W), F32), pltpu.VMEM((1, W), F32)], exchange=exchange,
    )(xp, xp, xp, xp, o_raw, s_all, d_out, lb_logits, g_hn)
    return res


def _in_bwd_x(x, dxp_m, dxp_h, dh1, w_in_al, g_pre, tm):
    T = x.shape[0]

    def body(x_ref, dm_ref, d0_ref, d1_ref, d2_ref, d3_ref, dh1_ref, w_ref, g_ref, dx_ref, dg_ref):
        i = pl.program_id(0)
        du = _dot_nt(dm_ref[...], w_ref[:, 0:MLA_IN])
        for j, d_ref in enumerate((d0_ref, d1_ref, d2_ref, d3_ref)):
            du = du + _dot_nt(d_ref[...], w_ref[:, MLA_IN + j * G_W:MLA_IN + (j + 1) * G_W])
        nx, r = _rms(x_ref[...], D)
        dx, dg = _rms_bwd(nx, r, g_ref[...], du, D)
        dx_ref[...] = dh1_ref[...] + dx

        @pl.when(i == 0)
        def _():
            dg_ref[...] = dg

        @pl.when(i > 0)
        def _():
            dg_ref[...] += dg

    return _pcall(body, "in_bwd_x", (T // tm,),
                  [_rows(tm, D), _rows(tm, MLA_IN)] + [_rows(tm, G_W)] * 4 + [_rows(tm, D), _full((D, XP_W)), _full((1, D))],
                  [_rows(tm, D), _full((1, D))], [_sds((T, D)), _sds((1, D))],
                  )(x, dxp_m, *dxp_h, dh1, w_in_al, g_pre)


def _aligned_col(c):
    return jnp.where(c < Q_RANK + KV_RANK + ROPE, c, c + (KR_PAD - ROPE))


def _align_w_in(g_in):
    tile = 384
    kr_end = Q_RANK + KV_RANK + ROPE

    def body(g_ref, o_ref, gp):
        gp[...] = jnp.zeros(gp.shape, BF)
        for j in range(N_DEV):
            gp[j, :, 0:IN_SH] = g_ref[j]
        r = lax.broadcasted_iota(jnp.int32, (tile, tile), 0)
        c = lax.broadcasted_iota(jnp.int32, (tile, tile), 1)
        for t in range(XP_W // tile):
            lo, hi = t * tile, (t + 1) * tile
            cols = [a if a < kr_end else a - (KR_PAD - ROPE) for a in (lo, hi - 1)]
            acc = jnp.zeros((D, tile), F32)
            for j in range(cols[0] // IN_SH, cols[-1] // IN_SH + 1):
                sel = (r < IN_SH) & (_aligned_col(j * IN_SH + r) == lo + c)
                acc = acc + jnp.dot(gp[j], sel.astype(BF), preferred_element_type=F32)
            o_ref[:, lo:hi] = acc.astype(BF)

    vm = pl.BlockSpec(memory_space=pltpu.VMEM)
    return pl.pallas_call(
        body, name="align_w_in", in_specs=[vm], out_specs=vm, out_shape=_sds((D, XP_W), BF),
        scratch_shapes=[pltpu.VMEM((N_DEV, D, tile), BF)],
        compiler_params=pltpu.CompilerParams(vmem_limit_bytes=_VMEM_LIMIT))(g_in)


def _in_bwd_w(u, dxp_m, dxp_h, tm):
    T = u.shape[0]
    nt = T // tm
    win = 640

    def body(u_ref, dm_ref, d0_ref, d1_ref, d2_ref, d3_ref, o_ref, acc):
        i = pl.program_id(0)
        ut = u_ref[...].T
        parts = [(0, MLA_IN, dm_ref)] + [(MLA_IN + j * G_W, G_W, d) for j, d in enumerate((d0_ref, d1_ref, d2_ref, d3_ref))]

        @pl.when(i == 0)
        def _():
            for lo, n, d in parts:
                acc[:, lo:lo + n] = jnp.dot(ut, d[...].astype(BF), preferred_element_type=F32)

        @pl.when(i > 0)
        def _():
            for lo, n, d in parts:
                acc[:, lo:lo + n] += jnp.dot(ut, d[...].astype(BF), preferred_element_type=F32)

        @pl.when(i == nt - 1)
        def _():
            wide = 384
            r = lax.broadcasted_iota(jnp.int32, (win, wide), 0)
            c = lax.broadcasted_iota(jnp.int32, (win, wide), 1)
            kr_end = Q_RANK + KV_RANK + ROPE
            for j in range(N_DEV):
                first = j * IN_SH if j * IN_SH < kr_end else j * IN_SH + (KR_PAD - ROPE)
                lo = min(first // HB * HB, XP_W - win)
                sel = (c < IN_SH) & (_aligned_col(j * IN_SH + c) == lo + r)
                res = jnp.dot(acc[:, lo:lo + win].astype(BF), sel.astype(BF), preferred_element_type=F32)
                o_ref[j] = res[:, 0:IN_SH].astype(BF)

    return _pcall(body, "in_bwd_w", (nt,),
                  [_rows(tm, D), _rows(tm, MLA_IN)] + [_rows(tm, G_W)] * 4,
                  [_full((N_DEV, D, IN_SH))], [_sds((N_DEV, D, IN_SH), BF)],
                  scratch=[pltpu.VMEM((D, XP_W), F32)])(u, dxp_m, *dxp_h)[0]


def _pad_heads(w, width, real):
    lead = w.shape[:-1]
    w = w.reshape(lead + (N_HEADS, real))
    w = jnp.pad(w, [(0, 0)] * len(lead) + [(0, 0), (0, width - real)])
    return w.reshape(lead + (N_HEADS * width,))


def _unpad_heads(w, width, real):
    lead = w.shape[:-1]
    return w.reshape(lead + (N_HEADS, width))[..., :real].reshape(lead + (N_HEADS * real,))


def _rope_tables(positions):
    half = ROPE // 2
    inv_freq = 1.0 / (ROPE_THETA ** (jnp.arange(0, ROPE, 2, dtype=F32) / ROPE))
    ang = positions.astype(F32)[:, None] * inv_freq
    cos, sin = jnp.cos(ang), jnp.sin(ang)
    T = positions.shape[0]
    z = lambda n: jnp.zeros((T, n), F32)
    ta = jnp.concatenate([jnp.ones((T, NOPE), F32), cos, cos, z(HB - QK_DIM)], 1)
    tb1 = jnp.concatenate([z(NOPE), -sin, z(half), z(HB - QK_DIM)], 1)
    tb2 = jnp.concatenate([z(NOPE), z(half), sin, z(HB - QK_DIM)], 1)
    return ta, tb1, tb2


def kernel(x, positions, attn_pre_norm, w_in, mla_q_norm, mla_w_uq, mla_kv_norm, mla_w_ukv, mla_out_norm, hgrn_lb_logits, hgrn_out_norm, w_out, attn_post_norm, ffn_pre_norm, w_gate, w_up, w_down, ffn_post_norm, loss_target, m_attn_pre_norm, m_w_in, m_mla_q_norm, m_mla_w_uq, m_mla_kv_norm, m_mla_w_ukv, m_mla_out_norm, m_hgrn_lb_logits, m_hgrn_out_norm, m_w_out, m_attn_post_norm, m_ffn_pre_norm, m_w_gate, m_w_up, m_w_down, m_ffn_post_norm, v_attn_pre_norm, v_w_in, v_mla_q_norm, v_mla_w_uq, v_mla_kv_norm, v_mla_w_ukv, v_mla_out_norm, v_hgrn_lb_logits, v_hgrn_out_norm, v_w_out, v_attn_post_norm, v_ffn_pre_norm, v_w_gate, v_w_up, v_w_down, v_ffn_post_norm):
    T = x.shape[1]
    tm = min(_TM, T)
    tq = min(_TQ, T)
    xs, tgt = x[0], loss_target[0]
    uq_sh = (Q_RANK // N_DEV, N_HEADS * QK_DIM)

    b_in, b_uq, b_out, b_g, b_u, b_d = _cast_shards(
        w_in[0], mla_w_uq[0].reshape(uq_sh), w_out[0], w_gate[0], w_up[0], w_down[0])
    g_in, g_uq = _exchange_call("ag_first", [GATHER, GATHER], [b_in, b_uq])
    w_in_al = _align_w_in(g_in)
    w_uq_p = _pad_heads(g_uq.reshape(Q_RANK, N_HEADS * QK_DIM), HB, QK_DIM)
    w_ukv = mla_w_ukv[0].astype(BF)
    w_uk_p = _pad_heads(w_ukv[..., :NOPE].reshape(KV_RANK, N_HEADS * NOPE), HB, NOPE)
    w_uv_p = _pad_heads(w_ukv[..., NOPE:].reshape(KV_RANK, N_HEADS * V_DIM), HB, V_DIM)
    g_on_p = _pad_heads(mla_out_norm, HB, V_DIM)
    tabs = _rope_tables(positions[0])

    xp_m, xp_h, u = _fwd_in(xs, attn_pre_norm, w_in_al, tm)
    q_att, k_att, v_att = _mla_prep(xp_m, tabs, mla_q_norm, mla_kv_norm, w_uq_p, w_uk_p, w_uv_p, tm)
    o_hgrn, o_raw, s_all, g_out, wd = _hgrn_fwd(xp_h, hgrn_lb_logits, hgrn_out_norm, ([GATHER, GATHER], [b_out, b_d]))
    wd = wd.reshape(N_DEV * FF_PAD, D)
    o_pad, lse, wg, wu = _flash_fwd(q_att, k_att, v_att, tq, ([GATHER_COLS, GATHER_COLS], [b_g, b_u]))
    w_out_full = g_out.reshape(D, D)
    w_out_mla = jnp.pad(w_out_full[:N_HEADS * V_DIM].reshape(N_HEADS, V_DIM, D), ((0, 0), (0, HB - V_DIM), (0, 0)))
    w_out_p = jnp.concatenate([w_out_mla.reshape(N_HEADS * HB, D), w_out_full[N_HEADS * V_DIM:]], 0)
    h1, y1, z, mix = _fwd_out(o_pad, o_hgrn, xs, g_on_p, w_out_p, attn_post_norm, ffn_pre_norm, tm)
    gs, us, ffs, dh2, dy2, d_fpost, loss_row = _ffn_fwd(z, wg, wu, wd, h1, tgt, ffn_post_norm, tm)

    dgs, dus, dh1, dy1, d_fpre, d_post = _ffn_bwd_x(dy2, gs, us, wg, wu, wd, h1, y1, dh2, ffn_pre_norm, attn_post_norm, tm)
    dwg, dwu, dwd = _ffn_bwd_w(z, ffs, dgs, dus, dy2, tm)
    do_pad, dl, d_ohg, dw_out_p, d_on_p = _out_bwd(dy1, mix, o_pad, w_out_p, g_on_p, tm)
    dw_out_mla = dw_out_p[:N_HEADS * HB].reshape(N_HEADS, HB, D)[:, :V_DIM].reshape(N_HEADS * V_DIM, D)
    dw_out = jnp.concatenate([dw_out_mla, dw_out_p[N_HEADS * HB:]], 0).reshape(N_DEV, D // N_DEV, D).astype(BF)
    dk_att, dv_att, dq_att, p_g, p_u, p_d, p_out = _flash_bwd(
        q_att, k_att, v_att, do_pad, lse, dl, tq,
        ([SCATTER_COLS, SCATTER_COLS, SCATTER, SCATTER], [dwg, dwu, dwd.reshape(N_DEV, FF_PAD, D), dw_out]))
    dxp_m, dw_uq_p, dw_uk_p, dw_uv_p, d_gq, d_gkv = _mla_prep_bwd(
        xp_m, tabs, dq_att, dk_att, dv_att, mla_q_norm, mla_kv_norm, w_uq_p, w_uk_p, w_uv_p, tm)
    dw_uq = _unpad_heads(dw_uq_p, HB, QK_DIM).reshape((N_DEV,) + uq_sh).astype(BF)
    dw_ukv = jnp.concatenate([_unpad_heads(dw_uk_p, HB, NOPE).reshape(KV_RANK, N_HEADS, NOPE),
                              _unpad_heads(dw_uv_p, HB, V_DIM).reshape(KV_RANK, N_HEADS, V_DIM)], -1)
    *dxp_h, d_lbl, d_ghn, p_uq, dw_ukv_all = _hgrn_bwd(
        xp_h, o_raw, s_all, d_ohg, hgrn_lb_logits, hgrn_out_norm,
        ([SCATTER, GATHER], [dw_uq, dw_ukv.reshape(KV_RANK, N_HEADS * HB)]))
    grad_x, d_pre = _in_bwd_x(xs, dxp_m, dxp_h, dh1, w_in_al, attn_pre_norm, tm)
    dw_in = _in_bwd_w(u, dxp_m, dxp_h, tm)
    d_on = _unpad_heads(d_on_p, HB, V_DIM)

    ukv2 = lambda a: a.reshape(KV_RANK, N_HEADS * HB)
    vecs = [d_pre, d_gq, d_gkv, d_on, d_lbl, d_ghn, d_post, d_fpre, d_fpost, loss_row]
    small_w = [attn_pre_norm, mla_q_norm, mla_kv_norm, ukv2(mla_w_ukv), mla_out_norm, hgrn_lb_logits, hgrn_out_norm,
               attn_post_norm, ffn_pre_norm, ffn_post_norm]
    small_m = [m_attn_pre_norm, m_mla_q_norm, m_mla_kv_norm, ukv2(m_mla_w_ukv), m_mla_out_norm, m_hgrn_lb_logits,
               m_hgrn_out_norm, m_attn_post_norm, m_ffn_pre_norm, m_ffn_post_norm]
    small_v = [v_attn_pre_norm, v_mla_q_norm, v_mla_kv_norm, ukv2(v_mla_w_ukv), v_mla_out_norm, v_hgrn_lb_logits,
               v_hgrn_out_norm, v_attn_post_norm, v_ffn_pre_norm, v_ffn_post_norm]
    rall, p_in = _final_exchange(vecs, dw_in)
    s_g, s_d, s_m, s_v, loss_all = _small_adam(rall, dw_ukv_all, 3, small_w, small_m, small_v)
    r_in = _shard_adam("adam_w_in", p_in, w_in[0], m_w_in[0], v_w_in[0], 256)
    r_uq = _shard_adam("adam_w_uq", p_uq, mla_w_uq[0].reshape(uq_sh), m_mla_w_uq[0].reshape(uq_sh),
                       v_mla_w_uq[0].reshape(uq_sh), uq_sh[0])
    r_out = _shard_adam("adam_w_out", p_out, w_out[0], m_w_out[0], v_w_out[0], D // N_DEV)
    r_g = _shard_adam("adam_w_gate", p_g, w_gate[0], m_w_gate[0], v_w_gate[0], 256)
    r_u = _shard_adam("adam_w_up", p_u, w_up[0], m_w_up[0], v_w_up[0], 256)
    r_d = _shard_adam("adam_w_down", p_d, w_down[0], m_w_down[0], v_w_down[0], FF_SH // 2)

    loss = loss_all[0, 0]

    def assemble(big, small):
        b_in, b_uq, b_out, b_g, b_u, b_d = big
        return [small[0], b_in[None], small[1], b_uq.reshape(mla_w_uq.shape), small[2],
                small[3].reshape(mla_w_ukv.shape), small[4], small[5], small[6], b_out[None], small[7], small[8],
                b_g[None], b_u[None], b_d[None], small[9]]

    outs = [loss, grad_x[None]]
    for idx, small in enumerate((s_g, s_d, s_m, s_v)):
        outs += assemble([r[idx] for r in (r_in, r_uq, r_out, r_g, r_u, r_d)], small)
    return tuple(outs)
```

```python
import jax
import jax.numpy as jnp
from jax import lax
from jax.experimental import pallas as pl
from jax.experimental.pallas import tpu as pltpu

BF = jnp.bfloat16
F32 = jnp.float32
MESH = pl.DeviceIdType.MESH

N_DEV = 8
D = 1024
EPS = 1e-6
ROPE_THETA = 10000.0
N_HEADS = 8
HB = 128
NOPE = 64
ROPE = 32
V_DIM = 64
QK_DIM = NOPE + ROPE
Q_RANK = 384
KV_RANK = 128
KR_PAD = 128
MLA_IN = Q_RANK + KV_RANK + KR_PAD
G_HEADS = 4
G_DIM = 128
G_W = G_HEADS * G_DIM
CHUNK = 64
SUB = 16
XP_W = MLA_IN + 4 * G_W
IN_SH = 324
IN_W = N_DEV * IN_SH
FF_SH = 352
FF_PAD = 384
MIX_W = N_HEADS * HB + G_W

ADAM_LR = 0.001
ADAM_B1 = 0.9
ADAM_B2 = 0.999
ADAM_EPS = 1e-08
ADAM_WD = 0.01
ADAM_STEP = 10

_TM = 512
_TQ = 512
_AH = 2
_FB = 768
_TB = 1024
_HP = 4
_VMEM_LIMIT = 56 * 1024 * 1024
NEG = -1e30


def _dot(a, b):
    return jnp.dot(a.astype(BF), b.astype(BF), preferred_element_type=F32)


def _dot_nt(a, b):
    return lax.dot_general(a.astype(BF), b.astype(BF), (((1,), (1,)), ((), ())), preferred_element_type=F32)


def _dot_tn(a, b):
    return lax.dot_general(a.astype(BF), b.astype(BF), (((0,), (0,)), ((), ())), preferred_element_type=F32)


def _sigmoid(x):
    return 1.0 / (1.0 + jnp.exp(-x))


def _rms(x, n):
    r = lax.rsqrt(jnp.sum(x * x, -1, keepdims=True) * (1.0 / n) + EPS)
    return x * r, r


def _rms_bwd(nx, r, g, dy, n):
    dg = jnp.sum(dy * nx, 0, keepdims=True)
    dn = dy * g
    dx = r * (dn - nx * (jnp.sum(dn * nx, -1, keepdims=True) * (1.0 / n)))
    return dx, dg


def _adamw(w, g, m, v):
    m2 = ADAM_B1 * m + (1.0 - ADAM_B1) * g
    v2 = ADAM_B2 * v + (1.0 - ADAM_B2) * (g * g)
    m_hat = m2 / (1.0 - ADAM_B1 ** ADAM_STEP)
    v_hat = v2 / (1.0 - ADAM_B2 ** ADAM_STEP)
    delta = -ADAM_LR * (m_hat / (jnp.sqrt(v_hat) + ADAM_EPS) + ADAM_WD * w)
    return delta, m2, v2


def _pcall(body, name, grid, in_specs, out_specs, out_shape, scratch=(), exchange=None):
    scratch = list(scratch)
    extra = ()
    if exchange is not None:
        kinds, extra = exchange
        in_specs, out_specs, out_shape = list(in_specs), list(out_specs), list(out_shape)
        n_in, n_out, n_scr, n_x = len(in_specs), len(out_specs), len(scratch), len(extra)
        inner = body

        def body(*refs):
            ins, rest = refs[:n_in], refs[n_in:]
            x_src, rest = rest[:n_x], rest[n_x:]
            outs, rest = rest[:n_out], rest[n_out:]
            x_dst, rest = rest[:n_x], rest[n_x:]
            ex = _Exchange(kinds, x_src, x_dst, *rest[n_scr:])
            first = pl.program_id(0) == 0
            last = pl.program_id(0) == grid[0] - 1
            for a in range(1, len(grid)):
                first = first & (pl.program_id(a) == 0)
                last = last & (pl.program_id(a) == grid[a] - 1)
            pl.when(first)(ex.start)
            inner(*ins, *outs, *rest[:n_scr])
            pl.when(last)(ex.wait)

        in_specs += [_HBM] * n_x
        out_specs += [_HBM] * n_x
        out_shape += _exchange_shapes(kinds, extra)
        scratch += _exchange_sems(n_x)
    call = pl.pallas_call(
        body, name=name, grid=grid, in_specs=in_specs, out_specs=out_specs, out_shape=out_shape,
        scratch_shapes=scratch,
        compiler_params=pltpu.CompilerParams(
            dimension_semantics=("arbitrary",) * len(grid), vmem_limit_bytes=_VMEM_LIMIT))
    return lambda *operands: call(*operands, *extra)


def _full(shape):
    return pl.BlockSpec(shape, lambda *_: (0,) * len(shape))


def _rows(tm, n):
    return pl.BlockSpec((tm, n), lambda i, *_: (i, 0))


def _sds(shape, dtype=F32):
    return jax.ShapeDtypeStruct(shape, dtype)


def _peer(k, x, y, c):
    px = 1 - x if (k >> 2) & 1 else x
    py = 1 - y if (k >> 1) & 1 else y
    pc = 1 - c if k & 1 else c
    return px, py, pc


GATHER, SCATTER = "gather", "scatter"


class _Exchange:
    def __init__(self, kinds, srcs, dsts, send_sems, recv_sems, loc_sems):
        self.kinds, self.srcs, self.dsts = kinds, srcs, dsts
        self.send_sems, self.recv_sems, self.loc_sems = send_sems, recv_sems, loc_sems
        self.x, self.y, self.c = lax.axis_index("x"), lax.axis_index("y"), lax.axis_index("c")
        self.me = 4 * self.x + 2 * self.y + self.c

    def _src(self, w, slot):
        return self.srcs[w] if self.kinds[w] == GATHER else self.srcs[w].at[slot]

    def _dst(self, w, slot):
        return self.dsts[w].at[slot]

    def _copy(self, w, k, outgoing):
        px, py, pc = _peer(k, self.x, self.y, self.c)
        pid = 4 * px + 2 * py + pc
        return pltpu.make_async_remote_copy(
            src_ref=self._src(w, pid if outgoing else self.me),
            dst_ref=self._dst(w, self.me if outgoing else pid),
            send_sem=self.send_sems.at[w, k - 1], recv_sem=self.recv_sems.at[w, k - 1],
            device_id=(px, py, pc), device_id_type=MESH)

    def _local(self, w):
        return pltpu.make_async_copy(self._src(w, self.me), self._dst(w, self.me), self.loc_sems.at[w])

    def start(self):
        for w in range(len(self.srcs)):
            self._local(w).start()
            for k in range(1, N_DEV):
                self._copy(w, k, True).start()

    def wait(self):
        for w in range(len(self.srcs)):
            self._local(w).wait()
            for k in range(1, N_DEV):
                self._copy(w, k, False).wait_recv()
        for w in range(len(self.srcs)):
            for k in range(1, N_DEV):
                self._copy(w, k, True).wait_send()


def _exchange_sems(n_w):
    return [pltpu.SemaphoreType.DMA((n_w, N_DEV - 1)), pltpu.SemaphoreType.DMA((n_w, N_DEV - 1)),
            pltpu.SemaphoreType.DMA((n_w,))]


def _exchange_shapes(kinds, srcs):
    return [_sds(((N_DEV,) if kd == GATHER else ()) + tuple(s.shape), s.dtype) for kd, s in zip(kinds, srcs)]


_HBM = pl.BlockSpec(memory_space=pl.ANY)


def _cast_shards(w_in, w_uq, w_out, w_gate_t, w_up_t, w_down):
    shapes = [(D, IN_SH), (Q_RANK // N_DEV, N_HEADS * QK_DIM), (D // N_DEV, D), (2 * FF_PAD, D), (FF_PAD, D)]

    def body(win, wuq, wout, wg, wu, wd, sin_, suq, sout, sgu, sd):
        sin_[...] = win[...].astype(BF)
        suq[...] = wuq[...].astype(BF)
        sout[...] = wout[...].astype(BF)
        sgu[...] = jnp.zeros(sgu.shape, BF)
        sgu[0:FF_SH, :] = wg[...].astype(BF)
        sgu[FF_PAD:FF_PAD + FF_SH, :] = wu[...].astype(BF)
        sd[...] = jnp.zeros(sd.shape, BF)
        sd[0:FF_SH, :] = wd[...].astype(BF)

    vm = pl.BlockSpec(memory_space=pltpu.VMEM)
    return pl.pallas_call(
        body, name="cast_shards", in_specs=[vm] * 6, out_specs=[vm] * 5,
        out_shape=[_sds(s, BF) for s in shapes],
        compiler_params=pltpu.CompilerParams(vmem_limit_bytes=_VMEM_LIMIT),
    )(w_in, w_uq, w_out, w_gate_t, w_up_t, w_down)


def _exchange_call(name, kinds, srcs):
    n_w = len(srcs)

    def body(*refs):
        ex = _Exchange(kinds, refs[:n_w], refs[n_w:2 * n_w], *refs[2 * n_w:])
        ex.start()
        ex.wait()

    return pl.pallas_call(
        body, name=name, in_specs=[_HBM] * n_w, out_specs=[_HBM] * n_w,
        out_shape=_exchange_shapes(kinds, srcs), scratch_shapes=_exchange_sems(n_w))(*srcs)


def _row_offsets(arrays):
    offs, rows = [], 0
    for a in arrays:
        offs.append(rows)
        rows += a.shape[0]
    return offs, -(-rows // 8) * 8


def _final_exchange(vecs):
    n_p = len(vecs)
    offs, rows = _row_offsets(vecs)

    def body(*refs):
        g_refs = refs[:n_p]
        rall, pk, send_sems, recv_sems, loc_sem = refs[n_p:]
        x, y, c = lax.axis_index("x"), lax.axis_index("y"), lax.axis_index("c")
        me = 4 * x + 2 * y + c
        pk[...] = jnp.zeros(pk.shape, F32)
        for p in range(n_p):
            r, n = g_refs[p].shape
            pk[offs[p]:offs[p] + r, 0:n] = g_refs[p][...]

        def remote(k):
            return pltpu.make_async_remote_copy(
                src_ref=pk, dst_ref=rall.at[me], send_sem=send_sems.at[k - 1], recv_sem=recv_sems.at[k - 1],
                device_id=_peer(k, x, y, c), device_id_type=MESH)

        def arrival(k):
            px, py, pc = _peer(k, x, y, c)
            return pltpu.make_async_remote_copy(
                src_ref=pk, dst_ref=rall.at[4 * px + 2 * py + pc], send_sem=send_sems.at[k - 1],
                recv_sem=recv_sems.at[k - 1], device_id=(px, py, pc), device_id_type=MESH)

        local = pltpu.make_async_copy(pk, rall.at[me], loc_sem)
        local.start()
        for k in range(1, N_DEV):
            remote(k).start()
        local.wait()
        for k in range(1, N_DEV):
            arrival(k).wait_recv()
        for k in range(1, N_DEV):
            remote(k).wait_send()

    vm = pl.BlockSpec(memory_space=pltpu.VMEM)
    return pl.pallas_call(
        body, name="final_exchange", in_specs=[vm] * n_p, out_specs=vm, out_shape=_sds((N_DEV, rows, D)),
        scratch_shapes=[pltpu.VMEM((rows, D), F32),
                        pltpu.SemaphoreType.DMA((N_DEV - 1,)), pltpu.SemaphoreType.DMA((N_DEV - 1,)),
                        pltpu.SemaphoreType.DMA],
    )(*vecs)


def _small_adam(rall, big_parts, big, ws, ms, vs):
    n_p = len(ws)
    packed = [w for p, w in enumerate(ws) if p != big] + [jax.ShapeDtypeStruct((1, HB), F32)]
    offs, _ = _row_offsets(packed)
    offs = offs[:big] + [None] + offs[big:]

    def total(ref, sl):
        g = ref[(0,) + sl]
        for j in range(1, N_DEV):
            g = g + ref[(j,) + sl]
        return g

    def body(*refs):
        rall_ref, big_ref = refs[:2]
        w_refs, m_refs, v_refs = refs[2:2 + n_p], refs[2 + n_p:2 + 2 * n_p], refs[2 + 2 * n_p:2 + 3 * n_p]
        outs = refs[2 + 3 * n_p:]
        for p in range(n_p):
            r, n = w_refs[p].shape
            if p == big:
                g = total(big_ref, (slice(0, r), slice(0, n)))
            else:
                g = total(rall_ref, (slice(offs[p], offs[p] + r), slice(0, n)))
            delta, m2, v2 = _adamw(w_refs[p][...], g, m_refs[p][...], v_refs[p][...])
            outs[p][...] = g
            outs[n_p + p][...] = delta
            outs[2 * n_p + p][...] = m2
            outs[3 * n_p + p][...] = v2
        outs[4 * n_p][...] = total(rall_ref, (slice(offs[n_p], offs[n_p] + 1), slice(0, HB)))

    vm = pl.BlockSpec(memory_space=pltpu.VMEM)
    res = pl.pallas_call(
        body, name="small_adam", in_specs=[vm] * (2 + 3 * n_p), out_specs=[vm] * (4 * n_p + 1),
        out_shape=[_sds(w.shape) for w in ws] * 4 + [_sds((1, HB))],
        compiler_params=pltpu.CompilerParams(vmem_limit_bytes=_VMEM_LIMIT),
    )(rall, big_parts, *ws, *ms, *vs)
    return res[:n_p], res[n_p:2 * n_p], res[2 * n_p:3 * n_p], res[3 * n_p:4 * n_p], res[4 * n_p]


def _shard_adam(name, parts, w, m, v, tr, row0=0):
    a0, b0 = w.shape
    b = parts.shape[2]
    blk0 = row0 // tr

    def body(p_ref, w_ref, m_ref, v_ref, g_out, d_out, m_out, v_out):
        g = p_ref[0].astype(F32)
        for j in range(1, N_DEV):
            g = g + p_ref[j].astype(F32)
        g = g[:, 0:b0]
        delta, m2, v2 = _adamw(w_ref[...], g, m_ref[...], v_ref[...])
        g_out[...] = g
        d_out[...] = delta
        m_out[...] = m2
        v_out[...] = v2

    blk = pl.BlockSpec((tr, b0), lambda i: (i, 0))
    return _pcall(
        body, name, (a0 // tr,),
        [pl.BlockSpec((N_DEV, tr, b), lambda i: (0, blk0 + i, 0)), blk, blk, blk],
        [blk] * 4, [_sds((a0, b0))] * 4)(parts, w, m, v)


def _fwd_in(x, g_pre, w_in_al, tm):
    T = x.shape[0]

    def body(x_ref, g_ref, w_ref, xm_ref, xh_ref, u_ref):
        nx, _ = _rms(x_ref[...], D)
        u = (nx * g_ref[...]).astype(BF)
        u_ref[...] = u
        xm_ref[...] = jnp.dot(u, w_ref[:, 0:MLA_IN], preferred_element_type=F32)
        xh_ref[...] = jnp.dot(u, w_ref[:, MLA_IN:XP_W], preferred_element_type=F32)

    return _pcall(body, "fwd_in", (T // tm,),
                  [_rows(tm, D), _full((1, D)), _full((D, XP_W))],
                  [_rows(tm, MLA_IN), _rows(tm, 4 * G_W), _rows(tm, D)],
                  [_sds((T, MLA_IN)), _sds((T, 4 * G_W)), _sds((T, D), BF)])(x, g_pre, w_in_al)


def _rope(blk, ta, tb1, tb2):
    return blk * ta + pltpu.roll(blk, HB - ROPE // 2, 1) * tb1 + pltpu.roll(blk, ROPE // 2, 1) * tb2


def _unrope(d, ta, tb1, tb2):
    return d * ta + pltpu.roll(d * tb1, ROPE // 2, 1) + pltpu.roll(d * tb2, HB - ROPE // 2, 1)


def _mla_prep(xp, tabs, g_q, g_kv, w_uq, w_uk, w_uv, tm):
    T = xp.shape[0]
    W = N_HEADS * HB

    def body(xp_ref, ta_ref, tb1_ref, tb2_ref, gq_ref, gkv_ref, wuq_ref, wuk_ref, wuv_ref, q_ref, k_ref, v_ref):
        ta, tb1, tb2 = ta_ref[...], tb1_ref[...], tb2_ref[...]
        nq, _ = _rms(xp_ref[:, 0:Q_RANK], Q_RANK)
        nkv, _ = _rms(xp_ref[:, Q_RANK:Q_RANK + KV_RANK], KV_RANK)
        nkv = (nkv * gkv_ref[...]).astype(BF)
        qpre = _dot(nq * gq_ref[...], wuq_ref[...])
        kpre = jnp.dot(nkv, wuk_ref[...], preferred_element_type=F32)
        v_ref[...] = jnp.dot(nkv, wuv_ref[...], preferred_element_type=F32).astype(BF)
        kr = _rope(pltpu.roll(xp_ref[:, Q_RANK + KV_RANK:MLA_IN], NOPE, 1), ta, tb1, tb2)
        for h in range(N_HEADS):
            sl = slice(h * HB, (h + 1) * HB)
            q_ref[:, sl] = _rope(qpre[:, sl], ta, tb1, tb2).astype(BF)
            k_ref[:, sl] = (kpre[:, sl] + kr).astype(BF)

    tab = _rows(tm, HB)
    return _pcall(body, "mla_prep", (T // tm,),
                  [_rows(tm, MLA_IN), tab, tab, tab, _full((1, Q_RANK)), _full((1, KV_RANK)),
                   _full((Q_RANK, W)), _full((KV_RANK, W)), _full((KV_RANK, W))],
                  [_rows(tm, W)] * 3, [_sds((T, W), BF)] * 3)(xp, *tabs, g_q, g_kv, w_uq, w_uk, w_uv)


def _flash_fwd(q, k, v, tq, exchange=None):
    T = q.shape[0]
    scale = QK_DIM ** -0.5

    hp = _AH
    W = hp * HB

    def body(q_ref, k_ref, v_ref, o_ref, lse_ref):
        i = pl.program_id(1)

        def blk(j, carry, masked):
            st = pl.multiple_of(j * tq, tq)
            out = []
            for h in range(hp):
                ls = slice(h * HB, (h + 1) * HB)
                m, l, acc = carry[h]
                s = _dot_nt(q_ref[:, ls], k_ref[pl.ds(st, tq), ls]) * scale
                if masked:
                    r = lax.broadcasted_iota(jnp.int32, (tq, tq), 0)
                    c = lax.broadcasted_iota(jnp.int32, (tq, tq), 1)
                    s = jnp.where(c <= r, s, NEG)
                m2 = jnp.maximum(m, jnp.max(s, -1, keepdims=True))
                p = jnp.exp(s - m2)
                a = jnp.exp(m - m2)
                out.append((m2, a * l + jnp.sum(p, -1, keepdims=True), a * acc + _dot(p, v_ref[pl.ds(st, tq), ls])))
            return tuple(out)

        init = tuple((jnp.full((tq, 1), NEG, F32), jnp.zeros((tq, 1), F32), jnp.zeros((tq, HB), F32))
                     for _ in range(hp))
        carry = lax.fori_loop(0, i, lambda j, cr: blk(j, cr, False), init)
        res = blk(i, carry, True)
        for h in range(hp):
            ls = slice(h * HB, (h + 1) * HB)
            m, l, acc = res[h]
            o_ref[:, ls] = acc / l
            lse_ref[:, ls] = jnp.broadcast_to(m + jnp.log(l), (tq, HB))

    qs = pl.BlockSpec((tq, W), lambda h, i: (i, h))
    kvs = pl.BlockSpec((T, W), lambda h, i: (0, h))
    return _pcall(body, "flash_fwd", (N_HEADS // hp, T // tq), [qs, kvs, kvs], [qs, qs],
                  [_sds((T, N_HEADS * HB))] * 2, exchange=exchange)(q, k, v)


def _gates(hq, hf, lb):
    sig = _sigmoid(hf)
    f = lb + (1.0 - lb) * sig
    sq = _sigmoid(hq)
    return hq * sq, 1.0 - f, f, jnp.log(f), sig, sq


def _lower_bound(lbl_ref):
    l0, l1 = lbl_ref[0:1, :], lbl_ref[1:2, :]
    mx = jnp.maximum(l0, l1)
    e0, e1 = jnp.exp(l0 - mx), jnp.exp(l1 - mx)
    return e0 / (e0 + e1)


def _split3(x):
    hi = x.astype(BF)
    r1 = x - hi.astype(F32)
    mid = r1.astype(BF)
    lo = (r1 - mid.astype(F32)).astype(BF)
    return hi, mid, lo


def _tri_mm(tri, x):
    hi, mid, lo = _split3(x)
    mm = lambda t: jnp.dot(tri, t, preferred_element_type=F32)
    return mm(hi) + mm(mid) + mm(lo)


def _anchor_mask(i, row, col):
    return ((row >> 4) == i) & (col < SUB * i)


def _diag_mask(d, row, col):
    return (col == row - d) & ((row & (SUB - 1)) >= d)


def _intra(q, k, b, b_s, row, col):
    a = jnp.zeros((CHUNK, CHUNK), F32)
    for i in range(1, CHUNK // SUB):
        b0 = b_s[SUB * i - 1:SUB * i, :]
        qs = q * jnp.exp(jnp.minimum(b - b0, 0.0))
        ks = k * jnp.exp(jnp.minimum(b0 - b, 0.0))
        a = a + jnp.where(_anchor_mask(i, row, col), _dot_nt(qs, ks), 0.0)
    for d in range(SUB):
        ksh = pltpu.roll(k, d, 0) if d else k
        bsh = pltpu.roll(b, d, 0) if d else b
        e = jnp.exp(jnp.minimum(b - bsh, 0.0))
        val = jnp.sum(q * ksh * e, -1, keepdims=True)
        a = a + jnp.where(_diag_mask(d, row, col), val, 0.0)
    return a


def _intra_bwd(q, k, b, b_s, da, row, col):
    dq = jnp.zeros((CHUNK, G_DIM), F32)
    dk = jnp.zeros((CHUNK, G_DIM), F32)
    for i in range(1, CHUNK // SUB):
        b0 = b_s[SUB * i - 1:SUB * i, :]
        eq = jnp.exp(jnp.minimum(b - b0, 0.0))
        ek = jnp.exp(jnp.minimum(b0 - b, 0.0))
        dai = jnp.where(_anchor_mask(i, row, col), da, 0.0)
        dq = dq + _dot(dai, k * ek) * eq
        dk = dk + _dot_tn(dai, q * eq) * ek
    for d in range(SUB):
        ksh = pltpu.roll(k, d, 0) if d else k
        bsh = pltpu.roll(b, d, 0) if d else b
        e = jnp.exp(jnp.minimum(b - bsh, 0.0))
        g = jnp.sum(jnp.where(_diag_mask(d, row, col), da, 0.0), -1, keepdims=True) * e
        dq = dq + g * ksh
        cb = g * q
        dk = dk + (pltpu.roll(cb, CHUNK - d, 0) if d else cb)
    return dq, dk


def _hgrn_fwd(xp, lb_logits, g_hn, exchange=None):
    T = xp.shape[0]
    tb = min(_TB, T)
    ncb = tb // CHUNK
    hp = _HP
    W = hp * G_DIM

    def body(hq_ref, hf_ref, hi_ref, hg_ref, lbl_ref, ghn_ref, out_ref, oraw_ref, sall_ref, st_ref, b_s):
        lb_all = _lower_bound(lbl_ref)

        @pl.when(pl.program_id(1) == 0)
        def _():
            st_ref[...] = jnp.zeros(st_ref.shape, F32)

        row = lax.broadcasted_iota(jnp.int32, (CHUNK, CHUNK), 0)
        col = lax.broadcasted_iota(jnp.int32, (CHUNK, CHUNK), 1)
        tri = (col <= row).astype(BF)

        def chunk(c, carry):
            sl = pl.ds(pl.multiple_of(c * CHUNK, CHUNK), CHUNK)
            for h in range(hp):
                ls = slice(h * G_DIM, (h + 1) * G_DIM)
                q, k, _, lf, _, _ = _gates(hq_ref[sl, ls], hf_ref[sl, ls], lb_all[:, ls])
                v = hi_ref[sl, ls]
                b = _tri_mm(tri, lf)
                b_s[h] = b
                st = st_ref[h]
                sall_ref[c, h] = st
                o = _dot_nt(q * jnp.exp(b), st) + _dot(_intra(q, k, b, b_s.at[h], row, col), v)
                bl = b_s[h, CHUNK - 1:CHUNK, :]
                st_ref[h] = st * jnp.exp(bl) + _dot_tn(v, k * jnp.exp(bl - b))
                oraw_ref[sl, ls] = o
                n, _ = _rms(o, G_DIM)
                hg = hg_ref[sl, ls]
                out_ref[sl, ls] = n * ghn_ref[:, ls] * (hg * _sigmoid(hg))
            return carry

        lax.fori_loop(0, ncb, chunk, 0)

    col_blk = lambda j: pl.BlockSpec((tb, W), lambda p, t: (t, j * (G_HEADS // hp) + p))
    head = pl.BlockSpec((tb, W), lambda p, t: (t, p))
    return _pcall(
        body, "hgrn_fwd", (G_HEADS // hp, T // tb),
        [col_blk(0), col_blk(1), col_blk(2), col_blk(3),
         pl.BlockSpec((2, W), lambda p, t: (0, p)), pl.BlockSpec((1, W), lambda p, t: (0, p))],
        [head, head, pl.BlockSpec((ncb, hp, G_DIM, G_DIM), lambda p, t: (t, p, 0, 0))],
        [_sds((T, G_W)), _sds((T, G_W)), _sds((T // CHUNK, G_HEADS, G_DIM, G_DIM))],
        scratch=[pltpu.VMEM((hp, G_DIM, G_DIM), F32), pltpu.VMEM((hp, CHUNK, G_DIM), F32)], exchange=exchange,
    )(xp, xp, xp, xp, lb_logits, g_hn)


def _fwd_out(o_pad, o_hgrn, x, g_on, w_out, g_post, g_fpre, tm):
    T = x.shape[0]

    def body(o_ref, oh_ref, x_ref, gon_ref, w_ref, gpost_ref, gfpre_ref, h1_ref, y1_ref, z_ref, mix_ref):
        for h in range(N_HEADS):
            sl = slice(h * HB, (h + 1) * HB)
            n, _ = _rms(o_ref[:, sl], V_DIM)
            mix_ref[:, sl] = (n * gon_ref[:, sl]).astype(BF)
        mix_ref[:, N_HEADS * HB:MIX_W] = oh_ref[...].astype(BF)
        y1 = jnp.dot(mix_ref[...], w_ref[...], preferred_element_type=F32)
        y1_ref[...] = y1
        ny, _ = _rms(y1, D)
        h1 = x_ref[...] + ny * gpost_ref[...]
        h1_ref[...] = h1
        nh, _ = _rms(h1, D)
        z_ref[...] = (nh * gfpre_ref[...]).astype(BF)

    return _pcall(body, "fwd_out", (T // tm,),
                  [_rows(tm, N_HEADS * HB), _rows(tm, G_W), _rows(tm, D), _full((1, N_HEADS * HB)),
                   _full((MIX_W, D)), _full((1, D)), _full((1, D))],
                  [_rows(tm, D), _rows(tm, D), _rows(tm, D), _rows(tm, MIX_W)],
                  [_sds((T, D)), _sds((T, D)), _sds((T, D), BF), _sds((T, MIX_W), BF)],
                  )(o_pad, o_hgrn, x, g_on, w_out, g_post, g_fpre)


def _ffn_fwd(z, wgu, wd, h1, tgt, g_fpost, tm):
    T = z.shape[0]
    nf = wd.shape[0] // _FB

    def body(z_ref, wgu_ref, wd_ref, h1_ref, t_ref, gp_ref,
             as_ref, bs_ref, ff_ref, dh2_ref, dy2_ref, dgp_ref, loss_ref, acc):
        i, j = pl.program_id(0), pl.program_id(1)
        gu = _dot_nt(z_ref[...], wgu_ref[...])
        g = jnp.concatenate([gu[:, 0:FF_PAD], gu[:, 2 * FF_PAD:3 * FF_PAD]], 1)
        u = jnp.concatenate([gu[:, FF_PAD:2 * FF_PAD], gu[:, 3 * FF_PAD:4 * FF_PAD]], 1)
        s = _sigmoid(g)
        b = g * s
        ff = (b * u).astype(BF)
        as_ref[...] = (u * _dsilu(g, s)).astype(BF)
        bs_ref[...] = b.astype(BF)
        ff_ref[...] = ff
        part = jnp.dot(ff, wd_ref[...], preferred_element_type=F32)

        @pl.when(j == 0)
        def _():
            acc[...] = part

        @pl.when(j > 0)
        def _():
            acc[...] += part

        @pl.when((i == 0) & (j == 0))
        def _():
            dgp_ref[...] = jnp.zeros(dgp_ref.shape, F32)
            loss_ref[...] = jnp.zeros(loss_ref.shape, F32)

        @pl.when(j == nf - 1)
        def _():
            ny, r = _rms(acc[...], D)
            err = h1_ref[...] + ny * gp_ref[...] - t_ref[...]
            loss_ref[...] += 0.5 * jnp.sum(jnp.sum(err * err, -1, keepdims=True) * (1.0 / D), 0, keepdims=True)
            dh2 = err * (1.0 / D)
            dh2_ref[...] = dh2
            dy2, dgp = _rms_bwd(ny, r, gp_ref[...], dh2, D)
            dy2_ref[...] = dy2.astype(BF)
            dgp_ref[...] += dgp

    tok = lambda n: pl.BlockSpec((tm, n), lambda i, j: (i, 0))
    col = pl.BlockSpec((tm, _FB), lambda i, j: (i, j))
    return _pcall(
        body, "ffn_fwd", (T // tm, nf),
        [tok(D), pl.BlockSpec((2 * _FB, D), lambda i, j: (j, 0)), pl.BlockSpec((_FB, D), lambda i, j: (j, 0)),
         tok(D), tok(D), _full((1, D))],
        [col, col, col, tok(D), tok(D), _full((1, D)), _full((1, HB))],
        [_sds((T, nf * _FB), BF)] * 3 + [_sds((T, D)), _sds((T, D), BF), _sds((1, D)), _sds((1, HB))],
        scratch=[pltpu.VMEM((tm, D), F32)],
    )(z, wgu, wd, h1, tgt, g_fpost)


def _dsilu(x, s):
    return s * (1.0 + x * (1.0 - s))


def _ffn_bwd_x(dy2, gs, us, wgu, wd, h1, y1, dh2, g_fpre, g_post, tm):
    T = dy2.shape[0]
    nf = wd.shape[0] // _FB

    def body(dy2_ref, gs_ref, us_ref, wgu_ref, wd_ref, h1_ref, y1_ref, dh2_ref, gf_ref, gp_ref,
             dgu_ref, dh1_ref, dy1_ref, dgf_ref, dgp_ref, acc):
        i, j = pl.program_id(0), pl.program_id(1)
        dff = _dot_nt(dy2_ref[...], wd_ref[...])
        dg = (dff * gs_ref[...].astype(F32)).astype(BF)
        du = (dff * us_ref[...].astype(F32)).astype(BF)
        dgu = jnp.concatenate([dg[:, 0:FF_PAD], du[:, 0:FF_PAD], dg[:, FF_PAD:_FB], du[:, FF_PAD:_FB]], 1)
        dgu_ref[...] = dgu
        part = jnp.dot(dgu, wgu_ref[...], preferred_element_type=F32)

        @pl.when(j == 0)
        def _():
            acc[...] = part

        @pl.when(j > 0)
        def _():
            acc[...] += part

        @pl.when((i == 0) & (j == 0))
        def _():
            dgf_ref[...] = jnp.zeros(dgf_ref.shape, F32)
            dgp_ref[...] = jnp.zeros(dgp_ref.shape, F32)

        @pl.when(j == nf - 1)
        def _():
            nh, rh = _rms(h1_ref[...], D)
            dh, dgf = _rms_bwd(nh, rh, gf_ref[...], acc[...], D)
            dh1 = dh2_ref[...] + dh
            dh1_ref[...] = dh1
            dgf_ref[...] += dgf
            ny, ry = _rms(y1_ref[...], D)
            dy1, dgp = _rms_bwd(ny, ry, gp_ref[...], dh1, D)
            dy1_ref[...] = dy1.astype(BF)
            dgp_ref[...] += dgp

    tok = lambda n: pl.BlockSpec((tm, n), lambda i, j: (i, 0))
    col = pl.BlockSpec((tm, _FB), lambda i, j: (i, j))
    return _pcall(
        body, "ffn_bwd_x", (T // tm, nf),
        [tok(D), col, col, pl.BlockSpec((2 * _FB, D), lambda i, j: (j, 0)), pl.BlockSpec((_FB, D), lambda i, j: (j, 0)),
         tok(D), tok(D), tok(D), _full((1, D)), _full((1, D))],
        [pl.BlockSpec((tm, 2 * _FB), lambda i, j: (i, j)), tok(D), tok(D), _full((1, D)), _full((1, D))],
        [_sds((T, 2 * nf * _FB), BF), _sds((T, D)), _sds((T, D), BF), _sds((1, D)), _sds((1, D))],
        scratch=[pltpu.VMEM((tm, D), F32)],
    )(dy2, gs, us, wgu, wd, h1, y1, dh2, g_fpre, g_post)


def _ffn_bwd_w(z, ffs, dgu, dy2, tm):
    T = z.shape[0]
    nf = ffs.shape[1] // _FB
    nt = T // tm

    def body(z_ref, ff_ref, dgu_ref, dy2_ref, dwgu_ref, dwd_ref, agu, ad):
        i = pl.program_id(1)
        pgu = _dot_tn(dgu_ref[...], z_ref[...])
        pd = _dot_tn(ff_ref[...], dy2_ref[...])

        @pl.when(i == 0)
        def _():
            agu[...] = pgu
            ad[...] = pd

        @pl.when(i > 0)
        def _():
            agu[...] += pgu
            ad[...] += pd

        @pl.when(i == nt - 1)
        def _():
            dwgu_ref[...] = agu[...].astype(BF)
            dwd_ref[...] = ad[...].astype(BF)

    F = nf * _FB
    tok = lambda n: pl.BlockSpec((tm, n), lambda j, i: (i, 0))
    return _pcall(
        body, "ffn_bwd_w", (nf, nt),
        [tok(D), pl.BlockSpec((tm, _FB), lambda j, i: (i, j)), pl.BlockSpec((tm, 2 * _FB), lambda j, i: (i, j)), tok(D)],
        [pl.BlockSpec((2 * _FB, D), lambda j, i: (j, 0)), pl.BlockSpec((_FB, D), lambda j, i: (j, 0))],
        [_sds((2 * F, D), BF), _sds((F, D), BF)],
        scratch=[pltpu.VMEM((2 * _FB, D), F32), pltpu.VMEM((_FB, D), F32)],
    )(z, ffs, dgu, dy2)


def _out_bwd(dy1, mix, o_pad, w_out, g_on, tm):
    T = dy1.shape[0]
    W = N_HEADS * HB

    def body(dy1_ref, mix_ref, o_ref, w_ref, gon_ref, do_ref, dl_ref, dohg_ref, dw_ref, dgon_ref):
        i = pl.program_id(0)
        dy1v = dy1_ref[...]
        dmix = _dot_nt(dy1v, w_ref[...])
        pw = _dot_tn(mix_ref[...], dy1v)

        @pl.when(i == 0)
        def _():
            dw_ref[...] = pw
            dgon_ref[...] = jnp.zeros(dgon_ref.shape, F32)

        @pl.when(i > 0)
        def _():
            dw_ref[...] += pw

        for h in range(N_HEADS):
            sl = slice(h * HB, (h + 1) * HB)
            ov = o_ref[:, sl]
            n, r = _rms(ov, V_DIM)
            do, dg = _rms_bwd(n, r, gon_ref[:, sl], dmix[:, sl], V_DIM)
            dgon_ref[:, sl] += dg
            do_ref[:, sl] = do.astype(BF)
            dl_ref[:, sl] = jnp.broadcast_to(jnp.sum(do * ov, -1, keepdims=True), (tm, HB))
        dohg_ref[...] = dmix[:, W:MIX_W]

    return _pcall(body, "out_bwd", (T // tm,),
                  [_rows(tm, D), _rows(tm, MIX_W), _rows(tm, W), _full((MIX_W, D)), _full((1, W))],
                  [_rows(tm, W), _rows(tm, W), _rows(tm, G_W), _full((MIX_W, D)), _full((1, W))],
                  [_sds((T, W), BF), _sds((T, W)), _sds((T, G_W)), _sds((MIX_W, D)), _sds((1, W))],
                  )(dy1, mix, o_pad, w_out, g_on)


def _flash_bwd(q, k, v, do, lse, dl, tq, exchange=None):
    T = q.shape[0]
    nq = T // tq
    scale = QK_DIM ** -0.5
    hp = _AH
    W = hp * HB

    def body(k_ref, v_ref, q_ref, do_ref, lse_ref, dl_ref, dk_ref, dv_ref, dq_ref):
        j = pl.program_id(1)

        @pl.when(j == 0)
        def _():
            dq_ref[...] = jnp.zeros(dq_ref.shape, F32)

        def blk(i, carry, masked):
            sl = pl.ds(pl.multiple_of(i * tq, tq), tq)
            out = []
            for h in range(hp):
                ls = slice(h * HB, (h + 1) * HB)
                dk, dv = carry[h]
                kv, vv = k_ref[:, ls], v_ref[:, ls]
                qv, dov = q_ref[sl, ls], do_ref[sl, ls]
                s = _dot_nt(qv, kv) * scale
                if masked:
                    r = lax.broadcasted_iota(jnp.int32, (tq, tq), 0)
                    c = lax.broadcasted_iota(jnp.int32, (tq, tq), 1)
                    s = jnp.where(c <= r, s, NEG)
                p = jnp.exp(s - lse_ref[sl, h * HB:h * HB + 1])
                ds = p * (_dot_nt(dov, vv) - dl_ref[sl, h * HB:h * HB + 1]) * scale
                dq_ref[sl, ls] += _dot(ds, kv)
                out.append((dk + _dot_tn(ds, qv), dv + _dot_tn(p, dov)))
            return tuple(out)

        zero = jnp.zeros((tq, HB), F32)
        carry = blk(j, tuple((zero, zero) for _ in range(hp)), True)
        res = lax.fori_loop(j + 1, nq, lambda i, cr: blk(i, cr, False), carry)
        for h in range(hp):
            ls = slice(h * HB, (h + 1) * HB)
            dk_ref[:, ls] = res[h][0]
            dv_ref[:, ls] = res[h][1]

    tile = pl.BlockSpec((tq, W), lambda h, j: (j, h))
    whole = pl.BlockSpec((T, W), lambda h, j: (0, h))
    return _pcall(body, "flash_bwd", (N_HEADS // hp, nq), [tile, tile, whole, whole, whole, whole],
                  [tile, tile, whole], [_sds((T, N_HEADS * HB))] * 3, exchange=exchange)(k, v, q, do, lse, dl)


def _mla_prep_bwd(xp, tabs, dq, dk, dv, g_q, g_kv, w_uq, w_uk, w_uv, tm):
    T = xp.shape[0]
    W = N_HEADS * HB

    def body(xp_ref, ta_ref, tb1_ref, tb2_ref, dq_ref, dk_ref, dv_ref, gq_ref, gkv_ref, wuq_ref, wuk_ref, wuv_ref,
             dxp_ref, dwuq_ref, dwuk_ref, dwuv_ref, dgq_ref, dgkv_ref, dqp):
        i = pl.program_id(0)
        ta, tb1, tb2 = ta_ref[...], tb1_ref[...], tb2_ref[...]
        nq, rq = _rms(xp_ref[:, 0:Q_RANK], Q_RANK)
        nkv, rkv = _rms(xp_ref[:, Q_RANK:Q_RANK + KV_RANK], KV_RANK)
        dkr = jnp.zeros((tm, HB), F32)
        for h in range(N_HEADS):
            sl = slice(h * HB, (h + 1) * HB)
            dqp[:, sl] = _unrope(dq_ref[:, sl], ta, tb1, tb2).astype(BF)
            dkr = dkr + dk_ref[:, sl]
        dkr = pltpu.roll(_unrope(dkr, ta, tb1, tb2), HB - NOPE, 1)
        lane = lax.broadcasted_iota(jnp.int32, (tm, HB), 1)
        dxp_ref[:, Q_RANK + KV_RANK:MLA_IN] = jnp.where(lane < ROPE, dkr, 0.0)
        dqpv = dqp[...]
        dkv, dvv = dk_ref[...].astype(BF), dv_ref[...].astype(BF)
        nqs = (nq * gq_ref[...]).astype(BF)
        nkvs = (nkv * gkv_ref[...]).astype(BF)
        pq, pk, pv = _dot_tn(nqs, dqpv), _dot_tn(nkvs, dkv), _dot_tn(nkvs, dvv)
        dcq, dgq = _rms_bwd(nq, rq, gq_ref[...], _dot_nt(dqpv, wuq_ref[...]), Q_RANK)
        dckv, dgkv = _rms_bwd(nkv, rkv, gkv_ref[...], _dot_nt(dkv, wuk_ref[...]) + _dot_nt(dvv, wuv_ref[...]), KV_RANK)
        dxp_ref[:, 0:Q_RANK] = dcq
        dxp_ref[:, Q_RANK:Q_RANK + KV_RANK] = dckv

        @pl.when(i == 0)
        def _():
            dwuq_ref[...] = pq
            dwuk_ref[...] = pk
            dwuv_ref[...] = pv
            dgq_ref[...] = dgq
            dgkv_ref[...] = dgkv

        @pl.when(i > 0)
        def _():
            dwuq_ref[...] += pq
            dwuk_ref[...] += pk
            dwuv_ref[...] += pv
            dgq_ref[...] += dgq
            dgkv_ref[...] += dgkv

    tab = _rows(tm, HB)
    return _pcall(
        body, "mla_prep_bwd", (T // tm,),
        [_rows(tm, MLA_IN), tab, tab, tab, _rows(tm, W), _rows(tm, W), _rows(tm, W), _full((1, Q_RANK)),
         _full((1, KV_RANK)), _full((Q_RANK, W)), _full((KV_RANK, W)), _full((KV_RANK, W))],
        [_rows(tm, MLA_IN), _full((Q_RANK, W)), _full((KV_RANK, W)), _full((KV_RANK, W)), _full((1, Q_RANK)),
         _full((1, KV_RANK))],
        [_sds((T, MLA_IN)), _sds((Q_RANK, W)), _sds((KV_RANK, W)), _sds((KV_RANK, W)), _sds((1, Q_RANK)),
         _sds((1, KV_RANK))],
        scratch=[pltpu.VMEM((tm, W), BF)],
    )(xp, *tabs, dq, dk, dv, g_q, g_kv, w_uq, w_uk, w_uv)


def _hgrn_bwd(xp, o_raw, s_all, d_out, lb_logits, g_hn, exchange=None):
    T = xp.shape[0]
    tb = min(_TB, T)
    ncb = tb // CHUNK
    nb = T // tb
    hp = _HP
    W = hp * G_DIM

    def body(hq_ref, hf_ref, hi_ref, hg_ref, o_ref, sall_ref, dout_ref, lbl_ref, ghn_ref,
             dhq_ref, dhf_ref, dhi_ref, dhg_ref, dlbl_ref, dghn_ref, dst_ref, b_s, acc_lb, acc_g):
        t = pl.program_id(1)
        lb_all = _lower_bound(lbl_ref)

        @pl.when(t == 0)
        def _():
            dst_ref[...] = jnp.zeros(dst_ref.shape, F32)
            acc_lb[...] = jnp.zeros(acc_lb.shape, F32)
            acc_g[...] = jnp.zeros(acc_g.shape, F32)

        row = lax.broadcasted_iota(jnp.int32, (CHUNK, CHUNK), 0)
        col = lax.broadcasted_iota(jnp.int32, (CHUNK, CHUNK), 1)
        tri = (col <= row).astype(BF)
        tri_t = (col >= row).astype(BF)
        last = lax.broadcasted_iota(jnp.int32, (CHUNK, G_DIM), 0) == CHUNK - 1

        def chunk(cc, carry):
            c = ncb - 1 - cc
            sl = pl.ds(pl.multiple_of(c * CHUNK, CHUNK), CHUNK)
            for h in range(hp):
                ls = slice(h * G_DIM, (h + 1) * G_DIM)
                lb, ghn = lb_all[:, ls], ghn_ref[:, ls]
                hq, hg = hq_ref[sl, ls], hg_ref[sl, ls]
                q, k, f, lf, sig, sq = _gates(hq, hf_ref[sl, ls], lb)
                v = hi_ref[sl, ls]
                b = _tri_mm(tri, lf)
                b_s[h] = b
                st = sall_ref[c, h]
                dstn = dst_ref[h]
                o = o_ref[sl, ls]
                dout = dout_ref[sl, ls]
                n, r = _rms(o, G_DIM)
                sg = _sigmoid(hg)
                dhg_ref[sl, ls] = dout * (n * ghn) * _dsilu(hg, sg)
                do, dg = _rms_bwd(n, r, ghn, dout * (hg * sg), G_DIM)
                acc_g[:, ls] += dg
                eb = jnp.exp(b)
                bl = b_s[h, CHUNK - 1:CHUNK, :]
                ebl = jnp.exp(bl)
                ekd = jnp.exp(bl - b)
                kd = k * ekd
                a = _intra(q, k, b, b_s.at[h], row, col)
                dq_i, dk_i = _intra_bwd(q, k, b, b_s.at[h], _dot_nt(do, v), row, col)
                dhi_ref[sl, ls] = _dot_tn(a, do) + _dot_nt(kd, dstn)
                dk_state = _dot(v, dstn) * ekd
                dq = dq_i + _dot(do, st) * eb
                dk = dk_i + dk_state
                dbl = jnp.sum(k * dk_state, 0, keepdims=True) + ebl * jnp.sum(dstn * st, 0, keepdims=True)
                db = q * dq - k * dk + jnp.where(last, dbl, 0.0)
                df = _tri_mm(tri_t, db) / f - dk
                dhf_ref[sl, ls] = df * (1.0 - lb) * sig * (1.0 - sig)
                acc_lb[:, ls] += jnp.sum(df * (1.0 - sig), 0, keepdims=True)
                dhq_ref[sl, ls] = dq * _dsilu(hq, sq)
                dst_ref[h] = dstn * ebl + _dot_tn(do, q * eb)
            return carry

        lax.fori_loop(0, ncb, chunk, 0)

        @pl.when(t == nb - 1)
        def _():
            dl0 = acc_lb[...] * lb_all * (1.0 - lb_all)
            dlbl_ref[0:1, :] = dl0
            dlbl_ref[1:2, :] = -dl0
            dghn_ref[...] = acc_g[...]

    col_blk = lambda j: pl.BlockSpec((tb, W), lambda p, t: (nb - 1 - t, j * (G_HEADS // hp) + p))
    head = pl.BlockSpec((tb, W), lambda p, t: (nb - 1 - t, p))
    two = pl.BlockSpec((2, W), lambda p, t: (0, p))
    one = pl.BlockSpec((1, W), lambda p, t: (0, p))
    res = _pcall(
        body, "hgrn_bwd", (G_HEADS // hp, nb),
        [col_blk(0), col_blk(1), col_blk(2), col_blk(3), head,
         pl.BlockSpec((ncb, hp, G_DIM, G_DIM), lambda p, t: (nb - 1 - t, p, 0, 0)), head, two, one],
        [head, head, head, head, two, one],
        [_sds((T, G_W))] * 4 + [_sds((2, G_W)), _sds((1, G_W))],
        scratch=[pltpu.VMEM((hp, G_DIM, G_DIM), F32), pltpu.VMEM((hp, CHUNK, G_DIM), F32),
                 pltpu.VMEM((1, W), F32), pltpu.VMEM((1, W), F32)], exchange=exchange,
    )(xp, xp, xp, xp, o_raw, s_all, d_out, lb_logits, g_hn)
    return res


def _in_bwd_x(x, dxp_m, dxp_h, dh1, w_in_al, g_pre, tm, exchange=None):
    T = x.shape[0]

    def body(x_ref, dm_ref, d0_ref, d1_ref, d2_ref, d3_ref, dh1_ref, w_ref, g_ref, dx_ref, dg_ref):
        i = pl.program_id(0)
        du = _dot_nt(dm_ref[...], w_ref[:, 0:MLA_IN])
        for j, d_ref in enumerate((d0_ref, d1_ref, d2_ref, d3_ref)):
            du = du + _dot_nt(d_ref[...], w_ref[:, MLA_IN + j * G_W:MLA_IN + (j + 1) * G_W])
        nx, r = _rms(x_ref[...], D)
        dx, dg = _rms_bwd(nx, r, g_ref[...], du, D)
        dx_ref[...] = dh1_ref[...] + dx

        @pl.when(i == 0)
        def _():
            dg_ref[...] = dg

        @pl.when(i > 0)
        def _():
            dg_ref[...] += dg

    return _pcall(body, "in_bwd_x", (T // tm,),
                  [_rows(tm, D), _rows(tm, MLA_IN)] + [_rows(tm, G_W)] * 4 + [_rows(tm, D), _full((D, XP_W)), _full((1, D))],
                  [_rows(tm, D), _full((1, D))], [_sds((T, D)), _sds((1, D))], exchange=exchange,
                  )(x, dxp_m, *dxp_h, dh1, w_in_al, g_pre)


def _aligned_col(c):
    return jnp.where(c < Q_RANK + KV_RANK + ROPE, c, c + (KR_PAD - ROPE))


def _align_w_in(g_in):
    tile = 384
    kr_end = Q_RANK + KV_RANK + ROPE

    def body(g_ref, o_ref, gp):
        gp[...] = jnp.zeros(gp.shape, BF)
        for j in range(N_DEV):
            gp[j, :, 0:IN_SH] = g_ref[j]
        r = lax.broadcasted_iota(jnp.int32, (tile, tile), 0)
        c = lax.broadcasted_iota(jnp.int32, (tile, tile), 1)
        for t in range(XP_W // tile):
            lo, hi = t * tile, (t + 1) * tile
            cols = [a if a < kr_end else a - (KR_PAD - ROPE) for a in (lo, hi - 1)]
            acc = jnp.zeros((D, tile), F32)
            for j in range(cols[0] // IN_SH, cols[-1] // IN_SH + 1):
                sel = (r < IN_SH) & (_aligned_col(j * IN_SH + r) == lo + c)
                acc = acc + jnp.dot(gp[j], sel.astype(BF), preferred_element_type=F32)
            o_ref[:, lo:hi] = acc.astype(BF)

    vm = pl.BlockSpec(memory_space=pltpu.VMEM)
    return pl.pallas_call(
        body, name="align_w_in", in_specs=[vm], out_specs=vm, out_shape=_sds((D, XP_W), BF),
        scratch_shapes=[pltpu.VMEM((N_DEV, D, tile), BF)],
        compiler_params=pltpu.CompilerParams(vmem_limit_bytes=_VMEM_LIMIT))(g_in)


def _in_bwd_w(u, dxp_m, dxp_h, tm):
    T = u.shape[0]
    nt = T // tm
    win = 640

    def body(u_ref, dm_ref, d0_ref, d1_ref, d2_ref, d3_ref, o_ref, acc):
        i = pl.program_id(0)
        ut = u_ref[...].T
        parts = [(0, MLA_IN, dm_ref)] + [(MLA_IN + j * G_W, G_W, d) for j, d in enumerate((d0_ref, d1_ref, d2_ref, d3_ref))]

        @pl.when(i == 0)
        def _():
            for lo, n, d in parts:
                acc[:, lo:lo + n] = jnp.dot(ut, d[...].astype(BF), preferred_element_type=F32)

        @pl.when(i > 0)
        def _():
            for lo, n, d in parts:
                acc[:, lo:lo + n] += jnp.dot(ut, d[...].astype(BF), preferred_element_type=F32)

        @pl.when(i == nt - 1)
        def _():
            wide = 384
            r = lax.broadcasted_iota(jnp.int32, (win, wide), 0)
            c = lax.broadcasted_iota(jnp.int32, (win, wide), 1)
            kr_end = Q_RANK + KV_RANK + ROPE
            for j in range(N_DEV):
                first = j * IN_SH if j * IN_SH < kr_end else j * IN_SH + (KR_PAD - ROPE)
                lo = min(first // HB * HB, XP_W - win)
                sel = (c < IN_SH) & (_aligned_col(j * IN_SH + c) == lo + r)
                res = jnp.dot(acc[:, lo:lo + win].astype(BF), sel.astype(BF), preferred_element_type=F32)
                o_ref[j] = res[:, 0:IN_SH].astype(BF)

    return _pcall(body, "in_bwd_w", (nt,),
                  [_rows(tm, D), _rows(tm, MLA_IN)] + [_rows(tm, G_W)] * 4,
                  [_full((N_DEV, D, IN_SH))], [_sds((N_DEV, D, IN_SH), BF)],
                  scratch=[pltpu.VMEM((D, XP_W), F32)])(u, dxp_m, *dxp_h)[0]


def _pad_heads(w, width, real):
    lead = w.shape[:-1]
    w = w.reshape(lead + (N_HEADS, real))
    w = jnp.pad(w, [(0, 0)] * len(lead) + [(0, 0), (0, width - real)])
    return w.reshape(lead + (N_HEADS * width,))


def _unpad_heads(w, width, real):
    lead = w.shape[:-1]
    return w.reshape(lead + (N_HEADS, width))[..., :real].reshape(lead + (N_HEADS * real,))


def _rope_tables(positions):
    half = ROPE // 2
    inv_freq = 1.0 / (ROPE_THETA ** (jnp.arange(0, ROPE, 2, dtype=F32) / ROPE))
    ang = positions.astype(F32)[:, None] * inv_freq
    cos, sin = jnp.cos(ang), jnp.sin(ang)
    T = positions.shape[0]
    z = lambda n: jnp.zeros((T, n), F32)
    ta = jnp.concatenate([jnp.ones((T, NOPE), F32), cos, cos, z(HB - QK_DIM)], 1)
    tb1 = jnp.concatenate([z(NOPE), -sin, z(half), z(HB - QK_DIM)], 1)
    tb2 = jnp.concatenate([z(NOPE), z(half), sin, z(HB - QK_DIM)], 1)
    return ta, tb1, tb2


def kernel(x, positions, attn_pre_norm, w_in, mla_q_norm, mla_w_uq, mla_kv_norm, mla_w_ukv, mla_out_norm, hgrn_lb_logits, hgrn_out_norm, w_out, attn_post_norm, ffn_pre_norm, w_gate, w_up, w_down, ffn_post_norm, loss_target, m_attn_pre_norm, m_w_in, m_mla_q_norm, m_mla_w_uq, m_mla_kv_norm, m_mla_w_ukv, m_mla_out_norm, m_hgrn_lb_logits, m_hgrn_out_norm, m_w_out, m_attn_post_norm, m_ffn_pre_norm, m_w_gate, m_w_up, m_w_down, m_ffn_post_norm, v_attn_pre_norm, v_w_in, v_mla_q_norm, v_mla_w_uq, v_mla_kv_norm, v_mla_w_ukv, v_mla_out_norm, v_hgrn_lb_logits, v_hgrn_out_norm, v_w_out, v_attn_post_norm, v_ffn_pre_norm, v_w_gate, v_w_up, v_w_down, v_ffn_post_norm):
    T = x.shape[1]
    tm = min(_TM, T)
    tq = min(_TQ, T)
    xs, tgt = x[0], loss_target[0]
    uq_sh = (Q_RANK // N_DEV, N_HEADS * QK_DIM)

    b_in, b_uq, b_out, b_gu, b_d = _cast_shards(
        w_in[0], mla_w_uq[0].reshape(uq_sh), w_out[0], w_gate[0].T, w_up[0].T, w_down[0])
    g_in, g_uq = _exchange_call("ag_first", [GATHER, GATHER], [b_in, b_uq])
    w_in_al = _align_w_in(g_in)
    w_uq_p = _pad_heads(g_uq.reshape(Q_RANK, N_HEADS * QK_DIM), HB, QK_DIM)
    w_ukv = mla_w_ukv[0].astype(BF)
    w_uk_p = _pad_heads(w_ukv[..., :NOPE].reshape(KV_RANK, N_HEADS * NOPE), HB, NOPE)
    w_uv_p = _pad_heads(w_ukv[..., NOPE:].reshape(KV_RANK, N_HEADS * V_DIM), HB, V_DIM)
    g_on_p = _pad_heads(mla_out_norm, HB, V_DIM)
    tabs = _rope_tables(positions[0])

    xp_m, xp_h, u = _fwd_in(xs, attn_pre_norm, w_in_al, tm)
    q_att, k_att, v_att = _mla_prep(xp_m, tabs, mla_q_norm, mla_kv_norm, w_uq_p, w_uk_p, w_uv_p, tm)
    o_hgrn, o_raw, s_all, g_out, wd = _hgrn_fwd(xp_h, hgrn_lb_logits, hgrn_out_norm, ([GATHER, GATHER], [b_out, b_d]))
    wd = wd.reshape(N_DEV * FF_PAD, D)
    o_pad, lse, wgu = _flash_fwd(q_att, k_att, v_att, tq, ([GATHER], [b_gu]))
    wgu = wgu.reshape(N_DEV * 2 * FF_PAD, D)
    w_out_full = g_out.reshape(D, D)
    w_out_mla = jnp.pad(w_out_full[:N_HEADS * V_DIM].reshape(N_HEADS, V_DIM, D), ((0, 0), (0, HB - V_DIM), (0, 0)))
    w_out_p = jnp.concatenate([w_out_mla.reshape(N_HEADS * HB, D), w_out_full[N_HEADS * V_DIM:]], 0)
    h1, y1, z, mix = _fwd_out(o_pad, o_hgrn, xs, g_on_p, w_out_p, attn_post_norm, ffn_pre_norm, tm)
    gs, us, ffs, dh2, dy2, d_fpost, loss_row = _ffn_fwd(z, wgu, wd, h1, tgt, ffn_post_norm, tm)

    dgu, dh1, dy1, d_fpre, d_post = _ffn_bwd_x(dy2, gs, us, wgu, wd, h1, y1, dh2, ffn_pre_norm, attn_post_norm, tm)
    dwgu, dwd = _ffn_bwd_w(z, ffs, dgu, dy2, tm)
    do_pad, dl, d_ohg, dw_out_p, d_on_p = _out_bwd(dy1, mix, o_pad, w_out_p, g_on_p, tm)
    dw_out_mla = dw_out_p[:N_HEADS * HB].reshape(N_HEADS, HB, D)[:, :V_DIM].reshape(N_HEADS * V_DIM, D)
    dw_out = jnp.concatenate([dw_out_mla, dw_out_p[N_HEADS * HB:]], 0).reshape(N_DEV, D // N_DEV, D).astype(BF)
    dk_att, dv_att, dq_att, p_gu, p_d, p_out = _flash_bwd(
        q_att, k_att, v_att, do_pad, lse, dl, tq,
        ([SCATTER] * 3, [dwgu.reshape(N_DEV, 2 * FF_PAD, D), dwd.reshape(N_DEV, FF_PAD, D), dw_out]))
    dxp_m, dw_uq_p, dw_uk_p, dw_uv_p, d_gq, d_gkv = _mla_prep_bwd(
        xp_m, tabs, dq_att, dk_att, dv_att, mla_q_norm, mla_kv_norm, w_uq_p, w_uk_p, w_uv_p, tm)
    dw_uq = _unpad_heads(dw_uq_p, HB, QK_DIM).reshape((N_DEV,) + uq_sh).astype(BF)
    dw_ukv = jnp.concatenate([_unpad_heads(dw_uk_p, HB, NOPE).reshape(KV_RANK, N_HEADS, NOPE),
                              _unpad_heads(dw_uv_p, HB, V_DIM).reshape(KV_RANK, N_HEADS, V_DIM)], -1)
    *dxp_h, d_lbl, d_ghn, p_uq, dw_ukv_all = _hgrn_bwd(
        xp_h, o_raw, s_all, d_ohg, hgrn_lb_logits, hgrn_out_norm,
        ([SCATTER, GATHER], [dw_uq, dw_ukv.reshape(KV_RANK, N_HEADS * HB)]))
    dw_in = _in_bwd_w(u, dxp_m, dxp_h, tm)
    grad_x, d_pre, p_in = _in_bwd_x(xs, dxp_m, dxp_h, dh1, w_in_al, attn_pre_norm, tm, ([SCATTER], [dw_in]))
    d_on = _unpad_heads(d_on_p, HB, V_DIM)

    ukv2 = lambda a: a.reshape(KV_RANK, N_HEADS * HB)
    vecs = [d_pre, d_gq, d_gkv, d_on, d_lbl, d_ghn, d_post, d_fpre, d_fpost, loss_row]
    small_w = [attn_pre_norm, mla_q_norm, mla_kv_norm, ukv2(mla_w_ukv), mla_out_norm, hgrn_lb_logits, hgrn_out_norm,
               attn_post_norm, ffn_pre_norm, ffn_post_norm]
    small_m = [m_attn_pre_norm, m_mla_q_norm, m_mla_kv_norm, ukv2(m_mla_w_ukv), m_mla_out_norm, m_hgrn_lb_logits,
               m_hgrn_out_norm, m_attn_post_norm, m_ffn_pre_norm, m_ffn_post_norm]
    small_v = [v_attn_pre_norm, v_mla_q_norm, v_mla_kv_norm, ukv2(v_mla_w_ukv), v_mla_out_norm, v_hgrn_lb_logits,
               v_hgrn_out_norm, v_attn_post_norm, v_ffn_pre_norm, v_ffn_post_norm]
    rall = _final_exchange(vecs)
    s_g, s_d, s_m, s_v, loss_all = _small_adam(rall, dw_ukv_all, 3, small_w, small_m, small_v)
    r_in = _shard_adam("adam_w_in", p_in, w_in[0], m_w_in[0], v_w_in[0], 256)
    r_uq = _shard_adam("adam_w_uq", p_uq, mla_w_uq[0].reshape(uq_sh), m_mla_w_uq[0].reshape(uq_sh),
                       v_mla_w_uq[0].reshape(uq_sh), uq_sh[0])
    r_out = _shard_adam("adam_w_out", p_out, w_out[0], m_w_out[0], v_w_out[0], D // N_DEV)
    r_g = [a.T for a in _shard_adam("adam_w_gate", p_gu, w_gate[0].T, m_w_gate[0].T, v_w_gate[0].T, 32)]
    r_u = [a.T for a in _shard_adam("adam_w_up", p_gu, w_up[0].T, m_w_up[0].T, v_w_up[0].T, 32, FF_PAD)]
    r_d = _shard_adam("adam_w_down", p_d, w_down[0], m_w_down[0], v_w_down[0], FF_SH // 2)

    loss = loss_all[0, 0]

    def assemble(big, small):
        b_in, b_uq, b_out, b_g, b_u, b_d = big
        return [small[0], b_in[None], small[1], b_uq.reshape(mla_w_uq.shape), small[2],
                small[3].reshape(mla_w_ukv.shape), small[4], small[5], small[6], b_out[None], small[7], small[8],
                b_g[None], b_u[None], b_d[None], small[9]]

    outs = [loss, grad_x[None]]
    for idx, small in enumerate((s_g, s_d, s_m, s_v)):
        outs += assemble([r[idx] for r in (r_in, r_uq, r_out, r_g, r_u, r_d)], small)
    return tuple(outs)
```

```python
import jax
import jax.numpy as jnp
from jax import lax
from jax.experimental import pallas as pl
from jax.experimental.pallas import tpu as pltpu

BF = jnp.bfloat16
F32 = jnp.float32
MESH = pl.DeviceIdType.MESH

N_DEV = 8
D = 1024
EPS = 1e-6
LOG2E = 1.4426950408889634
ROPE_THETA = 10000.0
N_HEADS = 8
HB = 128
NOPE = 64
ROPE = 32
V_DIM = 64
QK_DIM = NOPE + ROPE
Q_RANK = 384
KV_RANK = 128
KR_PAD = 128
MLA_IN = Q_RANK + KV_RANK + KR_PAD
G_HEADS = 4
G_DIM = 128
G_W = G_HEADS * G_DIM
CHUNK = 64
SUB_FWD = 16
SUB_BWD = 8
XP_W = MLA_IN + 4 * G_W
IN_SH = 324
IN_W = N_DEV * IN_SH
FF_SH = 352
FF_PAD = 384
MIX_W = N_HEADS * HB + G_W

ADAM_LR = 0.001
ADAM_B1 = 0.9
ADAM_B2 = 0.999
ADAM_EPS = 1e-08
ADAM_WD = 0.01
ADAM_STEP = 10

_TM = 512
_TQ = 512
_AH = 2
_FB = 768
_TB = 1024
_HP = 4
_VMEM_LIMIT = 56 * 1024 * 1024
NEG = -1e30


def _dot(a, b):
    return jnp.dot(a.astype(BF), b.astype(BF), preferred_element_type=F32)


def _dot_nt(a, b):
    return lax.dot_general(a.astype(BF), b.astype(BF), (((1,), (1,)), ((), ())), preferred_element_type=F32)


def _dot_tn(a, b):
    return lax.dot_general(a.astype(BF), b.astype(BF), (((0,), (0,)), ((), ())), preferred_element_type=F32)


def _sigmoid(x):
    return 1.0 / (1.0 + jnp.exp(-x))


def _rms(x, n):
    r = lax.rsqrt(jnp.sum(x * x, -1, keepdims=True) * (1.0 / n) + EPS)
    return x * r, r


def _rms_bwd(nx, r, g, dy, n):
    dg = jnp.sum(dy * nx, 0, keepdims=True)
    dn = dy * g
    dx = r * (dn - nx * (jnp.sum(dn * nx, -1, keepdims=True) * (1.0 / n)))
    return dx, dg


def _adamw(w, g, m, v):
    m2 = ADAM_B1 * m + (1.0 - ADAM_B1) * g
    v2 = ADAM_B2 * v + (1.0 - ADAM_B2) * (g * g)
    m_hat = m2 / (1.0 - ADAM_B1 ** ADAM_STEP)
    v_hat = v2 / (1.0 - ADAM_B2 ** ADAM_STEP)
    delta = -ADAM_LR * (m_hat / (jnp.sqrt(v_hat) + ADAM_EPS) + ADAM_WD * w)
    return delta, m2, v2


def _pcall(body, name, grid, in_specs, out_specs, out_shape, scratch=(), exchange=None):
    scratch = list(scratch)
    extra = ()
    if exchange is not None:
        kinds, extra = exchange
        in_specs, out_specs, out_shape = list(in_specs), list(out_specs), list(out_shape)
        n_in, n_out, n_scr, n_x = len(in_specs), len(out_specs), len(scratch), len(extra)
        inner = body

        def body(*refs):
            ins, rest = refs[:n_in], refs[n_in:]
            x_src, rest = rest[:n_x], rest[n_x:]
            outs, rest = rest[:n_out], rest[n_out:]
            x_dst, rest = rest[:n_x], rest[n_x:]
            ex = _Exchange(kinds, x_src, x_dst, *rest[n_scr:])
            first = pl.program_id(0) == 0
            last = pl.program_id(0) == grid[0] - 1
            for a in range(1, len(grid)):
                first = first & (pl.program_id(a) == 0)
                last = last & (pl.program_id(a) == grid[a] - 1)
            pl.when(first)(ex.start)
            inner(*ins, *outs, *rest[:n_scr])
            pl.when(last)(ex.wait)

        in_specs += [_HBM] * n_x
        out_specs += [_HBM] * n_x
        out_shape += _exchange_shapes(kinds, extra)
        scratch += _exchange_sems(n_x)
    call = pl.pallas_call(
        body, name=name, grid=grid, in_specs=in_specs, out_specs=out_specs, out_shape=out_shape,
        scratch_shapes=scratch,
        compiler_params=pltpu.CompilerParams(
            dimension_semantics=("arbitrary",) * len(grid), vmem_limit_bytes=_VMEM_LIMIT))
    return lambda *operands: call(*operands, *extra)


def _full(shape):
    return pl.BlockSpec(shape, lambda *_: (0,) * len(shape))


def _rows(tm, n):
    return pl.BlockSpec((tm, n), lambda i, *_: (i, 0))


def _sds(shape, dtype=F32):
    return jax.ShapeDtypeStruct(shape, dtype)


def _peer(k, x, y, c):
    px = 1 - x if (k >> 2) & 1 else x
    py = 1 - y if (k >> 1) & 1 else y
    pc = 1 - c if k & 1 else c
    return px, py, pc


GATHER, SCATTER = "gather", "scatter"


class _Exchange:
    def __init__(self, kinds, srcs, dsts, send_sems, recv_sems, loc_sems):
        self.kinds, self.srcs, self.dsts = kinds, srcs, dsts
        self.send_sems, self.recv_sems, self.loc_sems = send_sems, recv_sems, loc_sems
        self.x, self.y, self.c = lax.axis_index("x"), lax.axis_index("y"), lax.axis_index("c")
        self.me = 4 * self.x + 2 * self.y + self.c

    def _src(self, w, slot):
        return self.srcs[w] if self.kinds[w] == GATHER else self.srcs[w].at[slot]

    def _dst(self, w, slot):
        return self.dsts[w].at[slot]

    def _copy(self, w, k, outgoing):
        px, py, pc = _peer(k, self.x, self.y, self.c)
        pid = 4 * px + 2 * py + pc
        return pltpu.make_async_remote_copy(
            src_ref=self._src(w, pid if outgoing else self.me),
            dst_ref=self._dst(w, self.me if outgoing else pid),
            send_sem=self.send_sems.at[w, k - 1], recv_sem=self.recv_sems.at[w, k - 1],
            device_id=(px, py, pc), device_id_type=MESH)

    def _local(self, w):
        return pltpu.make_async_copy(self._src(w, self.me), self._dst(w, self.me), self.loc_sems.at[w])

    def start(self):
        for w in range(len(self.srcs)):
            self._local(w).start()
            for k in range(1, N_DEV):
                self._copy(w, k, True).start()

    def wait(self):
        for w in range(len(self.srcs)):
            self._local(w).wait()
            for k in range(1, N_DEV):
                self._copy(w, k, False).wait_recv()
        for w in range(len(self.srcs)):
            for k in range(1, N_DEV):
                self._copy(w, k, True).wait_send()


def _exchange_sems(n_w):
    return [pltpu.SemaphoreType.DMA((n_w, N_DEV - 1)), pltpu.SemaphoreType.DMA((n_w, N_DEV - 1)),
            pltpu.SemaphoreType.DMA((n_w,))]


def _exchange_shapes(kinds, srcs):
    return [_sds(((N_DEV,) if kd == GATHER else ()) + tuple(s.shape), s.dtype) for kd, s in zip(kinds, srcs)]


_HBM = pl.BlockSpec(memory_space=pl.ANY)


def _cast_shards(w_in, w_uq, w_out, w_gate_t, w_up_t, w_down):
    shapes = [(D, IN_SH), (Q_RANK // N_DEV, N_HEADS * QK_DIM), (D // N_DEV, D), (2 * FF_PAD, D), (FF_PAD, D)]

    def body(win, wuq, wout, wg, wu, wd, sin_, suq, sout, sgu, sd):
        sin_[...] = win[...].astype(BF)
        suq[...] = wuq[...].astype(BF)
        sout[...] = wout[...].astype(BF)
        sgu[...] = jnp.zeros(sgu.shape, BF)
        sgu[0:FF_SH, :] = wg[...].astype(BF)
        sgu[FF_PAD:FF_PAD + FF_SH, :] = wu[...].astype(BF)
        sd[...] = jnp.zeros(sd.shape, BF)
        sd[0:FF_SH, :] = wd[...].astype(BF)

    vm = pl.BlockSpec(memory_space=pltpu.VMEM)
    return pl.pallas_call(
        body, name="cast_shards", in_specs=[vm] * 6, out_specs=[vm] * 5,
        out_shape=[_sds(s, BF) for s in shapes],
        compiler_params=pltpu.CompilerParams(vmem_limit_bytes=_VMEM_LIMIT),
    )(w_in, w_uq, w_out, w_gate_t, w_up_t, w_down)


def _gather_two_level(name, srcs):
    n_w = len(srcs)

    def body(*refs):
        src, dst = refs[:n_w], refs[n_w:2 * n_w]
        send_sems, recv_sems, loc_sems = refs[2 * n_w:]
        x, y, c = lax.axis_index("x"), lax.axis_index("y"), lax.axis_index("c")
        me, sibling = (x, y, c), (x, y, 1 - c)
        chips = [(1 - x, y), (x, 1 - y), (1 - x, 1 - y)]
        slot = lambda p: 4 * p[0] + 2 * p[1] + p[2]

        def copy(w, k, block, to, own=False):
            return pltpu.make_async_remote_copy(
                src_ref=src[w] if own else dst[w].at[slot(block)], dst_ref=dst[w].at[slot(block)],
                send_sem=send_sems.at[w, k], recv_sem=recv_sems.at[w, k], device_id=to, device_id_type=MESH)

        local = [pltpu.make_async_copy(src[w], dst[w].at[slot(me)], loc_sems.at[w]) for w in range(n_w)]
        first, passed = [], []
        for w in range(n_w):
            local[w].start()
            first.append(copy(w, 0, me, sibling, own=True))
            first += [copy(w, 1 + j, me, (*chip, c), own=True) for j, chip in enumerate(chips)]
        for cp in first:
            cp.start()
        for w in range(n_w):
            for j, chip in enumerate(chips):
                copy(w, 1 + j, (*chip, c), me).wait_recv()
                passed.append(copy(w, 4 + j, (*chip, c), sibling))
                passed[-1].start()
        for w in range(n_w):
            copy(w, 0, sibling, me).wait_recv()
            for j, chip in enumerate(chips):
                copy(w, 4 + j, (*chip, 1 - c), me).wait_recv()
        for cp in first + passed:
            cp.wait_send()
        for w in range(n_w):
            local[w].wait()

    return pl.pallas_call(
        body, name=name, in_specs=[_HBM] * n_w, out_specs=[_HBM] * n_w,
        out_shape=_exchange_shapes([GATHER] * n_w, srcs), scratch_shapes=_exchange_sems(n_w))(*srcs)


def _row_offsets(arrays):
    offs, rows = [], 0
    for a in arrays:
        offs.append(rows)
        rows += a.shape[0]
    return offs, -(-rows // 8) * 8


def _final_exchange(vecs):
    n_p = len(vecs)
    offs, rows = _row_offsets(vecs)

    def body(*refs):
        g_refs = refs[:n_p]
        rall, pk, send_sems, recv_sems, loc_sem = refs[n_p:]
        x, y, c = lax.axis_index("x"), lax.axis_index("y"), lax.axis_index("c")
        me = 4 * x + 2 * y + c
        pk[...] = jnp.zeros(pk.shape, F32)
        for p in range(n_p):
            r, n = g_refs[p].shape
            pk[offs[p]:offs[p] + r, 0:n] = g_refs[p][...]

        def remote(k):
            return pltpu.make_async_remote_copy(
                src_ref=pk, dst_ref=rall.at[me], send_sem=send_sems.at[k - 1], recv_sem=recv_sems.at[k - 1],
                device_id=_peer(k, x, y, c), device_id_type=MESH)

        def arrival(k):
            px, py, pc = _peer(k, x, y, c)
            return pltpu.make_async_remote_copy(
                src_ref=pk, dst_ref=rall.at[4 * px + 2 * py + pc], send_sem=send_sems.at[k - 1],
                recv_sem=recv_sems.at[k - 1], device_id=(px, py, pc), device_id_type=MESH)

        local = pltpu.make_async_copy(pk, rall.at[me], loc_sem)
        local.start()
        for k in range(1, N_DEV):
            remote(k).start()
        local.wait()
        for k in range(1, N_DEV):
            arrival(k).wait_recv()
        for k in range(1, N_DEV):
            remote(k).wait_send()

    vm = pl.BlockSpec(memory_space=pltpu.VMEM)
    return pl.pallas_call(
        body, name="final_exchange", in_specs=[vm] * n_p, out_specs=vm, out_shape=_sds((N_DEV, rows, D)),
        scratch_shapes=[pltpu.VMEM((rows, D), F32),
                        pltpu.SemaphoreType.DMA((N_DEV - 1,)), pltpu.SemaphoreType.DMA((N_DEV - 1,)),
                        pltpu.SemaphoreType.DMA],
    )(*vecs)


def _small_adam(rall, big_parts, big, ws, ms, vs):
    n_p = len(ws)
    packed = [w for p, w in enumerate(ws) if p != big] + [jax.ShapeDtypeStruct((1, HB), F32)]
    offs, _ = _row_offsets(packed)
    offs = offs[:big] + [None] + offs[big:]

    def total(ref, sl):
        g = ref[(0,) + sl]
        for j in range(1, N_DEV):
            g = g + ref[(j,) + sl]
        return g

    def body(*refs):
        rall_ref, big_ref = refs[:2]
        w_refs, m_refs, v_refs = refs[2:2 + n_p], refs[2 + n_p:2 + 2 * n_p], refs[2 + 2 * n_p:2 + 3 * n_p]
        outs = refs[2 + 3 * n_p:]
        for p in range(n_p):
            r, n = w_refs[p].shape
            if p == big:
                g = total(big_ref, (slice(0, r), slice(0, n)))
            else:
                g = total(rall_ref, (slice(offs[p], offs[p] + r), slice(0, n)))
            delta, m2, v2 = _adamw(w_refs[p][...], g, m_refs[p][...], v_refs[p][...])
            outs[p][...] = g
            outs[n_p + p][...] = delta
            outs[2 * n_p + p][...] = m2
            outs[3 * n_p + p][...] = v2
        outs[4 * n_p][...] = total(rall_ref, (slice(offs[n_p], offs[n_p] + 1), slice(0, HB)))

    vm = pl.BlockSpec(memory_space=pltpu.VMEM)
    res = pl.pallas_call(
        body, name="small_adam", in_specs=[vm] * (2 + 3 * n_p), out_specs=[vm] * (4 * n_p + 1),
        out_shape=[_sds(w.shape) for w in ws] * 4 + [_sds((1, HB))],
        compiler_params=pltpu.CompilerParams(vmem_limit_bytes=_VMEM_LIMIT),
    )(rall, big_parts, *ws, *ms, *vs)
    return res[:n_p], res[n_p:2 * n_p], res[2 * n_p:3 * n_p], res[3 * n_p:4 * n_p], res[4 * n_p]


def _shard_adam(name, parts, w, m, v, tr, row0=0):
    a0, b0 = w.shape
    b = parts.shape[2]
    blk0 = row0 // tr

    def body(p_ref, w_ref, m_ref, v_ref, g_out, d_out, m_out, v_out):
        g = p_ref[0].astype(F32)
        for j in range(1, N_DEV):
            g = g + p_ref[j].astype(F32)
        g = g[:, 0:b0]
        delta, m2, v2 = _adamw(w_ref[...], g, m_ref[...], v_ref[...])
        g_out[...] = g
        d_out[...] = delta
        m_out[...] = m2
        v_out[...] = v2

    blk = pl.BlockSpec((tr, b0), lambda i: (i, 0))
    return _pcall(
        body, name, (a0 // tr,),
        [pl.BlockSpec((N_DEV, tr, b), lambda i: (0, blk0 + i, 0)), blk, blk, blk],
        [blk] * 4, [_sds((a0, b0))] * 4)(parts, w, m, v)


def _fwd_in(x, g_pre, w_in_al, tm):
    T = x.shape[0]

    def body(x_ref, g_ref, w_ref, xm_ref, xh_ref, u_ref):
        nx, _ = _rms(x_ref[...], D)
        u = (nx * g_ref[...]).astype(BF)
        u_ref[...] = u
        xm_ref[...] = jnp.dot(u, w_ref[:, 0:MLA_IN], preferred_element_type=F32)
        xh_ref[...] = jnp.dot(u, w_ref[:, MLA_IN:XP_W], preferred_element_type=F32)

    return _pcall(body, "fwd_in", (T // tm,),
                  [_rows(tm, D), _full((1, D)), _full((D, XP_W))],
                  [_rows(tm, MLA_IN), _rows(tm, 4 * G_W), _rows(tm, D)],
                  [_sds((T, MLA_IN)), _sds((T, 4 * G_W)), _sds((T, D), BF)])(x, g_pre, w_in_al)


def _rope(blk, ta, tb1, tb2):
    return blk * ta + pltpu.roll(blk, HB - ROPE // 2, 1) * tb1 + pltpu.roll(blk, ROPE // 2, 1) * tb2


def _unrope(d, ta, tb1, tb2):
    return d * ta + pltpu.roll(d * tb1, ROPE // 2, 1) + pltpu.roll(d * tb2, HB - ROPE // 2, 1)


def _mla_prep(xp, tabs, g_q, g_kv, w_uq, w_uk, w_uv, tm):
    T = xp.shape[0]
    W = N_HEADS * HB

    def body(xp_ref, ta_ref, tb1_ref, tb2_ref, gq_ref, gkv_ref, wuq_ref, wuk_ref, wuv_ref, q_ref, k_ref, v_ref):
        ta, tb1, tb2 = ta_ref[...], tb1_ref[...], tb2_ref[...]
        nq, _ = _rms(xp_ref[:, 0:Q_RANK], Q_RANK)
        nkv, _ = _rms(xp_ref[:, Q_RANK:Q_RANK + KV_RANK], KV_RANK)
        nkv = (nkv * gkv_ref[...]).astype(BF)
        qpre = _dot(nq * gq_ref[...], wuq_ref[...])
        kpre = jnp.dot(nkv, wuk_ref[...], preferred_element_type=F32)
        v_ref[...] = jnp.dot(nkv, wuv_ref[...], preferred_element_type=F32).astype(BF)
        kr = _rope(pltpu.roll(xp_ref[:, Q_RANK + KV_RANK:MLA_IN], NOPE, 1), ta, tb1, tb2)
        for h in range(N_HEADS):
            sl = slice(h * HB, (h + 1) * HB)
            q_ref[:, sl] = _rope(qpre[:, sl], ta, tb1, tb2).astype(BF)
            k_ref[:, sl] = (kpre[:, sl] + kr).astype(BF)

    tab = _rows(tm, HB)
    return _pcall(body, "mla_prep", (T // tm,),
                  [_rows(tm, MLA_IN), tab, tab, tab, _full((1, Q_RANK)), _full((1, KV_RANK)),
                   _full((Q_RANK, W)), _full((KV_RANK, W)), _full((KV_RANK, W))],
                  [_rows(tm, W)] * 3, [_sds((T, W), BF)] * 3)(xp, *tabs, g_q, g_kv, w_uq, w_uk, w_uv)


def _flash_fwd(q, k, v, tq, exchange=None):
    T = q.shape[0]
    scale = QK_DIM ** -0.5

    hp = _AH
    W = hp * HB

    def body(q_ref, k_ref, v_ref, o_ref, lse_ref):
        i = pl.program_id(1)

        def blk(j, carry, masked):
            st = pl.multiple_of(j * tq, tq)
            out = []
            for h in range(hp):
                ls = slice(h * HB, (h + 1) * HB)
                m, l, acc = carry[h]
                s = _dot_nt(q_ref[:, ls], k_ref[pl.ds(st, tq), ls]) * scale
                if masked:
                    r = lax.broadcasted_iota(jnp.int32, (tq, tq), 0)
                    c = lax.broadcasted_iota(jnp.int32, (tq, tq), 1)
                    s = jnp.where(c <= r, s, NEG)
                m2 = jnp.maximum(m, jnp.max(s, -1, keepdims=True))
                p = jnp.exp(s - m2)
                a = jnp.exp(m - m2)
                out.append((m2, a * l + jnp.sum(p, -1, keepdims=True), a * acc + _dot(p, v_ref[pl.ds(st, tq), ls])))
            return tuple(out)

        init = tuple((jnp.full((tq, 1), NEG, F32), jnp.zeros((tq, 1), F32), jnp.zeros((tq, HB), F32))
                     for _ in range(hp))
        carry = lax.fori_loop(0, i, lambda j, cr: blk(j, cr, False), init)
        res = blk(i, carry, True)
        for h in range(hp):
            ls = slice(h * HB, (h + 1) * HB)
            m, l, acc = res[h]
            o_ref[:, ls] = acc / l
            lse_ref[:, ls] = jnp.broadcast_to(m + jnp.log(l), (tq, HB))

    qs = pl.BlockSpec((tq, W), lambda h, i: (i, h))
    kvs = pl.BlockSpec((T, W), lambda h, i: (0, h))
    return _pcall(body, "flash_fwd", (N_HEADS // hp, T // tq), [qs, kvs, kvs], [qs, qs],
                  [_sds((T, N_HEADS * HB))] * 2, exchange=exchange)(q, k, v)


def _gates(hq, hf, lb):
    sig = _sigmoid(hf)
    f = lb + (1.0 - lb) * sig
    sq = _sigmoid(hq)
    return hq * sq, 1.0 - f, f, jnp.log(f), sig, sq


def _lower_bound(lbl_ref):
    l0, l1 = lbl_ref[0:1, :], lbl_ref[1:2, :]
    mx = jnp.maximum(l0, l1)
    e0, e1 = jnp.exp(l0 - mx), jnp.exp(l1 - mx)
    return e0 / (e0 + e1)


def _split3(x):
    hi = x.astype(BF)
    r1 = x - hi.astype(F32)
    mid = r1.astype(BF)
    lo = (r1 - mid.astype(F32)).astype(BF)
    return hi, mid, lo


def _tri_mm(tri, x):
    hi, mid, lo = _split3(x)
    mm = lambda t: jnp.dot(tri, t, preferred_element_type=F32)
    return mm(hi) + mm(mid) + mm(lo)


def _intra_codes(sub):
    row = lax.broadcasted_iota(jnp.int32, (CHUNK, CHUNK), 0)
    col = lax.broadcasted_iota(jnp.int32, (CHUNK, CHUNK), 1)
    sub_t, sub_s = row // sub, col // sub
    return sub, jnp.where(sub_s < sub_t, sub_t, 0), jnp.where((sub_s == sub_t) & (col <= row), row - col, -1)


def _intra(q, k, b2, b_s, codes, da=None):
    grad = da is not None
    pow2 = (lambda x: jnp.exp2(jnp.minimum(x, 0.0))) if grad else jnp.exp2
    sub, earlier, offset = codes
    a = jnp.zeros((CHUNK, CHUNK), F32)
    dq = jnp.zeros((CHUNK, G_DIM), F32)
    dk = jnp.zeros((CHUNK, G_DIM), F32)
    for i in range(1, CHUNK // sub):
        b0 = b_s[sub * i - 1:sub * i, :]
        eq, ek = pow2(b2 - b0), pow2(b0 - b2)
        mask = earlier == i
        a = jnp.where(mask, _dot_nt(q * eq, k * ek), a)
        if grad:
            dai = jnp.where(mask, da, 0.0)
            dq = dq + _dot(dai, k * ek) * eq
            dk = dk + _dot_tn(dai, q * eq) * ek
    for d in range(sub):
        ksh = pltpu.roll(k, d, 0) if d else k
        bsh = pltpu.roll(b2, d, 0) if d else b2
        e = pow2(b2 - bsh)
        mask = offset == d
        a = jnp.where(mask, jnp.sum(q * ksh * e, -1, keepdims=True), a)
        if grad:
            g = jnp.sum(jnp.where(mask, da, 0.0), -1, keepdims=True) * e
            dq = dq + g * ksh
            cb = g * q
            dk = dk + (pltpu.roll(cb, CHUNK - d, 0) if d else cb)
    return (a, dq, dk) if grad else a


def _hgrn_fwd(xp, lb_logits, g_hn, exchange=None):
    T = xp.shape[0]
    tb = min(_TB, T)
    ncb = tb // CHUNK
    hp = _HP
    W = hp * G_DIM

    def body(hq_ref, hf_ref, hi_ref, hg_ref, lbl_ref, ghn_ref, out_ref, oraw_ref, sall_ref, st_ref, b_s):
        lb_all = _lower_bound(lbl_ref)

        @pl.when(pl.program_id(1) == 0)
        def _():
            st_ref[...] = jnp.zeros(st_ref.shape, F32)

        row = lax.broadcasted_iota(jnp.int32, (CHUNK, CHUNK), 0)
        col = lax.broadcasted_iota(jnp.int32, (CHUNK, CHUNK), 1)
        tri = (col <= row).astype(BF)
        codes = _intra_codes(SUB_FWD)

        def chunk(c, carry):
            sl = pl.ds(pl.multiple_of(c * CHUNK, CHUNK), CHUNK)
            for h in range(hp):
                ls = slice(h * G_DIM, (h + 1) * G_DIM)
                q, k, _, lf, _, _ = _gates(hq_ref[sl, ls], hf_ref[sl, ls], lb_all[:, ls])
                v = hi_ref[sl, ls]
                b2 = _tri_mm(tri, lf) * LOG2E
                b_s[h] = b2
                st = st_ref[h]
                sall_ref[c, h] = st
                o = _dot_nt(q * jnp.exp2(b2), st) + _dot(_intra(q, k, b2, b_s.at[h], codes), v)
                bl = b_s[h, CHUNK - 1:CHUNK, :]
                st_ref[h] = st * jnp.exp2(bl) + _dot_tn(v, k * jnp.exp2(bl - b2))
                oraw_ref[sl, ls] = o
                n, _ = _rms(o, G_DIM)
                hg = hg_ref[sl, ls]
                out_ref[sl, ls] = n * ghn_ref[:, ls] * (hg * _sigmoid(hg))
            return carry

        lax.fori_loop(0, ncb, chunk, 0)

    col_blk = lambda j: pl.BlockSpec((tb, W), lambda p, t: (t, j * (G_HEADS // hp) + p))
    head = pl.BlockSpec((tb, W), lambda p, t: (t, p))
    return _pcall(
        body, "hgrn_fwd", (G_HEADS // hp, T // tb),
        [col_blk(0), col_blk(1), col_blk(2), col_blk(3),
         pl.BlockSpec((2, W), lambda p, t: (0, p)), pl.BlockSpec((1, W), lambda p, t: (0, p))],
        [head, head, pl.BlockSpec((ncb, hp, G_DIM, G_DIM), lambda p, t: (t, p, 0, 0))],
        [_sds((T, G_W)), _sds((T, G_W)), _sds((T // CHUNK, G_HEADS, G_DIM, G_DIM))],
        scratch=[pltpu.VMEM((hp, G_DIM, G_DIM), F32), pltpu.VMEM((hp, CHUNK, G_DIM), F32)], exchange=exchange,
    )(xp, xp, xp, xp, lb_logits, g_hn)


def _fwd_out(o_pad, o_hgrn, x, g_on, w_out, g_post, g_fpre, tm):
    T = x.shape[0]

    def body(o_ref, oh_ref, x_ref, gon_ref, w_ref, gpost_ref, gfpre_ref, h1_ref, y1_ref, z_ref, mix_ref):
        for h in range(N_HEADS):
            sl = slice(h * HB, (h + 1) * HB)
            n, _ = _rms(o_ref[:, sl], V_DIM)
            mix_ref[:, sl] = (n * gon_ref[:, sl]).astype(BF)
        mix_ref[:, N_HEADS * HB:MIX_W] = oh_ref[...].astype(BF)
        y1 = jnp.dot(mix_ref[...], w_ref[...], preferred_element_type=F32)
        y1_ref[...] = y1
        ny, _ = _rms(y1, D)
        h1 = x_ref[...] + ny * gpost_ref[...]
        h1_ref[...] = h1
        nh, _ = _rms(h1, D)
        z_ref[...] = (nh * gfpre_ref[...]).astype(BF)

    return _pcall(body, "fwd_out", (T // tm,),
                  [_rows(tm, N_HEADS * HB), _rows(tm, G_W), _rows(tm, D), _full((1, N_HEADS * HB)),
                   _full((MIX_W, D)), _full((1, D)), _full((1, D))],
                  [_rows(tm, D), _rows(tm, D), _rows(tm, D), _rows(tm, MIX_W)],
                  [_sds((T, D)), _sds((T, D)), _sds((T, D), BF), _sds((T, MIX_W), BF)],
                  )(o_pad, o_hgrn, x, g_on, w_out, g_post, g_fpre)


def _ffn_fwd(z, wgu, wd, h1, tgt, g_fpost, tm):
    T = z.shape[0]
    nf = wd.shape[0] // _FB

    def body(z_ref, wgu_ref, wd_ref, h1_ref, t_ref, gp_ref,
             as_ref, bs_ref, ff_ref, dh2_ref, dy2_ref, dgp_ref, loss_ref, acc):
        i, j = pl.program_id(0), pl.program_id(1)
        gu = _dot_nt(z_ref[...], wgu_ref[...])
        g = jnp.concatenate([gu[:, 0:FF_PAD], gu[:, 2 * FF_PAD:3 * FF_PAD]], 1)
        u = jnp.concatenate([gu[:, FF_PAD:2 * FF_PAD], gu[:, 3 * FF_PAD:4 * FF_PAD]], 1)
        s = _sigmoid(g)
        b = g * s
        ff = (b * u).astype(BF)
        as_ref[...] = (u * _dsilu(g, s)).astype(BF)
        bs_ref[...] = b.astype(BF)
        ff_ref[...] = ff
        part = jnp.dot(ff, wd_ref[...], preferred_element_type=F32)

        @pl.when(j == 0)
        def _():
            acc[...] = part

        @pl.when(j > 0)
        def _():
            acc[...] += part

        @pl.when((i == 0) & (j == 0))
        def _():
            dgp_ref[...] = jnp.zeros(dgp_ref.shape, F32)
            loss_ref[...] = jnp.zeros(loss_ref.shape, F32)

        @pl.when(j == nf - 1)
        def _():
            ny, r = _rms(acc[...], D)
            err = h1_ref[...] + ny * gp_ref[...] - t_ref[...]
            loss_ref[...] += 0.5 * jnp.sum(jnp.sum(err * err, -1, keepdims=True) * (1.0 / D), 0, keepdims=True)
            dh2 = err * (1.0 / D)
            dh2_ref[...] = dh2
            dy2, dgp = _rms_bwd(ny, r, gp_ref[...], dh2, D)
            dy2_ref[...] = dy2.astype(BF)
            dgp_ref[...] += dgp

    tok = lambda n: pl.BlockSpec((tm, n), lambda i, j: (i, 0))
    col = pl.BlockSpec((tm, _FB), lambda i, j: (i, j))
    return _pcall(
        body, "ffn_fwd", (T // tm, nf),
        [tok(D), pl.BlockSpec((2 * _FB, D), lambda i, j: (j, 0)), pl.BlockSpec((_FB, D), lambda i, j: (j, 0)),
         tok(D), tok(D), _full((1, D))],
        [col, col, col, tok(D), tok(D), _full((1, D)), _full((1, HB))],
        [_sds((T, nf * _FB), BF)] * 3 + [_sds((T, D)), _sds((T, D), BF), _sds((1, D)), _sds((1, HB))],
        scratch=[pltpu.VMEM((tm, D), F32)],
    )(z, wgu, wd, h1, tgt, g_fpost)


def _dsilu(x, s):
    return s * (1.0 + x * (1.0 - s))


def _ffn_bwd_x(dy2, gs, us, wgu, wd, h1, y1, dh2, g_fpre, g_post, tm):
    T = dy2.shape[0]
    nf = wd.shape[0] // _FB

    def body(dy2_ref, gs_ref, us_ref, wgu_ref, wd_ref, h1_ref, y1_ref, dh2_ref, gf_ref, gp_ref,
             dgu_ref, dh1_ref, dy1_ref, dgf_ref, dgp_ref, acc):
        i, j = pl.program_id(0), pl.program_id(1)
        dff = _dot_nt(dy2_ref[...], wd_ref[...])
        dg = (dff * gs_ref[...].astype(F32)).astype(BF)
        du = (dff * us_ref[...].astype(F32)).astype(BF)
        dgu = jnp.concatenate([dg[:, 0:FF_PAD], du[:, 0:FF_PAD], dg[:, FF_PAD:_FB], du[:, FF_PAD:_FB]], 1)
        dgu_ref[...] = dgu
        part = jnp.dot(dgu, wgu_ref[...], preferred_element_type=F32)

        @pl.when(j == 0)
        def _():
            acc[...] = part

        @pl.when(j > 0)
        def _():
            acc[...] += part

        @pl.when((i == 0) & (j == 0))
        def _():
            dgf_ref[...] = jnp.zeros(dgf_ref.shape, F32)
            dgp_ref[...] = jnp.zeros(dgp_ref.shape, F32)

        @pl.when(j == nf - 1)
        def _():
            nh, rh = _rms(h1_ref[...], D)
            dh, dgf = _rms_bwd(nh, rh, gf_ref[...], acc[...], D)
            dh1 = dh2_ref[...] + dh
            dh1_ref[...] = dh1
            dgf_ref[...] += dgf
            ny, ry = _rms(y1_ref[...], D)
            dy1, dgp = _rms_bwd(ny, ry, gp_ref[...], dh1, D)
            dy1_ref[...] = dy1.astype(BF)
            dgp_ref[...] += dgp

    tok = lambda n: pl.BlockSpec((tm, n), lambda i, j: (i, 0))
    col = pl.BlockSpec((tm, _FB), lambda i, j: (i, j))
    return _pcall(
        body, "ffn_bwd_x", (T // tm, nf),
        [tok(D), col, col, pl.BlockSpec((2 * _FB, D), lambda i, j: (j, 0)), pl.BlockSpec((_FB, D), lambda i, j: (j, 0)),
         tok(D), tok(D), tok(D), _full((1, D)), _full((1, D))],
        [pl.BlockSpec((tm, 2 * _FB), lambda i, j: (i, j)), tok(D), tok(D), _full((1, D)), _full((1, D))],
        [_sds((T, 2 * nf * _FB), BF), _sds((T, D)), _sds((T, D), BF), _sds((1, D)), _sds((1, D))],
        scratch=[pltpu.VMEM((tm, D), F32)],
    )(dy2, gs, us, wgu, wd, h1, y1, dh2, g_fpre, g_post)


def _ffn_bwd_w(z, ffs, dgu, dy2, tm):
    T = z.shape[0]
    nf = ffs.shape[1] // _FB
    nt = T // tm

    def body(z_ref, ff_ref, dgu_ref, dy2_ref, dwgu_ref, dwd_ref, agu, ad):
        i = pl.program_id(1)
        pgu = _dot_tn(dgu_ref[...], z_ref[...])
        pd = _dot_tn(ff_ref[...], dy2_ref[...])

        @pl.when(i == 0)
        def _():
            agu[...] = pgu
            ad[...] = pd

        @pl.when(i > 0)
        def _():
            agu[...] += pgu
            ad[...] += pd

        @pl.when(i == nt - 1)
        def _():
            dwgu_ref[...] = agu[...].astype(BF)
            dwd_ref[...] = ad[...].astype(BF)

    F = nf * _FB
    tok = lambda n: pl.BlockSpec((tm, n), lambda j, i: (i, 0))
    return _pcall(
        body, "ffn_bwd_w", (nf, nt),
        [tok(D), pl.BlockSpec((tm, _FB), lambda j, i: (i, j)), pl.BlockSpec((tm, 2 * _FB), lambda j, i: (i, j)), tok(D)],
        [pl.BlockSpec((2 * _FB, D), lambda j, i: (j, 0)), pl.BlockSpec((_FB, D), lambda j, i: (j, 0))],
        [_sds((2 * F, D), BF), _sds((F, D), BF)],
        scratch=[pltpu.VMEM((2 * _FB, D), F32), pltpu.VMEM((_FB, D), F32)],
    )(z, ffs, dgu, dy2)


def _out_bwd(dy1, mix, o_pad, w_out, g_on, tm):
    T = dy1.shape[0]
    W = N_HEADS * HB

    def body(dy1_ref, mix_ref, o_ref, w_ref, gon_ref, do_ref, dl_ref, dohg_ref, dw_ref, dgon_ref):
        i = pl.program_id(0)
        dy1v = dy1_ref[...]
        dmix = _dot_nt(dy1v, w_ref[...])
        pw = _dot_tn(mix_ref[...], dy1v)

        @pl.when(i == 0)
        def _():
            dw_ref[...] = pw
            dgon_ref[...] = jnp.zeros(dgon_ref.shape, F32)

        @pl.when(i > 0)
        def _():
            dw_ref[...] += pw

        for h in range(N_HEADS):
            sl = slice(h * HB, (h + 1) * HB)
            ov = o_ref[:, sl]
            n, r = _rms(ov, V_DIM)
            do, dg = _rms_bwd(n, r, gon_ref[:, sl], dmix[:, sl], V_DIM)
            dgon_ref[:, sl] += dg
            do_ref[:, sl] = do.astype(BF)
            dl_ref[:, sl] = jnp.broadcast_to(jnp.sum(do * ov, -1, keepdims=True), (tm, HB))
        dohg_ref[...] = dmix[:, W:MIX_W]

    return _pcall(body, "out_bwd", (T // tm,),
                  [_rows(tm, D), _rows(tm, MIX_W), _rows(tm, W), _full((MIX_W, D)), _full((1, W))],
                  [_rows(tm, W), _rows(tm, W), _rows(tm, G_W), _full((MIX_W, D)), _full((1, W))],
                  [_sds((T, W), BF), _sds((T, W)), _sds((T, G_W)), _sds((MIX_W, D)), _sds((1, W))],
                  )(dy1, mix, o_pad, w_out, g_on)


def _flash_bwd(q, k, v, do, lse, dl, tq, exchange=None):
    T = q.shape[0]
    nq = T // tq
    scale = QK_DIM ** -0.5
    hp = _AH
    W = hp * HB

    def body(k_ref, v_ref, q_ref, do_ref, lse_ref, dl_ref, dk_ref, dv_ref, dq_ref):
        j = pl.program_id(1)

        @pl.when(j == 0)
        def _():
            dq_ref[...] = jnp.zeros(dq_ref.shape, F32)

        def blk(i, carry, masked):
            sl = pl.ds(pl.multiple_of(i * tq, tq), tq)
            out = []
            for h in range(hp):
                ls = slice(h * HB, (h + 1) * HB)
                dk, dv = carry[h]
                kv, vv = k_ref[:, ls], v_ref[:, ls]
                qv, dov = q_ref[sl, ls], do_ref[sl, ls]
                s = _dot_nt(qv, kv) * scale
                if masked:
                    r = lax.broadcasted_iota(jnp.int32, (tq, tq), 0)
                    c = lax.broadcasted_iota(jnp.int32, (tq, tq), 1)
                    s = jnp.where(c <= r, s, NEG)
                p = jnp.exp(s - lse_ref[sl, h * HB:h * HB + 1])
                ds = p * (_dot_nt(dov, vv) - dl_ref[sl, h * HB:h * HB + 1]) * scale
                dq_ref[sl, ls] += _dot(ds, kv)
                out.append((dk + _dot_tn(ds, qv), dv + _dot_tn(p, dov)))
            return tuple(out)

        zero = jnp.zeros((tq, HB), F32)
        carry = blk(j, tuple((zero, zero) for _ in range(hp)), True)
        res = lax.fori_loop(j + 1, nq, lambda i, cr: blk(i, cr, False), carry)
        for h in range(hp):
            ls = slice(h * HB, (h + 1) * HB)
            dk_ref[:, ls] = res[h][0]
            dv_ref[:, ls] = res[h][1]

    tile = pl.BlockSpec((tq, W), lambda h, j: (j, h))
    whole = pl.BlockSpec((T, W), lambda h, j: (0, h))
    return _pcall(body, "flash_bwd", (N_HEADS // hp, nq), [tile, tile, whole, whole, whole, whole],
                  [tile, tile, whole], [_sds((T, N_HEADS * HB))] * 3, exchange=exchange)(k, v, q, do, lse, dl)


def _mla_prep_bwd(xp, tabs, dq, dk, dv, g_q, g_kv, w_uq, w_uk, w_uv, tm):
    T = xp.shape[0]
    W = N_HEADS * HB

    def body(xp_ref, ta_ref, tb1_ref, tb2_ref, dq_ref, dk_ref, dv_ref, gq_ref, gkv_ref, wuq_ref, wuk_ref, wuv_ref,
             dxp_ref, dwuq_ref, dwuk_ref, dwuv_ref, dgq_ref, dgkv_ref, dqp):
        i = pl.program_id(0)
        ta, tb1, tb2 = ta_ref[...], tb1_ref[...], tb2_ref[...]
        nq, rq = _rms(xp_ref[:, 0:Q_RANK], Q_RANK)
        nkv, rkv = _rms(xp_ref[:, Q_RANK:Q_RANK + KV_RANK], KV_RANK)
        dkr = jnp.zeros((tm, HB), F32)
        for h in range(N_HEADS):
            sl = slice(h * HB, (h + 1) * HB)
            dqp[:, sl] = _unrope(dq_ref[:, sl], ta, tb1, tb2).astype(BF)
            dkr = dkr + dk_ref[:, sl]
        dkr = pltpu.roll(_unrope(dkr, ta, tb1, tb2), HB - NOPE, 1)
        lane = lax.broadcasted_iota(jnp.int32, (tm, HB), 1)
        dxp_ref[:, Q_RANK + KV_RANK:MLA_IN] = jnp.where(lane < ROPE, dkr, 0.0)
        dqpv = dqp[...]
        dkv, dvv = dk_ref[...].astype(BF), dv_ref[...].astype(BF)
        nqs = (nq * gq_ref[...]).astype(BF)
        nkvs = (nkv * gkv_ref[...]).astype(BF)
        pq, pk, pv = _dot_tn(nqs, dqpv), _dot_tn(nkvs, dkv), _dot_tn(nkvs, dvv)
        dcq, dgq = _rms_bwd(nq, rq, gq_ref[...], _dot_nt(dqpv, wuq_ref[...]), Q_RANK)
        dckv, dgkv = _rms_bwd(nkv, rkv, gkv_ref[...], _dot_nt(dkv, wuk_ref[...]) + _dot_nt(dvv, wuv_ref[...]), KV_RANK)
        dxp_ref[:, 0:Q_RANK] = dcq
        dxp_ref[:, Q_RANK:Q_RANK + KV_RANK] = dckv

        @pl.when(i == 0)
        def _():
            dwuq_ref[...] = pq
            dwuk_ref[...] = pk
            dwuv_ref[...] = pv
            dgq_ref[...] = dgq
            dgkv_ref[...] = dgkv

        @pl.when(i > 0)
        def _():
            dwuq_ref[...] += pq
            dwuk_ref[...] += pk
            dwuv_ref[...] += pv
            dgq_ref[...] += dgq
            dgkv_ref[...] += dgkv

    tab = _rows(tm, HB)
    return _pcall(
        body, "mla_prep_bwd", (T // tm,),
        [_rows(tm, MLA_IN), tab, tab, tab, _rows(tm, W), _rows(tm, W), _rows(tm, W), _full((1, Q_RANK)),
         _full((1, KV_RANK)), _full((Q_RANK, W)), _full((KV_RANK, W)), _full((KV_RANK, W))],
        [_rows(tm, MLA_IN), _full((Q_RANK, W)), _full((KV_RANK, W)), _full((KV_RANK, W)), _full((1, Q_RANK)),
         _full((1, KV_RANK))],
        [_sds((T, MLA_IN)), _sds((Q_RANK, W)), _sds((KV_RANK, W)), _sds((KV_RANK, W)), _sds((1, Q_RANK)),
         _sds((1, KV_RANK))],
        scratch=[pltpu.VMEM((tm, W), BF)],
    )(xp, *tabs, dq, dk, dv, g_q, g_kv, w_uq, w_uk, w_uv)


def _hgrn_bwd(xp, o_raw, s_all, d_out, lb_logits, g_hn, exchange=None):
    T = xp.shape[0]
    tb = min(_TB, T)
    ncb = tb // CHUNK
    nb = T // tb
    hp = _HP
    W = hp * G_DIM

    def body(hq_ref, hf_ref, hi_ref, hg_ref, o_ref, sall_ref, dout_ref, lbl_ref, ghn_ref,
             dhq_ref, dhf_ref, dhi_ref, dhg_ref, dlbl_ref, dghn_ref, dst_ref, b_s, acc_lb, acc_g):
        t = pl.program_id(1)
        lb_all = _lower_bound(lbl_ref)

        @pl.when(t == 0)
        def _():
            dst_ref[...] = jnp.zeros(dst_ref.shape, F32)
            acc_lb[...] = jnp.zeros(acc_lb.shape, F32)
            acc_g[...] = jnp.zeros(acc_g.shape, F32)

        row = lax.broadcasted_iota(jnp.int32, (CHUNK, CHUNK), 0)
        col = lax.broadcasted_iota(jnp.int32, (CHUNK, CHUNK), 1)
        tri = (col <= row).astype(BF)
        tri_t = (col >= row).astype(BF)
        codes = _intra_codes(SUB_BWD)
        last = lax.broadcasted_iota(jnp.int32, (CHUNK, G_DIM), 0) == CHUNK - 1

        def chunk(cc, carry):
            c = ncb - 1 - cc
            sl = pl.ds(pl.multiple_of(c * CHUNK, CHUNK), CHUNK)
            for h in range(hp):
                ls = slice(h * G_DIM, (h + 1) * G_DIM)
                lb, ghn = lb_all[:, ls], ghn_ref[:, ls]
                hq, hg = hq_ref[sl, ls], hg_ref[sl, ls]
                q, k, f, lf, sig, sq = _gates(hq, hf_ref[sl, ls], lb)
                v = hi_ref[sl, ls]
                b2 = _tri_mm(tri, lf) * LOG2E
                b_s[h] = b2
                st = sall_ref[c, h]
                dstn = dst_ref[h]
                o = o_ref[sl, ls]
                dout = dout_ref[sl, ls]
                n, r = _rms(o, G_DIM)
                sg = _sigmoid(hg)
                dhg_ref[sl, ls] = dout * (n * ghn) * _dsilu(hg, sg)
                do, dg = _rms_bwd(n, r, ghn, dout * (hg * sg), G_DIM)
                acc_g[:, ls] += dg
                eb = jnp.exp2(b2)
                bl = b_s[h, CHUNK - 1:CHUNK, :]
                ebl = jnp.exp2(bl)
                ekd = jnp.exp2(bl - b2)
                kd = k * ekd
                a, dq_i, dk_i = _intra(q, k, b2, b_s.at[h], codes, _dot_nt(do, v))
                dhi_ref[sl, ls] = _dot_tn(a, do) + _dot_nt(kd, dstn)
                dk_state = _dot(v, dstn) * ekd
                dq = dq_i + _dot(do, st) * eb
                dk = dk_i + dk_state
                dbl = jnp.sum(k * dk_state, 0, keepdims=True) + ebl * jnp.sum(dstn * st, 0, keepdims=True)
                db = q * dq - k * dk + jnp.where(last, dbl, 0.0)
                df = _tri_mm(tri_t, db) / f - dk
                dhf_ref[sl, ls] = df * (1.0 - lb) * sig * (1.0 - sig)
                acc_lb[:, ls] += jnp.sum(df * (1.0 - sig), 0, keepdims=True)
                dhq_ref[sl, ls] = dq * _dsilu(hq, sq)
                dst_ref[h] = dstn * ebl + _dot_tn(do, q * eb)
            return carry

        lax.fori_loop(0, ncb, chunk, 0)

        @pl.when(t == nb - 1)
        def _():
            dl0 = acc_lb[...] * lb_all * (1.0 - lb_all)
            dlbl_ref[0:1, :] = dl0
            dlbl_ref[1:2, :] = -dl0
            dghn_ref[...] = acc_g[...]

    col_blk = lambda j: pl.BlockSpec((tb, W), lambda p, t: (nb - 1 - t, j * (G_HEADS // hp) + p))
    head = pl.BlockSpec((tb, W), lambda p, t: (nb - 1 - t, p))
    two = pl.BlockSpec((2, W), lambda p, t: (0, p))
    one = pl.BlockSpec((1, W), lambda p, t: (0, p))
    res = _pcall(
        body, "hgrn_bwd", (G_HEADS // hp, nb),
        [col_blk(0), col_blk(1), col_blk(2), col_blk(3), head,
         pl.BlockSpec((ncb, hp, G_DIM, G_DIM), lambda p, t: (nb - 1 - t, p, 0, 0)), head, two, one],
        [head, head, head, head, two, one],
        [_sds((T, G_W))] * 4 + [_sds((2, G_W)), _sds((1, G_W))],
        scratch=[pltpu.VMEM((hp, G_DIM, G_DIM), F32), pltpu.VMEM((hp, CHUNK, G_DIM), F32),
                 pltpu.VMEM((1, W), F32), pltpu.VMEM((1, W), F32)], exchange=exchange,
    )(xp, xp, xp, xp, o_raw, s_all, d_out, lb_logits, g_hn)
    return res


def _in_bwd_x(x, dxp_m, dxp_h, dh1, w_in_al, g_pre, tm, exchange=None):
    T = x.shape[0]

    def body(x_ref, dm_ref, d0_ref, d1_ref, d2_ref, d3_ref, dh1_ref, w_ref, g_ref, dx_ref, dg_ref):
        i = pl.program_id(0)
        du = _dot_nt(dm_ref[...], w_ref[:, 0:MLA_IN])
        for j, d_ref in enumerate((d0_ref, d1_ref, d2_ref, d3_ref)):
            du = du + _dot_nt(d_ref[...], w_ref[:, MLA_IN + j * G_W:MLA_IN + (j + 1) * G_W])
        nx, r = _rms(x_ref[...], D)
        dx, dg = _rms_bwd(nx, r, g_ref[...], du, D)
        dx_ref[...] = dh1_ref[...] + dx

        @pl.when(i == 0)
        def _():
            dg_ref[...] = dg

        @pl.when(i > 0)
        def _():
            dg_ref[...] += dg

    return _pcall(body, "in_bwd_x", (T // tm,),
                  [_rows(tm, D), _rows(tm, MLA_IN)] + [_rows(tm, G_W)] * 4 + [_rows(tm, D), _full((D, XP_W)), _full((1, D))],
                  [_rows(tm, D), _full((1, D))], [_sds((T, D)), _sds((1, D))], exchange=exchange,
                  )(x, dxp_m, *dxp_h, dh1, w_in_al, g_pre)


def _aligned_col(c):
    return jnp.where(c < Q_RANK + KV_RANK + ROPE, c, c + (KR_PAD - ROPE))


def _align_w_in(g_in):
    tile = 384
    kr_end = Q_RANK + KV_RANK + ROPE

    def body(g_ref, o_ref, gp):
        gp[...] = jnp.zeros(gp.shape, BF)
        for j in range(N_DEV):
            gp[j, :, 0:IN_SH] = g_ref[j]
        r = lax.broadcasted_iota(jnp.int32, (tile, tile), 0)
        c = lax.broadcasted_iota(jnp.int32, (tile, tile), 1)
        for t in range(XP_W // tile):
            lo, hi = t * tile, (t + 1) * tile
            cols = [a if a < kr_end else a - (KR_PAD - ROPE) for a in (lo, hi - 1)]
            acc = jnp.zeros((D, tile), F32)
            for j in range(cols[0] // IN_SH, cols[-1] // IN_SH + 1):
                sel = (r < IN_SH) & (_aligned_col(j * IN_SH + r) == lo + c)
                acc = acc + jnp.dot(gp[j], sel.astype(BF), preferred_element_type=F32)
            o_ref[:, lo:hi] = acc.astype(BF)

    vm = pl.BlockSpec(memory_space=pltpu.VMEM)
    return pl.pallas_call(
        body, name="align_w_in", in_specs=[vm], out_specs=vm, out_shape=_sds((D, XP_W), BF),
        scratch_shapes=[pltpu.VMEM((N_DEV, D, tile), BF)],
        compiler_params=pltpu.CompilerParams(vmem_limit_bytes=_VMEM_LIMIT))(g_in)


def _in_bwd_w(u, dxp_m, dxp_h, tm):
    T = u.shape[0]
    nt = T // tm
    win = 640

    def body(u_ref, dm_ref, d0_ref, d1_ref, d2_ref, d3_ref, o_ref, acc):
        i = pl.program_id(0)
        ut = u_ref[...].T
        parts = [(0, MLA_IN, dm_ref)] + [(MLA_IN + j * G_W, G_W, d) for j, d in enumerate((d0_ref, d1_ref, d2_ref, d3_ref))]

        @pl.when(i == 0)
        def _():
            for lo, n, d in parts:
                acc[:, lo:lo + n] = jnp.dot(ut, d[...].astype(BF), preferred_element_type=F32)

        @pl.when(i > 0)
        def _():
            for lo, n, d in parts:
                acc[:, lo:lo + n] += jnp.dot(ut, d[...].astype(BF), preferred_element_type=F32)

        @pl.when(i == nt - 1)
        def _():
            wide = 384
            r = lax.broadcasted_iota(jnp.int32, (win, wide), 0)
            c = lax.broadcasted_iota(jnp.int32, (win, wide), 1)
            kr_end = Q_RANK + KV_RANK + ROPE
            for j in range(N_DEV):
                first = j * IN_SH if j * IN_SH < kr_end else j * IN_SH + (KR_PAD - ROPE)
                lo = min(first // HB * HB, XP_W - win)
                sel = (c < IN_SH) & (_aligned_col(j * IN_SH + c) == lo + r)
                res = jnp.dot(acc[:, lo:lo + win].astype(BF), sel.astype(BF), preferred_element_type=F32)
                o_ref[j] = res[:, 0:IN_SH].astype(BF)

    return _pcall(body, "in_bwd_w", (nt,),
                  [_rows(tm, D), _rows(tm, MLA_IN)] + [_rows(tm, G_W)] * 4,
                  [_full((N_DEV, D, IN_SH))], [_sds((N_DEV, D, IN_SH), BF)],
                  scratch=[pltpu.VMEM((D, XP_W), F32)])(u, dxp_m, *dxp_h)[0]


def _pad_heads(w, width, real):
    lead = w.shape[:-1]
    w = w.reshape(lead + (N_HEADS, real))
    w = jnp.pad(w, [(0, 0)] * len(lead) + [(0, 0), (0, width - real)])
    return w.reshape(lead + (N_HEADS * width,))


def _unpad_heads(w, width, real):
    lead = w.shape[:-1]
    return w.reshape(lead + (N_HEADS, width))[..., :real].reshape(lead + (N_HEADS * real,))


def _rope_tables(positions):
    half = ROPE // 2
    inv_freq = 1.0 / (ROPE_THETA ** (jnp.arange(0, ROPE, 2, dtype=F32) / ROPE))
    ang = positions.astype(F32)[:, None] * inv_freq
    cos, sin = jnp.cos(ang), jnp.sin(ang)
    T = positions.shape[0]
    z = lambda n: jnp.zeros((T, n), F32)
    ta = jnp.concatenate([jnp.ones((T, NOPE), F32), cos, cos, z(HB - QK_DIM)], 1)
    tb1 = jnp.concatenate([z(NOPE), -sin, z(half), z(HB - QK_DIM)], 1)
    tb2 = jnp.concatenate([z(NOPE), z(half), sin, z(HB - QK_DIM)], 1)
    return ta, tb1, tb2


def kernel(x, positions, attn_pre_norm, w_in, mla_q_norm, mla_w_uq, mla_kv_norm, mla_w_ukv, mla_out_norm, hgrn_lb_logits, hgrn_out_norm, w_out, attn_post_norm, ffn_pre_norm, w_gate, w_up, w_down, ffn_post_norm, loss_target, m_attn_pre_norm, m_w_in, m_mla_q_norm, m_mla_w_uq, m_mla_kv_norm, m_mla_w_ukv, m_mla_out_norm, m_hgrn_lb_logits, m_hgrn_out_norm, m_w_out, m_attn_post_norm, m_ffn_pre_norm, m_w_gate, m_w_up, m_w_down, m_ffn_post_norm, v_attn_pre_norm, v_w_in, v_mla_q_norm, v_mla_w_uq, v_mla_kv_norm, v_mla_w_ukv, v_mla_out_norm, v_hgrn_lb_logits, v_hgrn_out_norm, v_w_out, v_attn_post_norm, v_ffn_pre_norm, v_w_gate, v_w_up, v_w_down, v_ffn_post_norm):
    T = x.shape[1]
    tm = min(_TM, T)
    tq = min(_TQ, T)
    xs, tgt = x[0], loss_target[0]
    uq_sh = (Q_RANK // N_DEV, N_HEADS * QK_DIM)

    b_in, b_uq, b_out, b_gu, b_d = _cast_shards(
        w_in[0], mla_w_uq[0].reshape(uq_sh), w_out[0], w_gate[0].T, w_up[0].T, w_down[0])
    g_in, g_uq = _gather_two_level("ag_first", [b_in, b_uq])
    w_in_al = _align_w_in(g_in)
    w_uq_p = _pad_heads(g_uq.reshape(Q_RANK, N_HEADS * QK_DIM), HB, QK_DIM)
    w_ukv = mla_w_ukv[0].astype(BF)
    w_uk_p = _pad_heads(w_ukv[..., :NOPE].reshape(KV_RANK, N_HEADS * NOPE), HB, NOPE)
    w_uv_p = _pad_heads(w_ukv[..., NOPE:].reshape(KV_RANK, N_HEADS * V_DIM), HB, V_DIM)
    g_on_p = _pad_heads(mla_out_norm, HB, V_DIM)
    tabs = _rope_tables(positions[0])

    xp_m, xp_h, u = _fwd_in(xs, attn_pre_norm, w_in_al, tm)
    q_att, k_att, v_att = _mla_prep(xp_m, tabs, mla_q_norm, mla_kv_norm, w_uq_p, w_uk_p, w_uv_p, tm)
    o_hgrn, o_raw, s_all, g_out, wd = _hgrn_fwd(xp_h, hgrn_lb_logits, hgrn_out_norm, ([GATHER, GATHER], [b_out, b_d]))
    wd = wd.reshape(N_DEV * FF_PAD, D)
    o_pad, lse, wgu = _flash_fwd(q_att, k_att, v_att, tq, ([GATHER], [b_gu]))
    wgu = wgu.reshape(N_DEV * 2 * FF_PAD, D)
    w_out_full = g_out.reshape(D, D)
    w_out_mla = jnp.pad(w_out_full[:N_HEADS * V_DIM].reshape(N_HEADS, V_DIM, D), ((0, 0), (0, HB - V_DIM), (0, 0)))
    w_out_p = jnp.concatenate([w_out_mla.reshape(N_HEADS * HB, D), w_out_full[N_HEADS * V_DIM:]], 0)
    h1, y1, z, mix = _fwd_out(o_pad, o_hgrn, xs, g_on_p, w_out_p, attn_post_norm, ffn_pre_norm, tm)
    gs, us, ffs, dh2, dy2, d_fpost, loss_row = _ffn_fwd(z, wgu, wd, h1, tgt, ffn_post_norm, tm)

    dgu, dh1, dy1, d_fpre, d_post = _ffn_bwd_x(dy2, gs, us, wgu, wd, h1, y1, dh2, ffn_pre_norm, attn_post_norm, tm)
    dwgu, dwd = _ffn_bwd_w(z, ffs, dgu, dy2, tm)
    do_pad, dl, d_ohg, dw_out_p, d_on_p = _out_bwd(dy1, mix, o_pad, w_out_p, g_on_p, tm)
    dw_out_mla = dw_out_p[:N_HEADS * HB].reshape(N_HEADS, HB, D)[:, :V_DIM].reshape(N_HEADS * V_DIM, D)
    dw_out = jnp.concatenate([dw_out_mla, dw_out_p[N_HEADS * HB:]], 0).reshape(N_DEV, D // N_DEV, D).astype(BF)
    dk_att, dv_att, dq_att, p_gu, p_d, p_out = _flash_bwd(
        q_att, k_att, v_att, do_pad, lse, dl, tq,
        ([SCATTER] * 3, [dwgu.reshape(N_DEV, 2 * FF_PAD, D), dwd.reshape(N_DEV, FF_PAD, D), dw_out]))
    dxp_m, dw_uq_p, dw_uk_p, dw_uv_p, d_gq, d_gkv = _mla_prep_bwd(
        xp_m, tabs, dq_att, dk_att, dv_att, mla_q_norm, mla_kv_norm, w_uq_p, w_uk_p, w_uv_p, tm)
    dw_uq = _unpad_heads(dw_uq_p, HB, QK_DIM).reshape((N_DEV,) + uq_sh).astype(BF)
    dw_ukv = jnp.concatenate([_unpad_heads(dw_uk_p, HB, NOPE).reshape(KV_RANK, N_HEADS, NOPE),
                              _unpad_heads(dw_uv_p, HB, V_DIM).reshape(KV_RANK, N_HEADS, V_DIM)], -1)
    *dxp_h, d_lbl, d_ghn, p_uq, dw_ukv_all = _hgrn_bwd(
        xp_h, o_raw, s_all, d_ohg, hgrn_lb_logits, hgrn_out_norm,
        ([SCATTER, GATHER], [dw_uq, dw_ukv.reshape(KV_RANK, N_HEADS * HB)]))
    dw_in = _in_bwd_w(u, dxp_m, dxp_h, tm)
    grad_x, d_pre, p_in = _in_bwd_x(xs, dxp_m, dxp_h, dh1, w_in_al, attn_pre_norm, tm, ([SCATTER], [dw_in]))
    d_on = _unpad_heads(d_on_p, HB, V_DIM)

    ukv2 = lambda a: a.reshape(KV_RANK, N_HEADS * HB)
    vecs = [d_pre, d_gq, d_gkv, d_on, d_lbl, d_ghn, d_post, d_fpre, d_fpost, loss_row]
    small_w = [attn_pre_norm, mla_q_norm, mla_kv_norm, ukv2(mla_w_ukv), mla_out_norm, hgrn_lb_logits, hgrn_out_norm,
               attn_post_norm, ffn_pre_norm, ffn_post_norm]
    small_m = [m_attn_pre_norm, m_mla_q_norm, m_mla_kv_norm, ukv2(m_mla_w_ukv), m_mla_out_norm, m_hgrn_lb_logits,
               m_hgrn_out_norm, m_attn_post_norm, m_ffn_pre_norm, m_ffn_post_norm]
    small_v = [v_attn_pre_norm, v_mla_q_norm, v_mla_kv_norm, ukv2(v_mla_w_ukv), v_mla_out_norm, v_hgrn_lb_logits,
               v_hgrn_out_norm, v_attn_post_norm, v_ffn_pre_norm, v_ffn_post_norm]
    rall = _final_exchange(vecs)
    s_g, s_d, s_m, s_v, loss_all = _small_adam(rall, dw_ukv_all, 3, small_w, small_m, small_v)
    r_in = _shard_adam("adam_w_in", p_in, w_in[0], m_w_in[0], v_w_in[0], 256)
    r_uq = _shard_adam("adam_w_uq", p_uq, mla_w_uq[0].reshape(uq_sh), m_mla_w_uq[0].reshape(uq_sh),
                       v_mla_w_uq[0].reshape(uq_sh), uq_sh[0])
    r_out = _shard_adam("adam_w_out", p_out, w_out[0], m_w_out[0], v_w_out[0], D // N_DEV)
    r_g = [a.T for a in _shard_adam("adam_w_gate", p_gu, w_gate[0].T, m_w_gate[0].T, v_w_gate[0].T, 32)]
    r_u = [a.T for a in _shard_adam("adam_w_up", p_gu, w_up[0].T, m_w_up[0].T, v_w_up[0].T, 32, FF_PAD)]
    r_d = _shard_adam("adam_w_down", p_d, w_down[0], m_w_down[0], v_w_down[0], FF_SH // 2)

    loss = loss_all[0, 0]

    def assemble(big, small):
        b_in, b_uq, b_out, b_g, b_u, b_d = big
        return [small[0], b_in[None], small[1], b_uq.reshape(mla_w_uq.shape), small[2],
                small[3].reshape(mla_w_ukv.shape), small[4], small[5], small[6], b_out[None], small[7], small[8],
                b_g[None], b_u[None], b_d[None], small[9]]

    outs = [loss, grad_x[None]]
    for idx, small in enumerate((s_g, s_d, s_m, s_v)):
        outs += assemble([r[idx] for r in (r_in, r_uq, r_out, r_g, r_u, r_d)], small)
    return tuple(outs)
```

```python
import jax
import jax.numpy as jnp
from jax import lax
from jax.experimental import pallas as pl
from jax.experimental.pallas import tpu as pltpu

BF = jnp.bfloat16
F32 = jnp.float32
MESH = pl.DeviceIdType.MESH

N_DEV = 8
D = 1024
EPS = 1e-6
LOG2E = 1.4426950408889634
ROPE_THETA = 10000.0
N_HEADS = 8
HB = 128
NOPE = 64
ROPE = 32
V_DIM = 64
QK_DIM = NOPE + ROPE
Q_RANK = 384
KV_RANK = 128
KR_PAD = 128
MLA_IN = Q_RANK + KV_RANK + KR_PAD
G_HEADS = 4
G_DIM = 128
G_W = G_HEADS * G_DIM
CHUNK = 64
SUB_FWD = 16
SUB_BWD = 16
XP_W = MLA_IN + 4 * G_W
IN_SH = 324
IN_W = N_DEV * IN_SH
FF_SH = 352
FF_PAD = 384
MIX_W = N_HEADS * HB + G_W

ADAM_LR = 0.001
ADAM_B1 = 0.9
ADAM_B2 = 0.999
ADAM_EPS = 1e-08
ADAM_WD = 0.01
ADAM_STEP = 10

_TM = 512
_TQ = 512
_AH = 2
_FB = 768
_TB = 1024
_HP = 4
_VMEM_LIMIT = 56 * 1024 * 1024
NEG = -1e30


def _dot(a, b):
    return jnp.dot(a.astype(BF), b.astype(BF), preferred_element_type=F32)


def _dot_nt(a, b):
    return lax.dot_general(a.astype(BF), b.astype(BF), (((1,), (1,)), ((), ())), preferred_element_type=F32)


def _dot_tn(a, b):
    return lax.dot_general(a.astype(BF), b.astype(BF), (((0,), (0,)), ((), ())), preferred_element_type=F32)


def _sigmoid(x):
    return 1.0 / (1.0 + jnp.exp(-x))


def _rms(x, n):
    r = lax.rsqrt(jnp.sum(x * x, -1, keepdims=True) * (1.0 / n) + EPS)
    return x * r, r


def _rms_bwd(nx, r, g, dy, n):
    dg = jnp.sum(dy * nx, 0, keepdims=True)
    dn = dy * g
    dx = r * (dn - nx * (jnp.sum(dn * nx, -1, keepdims=True) * (1.0 / n)))
    return dx, dg


def _adamw(w, g, m, v):
    m2 = ADAM_B1 * m + (1.0 - ADAM_B1) * g
    v2 = ADAM_B2 * v + (1.0 - ADAM_B2) * (g * g)
    m_hat = m2 / (1.0 - ADAM_B1 ** ADAM_STEP)
    v_hat = v2 / (1.0 - ADAM_B2 ** ADAM_STEP)
    delta = -ADAM_LR * (m_hat / (jnp.sqrt(v_hat) + ADAM_EPS) + ADAM_WD * w)
    return delta, m2, v2


def _pcall(body, name, grid, in_specs, out_specs, out_shape, scratch=(), exchange=None):
    scratch = list(scratch)
    extra = ()
    if exchange is not None:
        kinds, extra = exchange
        in_specs, out_specs, out_shape = list(in_specs), list(out_specs), list(out_shape)
        n_in, n_out, n_scr, n_x = len(in_specs), len(out_specs), len(scratch), len(extra)
        inner = body

        def body(*refs):
            ins, rest = refs[:n_in], refs[n_in:]
            x_src, rest = rest[:n_x], rest[n_x:]
            outs, rest = rest[:n_out], rest[n_out:]
            x_dst, rest = rest[:n_x], rest[n_x:]
            ex = _Exchange(kinds, x_src, x_dst, *rest[n_scr:])
            first = pl.program_id(0) == 0
            last = pl.program_id(0) == grid[0] - 1
            for a in range(1, len(grid)):
                first = first & (pl.program_id(a) == 0)
                last = last & (pl.program_id(a) == grid[a] - 1)
            pl.when(first)(ex.start)
            inner(*ins, *outs, *rest[:n_scr])
            pl.when(last)(ex.wait)

        in_specs += [_HBM] * n_x
        out_specs += [_HBM] * n_x
        out_shape += _exchange_shapes(kinds, extra)
        scratch += _exchange_sems(n_x)
    call = pl.pallas_call(
        body, name=name, grid=grid, in_specs=in_specs, out_specs=out_specs, out_shape=out_shape,
        scratch_shapes=scratch,
        compiler_params=pltpu.CompilerParams(
            dimension_semantics=("arbitrary",) * len(grid), vmem_limit_bytes=_VMEM_LIMIT))
    return lambda *operands: call(*operands, *extra)


def _full(shape):
    return pl.BlockSpec(shape, lambda *_: (0,) * len(shape))


def _rows(tm, n):
    return pl.BlockSpec((tm, n), lambda i, *_: (i, 0))


def _sds(shape, dtype=F32):
    return jax.ShapeDtypeStruct(shape, dtype)


def _peer(k, x, y, c):
    px = 1 - x if (k >> 2) & 1 else x
    py = 1 - y if (k >> 1) & 1 else y
    pc = 1 - c if k & 1 else c
    return px, py, pc


GATHER, SCATTER = "gather", "scatter"


class _Exchange:
    def __init__(self, kinds, srcs, dsts, send_sems, recv_sems, loc_sems):
        self.kinds, self.srcs, self.dsts = kinds, srcs, dsts
        self.send_sems, self.recv_sems, self.loc_sems = send_sems, recv_sems, loc_sems
        self.x, self.y, self.c = lax.axis_index("x"), lax.axis_index("y"), lax.axis_index("c")
        self.me = 4 * self.x + 2 * self.y + self.c

    def _src(self, w, slot):
        return self.srcs[w] if self.kinds[w] == GATHER else self.srcs[w].at[slot]

    def _dst(self, w, slot):
        return self.dsts[w].at[slot]

    def _copy(self, w, k, outgoing):
        px, py, pc = _peer(k, self.x, self.y, self.c)
        pid = 4 * px + 2 * py + pc
        return pltpu.make_async_remote_copy(
            src_ref=self._src(w, pid if outgoing else self.me),
            dst_ref=self._dst(w, self.me if outgoing else pid),
            send_sem=self.send_sems.at[w, k - 1], recv_sem=self.recv_sems.at[w, k - 1],
            device_id=(px, py, pc), device_id_type=MESH)

    def _local(self, w):
        return pltpu.make_async_copy(self._src(w, self.me), self._dst(w, self.me), self.loc_sems.at[w])

    def start(self):
        for w in range(len(self.srcs)):
            self._local(w).start()
            for k in range(1, N_DEV):
                self._copy(w, k, True).start()

    def wait(self):
        for w in range(len(self.srcs)):
            self._local(w).wait()
            for k in range(1, N_DEV):
                self._copy(w, k, False).wait_recv()
        for w in range(len(self.srcs)):
            for k in range(1, N_DEV):
                self._copy(w, k, True).wait_send()


def _exchange_sems(n_w):
    return [pltpu.SemaphoreType.DMA((n_w, N_DEV - 1)), pltpu.SemaphoreType.DMA((n_w, N_DEV - 1)),
            pltpu.SemaphoreType.DMA((n_w,))]


def _exchange_shapes(kinds, srcs):
    return [_sds(((N_DEV,) if kd == GATHER else ()) + tuple(s.shape), s.dtype) for kd, s in zip(kinds, srcs)]


_HBM = pl.BlockSpec(memory_space=pl.ANY)


def _cast_shards(w_in, w_uq, w_out, w_gate_t, w_up_t, w_down):
    shapes = [(D, IN_SH), (Q_RANK // N_DEV, N_HEADS * QK_DIM), (D // N_DEV, D), (2 * FF_PAD, D), (FF_PAD, D)]

    def body(win, wuq, wout, wg, wu, wd, sin_, suq, sout, sgu, sd):
        sin_[...] = win[...].astype(BF)
        suq[...] = wuq[...].astype(BF)
        sout[...] = wout[...].astype(BF)
        sgu[...] = jnp.zeros(sgu.shape, BF)
        sgu[0:FF_SH, :] = wg[...].astype(BF)
        sgu[FF_PAD:FF_PAD + FF_SH, :] = wu[...].astype(BF)
        sd[...] = jnp.zeros(sd.shape, BF)
        sd[0:FF_SH, :] = wd[...].astype(BF)

    vm = pl.BlockSpec(memory_space=pltpu.VMEM)
    return pl.pallas_call(
        body, name="cast_shards", in_specs=[vm] * 6, out_specs=[vm] * 5,
        out_shape=[_sds(s, BF) for s in shapes],
        compiler_params=pltpu.CompilerParams(vmem_limit_bytes=_VMEM_LIMIT),
    )(w_in, w_uq, w_out, w_gate_t, w_up_t, w_down)


def _gather_two_level(name, srcs):
    n_w = len(srcs)

    def body(*refs):
        src, dst = refs[:n_w], refs[n_w:2 * n_w]
        send_sems, recv_sems, loc_sems = refs[2 * n_w:]
        x, y, c = lax.axis_index("x"), lax.axis_index("y"), lax.axis_index("c")
        me, sibling = (x, y, c), (x, y, 1 - c)
        chips = [(1 - x, y), (x, 1 - y), (1 - x, 1 - y)]
        slot = lambda p: 4 * p[0] + 2 * p[1] + p[2]

        def copy(w, k, block, to, own=False):
            return pltpu.make_async_remote_copy(
                src_ref=src[w] if own else dst[w].at[slot(block)], dst_ref=dst[w].at[slot(block)],
                send_sem=send_sems.at[w, k], recv_sem=recv_sems.at[w, k], device_id=to, device_id_type=MESH)

        local = [pltpu.make_async_copy(src[w], dst[w].at[slot(me)], loc_sems.at[w]) for w in range(n_w)]
        first, passed = [], []
        for w in range(n_w):
            local[w].start()
            first.append(copy(w, 0, me, sibling, own=True))
            first += [copy(w, 1 + j, me, (*chip, c), own=True) for j, chip in enumerate(chips)]
        for cp in first:
            cp.start()
        for w in range(n_w):
            for j, chip in enumerate(chips):
                copy(w, 1 + j, (*chip, c), me).wait_recv()
                passed.append(copy(w, 4 + j, (*chip, c), sibling))
                passed[-1].start()
        for w in range(n_w):
            copy(w, 0, sibling, me).wait_recv()
            for j, chip in enumerate(chips):
                copy(w, 4 + j, (*chip, 1 - c), me).wait_recv()
        for cp in first + passed:
            cp.wait_send()
        for w in range(n_w):
            local[w].wait()

    return pl.pallas_call(
        body, name=name, in_specs=[_HBM] * n_w, out_specs=[_HBM] * n_w,
        out_shape=_exchange_shapes([GATHER] * n_w, srcs), scratch_shapes=_exchange_sems(n_w))(*srcs)


def _row_offsets(arrays):
    offs, rows = [], 0
    for a in arrays:
        offs.append(rows)
        rows += a.shape[0]
    return offs, -(-rows // 8) * 8


def _final_exchange(vecs):
    n_p = len(vecs)
    offs, rows = _row_offsets(vecs)

    def body(*refs):
        g_refs = refs[:n_p]
        rall, pk, send_sems, recv_sems, loc_sem = refs[n_p:]
        x, y, c = lax.axis_index("x"), lax.axis_index("y"), lax.axis_index("c")
        me = 4 * x + 2 * y + c
        pk[...] = jnp.zeros(pk.shape, F32)
        for p in range(n_p):
            r, n = g_refs[p].shape
            pk[offs[p]:offs[p] + r, 0:n] = g_refs[p][...]

        def remote(k):
            return pltpu.make_async_remote_copy(
                src_ref=pk, dst_ref=rall.at[me], send_sem=send_sems.at[k - 1], recv_sem=recv_sems.at[k - 1],
                device_id=_peer(k, x, y, c), device_id_type=MESH)

        def arrival(k):
            px, py, pc = _peer(k, x, y, c)
            return pltpu.make_async_remote_copy(
                src_ref=pk, dst_ref=rall.at[4 * px + 2 * py + pc], send_sem=send_sems.at[k - 1],
                recv_sem=recv_sems.at[k - 1], device_id=(px, py, pc), device_id_type=MESH)

        local = pltpu.make_async_copy(pk, rall.at[me], loc_sem)
        local.start()
        for k in range(1, N_DEV):
            remote(k).start()
        local.wait()
        for k in range(1, N_DEV):
            arrival(k).wait_recv()
        for k in range(1, N_DEV):
            remote(k).wait_send()

    vm = pl.BlockSpec(memory_space=pltpu.VMEM)
    return pl.pallas_call(
        body, name="final_exchange", in_specs=[vm] * n_p, out_specs=vm, out_shape=_sds((N_DEV, rows, D)),
        scratch_shapes=[pltpu.VMEM((rows, D), F32),
                        pltpu.SemaphoreType.DMA((N_DEV - 1,)), pltpu.SemaphoreType.DMA((N_DEV - 1,)),
                        pltpu.SemaphoreType.DMA],
    )(*vecs)


def _small_adam(rall, big_parts, big, ws, ms, vs):
    n_p = len(ws)
    packed = [w for p, w in enumerate(ws) if p != big] + [jax.ShapeDtypeStruct((1, HB), F32)]
    offs, _ = _row_offsets(packed)
    offs = offs[:big] + [None] + offs[big:]

    def total(ref, sl):
        g = ref[(0,) + sl]
        for j in range(1, N_DEV):
            g = g + ref[(j,) + sl]
        return g

    def body(*refs):
        rall_ref, big_ref = refs[:2]
        w_refs, m_refs, v_refs = refs[2:2 + n_p], refs[2 + n_p:2 + 2 * n_p], refs[2 + 2 * n_p:2 + 3 * n_p]
        outs = refs[2 + 3 * n_p:]
        for p in range(n_p):
            r, n = w_refs[p].shape
            if p == big:
                g = total(big_ref, (slice(0, r), slice(0, n)))
            else:
                g = total(rall_ref, (slice(offs[p], offs[p] + r), slice(0, n)))
            delta, m2, v2 = _adamw(w_refs[p][...], g, m_refs[p][...], v_refs[p][...])
            outs[p][...] = g
            outs[n_p + p][...] = delta
            outs[2 * n_p + p][...] = m2
            outs[3 * n_p + p][...] = v2
        outs[4 * n_p][...] = total(rall_ref, (slice(offs[n_p], offs[n_p] + 1), slice(0, HB)))

    vm = pl.BlockSpec(memory_space=pltpu.VMEM)
    res = pl.pallas_call(
        body, name="small_adam", in_specs=[vm] * (2 + 3 * n_p), out_specs=[vm] * (4 * n_p + 1),
        out_shape=[_sds(w.shape) for w in ws] * 4 + [_sds((1, HB))],
        compiler_params=pltpu.CompilerParams(vmem_limit_bytes=_VMEM_LIMIT),
    )(rall, big_parts, *ws, *ms, *vs)
    return res[:n_p], res[n_p:2 * n_p], res[2 * n_p:3 * n_p], res[3 * n_p:4 * n_p], res[4 * n_p]


def _shard_adam(name, parts, w, m, v, tr, row0=0):
    a0, b0 = w.shape
    b = parts.shape[2]
    blk0 = row0 // tr

    def body(p_ref, w_ref, m_ref, v_ref, g_out, d_out, m_out, v_out):
        g = p_ref[0].astype(F32)
        for j in range(1, N_DEV):
            g = g + p_ref[j].astype(F32)
        g = g[:, 0:b0]
        delta, m2, v2 = _adamw(w_ref[...], g, m_ref[...], v_ref[...])
        g_out[...] = g
        d_out[...] = delta
        m_out[...] = m2
        v_out[...] = v2

    blk = pl.BlockSpec((tr, b0), lambda i: (i, 0))
    return _pcall(
        body, name, (a0 // tr,),
        [pl.BlockSpec((N_DEV, tr, b), lambda i: (0, blk0 + i, 0)), blk, blk, blk],
        [blk] * 4, [_sds((a0, b0))] * 4)(parts, w, m, v)


def _fwd_in(x, g_pre, w_in_al, tm):
    T = x.shape[0]

    def body(x_ref, g_ref, w_ref, xm_ref, xh_ref, u_ref):
        nx, _ = _rms(x_ref[...], D)
        u = (nx * g_ref[...]).astype(BF)
        u_ref[...] = u
        xm_ref[...] = jnp.dot(u, w_ref[:, 0:MLA_IN], preferred_element_type=F32)
        xh_ref[...] = jnp.dot(u, w_ref[:, MLA_IN:XP_W], preferred_element_type=F32)

    return _pcall(body, "fwd_in", (T // tm,),
                  [_rows(tm, D), _full((1, D)), _full((D, XP_W))],
                  [_rows(tm, MLA_IN), _rows(tm, 4 * G_W), _rows(tm, D)],
                  [_sds((T, MLA_IN)), _sds((T, 4 * G_W)), _sds((T, D), BF)])(x, g_pre, w_in_al)


def _rope(blk, ta, tb1, tb2):
    return blk * ta + pltpu.roll(blk, HB - ROPE // 2, 1) * tb1 + pltpu.roll(blk, ROPE // 2, 1) * tb2


def _unrope(d, ta, tb1, tb2):
    return d * ta + pltpu.roll(d * tb1, ROPE // 2, 1) + pltpu.roll(d * tb2, HB - ROPE // 2, 1)


def _mla_prep(xp, tabs, g_q, g_kv, w_uq, w_uk, w_uv, tm):
    T = xp.shape[0]
    W = N_HEADS * HB

    def body(xp_ref, ta_ref, tb1_ref, tb2_ref, gq_ref, gkv_ref, wuq_ref, wuk_ref, wuv_ref, q_ref, qs_ref, k_ref, v_ref):
        ta, tb1, tb2 = ta_ref[...], tb1_ref[...], tb2_ref[...]
        nq, _ = _rms(xp_ref[:, 0:Q_RANK], Q_RANK)
        nkv, _ = _rms(xp_ref[:, Q_RANK:Q_RANK + KV_RANK], KV_RANK)
        nkv = (nkv * gkv_ref[...]).astype(BF)
        qpre = _dot(nq * gq_ref[...], wuq_ref[...])
        kpre = jnp.dot(nkv, wuk_ref[...], preferred_element_type=F32)
        v = jnp.dot(nkv, wuv_ref[...], preferred_element_type=F32)
        lane = lax.broadcasted_iota(jnp.int32, (tm, W), 1)
        v_ref[...] = jnp.where((lane & (HB - 1)) == V_DIM, 1.0, v).astype(BF)
        kr = _rope(pltpu.roll(xp_ref[:, Q_RANK + KV_RANK:MLA_IN], NOPE, 1), ta, tb1, tb2)
        for h in range(N_HEADS):
            sl = slice(h * HB, (h + 1) * HB)
            qr = _rope(qpre[:, sl], ta, tb1, tb2)
            q_ref[:, sl] = qr.astype(BF)
            qs_ref[:, sl] = (qr * (QK_DIM ** -0.5 * LOG2E)).astype(BF)
            k_ref[:, sl] = (kpre[:, sl] + kr).astype(BF)

    tab = _rows(tm, HB)
    return _pcall(body, "mla_prep", (T // tm,),
                  [_rows(tm, MLA_IN), tab, tab, tab, _full((1, Q_RANK)), _full((1, KV_RANK)),
                   _full((Q_RANK, W)), _full((KV_RANK, W)), _full((KV_RANK, W))],
                  [_rows(tm, W)] * 4, [_sds((T, W), BF)] * 4)(xp, *tabs, g_q, g_kv, w_uq, w_uk, w_uv)


def _flash_fwd(q, k, v, tq, exchange=None):
    T = q.shape[0]
    hp = _AH
    W = hp * HB

    def body(q_ref, k_ref, v_ref, o_ref, lse_ref):
        i = pl.program_id(1)

        def blk(j, carry, masked):
            st = pl.multiple_of(j * tq, tq)
            out = []
            for h in range(hp):
                ls = slice(h * HB, (h + 1) * HB)
                m, acc = carry[h]
                s = _dot_nt(q_ref[:, ls], k_ref[pl.ds(st, tq), ls])
                if masked:
                    r = lax.broadcasted_iota(jnp.int32, (tq, tq), 0)
                    c = lax.broadcasted_iota(jnp.int32, (tq, tq), 1)
                    s = jnp.where(c <= r, s, NEG)
                m2 = jnp.maximum(m, jnp.max(s, -1, keepdims=True))
                p = jnp.exp2(s - m2)
                out.append((m2, jnp.exp2(m - m2) * acc + _dot(p, v_ref[pl.ds(st, tq), ls])))
            return tuple(out)

        init = tuple((jnp.full((tq, 1), NEG, F32), jnp.zeros((tq, HB), F32)) for _ in range(hp))
        carry = lax.fori_loop(0, i, lambda j, cr: blk(j, cr, False), init)
        res = blk(i, carry, True)
        lane = lax.broadcasted_iota(jnp.int32, (tq, HB), 1)
        for h in range(hp):
            ls = slice(h * HB, (h + 1) * HB)
            m, acc = res[h]
            l = acc[:, V_DIM:V_DIM + 1]
            o_ref[:, ls] = jnp.where(lane < V_DIM, acc / l, 0.0)
            lse_ref[:, ls] = jnp.broadcast_to(m * (1.0 / LOG2E) + jnp.log(l), (tq, HB))

    qs = pl.BlockSpec((tq, W), lambda h, i: (i, h))
    kvs = pl.BlockSpec((T, W), lambda h, i: (0, h))
    return _pcall(body, "flash_fwd", (N_HEADS // hp, T // tq), [qs, kvs, kvs], [qs, qs],
                  [_sds((T, N_HEADS * HB))] * 2, exchange=exchange)(q, k, v)


def _gates(hq, hf, lb):
    sig = _sigmoid(hf)
    f = lb + (1.0 - lb) * sig
    sq = _sigmoid(hq)
    return hq * sq, 1.0 - f, f, jnp.log(f), sig, sq


def _lower_bound(lbl_ref):
    l0, l1 = lbl_ref[0:1, :], lbl_ref[1:2, :]
    mx = jnp.maximum(l0, l1)
    e0, e1 = jnp.exp(l0 - mx), jnp.exp(l1 - mx)
    return e0 / (e0 + e1)


def _split3(x):
    hi = x.astype(BF)
    r1 = x - hi.astype(F32)
    mid = r1.astype(BF)
    lo = (r1 - mid.astype(F32)).astype(BF)
    return hi, mid, lo


def _tri_mm(tri, x):
    hi, mid, lo = _split3(x)
    mm = lambda t: jnp.dot(tri, t, preferred_element_type=F32)
    return mm(hi) + mm(mid) + mm(lo)


def _intra_codes(sub):
    row = lax.broadcasted_iota(jnp.int32, (CHUNK, CHUNK), 0)
    col = lax.broadcasted_iota(jnp.int32, (CHUNK, CHUNK), 1)
    sub_t, sub_s = row // sub, col // sub
    return sub, jnp.where(sub_s < sub_t, sub_t, 0), jnp.where((sub_s == sub_t) & (col <= row), row - col, -1)


def _intra(q, k, b2, b_s, codes, da=None):
    grad = da is not None
    pow2 = (lambda x: jnp.exp2(jnp.minimum(x, 0.0))) if grad else jnp.exp2
    sub, earlier, offset = codes
    a = jnp.zeros((CHUNK, CHUNK), F32)
    dq = jnp.zeros((CHUNK, G_DIM), F32)
    dk = jnp.zeros((CHUNK, G_DIM), F32)
    for i in range(1, CHUNK // sub):
        b0 = b_s[sub * i - 1:sub * i, :]
        eq, ek = pow2(b2 - b0), pow2(b0 - b2)
        mask = earlier == i
        a = jnp.where(mask, _dot_nt(q * eq, k * ek), a)
        if grad:
            dai = jnp.where(mask, da, 0.0)
            dq = dq + _dot(dai, k * ek) * eq
            dk = dk + _dot_tn(dai, q * eq) * ek
    for d in range(sub):
        ksh = pltpu.roll(k, d, 0) if d else k
        bsh = pltpu.roll(b2, d, 0) if d else b2
        e = pow2(b2 - bsh)
        mask = offset == d
        a = jnp.where(mask, jnp.sum(q * ksh * e, -1, keepdims=True), a)
        if grad:
            g = jnp.sum(jnp.where(mask, da, 0.0), -1, keepdims=True) * e
            dq = dq + g * ksh
            cb = g * q
            dk = dk + (pltpu.roll(cb, CHUNK - d, 0) if d else cb)
    return (a, dq, dk) if grad else a


def _hgrn_fwd(xp, lb_logits, g_hn, exchange=None):
    T = xp.shape[0]
    tb = min(_TB, T)
    ncb = tb // CHUNK
    hp = _HP
    W = hp * G_DIM

    def body(hq_ref, hf_ref, hi_ref, hg_ref, lbl_ref, ghn_ref, out_ref, oraw_ref, sall_ref, st_ref, b_s):
        lb_all = _lower_bound(lbl_ref)

        @pl.when(pl.program_id(1) == 0)
        def _():
            st_ref[...] = jnp.zeros(st_ref.shape, F32)

        row = lax.broadcasted_iota(jnp.int32, (CHUNK, CHUNK), 0)
        col = lax.broadcasted_iota(jnp.int32, (CHUNK, CHUNK), 1)
        tri = (col <= row).astype(BF)
        codes = _intra_codes(SUB_FWD)

        def chunk(c, carry):
            sl = pl.ds(pl.multiple_of(c * CHUNK, CHUNK), CHUNK)
            for h in range(hp):
                ls = slice(h * G_DIM, (h + 1) * G_DIM)
                q, k, _, lf, _, _ = _gates(hq_ref[sl, ls], hf_ref[sl, ls], lb_all[:, ls])
                v = hi_ref[sl, ls]
                b2 = _tri_mm(tri, lf) * LOG2E
                b_s[h] = b2
                st = st_ref[h]
                sall_ref[c, h] = st
                o = _dot_nt(q * jnp.exp2(b2), st) + _dot(_intra(q, k, b2, b_s.at[h], codes), v)
                bl = b_s[h, CHUNK - 1:CHUNK, :]
                st_ref[h] = st * jnp.exp2(bl) + _dot_tn(v, k * jnp.exp2(bl - b2))
                oraw_ref[sl, ls] = o
                n, _ = _rms(o, G_DIM)
                hg = hg_ref[sl, ls]
                out_ref[sl, ls] = n * ghn_ref[:, ls] * (hg * _sigmoid(hg))
            return carry

        lax.fori_loop(0, ncb, chunk, 0)

    col_blk = lambda j: pl.BlockSpec((tb, W), lambda p, t: (t, j * (G_HEADS // hp) + p))
    head = pl.BlockSpec((tb, W), lambda p, t: (t, p))
    return _pcall(
        body, "hgrn_fwd", (G_HEADS // hp, T // tb),
        [col_blk(0), col_blk(1), col_blk(2), col_blk(3),
         pl.BlockSpec((2, W), lambda p, t: (0, p)), pl.BlockSpec((1, W), lambda p, t: (0, p))],
        [head, head, pl.BlockSpec((ncb, hp, G_DIM, G_DIM), lambda p, t: (t, p, 0, 0))],
        [_sds((T, G_W)), _sds((T, G_W)), _sds((T // CHUNK, G_HEADS, G_DIM, G_DIM))],
        scratch=[pltpu.VMEM((hp, G_DIM, G_DIM), F32), pltpu.VMEM((hp, CHUNK, G_DIM), F32)], exchange=exchange,
    )(xp, xp, xp, xp, lb_logits, g_hn)


def _fwd_out(o_pad, o_hgrn, x, g_on, w_out, g_post, g_fpre, tm):
    T = x.shape[0]

    def body(o_ref, oh_ref, x_ref, gon_ref, w_ref, gpost_ref, gfpre_ref, h1_ref, y1_ref, z_ref, mix_ref):
        for h in range(N_HEADS):
            sl = slice(h * HB, (h + 1) * HB)
            n, _ = _rms(o_ref[:, sl], V_DIM)
            mix_ref[:, sl] = (n * gon_ref[:, sl]).astype(BF)
        mix_ref[:, N_HEADS * HB:MIX_W] = oh_ref[...].astype(BF)
        y1 = jnp.dot(mix_ref[...], w_ref[...], preferred_element_type=F32)
        y1_ref[...] = y1
        ny, _ = _rms(y1, D)
        h1 = x_ref[...] + ny * gpost_ref[...]
        h1_ref[...] = h1
        nh, _ = _rms(h1, D)
        z_ref[...] = (nh * gfpre_ref[...]).astype(BF)

    return _pcall(body, "fwd_out", (T // tm,),
                  [_rows(tm, N_HEADS * HB), _rows(tm, G_W), _rows(tm, D), _full((1, N_HEADS * HB)),
                   _full((MIX_W, D)), _full((1, D)), _full((1, D))],
                  [_rows(tm, D), _rows(tm, D), _rows(tm, D), _rows(tm, MIX_W)],
                  [_sds((T, D)), _sds((T, D)), _sds((T, D), BF), _sds((T, MIX_W), BF)],
                  )(o_pad, o_hgrn, x, g_on, w_out, g_post, g_fpre)


def _ffn_fwd(z, wgu, wd, h1, tgt, g_fpost, tm):
    T = z.shape[0]
    nf = wd.shape[0] // _FB

    def body(z_ref, wgu_ref, wd_ref, h1_ref, t_ref, gp_ref,
             as_ref, bs_ref, ff_ref, dh2_ref, dy2_ref, dgp_ref, loss_ref, acc):
        i, j = pl.program_id(0), pl.program_id(1)
        gu = _dot_nt(z_ref[...], wgu_ref[...])
        g = jnp.concatenate([gu[:, 0:FF_PAD], gu[:, 2 * FF_PAD:3 * FF_PAD]], 1)
        u = jnp.concatenate([gu[:, FF_PAD:2 * FF_PAD], gu[:, 3 * FF_PAD:4 * FF_PAD]], 1)
        s = _sigmoid(g)
        b = g * s
        ff = (b * u).astype(BF)
        as_ref[...] = (u * _dsilu(g, s)).astype(BF)
        bs_ref[...] = b.astype(BF)
        ff_ref[...] = ff
        part = jnp.dot(ff, wd_ref[...], preferred_element_type=F32)

        @pl.when(j == 0)
        def _():
            acc[...] = part

        @pl.when(j > 0)
        def _():
            acc[...] += part

        @pl.when((i == 0) & (j == 0))
        def _():
            dgp_ref[...] = jnp.zeros(dgp_ref.shape, F32)
            loss_ref[...] = jnp.zeros(loss_ref.shape, F32)

        @pl.when(j == nf - 1)
        def _():
            ny, r = _rms(acc[...], D)
            err = h1_ref[...] + ny * gp_ref[...] - t_ref[...]
            loss_ref[...] += 0.5 * jnp.sum(jnp.sum(err * err, -1, keepdims=True) * (1.0 / D), 0, keepdims=True)
            dh2 = err * (1.0 / D)
            dh2_ref[...] = dh2
            dy2, dgp = _rms_bwd(ny, r, gp_ref[...], dh2, D)
            dy2_ref[...] = dy2.astype(BF)
            dgp_ref[...] += dgp

    tok = lambda n: pl.BlockSpec((tm, n), lambda i, j: (i, 0))
    col = pl.BlockSpec((tm, _FB), lambda i, j: (i, j))
    return _pcall(
        body, "ffn_fwd", (T // tm, nf),
        [tok(D), pl.BlockSpec((2 * _FB, D), lambda i, j: (j, 0)), pl.BlockSpec((_FB, D), lambda i, j: (j, 0)),
         tok(D), tok(D), _full((1, D))],
        [col, col, col, tok(D), tok(D), _full((1, D)), _full((1, HB))],
        [_sds((T, nf * _FB), BF)] * 3 + [_sds((T, D)), _sds((T, D), BF), _sds((1, D)), _sds((1, HB))],
        scratch=[pltpu.VMEM((tm, D), F32)],
    )(z, wgu, wd, h1, tgt, g_fpost)


def _dsilu(x, s):
    return s * (1.0 + x * (1.0 - s))


def _ffn_bwd_x(dy2, gs, us, wgu, wd, h1, y1, dh2, g_fpre, g_post, tm):
    T = dy2.shape[0]
    nf = wd.shape[0] // _FB

    def body(dy2_ref, gs_ref, us_ref, wgu_ref, wd_ref, h1_ref, y1_ref, dh2_ref, gf_ref, gp_ref,
             dgu_ref, dh1_ref, dy1_ref, dgf_ref, dgp_ref, acc):
        i, j = pl.program_id(0), pl.program_id(1)
        dff = _dot_nt(dy2_ref[...], wd_ref[...])
        dg = (dff * gs_ref[...].astype(F32)).astype(BF)
        du = (dff * us_ref[...].astype(F32)).astype(BF)
        dgu = jnp.concatenate([dg[:, 0:FF_PAD], du[:, 0:FF_PAD], dg[:, FF_PAD:_FB], du[:, FF_PAD:_FB]], 1)
        dgu_ref[...] = dgu
        part = jnp.dot(dgu, wgu_ref[...], preferred_element_type=F32)

        @pl.when(j == 0)
        def _():
            acc[...] = part

        @pl.when(j > 0)
        def _():
            acc[...] += part

        @pl.when((i == 0) & (j == 0))
        def _():
            dgf_ref[...] = jnp.zeros(dgf_ref.shape, F32)
            dgp_ref[...] = jnp.zeros(dgp_ref.shape, F32)

        @pl.when(j == nf - 1)
        def _():
            nh, rh = _rms(h1_ref[...], D)
            dh, dgf = _rms_bwd(nh, rh, gf_ref[...], acc[...], D)
            dh1 = dh2_ref[...] + dh
            dh1_ref[...] = dh1
            dgf_ref[...] += dgf
            ny, ry = _rms(y1_ref[...], D)
            dy1, dgp = _rms_bwd(ny, ry, gp_ref[...], dh1, D)
            dy1_ref[...] = dy1.astype(BF)
            dgp_ref[...] += dgp

    tok = lambda n: pl.BlockSpec((tm, n), lambda i, j: (i, 0))
    col = pl.BlockSpec((tm, _FB), lambda i, j: (i, j))
    return _pcall(
        body, "ffn_bwd_x", (T // tm, nf),
        [tok(D), col, col, pl.BlockSpec((2 * _FB, D), lambda i, j: (j, 0)), pl.BlockSpec((_FB, D), lambda i, j: (j, 0)),
         tok(D), tok(D), tok(D), _full((1, D)), _full((1, D))],
        [pl.BlockSpec((tm, 2 * _FB), lambda i, j: (i, j)), tok(D), tok(D), _full((1, D)), _full((1, D))],
        [_sds((T, 2 * nf * _FB), BF), _sds((T, D)), _sds((T, D), BF), _sds((1, D)), _sds((1, D))],
        scratch=[pltpu.VMEM((tm, D), F32)],
    )(dy2, gs, us, wgu, wd, h1, y1, dh2, g_fpre, g_post)


def _ffn_bwd_w(z, ffs, dgu, dy2, tm):
    T = z.shape[0]
    nf = ffs.shape[1] // _FB
    nt = T // tm

    def body(z_ref, ff_ref, dgu_ref, dy2_ref, dwgu_ref, dwd_ref, agu, ad):
        i = pl.program_id(1)
        pgu = _dot_tn(dgu_ref[...], z_ref[...])
        pd = _dot_tn(ff_ref[...], dy2_ref[...])

        @pl.when(i == 0)
        def _():
            agu[...] = pgu
            ad[...] = pd

        @pl.when(i > 0)
        def _():
            agu[...] += pgu
            ad[...] += pd

        @pl.when(i == nt - 1)
        def _():
            dwgu_ref[...] = agu[...].astype(BF)
            dwd_ref[...] = ad[...].astype(BF)

    F = nf * _FB
    tok = lambda n: pl.BlockSpec((tm, n), lambda j, i: (i, 0))
    return _pcall(
        body, "ffn_bwd_w", (nf, nt),
        [tok(D), pl.BlockSpec((tm, _FB), lambda j, i: (i, j)), pl.BlockSpec((tm, 2 * _FB), lambda j, i: (i, j)), tok(D)],
        [pl.BlockSpec((2 * _FB, D), lambda j, i: (j, 0)), pl.BlockSpec((_FB, D), lambda j, i: (j, 0))],
        [_sds((2 * F, D), BF), _sds((F, D), BF)],
        scratch=[pltpu.VMEM((2 * _FB, D), F32), pltpu.VMEM((_FB, D), F32)],
    )(z, ffs, dgu, dy2)


def _out_bwd(dy1, mix, o_pad, w_out, g_on, tm):
    T = dy1.shape[0]
    W = N_HEADS * HB

    def body(dy1_ref, mix_ref, o_ref, w_ref, gon_ref, do_ref, dl_ref, dohg_ref, dw_ref, dgon_ref):
        i = pl.program_id(0)
        dy1v = dy1_ref[...]
        dmix = _dot_nt(dy1v, w_ref[...])
        pw = _dot_tn(mix_ref[...], dy1v)

        @pl.when(i == 0)
        def _():
            dw_ref[...] = pw
            dgon_ref[...] = jnp.zeros(dgon_ref.shape, F32)

        @pl.when(i > 0)
        def _():
            dw_ref[...] += pw

        for h in range(N_HEADS):
            sl = slice(h * HB, (h + 1) * HB)
            ov = o_ref[:, sl]
            n, r = _rms(ov, V_DIM)
            do, dg = _rms_bwd(n, r, gon_ref[:, sl], dmix[:, sl], V_DIM)
            dgon_ref[:, sl] += dg
            do_ref[:, sl] = do.astype(BF)
            dl_ref[:, sl] = jnp.broadcast_to(jnp.sum(do * ov, -1, keepdims=True), (tm, HB))
        dohg_ref[...] = dmix[:, W:MIX_W]

    return _pcall(body, "out_bwd", (T // tm,),
                  [_rows(tm, D), _rows(tm, MIX_W), _rows(tm, W), _full((MIX_W, D)), _full((1, W))],
                  [_rows(tm, W), _rows(tm, W), _rows(tm, G_W), _full((MIX_W, D)), _full((1, W))],
                  [_sds((T, W), BF), _sds((T, W)), _sds((T, G_W)), _sds((MIX_W, D)), _sds((1, W))],
                  )(dy1, mix, o_pad, w_out, g_on)


def _flash_bwd(q, k, v, do, lse, dl, tq, exchange=None):
    T = q.shape[0]
    nq = T // tq
    scale = QK_DIM ** -0.5
    hp = _AH
    W = hp * HB

    def body(k_ref, v_ref, q_ref, do_ref, lse_ref, dl_ref, dk_ref, dv_ref, dq_ref):
        j = pl.program_id(1)

        @pl.when(j == 0)
        def _():
            dq_ref[...] = jnp.zeros(dq_ref.shape, F32)

        def blk(i, carry, masked):
            sl = pl.ds(pl.multiple_of(i * tq, tq), tq)
            out = []
            for h in range(hp):
                ls = slice(h * HB, (h + 1) * HB)
                dk, dv = carry[h]
                kv, vv = k_ref[:, ls], v_ref[:, ls]
                qv, dov = q_ref[sl, ls], do_ref[sl, ls]
                s = _dot_nt(qv, kv) * scale
                if masked:
                    r = lax.broadcasted_iota(jnp.int32, (tq, tq), 0)
                    c = lax.broadcasted_iota(jnp.int32, (tq, tq), 1)
                    s = jnp.where(c <= r, s, NEG)
                p = jnp.exp(s - lse_ref[sl, h * HB:h * HB + 1])
                ds = p * (_dot_nt(dov, vv) - dl_ref[sl, h * HB:h * HB + 1]) * scale
                dq_ref[sl, ls] += _dot(ds, kv)
                out.append((dk + _dot_tn(ds, qv), dv + _dot_tn(p, dov)))
            return tuple(out)

        zero = jnp.zeros((tq, HB), F32)
        carry = blk(j, tuple((zero, zero) for _ in range(hp)), True)
        res = lax.fori_loop(j + 1, nq, lambda i, cr: blk(i, cr, False), carry)
        for h in range(hp):
            ls = slice(h * HB, (h + 1) * HB)
            dk_ref[:, ls] = res[h][0]
            dv_ref[:, ls] = res[h][1]

    tile = pl.BlockSpec((tq, W), lambda h, j: (j, h))
    whole = pl.BlockSpec((T, W), lambda h, j: (0, h))
    return _pcall(body, "flash_bwd", (N_HEADS // hp, nq), [tile, tile, whole, whole, whole, whole],
                  [tile, tile, whole], [_sds((T, N_HEADS * HB))] * 3, exchange=exchange)(k, v, q, do, lse, dl)


def _mla_prep_bwd(xp, tabs, dq, dk, dv, g_q, g_kv, w_uq, w_uk, w_uv, tm):
    T = xp.shape[0]
    W = N_HEADS * HB

    def body(xp_ref, ta_ref, tb1_ref, tb2_ref, dq_ref, dk_ref, dv_ref, gq_ref, gkv_ref, wuq_ref, wuk_ref, wuv_ref,
             dxp_ref, dwuq_ref, dwuk_ref, dwuv_ref, dgq_ref, dgkv_ref, dqp):
        i = pl.program_id(0)
        ta, tb1, tb2 = ta_ref[...], tb1_ref[...], tb2_ref[...]
        nq, rq = _rms(xp_ref[:, 0:Q_RANK], Q_RANK)
        nkv, rkv = _rms(xp_ref[:, Q_RANK:Q_RANK + KV_RANK], KV_RANK)
        dkr = jnp.zeros((tm, HB), F32)
        for h in range(N_HEADS):
            sl = slice(h * HB, (h + 1) * HB)
            dqp[:, sl] = _unrope(dq_ref[:, sl], ta, tb1, tb2).astype(BF)
            dkr = dkr + dk_ref[:, sl]
        dkr = pltpu.roll(_unrope(dkr, ta, tb1, tb2), HB - NOPE, 1)
        lane = lax.broadcasted_iota(jnp.int32, (tm, HB), 1)
        dxp_ref[:, Q_RANK + KV_RANK:MLA_IN] = jnp.where(lane < ROPE, dkr, 0.0)
        dqpv = dqp[...]
        dkv, dvv = dk_ref[...].astype(BF), dv_ref[...].astype(BF)
        nqs = (nq * gq_ref[...]).astype(BF)
        nkvs = (nkv * gkv_ref[...]).astype(BF)
        pq, pk, pv = _dot_tn(nqs, dqpv), _dot_tn(nkvs, dkv), _dot_tn(nkvs, dvv)
        dcq, dgq = _rms_bwd(nq, rq, gq_ref[...], _dot_nt(dqpv, wuq_ref[...]), Q_RANK)
        dckv, dgkv = _rms_bwd(nkv, rkv, gkv_ref[...], _dot_nt(dkv, wuk_ref[...]) + _dot_nt(dvv, wuv_ref[...]), KV_RANK)
        dxp_ref[:, 0:Q_RANK] = dcq
        dxp_ref[:, Q_RANK:Q_RANK + KV_RANK] = dckv

        @pl.when(i == 0)
        def _():
            dwuq_ref[...] = pq
            dwuk_ref[...] = pk
            dwuv_ref[...] = pv
            dgq_ref[...] = dgq
            dgkv_ref[...] = dgkv

        @pl.when(i > 0)
        def _():
            dwuq_ref[...] += pq
            dwuk_ref[...] += pk
            dwuv_ref[...] += pv
            dgq_ref[...] += dgq
            dgkv_ref[...] += dgkv

    tab = _rows(tm, HB)
    return _pcall(
        body, "mla_prep_bwd", (T // tm,),
        [_rows(tm, MLA_IN), tab, tab, tab, _rows(tm, W), _rows(tm, W), _rows(tm, W), _full((1, Q_RANK)),
         _full((1, KV_RANK)), _full((Q_RANK, W)), _full((KV_RANK, W)), _full((KV_RANK, W))],
        [_rows(tm, MLA_IN), _full((Q_RANK, W)), _full((KV_RANK, W)), _full((KV_RANK, W)), _full((1, Q_RANK)),
         _full((1, KV_RANK))],
        [_sds((T, MLA_IN)), _sds((Q_RANK, W)), _sds((KV_RANK, W)), _sds((KV_RANK, W)), _sds((1, Q_RANK)),
         _sds((1, KV_RANK))],
        scratch=[pltpu.VMEM((tm, W), BF)],
    )(xp, *tabs, dq, dk, dv, g_q, g_kv, w_uq, w_uk, w_uv)


def _hgrn_bwd(xp, o_raw, s_all, d_out, lb_logits, g_hn, exchange=None):
    T = xp.shape[0]
    tb = min(_TB, T)
    ncb = tb // CHUNK
    nb = T // tb
    hp = _HP
    W = hp * G_DIM

    def body(hq_ref, hf_ref, hi_ref, hg_ref, o_ref, sall_ref, dout_ref, lbl_ref, ghn_ref,
             dhq_ref, dhf_ref, dhi_ref, dhg_ref, dlbl_ref, dghn_ref, dst_ref, b_s, acc_lb, acc_g):
        t = pl.program_id(1)
        lb_all = _lower_bound(lbl_ref)

        @pl.when(t == 0)
        def _():
            dst_ref[...] = jnp.zeros(dst_ref.shape, F32)
            acc_lb[...] = jnp.zeros(acc_lb.shape, F32)
            acc_g[...] = jnp.zeros(acc_g.shape, F32)

        row = lax.broadcasted_iota(jnp.int32, (CHUNK, CHUNK), 0)
        col = lax.broadcasted_iota(jnp.int32, (CHUNK, CHUNK), 1)
        tri = (col <= row).astype(BF)
        tri_t = (col >= row).astype(BF)
        codes = _intra_codes(SUB_BWD)
        last = lax.broadcasted_iota(jnp.int32, (CHUNK, G_DIM), 0) == CHUNK - 1

        def chunk(cc, carry):
            c = ncb - 1 - cc
            sl = pl.ds(pl.multiple_of(c * CHUNK, CHUNK), CHUNK)
            for h in range(hp):
                ls = slice(h * G_DIM, (h + 1) * G_DIM)
                lb, ghn = lb_all[:, ls], ghn_ref[:, ls]
                hq, hg = hq_ref[sl, ls], hg_ref[sl, ls]
                q, k, f, lf, sig, sq = _gates(hq, hf_ref[sl, ls], lb)
                v = hi_ref[sl, ls]
                b2 = _tri_mm(tri, lf) * LOG2E
                b_s[h] = b2
                st = sall_ref[c, h]
                dstn = dst_ref[h]
                o = o_ref[sl, ls]
                dout = dout_ref[sl, ls]
                n, r = _rms(o, G_DIM)
                sg = _sigmoid(hg)
                dhg_ref[sl, ls] = dout * (n * ghn) * _dsilu(hg, sg)
                do, dg = _rms_bwd(n, r, ghn, dout * (hg * sg), G_DIM)
                acc_g[:, ls] += dg
                eb = jnp.exp2(b2)
                bl = b_s[h, CHUNK - 1:CHUNK, :]
                ebl = jnp.exp2(bl)
                ekd = jnp.exp2(bl - b2)
                kd = k * ekd
                a, dq_i, dk_i = _intra(q, k, b2, b_s.at[h], codes, _dot_nt(do, v))
                dhi_ref[sl, ls] = _dot_tn(a, do) + _dot_nt(kd, dstn)
                dk_state = _dot(v, dstn) * ekd
                dq = dq_i + _dot(do, st) * eb
                dk = dk_i + dk_state
                dbl = jnp.sum(k * dk_state, 0, keepdims=True) + ebl * jnp.sum(dstn * st, 0, keepdims=True)
                db = q * dq - k * dk + jnp.where(last, dbl, 0.0)
                df = _tri_mm(tri_t, db) / f - dk
                dhf_ref[sl, ls] = df * (1.0 - lb) * sig * (1.0 - sig)
                acc_lb[:, ls] += jnp.sum(df * (1.0 - sig), 0, keepdims=True)
                dhq_ref[sl, ls] = dq * _dsilu(hq, sq)
                dst_ref[h] = dstn * ebl + _dot_tn(do, q * eb)
            return carry

        lax.fori_loop(0, ncb, chunk, 0)

        @pl.when(t == nb - 1)
        def _():
            dl0 = acc_lb[...] * lb_all * (1.0 - lb_all)
            dlbl_ref[0:1, :] = dl0
            dlbl_ref[1:2, :] = -dl0
            dghn_ref[...] = acc_g[...]

    col_blk = lambda j: pl.BlockSpec((tb, W), lambda p, t: (nb - 1 - t, j * (G_HEADS // hp) + p))
    head = pl.BlockSpec((tb, W), lambda p, t: (nb - 1 - t, p))
    two = pl.BlockSpec((2, W), lambda p, t: (0, p))
    one = pl.BlockSpec((1, W), lambda p, t: (0, p))
    res = _pcall(
        body, "hgrn_bwd", (G_HEADS // hp, nb),
        [col_blk(0), col_blk(1), col_blk(2), col_blk(3), head,
         pl.BlockSpec((ncb, hp, G_DIM, G_DIM), lambda p, t: (nb - 1 - t, p, 0, 0)), head, two, one],
        [head, head, head, head, two, one],
        [_sds((T, G_W))] * 4 + [_sds((2, G_W)), _sds((1, G_W))],
        scratch=[pltpu.VMEM((hp, G_DIM, G_DIM), F32), pltpu.VMEM((hp, CHUNK, G_DIM), F32),
                 pltpu.VMEM((1, W), F32), pltpu.VMEM((1, W), F32)], exchange=exchange,
    )(xp, xp, xp, xp, o_raw, s_all, d_out, lb_logits, g_hn)
    return res


def _in_bwd_x(x, dxp_m, dxp_h, dh1, w_in_al, g_pre, tm, exchange=None):
    T = x.shape[0]

    def body(x_ref, dm_ref, d0_ref, d1_ref, d2_ref, d3_ref, dh1_ref, w_ref, g_ref, dx_ref, dg_ref):
        i = pl.program_id(0)
        du = _dot_nt(dm_ref[...], w_ref[:, 0:MLA_IN])
        for j, d_ref in enumerate((d0_ref, d1_ref, d2_ref, d3_ref)):
            du = du + _dot_nt(d_ref[...], w_ref[:, MLA_IN + j * G_W:MLA_IN + (j + 1) * G_W])
        nx, r = _rms(x_ref[...], D)
        dx, dg = _rms_bwd(nx, r, g_ref[...], du, D)
        dx_ref[...] = dh1_ref[...] + dx

        @pl.when(i == 0)
        def _():
            dg_ref[...] = dg

        @pl.when(i > 0)
        def _():
            dg_ref[...] += dg

    return _pcall(body, "in_bwd_x", (T // tm,),
                  [_rows(tm, D), _rows(tm, MLA_IN)] + [_rows(tm, G_W)] * 4 + [_rows(tm, D), _full((D, XP_W)), _full((1, D))],
                  [_rows(tm, D), _full((1, D))], [_sds((T, D)), _sds((1, D))], exchange=exchange,
                  )(x, dxp_m, *dxp_h, dh1, w_in_al, g_pre)


def _aligned_col(c):
    return jnp.where(c < Q_RANK + KV_RANK + ROPE, c, c + (KR_PAD - ROPE))


def _align_w_in(g_in):
    tile = 384
    kr_end = Q_RANK + KV_RANK + ROPE

    def body(g_ref, o_ref, gp):
        gp[...] = jnp.zeros(gp.shape, BF)
        for j in range(N_DEV):
            gp[j, :, 0:IN_SH] = g_ref[j]
        r = lax.broadcasted_iota(jnp.int32, (tile, tile), 0)
        c = lax.broadcasted_iota(jnp.int32, (tile, tile), 1)
        for t in range(XP_W // tile):
            lo, hi = t * tile, (t + 1) * tile
            cols = [a if a < kr_end else a - (KR_PAD - ROPE) for a in (lo, hi - 1)]
            acc = jnp.zeros((D, tile), F32)
            for j in range(cols[0] // IN_SH, cols[-1] // IN_SH + 1):
                sel = (r < IN_SH) & (_aligned_col(j * IN_SH + r) == lo + c)
                acc = acc + jnp.dot(gp[j], sel.astype(BF), preferred_element_type=F32)
            o_ref[:, lo:hi] = acc.astype(BF)

    vm = pl.BlockSpec(memory_space=pltpu.VMEM)
    return pl.pallas_call(
        body, name="align_w_in", in_specs=[vm], out_specs=vm, out_shape=_sds((D, XP_W), BF),
        scratch_shapes=[pltpu.VMEM((N_DEV, D, tile), BF)],
        compiler_params=pltpu.CompilerParams(vmem_limit_bytes=_VMEM_LIMIT))(g_in)


def _in_bwd_w(u, dxp_m, dxp_h, tm):
    T = u.shape[0]
    nt = T // tm
    win = 640

    def body(u_ref, dm_ref, d0_ref, d1_ref, d2_ref, d3_ref, o_ref, acc):
        i = pl.program_id(0)
        ut = u_ref[...].T
        parts = [(0, MLA_IN, dm_ref)] + [(MLA_IN + j * G_W, G_W, d) for j, d in enumerate((d0_ref, d1_ref, d2_ref, d3_ref))]

        @pl.when(i == 0)
        def _():
            for lo, n, d in parts:
                acc[:, lo:lo + n] = jnp.dot(ut, d[...].astype(BF), preferred_element_type=F32)

        @pl.when(i > 0)
        def _():
            for lo, n, d in parts:
                acc[:, lo:lo + n] += jnp.dot(ut, d[...].astype(BF), preferred_element_type=F32)

        @pl.when(i == nt - 1)
        def _():
            wide = 384
            r = lax.broadcasted_iota(jnp.int32, (win, wide), 0)
            c = lax.broadcasted_iota(jnp.int32, (win, wide), 1)
            kr_end = Q_RANK + KV_RANK + ROPE
            for j in range(N_DEV):
                first = j * IN_SH if j * IN_SH < kr_end else j * IN_SH + (KR_PAD - ROPE)
                lo = min(first // HB * HB, XP_W - win)
                sel = (c < IN_SH) & (_aligned_col(j * IN_SH + c) == lo + r)
                res = jnp.dot(acc[:, lo:lo + win].astype(BF), sel.astype(BF), preferred_element_type=F32)
                o_ref[j] = res[:, 0:IN_SH].astype(BF)

    return _pcall(body, "in_bwd_w", (nt,),
                  [_rows(tm, D), _rows(tm, MLA_IN)] + [_rows(tm, G_W)] * 4,
                  [_full((N_DEV, D, IN_SH))], [_sds((N_DEV, D, IN_SH), BF)],
                  scratch=[pltpu.VMEM((D, XP_W), F32)])(u, dxp_m, *dxp_h)[0]


def _pad_heads(w, width, real):
    lead = w.shape[:-1]
    w = w.reshape(lead + (N_HEADS, real))
    w = jnp.pad(w, [(0, 0)] * len(lead) + [(0, 0), (0, width - real)])
    return w.reshape(lead + (N_HEADS * width,))


def _unpad_heads(w, width, real):
    lead = w.shape[:-1]
    return w.reshape(lead + (N_HEADS, width))[..., :real].reshape(lead + (N_HEADS * real,))


def _rope_tables(positions):
    half = ROPE // 2
    inv_freq = 1.0 / (ROPE_THETA ** (jnp.arange(0, ROPE, 2, dtype=F32) / ROPE))
    ang = positions.astype(F32)[:, None] * inv_freq
    cos, sin = jnp.cos(ang), jnp.sin(ang)
    T = positions.shape[0]
    z = lambda n: jnp.zeros((T, n), F32)
    ta = jnp.concatenate([jnp.ones((T, NOPE), F32), cos, cos, z(HB - QK_DIM)], 1)
    tb1 = jnp.concatenate([z(NOPE), -sin, z(half), z(HB - QK_DIM)], 1)
    tb2 = jnp.concatenate([z(NOPE), z(half), sin, z(HB - QK_DIM)], 1)
    return ta, tb1, tb2


def kernel(x, positions, attn_pre_norm, w_in, mla_q_norm, mla_w_uq, mla_kv_norm, mla_w_ukv, mla_out_norm, hgrn_lb_logits, hgrn_out_norm, w_out, attn_post_norm, ffn_pre_norm, w_gate, w_up, w_down, ffn_post_norm, loss_target, m_attn_pre_norm, m_w_in, m_mla_q_norm, m_mla_w_uq, m_mla_kv_norm, m_mla_w_ukv, m_mla_out_norm, m_hgrn_lb_logits, m_hgrn_out_norm, m_w_out, m_attn_post_norm, m_ffn_pre_norm, m_w_gate, m_w_up, m_w_down, m_ffn_post_norm, v_attn_pre_norm, v_w_in, v_mla_q_norm, v_mla_w_uq, v_mla_kv_norm, v_mla_w_ukv, v_mla_out_norm, v_hgrn_lb_logits, v_hgrn_out_norm, v_w_out, v_attn_post_norm, v_ffn_pre_norm, v_w_gate, v_w_up, v_w_down, v_ffn_post_norm):
    T = x.shape[1]
    tm = min(_TM, T)
    tq = min(_TQ, T)
    xs, tgt = x[0], loss_target[0]
    uq_sh = (Q_RANK // N_DEV, N_HEADS * QK_DIM)

    b_in, b_uq, b_out, b_gu, b_d = _cast_shards(
        w_in[0], mla_w_uq[0].reshape(uq_sh), w_out[0], w_gate[0].T, w_up[0].T, w_down[0])
    g_in, g_uq = _gather_two_level("ag_first", [b_in, b_uq])
    w_in_al = _align_w_in(g_in)
    w_uq_p = _pad_heads(g_uq.reshape(Q_RANK, N_HEADS * QK_DIM), HB, QK_DIM)
    w_ukv = mla_w_ukv[0].astype(BF)
    w_uk_p = _pad_heads(w_ukv[..., :NOPE].reshape(KV_RANK, N_HEADS * NOPE), HB, NOPE)
    w_uv_p = _pad_heads(w_ukv[..., NOPE:].reshape(KV_RANK, N_HEADS * V_DIM), HB, V_DIM)
    g_on_p = _pad_heads(mla_out_norm, HB, V_DIM)
    tabs = _rope_tables(positions[0])

    xp_m, xp_h, u = _fwd_in(xs, attn_pre_norm, w_in_al, tm)
    q_att, qs_att, k_att, v_att = _mla_prep(xp_m, tabs, mla_q_norm, mla_kv_norm, w_uq_p, w_uk_p, w_uv_p, tm)
    o_hgrn, o_raw, s_all, g_out, wd = _hgrn_fwd(xp_h, hgrn_lb_logits, hgrn_out_norm, ([GATHER, GATHER], [b_out, b_d]))
    wd = wd.reshape(N_DEV * FF_PAD, D)
    o_pad, lse, wgu = _flash_fwd(qs_att, k_att, v_att, tq, ([GATHER], [b_gu]))
    wgu = wgu.reshape(N_DEV * 2 * FF_PAD, D)
    w_out_full = g_out.reshape(D, D)
    w_out_mla = jnp.pad(w_out_full[:N_HEADS * V_DIM].reshape(N_HEADS, V_DIM, D), ((0, 0), (0, HB - V_DIM), (0, 0)))
    w_out_p = jnp.concatenate([w_out_mla.reshape(N_HEADS * HB, D), w_out_full[N_HEADS * V_DIM:]], 0)
    h1, y1, z, mix = _fwd_out(o_pad, o_hgrn, xs, g_on_p, w_out_p, attn_post_norm, ffn_pre_norm, tm)
    gs, us, ffs, dh2, dy2, d_fpost, loss_row = _ffn_fwd(z, wgu, wd, h1, tgt, ffn_post_norm, tm)

    dgu, dh1, dy1, d_fpre, d_post = _ffn_bwd_x(dy2, gs, us, wgu, wd, h1, y1, dh2, ffn_pre_norm, attn_post_norm, tm)
    dwgu, dwd = _ffn_bwd_w(z, ffs, dgu, dy2, tm)
    do_pad, dl, d_ohg, dw_out_p, d_on_p = _out_bwd(dy1, mix, o_pad, w_out_p, g_on_p, tm)
    dw_out_mla = dw_out_p[:N_HEADS * HB].reshape(N_HEADS, HB, D)[:, :V_DIM].reshape(N_HEADS * V_DIM, D)
    dw_out = jnp.concatenate([dw_out_mla, dw_out_p[N_HEADS * HB:]], 0).reshape(N_DEV, D // N_DEV, D).astype(BF)
    dk_att, dv_att, dq_att, p_gu, p_d, p_out = _flash_bwd(
        q_att, k_att, v_att, do_pad, lse, dl, tq,
        ([SCATTER] * 3, [dwgu.reshape(N_DEV, 2 * FF_PAD, D), dwd.reshape(N_DEV, FF_PAD, D), dw_out]))
    dxp_m, dw_uq_p, dw_uk_p, dw_uv_p, d_gq, d_gkv = _mla_prep_bwd(
        xp_m, tabs, dq_att, dk_att, dv_att, mla_q_norm, mla_kv_norm, w_uq_p, w_uk_p, w_uv_p, tm)
    dw_uq = _unpad_heads(dw_uq_p, HB, QK_DIM).reshape((N_DEV,) + uq_sh).astype(BF)
    dw_ukv = jnp.concatenate([_unpad_heads(dw_uk_p, HB, NOPE).reshape(KV_RANK, N_HEADS, NOPE),
                              _unpad_heads(dw_uv_p, HB, V_DIM).reshape(KV_RANK, N_HEADS, V_DIM)], -1)
    *dxp_h, d_lbl, d_ghn, p_uq, dw_ukv_all = _hgrn_bwd(
        xp_h, o_raw, s_all, d_ohg, hgrn_lb_logits, hgrn_out_norm,
        ([SCATTER, GATHER], [dw_uq, dw_ukv.reshape(KV_RANK, N_HEADS * HB)]))
    dw_in = _in_bwd_w(u, dxp_m, dxp_h, tm)
    grad_x, d_pre, p_in = _in_bwd_x(xs, dxp_m, dxp_h, dh1, w_in_al, attn_pre_norm, tm, ([SCATTER], [dw_in]))
    d_on = _unpad_heads(d_on_p, HB, V_DIM)

    ukv2 = lambda a: a.reshape(KV_RANK, N_HEADS * HB)
    vecs = [d_pre, d_gq, d_gkv, d_on, d_lbl, d_ghn, d_post, d_fpre, d_fpost, loss_row]
    small_w = [attn_pre_norm, mla_q_norm, mla_kv_norm, ukv2(mla_w_ukv), mla_out_norm, hgrn_lb_logits, hgrn_out_norm,
               attn_post_norm, ffn_pre_norm, ffn_post_norm]
    small_m = [m_attn_pre_norm, m_mla_q_norm, m_mla_kv_norm, ukv2(m_mla_w_ukv), m_mla_out_norm, m_hgrn_lb_logits,
               m_hgrn_out_norm, m_attn_post_norm, m_ffn_pre_norm, m_ffn_post_norm]
    small_v = [v_attn_pre_norm, v_mla_q_norm, v_mla_kv_norm, ukv2(v_mla_w_ukv), v_mla_out_norm, v_hgrn_lb_logits,
               v_hgrn_out_norm, v_attn_post_norm, v_ffn_pre_norm, v_ffn_post_norm]
    rall = _final_exchange(vecs)
    s_g, s_d, s_m, s_v, loss_all = _small_adam(rall, dw_ukv_all, 3, small_w, small_m, small_v)
    r_in = _shard_adam("adam_w_in", p_in, w_in[0], m_w_in[0], v_w_in[0], 256)
    r_uq = _shard_adam("adam_w_uq", p_uq, mla_w_uq[0].reshape(uq_sh), m_mla_w_uq[0].reshape(uq_sh),
                       v_mla_w_uq[0].reshape(uq_sh), uq_sh[0])
    r_out = _shard_adam("adam_w_out", p_out, w_out[0], m_w_out[0], v_w_out[0], D // N_DEV)
    r_g = [a.T for a in _shard_adam("adam_w_gate", p_gu, w_gate[0].T, m_w_gate[0].T, v_w_gate[0].T, 32)]
    r_u = [a.T for a in _shard_adam("adam_w_up", p_gu, w_up[0].T, m_w_up[0].T, v_w_up[0].T, 32, FF_PAD)]
    r_d = _shard_adam("adam_w_down", p_d, w_down[0], m_w_down[0], v_w_down[0], FF_SH // 2)

    loss = loss_all[0, 0]

    def assemble(big, small):
        b_in, b_uq, b_out, b_g, b_u, b_d = big
        return [small[0], b_in[None], small[1], b_uq.reshape(mla_w_uq.shape), small[2],
                small[3].reshape(mla_w_ukv.shape), small[4], small[5], small[6], b_out[None], small[7], small[8],
                b_g[None], b_u[None], b_d[None], small[9]]

    outs = [loss, grad_x[None]]
    for idx, small in enumerate((s_g, s_d, s_m, s_v)):
        outs += assemble([r[idx] for r in (r_in, r_uq, r_out, r_g, r_u, r_d)], small)
    return tuple(outs)
```

```python
import jax
import jax.numpy as jnp
from jax import lax
from jax.experimental import pallas as pl
from jax.experimental.pallas import tpu as pltpu

BF = jnp.bfloat16
F32 = jnp.float32
MESH = pl.DeviceIdType.MESH

N_DEV = 8
D = 1024
EPS = 1e-6
LOG2E = 1.4426950408889634
ROPE_THETA = 10000.0
N_HEADS = 8
HB = 128
NOPE = 64
ROPE = 32
V_DIM = 64
QK_DIM = NOPE + ROPE
Q_RANK = 384
KV_RANK = 128
KR_PAD = 128
MLA_IN = Q_RANK + KV_RANK + KR_PAD
G_HEADS = 4
G_DIM = 128
G_W = G_HEADS * G_DIM
CHUNK = 64
SUB_FWD = 16
SUB_BWD = 16
XP_W = MLA_IN + 4 * G_W
IN_SH = 324
IN_W = N_DEV * IN_SH
FF_SH = 352
FF_PAD = 384
MIX_W = N_HEADS * HB + G_W

ADAM_LR = 0.001
ADAM_B1 = 0.9
ADAM_B2 = 0.999
ADAM_EPS = 1e-08
ADAM_WD = 0.01
ADAM_STEP = 10

_TM = 512
_TMF = 1024
_TQ = 512
_AH = 2
_FB = 768
_TB = 1024
_HP = 4
_VMEM_LIMIT = 56 * 1024 * 1024
NEG = -1e30


def _dot(a, b):
    return jnp.dot(a.astype(BF), b.astype(BF), preferred_element_type=F32)


def _dot_nt(a, b):
    return lax.dot_general(a.astype(BF), b.astype(BF), (((1,), (1,)), ((), ())), preferred_element_type=F32)


def _dot_tn(a, b):
    return lax.dot_general(a.astype(BF), b.astype(BF), (((0,), (0,)), ((), ())), preferred_element_type=F32)


def _sigmoid(x):
    return 1.0 / (1.0 + jnp.exp(-x))


def _rms(x, n):
    r = lax.rsqrt(jnp.sum(x * x, -1, keepdims=True) * (1.0 / n) + EPS)
    return x * r, r


def _rms_bwd(nx, r, g, dy, n):
    dg = jnp.sum(dy * nx, 0, keepdims=True)
    dn = dy * g
    dx = r * (dn - nx * (jnp.sum(dn * nx, -1, keepdims=True) * (1.0 / n)))
    return dx, dg


def _adamw(w, g, m, v):
    m2 = ADAM_B1 * m + (1.0 - ADAM_B1) * g
    v2 = ADAM_B2 * v + (1.0 - ADAM_B2) * (g * g)
    m_hat = m2 / (1.0 - ADAM_B1 ** ADAM_STEP)
    v_hat = v2 / (1.0 - ADAM_B2 ** ADAM_STEP)
    delta = -ADAM_LR * (m_hat / (jnp.sqrt(v_hat) + ADAM_EPS) + ADAM_WD * w)
    return delta, m2, v2


def _pcall(body, name, grid, in_specs, out_specs, out_shape, scratch=(), exchange=None):
    scratch = list(scratch)
    extra = ()
    if exchange is not None:
        kinds, extra = exchange
        in_specs, out_specs, out_shape = list(in_specs), list(out_specs), list(out_shape)
        n_in, n_out, n_scr, n_x = len(in_specs), len(out_specs), len(scratch), len(extra)
        inner = body

        def body(*refs):
            ins, rest = refs[:n_in], refs[n_in:]
            x_src, rest = rest[:n_x], rest[n_x:]
            outs, rest = rest[:n_out], rest[n_out:]
            x_dst, rest = rest[:n_x], rest[n_x:]
            ex = _Exchange(kinds, x_src, x_dst, *rest[n_scr:])
            first = pl.program_id(0) == 0
            last = pl.program_id(0) == grid[0] - 1
            for a in range(1, len(grid)):
                first = first & (pl.program_id(a) == 0)
                last = last & (pl.program_id(a) == grid[a] - 1)
            pl.when(first)(ex.start)
            inner(*ins, *outs, *rest[:n_scr])
            pl.when(last)(ex.wait)

        in_specs += [_HBM] * n_x
        out_specs += [_HBM] * n_x
        out_shape += _exchange_shapes(kinds, extra)
        scratch += _exchange_sems(n_x)
    call = pl.pallas_call(
        body, name=name, grid=grid, in_specs=in_specs, out_specs=out_specs, out_shape=out_shape,
        scratch_shapes=scratch,
        compiler_params=pltpu.CompilerParams(
            dimension_semantics=("arbitrary",) * len(grid), vmem_limit_bytes=_VMEM_LIMIT))
    return lambda *operands: call(*operands, *extra)


def _full(shape):
    return pl.BlockSpec(shape, lambda *_: (0,) * len(shape))


def _rows(tm, n):
    return pl.BlockSpec((tm, n), lambda i, *_: (i, 0))


def _sds(shape, dtype=F32):
    return jax.ShapeDtypeStruct(shape, dtype)


def _peer(k, x, y, c):
    px = 1 - x if (k >> 2) & 1 else x
    py = 1 - y if (k >> 1) & 1 else y
    pc = 1 - c if k & 1 else c
    return px, py, pc


GATHER, SCATTER = "gather", "scatter"


class _Exchange:
    def __init__(self, kinds, srcs, dsts, send_sems, recv_sems, loc_sems):
        self.kinds, self.srcs, self.dsts = kinds, srcs, dsts
        self.send_sems, self.recv_sems, self.loc_sems = send_sems, recv_sems, loc_sems
        self.x, self.y, self.c = lax.axis_index("x"), lax.axis_index("y"), lax.axis_index("c")
        self.me = 4 * self.x + 2 * self.y + self.c

    def _src(self, w, slot):
        return self.srcs[w] if self.kinds[w] == GATHER else self.srcs[w].at[slot]

    def _dst(self, w, slot):
        return self.dsts[w].at[slot]

    def _copy(self, w, k, outgoing):
        px, py, pc = _peer(k, self.x, self.y, self.c)
        pid = 4 * px + 2 * py + pc
        return pltpu.make_async_remote_copy(
            src_ref=self._src(w, pid if outgoing else self.me),
            dst_ref=self._dst(w, self.me if outgoing else pid),
            send_sem=self.send_sems.at[w, k - 1], recv_sem=self.recv_sems.at[w, k - 1],
            device_id=(px, py, pc), device_id_type=MESH)

    def _local(self, w):
        return pltpu.make_async_copy(self._src(w, self.me), self._dst(w, self.me), self.loc_sems.at[w])

    def start(self):
        for w in range(len(self.srcs)):
            self._local(w).start()
            for k in range(1, N_DEV):
                self._copy(w, k, True).start()

    def wait(self):
        for w in range(len(self.srcs)):
            self._local(w).wait()
            for k in range(1, N_DEV):
                self._copy(w, k, False).wait_recv()
        for w in range(len(self.srcs)):
            for k in range(1, N_DEV):
                self._copy(w, k, True).wait_send()


def _exchange_sems(n_w):
    return [pltpu.SemaphoreType.DMA((n_w, N_DEV - 1)), pltpu.SemaphoreType.DMA((n_w, N_DEV - 1)),
            pltpu.SemaphoreType.DMA((n_w,))]


def _exchange_shapes(kinds, srcs):
    return [_sds(((N_DEV,) if kd == GATHER else ()) + tuple(s.shape), s.dtype) for kd, s in zip(kinds, srcs)]


_HBM = pl.BlockSpec(memory_space=pl.ANY)


def _cast_shards(w_in, w_uq, w_out, w_gate_t, w_up_t, w_down):
    shapes = [(D, IN_SH), (Q_RANK // N_DEV, N_HEADS * QK_DIM), (D // N_DEV, D), (2 * FF_PAD, D), (FF_PAD, D)]

    def body(win, wuq, wout, wg, wu, wd, sin_, suq, sout, sgu, sd):
        sin_[...] = win[...].astype(BF)
        suq[...] = wuq[...].astype(BF)
        sout[...] = wout[...].astype(BF)
        sgu[...] = jnp.zeros(sgu.shape, BF)
        sgu[0:FF_SH, :] = wg[...].astype(BF)
        sgu[FF_PAD:FF_PAD + FF_SH, :] = wu[...].astype(BF)
        sd[...] = jnp.zeros(sd.shape, BF)
        sd[0:FF_SH, :] = wd[...].astype(BF)

    vm = pl.BlockSpec(memory_space=pltpu.VMEM)
    return pl.pallas_call(
        body, name="cast_shards", in_specs=[vm] * 6, out_specs=[vm] * 5,
        out_shape=[_sds(s, BF) for s in shapes],
        compiler_params=pltpu.CompilerParams(vmem_limit_bytes=_VMEM_LIMIT),
    )(w_in, w_uq, w_out, w_gate_t, w_up_t, w_down)


def _gather_two_level(name, srcs):
    n_w = len(srcs)

    def body(*refs):
        src, dst = refs[:n_w], refs[n_w:2 * n_w]
        send_sems, recv_sems, loc_sems = refs[2 * n_w:]
        x, y, c = lax.axis_index("x"), lax.axis_index("y"), lax.axis_index("c")
        me, sibling = (x, y, c), (x, y, 1 - c)
        chips = [(1 - x, y), (x, 1 - y), (1 - x, 1 - y)]
        slot = lambda p: 4 * p[0] + 2 * p[1] + p[2]

        def copy(w, k, block, to, own=False):
            return pltpu.make_async_remote_copy(
                src_ref=src[w] if own else dst[w].at[slot(block)], dst_ref=dst[w].at[slot(block)],
                send_sem=send_sems.at[w, k], recv_sem=recv_sems.at[w, k], device_id=to, device_id_type=MESH)

        local = [pltpu.make_async_copy(src[w], dst[w].at[slot(me)], loc_sems.at[w]) for w in range(n_w)]
        first, passed = [], []
        for w in range(n_w):
            local[w].start()
            first.append(copy(w, 0, me, sibling, own=True))
            first += [copy(w, 1 + j, me, (*chip, c), own=True) for j, chip in enumerate(chips)]
        for cp in first:
            cp.start()
        for w in range(n_w):
            for j, chip in enumerate(chips):
                copy(w, 1 + j, (*chip, c), me).wait_recv()
                passed.append(copy(w, 4 + j, (*chip, c), sibling))
                passed[-1].start()
        for w in range(n_w):
            copy(w, 0, sibling, me).wait_recv()
            for j, chip in enumerate(chips):
                copy(w, 4 + j, (*chip, 1 - c), me).wait_recv()
        for cp in first + passed:
            cp.wait_send()
        for w in range(n_w):
            local[w].wait()

    return pl.pallas_call(
        body, name=name, in_specs=[_HBM] * n_w, out_specs=[_HBM] * n_w,
        out_shape=_exchange_shapes([GATHER] * n_w, srcs), scratch_shapes=_exchange_sems(n_w))(*srcs)


def _row_offsets(arrays):
    offs, rows = [], 0
    for a in arrays:
        offs.append(rows)
        rows += a.shape[0]
    return offs, -(-rows // 8) * 8


def _final_exchange(vecs):
    n_p = len(vecs)
    offs, rows = _row_offsets(vecs)

    def body(*refs):
        g_refs = refs[:n_p]
        rall, pk, send_sems, recv_sems, loc_sem = refs[n_p:]
        x, y, c = lax.axis_index("x"), lax.axis_index("y"), lax.axis_index("c")
        me = 4 * x + 2 * y + c
        pk[...] = jnp.zeros(pk.shape, F32)
        for p in range(n_p):
            r, n = g_refs[p].shape
            pk[offs[p]:offs[p] + r, 0:n] = g_refs[p][...]

        def remote(k):
            return pltpu.make_async_remote_copy(
                src_ref=pk, dst_ref=rall.at[me], send_sem=send_sems.at[k - 1], recv_sem=recv_sems.at[k - 1],
                device_id=_peer(k, x, y, c), device_id_type=MESH)

        def arrival(k):
            px, py, pc = _peer(k, x, y, c)
            return pltpu.make_async_remote_copy(
                src_ref=pk, dst_ref=rall.at[4 * px + 2 * py + pc], send_sem=send_sems.at[k - 1],
                recv_sem=recv_sems.at[k - 1], device_id=(px, py, pc), device_id_type=MESH)

        local = pltpu.make_async_copy(pk, rall.at[me], loc_sem)
        local.start()
        for k in range(1, N_DEV):
            remote(k).start()
        local.wait()
        for k in range(1, N_DEV):
            arrival(k).wait_recv()
        for k in range(1, N_DEV):
            remote(k).wait_send()

    vm = pl.BlockSpec(memory_space=pltpu.VMEM)
    return pl.pallas_call(
        body, name="final_exchange", in_specs=[vm] * n_p, out_specs=vm, out_shape=_sds((N_DEV, rows, D)),
        scratch_shapes=[pltpu.VMEM((rows, D), F32),
                        pltpu.SemaphoreType.DMA((N_DEV - 1,)), pltpu.SemaphoreType.DMA((N_DEV - 1,)),
                        pltpu.SemaphoreType.DMA],
    )(*vecs)


def _small_adam(rall, big_parts, big, ws, ms, vs):
    n_p = len(ws)
    packed = [w for p, w in enumerate(ws) if p != big] + [jax.ShapeDtypeStruct((1, HB), F32)]
    offs, _ = _row_offsets(packed)
    offs = offs[:big] + [None] + offs[big:]

    def total(ref, sl):
        g = ref[(0,) + sl]
        for j in range(1, N_DEV):
            g = g + ref[(j,) + sl]
        return g

    def body(*refs):
        rall_ref, big_ref = refs[:2]
        w_refs, m_refs, v_refs = refs[2:2 + n_p], refs[2 + n_p:2 + 2 * n_p], refs[2 + 2 * n_p:2 + 3 * n_p]
        outs = refs[2 + 3 * n_p:]
        for p in range(n_p):
            r, n = w_refs[p].shape
            if p == big:
                g = total(big_ref, (slice(0, r), slice(0, n)))
            else:
                g = total(rall_ref, (slice(offs[p], offs[p] + r), slice(0, n)))
            delta, m2, v2 = _adamw(w_refs[p][...], g, m_refs[p][...], v_refs[p][...])
            outs[p][...] = g
            outs[n_p + p][...] = delta
            outs[2 * n_p + p][...] = m2
            outs[3 * n_p + p][...] = v2
        outs[4 * n_p][...] = total(rall_ref, (slice(offs[n_p], offs[n_p] + 1), slice(0, HB)))

    vm = pl.BlockSpec(memory_space=pltpu.VMEM)
    res = pl.pallas_call(
        body, name="small_adam", in_specs=[vm] * (2 + 3 * n_p), out_specs=[vm] * (4 * n_p + 1),
        out_shape=[_sds(w.shape) for w in ws] * 4 + [_sds((1, HB))],
        compiler_params=pltpu.CompilerParams(vmem_limit_bytes=_VMEM_LIMIT),
    )(rall, big_parts, *ws, *ms, *vs)
    return res[:n_p], res[n_p:2 * n_p], res[2 * n_p:3 * n_p], res[3 * n_p:4 * n_p], res[4 * n_p]


def _shard_adam(name, parts, w, m, v, tr, row0=0):
    a0, b0 = w.shape
    b = parts.shape[2]
    blk0 = row0 // tr

    def body(p_ref, w_ref, m_ref, v_ref, g_out, d_out, m_out, v_out):
        g = p_ref[0].astype(F32)
        for j in range(1, N_DEV):
            g = g + p_ref[j].astype(F32)
        g = g[:, 0:b0]
        delta, m2, v2 = _adamw(w_ref[...], g, m_ref[...], v_ref[...])
        g_out[...] = g
        d_out[...] = delta
        m_out[...] = m2
        v_out[...] = v2

    blk = pl.BlockSpec((tr, b0), lambda i: (i, 0))
    return _pcall(
        body, name, (a0 // tr,),
        [pl.BlockSpec((N_DEV, tr, b), lambda i: (0, blk0 + i, 0)), blk, blk, blk],
        [blk] * 4, [_sds((a0, b0))] * 4)(parts, w, m, v)


def _fwd_in(x, g_pre, w_in_al, tm):
    T = x.shape[0]

    def body(x_ref, g_ref, w_ref, xm_ref, xh_ref, u_ref):
        nx, _ = _rms(x_ref[...], D)
        u = (nx * g_ref[...]).astype(BF)
        u_ref[...] = u
        xm_ref[...] = jnp.dot(u, w_ref[:, 0:MLA_IN], preferred_element_type=F32)
        xh_ref[...] = jnp.dot(u, w_ref[:, MLA_IN:XP_W], preferred_element_type=F32)

    return _pcall(body, "fwd_in", (T // tm,),
                  [_rows(tm, D), _full((1, D)), _full((D, XP_W))],
                  [_rows(tm, MLA_IN), _rows(tm, 4 * G_W), _rows(tm, D)],
                  [_sds((T, MLA_IN)), _sds((T, 4 * G_W)), _sds((T, D), BF)])(x, g_pre, w_in_al)


def _rope(blk, ta, tb1, tb2):
    return blk * ta + pltpu.roll(blk, HB - ROPE // 2, 1) * tb1 + pltpu.roll(blk, ROPE // 2, 1) * tb2


def _unrope(d, ta, tb1, tb2):
    return d * ta + pltpu.roll(d * tb1, ROPE // 2, 1) + pltpu.roll(d * tb2, HB - ROPE // 2, 1)


def _mla_prep(xp, tabs, g_q, g_kv, w_uq, w_uk, w_uv, tm):
    T = xp.shape[0]
    W = N_HEADS * HB

    def body(xp_ref, ta_ref, tb1_ref, tb2_ref, gq_ref, gkv_ref, wuq_ref, wuk_ref, wuv_ref, q_ref, qs_ref, k_ref, v_ref):
        ta, tb1, tb2 = ta_ref[...], tb1_ref[...], tb2_ref[...]
        nq, _ = _rms(xp_ref[:, 0:Q_RANK], Q_RANK)
        nkv, _ = _rms(xp_ref[:, Q_RANK:Q_RANK + KV_RANK], KV_RANK)
        nkv = (nkv * gkv_ref[...]).astype(BF)
        qpre = _dot(nq * gq_ref[...], wuq_ref[...])
        kpre = jnp.dot(nkv, wuk_ref[...], preferred_element_type=F32)
        v = jnp.dot(nkv, wuv_ref[...], preferred_element_type=F32)
        lane = lax.broadcasted_iota(jnp.int32, (tm, W), 1)
        v_ref[...] = jnp.where((lane & (HB - 1)) == V_DIM, 1.0, v).astype(BF)
        kr = _rope(pltpu.roll(xp_ref[:, Q_RANK + KV_RANK:MLA_IN], NOPE, 1), ta, tb1, tb2)
        for h in range(N_HEADS):
            sl = slice(h * HB, (h + 1) * HB)
            qr = _rope(qpre[:, sl], ta, tb1, tb2)
            q_ref[:, sl] = qr.astype(BF)
            qs_ref[:, sl] = (qr * (QK_DIM ** -0.5 * LOG2E)).astype(BF)
            k_ref[:, sl] = (kpre[:, sl] + kr).astype(BF)

    tab = _rows(tm, HB)
    return _pcall(body, "mla_prep", (T // tm,),
                  [_rows(tm, MLA_IN), tab, tab, tab, _full((1, Q_RANK)), _full((1, KV_RANK)),
                   _full((Q_RANK, W)), _full((KV_RANK, W)), _full((KV_RANK, W))],
                  [_rows(tm, W)] * 4, [_sds((T, W), BF)] * 4)(xp, *tabs, g_q, g_kv, w_uq, w_uk, w_uv)


def _flash_fwd(q, k, v, tq, exchange=None):
    T = q.shape[0]
    hp = _AH
    W = hp * HB

    def body(q_ref, k_ref, v_ref, o_ref, lse_ref):
        i = pl.program_id(1)

        def blk(j, carry, masked):
            st = pl.multiple_of(j * tq, tq)
            out = []
            for h in range(hp):
                ls = slice(h * HB, (h + 1) * HB)
                m, acc = carry[h]
                s = _dot_nt(q_ref[:, ls], k_ref[pl.ds(st, tq), ls])
                if masked:
                    r = lax.broadcasted_iota(jnp.int32, (tq, tq), 0)
                    c = lax.broadcasted_iota(jnp.int32, (tq, tq), 1)
                    s = jnp.where(c <= r, s, NEG)
                m2 = jnp.maximum(m, jnp.max(s, -1, keepdims=True))
                p = jnp.exp2(s - m2)
                out.append((m2, jnp.exp2(m - m2) * acc + _dot(p, v_ref[pl.ds(st, tq), ls])))
            return tuple(out)

        init = tuple((jnp.full((tq, 1), NEG, F32), jnp.zeros((tq, HB), F32)) for _ in range(hp))
        carry = lax.fori_loop(0, i, lambda j, cr: blk(j, cr, False), init)
        res = blk(i, carry, True)
        lane = lax.broadcasted_iota(jnp.int32, (tq, HB), 1)
        for h in range(hp):
            ls = slice(h * HB, (h + 1) * HB)
            m, acc = res[h]
            l = acc[:, V_DIM:V_DIM + 1]
            o_ref[:, ls] = jnp.where(lane < V_DIM, acc / l, 0.0)
            lse_ref[:, ls] = jnp.broadcast_to(m * (1.0 / LOG2E) + jnp.log(l), (tq, HB))

    qs = pl.BlockSpec((tq, W), lambda h, i: (i, h))
    kvs = pl.BlockSpec((T, W), lambda h, i: (0, h))
    return _pcall(body, "flash_fwd", (N_HEADS // hp, T // tq), [qs, kvs, kvs], [qs, qs],
                  [_sds((T, N_HEADS * HB))] * 2, exchange=exchange)(q, k, v)


def _gates(hq, hf, lb):
    sig = _sigmoid(hf)
    f = lb + (1.0 - lb) * sig
    sq = _sigmoid(hq)
    return hq * sq, 1.0 - f, f, jnp.log(f), sig, sq


def _lower_bound(lbl_ref):
    l0, l1 = lbl_ref[0:1, :], lbl_ref[1:2, :]
    mx = jnp.maximum(l0, l1)
    e0, e1 = jnp.exp(l0 - mx), jnp.exp(l1 - mx)
    return e0 / (e0 + e1)


def _split3(x):
    hi = x.astype(BF)
    r1 = x - hi.astype(F32)
    mid = r1.astype(BF)
    lo = (r1 - mid.astype(F32)).astype(BF)
    return hi, mid, lo


def _tri_mm(tri, x):
    hi, mid, lo = _split3(x)
    mm = lambda t: jnp.dot(tri, t, preferred_element_type=F32)
    return mm(hi) + mm(mid) + mm(lo)


def _intra_codes(sub):
    row = lax.broadcasted_iota(jnp.int32, (CHUNK, CHUNK), 0)
    col = lax.broadcasted_iota(jnp.int32, (CHUNK, CHUNK), 1)
    return sub, row, col


def _intra(q, k, b2, b_s, codes, da=None):
    grad = da is not None
    pow2 = (lambda x: jnp.exp2(jnp.minimum(x, 0.0))) if grad else jnp.exp2
    sub, row, col = codes
    a = jnp.zeros((CHUNK, CHUNK), F32)
    dq = jnp.zeros((CHUNK, G_DIM), F32)
    dk = jnp.zeros((CHUNK, G_DIM), F32)
    for i in range(1, CHUNK // sub):
        b0 = b_s[sub * i - 1:sub * i, :]
        eq, ek = pow2(b2 - b0), pow2(b0 - b2)
        mask = ((row // sub) == i) & (col < sub * i)
        a = jnp.where(mask, _dot_nt(q * eq, k * ek), a)
        if grad:
            dai = jnp.where(mask, da, 0.0)
            dq = dq + _dot(dai, k * ek) * eq
            dk = dk + _dot_tn(dai, q * eq) * ek
    for d in range(sub):
        ksh = pltpu.roll(k, d, 0) if d else k
        bsh = pltpu.roll(b2, d, 0) if d else b2
        e = pow2(b2 - bsh)
        mask = (col == row - d) & ((row & (sub - 1)) >= d)
        a = jnp.where(mask, jnp.sum(q * ksh * e, -1, keepdims=True), a)
        if grad:
            g = jnp.sum(jnp.where(mask, da, 0.0), -1, keepdims=True) * e
            dq = dq + g * ksh
            cb = g * q
            dk = dk + (pltpu.roll(cb, CHUNK - d, 0) if d else cb)
    return (a, dq, dk) if grad else a


def _hgrn_fwd(xp, lb_logits, g_hn, exchange=None):
    T = xp.shape[0]
    tb = min(_TB, T)
    ncb = tb // CHUNK
    hp = _HP
    W = hp * G_DIM

    def body(hq_ref, hf_ref, hi_ref, hg_ref, lbl_ref, ghn_ref, out_ref, oraw_ref, sall_ref, st_ref, b_s):
        lb_all = _lower_bound(lbl_ref)

        @pl.when(pl.program_id(1) == 0)
        def _():
            st_ref[...] = jnp.zeros(st_ref.shape, F32)

        row = lax.broadcasted_iota(jnp.int32, (CHUNK, CHUNK), 0)
        col = lax.broadcasted_iota(jnp.int32, (CHUNK, CHUNK), 1)
        tri = (col <= row).astype(BF)
        codes = _intra_codes(SUB_FWD)

        def chunk(c, carry):
            sl = pl.ds(pl.multiple_of(c * CHUNK, CHUNK), CHUNK)
            for h in range(hp):
                ls = slice(h * G_DIM, (h + 1) * G_DIM)
                q, k, _, lf, _, _ = _gates(hq_ref[sl, ls], hf_ref[sl, ls], lb_all[:, ls])
                v = hi_ref[sl, ls]
                b2 = _tri_mm(tri, lf) * LOG2E
                b_s[h] = b2
                st = st_ref[h]
                sall_ref[c, h] = st
                o = _dot_nt(q * jnp.exp2(b2), st) + _dot(_intra(q, k, b2, b_s.at[h], codes), v)
                bl = b_s[h, CHUNK - 1:CHUNK, :]
                st_ref[h] = st * jnp.exp2(bl) + _dot_tn(v, k * jnp.exp2(bl - b2))
                oraw_ref[sl, ls] = o
                n, _ = _rms(o, G_DIM)
                hg = hg_ref[sl, ls]
                out_ref[sl, ls] = n * ghn_ref[:, ls] * (hg * _sigmoid(hg))
            return carry

        lax.fori_loop(0, ncb, chunk, 0)

    col_blk = lambda j: pl.BlockSpec((tb, W), lambda p, t: (t, j * (G_HEADS // hp) + p))
    head = pl.BlockSpec((tb, W), lambda p, t: (t, p))
    return _pcall(
        body, "hgrn_fwd", (G_HEADS // hp, T // tb),
        [col_blk(0), col_blk(1), col_blk(2), col_blk(3),
         pl.BlockSpec((2, W), lambda p, t: (0, p)), pl.BlockSpec((1, W), lambda p, t: (0, p))],
        [head, head, pl.BlockSpec((ncb, hp, G_DIM, G_DIM), lambda p, t: (t, p, 0, 0))],
        [_sds((T, G_W)), _sds((T, G_W)), _sds((T // CHUNK, G_HEADS, G_DIM, G_DIM))],
        scratch=[pltpu.VMEM((hp, G_DIM, G_DIM), F32), pltpu.VMEM((hp, CHUNK, G_DIM), F32)], exchange=exchange,
    )(xp, xp, xp, xp, lb_logits, g_hn)


def _fwd_out(o_pad, o_hgrn, x, g_on, w_out, g_post, g_fpre, tm):
    T = x.shape[0]

    def body(o_ref, oh_ref, x_ref, gon_ref, w_ref, gpost_ref, gfpre_ref, h1_ref, y1_ref, z_ref, mix_ref):
        for h in range(N_HEADS):
            sl = slice(h * HB, (h + 1) * HB)
            n, _ = _rms(o_ref[:, sl], V_DIM)
            mix_ref[:, sl] = (n * gon_ref[:, sl]).astype(BF)
        mix_ref[:, N_HEADS * HB:MIX_W] = oh_ref[...].astype(BF)
        y1 = jnp.dot(mix_ref[...], w_ref[...], preferred_element_type=F32)
        y1_ref[...] = y1
        ny, _ = _rms(y1, D)
        h1 = x_ref[...] + ny * gpost_ref[...]
        h1_ref[...] = h1
        nh, _ = _rms(h1, D)
        z_ref[...] = (nh * gfpre_ref[...]).astype(BF)

    return _pcall(body, "fwd_out", (T // tm,),
                  [_rows(tm, N_HEADS * HB), _rows(tm, G_W), _rows(tm, D), _full((1, N_HEADS * HB)),
                   _full((MIX_W, D)), _full((1, D)), _full((1, D))],
                  [_rows(tm, D), _rows(tm, D), _rows(tm, D), _rows(tm, MIX_W)],
                  [_sds((T, D)), _sds((T, D)), _sds((T, D), BF), _sds((T, MIX_W), BF)],
                  )(o_pad, o_hgrn, x, g_on, w_out, g_post, g_fpre)


def _ffn_fwd(z, wgu, wd, h1, tgt, g_fpost, tm):
    T = z.shape[0]
    nf = wd.shape[0] // _FB

    def body(z_ref, wgu_ref, wd_ref, h1_ref, t_ref, gp_ref,
             as_ref, bs_ref, ff_ref, dh2_ref, dy2_ref, dgp_ref, loss_ref, acc):
        i, j = pl.program_id(0), pl.program_id(1)
        gu = _dot_nt(z_ref[...], wgu_ref[...])
        g = jnp.concatenate([gu[:, 0:FF_PAD], gu[:, 2 * FF_PAD:3 * FF_PAD]], 1)
        u = jnp.concatenate([gu[:, FF_PAD:2 * FF_PAD], gu[:, 3 * FF_PAD:4 * FF_PAD]], 1)
        s = _sigmoid(g)
        b = g * s
        ff = (b * u).astype(BF)
        as_ref[...] = (u * _dsilu(g, s)).astype(BF)
        bs_ref[...] = b.astype(BF)
        ff_ref[...] = ff
        part = jnp.dot(ff, wd_ref[...], preferred_element_type=F32)

        @pl.when(j == 0)
        def _():
            acc[...] = part

        @pl.when(j > 0)
        def _():
            acc[...] += part

        @pl.when((i == 0) & (j == 0))
        def _():
            dgp_ref[...] = jnp.zeros(dgp_ref.shape, F32)
            loss_ref[...] = jnp.zeros(loss_ref.shape, F32)

        @pl.when(j == nf - 1)
        def _():
            ny, r = _rms(acc[...], D)
            err = h1_ref[...] + ny * gp_ref[...] - t_ref[...]
            loss_ref[...] += 0.5 * jnp.sum(jnp.sum(err * err, -1, keepdims=True) * (1.0 / D), 0, keepdims=True)
            dh2 = err * (1.0 / D)
            dh2_ref[...] = dh2
            dy2, dgp = _rms_bwd(ny, r, gp_ref[...], dh2, D)
            dy2_ref[...] = dy2.astype(BF)
            dgp_ref[...] += dgp

    tok = lambda n: pl.BlockSpec((tm, n), lambda i, j: (i, 0))
    col = pl.BlockSpec((tm, _FB), lambda i, j: (i, j))
    return _pcall(
        body, "ffn_fwd", (T // tm, nf),
        [tok(D), pl.BlockSpec((2 * _FB, D), lambda i, j: (j, 0)), pl.BlockSpec((_FB, D), lambda i, j: (j, 0)),
         tok(D), tok(D), _full((1, D))],
        [col, col, col, tok(D), tok(D), _full((1, D)), _full((1, HB))],
        [_sds((T, nf * _FB), BF)] * 3 + [_sds((T, D)), _sds((T, D), BF), _sds((1, D)), _sds((1, HB))],
        scratch=[pltpu.VMEM((tm, D), F32)],
    )(z, wgu, wd, h1, tgt, g_fpost)


def _dsilu(x, s):
    return s * (1.0 + x * (1.0 - s))


def _ffn_bwd_x(dy2, gs, us, wgu, wd, h1, y1, dh2, g_fpre, g_post, tm):
    T = dy2.shape[0]
    nf = wd.shape[0] // _FB

    def body(dy2_ref, gs_ref, us_ref, wgu_ref, wd_ref, h1_ref, y1_ref, dh2_ref, gf_ref, gp_ref,
             dgu_ref, dh1_ref, dy1_ref, dgf_ref, dgp_ref, acc):
        i, j = pl.program_id(0), pl.program_id(1)
        dff = _dot_nt(dy2_ref[...], wd_ref[...])
        dg = (dff * gs_ref[...].astype(F32)).astype(BF)
        du = (dff * us_ref[...].astype(F32)).astype(BF)
        dgu = jnp.concatenate([dg[:, 0:FF_PAD], du[:, 0:FF_PAD], dg[:, FF_PAD:_FB], du[:, FF_PAD:_FB]], 1)
        dgu_ref[...] = dgu
        part = jnp.dot(dgu, wgu_ref[...], preferred_element_type=F32)

        @pl.when(j == 0)
        def _():
            acc[...] = part

        @pl.when(j > 0)
        def _():
            acc[...] += part

        @pl.when((i == 0) & (j == 0))
        def _():
            dgf_ref[...] = jnp.zeros(dgf_ref.shape, F32)
            dgp_ref[...] = jnp.zeros(dgp_ref.shape, F32)

        @pl.when(j == nf - 1)
        def _():
            nh, rh = _rms(h1_ref[...], D)
            dh, dgf = _rms_bwd(nh, rh, gf_ref[...], acc[...], D)
            dh1 = dh2_ref[...] + dh
            dh1_ref[...] = dh1
            dgf_ref[...] += dgf
            ny, ry = _rms(y1_ref[...], D)
            dy1, dgp = _rms_bwd(ny, ry, gp_ref[...], dh1, D)
            dy1_ref[...] = dy1.astype(BF)
            dgp_ref[...] += dgp

    tok = lambda n: pl.BlockSpec((tm, n), lambda i, j: (i, 0))
    col = pl.BlockSpec((tm, _FB), lambda i, j: (i, j))
    return _pcall(
        body, "ffn_bwd_x", (T // tm, nf),
        [tok(D), col, col, pl.BlockSpec((2 * _FB, D), lambda i, j: (j, 0)), pl.BlockSpec((_FB, D), lambda i, j: (j, 0)),
         tok(D), tok(D), tok(D), _full((1, D)), _full((1, D))],
        [pl.BlockSpec((tm, 2 * _FB), lambda i, j: (i, j)), tok(D), tok(D), _full((1, D)), _full((1, D))],
        [_sds((T, 2 * nf * _FB), BF), _sds((T, D)), _sds((T, D), BF), _sds((1, D)), _sds((1, D))],
        scratch=[pltpu.VMEM((tm, D), F32)],
    )(dy2, gs, us, wgu, wd, h1, y1, dh2, g_fpre, g_post)


def _ffn_bwd_w(z, ffs, dgu, dy2, tm):
    T = z.shape[0]
    nf = ffs.shape[1] // _FB
    nt = T // tm

    def body(z_ref, ff_ref, dgu_ref, dy2_ref, dwgu_ref, dwd_ref, agu, ad):
        i = pl.program_id(1)
        pgu = _dot_tn(dgu_ref[...], z_ref[...])
        pd = _dot_tn(ff_ref[...], dy2_ref[...])

        @pl.when(i == 0)
        def _():
            agu[...] = pgu
            ad[...] = pd

        @pl.when(i > 0)
        def _():
            agu[...] += pgu
            ad[...] += pd

        @pl.when(i == nt - 1)
        def _():
            dwgu_ref[...] = agu[...].astype(BF)
            dwd_ref[...] = ad[...].astype(BF)

    F = nf * _FB
    tok = lambda n: pl.BlockSpec((tm, n), lambda j, i: (i, 0))
    return _pcall(
        body, "ffn_bwd_w", (nf, nt),
        [tok(D), pl.BlockSpec((tm, _FB), lambda j, i: (i, j)), pl.BlockSpec((tm, 2 * _FB), lambda j, i: (i, j)), tok(D)],
        [pl.BlockSpec((2 * _FB, D), lambda j, i: (j, 0)), pl.BlockSpec((_FB, D), lambda j, i: (j, 0))],
        [_sds((2 * F, D), BF), _sds((F, D), BF)],
        scratch=[pltpu.VMEM((2 * _FB, D), F32), pltpu.VMEM((_FB, D), F32)],
    )(z, ffs, dgu, dy2)


def _out_bwd(dy1, mix, o_pad, w_out, g_on, tm):
    T = dy1.shape[0]
    W = N_HEADS * HB

    def body(dy1_ref, mix_ref, o_ref, w_ref, gon_ref, do_ref, dl_ref, dohg_ref, dw_ref, dgon_ref):
        i = pl.program_id(0)
        dy1v = dy1_ref[...]
        dmix = _dot_nt(dy1v, w_ref[...])
        pw = _dot_tn(mix_ref[...], dy1v)

        @pl.when(i == 0)
        def _():
            dw_ref[...] = pw
            dgon_ref[...] = jnp.zeros(dgon_ref.shape, F32)

        @pl.when(i > 0)
        def _():
            dw_ref[...] += pw

        for h in range(N_HEADS):
            sl = slice(h * HB, (h + 1) * HB)
            ov = o_ref[:, sl]
            n, r = _rms(ov, V_DIM)
            do, dg = _rms_bwd(n, r, gon_ref[:, sl], dmix[:, sl], V_DIM)
            dgon_ref[:, sl] += dg
            do_ref[:, sl] = do.astype(BF)
            dl_ref[:, sl] = jnp.broadcast_to(jnp.sum(do * ov, -1, keepdims=True), (tm, HB))
        dohg_ref[...] = dmix[:, W:MIX_W]

    return _pcall(body, "out_bwd", (T // tm,),
                  [_rows(tm, D), _rows(tm, MIX_W), _rows(tm, W), _full((MIX_W, D)), _full((1, W))],
                  [_rows(tm, W), _rows(tm, W), _rows(tm, G_W), _full((MIX_W, D)), _full((1, W))],
                  [_sds((T, W), BF), _sds((T, W)), _sds((T, G_W)), _sds((MIX_W, D)), _sds((1, W))],
                  )(dy1, mix, o_pad, w_out, g_on)


def _flash_bwd(q, k, v, do, lse, dl, tq, exchange=None):
    T = q.shape[0]
    nq = T // tq
    scale = QK_DIM ** -0.5
    hp = _AH
    W = hp * HB

    def body(k_ref, v_ref, q_ref, do_ref, lse_ref, dl_ref, dk_ref, dv_ref, dq_ref):
        j = pl.program_id(1)

        @pl.when(j == 0)
        def _():
            dq_ref[...] = jnp.zeros(dq_ref.shape, F32)

        def blk(i, carry, masked):
            sl = pl.ds(pl.multiple_of(i * tq, tq), tq)
            out = []
            for h in range(hp):
                ls = slice(h * HB, (h + 1) * HB)
                dk, dv = carry[h]
                kv, vv = k_ref[:, ls], v_ref[:, ls]
                qv, dov = q_ref[sl, ls], do_ref[sl, ls]
                s = _dot_nt(qv, kv) * scale
                if masked:
                    r = lax.broadcasted_iota(jnp.int32, (tq, tq), 0)
                    c = lax.broadcasted_iota(jnp.int32, (tq, tq), 1)
                    s = jnp.where(c <= r, s, NEG)
                p = jnp.exp(s - lse_ref[sl, h * HB:h * HB + 1])
                ds = p * (_dot_nt(dov, vv) - dl_ref[sl, h * HB:h * HB + 1]) * scale
                dq_ref[sl, ls] += _dot(ds, kv)
                out.append((dk + _dot_tn(ds, qv), dv + _dot_tn(p, dov)))
            return tuple(out)

        zero = jnp.zeros((tq, HB), F32)
        carry = blk(j, tuple((zero, zero) for _ in range(hp)), True)
        res = lax.fori_loop(j + 1, nq, lambda i, cr: blk(i, cr, False), carry)
        for h in range(hp):
            ls = slice(h * HB, (h + 1) * HB)
            dk_ref[:, ls] = res[h][0]
            dv_ref[:, ls] = res[h][1]

    tile = pl.BlockSpec((tq, W), lambda h, j: (j, h))
    whole = pl.BlockSpec((T, W), lambda h, j: (0, h))
    return _pcall(body, "flash_bwd", (N_HEADS // hp, nq), [tile, tile, whole, whole, whole, whole],
                  [tile, tile, whole], [_sds((T, N_HEADS * HB))] * 3, exchange=exchange)(k, v, q, do, lse, dl)


def _mla_prep_bwd(xp, tabs, dq, dk, dv, g_q, g_kv, w_uq, w_uk, w_uv, tm):
    T = xp.shape[0]
    W = N_HEADS * HB

    def body(xp_ref, ta_ref, tb1_ref, tb2_ref, dq_ref, dk_ref, dv_ref, gq_ref, gkv_ref, wuq_ref, wuk_ref, wuv_ref,
             dxp_ref, dwuq_ref, dwuk_ref, dwuv_ref, dgq_ref, dgkv_ref, dqp):
        i = pl.program_id(0)
        ta, tb1, tb2 = ta_ref[...], tb1_ref[...], tb2_ref[...]
        nq, rq = _rms(xp_ref[:, 0:Q_RANK], Q_RANK)
        nkv, rkv = _rms(xp_ref[:, Q_RANK:Q_RANK + KV_RANK], KV_RANK)
        dkr = jnp.zeros((tm, HB), F32)
        for h in range(N_HEADS):
            sl = slice(h * HB, (h + 1) * HB)
            dqp[:, sl] = _unrope(dq_ref[:, sl], ta, tb1, tb2).astype(BF)
            dkr = dkr + dk_ref[:, sl]
        dkr = pltpu.roll(_unrope(dkr, ta, tb1, tb2), HB - NOPE, 1)
        lane = lax.broadcasted_iota(jnp.int32, (tm, HB), 1)
        dxp_ref[:, Q_RANK + KV_RANK:MLA_IN] = jnp.where(lane < ROPE, dkr, 0.0)
        dqpv = dqp[...]
        dkv, dvv = dk_ref[...].astype(BF), dv_ref[...].astype(BF)
        nqs = (nq * gq_ref[...]).astype(BF)
        nkvs = (nkv * gkv_ref[...]).astype(BF)
        pq, pk, pv = _dot_tn(nqs, dqpv), _dot_tn(nkvs, dkv), _dot_tn(nkvs, dvv)
        dcq, dgq = _rms_bwd(nq, rq, gq_ref[...], _dot_nt(dqpv, wuq_ref[...]), Q_RANK)
        dckv, dgkv = _rms_bwd(nkv, rkv, gkv_ref[...], _dot_nt(dkv, wuk_ref[...]) + _dot_nt(dvv, wuv_ref[...]), KV_RANK)
        dxp_ref[:, 0:Q_RANK] = dcq
        dxp_ref[:, Q_RANK:Q_RANK + KV_RANK] = dckv

        @pl.when(i == 0)
        def _():
            dwuq_ref[...] = pq
            dwuk_ref[...] = pk
            dwuv_ref[...] = pv
            dgq_ref[...] = dgq
            dgkv_ref[...] = dgkv

        @pl.when(i > 0)
        def _():
            dwuq_ref[...] += pq
            dwuk_ref[...] += pk
            dwuv_ref[...] += pv
            dgq_ref[...] += dgq
            dgkv_ref[...] += dgkv

    tab = _rows(tm, HB)
    return _pcall(
        body, "mla_prep_bwd", (T // tm,),
        [_rows(tm, MLA_IN), tab, tab, tab, _rows(tm, W), _rows(tm, W), _rows(tm, W), _full((1, Q_RANK)),
         _full((1, KV_RANK)), _full((Q_RANK, W)), _full((KV_RANK, W)), _full((KV_RANK, W))],
        [_rows(tm, MLA_IN), _full((Q_RANK, W)), _full((KV_RANK, W)), _full((KV_RANK, W)), _full((1, Q_RANK)),
         _full((1, KV_RANK))],
        [_sds((T, MLA_IN)), _sds((Q_RANK, W)), _sds((KV_RANK, W)), _sds((KV_RANK, W)), _sds((1, Q_RANK)),
         _sds((1, KV_RANK))],
        scratch=[pltpu.VMEM((tm, W), BF)],
    )(xp, *tabs, dq, dk, dv, g_q, g_kv, w_uq, w_uk, w_uv)


def _hgrn_bwd(xp, o_raw, s_all, d_out, lb_logits, g_hn, exchange=None):
    T = xp.shape[0]
    tb = min(_TB, T)
    ncb = tb // CHUNK
    nb = T // tb
    hp = _HP
    W = hp * G_DIM

    def body(hq_ref, hf_ref, hi_ref, hg_ref, o_ref, sall_ref, dout_ref, lbl_ref, ghn_ref,
             dhq_ref, dhf_ref, dhi_ref, dhg_ref, dlbl_ref, dghn_ref, dst_ref, b_s, acc_lb, acc_g):
        t = pl.program_id(1)
        lb_all = _lower_bound(lbl_ref)

        @pl.when(t == 0)
        def _():
            dst_ref[...] = jnp.zeros(dst_ref.shape, F32)
            acc_lb[...] = jnp.zeros(acc_lb.shape, F32)
            acc_g[...] = jnp.zeros(acc_g.shape, F32)

        row = lax.broadcasted_iota(jnp.int32, (CHUNK, CHUNK), 0)
        col = lax.broadcasted_iota(jnp.int32, (CHUNK, CHUNK), 1)
        tri = (col <= row).astype(BF)
        tri_t = (col >= row).astype(BF)
        codes = _intra_codes(SUB_BWD)
        last = lax.broadcasted_iota(jnp.int32, (CHUNK, G_DIM), 0) == CHUNK - 1

        def chunk(cc, carry):
            c = ncb - 1 - cc
            sl = pl.ds(pl.multiple_of(c * CHUNK, CHUNK), CHUNK)
            for h in range(hp):
                ls = slice(h * G_DIM, (h + 1) * G_DIM)
                lb, ghn = lb_all[:, ls], ghn_ref[:, ls]
                hq, hg = hq_ref[sl, ls], hg_ref[sl, ls]
                q, k, f, lf, sig, sq = _gates(hq, hf_ref[sl, ls], lb)
                v = hi_ref[sl, ls]
                b2 = _tri_mm(tri, lf) * LOG2E
                b_s[h] = b2
                st = sall_ref[c, h]
                dstn = dst_ref[h]
                o = o_ref[sl, ls]
                dout = dout_ref[sl, ls]
                n, r = _rms(o, G_DIM)
                sg = _sigmoid(hg)
                dhg_ref[sl, ls] = dout * (n * ghn) * _dsilu(hg, sg)
                do, dg = _rms_bwd(n, r, ghn, dout * (hg * sg), G_DIM)
                acc_g[:, ls] += dg
                eb = jnp.exp2(b2)
                bl = b_s[h, CHUNK - 1:CHUNK, :]
                ebl = jnp.exp2(bl)
                ekd = jnp.exp2(bl - b2)
                kd = k * ekd
                a, dq_i, dk_i = _intra(q, k, b2, b_s.at[h], codes, _dot_nt(do, v))
                dhi_ref[sl, ls] = _dot_tn(a, do) + _dot_nt(kd, dstn)
                dk_state = _dot(v, dstn) * ekd
                dq = dq_i + _dot(do, st) * eb
                dk = dk_i + dk_state
                dbl = jnp.sum(k * dk_state, 0, keepdims=True) + ebl * jnp.sum(dstn * st, 0, keepdims=True)
                db = q * dq - k * dk + jnp.where(last, dbl, 0.0)
                df = _tri_mm(tri_t, db) / f - dk
                dhf_ref[sl, ls] = df * (1.0 - lb) * sig * (1.0 - sig)
                acc_lb[:, ls] += jnp.sum(df * (1.0 - sig), 0, keepdims=True)
                dhq_ref[sl, ls] = dq * _dsilu(hq, sq)
                dst_ref[h] = dstn * ebl + _dot_tn(do, q * eb)
            return carry

        lax.fori_loop(0, ncb, chunk, 0)

        @pl.when(t == nb - 1)
        def _():
            dl0 = acc_lb[...] * lb_all * (1.0 - lb_all)
            dlbl_ref[0:1, :] = dl0
            dlbl_ref[1:2, :] = -dl0
            dghn_ref[...] = acc_g[...]

    col_blk = lambda j: pl.BlockSpec((tb, W), lambda p, t: (nb - 1 - t, j * (G_HEADS // hp) + p))
    head = pl.BlockSpec((tb, W), lambda p, t: (nb - 1 - t, p))
    two = pl.BlockSpec((2, W), lambda p, t: (0, p))
    one = pl.BlockSpec((1, W), lambda p, t: (0, p))
    res = _pcall(
        body, "hgrn_bwd", (G_HEADS // hp, nb),
        [col_blk(0), col_blk(1), col_blk(2), col_blk(3), head,
         pl.BlockSpec((ncb, hp, G_DIM, G_DIM), lambda p, t: (nb - 1 - t, p, 0, 0)), head, two, one],
        [head, head, head, head, two, one],
        [_sds((T, G_W))] * 4 + [_sds((2, G_W)), _sds((1, G_W))],
        scratch=[pltpu.VMEM((hp, G_DIM, G_DIM), F32), pltpu.VMEM((hp, CHUNK, G_DIM), F32),
                 pltpu.VMEM((1, W), F32), pltpu.VMEM((1, W), F32)], exchange=exchange,
    )(xp, xp, xp, xp, o_raw, s_all, d_out, lb_logits, g_hn)
    return res


def _in_bwd_x(x, dxp_m, dxp_h, dh1, w_in_al, g_pre, tm, exchange=None):
    T = x.shape[0]

    def body(x_ref, dm_ref, d0_ref, d1_ref, d2_ref, d3_ref, dh1_ref, w_ref, g_ref, dx_ref, dg_ref):
        i = pl.program_id(0)
        du = _dot_nt(dm_ref[...], w_ref[:, 0:MLA_IN])
        for j, d_ref in enumerate((d0_ref, d1_ref, d2_ref, d3_ref)):
            du = du + _dot_nt(d_ref[...], w_ref[:, MLA_IN + j * G_W:MLA_IN + (j + 1) * G_W])
        nx, r = _rms(x_ref[...], D)
        dx, dg = _rms_bwd(nx, r, g_ref[...], du, D)
        dx_ref[...] = dh1_ref[...] + dx

        @pl.when(i == 0)
        def _():
            dg_ref[...] = dg

        @pl.when(i > 0)
        def _():
            dg_ref[...] += dg

    return _pcall(body, "in_bwd_x", (T // tm,),
                  [_rows(tm, D), _rows(tm, MLA_IN)] + [_rows(tm, G_W)] * 4 + [_rows(tm, D), _full((D, XP_W)), _full((1, D))],
                  [_rows(tm, D), _full((1, D))], [_sds((T, D)), _sds((1, D))], exchange=exchange,
                  )(x, dxp_m, *dxp_h, dh1, w_in_al, g_pre)


def _aligned_col(c):
    return jnp.where(c < Q_RANK + KV_RANK + ROPE, c, c + (KR_PAD - ROPE))


def _align_w_in(g_in):
    tile = 384
    kr_end = Q_RANK + KV_RANK + ROPE

    def body(g_ref, o_ref, gp):
        gp[...] = jnp.zeros(gp.shape, BF)
        for j in range(N_DEV):
            gp[j, :, 0:IN_SH] = g_ref[j]
        r = lax.broadcasted_iota(jnp.int32, (tile, tile), 0)
        c = lax.broadcasted_iota(jnp.int32, (tile, tile), 1)
        for t in range(XP_W // tile):
            lo, hi = t * tile, (t + 1) * tile
            cols = [a if a < kr_end else a - (KR_PAD - ROPE) for a in (lo, hi - 1)]
            acc = jnp.zeros((D, tile), F32)
            for j in range(cols[0] // IN_SH, cols[-1] // IN_SH + 1):
                sel = (r < IN_SH) & (_aligned_col(j * IN_SH + r) == lo + c)
                acc = acc + jnp.dot(gp[j], sel.astype(BF), preferred_element_type=F32)
            o_ref[:, lo:hi] = acc.astype(BF)

    vm = pl.BlockSpec(memory_space=pltpu.VMEM)
    return pl.pallas_call(
        body, name="align_w_in", in_specs=[vm], out_specs=vm, out_shape=_sds((D, XP_W), BF),
        scratch_shapes=[pltpu.VMEM((N_DEV, D, tile), BF)],
        compiler_params=pltpu.CompilerParams(vmem_limit_bytes=_VMEM_LIMIT))(g_in)


def _in_bwd_w(u, dxp_m, dxp_h, tm):
    T = u.shape[0]
    nt = T // tm
    win = 640

    def body(u_ref, dm_ref, d0_ref, d1_ref, d2_ref, d3_ref, o_ref, acc):
        i = pl.program_id(0)
        ut = u_ref[...].T
        parts = [(0, MLA_IN, dm_ref)] + [(MLA_IN + j * G_W, G_W, d) for j, d in enumerate((d0_ref, d1_ref, d2_ref, d3_ref))]

        @pl.when(i == 0)
        def _():
            for lo, n, d in parts:
                acc[:, lo:lo + n] = jnp.dot(ut, d[...].astype(BF), preferred_element_type=F32)

        @pl.when(i > 0)
        def _():
            for lo, n, d in parts:
                acc[:, lo:lo + n] += jnp.dot(ut, d[...].astype(BF), preferred_element_type=F32)

        @pl.when(i == nt - 1)
        def _():
            wide = 384
            r = lax.broadcasted_iota(jnp.int32, (win, wide), 0)
            c = lax.broadcasted_iota(jnp.int32, (win, wide), 1)
            kr_end = Q_RANK + KV_RANK + ROPE
            for j in range(N_DEV):
                first = j * IN_SH if j * IN_SH < kr_end else j * IN_SH + (KR_PAD - ROPE)
                lo = min(first // HB * HB, XP_W - win)
                sel = (c < IN_SH) & (_aligned_col(j * IN_SH + c) == lo + r)
                res = jnp.dot(acc[:, lo:lo + win].astype(BF), sel.astype(BF), preferred_element_type=F32)
                o_ref[j] = res[:, 0:IN_SH].astype(BF)

    return _pcall(body, "in_bwd_w", (nt,),
                  [_rows(tm, D), _rows(tm, MLA_IN)] + [_rows(tm, G_W)] * 4,
                  [_full((N_DEV, D, IN_SH))], [_sds((N_DEV, D, IN_SH), BF)],
                  scratch=[pltpu.VMEM((D, XP_W), F32)])(u, dxp_m, *dxp_h)[0]


def _pad_heads(w, width, real):
    lead = w.shape[:-1]
    w = w.reshape(lead + (N_HEADS, real))
    w = jnp.pad(w, [(0, 0)] * len(lead) + [(0, 0), (0, width - real)])
    return w.reshape(lead + (N_HEADS * width,))


def _unpad_heads(w, width, real):
    lead = w.shape[:-1]
    return w.reshape(lead + (N_HEADS, width))[..., :real].reshape(lead + (N_HEADS * real,))


def _rope_tables(positions):
    half = ROPE // 2
    inv_freq = 1.0 / (ROPE_THETA ** (jnp.arange(0, ROPE, 2, dtype=F32) / ROPE))
    ang = positions.astype(F32)[:, None] * inv_freq
    cos, sin = jnp.cos(ang), jnp.sin(ang)
    T = positions.shape[0]
    z = lambda n: jnp.zeros((T, n), F32)
    ta = jnp.concatenate([jnp.ones((T, NOPE), F32), cos, cos, z(HB - QK_DIM)], 1)
    tb1 = jnp.concatenate([z(NOPE), -sin, z(half), z(HB - QK_DIM)], 1)
    tb2 = jnp.concatenate([z(NOPE), z(half), sin, z(HB - QK_DIM)], 1)
    return ta, tb1, tb2


def kernel(x, positions, attn_pre_norm, w_in, mla_q_norm, mla_w_uq, mla_kv_norm, mla_w_ukv, mla_out_norm, hgrn_lb_logits, hgrn_out_norm, w_out, attn_post_norm, ffn_pre_norm, w_gate, w_up, w_down, ffn_post_norm, loss_target, m_attn_pre_norm, m_w_in, m_mla_q_norm, m_mla_w_uq, m_mla_kv_norm, m_mla_w_ukv, m_mla_out_norm, m_hgrn_lb_logits, m_hgrn_out_norm, m_w_out, m_attn_post_norm, m_ffn_pre_norm, m_w_gate, m_w_up, m_w_down, m_ffn_post_norm, v_attn_pre_norm, v_w_in, v_mla_q_norm, v_mla_w_uq, v_mla_kv_norm, v_mla_w_ukv, v_mla_out_norm, v_hgrn_lb_logits, v_hgrn_out_norm, v_w_out, v_attn_post_norm, v_ffn_pre_norm, v_w_gate, v_w_up, v_w_down, v_ffn_post_norm):
    T = x.shape[1]
    tm = min(_TM, T)
    tq = min(_TQ, T)
    xs, tgt = x[0], loss_target[0]
    uq_sh = (Q_RANK // N_DEV, N_HEADS * QK_DIM)

    b_in, b_uq, b_out, b_gu, b_d = _cast_shards(
        w_in[0], mla_w_uq[0].reshape(uq_sh), w_out[0], w_gate[0].T, w_up[0].T, w_down[0])
    g_in, g_uq = _gather_two_level("ag_first", [b_in, b_uq])
    w_in_al = _align_w_in(g_in)
    w_uq_p = _pad_heads(g_uq.reshape(Q_RANK, N_HEADS * QK_DIM), HB, QK_DIM)
    w_ukv = mla_w_ukv[0].astype(BF)
    w_uk_p = _pad_heads(w_ukv[..., :NOPE].reshape(KV_RANK, N_HEADS * NOPE), HB, NOPE)
    w_uv_p = _pad_heads(w_ukv[..., NOPE:].reshape(KV_RANK, N_HEADS * V_DIM), HB, V_DIM)
    g_on_p = _pad_heads(mla_out_norm, HB, V_DIM)
    tabs = _rope_tables(positions[0])

    xp_m, xp_h, u = _fwd_in(xs, attn_pre_norm, w_in_al, tm)
    q_att, qs_att, k_att, v_att = _mla_prep(xp_m, tabs, mla_q_norm, mla_kv_norm, w_uq_p, w_uk_p, w_uv_p, tm)
    o_hgrn, o_raw, s_all, g_out, wd = _hgrn_fwd(xp_h, hgrn_lb_logits, hgrn_out_norm, ([GATHER, GATHER], [b_out, b_d]))
    wd = wd.reshape(N_DEV * FF_PAD, D)
    o_pad, lse, wgu = _flash_fwd(qs_att, k_att, v_att, tq, ([GATHER], [b_gu]))
    wgu = wgu.reshape(N_DEV * 2 * FF_PAD, D)
    w_out_full = g_out.reshape(D, D)
    w_out_mla = jnp.pad(w_out_full[:N_HEADS * V_DIM].reshape(N_HEADS, V_DIM, D), ((0, 0), (0, HB - V_DIM), (0, 0)))
    w_out_p = jnp.concatenate([w_out_mla.reshape(N_HEADS * HB, D), w_out_full[N_HEADS * V_DIM:]], 0)
    h1, y1, z, mix = _fwd_out(o_pad, o_hgrn, xs, g_on_p, w_out_p, attn_post_norm, ffn_pre_norm, tm)
    tmf = min(_TMF, T)
    gs, us, ffs, dh2, dy2, d_fpost, loss_row = _ffn_fwd(z, wgu, wd, h1, tgt, ffn_post_norm, tm)

    dgu, dh1, dy1, d_fpre, d_post = _ffn_bwd_x(dy2, gs, us, wgu, wd, h1, y1, dh2, ffn_pre_norm, attn_post_norm, tm)
    dwgu, dwd = _ffn_bwd_w(z, ffs, dgu, dy2, tmf)
    do_pad, dl, d_ohg, dw_out_p, d_on_p = _out_bwd(dy1, mix, o_pad, w_out_p, g_on_p, tm)
    dw_out_mla = dw_out_p[:N_HEADS * HB].reshape(N_HEADS, HB, D)[:, :V_DIM].reshape(N_HEADS * V_DIM, D)
    dw_out = jnp.concatenate([dw_out_mla, dw_out_p[N_HEADS * HB:]], 0).reshape(N_DEV, D // N_DEV, D).astype(BF)
    dk_att, dv_att, dq_att, p_gu, p_d, p_out = _flash_bwd(
        q_att, k_att, v_att, do_pad, lse, dl, tq,
        ([SCATTER] * 3, [dwgu.reshape(N_DEV, 2 * FF_PAD, D), dwd.reshape(N_DEV, FF_PAD, D), dw_out]))
    dxp_m, dw_uq_p, dw_uk_p, dw_uv_p, d_gq, d_gkv = _mla_prep_bwd(
        xp_m, tabs, dq_att, dk_att, dv_att, mla_q_norm, mla_kv_norm, w_uq_p, w_uk_p, w_uv_p, tm)
    dw_uq = _unpad_heads(dw_uq_p, HB, QK_DIM).reshape((N_DEV,) + uq_sh).astype(BF)
    dw_ukv = jnp.concatenate([_unpad_heads(dw_uk_p, HB, NOPE).reshape(KV_RANK, N_HEADS, NOPE),
                              _unpad_heads(dw_uv_p, HB, V_DIM).reshape(KV_RANK, N_HEADS, V_DIM)], -1)
    *dxp_h, d_lbl, d_ghn, p_uq, dw_ukv_all = _hgrn_bwd(
        xp_h, o_raw, s_all, d_ohg, hgrn_lb_logits, hgrn_out_norm,
        ([SCATTER, GATHER], [dw_uq, dw_ukv.reshape(KV_RANK, N_HEADS * HB)]))
    dw_in = _in_bwd_w(u, dxp_m, dxp_h, tm)
    grad_x, d_pre, p_in = _in_bwd_x(xs, dxp_m, dxp_h, dh1, w_in_al, attn_pre_norm, tm, ([SCATTER], [dw_in]))
    d_on = _unpad_heads(d_on_p, HB, V_DIM)

    ukv2 = lambda a: a.reshape(KV_RANK, N_HEADS * HB)
    vecs = [d_pre, d_gq, d_gkv, d_on, d_lbl, d_ghn, d_post, d_fpre, d_fpost, loss_row]
    small_w = [attn_pre_norm, mla_q_norm, mla_kv_norm, ukv2(mla_w_ukv), mla_out_norm, hgrn_lb_logits, hgrn_out_norm,
               attn_post_norm, ffn_pre_norm, ffn_post_norm]
    small_m = [m_attn_pre_norm, m_mla_q_norm, m_mla_kv_norm, ukv2(m_mla_w_ukv), m_mla_out_norm, m_hgrn_lb_logits,
               m_hgrn_out_norm, m_attn_post_norm, m_ffn_pre_norm, m_ffn_post_norm]
    small_v = [v_attn_pre_norm, v_mla_q_norm, v_mla_kv_norm, ukv2(v_mla_w_ukv), v_mla_out_norm, v_hgrn_lb_logits,
               v_hgrn_out_norm, v_attn_post_norm, v_ffn_pre_norm, v_ffn_post_norm]
    rall = _final_exchange(vecs)
    s_g, s_d, s_m, s_v, loss_all = _small_adam(rall, dw_ukv_all, 3, small_w, small_m, small_v)
    r_in = _shard_adam("adam_w_in", p_in, w_in[0], m_w_in[0], v_w_in[0], 256)
    r_uq = _shard_adam("adam_w_uq", p_uq, mla_w_uq[0].reshape(uq_sh), m_mla_w_uq[0].reshape(uq_sh),
                       v_mla_w_uq[0].reshape(uq_sh), uq_sh[0])
    r_out = _shard_adam("adam_w_out", p_out, w_out[0], m_w_out[0], v_w_out[0], D // N_DEV)
    r_g = [a.T for a in _shard_adam("adam_w_gate", p_gu, w_gate[0].T, m_w_gate[0].T, v_w_gate[0].T, 32)]
    r_u = [a.T for a in _shard_adam("adam_w_up", p_gu, w_up[0].T, m_w_up[0].T, v_w_up[0].T, 32, FF_PAD)]
    r_d = _shard_adam("adam_w_down", p_d, w_down[0], m_w_down[0], v_w_down[0], FF_SH // 2)

    loss = loss_all[0, 0]

    def assemble(big, small):
        b_in, b_uq, b_out, b_g, b_u, b_d = big
        return [small[0], b_in[None], small[1], b_uq.reshape(mla_w_uq.shape), small[2],
                small[3].reshape(mla_w_ukv.shape), small[4], small[5], small[6], b_out[None], small[7], small[8],
                b_g[None], b_u[None], b_d[None], small[9]]

    outs = [loss, grad_x[None]]
    for idx, small in enumerate((s_g, s_d, s_m, s_v)):
        outs += assemble([r[idx] for r in (r_in, r_uq, r_out, r_g, r_u, r_d)], small)
    return tuple(outs)
```

```python
import jax
import jax.numpy as jnp
from jax import lax
from jax.experimental import pallas as pl
from jax.experimental.pallas import tpu as pltpu

BF = jnp.bfloat16
F32 = jnp.float32
MESH = pl.DeviceIdType.MESH

N_DEV = 8
D = 1024
EPS = 1e-6
LOG2E = 1.4426950408889634
ROPE_THETA = 10000.0
N_HEADS = 8
HB = 128
NOPE = 64
ROPE = 32
V_DIM = 64
QK_DIM = NOPE + ROPE
Q_RANK = 384
KV_RANK = 128
KR_PAD = 128
MLA_IN = Q_RANK + KV_RANK + KR_PAD
G_HEADS = 4
G_DIM = 128
G_W = G_HEADS * G_DIM
CHUNK = 64
SUB_FWD = 16
SUB_BWD = 16
XP_W = MLA_IN + 4 * G_W
IN_SH = 324
IN_W = N_DEV * IN_SH
FF_SH = 352
FF_PAD = 384
MIX_W = N_HEADS * HB + G_W

ADAM_LR = 0.001
ADAM_B1 = 0.9
ADAM_B2 = 0.999
ADAM_EPS = 1e-08
ADAM_WD = 0.01
ADAM_STEP = 10

_TM = 512
_TMF = 1024
_TQ = 512
_AH = 2
_FB = 768
_TB = 1024
_HP = 4
_VMEM_LIMIT = 56 * 1024 * 1024
NEG = -1e30


def _dot(a, b):
    return jnp.dot(a.astype(BF), b.astype(BF), preferred_element_type=F32)


def _dot_nt(a, b):
    return lax.dot_general(a.astype(BF), b.astype(BF), (((1,), (1,)), ((), ())), preferred_element_type=F32)


def _dot_tn(a, b):
    return lax.dot_general(a.astype(BF), b.astype(BF), (((0,), (0,)), ((), ())), preferred_element_type=F32)


def _sigmoid(x):
    return 1.0 / (1.0 + jnp.exp(-x))


def _rms(x, n):
    r = lax.rsqrt(jnp.sum(x * x, -1, keepdims=True) * (1.0 / n) + EPS)
    return x * r, r


def _rms_bwd(nx, r, g, dy, n):
    dg = jnp.sum(dy * nx, 0, keepdims=True)
    dn = dy * g
    dx = r * (dn - nx * (jnp.sum(dn * nx, -1, keepdims=True) * (1.0 / n)))
    return dx, dg


def _adamw(w, g, m, v):
    m2 = ADAM_B1 * m + (1.0 - ADAM_B1) * g
    v2 = ADAM_B2 * v + (1.0 - ADAM_B2) * (g * g)
    m_hat = m2 / (1.0 - ADAM_B1 ** ADAM_STEP)
    v_hat = v2 / (1.0 - ADAM_B2 ** ADAM_STEP)
    delta = -ADAM_LR * (m_hat / (jnp.sqrt(v_hat) + ADAM_EPS) + ADAM_WD * w)
    return delta, m2, v2


def _pcall(body, name, grid, in_specs, out_specs, out_shape, scratch=(), exchange=None):
    scratch = list(scratch)
    extra = ()
    if exchange is not None:
        kinds, extra = exchange
        in_specs, out_specs, out_shape = list(in_specs), list(out_specs), list(out_shape)
        n_in, n_out, n_scr, n_x = len(in_specs), len(out_specs), len(scratch), len(extra)
        inner = body

        def body(*refs):
            ins, rest = refs[:n_in], refs[n_in:]
            x_src, rest = rest[:n_x], rest[n_x:]
            outs, rest = rest[:n_out], rest[n_out:]
            x_dst, rest = rest[:n_x], rest[n_x:]
            ex = _Exchange(kinds, x_src, x_dst, *rest[n_scr:])
            first = pl.program_id(0) == 0
            last = pl.program_id(0) == grid[0] - 1
            for a in range(1, len(grid)):
                first = first & (pl.program_id(a) == 0)
                last = last & (pl.program_id(a) == grid[a] - 1)
            pl.when(first)(ex.start)
            inner(*ins, *outs, *rest[:n_scr])
            pl.when(last)(ex.wait)

        in_specs += [_HBM] * n_x
        out_specs += [_HBM] * n_x
        out_shape += _exchange_shapes(kinds, extra)
        scratch += _exchange_sems(n_x)
    call = pl.pallas_call(
        body, name=name, grid=grid, in_specs=in_specs, out_specs=out_specs, out_shape=out_shape,
        scratch_shapes=scratch,
        compiler_params=pltpu.CompilerParams(
            dimension_semantics=("arbitrary",) * len(grid), vmem_limit_bytes=_VMEM_LIMIT))
    return lambda *operands: call(*operands, *extra)


def _full(shape):
    return pl.BlockSpec(shape, lambda *_: (0,) * len(shape))


def _rows(tm, n):
    return pl.BlockSpec((tm, n), lambda i, *_: (i, 0))


def _sds(shape, dtype=F32):
    return jax.ShapeDtypeStruct(shape, dtype)


def _peer(k, x, y, c):
    px = 1 - x if (k >> 2) & 1 else x
    py = 1 - y if (k >> 1) & 1 else y
    pc = 1 - c if k & 1 else c
    return px, py, pc


GATHER, SCATTER = "gather", "scatter"


class _Exchange:
    def __init__(self, kinds, srcs, dsts, send_sems, recv_sems, loc_sems):
        self.kinds, self.srcs, self.dsts = kinds, srcs, dsts
        self.send_sems, self.recv_sems, self.loc_sems = send_sems, recv_sems, loc_sems
        self.x, self.y, self.c = lax.axis_index("x"), lax.axis_index("y"), lax.axis_index("c")
        self.me = 4 * self.x + 2 * self.y + self.c

    def _src(self, w, slot):
        return self.srcs[w] if self.kinds[w] == GATHER else self.srcs[w].at[slot]

    def _dst(self, w, slot):
        return self.dsts[w].at[slot]

    def _copy(self, w, k, outgoing):
        px, py, pc = _peer(k, self.x, self.y, self.c)
        pid = 4 * px + 2 * py + pc
        return pltpu.make_async_remote_copy(
            src_ref=self._src(w, pid if outgoing else self.me),
            dst_ref=self._dst(w, self.me if outgoing else pid),
            send_sem=self.send_sems.at[w, k - 1], recv_sem=self.recv_sems.at[w, k - 1],
            device_id=(px, py, pc), device_id_type=MESH)

    def _local(self, w):
        return pltpu.make_async_copy(self._src(w, self.me), self._dst(w, self.me), self.loc_sems.at[w])

    def start(self):
        for w in range(len(self.srcs)):
            self._local(w).start()
            for k in range(1, N_DEV):
                self._copy(w, k, True).start()

    def wait(self):
        for w in range(len(self.srcs)):
            self._local(w).wait()
            for k in range(1, N_DEV):
                self._copy(w, k, False).wait_recv()
        for w in range(len(self.srcs)):
            for k in range(1, N_DEV):
                self._copy(w, k, True).wait_send()


def _exchange_sems(n_w):
    return [pltpu.SemaphoreType.DMA((n_w, N_DEV - 1)), pltpu.SemaphoreType.DMA((n_w, N_DEV - 1)),
            pltpu.SemaphoreType.DMA((n_w,))]


def _exchange_shapes(kinds, srcs):
    return [_sds(((N_DEV,) if kd == GATHER else ()) + tuple(s.shape), s.dtype) for kd, s in zip(kinds, srcs)]


_HBM = pl.BlockSpec(memory_space=pl.ANY)


def _cast_shards(w_in, w_uq, w_out, w_gate_t, w_up_t, w_down):
    shapes = [(D, IN_SH), (Q_RANK // N_DEV, N_HEADS * QK_DIM), (D // N_DEV, D), (2 * FF_PAD, D), (FF_PAD, D)]

    def body(win, wuq, wout, wg, wu, wd, sin_, suq, sout, sgu, sd):
        sin_[...] = win[...].astype(BF)
        suq[...] = wuq[...].astype(BF)
        sout[...] = wout[...].astype(BF)
        sgu[...] = jnp.zeros(sgu.shape, BF)
        sgu[0:FF_SH, :] = wg[...].astype(BF)
        sgu[FF_PAD:FF_PAD + FF_SH, :] = wu[...].astype(BF)
        sd[...] = jnp.zeros(sd.shape, BF)
        sd[0:FF_SH, :] = wd[...].astype(BF)

    vm = pl.BlockSpec(memory_space=pltpu.VMEM)
    return pl.pallas_call(
        body, name="cast_shards", in_specs=[vm] * 6, out_specs=[vm] * 5,
        out_shape=[_sds(s, BF) for s in shapes],
        compiler_params=pltpu.CompilerParams(vmem_limit_bytes=_VMEM_LIMIT),
    )(w_in, w_uq, w_out, w_gate_t, w_up_t, w_down)


def _gather_two_level(name, srcs):
    n_w = len(srcs)

    def body(*refs):
        src, dst = refs[:n_w], refs[n_w:2 * n_w]
        send_sems, recv_sems, loc_sems = refs[2 * n_w:]
        x, y, c = lax.axis_index("x"), lax.axis_index("y"), lax.axis_index("c")
        me, sibling = (x, y, c), (x, y, 1 - c)
        chips = [(1 - x, y), (x, 1 - y), (1 - x, 1 - y)]
        slot = lambda p: 4 * p[0] + 2 * p[1] + p[2]

        def copy(w, k, block, to, own=False):
            return pltpu.make_async_remote_copy(
                src_ref=src[w] if own else dst[w].at[slot(block)], dst_ref=dst[w].at[slot(block)],
                send_sem=send_sems.at[w, k], recv_sem=recv_sems.at[w, k], device_id=to, device_id_type=MESH)

        local = [pltpu.make_async_copy(src[w], dst[w].at[slot(me)], loc_sems.at[w]) for w in range(n_w)]
        first, passed = [], []
        for w in range(n_w):
            local[w].start()
            first.append(copy(w, 0, me, sibling, own=True))
            first += [copy(w, 1 + j, me, (*chip, c), own=True) for j, chip in enumerate(chips)]
        for cp in first:
            cp.start()
        for w in range(n_w):
            for j, chip in enumerate(chips):
                copy(w, 1 + j, (*chip, c), me).wait_recv()
                passed.append(copy(w, 4 + j, (*chip, c), sibling))
                passed[-1].start()
        for w in range(n_w):
            copy(w, 0, sibling, me).wait_recv()
            for j, chip in enumerate(chips):
                copy(w, 4 + j, (*chip, 1 - c), me).wait_recv()
        for cp in first + passed:
            cp.wait_send()
        for w in range(n_w):
            local[w].wait()

    return pl.pallas_call(
        body, name=name, in_specs=[_HBM] * n_w, out_specs=[_HBM] * n_w,
        out_shape=_exchange_shapes([GATHER] * n_w, srcs), scratch_shapes=_exchange_sems(n_w))(*srcs)


def _row_offsets(arrays):
    offs, rows = [], 0
    for a in arrays:
        offs.append(rows)
        rows += a.shape[0]
    return offs, -(-rows // 8) * 8


def _final_exchange(vecs):
    n_p = len(vecs)
    offs, rows = _row_offsets(vecs)

    def body(*refs):
        g_refs = refs[:n_p]
        rall, pk, send_sems, recv_sems, loc_sem = refs[n_p:]
        x, y, c = lax.axis_index("x"), lax.axis_index("y"), lax.axis_index("c")
        me = 4 * x + 2 * y + c
        pk[...] = jnp.zeros(pk.shape, F32)
        for p in range(n_p):
            r, n = g_refs[p].shape
            pk[offs[p]:offs[p] + r, 0:n] = g_refs[p][...]

        def remote(k):
            return pltpu.make_async_remote_copy(
                src_ref=pk, dst_ref=rall.at[me], send_sem=send_sems.at[k - 1], recv_sem=recv_sems.at[k - 1],
                device_id=_peer(k, x, y, c), device_id_type=MESH)

        def arrival(k):
            px, py, pc = _peer(k, x, y, c)
            return pltpu.make_async_remote_copy(
                src_ref=pk, dst_ref=rall.at[4 * px + 2 * py + pc], send_sem=send_sems.at[k - 1],
                recv_sem=recv_sems.at[k - 1], device_id=(px, py, pc), device_id_type=MESH)

        local = pltpu.make_async_copy(pk, rall.at[me], loc_sem)
        local.start()
        for k in range(1, N_DEV):
            remote(k).start()
        local.wait()
        for k in range(1, N_DEV):
            arrival(k).wait_recv()
        for k in range(1, N_DEV):
            remote(k).wait_send()

    vm = pl.BlockSpec(memory_space=pltpu.VMEM)
    return pl.pallas_call(
        body, name="final_exchange", in_specs=[vm] * n_p, out_specs=vm, out_shape=_sds((N_DEV, rows, D)),
        scratch_shapes=[pltpu.VMEM((rows, D), F32),
                        pltpu.SemaphoreType.DMA((N_DEV - 1,)), pltpu.SemaphoreType.DMA((N_DEV - 1,)),
                        pltpu.SemaphoreType.DMA],
    )(*vecs)


def _small_adam(rall, big_parts, big, ws, ms, vs):
    n_p = len(ws)
    packed = [w for p, w in enumerate(ws) if p != big] + [jax.ShapeDtypeStruct((1, HB), F32)]
    offs, _ = _row_offsets(packed)
    offs = offs[:big] + [None] + offs[big:]

    def total(ref, sl):
        g = ref[(0,) + sl]
        for j in range(1, N_DEV):
            g = g + ref[(j,) + sl]
        return g

    def body(*refs):
        rall_ref, big_ref = refs[:2]
        w_refs, m_refs, v_refs = refs[2:2 + n_p], refs[2 + n_p:2 + 2 * n_p], refs[2 + 2 * n_p:2 + 3 * n_p]
        outs = refs[2 + 3 * n_p:]
        for p in range(n_p):
            r, n = w_refs[p].shape
            if p == big:
                g = total(big_ref, (slice(0, r), slice(0, n)))
            else:
                g = total(rall_ref, (slice(offs[p], offs[p] + r), slice(0, n)))
            delta, m2, v2 = _adamw(w_refs[p][...], g, m_refs[p][...], v_refs[p][...])
            outs[p][...] = g
            outs[n_p + p][...] = delta
            outs[2 * n_p + p][...] = m2
            outs[3 * n_p + p][...] = v2
        outs[4 * n_p][...] = total(rall_ref, (slice(offs[n_p], offs[n_p] + 1), slice(0, HB)))

    vm = pl.BlockSpec(memory_space=pltpu.VMEM)
    res = pl.pallas_call(
        body, name="small_adam", in_specs=[vm] * (2 + 3 * n_p), out_specs=[vm] * (4 * n_p + 1),
        out_shape=[_sds(w.shape) for w in ws] * 4 + [_sds((1, HB))],
        compiler_params=pltpu.CompilerParams(vmem_limit_bytes=_VMEM_LIMIT),
    )(rall, big_parts, *ws, *ms, *vs)
    return res[:n_p], res[n_p:2 * n_p], res[2 * n_p:3 * n_p], res[3 * n_p:4 * n_p], res[4 * n_p]


def _device_sum(p_ref):
    g = p_ref[0].astype(F32)
    for j in range(1, N_DEV):
        g = g + p_ref[j].astype(F32)
    return g


def _shard_adam(name, parts, w, m, v, tr):
    a0, b0 = w.shape
    n_p = len(parts)
    b = parts[0].shape[2]
    first = [0]
    for p in parts:
        first.append(first[-1] + p.shape[1] // tr)

    def body(*refs):
        p_refs = refs[:n_p]
        w_ref, m_ref, v_ref, g_out, d_out, m_out, v_out = refs[n_p:]
        i = pl.program_id(0)
        g = _device_sum(p_refs[0])
        for k in range(1, n_p):
            g = jnp.where(i >= first[k], _device_sum(p_refs[k]), g)
        g = g[:, 0:b0]
        delta, m2, v2 = _adamw(w_ref[...], g, m_ref[...], v_ref[...])
        g_out[...] = g
        d_out[...] = delta
        m_out[...] = m2
        v_out[...] = v2

    def part_spec(k):
        last = first[k + 1] - first[k] - 1
        return pl.BlockSpec((N_DEV, tr, b), lambda i: (0, jnp.minimum(jnp.maximum(i - first[k], 0), last), 0))

    blk = pl.BlockSpec((tr, b0), lambda i: (i, 0))
    return _pcall(
        body, name, (a0 // tr,), [part_spec(k) for k in range(n_p)] + [blk, blk, blk],
        [blk] * 4, [_sds((a0, b0))] * 4)(*parts, w, m, v)


def _gate_up_adam(parts, ws, ms, vs):
    tc = 256

    def body(p_ref, wg, wu, mg, mu, vg, vu, *outs):
        g = _device_sum(p_ref)
        for k, (w_ref, m_ref, v_ref) in enumerate(((wg, mg, vg), (wu, mu, vu))):
            gk = g[k * FF_PAD:k * FF_PAD + FF_SH]
            delta, m2, v2 = _adamw(w_ref[...], gk, m_ref[...], v_ref[...])
            for o, val in zip(outs[4 * k:4 * k + 4], (gk, delta, m2, v2)):
                o[...] = val

    blk = pl.BlockSpec((FF_SH, tc), lambda i: (0, i))
    res = _pcall(
        body, "adam_w_gate_up", (D // tc,), [pl.BlockSpec((N_DEV, 2 * FF_PAD, tc), lambda i: (0, 0, i))] + [blk] * 6,
        [blk] * 8, [_sds((FF_SH, D))] * 8)(parts, *ws, *ms, *vs)
    return res[:4], res[4:]


def _fwd_in(x, g_pre, w_in_al, tm):
    T = x.shape[0]

    def body(x_ref, g_ref, w_ref, xm_ref, xh_ref, u_ref):
        nx, _ = _rms(x_ref[...], D)
        u = (nx * g_ref[...]).astype(BF)
        u_ref[...] = u
        xm_ref[...] = jnp.dot(u, w_ref[:, 0:MLA_IN], preferred_element_type=F32)
        xh_ref[...] = jnp.dot(u, w_ref[:, MLA_IN:XP_W], preferred_element_type=F32)

    return _pcall(body, "fwd_in", (T // tm,),
                  [_rows(tm, D), _full((1, D)), _full((D, XP_W))],
                  [_rows(tm, MLA_IN), _rows(tm, 4 * G_W), _rows(tm, D)],
                  [_sds((T, MLA_IN)), _sds((T, 4 * G_W)), _sds((T, D), BF)])(x, g_pre, w_in_al)


def _rope(blk, ta, tb1, tb2):
    return blk * ta + pltpu.roll(blk, HB - ROPE // 2, 1) * tb1 + pltpu.roll(blk, ROPE // 2, 1) * tb2


def _unrope(d, ta, tb1, tb2):
    return d * ta + pltpu.roll(d * tb1, ROPE // 2, 1) + pltpu.roll(d * tb2, HB - ROPE // 2, 1)


def _mla_prep(xp, tabs, g_q, g_kv, w_uq, w_uk, w_uv, tm):
    T = xp.shape[0]
    W = N_HEADS * HB

    def body(xp_ref, ta_ref, tb1_ref, tb2_ref, gq_ref, gkv_ref, wuq_ref, wuk_ref, wuv_ref, q_ref, qs_ref, k_ref, v_ref):
        ta, tb1, tb2 = ta_ref[...], tb1_ref[...], tb2_ref[...]
        nq, _ = _rms(xp_ref[:, 0:Q_RANK], Q_RANK)
        nkv, _ = _rms(xp_ref[:, Q_RANK:Q_RANK + KV_RANK], KV_RANK)
        nkv = (nkv * gkv_ref[...]).astype(BF)
        qpre = _dot(nq * gq_ref[...], wuq_ref[...])
        kpre = jnp.dot(nkv, wuk_ref[...], preferred_element_type=F32)
        v = jnp.dot(nkv, wuv_ref[...], preferred_element_type=F32)
        lane = lax.broadcasted_iota(jnp.int32, (tm, W), 1)
        v_ref[...] = jnp.where((lane & (HB - 1)) == V_DIM, 1.0, v).astype(BF)
        kr = _rope(pltpu.roll(xp_ref[:, Q_RANK + KV_RANK:MLA_IN], NOPE, 1), ta, tb1, tb2)
        for h in range(N_HEADS):
            sl = slice(h * HB, (h + 1) * HB)
            qr = _rope(qpre[:, sl], ta, tb1, tb2)
            q_ref[:, sl] = qr.astype(BF)
            qs_ref[:, sl] = (qr * (QK_DIM ** -0.5 * LOG2E)).astype(BF)
            k_ref[:, sl] = (kpre[:, sl] + kr).astype(BF)

    tab = _rows(tm, HB)
    return _pcall(body, "mla_prep", (T // tm,),
                  [_rows(tm, MLA_IN), tab, tab, tab, _full((1, Q_RANK)), _full((1, KV_RANK)),
                   _full((Q_RANK, W)), _full((KV_RANK, W)), _full((KV_RANK, W))],
                  [_rows(tm, W)] * 4, [_sds((T, W), BF)] * 4)(xp, *tabs, g_q, g_kv, w_uq, w_uk, w_uv)


def _flash_fwd(q, k, v, tq, exchange=None):
    T = q.shape[0]
    hp = _AH
    W = hp * HB

    def body(q_ref, k_ref, v_ref, o_ref, lse_ref):
        i = pl.program_id(1)

        def blk(j, carry, masked):
            st = pl.multiple_of(j * tq, tq)
            out = []
            for h in range(hp):
                ls = slice(h * HB, (h + 1) * HB)
                m, acc = carry[h]
                s = _dot_nt(q_ref[:, ls], k_ref[pl.ds(st, tq), ls])
                if masked:
                    r = lax.broadcasted_iota(jnp.int32, (tq, tq), 0)
                    c = lax.broadcasted_iota(jnp.int32, (tq, tq), 1)
                    s = jnp.where(c <= r, s, NEG)
                m2 = jnp.maximum(m, jnp.max(s, -1, keepdims=True))
                p = jnp.exp2(s - m2)
                out.append((m2, jnp.exp2(m - m2) * acc + _dot(p, v_ref[pl.ds(st, tq), ls])))
            return tuple(out)

        init = tuple((jnp.full((tq, 1), NEG, F32), jnp.zeros((tq, HB), F32)) for _ in range(hp))
        carry = lax.fori_loop(0, i, lambda j, cr: blk(j, cr, False), init)
        res = blk(i, carry, True)
        lane = lax.broadcasted_iota(jnp.int32, (tq, HB), 1)
        for h in range(hp):
            ls = slice(h * HB, (h + 1) * HB)
            m, acc = res[h]
            l = acc[:, V_DIM:V_DIM + 1]
            o_ref[:, ls] = jnp.where(lane < V_DIM, acc / l, 0.0)
            lse_ref[:, ls] = jnp.broadcast_to(m * (1.0 / LOG2E) + jnp.log(l), (tq, HB))

    qs = pl.BlockSpec((tq, W), lambda h, i: (i, h))
    kvs = pl.BlockSpec((T, W), lambda h, i: (0, h))
    return _pcall(body, "flash_fwd", (N_HEADS // hp, T // tq), [qs, kvs, kvs], [qs, qs],
                  [_sds((T, N_HEADS * HB))] * 2, exchange=exchange)(q, k, v)


def _gates(hq, hf, lb):
    sig = _sigmoid(hf)
    f = lb + (1.0 - lb) * sig
    sq = _sigmoid(hq)
    return hq * sq, 1.0 - f, f, jnp.log(f), sig, sq


def _lower_bound(lbl_ref):
    l0, l1 = lbl_ref[0:1, :], lbl_ref[1:2, :]
    mx = jnp.maximum(l0, l1)
    e0, e1 = jnp.exp(l0 - mx), jnp.exp(l1 - mx)
    return e0 / (e0 + e1)


def _split3(x):
    hi = x.astype(BF)
    r1 = x - hi.astype(F32)
    mid = r1.astype(BF)
    lo = (r1 - mid.astype(F32)).astype(BF)
    return hi, mid, lo


def _tri_mm(tri, x):
    hi, mid, lo = _split3(x)
    mm = lambda t: jnp.dot(tri, t, preferred_element_type=F32)
    return mm(hi) + mm(mid) + mm(lo)


def _intra_codes(sub):
    row = lax.broadcasted_iota(jnp.int32, (CHUNK, CHUNK), 0)
    col = lax.broadcasted_iota(jnp.int32, (CHUNK, CHUNK), 1)
    return sub, row, col


def _intra(q, k, b2, b_s, codes, da=None):
    grad = da is not None
    pow2 = (lambda x: jnp.exp2(jnp.minimum(x, 0.0))) if grad else jnp.exp2
    sub, row, col = codes
    a = jnp.zeros((CHUNK, CHUNK), F32)
    dq = jnp.zeros((CHUNK, G_DIM), F32)
    dk = jnp.zeros((CHUNK, G_DIM), F32)
    for i in range(1, CHUNK // sub):
        b0 = b_s[sub * i - 1:sub * i, :]
        eq, ek = pow2(b2 - b0), pow2(b0 - b2)
        mask = ((row // sub) == i) & (col < sub * i)
        a = jnp.where(mask, _dot_nt(q * eq, k * ek), a)
        if grad:
            dai = jnp.where(mask, da, 0.0)
            dq = dq + _dot(dai, k * ek) * eq
            dk = dk + _dot_tn(dai, q * eq) * ek
    for d in range(sub):
        ksh = pltpu.roll(k, d, 0) if d else k
        bsh = pltpu.roll(b2, d, 0) if d else b2
        e = pow2(b2 - bsh)
        mask = (col == row - d) & ((row & (sub - 1)) >= d)
        a = jnp.where(mask, jnp.sum(q * ksh * e, -1, keepdims=True), a)
        if grad:
            g = jnp.sum(jnp.where(mask, da, 0.0), -1, keepdims=True) * e
            dq = dq + g * ksh
            cb = g * q
            dk = dk + (pltpu.roll(cb, CHUNK - d, 0) if d else cb)
    return (a, dq, dk) if grad else a


def _hgrn_fwd(xp, lb_logits, g_hn, exchange=None):
    T = xp.shape[0]
    tb = min(_TB, T)
    ncb = tb // CHUNK
    hp = _HP
    W = hp * G_DIM

    def body(hq_ref, hf_ref, hi_ref, hg_ref, lbl_ref, ghn_ref, out_ref, oraw_ref, sall_ref, st_ref, b_s):
        lb_all = _lower_bound(lbl_ref)

        @pl.when(pl.program_id(1) == 0)
        def _():
            st_ref[...] = jnp.zeros(st_ref.shape, F32)

        row = lax.broadcasted_iota(jnp.int32, (CHUNK, CHUNK), 0)
        col = lax.broadcasted_iota(jnp.int32, (CHUNK, CHUNK), 1)
        tri = (col <= row).astype(BF)
        codes = _intra_codes(SUB_FWD)

        def chunk(c, carry):
            sl = pl.ds(pl.multiple_of(c * CHUNK, CHUNK), CHUNK)
            for h in range(hp):
                ls = slice(h * G_DIM, (h + 1) * G_DIM)
                q, k, _, lf, _, _ = _gates(hq_ref[sl, ls], hf_ref[sl, ls], lb_all[:, ls])
                v = hi_ref[sl, ls]
                b2 = _tri_mm(tri, lf) * LOG2E
                b_s[h] = b2
                st = st_ref[h]
                sall_ref[c, h] = st
                o = _dot_nt(q * jnp.exp2(b2), st) + _dot(_intra(q, k, b2, b_s.at[h], codes), v)
                bl = b_s[h, CHUNK - 1:CHUNK, :]
                st_ref[h] = st * jnp.exp2(bl) + _dot_tn(v, k * jnp.exp2(bl - b2))
                oraw_ref[sl, ls] = o
                n, _ = _rms(o, G_DIM)
                hg = hg_ref[sl, ls]
                out_ref[sl, ls] = n * ghn_ref[:, ls] * (hg * _sigmoid(hg))
            return carry

        lax.fori_loop(0, ncb, chunk, 0)

    col_blk = lambda j: pl.BlockSpec((tb, W), lambda p, t: (t, j * (G_HEADS // hp) + p))
    head = pl.BlockSpec((tb, W), lambda p, t: (t, p))
    return _pcall(
        body, "hgrn_fwd", (G_HEADS // hp, T // tb),
        [col_blk(0), col_blk(1), col_blk(2), col_blk(3),
         pl.BlockSpec((2, W), lambda p, t: (0, p)), pl.BlockSpec((1, W), lambda p, t: (0, p))],
        [head, head, pl.BlockSpec((ncb, hp, G_DIM, G_DIM), lambda p, t: (t, p, 0, 0))],
        [_sds((T, G_W)), _sds((T, G_W)), _sds((T // CHUNK, G_HEADS, G_DIM, G_DIM))],
        scratch=[pltpu.VMEM((hp, G_DIM, G_DIM), F32), pltpu.VMEM((hp, CHUNK, G_DIM), F32)], exchange=exchange,
    )(xp, xp, xp, xp, lb_logits, g_hn)


def _fwd_out(o_pad, o_hgrn, x, g_on, w_out, g_post, g_fpre, tm):
    T = x.shape[0]

    def body(o_ref, oh_ref, x_ref, gon_ref, w_ref, gpost_ref, gfpre_ref, h1_ref, y1_ref, z_ref, mix_ref):
        for h in range(N_HEADS):
            sl = slice(h * HB, (h + 1) * HB)
            n, _ = _rms(o_ref[:, sl], V_DIM)
            mix_ref[:, sl] = (n * gon_ref[:, sl]).astype(BF)
        mix_ref[:, N_HEADS * HB:MIX_W] = oh_ref[...].astype(BF)
        y1 = jnp.dot(mix_ref[...], w_ref[...], preferred_element_type=F32)
        y1_ref[...] = y1
        ny, _ = _rms(y1, D)
        h1 = x_ref[...] + ny * gpost_ref[...]
        h1_ref[...] = h1
        nh, _ = _rms(h1, D)
        z_ref[...] = (nh * gfpre_ref[...]).astype(BF)

    return _pcall(body, "fwd_out", (T // tm,),
                  [_rows(tm, N_HEADS * HB), _rows(tm, G_W), _rows(tm, D), _full((1, N_HEADS * HB)),
                   _full((MIX_W, D)), _full((1, D)), _full((1, D))],
                  [_rows(tm, D), _rows(tm, D), _rows(tm, D), _rows(tm, MIX_W)],
                  [_sds((T, D)), _sds((T, D)), _sds((T, D), BF), _sds((T, MIX_W), BF)],
                  )(o_pad, o_hgrn, x, g_on, w_out, g_post, g_fpre)


def _ffn_fwd(z, wgu, wd, h1, tgt, g_fpost, tm, nd):
    T = z.shape[0]
    fb = nd * FF_PAD
    nf = wd.shape[0] // fb

    def body(z_ref, wgu_ref, wd_ref, h1_ref, t_ref, gp_ref,
             as_ref, bs_ref, ff_ref, dh2_ref, dy2_ref, dgp_ref, loss_ref, acc):
        i, j = pl.program_id(0), pl.program_id(1)
        gu = _dot_nt(z_ref[...], wgu_ref[...])
        piece = lambda n: gu[:, n * FF_PAD:(n + 1) * FF_PAD]
        g = piece(0) if nd == 1 else jnp.concatenate([piece(2 * n) for n in range(nd)], 1)
        u = piece(1) if nd == 1 else jnp.concatenate([piece(2 * n + 1) for n in range(nd)], 1)
        s = _sigmoid(g)
        b = g * s
        ff = (b * u).astype(BF)
        as_ref[...] = (u * _dsilu(g, s)).astype(BF)
        bs_ref[...] = b.astype(BF)
        ff_ref[...] = ff
        part = jnp.dot(ff, wd_ref[...], preferred_element_type=F32)

        @pl.when(j == 0)
        def _():
            acc[...] = part

        @pl.when(j > 0)
        def _():
            acc[...] += part

        @pl.when((i == 0) & (j == 0))
        def _():
            dgp_ref[...] = jnp.zeros(dgp_ref.shape, F32)
            loss_ref[...] = jnp.zeros(loss_ref.shape, F32)

        @pl.when(j == nf - 1)
        def _():
            ny, r = _rms(acc[...], D)
            err = h1_ref[...] + ny * gp_ref[...] - t_ref[...]
            loss_ref[...] += 0.5 * jnp.sum(jnp.sum(err * err, -1, keepdims=True) * (1.0 / D), 0, keepdims=True)
            dh2 = err * (1.0 / D)
            dh2_ref[...] = dh2
            dy2, dgp = _rms_bwd(ny, r, gp_ref[...], dh2, D)
            dy2_ref[...] = dy2.astype(BF)
            dgp_ref[...] += dgp

    tok = lambda n: pl.BlockSpec((tm, n), lambda i, j: (i, 0))
    col = pl.BlockSpec((tm, fb), lambda i, j: (i, j))
    return _pcall(
        body, "ffn_fwd", (T // tm, nf),
        [tok(D), pl.BlockSpec((2 * fb, D), lambda i, j: (j, 0)), pl.BlockSpec((fb, D), lambda i, j: (j, 0)),
         tok(D), tok(D), _full((1, D))],
        [col, col, col, tok(D), tok(D), _full((1, D)), _full((1, HB))],
        [_sds((T, nf * fb), BF)] * 3 + [_sds((T, D)), _sds((T, D), BF), _sds((1, D)), _sds((1, HB))],
        scratch=[pltpu.VMEM((tm, D), F32)],
    )(z, wgu, wd, h1, tgt, g_fpost)


def _dsilu(x, s):
    return s * (1.0 + x * (1.0 - s))


def _ffn_bwd_x(dy2, gs, us, wgu, wd, h1, y1, dh2, g_fpre, g_post, tm):
    T = dy2.shape[0]
    nf = wd.shape[0] // _FB

    def body(dy2_ref, gs_ref, us_ref, wgu_ref, wd_ref, h1_ref, y1_ref, dh2_ref, gf_ref, gp_ref,
             dgu_ref, dh1_ref, dy1_ref, dgf_ref, dgp_ref, acc):
        i, j = pl.program_id(0), pl.program_id(1)
        dff = _dot_nt(dy2_ref[...], wd_ref[...])
        dg = (dff * gs_ref[...].astype(F32)).astype(BF)
        du = (dff * us_ref[...].astype(F32)).astype(BF)
        dgu = jnp.concatenate([dg[:, 0:FF_PAD], du[:, 0:FF_PAD], dg[:, FF_PAD:_FB], du[:, FF_PAD:_FB]], 1)
        dgu_ref[...] = dgu
        part = jnp.dot(dgu, wgu_ref[...], preferred_element_type=F32)

        @pl.when(j == 0)
        def _():
            acc[...] = part

        @pl.when(j > 0)
        def _():
            acc[...] += part

        @pl.when((i == 0) & (j == 0))
        def _():
            dgf_ref[...] = jnp.zeros(dgf_ref.shape, F32)
            dgp_ref[...] = jnp.zeros(dgp_ref.shape, F32)

        @pl.when(j == nf - 1)
        def _():
            nh, rh = _rms(h1_ref[...], D)
            dh, dgf = _rms_bwd(nh, rh, gf_ref[...], acc[...], D)
            dh1 = dh2_ref[...] + dh
            dh1_ref[...] = dh1
            dgf_ref[...] += dgf
            ny, ry = _rms(y1_ref[...], D)
            dy1, dgp = _rms_bwd(ny, ry, gp_ref[...], dh1, D)
            dy1_ref[...] = dy1.astype(BF)
            dgp_ref[...] += dgp

    tok = lambda n: pl.BlockSpec((tm, n), lambda i, j: (i, 0))
    col = pl.BlockSpec((tm, _FB), lambda i, j: (i, j))
    return _pcall(
        body, "ffn_bwd_x", (T // tm, nf),
        [tok(D), col, col, pl.BlockSpec((2 * _FB, D), lambda i, j: (j, 0)), pl.BlockSpec((_FB, D), lambda i, j: (j, 0)),
         tok(D), tok(D), tok(D), _full((1, D)), _full((1, D))],
        [pl.BlockSpec((tm, 2 * _FB), lambda i, j: (i, j)), tok(D), tok(D), _full((1, D)), _full((1, D))],
        [_sds((T, 2 * nf * _FB), BF), _sds((T, D)), _sds((T, D), BF), _sds((1, D)), _sds((1, D))],
        scratch=[pltpu.VMEM((tm, D), F32)],
    )(dy2, gs, us, wgu, wd, h1, y1, dh2, g_fpre, g_post)


def _ffn_bwd_w(z, ffs, dgu, dy2, tm):
    T = z.shape[0]
    nf = ffs.shape[1] // _FB
    nt = T // tm

    def body(z_ref, ff_ref, dgu_ref, dy2_ref, dwgu_ref, dwd_ref, agu, ad):
        i = pl.program_id(1)
        pgu = _dot_tn(dgu_ref[...], z_ref[...])
        pd = _dot_tn(ff_ref[...], dy2_ref[...])

        @pl.when(i == 0)
        def _():
            agu[...] = pgu
            ad[...] = pd

        @pl.when(i > 0)
        def _():
            agu[...] += pgu
            ad[...] += pd

        @pl.when(i == nt - 1)
        def _():
            dwgu_ref[...] = agu[...].astype(BF)
            dwd_ref[...] = ad[...].astype(BF)

    F = nf * _FB
    tok = lambda n: pl.BlockSpec((tm, n), lambda j, i: (i, 0))
    return _pcall(
        body, "ffn_bwd_w", (nf, nt),
        [tok(D), pl.BlockSpec((tm, _FB), lambda j, i: (i, j)), pl.BlockSpec((tm, 2 * _FB), lambda j, i: (i, j)), tok(D)],
        [pl.BlockSpec((2 * _FB, D), lambda j, i: (j, 0)), pl.BlockSpec((_FB, D), lambda j, i: (j, 0))],
        [_sds((2 * F, D), BF), _sds((F, D), BF)],
        scratch=[pltpu.VMEM((2 * _FB, D), F32), pltpu.VMEM((_FB, D), F32)],
    )(z, ffs, dgu, dy2)


def _out_bwd(dy1, mix, o_pad, w_out, g_on, tm):
    T = dy1.shape[0]
    W = N_HEADS * HB

    def body(dy1_ref, mix_ref, o_ref, w_ref, gon_ref, do_ref, dl_ref, dohg_ref, dw_ref, dgon_ref):
        i = pl.program_id(0)
        dy1v = dy1_ref[...]
        dmix = _dot_nt(dy1v, w_ref[...])
        pw = _dot_tn(mix_ref[...], dy1v)

        @pl.when(i == 0)
        def _():
            dw_ref[...] = pw
            dgon_ref[...] = jnp.zeros(dgon_ref.shape, F32)

        @pl.when(i > 0)
        def _():
            dw_ref[...] += pw

        for h in range(N_HEADS):
            sl = slice(h * HB, (h + 1) * HB)
            ov = o_ref[:, sl]
            n, r = _rms(ov, V_DIM)
            do, dg = _rms_bwd(n, r, gon_ref[:, sl], dmix[:, sl], V_DIM)
            dgon_ref[:, sl] += dg
            do_ref[:, sl] = do.astype(BF)
            dl_ref[:, sl] = jnp.broadcast_to(jnp.sum(do * ov, -1, keepdims=True), (tm, HB))
        dohg_ref[...] = dmix[:, W:MIX_W]

    return _pcall(body, "out_bwd", (T // tm,),
                  [_rows(tm, D), _rows(tm, MIX_W), _rows(tm, W), _full((MIX_W, D)), _full((1, W))],
                  [_rows(tm, W), _rows(tm, W), _rows(tm, G_W), _full((MIX_W, D)), _full((1, W))],
                  [_sds((T, W), BF), _sds((T, W)), _sds((T, G_W)), _sds((MIX_W, D)), _sds((1, W))],
                  )(dy1, mix, o_pad, w_out, g_on)


def _flash_bwd(q, k, v, do, lse, dl, tq, exchange=None):
    T = q.shape[0]
    nq = T // tq
    scale = QK_DIM ** -0.5
    hp = _AH
    W = hp * HB

    def body(k_ref, v_ref, q_ref, do_ref, lse_ref, dl_ref, dk_ref, dv_ref, dq_ref):
        j = pl.program_id(1)

        @pl.when(j == 0)
        def _():
            dq_ref[...] = jnp.zeros(dq_ref.shape, F32)

        def blk(i, carry, masked):
            sl = pl.ds(pl.multiple_of(i * tq, tq), tq)
            out = []
            for h in range(hp):
                ls = slice(h * HB, (h + 1) * HB)
                dk, dv = carry[h]
                kv, vv = k_ref[:, ls], v_ref[:, ls]
                qv, dov = q_ref[sl, ls], do_ref[sl, ls]
                s = _dot_nt(qv, kv) * scale
                if masked:
                    r = lax.broadcasted_iota(jnp.int32, (tq, tq), 0)
                    c = lax.broadcasted_iota(jnp.int32, (tq, tq), 1)
                    s = jnp.where(c <= r, s, NEG)
                p = jnp.exp(s - lse_ref[sl, h * HB:h * HB + 1])
                ds = p * (_dot_nt(dov, vv) - dl_ref[sl, h * HB:h * HB + 1]) * scale
                dq_ref[sl, ls] += _dot(ds, kv)
                out.append((dk + _dot_tn(ds, qv), dv + _dot_tn(p, dov)))
            return tuple(out)

        zero = jnp.zeros((tq, HB), F32)
        carry = blk(j, tuple((zero, zero) for _ in range(hp)), True)
        res = lax.fori_loop(j + 1, nq, lambda i, cr: blk(i, cr, False), carry)
        for h in range(hp):
            ls = slice(h * HB, (h + 1) * HB)
            dk_ref[:, ls] = res[h][0]
            dv_ref[:, ls] = res[h][1]

    tile = pl.BlockSpec((tq, W), lambda h, j: (j, h))
    whole = pl.BlockSpec((T, W), lambda h, j: (0, h))
    return _pcall(body, "flash_bwd", (N_HEADS // hp, nq), [tile, tile, whole, whole, whole, whole],
                  [tile, tile, whole], [_sds((T, N_HEADS * HB))] * 3, exchange=exchange)(k, v, q, do, lse, dl)


def _mla_prep_bwd(xp, tabs, dq, dk, dv, g_q, g_kv, w_uq, w_uk, w_uv, tm):
    T = xp.shape[0]
    W = N_HEADS * HB

    def body(xp_ref, ta_ref, tb1_ref, tb2_ref, dq_ref, dk_ref, dv_ref, gq_ref, gkv_ref, wuq_ref, wuk_ref, wuv_ref,
             dxp_ref, dwuq_ref, dwuk_ref, dwuv_ref, dgq_ref, dgkv_ref, dqp):
        i = pl.program_id(0)
        ta, tb1, tb2 = ta_ref[...], tb1_ref[...], tb2_ref[...]
        nq, rq = _rms(xp_ref[:, 0:Q_RANK], Q_RANK)
        nkv, rkv = _rms(xp_ref[:, Q_RANK:Q_RANK + KV_RANK], KV_RANK)
        dkr = jnp.zeros((tm, HB), F32)
        for h in range(N_HEADS):
            sl = slice(h * HB, (h + 1) * HB)
            dqp[:, sl] = _unrope(dq_ref[:, sl], ta, tb1, tb2).astype(BF)
            dkr = dkr + dk_ref[:, sl]
        dkr = pltpu.roll(_unrope(dkr, ta, tb1, tb2), HB - NOPE, 1)
        lane = lax.broadcasted_iota(jnp.int32, (tm, HB), 1)
        dxp_ref[:, Q_RANK + KV_RANK:MLA_IN] = jnp.where(lane < ROPE, dkr, 0.0)
        dqpv = dqp[...]
        dkv, dvv = dk_ref[...].astype(BF), dv_ref[...].astype(BF)
        nqs = (nq * gq_ref[...]).astype(BF)
        nkvs = (nkv * gkv_ref[...]).astype(BF)
        pq, pk, pv = _dot_tn(nqs, dqpv), _dot_tn(nkvs, dkv), _dot_tn(nkvs, dvv)
        dcq, dgq = _rms_bwd(nq, rq, gq_ref[...], _dot_nt(dqpv, wuq_ref[...]), Q_RANK)
        dckv, dgkv = _rms_bwd(nkv, rkv, gkv_ref[...], _dot_nt(dkv, wuk_ref[...]) + _dot_nt(dvv, wuv_ref[...]), KV_RANK)
        dxp_ref[:, 0:Q_RANK] = dcq
        dxp_ref[:, Q_RANK:Q_RANK + KV_RANK] = dckv

        @pl.when(i == 0)
        def _():
            dwuq_ref[...] = pq
            dwuk_ref[...] = pk
            dwuv_ref[...] = pv
            dgq_ref[...] = dgq
            dgkv_ref[...] = dgkv

        @pl.when(i > 0)
        def _():
            dwuq_ref[...] += pq
            dwuk_ref[...] += pk
            dwuv_ref[...] += pv
            dgq_ref[...] += dgq
            dgkv_ref[...] += dgkv

    tab = _rows(tm, HB)
    return _pcall(
        body, "mla_prep_bwd", (T // tm,),
        [_rows(tm, MLA_IN), tab, tab, tab, _rows(tm, W), _rows(tm, W), _rows(tm, W), _full((1, Q_RANK)),
         _full((1, KV_RANK)), _full((Q_RANK, W)), _full((KV_RANK, W)), _full((KV_RANK, W))],
        [_rows(tm, MLA_IN), _full((Q_RANK, W)), _full((KV_RANK, W)), _full((KV_RANK, W)), _full((1, Q_RANK)),
         _full((1, KV_RANK))],
        [_sds((T, MLA_IN)), _sds((Q_RANK, W)), _sds((KV_RANK, W)), _sds((KV_RANK, W)), _sds((1, Q_RANK)),
         _sds((1, KV_RANK))],
        scratch=[pltpu.VMEM((tm, W), BF)],
    )(xp, *tabs, dq, dk, dv, g_q, g_kv, w_uq, w_uk, w_uv)


def _hgrn_bwd(xp, o_raw, s_all, d_out, lb_logits, g_hn, exchange=None):
    T = xp.shape[0]
    tb = min(_TB, T)
    ncb = tb // CHUNK
    nb = T // tb
    hp = _HP
    W = hp * G_DIM

    def body(hq_ref, hf_ref, hi_ref, hg_ref, o_ref, sall_ref, dout_ref, lbl_ref, ghn_ref,
             dhq_ref, dhf_ref, dhi_ref, dhg_ref, dlbl_ref, dghn_ref, dst_ref, b_s, acc_lb, acc_g):
        t = pl.program_id(1)
        lb_all = _lower_bound(lbl_ref)

        @pl.when(t == 0)
        def _():
            dst_ref[...] = jnp.zeros(dst_ref.shape, F32)
            acc_lb[...] = jnp.zeros(acc_lb.shape, F32)
            acc_g[...] = jnp.zeros(acc_g.shape, F32)

        row = lax.broadcasted_iota(jnp.int32, (CHUNK, CHUNK), 0)
        col = lax.broadcasted_iota(jnp.int32, (CHUNK, CHUNK), 1)
        tri = (col <= row).astype(BF)
        tri_t = (col >= row).astype(BF)
        codes = _intra_codes(SUB_BWD)
        last = lax.broadcasted_iota(jnp.int32, (CHUNK, G_DIM), 0) == CHUNK - 1

        def chunk(cc, carry):
            c = ncb - 1 - cc
            sl = pl.ds(pl.multiple_of(c * CHUNK, CHUNK), CHUNK)
            for h in range(hp):
                ls = slice(h * G_DIM, (h + 1) * G_DIM)
                lb, ghn = lb_all[:, ls], ghn_ref[:, ls]
                hq, hg = hq_ref[sl, ls], hg_ref[sl, ls]
                q, k, f, lf, sig, sq = _gates(hq, hf_ref[sl, ls], lb)
                v = hi_ref[sl, ls]
                b2 = _tri_mm(tri, lf) * LOG2E
                b_s[h] = b2
                st = sall_ref[c, h]
                dstn = dst_ref[h]
                o = o_ref[sl, ls]
                dout = dout_ref[sl, ls]
                n, r = _rms(o, G_DIM)
                sg = _sigmoid(hg)
                dhg_ref[sl, ls] = dout * (n * ghn) * _dsilu(hg, sg)
                do, dg = _rms_bwd(n, r, ghn, dout * (hg * sg), G_DIM)
                acc_g[:, ls] += dg
                eb = jnp.exp2(b2)
                bl = b_s[h, CHUNK - 1:CHUNK, :]
                ebl = jnp.exp2(bl)
                ekd = jnp.exp2(bl - b2)
                kd = k * ekd
                a, dq_i, dk_i = _intra(q, k, b2, b_s.at[h], codes, _dot_nt(do, v))
                dhi_ref[sl, ls] = _dot_tn(a, do) + _dot_nt(kd, dstn)
                dk_state = _dot(v, dstn) * ekd
                dq = dq_i + _dot(do, st) * eb
                dk = dk_i + dk_state
                dbl = jnp.sum(k * dk_state, 0, keepdims=True) + ebl * jnp.sum(dstn * st, 0, keepdims=True)
                db = q * dq - k * dk + jnp.where(last, dbl, 0.0)
                df = _tri_mm(tri_t, db) / f - dk
                dhf_ref[sl, ls] = df * (1.0 - lb) * sig * (1.0 - sig)
                acc_lb[:, ls] += jnp.sum(df * (1.0 - sig), 0, keepdims=True)
                dhq_ref[sl, ls] = dq * _dsilu(hq, sq)
                dst_ref[h] = dstn * ebl + _dot_tn(do, q * eb)
            return carry

        lax.fori_loop(0, ncb, chunk, 0)

        @pl.when(t == nb - 1)
        def _():
            dl0 = acc_lb[...] * lb_all * (1.0 - lb_all)
            dlbl_ref[0:1, :] = dl0
            dlbl_ref[1:2, :] = -dl0
            dghn_ref[...] = acc_g[...]

    col_blk = lambda j: pl.BlockSpec((tb, W), lambda p, t: (nb - 1 - t, j * (G_HEADS // hp) + p))
    head = pl.BlockSpec((tb, W), lambda p, t: (nb - 1 - t, p))
    two = pl.BlockSpec((2, W), lambda p, t: (0, p))
    one = pl.BlockSpec((1, W), lambda p, t: (0, p))
    res = _pcall(
        body, "hgrn_bwd", (G_HEADS // hp, nb),
        [col_blk(0), col_blk(1), col_blk(2), col_blk(3), head,
         pl.BlockSpec((ncb, hp, G_DIM, G_DIM), lambda p, t: (nb - 1 - t, p, 0, 0)), head, two, one],
        [head, head, head, head, two, one],
        [_sds((T, G_W))] * 4 + [_sds((2, G_W)), _sds((1, G_W))],
        scratch=[pltpu.VMEM((hp, G_DIM, G_DIM), F32), pltpu.VMEM((hp, CHUNK, G_DIM), F32),
                 pltpu.VMEM((1, W), F32), pltpu.VMEM((1, W), F32)], exchange=exchange,
    )(xp, xp, xp, xp, o_raw, s_all, d_out, lb_logits, g_hn)
    return res


def _in_bwd_x(x, dxp_m, dxp_h, dh1, w_in_al, g_pre, tm, exchange=None):
    T = x.shape[0]

    def body(x_ref, dm_ref, d0_ref, d1_ref, d2_ref, d3_ref, dh1_ref, w_ref, g_ref, dx_ref, dg_ref):
        i = pl.program_id(0)
        du = _dot_nt(dm_ref[...], w_ref[:, 0:MLA_IN])
        for j, d_ref in enumerate((d0_ref, d1_ref, d2_ref, d3_ref)):
            du = du + _dot_nt(d_ref[...], w_ref[:, MLA_IN + j * G_W:MLA_IN + (j + 1) * G_W])
        nx, r = _rms(x_ref[...], D)
        dx, dg = _rms_bwd(nx, r, g_ref[...], du, D)
        dx_ref[...] = dh1_ref[...] + dx

        @pl.when(i == 0)
        def _():
            dg_ref[...] = dg

        @pl.when(i > 0)
        def _():
            dg_ref[...] += dg

    return _pcall(body, "in_bwd_x", (T // tm,),
                  [_rows(tm, D), _rows(tm, MLA_IN)] + [_rows(tm, G_W)] * 4 + [_rows(tm, D), _full((D, XP_W)), _full((1, D))],
                  [_rows(tm, D), _full((1, D))], [_sds((T, D)), _sds((1, D))], exchange=exchange,
                  )(x, dxp_m, *dxp_h, dh1, w_in_al, g_pre)


def _aligned_col(c):
    return jnp.where(c < Q_RANK + KV_RANK + ROPE, c, c + (KR_PAD - ROPE))


def _align_w_in(g_in):
    tile = 384
    kr_end = Q_RANK + KV_RANK + ROPE

    def body(g_ref, o_ref, gp):
        gp[...] = jnp.zeros(gp.shape, BF)
        for j in range(N_DEV):
            gp[j, :, 0:IN_SH] = g_ref[j]
        r = lax.broadcasted_iota(jnp.int32, (tile, tile), 0)
        c = lax.broadcasted_iota(jnp.int32, (tile, tile), 1)
        for t in range(XP_W // tile):
            lo, hi = t * tile, (t + 1) * tile
            cols = [a if a < kr_end else a - (KR_PAD - ROPE) for a in (lo, hi - 1)]
            acc = jnp.zeros((D, tile), F32)
            for j in range(cols[0] // IN_SH, cols[-1] // IN_SH + 1):
                sel = (r < IN_SH) & (_aligned_col(j * IN_SH + r) == lo + c)
                acc = acc + jnp.dot(gp[j], sel.astype(BF), preferred_element_type=F32)
            o_ref[:, lo:hi] = acc.astype(BF)

    vm = pl.BlockSpec(memory_space=pltpu.VMEM)
    return pl.pallas_call(
        body, name="align_w_in", in_specs=[vm], out_specs=vm, out_shape=_sds((D, XP_W), BF),
        scratch_shapes=[pltpu.VMEM((N_DEV, D, tile), BF)],
        compiler_params=pltpu.CompilerParams(vmem_limit_bytes=_VMEM_LIMIT))(g_in)


def _in_bwd_w(name, u, dxp_m, dxp_h, tm, half, exchange=None):
    T = u.shape[0]
    nt = T // tm
    nr = D // 2
    win = 640

    def body(u_ref, dm_ref, d0_ref, d1_ref, d2_ref, d3_ref, o_ref, acc):
        i = pl.program_id(0)
        ut = u_ref[...].T
        parts = [(0, MLA_IN, dm_ref)] + [(MLA_IN + j * G_W, G_W, d) for j, d in enumerate((d0_ref, d1_ref, d2_ref, d3_ref))]

        @pl.when(i == 0)
        def _():
            for lo, n, d in parts:
                acc[:, lo:lo + n] = jnp.dot(ut, d[...].astype(BF), preferred_element_type=F32)

        @pl.when(i > 0)
        def _():
            for lo, n, d in parts:
                acc[:, lo:lo + n] += jnp.dot(ut, d[...].astype(BF), preferred_element_type=F32)

        @pl.when(i == nt - 1)
        def _():
            wide = 384
            r = lax.broadcasted_iota(jnp.int32, (win, wide), 0)
            c = lax.broadcasted_iota(jnp.int32, (win, wide), 1)
            kr_end = Q_RANK + KV_RANK + ROPE
            for j in range(N_DEV):
                first = j * IN_SH if j * IN_SH < kr_end else j * IN_SH + (KR_PAD - ROPE)
                lo = min(first // HB * HB, XP_W - win)
                sel = (c < IN_SH) & (_aligned_col(j * IN_SH + c) == lo + r)
                res = jnp.dot(acc[:, lo:lo + win].astype(BF), sel.astype(BF), preferred_element_type=F32)
                o_ref[j] = res[:, 0:IN_SH].astype(BF)

    return _pcall(body, name, (nt,),
                  [pl.BlockSpec((tm, nr), lambda i: (i, half)), _rows(tm, MLA_IN)] + [_rows(tm, G_W)] * 4,
                  [_full((N_DEV, nr, IN_SH))], [_sds((N_DEV, nr, IN_SH), BF)],
                  scratch=[pltpu.VMEM((nr, XP_W), F32)], exchange=exchange)(u, dxp_m, *dxp_h)


def _pad_heads(w, width, real):
    lead = w.shape[:-1]
    w = w.reshape(lead + (N_HEADS, real))
    w = jnp.pad(w, [(0, 0)] * len(lead) + [(0, 0), (0, width - real)])
    return w.reshape(lead + (N_HEADS * width,))


def _unpad_heads(w, width, real):
    lead = w.shape[:-1]
    return w.reshape(lead + (N_HEADS, width))[..., :real].reshape(lead + (N_HEADS * real,))


def _rope_tables(positions):
    half = ROPE // 2
    inv_freq = 1.0 / (ROPE_THETA ** (jnp.arange(0, ROPE, 2, dtype=F32) / ROPE))
    ang = positions.astype(F32)[:, None] * inv_freq
    cos, sin = jnp.cos(ang), jnp.sin(ang)
    T = positions.shape[0]
    z = lambda n: jnp.zeros((T, n), F32)
    ta = jnp.concatenate([jnp.ones((T, NOPE), F32), cos, cos, z(HB - QK_DIM)], 1)
    tb1 = jnp.concatenate([z(NOPE), -sin, z(half), z(HB - QK_DIM)], 1)
    tb2 = jnp.concatenate([z(NOPE), z(half), sin, z(HB - QK_DIM)], 1)
    return ta, tb1, tb2


def kernel(x, positions, attn_pre_norm, w_in, mla_q_norm, mla_w_uq, mla_kv_norm, mla_w_ukv, mla_out_norm, hgrn_lb_logits, hgrn_out_norm, w_out, attn_post_norm, ffn_pre_norm, w_gate, w_up, w_down, ffn_post_norm, loss_target, m_attn_pre_norm, m_w_in, m_mla_q_norm, m_mla_w_uq, m_mla_kv_norm, m_mla_w_ukv, m_mla_out_norm, m_hgrn_lb_logits, m_hgrn_out_norm, m_w_out, m_attn_post_norm, m_ffn_pre_norm, m_w_gate, m_w_up, m_w_down, m_ffn_post_norm, v_attn_pre_norm, v_w_in, v_mla_q_norm, v_mla_w_uq, v_mla_kv_norm, v_mla_w_ukv, v_mla_out_norm, v_hgrn_lb_logits, v_hgrn_out_norm, v_w_out, v_attn_post_norm, v_ffn_pre_norm, v_w_gate, v_w_up, v_w_down, v_ffn_post_norm):
    T = x.shape[1]
    tm = min(_TM, T)
    tq = min(_TQ, T)
    xs, tgt = x[0], loss_target[0]
    uq_sh = (Q_RANK // N_DEV, N_HEADS * QK_DIM)

    b_in, b_uq, b_out, b_gu, b_d = _cast_shards(
        w_in[0], mla_w_uq[0].reshape(uq_sh), w_out[0], w_gate[0].T, w_up[0].T, w_down[0])
    g_in, g_uq = _gather_two_level("ag_first", [b_in, b_uq])
    w_in_al = _align_w_in(g_in)
    w_uq_p = _pad_heads(g_uq.reshape(Q_RANK, N_HEADS * QK_DIM), HB, QK_DIM)
    w_ukv = mla_w_ukv[0].astype(BF)
    w_uk_p = _pad_heads(w_ukv[..., :NOPE].reshape(KV_RANK, N_HEADS * NOPE), HB, NOPE)
    w_uv_p = _pad_heads(w_ukv[..., NOPE:].reshape(KV_RANK, N_HEADS * V_DIM), HB, V_DIM)
    g_on_p = _pad_heads(mla_out_norm, HB, V_DIM)
    tabs = _rope_tables(positions[0])

    xp_m, xp_h, u = _fwd_in(xs, attn_pre_norm, w_in_al, tm)
    q_att, qs_att, k_att, v_att = _mla_prep(xp_m, tabs, mla_q_norm, mla_kv_norm, w_uq_p, w_uk_p, w_uv_p, tm)
    o_hgrn, o_raw, s_all, g_out, wd = _hgrn_fwd(xp_h, hgrn_lb_logits, hgrn_out_norm, ([GATHER, GATHER], [b_out, b_d]))
    wd = wd.reshape(N_DEV * FF_PAD, D)
    o_pad, lse, wgu = _flash_fwd(qs_att, k_att, v_att, tq, ([GATHER], [b_gu]))
    wgu = wgu.reshape(N_DEV * 2 * FF_PAD, D)
    w_out_full = g_out.reshape(D, D)
    w_out_mla = jnp.pad(w_out_full[:N_HEADS * V_DIM].reshape(N_HEADS, V_DIM, D), ((0, 0), (0, HB - V_DIM), (0, 0)))
    w_out_p = jnp.concatenate([w_out_mla.reshape(N_HEADS * HB, D), w_out_full[N_HEADS * V_DIM:]], 0)
    h1, y1, z, mix = _fwd_out(o_pad, o_hgrn, xs, g_on_p, w_out_p, attn_post_norm, ffn_pre_norm, tm)
    tmf = min(_TMF, T)
    gs, us, ffs, dh2, dy2, d_fpost, loss_row = _ffn_fwd(z, wgu, wd, h1, tgt, ffn_post_norm, tm, _FB // FF_PAD)

    dgu, dh1, dy1, d_fpre, d_post = _ffn_bwd_x(dy2, gs, us, wgu, wd, h1, y1, dh2, ffn_pre_norm, attn_post_norm, tm)
    dwgu, dwd = _ffn_bwd_w(z, ffs, dgu, dy2, tmf)
    do_pad, dl, d_ohg, dw_out_p, d_on_p = _out_bwd(dy1, mix, o_pad, w_out_p, g_on_p, tm)
    dw_out_mla = dw_out_p[:N_HEADS * HB].reshape(N_HEADS, HB, D)[:, :V_DIM].reshape(N_HEADS * V_DIM, D)
    dw_out = jnp.concatenate([dw_out_mla, dw_out_p[N_HEADS * HB:]], 0).reshape(N_DEV, D // N_DEV, D).astype(BF)
    dk_att, dv_att, dq_att, p_gu, p_d, p_out = _flash_bwd(
        q_att, k_att, v_att, do_pad, lse, dl, tq,
        ([SCATTER] * 3, [dwgu.reshape(N_DEV, 2 * FF_PAD, D), dwd.reshape(N_DEV, FF_PAD, D), dw_out]))
    dxp_m, dw_uq_p, dw_uk_p, dw_uv_p, d_gq, d_gkv = _mla_prep_bwd(
        xp_m, tabs, dq_att, dk_att, dv_att, mla_q_norm, mla_kv_norm, w_uq_p, w_uk_p, w_uv_p, tm)
    dw_uq = _unpad_heads(dw_uq_p, HB, QK_DIM).reshape((N_DEV,) + uq_sh).astype(BF)
    dw_ukv = jnp.concatenate([_unpad_heads(dw_uk_p, HB, NOPE).reshape(KV_RANK, N_HEADS, NOPE),
                              _unpad_heads(dw_uv_p, HB, V_DIM).reshape(KV_RANK, N_HEADS, V_DIM)], -1)
    *dxp_h, d_lbl, d_ghn, p_uq, dw_ukv_all = _hgrn_bwd(
        xp_h, o_raw, s_all, d_ohg, hgrn_lb_logits, hgrn_out_norm,
        ([SCATTER, GATHER], [dw_uq, dw_ukv.reshape(KV_RANK, N_HEADS * HB)]))
    dw_in_a, = _in_bwd_w("in_bwd_w_a", u, dxp_m, dxp_h, tm, 0)
    dw_in_b, p_in_a = _in_bwd_w("in_bwd_w_b", u, dxp_m, dxp_h, tm, 1, ([SCATTER], [dw_in_a]))
    grad_x, d_pre, p_in_b = _in_bwd_x(xs, dxp_m, dxp_h, dh1, w_in_al, attn_pre_norm, tm, ([SCATTER], [dw_in_b]))
    d_on = _unpad_heads(d_on_p, HB, V_DIM)

    ukv2 = lambda a: a.reshape(KV_RANK, N_HEADS * HB)
    vecs = [d_pre, d_gq, d_gkv, d_on, d_lbl, d_ghn, d_post, d_fpre, d_fpost, loss_row]
    small_w = [attn_pre_norm, mla_q_norm, mla_kv_norm, ukv2(mla_w_ukv), mla_out_norm, hgrn_lb_logits, hgrn_out_norm,
               attn_post_norm, ffn_pre_norm, ffn_post_norm]
    small_m = [m_attn_pre_norm, m_mla_q_norm, m_mla_kv_norm, ukv2(m_mla_w_ukv), m_mla_out_norm, m_hgrn_lb_logits,
               m_hgrn_out_norm, m_attn_post_norm, m_ffn_pre_norm, m_ffn_post_norm]
    small_v = [v_attn_pre_norm, v_mla_q_norm, v_mla_kv_norm, ukv2(v_mla_w_ukv), v_mla_out_norm, v_hgrn_lb_logits,
               v_hgrn_out_norm, v_attn_post_norm, v_ffn_pre_norm, v_ffn_post_norm]
    rall = _final_exchange(vecs)
    s_g, s_d, s_m, s_v, loss_all = _small_adam(rall, dw_ukv_all, 3, small_w, small_m, small_v)
    r_in = _shard_adam("adam_w_in", [p_in_a, p_in_b], w_in[0], m_w_in[0], v_w_in[0], 256)
    r_uq = _shard_adam("adam_w_uq", [p_uq], mla_w_uq[0].reshape(uq_sh), m_mla_w_uq[0].reshape(uq_sh),
                       v_mla_w_uq[0].reshape(uq_sh), uq_sh[0])
    r_out = _shard_adam("adam_w_out", [p_out], w_out[0], m_w_out[0], v_w_out[0], D // N_DEV)
    r_g, r_u = _gate_up_adam(p_gu, (w_gate[0].T, w_up[0].T), (m_w_gate[0].T, m_w_up[0].T),
                             (v_w_gate[0].T, v_w_up[0].T))
    r_g, r_u = [a.T for a in r_g], [a.T for a in r_u]
    r_d = _shard_adam("adam_w_down", [p_d], w_down[0], m_w_down[0], v_w_down[0], FF_SH // 2)

    loss = loss_all[0, 0]

    def assemble(big, small):
        b_in, b_uq, b_out, b_g, b_u, b_d = big
        return [small[0], b_in[None], small[1], b_uq.reshape(mla_w_uq.shape), small[2],
                small[3].reshape(mla_w_ukv.shape), small[4], small[5], small[6], b_out[None], small[7], small[8],
                b_g[None], b_u[None], b_d[None], small[9]]

    outs = [loss, grad_x[None]]
    for idx, small in enumerate((s_g, s_d, s_m, s_v)):
        outs += assemble([r[idx] for r in (r_in, r_uq, r_out, r_g, r_u, r_d)], small)
    return tuple(outs)
```

```python
import jax
import jax.numpy as jnp
from jax import lax
from jax.experimental import pallas as pl
from jax.experimental.pallas import tpu as pltpu

BF = jnp.bfloat16
F32 = jnp.float32
MESH = pl.DeviceIdType.MESH

N_DEV = 8
D = 1024
EPS = 1e-6
LOG2E = 1.4426950408889634
ROPE_THETA = 10000.0
N_HEADS = 8
HB = 128
NOPE = 64
ROPE = 32
V_DIM = 64
QK_DIM = NOPE + ROPE
Q_RANK = 384
KV_RANK = 128
KR_PAD = 128
MLA_IN = Q_RANK + KV_RANK + KR_PAD
G_HEADS = 4
G_DIM = 128
G_W = G_HEADS * G_DIM
CHUNK = 64
SUB_FWD = 16
SUB_BWD = 16
XP_W = MLA_IN + 4 * G_W
IN_SH = 324
IN_W = N_DEV * IN_SH
FF_SH = 352
FF_PAD = 384
MIX_W = N_HEADS * HB + G_W

ADAM_LR = 0.001
ADAM_B1 = 0.9
ADAM_B2 = 0.999
ADAM_EPS = 1e-08
ADAM_WD = 0.01
ADAM_STEP = 10

_TM = 512
_TMF = 1024
_TQ = 512
_AH = 2
_FB = 768
_TB = 1024
_HP = 4
_VMEM_LIMIT = 56 * 1024 * 1024
NEG = -1e30


def _dot(a, b):
    return jnp.dot(a.astype(BF), b.astype(BF), preferred_element_type=F32)


def _dot_nt(a, b):
    return lax.dot_general(a.astype(BF), b.astype(BF), (((1,), (1,)), ((), ())), preferred_element_type=F32)


def _dot_tn(a, b):
    return lax.dot_general(a.astype(BF), b.astype(BF), (((0,), (0,)), ((), ())), preferred_element_type=F32)


def _sigmoid(x):
    return 1.0 / (1.0 + jnp.exp(-x))


def _rms(x, n):
    r = lax.rsqrt(jnp.sum(x * x, -1, keepdims=True) * (1.0 / n) + EPS)
    return x * r, r


def _rms_bwd(nx, r, g, dy, n):
    dg = jnp.sum(dy * nx, 0, keepdims=True)
    dn = dy * g
    dx = r * (dn - nx * (jnp.sum(dn * nx, -1, keepdims=True) * (1.0 / n)))
    return dx, dg


def _adamw(w, g, m, v):
    m2 = ADAM_B1 * m + (1.0 - ADAM_B1) * g
    v2 = ADAM_B2 * v + (1.0 - ADAM_B2) * (g * g)
    m_hat = m2 / (1.0 - ADAM_B1 ** ADAM_STEP)
    v_hat = v2 / (1.0 - ADAM_B2 ** ADAM_STEP)
    delta = -ADAM_LR * (m_hat / (jnp.sqrt(v_hat) + ADAM_EPS) + ADAM_WD * w)
    return delta, m2, v2


def _pcall(body, name, grid, in_specs, out_specs, out_shape, scratch=(), exchange=None):
    scratch = list(scratch)
    extra = ()
    if exchange is not None:
        kinds, extra = exchange
        in_specs, out_specs, out_shape = list(in_specs), list(out_specs), list(out_shape)
        n_in, n_out, n_scr, n_x = len(in_specs), len(out_specs), len(scratch), len(extra)
        inner = body

        def body(*refs):
            ins, rest = refs[:n_in], refs[n_in:]
            x_src, rest = rest[:n_x], rest[n_x:]
            outs, rest = rest[:n_out], rest[n_out:]
            x_dst, rest = rest[:n_x], rest[n_x:]
            ex = _Exchange(kinds, x_src, x_dst, *rest[n_scr:])
            first = pl.program_id(0) == 0
            last = pl.program_id(0) == grid[0] - 1
            for a in range(1, len(grid)):
                first = first & (pl.program_id(a) == 0)
                last = last & (pl.program_id(a) == grid[a] - 1)
            pl.when(first)(ex.start)
            inner(*ins, *outs, *rest[:n_scr])
            pl.when(last)(ex.wait)

        in_specs += [_HBM] * n_x
        out_specs += [_HBM] * n_x
        out_shape += _exchange_shapes(kinds, extra)
        scratch += _exchange_sems(n_x)
    call = pl.pallas_call(
        body, name=name, grid=grid, in_specs=in_specs, out_specs=out_specs, out_shape=out_shape,
        scratch_shapes=scratch,
        compiler_params=pltpu.CompilerParams(
            dimension_semantics=("arbitrary",) * len(grid), vmem_limit_bytes=_VMEM_LIMIT))
    return lambda *operands: call(*operands, *extra)


def _full(shape):
    return pl.BlockSpec(shape, lambda *_: (0,) * len(shape))


def _rows(tm, n):
    return pl.BlockSpec((tm, n), lambda i, *_: (i, 0))


def _sds(shape, dtype=F32):
    return jax.ShapeDtypeStruct(shape, dtype)


def _peer(k, x, y, c):
    px = 1 - x if (k >> 2) & 1 else x
    py = 1 - y if (k >> 1) & 1 else y
    pc = 1 - c if k & 1 else c
    return px, py, pc


GATHER, SCATTER = "gather", "scatter"


class _Exchange:
    def __init__(self, kinds, srcs, dsts, send_sems, recv_sems, loc_sems):
        self.kinds, self.srcs, self.dsts = kinds, srcs, dsts
        self.send_sems, self.recv_sems, self.loc_sems = send_sems, recv_sems, loc_sems
        self.x, self.y, self.c = lax.axis_index("x"), lax.axis_index("y"), lax.axis_index("c")
        self.me = 4 * self.x + 2 * self.y + self.c

    def _src(self, w, slot):
        return self.srcs[w] if self.kinds[w] == GATHER else self.srcs[w].at[slot]

    def _dst(self, w, slot):
        return self.dsts[w].at[slot]

    def _copy(self, w, k, outgoing):
        px, py, pc = _peer(k, self.x, self.y, self.c)
        pid = 4 * px + 2 * py + pc
        return pltpu.make_async_remote_copy(
            src_ref=self._src(w, pid if outgoing else self.me),
            dst_ref=self._dst(w, self.me if outgoing else pid),
            send_sem=self.send_sems.at[w, k - 1], recv_sem=self.recv_sems.at[w, k - 1],
            device_id=(px, py, pc), device_id_type=MESH)

    def _local(self, w):
        return pltpu.make_async_copy(self._src(w, self.me), self._dst(w, self.me), self.loc_sems.at[w])

    def start(self):
        for w in range(len(self.srcs)):
            self._local(w).start()
            for k in range(1, N_DEV):
                self._copy(w, k, True).start()

    def wait(self):
        for w in range(len(self.srcs)):
            self._local(w).wait()
            for k in range(1, N_DEV):
                self._copy(w, k, False).wait_recv()
        for w in range(len(self.srcs)):
            for k in range(1, N_DEV):
                self._copy(w, k, True).wait_send()


def _exchange_sems(n_w):
    return [pltpu.SemaphoreType.DMA((n_w, N_DEV - 1)), pltpu.SemaphoreType.DMA((n_w, N_DEV - 1)),
            pltpu.SemaphoreType.DMA((n_w,))]


def _exchange_shapes(kinds, srcs):
    return [_sds(((N_DEV,) if kd == GATHER else ()) + tuple(s.shape), s.dtype) for kd, s in zip(kinds, srcs)]


_HBM = pl.BlockSpec(memory_space=pl.ANY)


def _cast_shards(w_in, w_uq, w_out, w_gate_t, w_up_t, w_down):
    shapes = [(D, IN_SH), (Q_RANK // N_DEV, N_HEADS * QK_DIM), (D // N_DEV, D), (2 * FF_PAD, D), (FF_PAD, D)]

    def body(win, wuq, wout, wg, wu, wd, sin_, suq, sout, sgu, sd):
        sin_[...] = win[...].astype(BF)
        suq[...] = wuq[...].astype(BF)
        sout[...] = wout[...].astype(BF)
        sgu[...] = jnp.zeros(sgu.shape, BF)
        sgu[0:FF_SH, :] = wg[...].astype(BF)
        sgu[FF_PAD:FF_PAD + FF_SH, :] = wu[...].astype(BF)
        sd[...] = jnp.zeros(sd.shape, BF)
        sd[0:FF_SH, :] = wd[...].astype(BF)

    vm = pl.BlockSpec(memory_space=pltpu.VMEM)
    return pl.pallas_call(
        body, name="cast_shards", in_specs=[vm] * 6, out_specs=[vm] * 5,
        out_shape=[_sds(s, BF) for s in shapes],
        compiler_params=pltpu.CompilerParams(vmem_limit_bytes=_VMEM_LIMIT),
    )(w_in, w_uq, w_out, w_gate_t, w_up_t, w_down)


def _gather_two_level(name, srcs):
    n_w = len(srcs)

    def body(*refs):
        src, dst = refs[:n_w], refs[n_w:2 * n_w]
        send_sems, recv_sems, loc_sems = refs[2 * n_w:]
        x, y, c = lax.axis_index("x"), lax.axis_index("y"), lax.axis_index("c")
        me, sibling = (x, y, c), (x, y, 1 - c)
        chips = [(1 - x, y), (x, 1 - y), (1 - x, 1 - y)]
        slot = lambda p: 4 * p[0] + 2 * p[1] + p[2]

        def copy(w, k, block, to, own=False):
            return pltpu.make_async_remote_copy(
                src_ref=src[w] if own else dst[w].at[slot(block)], dst_ref=dst[w].at[slot(block)],
                send_sem=send_sems.at[w, k], recv_sem=recv_sems.at[w, k], device_id=to, device_id_type=MESH)

        local = [pltpu.make_async_copy(src[w], dst[w].at[slot(me)], loc_sems.at[w]) for w in range(n_w)]
        first, passed = [], []
        for w in range(n_w):
            local[w].start()
            first.append(copy(w, 0, me, sibling, own=True))
            first += [copy(w, 1 + j, me, (*chip, c), own=True) for j, chip in enumerate(chips)]
        for cp in first:
            cp.start()
        for w in range(n_w):
            for j, chip in enumerate(chips):
                copy(w, 1 + j, (*chip, c), me).wait_recv()
                passed.append(copy(w, 4 + j, (*chip, c), sibling))
                passed[-1].start()
        for w in range(n_w):
            copy(w, 0, sibling, me).wait_recv()
            for j, chip in enumerate(chips):
                copy(w, 4 + j, (*chip, 1 - c), me).wait_recv()
        for cp in first + passed:
            cp.wait_send()
        for w in range(n_w):
            local[w].wait()

    return pl.pallas_call(
        body, name=name, in_specs=[_HBM] * n_w, out_specs=[_HBM] * n_w,
        out_shape=_exchange_shapes([GATHER] * n_w, srcs), scratch_shapes=_exchange_sems(n_w))(*srcs)


def _row_offsets(arrays):
    offs, rows = [], 0
    for a in arrays:
        offs.append(rows)
        rows += a.shape[0]
    return offs, -(-rows // 8) * 8


def _final_exchange(vecs):
    n_p = len(vecs)
    offs, rows = _row_offsets(vecs)

    def body(*refs):
        g_refs = refs[:n_p]
        rall, pk, send_sems, recv_sems, loc_sem = refs[n_p:]
        x, y, c = lax.axis_index("x"), lax.axis_index("y"), lax.axis_index("c")
        me = 4 * x + 2 * y + c
        pk[...] = jnp.zeros(pk.shape, F32)
        for p in range(n_p):
            r, n = g_refs[p].shape
            pk[offs[p]:offs[p] + r, 0:n] = g_refs[p][...]

        def remote(k):
            return pltpu.make_async_remote_copy(
                src_ref=pk, dst_ref=rall.at[me], send_sem=send_sems.at[k - 1], recv_sem=recv_sems.at[k - 1],
                device_id=_peer(k, x, y, c), device_id_type=MESH)

        def arrival(k):
            px, py, pc = _peer(k, x, y, c)
            return pltpu.make_async_remote_copy(
                src_ref=pk, dst_ref=rall.at[4 * px + 2 * py + pc], send_sem=send_sems.at[k - 1],
                recv_sem=recv_sems.at[k - 1], device_id=(px, py, pc), device_id_type=MESH)

        local = pltpu.make_async_copy(pk, rall.at[me], loc_sem)
        local.start()
        for k in range(1, N_DEV):
            remote(k).start()
        local.wait()
        for k in range(1, N_DEV):
            arrival(k).wait_recv()
        for k in range(1, N_DEV):
            remote(k).wait_send()

    vm = pl.BlockSpec(memory_space=pltpu.VMEM)
    return pl.pallas_call(
        body, name="final_exchange", in_specs=[vm] * n_p, out_specs=vm, out_shape=_sds((N_DEV, rows, D)),
        scratch_shapes=[pltpu.VMEM((rows, D), F32),
                        pltpu.SemaphoreType.DMA((N_DEV - 1,)), pltpu.SemaphoreType.DMA((N_DEV - 1,)),
                        pltpu.SemaphoreType.DMA],
    )(*vecs)


def _small_adam(rall, big_parts, big, ws, ms, vs):
    n_p = len(ws)
    packed = [w for p, w in enumerate(ws) if p != big] + [jax.ShapeDtypeStruct((1, HB), F32)]
    offs, _ = _row_offsets(packed)
    offs = offs[:big] + [None] + offs[big:]

    def total(ref, sl):
        g = ref[(0,) + sl]
        for j in range(1, N_DEV):
            g = g + ref[(j,) + sl]
        return g

    def body(*refs):
        rall_ref, big_ref = refs[:2]
        w_refs, m_refs, v_refs = refs[2:2 + n_p], refs[2 + n_p:2 + 2 * n_p], refs[2 + 2 * n_p:2 + 3 * n_p]
        outs = refs[2 + 3 * n_p:]
        for p in range(n_p):
            r, n = w_refs[p].shape
            if p == big:
                g = total(big_ref, (slice(0, r), slice(0, n)))
            else:
                g = total(rall_ref, (slice(offs[p], offs[p] + r), slice(0, n)))
            delta, m2, v2 = _adamw(w_refs[p][...], g, m_refs[p][...], v_refs[p][...])
            outs[p][...] = g
            outs[n_p + p][...] = delta
            outs[2 * n_p + p][...] = m2
            outs[3 * n_p + p][...] = v2
        outs[4 * n_p][...] = total(rall_ref, (slice(offs[n_p], offs[n_p] + 1), slice(0, HB)))

    vm = pl.BlockSpec(memory_space=pltpu.VMEM)
    res = pl.pallas_call(
        body, name="small_adam", in_specs=[vm] * (2 + 3 * n_p), out_specs=[vm] * (4 * n_p + 1),
        out_shape=[_sds(w.shape) for w in ws] * 4 + [_sds((1, HB))],
        compiler_params=pltpu.CompilerParams(vmem_limit_bytes=_VMEM_LIMIT),
    )(rall, big_parts, *ws, *ms, *vs)
    return res[:n_p], res[n_p:2 * n_p], res[2 * n_p:3 * n_p], res[3 * n_p:4 * n_p], res[4 * n_p]


def _device_sum(p_ref):
    g = p_ref[0].astype(F32)
    for j in range(1, N_DEV):
        g = g + p_ref[j].astype(F32)
    return g


def _shard_adam(name, parts, w, m, v, tr):
    a0, b0 = w.shape
    n_p = len(parts)
    b = parts[0].shape[2]
    first = [0]
    for p in parts:
        first.append(first[-1] + p.shape[1] // tr)

    def body(*refs):
        p_refs = refs[:n_p]
        w_ref, m_ref, v_ref, g_out, d_out, m_out, v_out = refs[n_p:]
        i = pl.program_id(0)
        g = _device_sum(p_refs[0])
        for k in range(1, n_p):
            g = jnp.where(i >= first[k], _device_sum(p_refs[k]), g)
        g = g[:, 0:b0]
        delta, m2, v2 = _adamw(w_ref[...], g, m_ref[...], v_ref[...])
        g_out[...] = g
        d_out[...] = delta
        m_out[...] = m2
        v_out[...] = v2

    def part_spec(k):
        last = first[k + 1] - first[k] - 1
        return pl.BlockSpec((N_DEV, tr, b), lambda i: (0, jnp.minimum(jnp.maximum(i - first[k], 0), last), 0))

    blk = pl.BlockSpec((tr, b0), lambda i: (i, 0))
    return _pcall(
        body, name, (a0 // tr,), [part_spec(k) for k in range(n_p)] + [blk, blk, blk],
        [blk] * 4, [_sds((a0, b0))] * 4)(*parts, w, m, v)


def _gate_up_adam(parts, ws, ms, vs):
    tc = 256

    def body(p_ref, wg, wu, mg, mu, vg, vu, *outs):
        g = _device_sum(p_ref)
        for k, (w_ref, m_ref, v_ref) in enumerate(((wg, mg, vg), (wu, mu, vu))):
            gk = g[k * FF_PAD:k * FF_PAD + FF_SH]
            delta, m2, v2 = _adamw(w_ref[...], gk, m_ref[...], v_ref[...])
            for o, val in zip(outs[4 * k:4 * k + 4], (gk, delta, m2, v2)):
                o[...] = val

    blk = pl.BlockSpec((FF_SH, tc), lambda i: (0, i))
    res = _pcall(
        body, "adam_w_gate_up", (D // tc,), [pl.BlockSpec((N_DEV, 2 * FF_PAD, tc), lambda i: (0, 0, i))] + [blk] * 6,
        [blk] * 8, [_sds((FF_SH, D))] * 8)(parts, *ws, *ms, *vs)
    return res[:4], res[4:]


def _fwd_in(x, g_pre, w_in_al, tm):
    T = x.shape[0]

    def body(x_ref, g_ref, w_ref, xm_ref, xh_ref, u_ref):
        nx, _ = _rms(x_ref[...], D)
        u = (nx * g_ref[...]).astype(BF)
        u_ref[...] = u
        xm_ref[...] = jnp.dot(u, w_ref[:, 0:MLA_IN], preferred_element_type=F32)
        xh_ref[...] = jnp.dot(u, w_ref[:, MLA_IN:XP_W], preferred_element_type=F32)

    return _pcall(body, "fwd_in", (T // tm,),
                  [_rows(tm, D), _full((1, D)), _full((D, XP_W))],
                  [_rows(tm, MLA_IN), _rows(tm, 4 * G_W), _rows(tm, D)],
                  [_sds((T, MLA_IN)), _sds((T, 4 * G_W)), _sds((T, D), BF)])(x, g_pre, w_in_al)


def _rope(blk, ta, tb1, tb2):
    return blk * ta + pltpu.roll(blk, HB - ROPE // 2, 1) * tb1 + pltpu.roll(blk, ROPE // 2, 1) * tb2


def _unrope(d, ta, tb1, tb2):
    return d * ta + pltpu.roll(d * tb1, ROPE // 2, 1) + pltpu.roll(d * tb2, HB - ROPE // 2, 1)


def _mla_prep(xp, tabs, g_q, g_kv, w_uq, w_uk, w_uv, tm):
    T = xp.shape[0]
    W = N_HEADS * HB

    def body(xp_ref, ta_ref, tb1_ref, tb2_ref, gq_ref, gkv_ref, wuq_ref, wuk_ref, wuv_ref, q_ref, qs_ref, k_ref, v_ref):
        ta, tb1, tb2 = ta_ref[...], tb1_ref[...], tb2_ref[...]
        nq, _ = _rms(xp_ref[:, 0:Q_RANK], Q_RANK)
        nkv, _ = _rms(xp_ref[:, Q_RANK:Q_RANK + KV_RANK], KV_RANK)
        nkv = (nkv * gkv_ref[...]).astype(BF)
        qpre = _dot(nq * gq_ref[...], wuq_ref[...])
        kpre = jnp.dot(nkv, wuk_ref[...], preferred_element_type=F32)
        v = jnp.dot(nkv, wuv_ref[...], preferred_element_type=F32)
        lane = lax.broadcasted_iota(jnp.int32, (tm, W), 1)
        v_ref[...] = jnp.where((lane & (HB - 1)) == V_DIM, 1.0, v).astype(BF)
        kr = _rope(pltpu.roll(xp_ref[:, Q_RANK + KV_RANK:MLA_IN], NOPE, 1), ta, tb1, tb2)
        for h in range(N_HEADS):
            sl = slice(h * HB, (h + 1) * HB)
            qr = _rope(qpre[:, sl], ta, tb1, tb2)
            q_ref[:, sl] = qr.astype(BF)
            qs_ref[:, sl] = (qr * (QK_DIM ** -0.5 * LOG2E)).astype(BF)
            k_ref[:, sl] = (kpre[:, sl] + kr).astype(BF)

    tab = _rows(tm, HB)
    return _pcall(body, "mla_prep", (T // tm,),
                  [_rows(tm, MLA_IN), tab, tab, tab, _full((1, Q_RANK)), _full((1, KV_RANK)),
                   _full((Q_RANK, W)), _full((KV_RANK, W)), _full((KV_RANK, W))],
                  [_rows(tm, W)] * 4, [_sds((T, W), BF)] * 4)(xp, *tabs, g_q, g_kv, w_uq, w_uk, w_uv)


def _flash_fwd(q, k, v, tq, exchange=None):
    T = q.shape[0]
    hp = _AH
    W = hp * HB

    def body(q_ref, k_ref, v_ref, o_ref, lse_ref):
        i = pl.program_id(1)

        def blk(j, carry, masked):
            st = pl.multiple_of(j * tq, tq)
            out = []
            for h in range(hp):
                ls = slice(h * HB, (h + 1) * HB)
                m, acc = carry[h]
                s = _dot_nt(q_ref[:, ls], k_ref[pl.ds(st, tq), ls])
                if masked:
                    r = lax.broadcasted_iota(jnp.int32, (tq, tq), 0)
                    c = lax.broadcasted_iota(jnp.int32, (tq, tq), 1)
                    s = jnp.where(c <= r, s, NEG)
                m2 = jnp.maximum(m, jnp.max(s, -1, keepdims=True))
                p = jnp.exp2(s - m2)
                out.append((m2, jnp.exp2(m - m2) * acc + _dot(p, v_ref[pl.ds(st, tq), ls])))
            return tuple(out)

        init = tuple((jnp.full((tq, 1), NEG, F32), jnp.zeros((tq, HB), F32)) for _ in range(hp))
        carry = lax.fori_loop(0, i, lambda j, cr: blk(j, cr, False), init)
        res = blk(i, carry, True)
        lane = lax.broadcasted_iota(jnp.int32, (tq, HB), 1)
        for h in range(hp):
            ls = slice(h * HB, (h + 1) * HB)
            m, acc = res[h]
            l = acc[:, V_DIM:V_DIM + 1]
            o_ref[:, ls] = jnp.where(lane < V_DIM, acc / l, 0.0)
            lse_ref[:, ls] = jnp.broadcast_to(m * (1.0 / LOG2E) + jnp.log(l), (tq, HB))

    qs = pl.BlockSpec((tq, W), lambda h, i: (i, h))
    kvs = pl.BlockSpec((T, W), lambda h, i: (0, h))
    return _pcall(body, "flash_fwd", (N_HEADS // hp, T // tq), [qs, kvs, kvs], [qs, qs],
                  [_sds((T, N_HEADS * HB))] * 2, exchange=exchange)(q, k, v)


def _gates(hq, hf, lb):
    sig = _sigmoid(hf)
    f = lb + (1.0 - lb) * sig
    sq = _sigmoid(hq)
    return hq * sq, 1.0 - f, f, jnp.log(f), sig, sq


def _lower_bound(lbl_ref):
    l0, l1 = lbl_ref[0:1, :], lbl_ref[1:2, :]
    mx = jnp.maximum(l0, l1)
    e0, e1 = jnp.exp(l0 - mx), jnp.exp(l1 - mx)
    return e0 / (e0 + e1)


def _split3(x):
    hi = x.astype(BF)
    r1 = x - hi.astype(F32)
    mid = r1.astype(BF)
    lo = (r1 - mid.astype(F32)).astype(BF)
    return hi, mid, lo


def _tri_mm(tri, x):
    hi, mid, lo = _split3(x)
    mm = lambda t: jnp.dot(tri, t, preferred_element_type=F32)
    return mm(hi) + mm(mid) + mm(lo)


def _intra_codes(sub):
    row = lax.broadcasted_iota(jnp.int32, (CHUNK, CHUNK), 0)
    col = lax.broadcasted_iota(jnp.int32, (CHUNK, CHUNK), 1)
    return sub, row, col


def _intra(q, k, b2, b_s, codes, da=None):
    grad = da is not None
    pow2 = (lambda x: jnp.exp2(jnp.minimum(x, 0.0))) if grad else jnp.exp2
    sub, row, col = codes
    a = jnp.zeros((CHUNK, CHUNK), F32)
    dq = jnp.zeros((CHUNK, G_DIM), F32)
    dk = jnp.zeros((CHUNK, G_DIM), F32)
    for i in range(1, CHUNK // sub):
        b0 = b_s[sub * i - 1:sub * i, :]
        eq, ek = pow2(b2 - b0), pow2(b0 - b2)
        mask = ((row // sub) == i) & (col < sub * i)
        if grad:
            dai = jnp.where(mask, da, 0.0)
            dq = dq + _dot(dai, k * ek) * eq
            dk = dk + _dot_tn(dai, q * eq) * ek
        else:
            a = jnp.where(mask, _dot_nt(q * eq, k * ek), a)
    for d in range(sub):
        ksh = pltpu.roll(k, d, 0) if d else k
        bsh = pltpu.roll(b2, d, 0) if d else b2
        e = pow2(b2 - bsh)
        mask = (col == row - d) & ((row & (sub - 1)) >= d)
        if grad:
            g = jnp.sum(jnp.where(mask, da, 0.0), -1, keepdims=True) * e
            dq = dq + g * ksh
            cb = g * q
            dk = dk + (pltpu.roll(cb, CHUNK - d, 0) if d else cb)
        else:
            a = jnp.where(mask, jnp.sum(q * ksh * e, -1, keepdims=True), a)
    return (dq, dk) if grad else a


def _hgrn_fwd(xp, lb_logits, g_hn, exchange=None):
    T = xp.shape[0]
    tb = min(_TB, T)
    ncb = tb // CHUNK
    hp = _HP
    W = hp * G_DIM

    def body(hq_ref, hf_ref, hi_ref, hg_ref, lbl_ref, ghn_ref, out_ref, oraw_ref, sall_ref, aall_ref, st_ref, b_s):
        lb_all = _lower_bound(lbl_ref)

        @pl.when(pl.program_id(1) == 0)
        def _():
            st_ref[...] = jnp.zeros(st_ref.shape, F32)

        row = lax.broadcasted_iota(jnp.int32, (CHUNK, CHUNK), 0)
        col = lax.broadcasted_iota(jnp.int32, (CHUNK, CHUNK), 1)
        tri = (col <= row).astype(BF)
        codes = _intra_codes(SUB_FWD)

        def chunk(c, carry):
            sl = pl.ds(pl.multiple_of(c * CHUNK, CHUNK), CHUNK)
            for h in range(hp):
                ls = slice(h * G_DIM, (h + 1) * G_DIM)
                q, k, _, lf, _, _ = _gates(hq_ref[sl, ls], hf_ref[sl, ls], lb_all[:, ls])
                v = hi_ref[sl, ls]
                b2 = _tri_mm(tri, lf) * LOG2E
                b_s[h] = b2
                st = st_ref[h]
                sall_ref[c, h] = st
                a = _intra(q, k, b2, b_s.at[h], codes)
                aall_ref[c, h] = a
                o = _dot_nt(q * jnp.exp2(b2), st) + _dot(a, v)
                bl = b_s[h, CHUNK - 1:CHUNK, :]
                st_ref[h] = st * jnp.exp2(bl) + _dot_tn(v, k * jnp.exp2(bl - b2))
                oraw_ref[sl, ls] = o
                n, _ = _rms(o, G_DIM)
                hg = hg_ref[sl, ls]
                out_ref[sl, ls] = n * ghn_ref[:, ls] * (hg * _sigmoid(hg))
            return carry

        lax.fori_loop(0, ncb, chunk, 0)

    col_blk = lambda j: pl.BlockSpec((tb, W), lambda p, t: (t, j * (G_HEADS // hp) + p))
    head = pl.BlockSpec((tb, W), lambda p, t: (t, p))
    return _pcall(
        body, "hgrn_fwd", (G_HEADS // hp, T // tb),
        [col_blk(0), col_blk(1), col_blk(2), col_blk(3),
         pl.BlockSpec((2, W), lambda p, t: (0, p)), pl.BlockSpec((1, W), lambda p, t: (0, p))],
        [head, head, pl.BlockSpec((ncb, hp, G_DIM, G_DIM), lambda p, t: (t, p, 0, 0)),
         pl.BlockSpec((ncb, hp, CHUNK, CHUNK), lambda p, t: (t, p, 0, 0))],
        [_sds((T, G_W)), _sds((T, G_W)), _sds((T // CHUNK, G_HEADS, G_DIM, G_DIM)),
         _sds((T // CHUNK, G_HEADS, CHUNK, CHUNK))],
        scratch=[pltpu.VMEM((hp, G_DIM, G_DIM), F32), pltpu.VMEM((hp, CHUNK, G_DIM), F32)], exchange=exchange,
    )(xp, xp, xp, xp, lb_logits, g_hn)


def _fwd_out(o_pad, o_hgrn, x, g_on, w_out, g_post, g_fpre, tm):
    T = x.shape[0]

    def body(o_ref, oh_ref, x_ref, gon_ref, w_ref, gpost_ref, gfpre_ref, h1_ref, y1_ref, z_ref, mix_ref):
        for h in range(N_HEADS):
            sl = slice(h * HB, (h + 1) * HB)
            n, _ = _rms(o_ref[:, sl], V_DIM)
            mix_ref[:, sl] = (n * gon_ref[:, sl]).astype(BF)
        mix_ref[:, N_HEADS * HB:MIX_W] = oh_ref[...].astype(BF)
        y1 = jnp.dot(mix_ref[...], w_ref[...], preferred_element_type=F32)
        y1_ref[...] = y1
        ny, _ = _rms(y1, D)
        h1 = x_ref[...] + ny * gpost_ref[...]
        h1_ref[...] = h1
        nh, _ = _rms(h1, D)
        z_ref[...] = (nh * gfpre_ref[...]).astype(BF)

    return _pcall(body, "fwd_out", (T // tm,),
                  [_rows(tm, N_HEADS * HB), _rows(tm, G_W), _rows(tm, D), _full((1, N_HEADS * HB)),
                   _full((MIX_W, D)), _full((1, D)), _full((1, D))],
                  [_rows(tm, D), _rows(tm, D), _rows(tm, D), _rows(tm, MIX_W)],
                  [_sds((T, D)), _sds((T, D)), _sds((T, D), BF), _sds((T, MIX_W), BF)],
                  )(o_pad, o_hgrn, x, g_on, w_out, g_post, g_fpre)


def _ffn_fwd(z, wgu, wd, h1, tgt, g_fpost, tm, nd):
    T = z.shape[0]
    fb = nd * FF_PAD
    nf = wd.shape[0] // fb

    def body(z_ref, wgu_ref, wd_ref, h1_ref, t_ref, gp_ref,
             as_ref, bs_ref, ff_ref, dh2_ref, dy2_ref, dgp_ref, loss_ref, acc):
        i, j = pl.program_id(0), pl.program_id(1)
        gu = _dot_nt(z_ref[...], wgu_ref[...])
        piece = lambda n: gu[:, n * FF_PAD:(n + 1) * FF_PAD]
        g = piece(0) if nd == 1 else jnp.concatenate([piece(2 * n) for n in range(nd)], 1)
        u = piece(1) if nd == 1 else jnp.concatenate([piece(2 * n + 1) for n in range(nd)], 1)
        s = _sigmoid(g)
        b = g * s
        ff = (b * u).astype(BF)
        as_ref[...] = (u * _dsilu(g, s)).astype(BF)
        bs_ref[...] = b.astype(BF)
        ff_ref[...] = ff
        part = jnp.dot(ff, wd_ref[...], preferred_element_type=F32)

        @pl.when(j == 0)
        def _():
            acc[...] = part

        @pl.when(j > 0)
        def _():
            acc[...] += part

        @pl.when((i == 0) & (j == 0))
        def _():
            dgp_ref[...] = jnp.zeros(dgp_ref.shape, F32)
            loss_ref[...] = jnp.zeros(loss_ref.shape, F32)

        @pl.when(j == nf - 1)
        def _():
            ny, r = _rms(acc[...], D)
            err = h1_ref[...] + ny * gp_ref[...] - t_ref[...]
            loss_ref[...] += 0.5 * jnp.sum(jnp.sum(err * err, -1, keepdims=True) * (1.0 / D), 0, keepdims=True)
            dh2 = err * (1.0 / D)
            dh2_ref[...] = dh2
            dy2, dgp = _rms_bwd(ny, r, gp_ref[...], dh2, D)
            dy2_ref[...] = dy2.astype(BF)
            dgp_ref[...] += dgp

    tok = lambda n: pl.BlockSpec((tm, n), lambda i, j: (i, 0))
    col = pl.BlockSpec((tm, fb), lambda i, j: (i, j))
    return _pcall(
        body, "ffn_fwd", (T // tm, nf),
        [tok(D), pl.BlockSpec((2 * fb, D), lambda i, j: (j, 0)), pl.BlockSpec((fb, D), lambda i, j: (j, 0)),
         tok(D), tok(D), _full((1, D))],
        [col, col, col, tok(D), tok(D), _full((1, D)), _full((1, HB))],
        [_sds((T, nf * fb), BF)] * 3 + [_sds((T, D)), _sds((T, D), BF), _sds((1, D)), _sds((1, HB))],
        scratch=[pltpu.VMEM((tm, D), F32)],
    )(z, wgu, wd, h1, tgt, g_fpost)


def _dsilu(x, s):
    return s * (1.0 + x * (1.0 - s))


def _ffn_bwd_x(dy2, gs, us, wgu, wd, h1, y1, dh2, g_fpre, g_post, tm):
    T = dy2.shape[0]
    nf = wd.shape[0] // _FB

    def body(dy2_ref, gs_ref, us_ref, wgu_ref, wd_ref, h1_ref, y1_ref, dh2_ref, gf_ref, gp_ref,
             dgu_ref, dh1_ref, dy1_ref, dgf_ref, dgp_ref, acc):
        i, j = pl.program_id(0), pl.program_id(1)
        dff = _dot_nt(dy2_ref[...], wd_ref[...])
        dg = (dff * gs_ref[...].astype(F32)).astype(BF)
        du = (dff * us_ref[...].astype(F32)).astype(BF)
        dgu = jnp.concatenate([dg[:, 0:FF_PAD], du[:, 0:FF_PAD], dg[:, FF_PAD:_FB], du[:, FF_PAD:_FB]], 1)
        dgu_ref[...] = dgu
        part = jnp.dot(dgu, wgu_ref[...], preferred_element_type=F32)

        @pl.when(j == 0)
        def _():
            acc[...] = part

        @pl.when(j > 0)
        def _():
            acc[...] += part

        @pl.when((i == 0) & (j == 0))
        def _():
            dgf_ref[...] = jnp.zeros(dgf_ref.shape, F32)
            dgp_ref[...] = jnp.zeros(dgp_ref.shape, F32)

        @pl.when(j == nf - 1)
        def _():
            nh, rh = _rms(h1_ref[...], D)
            dh, dgf = _rms_bwd(nh, rh, gf_ref[...], acc[...], D)
            dh1 = dh2_ref[...] + dh
            dh1_ref[...] = dh1
            dgf_ref[...] += dgf
            ny, ry = _rms(y1_ref[...], D)
            dy1, dgp = _rms_bwd(ny, ry, gp_ref[...], dh1, D)
            dy1_ref[...] = dy1.astype(BF)
            dgp_ref[...] += dgp

    tok = lambda n: pl.BlockSpec((tm, n), lambda i, j: (i, 0))
    col = pl.BlockSpec((tm, _FB), lambda i, j: (i, j))
    return _pcall(
        body, "ffn_bwd_x", (T // tm, nf),
        [tok(D), col, col, pl.BlockSpec((2 * _FB, D), lambda i, j: (j, 0)), pl.BlockSpec((_FB, D), lambda i, j: (j, 0)),
         tok(D), tok(D), tok(D), _full((1, D)), _full((1, D))],
        [pl.BlockSpec((tm, 2 * _FB), lambda i, j: (i, j)), tok(D), tok(D), _full((1, D)), _full((1, D))],
        [_sds((T, 2 * nf * _FB), BF), _sds((T, D)), _sds((T, D), BF), _sds((1, D)), _sds((1, D))],
        scratch=[pltpu.VMEM((tm, D), F32)],
    )(dy2, gs, us, wgu, wd, h1, y1, dh2, g_fpre, g_post)


def _ffn_bwd_w(z, ffs, dgu, dy2, tm):
    T = z.shape[0]
    nf = ffs.shape[1] // _FB
    nt = T // tm

    def body(z_ref, ff_ref, dgu_ref, dy2_ref, dwgu_ref, dwd_ref, agu, ad):
        i = pl.program_id(1)
        pgu = _dot_tn(dgu_ref[...], z_ref[...])
        pd = _dot_tn(ff_ref[...], dy2_ref[...])

        @pl.when(i == 0)
        def _():
            agu[...] = pgu
            ad[...] = pd

        @pl.when(i > 0)
        def _():
            agu[...] += pgu
            ad[...] += pd

        @pl.when(i == nt - 1)
        def _():
            dwgu_ref[...] = agu[...].astype(BF)
            dwd_ref[...] = ad[...].astype(BF)

    F = nf * _FB
    tok = lambda n: pl.BlockSpec((tm, n), lambda j, i: (i, 0))
    return _pcall(
        body, "ffn_bwd_w", (nf, nt),
        [tok(D), pl.BlockSpec((tm, _FB), lambda j, i: (i, j)), pl.BlockSpec((tm, 2 * _FB), lambda j, i: (i, j)), tok(D)],
        [pl.BlockSpec((2 * _FB, D), lambda j, i: (j, 0)), pl.BlockSpec((_FB, D), lambda j, i: (j, 0))],
        [_sds((2 * F, D), BF), _sds((F, D), BF)],
        scratch=[pltpu.VMEM((2 * _FB, D), F32), pltpu.VMEM((_FB, D), F32)],
    )(z, ffs, dgu, dy2)


def _out_bwd(dy1, mix, o_pad, w_out, g_on, tm):
    T = dy1.shape[0]
    W = N_HEADS * HB

    def body(dy1_ref, mix_ref, o_ref, w_ref, gon_ref, do_ref, dl_ref, dohg_ref, dw_ref, dgon_ref):
        i = pl.program_id(0)
        dy1v = dy1_ref[...]
        dmix = _dot_nt(dy1v, w_ref[...])
        pw = _dot_tn(mix_ref[...], dy1v)

        @pl.when(i == 0)
        def _():
            dw_ref[...] = pw
            dgon_ref[...] = jnp.zeros(dgon_ref.shape, F32)

        @pl.when(i > 0)
        def _():
            dw_ref[...] += pw

        for h in range(N_HEADS):
            sl = slice(h * HB, (h + 1) * HB)
            ov = o_ref[:, sl]
            n, r = _rms(ov, V_DIM)
            do, dg = _rms_bwd(n, r, gon_ref[:, sl], dmix[:, sl], V_DIM)
            dgon_ref[:, sl] += dg
            do_ref[:, sl] = do.astype(BF)
            dl_ref[:, sl] = jnp.broadcast_to(jnp.sum(do * ov, -1, keepdims=True), (tm, HB))
        dohg_ref[...] = dmix[:, W:MIX_W]

    return _pcall(body, "out_bwd", (T // tm,),
                  [_rows(tm, D), _rows(tm, MIX_W), _rows(tm, W), _full((MIX_W, D)), _full((1, W))],
                  [_rows(tm, W), _rows(tm, W), _rows(tm, G_W), _full((MIX_W, D)), _full((1, W))],
                  [_sds((T, W), BF), _sds((T, W)), _sds((T, G_W)), _sds((MIX_W, D)), _sds((1, W))],
                  )(dy1, mix, o_pad, w_out, g_on)


def _flash_bwd(q, k, v, do, lse, dl, tq, exchange=None):
    T = q.shape[0]
    nq = T // tq
    scale = QK_DIM ** -0.5
    hp = _AH
    W = hp * HB

    def body(k_ref, v_ref, q_ref, do_ref, lse_ref, dl_ref, dk_ref, dv_ref, dq_ref):
        j = pl.program_id(1)

        @pl.when(j == 0)
        def _():
            dq_ref[...] = jnp.zeros(dq_ref.shape, F32)

        def blk(i, carry, masked):
            sl = pl.ds(pl.multiple_of(i * tq, tq), tq)
            out = []
            for h in range(hp):
                ls = slice(h * HB, (h + 1) * HB)
                dk, dv = carry[h]
                kv, vv = k_ref[:, ls], v_ref[:, ls]
                qv, dov = q_ref[sl, ls], do_ref[sl, ls]
                s = _dot_nt(qv, kv) * scale
                if masked:
                    r = lax.broadcasted_iota(jnp.int32, (tq, tq), 0)
                    c = lax.broadcasted_iota(jnp.int32, (tq, tq), 1)
                    s = jnp.where(c <= r, s, NEG)
                p = jnp.exp(s - lse_ref[sl, h * HB:h * HB + 1])
                ds = p * (_dot_nt(dov, vv) - dl_ref[sl, h * HB:h * HB + 1]) * scale
                dq_ref[sl, ls] += _dot(ds, kv)
                out.append((dk + _dot_tn(ds, qv), dv + _dot_tn(p, dov)))
            return tuple(out)

        zero = jnp.zeros((tq, HB), F32)
        carry = blk(j, tuple((zero, zero) for _ in range(hp)), True)
        res = lax.fori_loop(j + 1, nq, lambda i, cr: blk(i, cr, False), carry)
        for h in range(hp):
            ls = slice(h * HB, (h + 1) * HB)
            dk_ref[:, ls] = res[h][0]
            dv_ref[:, ls] = res[h][1]

    tile = pl.BlockSpec((tq, W), lambda h, j: (j, h))
    whole = pl.BlockSpec((T, W), lambda h, j: (0, h))
    return _pcall(body, "flash_bwd", (N_HEADS // hp, nq), [tile, tile, whole, whole, whole, whole],
                  [tile, tile, whole], [_sds((T, N_HEADS * HB))] * 3, exchange=exchange)(k, v, q, do, lse, dl)


def _mla_prep_bwd(xp, tabs, dq, dk, dv, g_q, g_kv, w_uq, w_uk, w_uv, tm):
    T = xp.shape[0]
    W = N_HEADS * HB

    def body(xp_ref, ta_ref, tb1_ref, tb2_ref, dq_ref, dk_ref, dv_ref, gq_ref, gkv_ref, wuq_ref, wuk_ref, wuv_ref,
             dxp_ref, dwuq_ref, dwuk_ref, dwuv_ref, dgq_ref, dgkv_ref, dqp):
        i = pl.program_id(0)
        ta, tb1, tb2 = ta_ref[...], tb1_ref[...], tb2_ref[...]
        nq, rq = _rms(xp_ref[:, 0:Q_RANK], Q_RANK)
        nkv, rkv = _rms(xp_ref[:, Q_RANK:Q_RANK + KV_RANK], KV_RANK)
        dkr = jnp.zeros((tm, HB), F32)
        for h in range(N_HEADS):
            sl = slice(h * HB, (h + 1) * HB)
            dqp[:, sl] = _unrope(dq_ref[:, sl], ta, tb1, tb2).astype(BF)
            dkr = dkr + dk_ref[:, sl]
        dkr = pltpu.roll(_unrope(dkr, ta, tb1, tb2), HB - NOPE, 1)
        lane = lax.broadcasted_iota(jnp.int32, (tm, HB), 1)
        dxp_ref[:, Q_RANK + KV_RANK:MLA_IN] = jnp.where(lane < ROPE, dkr, 0.0)
        dqpv = dqp[...]
        dkv, dvv = dk_ref[...].astype(BF), dv_ref[...].astype(BF)
        nqs = (nq * gq_ref[...]).astype(BF)
        nkvs = (nkv * gkv_ref[...]).astype(BF)
        pq, pk, pv = _dot_tn(nqs, dqpv), _dot_tn(nkvs, dkv), _dot_tn(nkvs, dvv)
        dcq, dgq = _rms_bwd(nq, rq, gq_ref[...], _dot_nt(dqpv, wuq_ref[...]), Q_RANK)
        dckv, dgkv = _rms_bwd(nkv, rkv, gkv_ref[...], _dot_nt(dkv, wuk_ref[...]) + _dot_nt(dvv, wuv_ref[...]), KV_RANK)
        dxp_ref[:, 0:Q_RANK] = dcq
        dxp_ref[:, Q_RANK:Q_RANK + KV_RANK] = dckv

        @pl.when(i == 0)
        def _():
            dwuq_ref[...] = pq
            dwuk_ref[...] = pk
            dwuv_ref[...] = pv
            dgq_ref[...] = dgq
            dgkv_ref[...] = dgkv

        @pl.when(i > 0)
        def _():
            dwuq_ref[...] += pq
            dwuk_ref[...] += pk
            dwuv_ref[...] += pv
            dgq_ref[...] += dgq
            dgkv_ref[...] += dgkv

    tab = _rows(tm, HB)
    return _pcall(
        body, "mla_prep_bwd", (T // tm,),
        [_rows(tm, MLA_IN), tab, tab, tab, _rows(tm, W), _rows(tm, W), _rows(tm, W), _full((1, Q_RANK)),
         _full((1, KV_RANK)), _full((Q_RANK, W)), _full((KV_RANK, W)), _full((KV_RANK, W))],
        [_rows(tm, MLA_IN), _full((Q_RANK, W)), _full((KV_RANK, W)), _full((KV_RANK, W)), _full((1, Q_RANK)),
         _full((1, KV_RANK))],
        [_sds((T, MLA_IN)), _sds((Q_RANK, W)), _sds((KV_RANK, W)), _sds((KV_RANK, W)), _sds((1, Q_RANK)),
         _sds((1, KV_RANK))],
        scratch=[pltpu.VMEM((tm, W), BF)],
    )(xp, *tabs, dq, dk, dv, g_q, g_kv, w_uq, w_uk, w_uv)


def _hgrn_bwd(xp, o_raw, s_all, a_all, d_out, lb_logits, g_hn, exchange=None):
    T = xp.shape[0]
    tb = min(_TB, T)
    ncb = tb // CHUNK
    nb = T // tb
    hp = _HP
    W = hp * G_DIM

    def body(hq_ref, hf_ref, hi_ref, hg_ref, o_ref, sall_ref, aall_ref, dout_ref, lbl_ref, ghn_ref,
             dhq_ref, dhf_ref, dhi_ref, dhg_ref, dlbl_ref, dghn_ref, dst_ref, b_s, acc_lb, acc_g):
        t = pl.program_id(1)
        lb_all = _lower_bound(lbl_ref)

        @pl.when(t == 0)
        def _():
            dst_ref[...] = jnp.zeros(dst_ref.shape, F32)
            acc_lb[...] = jnp.zeros(acc_lb.shape, F32)
            acc_g[...] = jnp.zeros(acc_g.shape, F32)

        row = lax.broadcasted_iota(jnp.int32, (CHUNK, CHUNK), 0)
        col = lax.broadcasted_iota(jnp.int32, (CHUNK, CHUNK), 1)
        tri = (col <= row).astype(BF)
        tri_t = (col >= row).astype(BF)
        codes = _intra_codes(SUB_BWD)
        last = lax.broadcasted_iota(jnp.int32, (CHUNK, G_DIM), 0) == CHUNK - 1

        def chunk(cc, carry):
            c = ncb - 1 - cc
            sl = pl.ds(pl.multiple_of(c * CHUNK, CHUNK), CHUNK)
            for h in range(hp):
                ls = slice(h * G_DIM, (h + 1) * G_DIM)
                lb, ghn = lb_all[:, ls], ghn_ref[:, ls]
                hq, hg = hq_ref[sl, ls], hg_ref[sl, ls]
                q, k, f, lf, sig, sq = _gates(hq, hf_ref[sl, ls], lb)
                v = hi_ref[sl, ls]
                b2 = _tri_mm(tri, lf) * LOG2E
                b_s[h] = b2
                st = sall_ref[c, h]
                dstn = dst_ref[h]
                o = o_ref[sl, ls]
                dout = dout_ref[sl, ls]
                n, r = _rms(o, G_DIM)
                sg = _sigmoid(hg)
                dhg_ref[sl, ls] = dout * (n * ghn) * _dsilu(hg, sg)
                do, dg = _rms_bwd(n, r, ghn, dout * (hg * sg), G_DIM)
                acc_g[:, ls] += dg
                eb = jnp.exp2(b2)
                bl = b_s[h, CHUNK - 1:CHUNK, :]
                ebl = jnp.exp2(bl)
                ekd = jnp.exp2(bl - b2)
                kd = k * ekd
                a = aall_ref[c, h]
                dq_i, dk_i = _intra(q, k, b2, b_s.at[h], codes, _dot_nt(do, v))
                dhi_ref[sl, ls] = _dot_tn(a, do) + _dot_nt(kd, dstn)
                dk_state = _dot(v, dstn) * ekd
                dq = dq_i + _dot(do, st) * eb
                dk = dk_i + dk_state
                dbl = jnp.sum(k * dk_state, 0, keepdims=True) + ebl * jnp.sum(dstn * st, 0, keepdims=True)
                db = q * dq - k * dk + jnp.where(last, dbl, 0.0)
                df = _tri_mm(tri_t, db) / f - dk
                dhf_ref[sl, ls] = df * (1.0 - lb) * sig * (1.0 - sig)
                acc_lb[:, ls] += jnp.sum(df * (1.0 - sig), 0, keepdims=True)
                dhq_ref[sl, ls] = dq * _dsilu(hq, sq)
                dst_ref[h] = dstn * ebl + _dot_tn(do, q * eb)
            return carry

        lax.fori_loop(0, ncb, chunk, 0)

        @pl.when(t == nb - 1)
        def _():
            dl0 = acc_lb[...] * lb_all * (1.0 - lb_all)
            dlbl_ref[0:1, :] = dl0
            dlbl_ref[1:2, :] = -dl0
            dghn_ref[...] = acc_g[...]

    col_blk = lambda j: pl.BlockSpec((tb, W), lambda p, t: (nb - 1 - t, j * (G_HEADS // hp) + p))
    head = pl.BlockSpec((tb, W), lambda p, t: (nb - 1 - t, p))
    two = pl.BlockSpec((2, W), lambda p, t: (0, p))
    one = pl.BlockSpec((1, W), lambda p, t: (0, p))
    res = _pcall(
        body, "hgrn_bwd", (G_HEADS // hp, nb),
        [col_blk(0), col_blk(1), col_blk(2), col_blk(3), head,
         pl.BlockSpec((ncb, hp, G_DIM, G_DIM), lambda p, t: (nb - 1 - t, p, 0, 0)),
         pl.BlockSpec((ncb, hp, CHUNK, CHUNK), lambda p, t: (nb - 1 - t, p, 0, 0)), head, two, one],
        [head, head, head, head, two, one],
        [_sds((T, G_W))] * 4 + [_sds((2, G_W)), _sds((1, G_W))],
        scratch=[pltpu.VMEM((hp, G_DIM, G_DIM), F32), pltpu.VMEM((hp, CHUNK, G_DIM), F32),
                 pltpu.VMEM((1, W), F32), pltpu.VMEM((1, W), F32)], exchange=exchange,
    )(xp, xp, xp, xp, o_raw, s_all, a_all, d_out, lb_logits, g_hn)
    return res


def _in_bwd_x(x, dxp_m, dxp_h, dh1, w_in_al, g_pre, tm, exchange=None):
    T = x.shape[0]

    def body(x_ref, dm_ref, d0_ref, d1_ref, d2_ref, d3_ref, dh1_ref, w_ref, g_ref, dx_ref, dg_ref):
        i = pl.program_id(0)
        du = _dot_nt(dm_ref[...], w_ref[:, 0:MLA_IN])
        for j, d_ref in enumerate((d0_ref, d1_ref, d2_ref, d3_ref)):
            du = du + _dot_nt(d_ref[...], w_ref[:, MLA_IN + j * G_W:MLA_IN + (j + 1) * G_W])
        nx, r = _rms(x_ref[...], D)
        dx, dg = _rms_bwd(nx, r, g_ref[...], du, D)
        dx_ref[...] = dh1_ref[...] + dx

        @pl.when(i == 0)
        def _():
            dg_ref[...] = dg

        @pl.when(i > 0)
        def _():
            dg_ref[...] += dg

    return _pcall(body, "in_bwd_x", (T // tm,),
                  [_rows(tm, D), _rows(tm, MLA_IN)] + [_rows(tm, G_W)] * 4 + [_rows(tm, D), _full((D, XP_W)), _full((1, D))],
                  [_rows(tm, D), _full((1, D))], [_sds((T, D)), _sds((1, D))], exchange=exchange,
                  )(x, dxp_m, *dxp_h, dh1, w_in_al, g_pre)


def _aligned_col(c):
    return jnp.where(c < Q_RANK + KV_RANK + ROPE, c, c + (KR_PAD - ROPE))


def _align_w_in(g_in):
    tile = 384
    kr_end = Q_RANK + KV_RANK + ROPE

    def body(g_ref, o_ref, gp):
        gp[...] = jnp.zeros(gp.shape, BF)
        for j in range(N_DEV):
            gp[j, :, 0:IN_SH] = g_ref[j]
        r = lax.broadcasted_iota(jnp.int32, (tile, tile), 0)
        c = lax.broadcasted_iota(jnp.int32, (tile, tile), 1)
        for t in range(XP_W // tile):
            lo, hi = t * tile, (t + 1) * tile
            cols = [a if a < kr_end else a - (KR_PAD - ROPE) for a in (lo, hi - 1)]
            acc = jnp.zeros((D, tile), F32)
            for j in range(cols[0] // IN_SH, cols[-1] // IN_SH + 1):
                sel = (r < IN_SH) & (_aligned_col(j * IN_SH + r) == lo + c)
                acc = acc + jnp.dot(gp[j], sel.astype(BF), preferred_element_type=F32)
            o_ref[:, lo:hi] = acc.astype(BF)

    vm = pl.BlockSpec(memory_space=pltpu.VMEM)
    return pl.pallas_call(
        body, name="align_w_in", in_specs=[vm], out_specs=vm, out_shape=_sds((D, XP_W), BF),
        scratch_shapes=[pltpu.VMEM((N_DEV, D, tile), BF)],
        compiler_params=pltpu.CompilerParams(vmem_limit_bytes=_VMEM_LIMIT))(g_in)


def _in_bwd_w(name, u, dxp_m, dxp_h, tm, half, exchange=None):
    T = u.shape[0]
    nt = T // tm
    nr = D // 2
    win = 640

    def body(u_ref, dm_ref, d0_ref, d1_ref, d2_ref, d3_ref, o_ref, acc):
        i = pl.program_id(0)
        ut = u_ref[...].T
        parts = [(0, MLA_IN, dm_ref)] + [(MLA_IN + j * G_W, G_W, d) for j, d in enumerate((d0_ref, d1_ref, d2_ref, d3_ref))]

        @pl.when(i == 0)
        def _():
            for lo, n, d in parts:
                acc[:, lo:lo + n] = jnp.dot(ut, d[...].astype(BF), preferred_element_type=F32)

        @pl.when(i > 0)
        def _():
            for lo, n, d in parts:
                acc[:, lo:lo + n] += jnp.dot(ut, d[...].astype(BF), preferred_element_type=F32)

        @pl.when(i == nt - 1)
        def _():
            wide = 384
            r = lax.broadcasted_iota(jnp.int32, (win, wide), 0)
            c = lax.broadcasted_iota(jnp.int32, (win, wide), 1)
            kr_end = Q_RANK + KV_RANK + ROPE
            for j in range(N_DEV):
                first = j * IN_SH if j * IN_SH < kr_end else j * IN_SH + (KR_PAD - ROPE)
                lo = min(first // HB * HB, XP_W - win)
                sel = (c < IN_SH) & (_aligned_col(j * IN_SH + c) == lo + r)
                res = jnp.dot(acc[:, lo:lo + win].astype(BF), sel.astype(BF), preferred_element_type=F32)
                o_ref[j] = res[:, 0:IN_SH].astype(BF)

    return _pcall(body, name, (nt,),
                  [pl.BlockSpec((tm, nr), lambda i: (i, half)), _rows(tm, MLA_IN)] + [_rows(tm, G_W)] * 4,
                  [_full((N_DEV, nr, IN_SH))], [_sds((N_DEV, nr, IN_SH), BF)],
                  scratch=[pltpu.VMEM((nr, XP_W), F32)], exchange=exchange)(u, dxp_m, *dxp_h)


def _pad_heads(w, width, real):
    lead = w.shape[:-1]
    w = w.reshape(lead + (N_HEADS, real))
    w = jnp.pad(w, [(0, 0)] * len(lead) + [(0, 0), (0, width - real)])
    return w.reshape(lead + (N_HEADS * width,))


def _unpad_heads(w, width, real):
    lead = w.shape[:-1]
    return w.reshape(lead + (N_HEADS, width))[..., :real].reshape(lead + (N_HEADS * real,))


def _rope_tables(positions):
    half = ROPE // 2
    inv_freq = 1.0 / (ROPE_THETA ** (jnp.arange(0, ROPE, 2, dtype=F32) / ROPE))
    ang = positions.astype(F32)[:, None] * inv_freq
    cos, sin = jnp.cos(ang), jnp.sin(ang)
    T = positions.shape[0]
    z = lambda n: jnp.zeros((T, n), F32)
    ta = jnp.concatenate([jnp.ones((T, NOPE), F32), cos, cos, z(HB - QK_DIM)], 1)
    tb1 = jnp.concatenate([z(NOPE), -sin, z(half), z(HB - QK_DIM)], 1)
    tb2 = jnp.concatenate([z(NOPE), z(half), sin, z(HB - QK_DIM)], 1)
    return ta, tb1, tb2


def kernel(x, positions, attn_pre_norm, w_in, mla_q_norm, mla_w_uq, mla_kv_norm, mla_w_ukv, mla_out_norm, hgrn_lb_logits, hgrn_out_norm, w_out, attn_post_norm, ffn_pre_norm, w_gate, w_up, w_down, ffn_post_norm, loss_target, m_attn_pre_norm, m_w_in, m_mla_q_norm, m_mla_w_uq, m_mla_kv_norm, m_mla_w_ukv, m_mla_out_norm, m_hgrn_lb_logits, m_hgrn_out_norm, m_w_out, m_attn_post_norm, m_ffn_pre_norm, m_w_gate, m_w_up, m_w_down, m_ffn_post_norm, v_attn_pre_norm, v_w_in, v_mla_q_norm, v_mla_w_uq, v_mla_kv_norm, v_mla_w_ukv, v_mla_out_norm, v_hgrn_lb_logits, v_hgrn_out_norm, v_w_out, v_attn_post_norm, v_ffn_pre_norm, v_w_gate, v_w_up, v_w_down, v_ffn_post_norm):
    T = x.shape[1]
    tm = min(_TM, T)
    tq = min(_TQ, T)
    xs, tgt = x[0], loss_target[0]
    uq_sh = (Q_RANK // N_DEV, N_HEADS * QK_DIM)

    b_in, b_uq, b_out, b_gu, b_d = _cast_shards(
        w_in[0], mla_w_uq[0].reshape(uq_sh), w_out[0], w_gate[0].T, w_up[0].T, w_down[0])
    g_in, g_uq = _gather_two_level("ag_first", [b_in, b_uq])
    w_in_al = _align_w_in(g_in)
    w_uq_p = _pad_heads(g_uq.reshape(Q_RANK, N_HEADS * QK_DIM), HB, QK_DIM)
    w_ukv = mla_w_ukv[0].astype(BF)
    w_uk_p = _pad_heads(w_ukv[..., :NOPE].reshape(KV_RANK, N_HEADS * NOPE), HB, NOPE)
    w_uv_p = _pad_heads(w_ukv[..., NOPE:].reshape(KV_RANK, N_HEADS * V_DIM), HB, V_DIM)
    g_on_p = _pad_heads(mla_out_norm, HB, V_DIM)
    tabs = _rope_tables(positions[0])

    xp_m, xp_h, u = _fwd_in(xs, attn_pre_norm, w_in_al, tm)
    q_att, qs_att, k_att, v_att = _mla_prep(xp_m, tabs, mla_q_norm, mla_kv_norm, w_uq_p, w_uk_p, w_uv_p, tm)
    o_hgrn, o_raw, s_all, a_all, g_out, wd = _hgrn_fwd(xp_h, hgrn_lb_logits, hgrn_out_norm, ([GATHER, GATHER], [b_out, b_d]))
    wd = wd.reshape(N_DEV * FF_PAD, D)
    o_pad, lse, wgu = _flash_fwd(qs_att, k_att, v_att, tq, ([GATHER], [b_gu]))
    wgu = wgu.reshape(N_DEV * 2 * FF_PAD, D)
    w_out_full = g_out.reshape(D, D)
    w_out_mla = jnp.pad(w_out_full[:N_HEADS * V_DIM].reshape(N_HEADS, V_DIM, D), ((0, 0), (0, HB - V_DIM), (0, 0)))
    w_out_p = jnp.concatenate([w_out_mla.reshape(N_HEADS * HB, D), w_out_full[N_HEADS * V_DIM:]], 0)
    h1, y1, z, mix = _fwd_out(o_pad, o_hgrn, xs, g_on_p, w_out_p, attn_post_norm, ffn_pre_norm, tm)
    tmf = min(_TMF, T)
    gs, us, ffs, dh2, dy2, d_fpost, loss_row = _ffn_fwd(z, wgu, wd, h1, tgt, ffn_post_norm, tm, _FB // FF_PAD)

    dgu, dh1, dy1, d_fpre, d_post = _ffn_bwd_x(dy2, gs, us, wgu, wd, h1, y1, dh2, ffn_pre_norm, attn_post_norm, tm)
    dwgu, dwd = _ffn_bwd_w(z, ffs, dgu, dy2, tmf)
    do_pad, dl, d_ohg, dw_out_p, d_on_p = _out_bwd(dy1, mix, o_pad, w_out_p, g_on_p, tm)
    dw_out_mla = dw_out_p[:N_HEADS * HB].reshape(N_HEADS, HB, D)[:, :V_DIM].reshape(N_HEADS * V_DIM, D)
    dw_out = jnp.concatenate([dw_out_mla, dw_out_p[N_HEADS * HB:]], 0).reshape(N_DEV, D // N_DEV, D).astype(BF)
    dk_att, dv_att, dq_att, p_gu, p_d, p_out = _flash_bwd(
        q_att, k_att, v_att, do_pad, lse, dl, tq,
        ([SCATTER] * 3, [dwgu.reshape(N_DEV, 2 * FF_PAD, D), dwd.reshape(N_DEV, FF_PAD, D), dw_out]))
    dxp_m, dw_uq_p, dw_uk_p, dw_uv_p, d_gq, d_gkv = _mla_prep_bwd(
        xp_m, tabs, dq_att, dk_att, dv_att, mla_q_norm, mla_kv_norm, w_uq_p, w_uk_p, w_uv_p, tm)
    dw_uq = _unpad_heads(dw_uq_p, HB, QK_DIM).reshape((N_DEV,) + uq_sh).astype(BF)
    dw_ukv = jnp.concatenate([_unpad_heads(dw_uk_p, HB, NOPE).reshape(KV_RANK, N_HEADS, NOPE),
                              _unpad_heads(dw_uv_p, HB, V_DIM).reshape(KV_RANK, N_HEADS, V_DIM)], -1)
    *dxp_h, d_lbl, d_ghn, p_uq, dw_ukv_all = _hgrn_bwd(
        xp_h, o_raw, s_all, a_all, d_ohg, hgrn_lb_logits, hgrn_out_norm,
        ([SCATTER, GATHER], [dw_uq, dw_ukv.reshape(KV_RANK, N_HEADS * HB)]))
    dw_in_a, = _in_bwd_w("in_bwd_w_a", u, dxp_m, dxp_h, tm, 0)
    dw_in_b, p_in_a = _in_bwd_w("in_bwd_w_b", u, dxp_m, dxp_h, tm, 1, ([SCATTER], [dw_in_a]))
    grad_x, d_pre, p_in_b = _in_bwd_x(xs, dxp_m, dxp_h, dh1, w_in_al, attn_pre_norm, tm, ([SCATTER], [dw_in_b]))
    d_on = _unpad_heads(d_on_p, HB, V_DIM)

    ukv2 = lambda a: a.reshape(KV_RANK, N_HEADS * HB)
    vecs = [d_pre, d_gq, d_gkv, d_on, d_lbl, d_ghn, d_post, d_fpre, d_fpost, loss_row]
    small_w = [attn_pre_norm, mla_q_norm, mla_kv_norm, ukv2(mla_w_ukv), mla_out_norm, hgrn_lb_logits, hgrn_out_norm,
               attn_post_norm, ffn_pre_norm, ffn_post_norm]
    small_m = [m_attn_pre_norm, m_mla_q_norm, m_mla_kv_norm, ukv2(m_mla_w_ukv), m_mla_out_norm, m_hgrn_lb_logits,
               m_hgrn_out_norm, m_attn_post_norm, m_ffn_pre_norm, m_ffn_post_norm]
    small_v = [v_attn_pre_norm, v_mla_q_norm, v_mla_kv_norm, ukv2(v_mla_w_ukv), v_mla_out_norm, v_hgrn_lb_logits,
               v_hgrn_out_norm, v_attn_post_norm, v_ffn_pre_norm, v_ffn_post_norm]
    rall = _final_exchange(vecs)
    s_g, s_d, s_m, s_v, loss_all = _small_adam(rall, dw_ukv_all, 3, small_w, small_m, small_v)
    r_in = _shard_adam("adam_w_in", [p_in_a, p_in_b], w_in[0], m_w_in[0], v_w_in[0], 256)
    r_uq = _shard_adam("adam_w_uq", [p_uq], mla_w_uq[0].reshape(uq_sh), m_mla_w_uq[0].reshape(uq_sh),
                       v_mla_w_uq[0].reshape(uq_sh), uq_sh[0])
    r_out = _shard_adam("adam_w_out", [p_out], w_out[0], m_w_out[0], v_w_out[0], D // N_DEV)
    r_g, r_u = _gate_up_adam(p_gu, (w_gate[0].T, w_up[0].T), (m_w_gate[0].T, m_w_up[0].T),
                             (v_w_gate[0].T, v_w_up[0].T))
    r_g, r_u = [a.T for a in r_g], [a.T for a in r_u]
    r_d = _shard_adam("adam_w_down", [p_d], w_down[0], m_w_down[0], v_w_down[0], FF_SH // 2)

    loss = loss_all[0, 0]

    def assemble(big, small):
        b_in, b_uq, b_out, b_g, b_u, b_d = big
        return [small[0], b_in[None], small[1], b_uq.reshape(mla_w_uq.shape), small[2],
                small[3].reshape(mla_w_ukv.shape), small[4], small[5], small[6], b_out[None], small[7], small[8],
                b_g[None], b_u[None], b_d[None], small[9]]

    outs = [loss, grad_x[None]]
    for idx, small in enumerate((s_g, s_d, s_m, s_v)):
        outs += assemble([r[idx] for r in (r_in, r_uq, r_out, r_g, r_u, r_d)], small)
    return tuple(outs)
```

```python
import jax
import jax.numpy as jnp
from jax import lax
from jax.experimental import pallas as pl
from jax.experimental.pallas import tpu as pltpu

BF = jnp.bfloat16
F32 = jnp.float32
MESH = pl.DeviceIdType.MESH

N_DEV = 8
D = 1024
EPS = 1e-6
LOG2E = 1.4426950408889634
ROPE_THETA = 10000.0
N_HEADS = 8
HB = 128
NOPE = 64
ROPE = 32
V_DIM = 64
QK_DIM = NOPE + ROPE
Q_RANK = 384
KV_RANK = 128
KR_PAD = 128
MLA_IN = Q_RANK + KV_RANK + KR_PAD
G_HEADS = 4
G_DIM = 128
G_W = G_HEADS * G_DIM
CHUNK = 64
SUB_FWD = 16
SUB_BWD = 16
XP_W = MLA_IN + 4 * G_W
IN_SH = 324
IN_W = N_DEV * IN_SH
FF_SH = 352
FF_PAD = 384
MIX_W = N_HEADS * HB + G_W

ADAM_LR = 0.001
ADAM_B1 = 0.9
ADAM_B2 = 0.999
ADAM_EPS = 1e-08
ADAM_WD = 0.01
ADAM_STEP = 10

_TM = 512
_TMF = 1024
_TQ = 512
_AH = 2
_AH_FWD = 4
_FB = 768
_TB = 1024
_HP = 4
_VMEM_LIMIT = 56 * 1024 * 1024
NEG = -1e30


def _dot(a, b):
    return jnp.dot(a.astype(BF), b.astype(BF), preferred_element_type=F32)


def _dot_nt(a, b):
    return lax.dot_general(a.astype(BF), b.astype(BF), (((1,), (1,)), ((), ())), preferred_element_type=F32)


def _dot_tn(a, b):
    return lax.dot_general(a.astype(BF), b.astype(BF), (((0,), (0,)), ((), ())), preferred_element_type=F32)


def _sigmoid(x):
    return 1.0 / (1.0 + jnp.exp(-x))


def _rms(x, n):
    r = lax.rsqrt(jnp.sum(x * x, -1, keepdims=True) * (1.0 / n) + EPS)
    return x * r, r


def _rms_bwd(nx, r, g, dy, n):
    dg = jnp.sum(dy * nx, 0, keepdims=True)
    dn = dy * g
    dx = r * (dn - nx * (jnp.sum(dn * nx, -1, keepdims=True) * (1.0 / n)))
    return dx, dg


def _adamw(w, g, m, v):
    m2 = ADAM_B1 * m + (1.0 - ADAM_B1) * g
    v2 = ADAM_B2 * v + (1.0 - ADAM_B2) * (g * g)
    m_hat = m2 / (1.0 - ADAM_B1 ** ADAM_STEP)
    v_hat = v2 / (1.0 - ADAM_B2 ** ADAM_STEP)
    delta = -ADAM_LR * (m_hat / (jnp.sqrt(v_hat) + ADAM_EPS) + ADAM_WD * w)
    return delta, m2, v2


def _pcall(body, name, grid, in_specs, out_specs, out_shape, scratch=(), exchange=None):
    scratch = list(scratch)
    extra = ()
    if exchange is not None:
        kinds, extra = exchange
        in_specs, out_specs, out_shape = list(in_specs), list(out_specs), list(out_shape)
        n_in, n_out, n_scr, n_x = len(in_specs), len(out_specs), len(scratch), len(extra)
        inner = body

        def body(*refs):
            ins, rest = refs[:n_in], refs[n_in:]
            x_src, rest = rest[:n_x], rest[n_x:]
            outs, rest = rest[:n_out], rest[n_out:]
            x_dst, rest = rest[:n_x], rest[n_x:]
            ex = _Exchange(kinds, x_src, x_dst, *rest[n_scr:])
            first = pl.program_id(0) == 0
            last = pl.program_id(0) == grid[0] - 1
            for a in range(1, len(grid)):
                first = first & (pl.program_id(a) == 0)
                last = last & (pl.program_id(a) == grid[a] - 1)
            pl.when(first)(ex.start)
            inner(*ins, *outs, *rest[:n_scr])
            pl.when(last)(ex.wait)

        in_specs += [_HBM] * n_x
        out_specs += [_HBM] * n_x
        out_shape += _exchange_shapes(kinds, extra)
        scratch += _exchange_sems(n_x)
    call = pl.pallas_call(
        body, name=name, grid=grid, in_specs=in_specs, out_specs=out_specs, out_shape=out_shape,
        scratch_shapes=scratch,
        compiler_params=pltpu.CompilerParams(
            dimension_semantics=("arbitrary",) * len(grid), vmem_limit_bytes=_VMEM_LIMIT))
    return lambda *operands: call(*operands, *extra)


def _full(shape):
    return pl.BlockSpec(shape, lambda *_: (0,) * len(shape))


def _rows(tm, n):
    return pl.BlockSpec((tm, n), lambda i, *_: (i, 0))


def _sds(shape, dtype=F32):
    return jax.ShapeDtypeStruct(shape, dtype)


def _peer(k, x, y, c):
    px = 1 - x if (k >> 2) & 1 else x
    py = 1 - y if (k >> 1) & 1 else y
    pc = 1 - c if k & 1 else c
    return px, py, pc


GATHER, SCATTER = "gather", "scatter"


class _Exchange:
    def __init__(self, kinds, srcs, dsts, send_sems, recv_sems, loc_sems):
        self.kinds, self.srcs, self.dsts = kinds, srcs, dsts
        self.send_sems, self.recv_sems, self.loc_sems = send_sems, recv_sems, loc_sems
        self.x, self.y, self.c = lax.axis_index("x"), lax.axis_index("y"), lax.axis_index("c")
        self.me = 4 * self.x + 2 * self.y + self.c

    def _src(self, w, slot):
        return self.srcs[w] if self.kinds[w] == GATHER else self.srcs[w].at[slot]

    def _dst(self, w, slot):
        return self.dsts[w].at[slot]

    def _copy(self, w, k, outgoing):
        px, py, pc = _peer(k, self.x, self.y, self.c)
        pid = 4 * px + 2 * py + pc
        return pltpu.make_async_remote_copy(
            src_ref=self._src(w, pid if outgoing else self.me),
            dst_ref=self._dst(w, self.me if outgoing else pid),
            send_sem=self.send_sems.at[w, k - 1], recv_sem=self.recv_sems.at[w, k - 1],
            device_id=(px, py, pc), device_id_type=MESH)

    def _local(self, w):
        return pltpu.make_async_copy(self._src(w, self.me), self._dst(w, self.me), self.loc_sems.at[w])

    def start(self):
        for w in range(len(self.srcs)):
            self._local(w).start()
            for k in range(1, N_DEV):
                self._copy(w, k, True).start()

    def wait(self):
        for w in range(len(self.srcs)):
            self._local(w).wait()
            for k in range(1, N_DEV):
                self._copy(w, k, False).wait_recv()
        for w in range(len(self.srcs)):
            for k in range(1, N_DEV):
                self._copy(w, k, True).wait_send()


def _exchange_sems(n_w):
    return [pltpu.SemaphoreType.DMA((n_w, N_DEV - 1)), pltpu.SemaphoreType.DMA((n_w, N_DEV - 1)),
            pltpu.SemaphoreType.DMA((n_w,))]


def _exchange_shapes(kinds, srcs):
    return [_sds(((N_DEV,) if kd == GATHER else ()) + tuple(s.shape), s.dtype) for kd, s in zip(kinds, srcs)]


_HBM = pl.BlockSpec(memory_space=pl.ANY)


def _cast_shards(w_in, w_uq, w_out, w_gate_t, w_up_t, w_down):
    shapes = [(D, IN_SH), (Q_RANK // N_DEV, N_HEADS * QK_DIM), (D // N_DEV, D), (2 * FF_PAD, D), (FF_PAD, D)]

    def body(win, wuq, wout, wg, wu, wd, sin_, suq, sout, sgu, sd):
        sin_[...] = win[...].astype(BF)
        suq[...] = wuq[...].astype(BF)
        sout[...] = wout[...].astype(BF)
        sgu[...] = jnp.zeros(sgu.shape, BF)
        sgu[0:FF_SH, :] = wg[...].astype(BF)
        sgu[FF_PAD:FF_PAD + FF_SH, :] = wu[...].astype(BF)
        sd[...] = jnp.zeros(sd.shape, BF)
        sd[0:FF_SH, :] = wd[...].astype(BF)

    vm = pl.BlockSpec(memory_space=pltpu.VMEM)
    return pl.pallas_call(
        body, name="cast_shards", in_specs=[vm] * 6, out_specs=[vm] * 5,
        out_shape=[_sds(s, BF) for s in shapes],
        compiler_params=pltpu.CompilerParams(vmem_limit_bytes=_VMEM_LIMIT),
    )(w_in, w_uq, w_out, w_gate_t, w_up_t, w_down)


def _gather_two_level(name, srcs):
    n_w = len(srcs)

    def body(*refs):
        src, dst = refs[:n_w], refs[n_w:2 * n_w]
        send_sems, recv_sems, loc_sems = refs[2 * n_w:]
        x, y, c = lax.axis_index("x"), lax.axis_index("y"), lax.axis_index("c")
        me, sibling = (x, y, c), (x, y, 1 - c)
        chips = [(1 - x, y), (x, 1 - y), (1 - x, 1 - y)]
        slot = lambda p: 4 * p[0] + 2 * p[1] + p[2]

        def copy(w, k, block, to, own=False):
            return pltpu.make_async_remote_copy(
                src_ref=src[w] if own else dst[w].at[slot(block)], dst_ref=dst[w].at[slot(block)],
                send_sem=send_sems.at[w, k], recv_sem=recv_sems.at[w, k], device_id=to, device_id_type=MESH)

        local = [pltpu.make_async_copy(src[w], dst[w].at[slot(me)], loc_sems.at[w]) for w in range(n_w)]
        first, passed = [], []
        for w in range(n_w):
            local[w].start()
            first.append(copy(w, 0, me, sibling, own=True))
            first += [copy(w, 1 + j, me, (*chip, c), own=True) for j, chip in enumerate(chips)]
        for cp in first:
            cp.start()
        for w in range(n_w):
            for j, chip in enumerate(chips):
                copy(w, 1 + j, (*chip, c), me).wait_recv()
                passed.append(copy(w, 4 + j, (*chip, c), sibling))
                passed[-1].start()
        for w in range(n_w):
            copy(w, 0, sibling, me).wait_recv()
            for j, chip in enumerate(chips):
                copy(w, 4 + j, (*chip, 1 - c), me).wait_recv()
        for cp in first + passed:
            cp.wait_send()
        for w in range(n_w):
            local[w].wait()

    return pl.pallas_call(
        body, name=name, in_specs=[_HBM] * n_w, out_specs=[_HBM] * n_w,
        out_shape=_exchange_shapes([GATHER] * n_w, srcs), scratch_shapes=_exchange_sems(n_w))(*srcs)


def _row_offsets(arrays):
    offs, rows = [], 0
    for a in arrays:
        offs.append(rows)
        rows += a.shape[0]
    return offs, -(-rows // 8) * 8


def _final_exchange(vecs):
    n_p = len(vecs)
    offs, rows = _row_offsets(vecs)

    def body(*refs):
        g_refs = refs[:n_p]
        rall, pk, send_sems, recv_sems, loc_sem = refs[n_p:]
        x, y, c = lax.axis_index("x"), lax.axis_index("y"), lax.axis_index("c")
        me = 4 * x + 2 * y + c
        pk[...] = jnp.zeros(pk.shape, F32)
        for p in range(n_p):
            r, n = g_refs[p].shape
            pk[offs[p]:offs[p] + r, 0:n] = g_refs[p][...]

        def remote(k):
            return pltpu.make_async_remote_copy(
                src_ref=pk, dst_ref=rall.at[me], send_sem=send_sems.at[k - 1], recv_sem=recv_sems.at[k - 1],
                device_id=_peer(k, x, y, c), device_id_type=MESH)

        def arrival(k):
            px, py, pc = _peer(k, x, y, c)
            return pltpu.make_async_remote_copy(
                src_ref=pk, dst_ref=rall.at[4 * px + 2 * py + pc], send_sem=send_sems.at[k - 1],
                recv_sem=recv_sems.at[k - 1], device_id=(px, py, pc), device_id_type=MESH)

        local = pltpu.make_async_copy(pk, rall.at[me], loc_sem)
        local.start()
        for k in range(1, N_DEV):
            remote(k).start()
        local.wait()
        for k in range(1, N_DEV):
            arrival(k).wait_recv()
        for k in range(1, N_DEV):
            remote(k).wait_send()

    vm = pl.BlockSpec(memory_space=pltpu.VMEM)
    return pl.pallas_call(
        body, name="final_exchange", in_specs=[vm] * n_p, out_specs=vm, out_shape=_sds((N_DEV, rows, D)),
        scratch_shapes=[pltpu.VMEM((rows, D), F32),
                        pltpu.SemaphoreType.DMA((N_DEV - 1,)), pltpu.SemaphoreType.DMA((N_DEV - 1,)),
                        pltpu.SemaphoreType.DMA],
    )(*vecs)


def _small_adam(rall, big_parts, big, ws, ms, vs):
    n_p = len(ws)
    packed = [w for p, w in enumerate(ws) if p != big] + [jax.ShapeDtypeStruct((1, HB), F32)]
    offs, _ = _row_offsets(packed)
    offs = offs[:big] + [None] + offs[big:]

    def total(ref, sl):
        g = ref[(0,) + sl]
        for j in range(1, N_DEV):
            g = g + ref[(j,) + sl]
        return g

    def body(*refs):
        rall_ref, big_ref = refs[:2]
        w_refs, m_refs, v_refs = refs[2:2 + n_p], refs[2 + n_p:2 + 2 * n_p], refs[2 + 2 * n_p:2 + 3 * n_p]
        outs = refs[2 + 3 * n_p:]
        for p in range(n_p):
            r, n = w_refs[p].shape
            if p == big:
                g = total(big_ref, (slice(0, r), slice(0, n)))
            else:
                g = total(rall_ref, (slice(offs[p], offs[p] + r), slice(0, n)))
            delta, m2, v2 = _adamw(w_refs[p][...], g, m_refs[p][...], v_refs[p][...])
            outs[p][...] = g
            outs[n_p + p][...] = delta
            outs[2 * n_p + p][...] = m2
            outs[3 * n_p + p][...] = v2
        outs[4 * n_p][...] = total(rall_ref, (slice(offs[n_p], offs[n_p] + 1), slice(0, HB)))

    vm = pl.BlockSpec(memory_space=pltpu.VMEM)
    res = pl.pallas_call(
        body, name="small_adam", in_specs=[vm] * (2 + 3 * n_p), out_specs=[vm] * (4 * n_p + 1),
        out_shape=[_sds(w.shape) for w in ws] * 4 + [_sds((1, HB))],
        compiler_params=pltpu.CompilerParams(vmem_limit_bytes=_VMEM_LIMIT),
    )(rall, big_parts, *ws, *ms, *vs)
    return res[:n_p], res[n_p:2 * n_p], res[2 * n_p:3 * n_p], res[3 * n_p:4 * n_p], res[4 * n_p]


def _device_sum(p_ref):
    g = p_ref[0].astype(F32)
    for j in range(1, N_DEV):
        g = g + p_ref[j].astype(F32)
    return g


def _shard_adam(name, parts, w, m, v, tr):
    a0, b0 = w.shape
    n_p = len(parts)
    b = parts[0].shape[2]
    first = [0]
    for p in parts:
        first.append(first[-1] + p.shape[1] // tr)

    def body(*refs):
        p_refs = refs[:n_p]
        w_ref, m_ref, v_ref, g_out, d_out, m_out, v_out = refs[n_p:]
        i = pl.program_id(0)
        g = _device_sum(p_refs[0])
        for k in range(1, n_p):
            g = jnp.where(i >= first[k], _device_sum(p_refs[k]), g)
        g = g[:, 0:b0]
        delta, m2, v2 = _adamw(w_ref[...], g, m_ref[...], v_ref[...])
        g_out[...] = g
        d_out[...] = delta
        m_out[...] = m2
        v_out[...] = v2

    def part_spec(k):
        last = first[k + 1] - first[k] - 1
        return pl.BlockSpec((N_DEV, tr, b), lambda i: (0, jnp.minimum(jnp.maximum(i - first[k], 0), last), 0))

    blk = pl.BlockSpec((tr, b0), lambda i: (i, 0))
    return _pcall(
        body, name, (a0 // tr,), [part_spec(k) for k in range(n_p)] + [blk, blk, blk],
        [blk] * 4, [_sds((a0, b0))] * 4)(*parts, w, m, v)


def _gate_up_adam(parts, ws, ms, vs):
    tc = 256

    def body(p_ref, wg, wu, mg, mu, vg, vu, *outs):
        g = _device_sum(p_ref)
        for k, (w_ref, m_ref, v_ref) in enumerate(((wg, mg, vg), (wu, mu, vu))):
            gk = g[k * FF_PAD:k * FF_PAD + FF_SH]
            delta, m2, v2 = _adamw(w_ref[...], gk, m_ref[...], v_ref[...])
            for o, val in zip(outs[4 * k:4 * k + 4], (gk, delta, m2, v2)):
                o[...] = val

    blk = pl.BlockSpec((FF_SH, tc), lambda i: (0, i))
    res = _pcall(
        body, "adam_w_gate_up", (D // tc,), [pl.BlockSpec((N_DEV, 2 * FF_PAD, tc), lambda i: (0, 0, i))] + [blk] * 6,
        [blk] * 8, [_sds((FF_SH, D))] * 8)(parts, *ws, *ms, *vs)
    return res[:4], res[4:]


def _fwd_in(x, g_pre, w_in_al, tm):
    T = x.shape[0]

    def body(x_ref, g_ref, w_ref, xm_ref, xh_ref, u_ref):
        nx, _ = _rms(x_ref[...], D)
        u = (nx * g_ref[...]).astype(BF)
        u_ref[...] = u
        xm_ref[...] = jnp.dot(u, w_ref[:, 0:MLA_IN], preferred_element_type=F32)
        xh_ref[...] = jnp.dot(u, w_ref[:, MLA_IN:XP_W], preferred_element_type=F32)

    return _pcall(body, "fwd_in", (T // tm,),
                  [_rows(tm, D), _full((1, D)), _full((D, XP_W))],
                  [_rows(tm, MLA_IN), _rows(tm, 4 * G_W), _rows(tm, D)],
                  [_sds((T, MLA_IN)), _sds((T, 4 * G_W)), _sds((T, D), BF)])(x, g_pre, w_in_al)


def _rope(blk, ta, tb1, tb2):
    return blk * ta + pltpu.roll(blk, HB - ROPE // 2, 1) * tb1 + pltpu.roll(blk, ROPE // 2, 1) * tb2


def _unrope(d, ta, tb1, tb2):
    return d * ta + pltpu.roll(d * tb1, ROPE // 2, 1) + pltpu.roll(d * tb2, HB - ROPE // 2, 1)


def _mla_prep(xp, tabs, g_q, g_kv, w_uq, w_uk, w_uv, tm):
    T = xp.shape[0]
    W = N_HEADS * HB

    def body(xp_ref, ta_ref, tb1_ref, tb2_ref, gq_ref, gkv_ref, wuq_ref, wuk_ref, wuv_ref, q_ref, qs_ref, k_ref, v_ref):
        ta, tb1, tb2 = ta_ref[...], tb1_ref[...], tb2_ref[...]
        nq, _ = _rms(xp_ref[:, 0:Q_RANK], Q_RANK)
        nkv, _ = _rms(xp_ref[:, Q_RANK:Q_RANK + KV_RANK], KV_RANK)
        nkv = (nkv * gkv_ref[...]).astype(BF)
        qpre = _dot(nq * gq_ref[...], wuq_ref[...])
        kpre = jnp.dot(nkv, wuk_ref[...], preferred_element_type=F32)
        v = jnp.dot(nkv, wuv_ref[...], preferred_element_type=F32)
        lane = lax.broadcasted_iota(jnp.int32, (tm, W), 1)
        v_ref[...] = jnp.where((lane & (HB - 1)) == V_DIM, 1.0, v).astype(BF)
        kr = _rope(pltpu.roll(xp_ref[:, Q_RANK + KV_RANK:MLA_IN], NOPE, 1), ta, tb1, tb2)
        for h in range(N_HEADS):
            sl = slice(h * HB, (h + 1) * HB)
            qr = _rope(qpre[:, sl], ta, tb1, tb2)
            q_ref[:, sl] = qr.astype(BF)
            qs_ref[:, sl] = (qr * (QK_DIM ** -0.5 * LOG2E)).astype(BF)
            k_ref[:, sl] = (kpre[:, sl] + kr).astype(BF)

    tab = _rows(tm, HB)
    return _pcall(body, "mla_prep", (T // tm,),
                  [_rows(tm, MLA_IN), tab, tab, tab, _full((1, Q_RANK)), _full((1, KV_RANK)),
                   _full((Q_RANK, W)), _full((KV_RANK, W)), _full((KV_RANK, W))],
                  [_rows(tm, W)] * 4, [_sds((T, W), BF)] * 4)(xp, *tabs, g_q, g_kv, w_uq, w_uk, w_uv)


def _flash_fwd(q, k, v, tq, exchange=None):
    T = q.shape[0]
    hp = _AH_FWD
    W = hp * HB

    def body(q_ref, k_ref, v_ref, o_ref, lse_ref):
        i = pl.program_id(1)

        def blk(j, carry, masked):
            st = pl.multiple_of(j * tq, tq)
            out = []
            for h in range(hp):
                ls = slice(h * HB, (h + 1) * HB)
                m, acc = carry[h]
                s = _dot_nt(q_ref[:, ls], k_ref[pl.ds(st, tq), ls])
                if masked:
                    r = lax.broadcasted_iota(jnp.int32, (tq, tq), 0)
                    c = lax.broadcasted_iota(jnp.int32, (tq, tq), 1)
                    s = jnp.where(c <= r, s, NEG)
                m2 = jnp.maximum(m, jnp.max(s, -1, keepdims=True))
                p = jnp.exp2(s - m2)
                out.append((m2, jnp.exp2(m - m2) * acc + _dot(p, v_ref[pl.ds(st, tq), ls])))
            return tuple(out)

        init = tuple((jnp.full((tq, 1), NEG, F32), jnp.zeros((tq, HB), F32)) for _ in range(hp))
        carry = lax.fori_loop(0, i, lambda j, cr: blk(j, cr, False), init)
        res = blk(i, carry, True)
        lane = lax.broadcasted_iota(jnp.int32, (tq, HB), 1)
        for h in range(hp):
            ls = slice(h * HB, (h + 1) * HB)
            m, acc = res[h]
            l = acc[:, V_DIM:V_DIM + 1]
            o_ref[:, ls] = jnp.where(lane < V_DIM, acc / l, 0.0)
            lse_ref[:, ls] = jnp.broadcast_to(m * (1.0 / LOG2E) + jnp.log(l), (tq, HB))

    qs = pl.BlockSpec((tq, W), lambda h, i: (i, h))
    kvs = pl.BlockSpec((T, W), lambda h, i: (0, h))
    return _pcall(body, "flash_fwd", (N_HEADS // hp, T // tq), [qs, kvs, kvs], [qs, qs],
                  [_sds((T, N_HEADS * HB))] * 2, exchange=exchange)(q, k, v)


def _gates(hq, hf, lb):
    sig = _sigmoid(hf)
    f = lb + (1.0 - lb) * sig
    sq = _sigmoid(hq)
    return hq * sq, 1.0 - f, f, jnp.log(f), sig, sq


def _lower_bound(lbl_ref):
    l0, l1 = lbl_ref[0:1, :], lbl_ref[1:2, :]
    mx = jnp.maximum(l0, l1)
    e0, e1 = jnp.exp(l0 - mx), jnp.exp(l1 - mx)
    return e0 / (e0 + e1)


def _split3(x):
    hi = x.astype(BF)
    r1 = x - hi.astype(F32)
    mid = r1.astype(BF)
    lo = (r1 - mid.astype(F32)).astype(BF)
    return hi, mid, lo


def _tri_mm(tri, x):
    hi, mid, lo = _split3(x)
    mm = lambda t: jnp.dot(tri, t, preferred_element_type=F32)
    return mm(hi) + mm(mid) + mm(lo)


def _intra_codes(sub):
    row = lax.broadcasted_iota(jnp.int32, (CHUNK, CHUNK), 0)
    col = lax.broadcasted_iota(jnp.int32, (CHUNK, CHUNK), 1)
    return sub, row, col


def _intra(q, k, b2, b_s, codes, da=None):
    grad = da is not None
    pow2 = (lambda x: jnp.exp2(jnp.minimum(x, 0.0))) if grad else jnp.exp2
    sub, row, col = codes
    a = jnp.zeros((CHUNK, CHUNK), F32)
    dq = jnp.zeros((CHUNK, G_DIM), F32)
    dk = jnp.zeros((CHUNK, G_DIM), F32)
    for i in range(1, CHUNK // sub):
        b0 = b_s[sub * i - 1:sub * i, :]
        eq, ek = pow2(b2 - b0), pow2(b0 - b2)
        mask = ((row // sub) == i) & (col < sub * i)
        if grad:
            dai = jnp.where(mask, da, 0.0)
            dq = dq + _dot(dai, k * ek) * eq
            dk = dk + _dot_tn(dai, q * eq) * ek
        else:
            a = jnp.where(mask, _dot_nt(q * eq, k * ek), a)
    for d in range(sub):
        ksh = pltpu.roll(k, d, 0) if d else k
        bsh = pltpu.roll(b2, d, 0) if d else b2
        e = pow2(b2 - bsh)
        mask = (col == row - d) & ((row & (sub - 1)) >= d)
        if grad:
            g = jnp.sum(jnp.where(mask, da, 0.0), -1, keepdims=True) * e
            dq = dq + g * ksh
            cb = g * q
            dk = dk + (pltpu.roll(cb, CHUNK - d, 0) if d else cb)
        else:
            a = jnp.where(mask, jnp.sum(q * ksh * e, -1, keepdims=True), a)
    return (dq, dk) if grad else a


def _hgrn_fwd(xp, lb_logits, g_hn, exchange=None):
    T = xp.shape[0]
    tb = min(_TB, T)
    ncb = tb // CHUNK
    hp = _HP
    W = hp * G_DIM

    def body(hq_ref, hf_ref, hi_ref, hg_ref, lbl_ref, ghn_ref, out_ref, oraw_ref, sall_ref, aall_ref, st_ref, b_s):
        lb_all = _lower_bound(lbl_ref)

        @pl.when(pl.program_id(1) == 0)
        def _():
            st_ref[...] = jnp.zeros(st_ref.shape, F32)

        row = lax.broadcasted_iota(jnp.int32, (CHUNK, CHUNK), 0)
        col = lax.broadcasted_iota(jnp.int32, (CHUNK, CHUNK), 1)
        tri = (col <= row).astype(BF)
        codes = _intra_codes(SUB_FWD)

        def chunk(c, carry):
            sl = pl.ds(pl.multiple_of(c * CHUNK, CHUNK), CHUNK)
            for h in range(hp):
                ls = slice(h * G_DIM, (h + 1) * G_DIM)
                q, k, _, lf, _, _ = _gates(hq_ref[sl, ls], hf_ref[sl, ls], lb_all[:, ls])
                v = hi_ref[sl, ls]
                b2 = _tri_mm(tri, lf) * LOG2E
                b_s[h] = b2
                st = st_ref[h]
                sall_ref[c, h] = st
                a = _intra(q, k, b2, b_s.at[h], codes)
                aall_ref[c, h] = a
                o = _dot_nt(q * jnp.exp2(b2), st) + _dot(a, v)
                bl = b_s[h, CHUNK - 1:CHUNK, :]
                st_ref[h] = st * jnp.exp2(bl) + _dot_tn(v, k * jnp.exp2(bl - b2))
                oraw_ref[sl, ls] = o
                n, _ = _rms(o, G_DIM)
                hg = hg_ref[sl, ls]
                out_ref[sl, ls] = n * ghn_ref[:, ls] * (hg * _sigmoid(hg))
            return carry

        lax.fori_loop(0, ncb, chunk, 0)

    col_blk = lambda j: pl.BlockSpec((tb, W), lambda p, t: (t, j * (G_HEADS // hp) + p))
    head = pl.BlockSpec((tb, W), lambda p, t: (t, p))
    return _pcall(
        body, "hgrn_fwd", (G_HEADS // hp, T // tb),
        [col_blk(0), col_blk(1), col_blk(2), col_blk(3),
         pl.BlockSpec((2, W), lambda p, t: (0, p)), pl.BlockSpec((1, W), lambda p, t: (0, p))],
        [head, head, pl.BlockSpec((ncb, hp, G_DIM, G_DIM), lambda p, t: (t, p, 0, 0)),
         pl.BlockSpec((ncb, hp, CHUNK, CHUNK), lambda p, t: (t, p, 0, 0))],
        [_sds((T, G_W)), _sds((T, G_W)), _sds((T // CHUNK, G_HEADS, G_DIM, G_DIM)),
         _sds((T // CHUNK, G_HEADS, CHUNK, CHUNK))],
        scratch=[pltpu.VMEM((hp, G_DIM, G_DIM), F32), pltpu.VMEM((hp, CHUNK, G_DIM), F32)], exchange=exchange,
    )(xp, xp, xp, xp, lb_logits, g_hn)


def _fwd_out(o_pad, o_hgrn, x, g_on, w_out, g_post, g_fpre, tm):
    T = x.shape[0]

    def body(o_ref, oh_ref, x_ref, gon_ref, w_ref, gpost_ref, gfpre_ref, h1_ref, y1_ref, z_ref, mix_ref):
        for h in range(N_HEADS):
            sl = slice(h * HB, (h + 1) * HB)
            n, _ = _rms(o_ref[:, sl], V_DIM)
            mix_ref[:, sl] = (n * gon_ref[:, sl]).astype(BF)
        mix_ref[:, N_HEADS * HB:MIX_W] = oh_ref[...].astype(BF)
        y1 = jnp.dot(mix_ref[...], w_ref[...], preferred_element_type=F32)
        y1_ref[...] = y1
        ny, _ = _rms(y1, D)
        h1 = x_ref[...] + ny * gpost_ref[...]
        h1_ref[...] = h1
        nh, _ = _rms(h1, D)
        z_ref[...] = (nh * gfpre_ref[...]).astype(BF)

    return _pcall(body, "fwd_out", (T // tm,),
                  [_rows(tm, N_HEADS * HB), _rows(tm, G_W), _rows(tm, D), _full((1, N_HEADS * HB)),
                   _full((MIX_W, D)), _full((1, D)), _full((1, D))],
                  [_rows(tm, D), _rows(tm, D), _rows(tm, D), _rows(tm, MIX_W)],
                  [_sds((T, D)), _sds((T, D)), _sds((T, D), BF), _sds((T, MIX_W), BF)],
                  )(o_pad, o_hgrn, x, g_on, w_out, g_post, g_fpre)


def _ffn_fwd(z, wgu, wd, h1, tgt, g_fpost, tm, nd):
    T = z.shape[0]
    fb = nd * FF_PAD
    nf = wd.shape[0] // fb

    def body(z_ref, wgu_ref, wd_ref, h1_ref, t_ref, gp_ref,
             as_ref, bs_ref, ff_ref, dh2_ref, dy2_ref, dgp_ref, loss_ref, acc):
        i, j = pl.program_id(0), pl.program_id(1)
        gu = _dot_nt(z_ref[...], wgu_ref[...])
        piece = lambda n: gu[:, n * FF_PAD:(n + 1) * FF_PAD]
        g = piece(0) if nd == 1 else jnp.concatenate([piece(2 * n) for n in range(nd)], 1)
        u = piece(1) if nd == 1 else jnp.concatenate([piece(2 * n + 1) for n in range(nd)], 1)
        s = _sigmoid(g)
        b = g * s
        ff = (b * u).astype(BF)
        as_ref[...] = (u * _dsilu(g, s)).astype(BF)
        bs_ref[...] = b.astype(BF)
        ff_ref[...] = ff
        part = jnp.dot(ff, wd_ref[...], preferred_element_type=F32)

        @pl.when(j == 0)
        def _():
            acc[...] = part

        @pl.when(j > 0)
        def _():
            acc[...] += part

        @pl.when((i == 0) & (j == 0))
        def _():
            dgp_ref[...] = jnp.zeros(dgp_ref.shape, F32)
            loss_ref[...] = jnp.zeros(loss_ref.shape, F32)

        @pl.when(j == nf - 1)
        def _():
            ny, r = _rms(acc[...], D)
            err = h1_ref[...] + ny * gp_ref[...] - t_ref[...]
            loss_ref[...] += 0.5 * jnp.sum(jnp.sum(err * err, -1, keepdims=True) * (1.0 / D), 0, keepdims=True)
            dh2 = err * (1.0 / D)
            dh2_ref[...] = dh2
            dy2, dgp = _rms_bwd(ny, r, gp_ref[...], dh2, D)
            dy2_ref[...] = dy2.astype(BF)
            dgp_ref[...] += dgp

    tok = lambda n: pl.BlockSpec((tm, n), lambda i, j: (i, 0))
    col = pl.BlockSpec((tm, fb), lambda i, j: (i, j))
    return _pcall(
        body, "ffn_fwd", (T // tm, nf),
        [tok(D), pl.BlockSpec((2 * fb, D), lambda i, j: (j, 0)), pl.BlockSpec((fb, D), lambda i, j: (j, 0)),
         tok(D), tok(D), _full((1, D))],
        [col, col, col, tok(D), tok(D), _full((1, D)), _full((1, HB))],
        [_sds((T, nf * fb), BF)] * 3 + [_sds((T, D)), _sds((T, D), BF), _sds((1, D)), _sds((1, HB))],
        scratch=[pltpu.VMEM((tm, D), F32)],
    )(z, wgu, wd, h1, tgt, g_fpost)


def _dsilu(x, s):
    return s * (1.0 + x * (1.0 - s))


def _ffn_bwd_x(dy2, gs, us, wgu, wd, h1, y1, dh2, g_fpre, g_post, tm):
    T = dy2.shape[0]
    nf = wd.shape[0] // _FB

    def body(dy2_ref, gs_ref, us_ref, wgu_ref, wd_ref, h1_ref, y1_ref, dh2_ref, gf_ref, gp_ref,
             dgu_ref, dh1_ref, dy1_ref, dgf_ref, dgp_ref, acc):
        i, j = pl.program_id(0), pl.program_id(1)
        dff = _dot_nt(dy2_ref[...], wd_ref[...])
        dg = (dff * gs_ref[...].astype(F32)).astype(BF)
        du = (dff * us_ref[...].astype(F32)).astype(BF)
        dgu = jnp.concatenate([dg[:, 0:FF_PAD], du[:, 0:FF_PAD], dg[:, FF_PAD:_FB], du[:, FF_PAD:_FB]], 1)
        dgu_ref[...] = dgu
        part = jnp.dot(dgu, wgu_ref[...], preferred_element_type=F32)

        @pl.when(j == 0)
        def _():
            acc[...] = part

        @pl.when(j > 0)
        def _():
            acc[...] += part

        @pl.when((i == 0) & (j == 0))
        def _():
            dgf_ref[...] = jnp.zeros(dgf_ref.shape, F32)
            dgp_ref[...] = jnp.zeros(dgp_ref.shape, F32)

        @pl.when(j == nf - 1)
        def _():
            nh, rh = _rms(h1_ref[...], D)
            dh, dgf = _rms_bwd(nh, rh, gf_ref[...], acc[...], D)
            dh1 = dh2_ref[...] + dh
            dh1_ref[...] = dh1
            dgf_ref[...] += dgf
            ny, ry = _rms(y1_ref[...], D)
            dy1, dgp = _rms_bwd(ny, ry, gp_ref[...], dh1, D)
            dy1_ref[...] = dy1.astype(BF)
            dgp_ref[...] += dgp

    tok = lambda n: pl.BlockSpec((tm, n), lambda i, j: (i, 0))
    col = pl.BlockSpec((tm, _FB), lambda i, j: (i, j))
    return _pcall(
        body, "ffn_bwd_x", (T // tm, nf),
        [tok(D), col, col, pl.BlockSpec((2 * _FB, D), lambda i, j: (j, 0)), pl.BlockSpec((_FB, D), lambda i, j: (j, 0)),
         tok(D), tok(D), tok(D), _full((1, D)), _full((1, D))],
        [pl.BlockSpec((tm, 2 * _FB), lambda i, j: (i, j)), tok(D), tok(D), _full((1, D)), _full((1, D))],
        [_sds((T, 2 * nf * _FB), BF), _sds((T, D)), _sds((T, D), BF), _sds((1, D)), _sds((1, D))],
        scratch=[pltpu.VMEM((tm, D), F32)],
    )(dy2, gs, us, wgu, wd, h1, y1, dh2, g_fpre, g_post)


def _ffn_bwd_w(z, ffs, dgu, dy2, tm):
    T = z.shape[0]
    nf = ffs.shape[1] // _FB
    nt = T // tm

    def body(z_ref, ff_ref, dgu_ref, dy2_ref, dwgu_ref, dwd_ref, agu, ad):
        i = pl.program_id(1)
        pgu = _dot_tn(dgu_ref[...], z_ref[...])
        pd = _dot_tn(ff_ref[...], dy2_ref[...])

        @pl.when(i == 0)
        def _():
            agu[...] = pgu
            ad[...] = pd

        @pl.when(i > 0)
        def _():
            agu[...] += pgu
            ad[...] += pd

        @pl.when(i == nt - 1)
        def _():
            dwgu_ref[...] = agu[...].astype(BF)
            dwd_ref[...] = ad[...].astype(BF)

    F = nf * _FB
    tok = lambda n: pl.BlockSpec((tm, n), lambda j, i: (i, 0))
    return _pcall(
        body, "ffn_bwd_w", (nf, nt),
        [tok(D), pl.BlockSpec((tm, _FB), lambda j, i: (i, j)), pl.BlockSpec((tm, 2 * _FB), lambda j, i: (i, j)), tok(D)],
        [pl.BlockSpec((2 * _FB, D), lambda j, i: (j, 0)), pl.BlockSpec((_FB, D), lambda j, i: (j, 0))],
        [_sds((2 * F, D), BF), _sds((F, D), BF)],
        scratch=[pltpu.VMEM((2 * _FB, D), F32), pltpu.VMEM((_FB, D), F32)],
    )(z, ffs, dgu, dy2)


def _out_bwd(dy1, mix, o_pad, w_out, g_on, tm):
    T = dy1.shape[0]
    W = N_HEADS * HB

    def body(dy1_ref, mix_ref, o_ref, w_ref, gon_ref, do_ref, dl_ref, dohg_ref, dw_ref, dgon_ref):
        i = pl.program_id(0)
        dy1v = dy1_ref[...]
        dmix = _dot_nt(dy1v, w_ref[...])
        pw = _dot_tn(mix_ref[...], dy1v)

        @pl.when(i == 0)
        def _():
            dw_ref[...] = pw
            dgon_ref[...] = jnp.zeros(dgon_ref.shape, F32)

        @pl.when(i > 0)
        def _():
            dw_ref[...] += pw

        for h in range(N_HEADS):
            sl = slice(h * HB, (h + 1) * HB)
            ov = o_ref[:, sl]
            n, r = _rms(ov, V_DIM)
            do, dg = _rms_bwd(n, r, gon_ref[:, sl], dmix[:, sl], V_DIM)
            dgon_ref[:, sl] += dg
            do_ref[:, sl] = do.astype(BF)
            dl_ref[:, sl] = jnp.broadcast_to(jnp.sum(do * ov, -1, keepdims=True), (tm, HB))
        dohg_ref[...] = dmix[:, W:MIX_W]

    return _pcall(body, "out_bwd", (T // tm,),
                  [_rows(tm, D), _rows(tm, MIX_W), _rows(tm, W), _full((MIX_W, D)), _full((1, W))],
                  [_rows(tm, W), _rows(tm, W), _rows(tm, G_W), _full((MIX_W, D)), _full((1, W))],
                  [_sds((T, W), BF), _sds((T, W)), _sds((T, G_W)), _sds((MIX_W, D)), _sds((1, W))],
                  )(dy1, mix, o_pad, w_out, g_on)


def _flash_bwd(q, k, v, do, lse, dl, tq, exchange=None):
    T = q.shape[0]
    nq = T // tq
    scale = QK_DIM ** -0.5
    hp = _AH
    W = hp * HB

    def body(k_ref, v_ref, q_ref, do_ref, lse_ref, dl_ref, dk_ref, dv_ref, dq_ref):
        j = pl.program_id(1)

        @pl.when(j == 0)
        def _():
            dq_ref[...] = jnp.zeros(dq_ref.shape, F32)

        def blk(i, carry, masked):
            sl = pl.ds(pl.multiple_of(i * tq, tq), tq)
            out = []
            for h in range(hp):
                ls = slice(h * HB, (h + 1) * HB)
                dk, dv = carry[h]
                kv, vv = k_ref[:, ls], v_ref[:, ls]
                qv, dov = q_ref[sl, ls], do_ref[sl, ls]
                s = _dot_nt(qv, kv) * scale
                if masked:
                    r = lax.broadcasted_iota(jnp.int32, (tq, tq), 0)
                    c = lax.broadcasted_iota(jnp.int32, (tq, tq), 1)
                    s = jnp.where(c <= r, s, NEG)
                p = jnp.exp(s - lse_ref[sl, h * HB:h * HB + 1])
                ds = p * (_dot_nt(dov, vv) - dl_ref[sl, h * HB:h * HB + 1]) * scale
                dq_ref[sl, ls] += _dot(ds, kv)
                out.append((dk + _dot_tn(ds, qv), dv + _dot_tn(p, dov)))
            return tuple(out)

        zero = jnp.zeros((tq, HB), F32)
        carry = blk(j, tuple((zero, zero) for _ in range(hp)), True)
        res = lax.fori_loop(j + 1, nq, lambda i, cr: blk(i, cr, False), carry)
        for h in range(hp):
            ls = slice(h * HB, (h + 1) * HB)
            dk_ref[:, ls] = res[h][0]
            dv_ref[:, ls] = res[h][1]

    tile = pl.BlockSpec((tq, W), lambda h, j: (j, h))
    whole = pl.BlockSpec((T, W), lambda h, j: (0, h))
    return _pcall(body, "flash_bwd", (N_HEADS // hp, nq), [tile, tile, whole, whole, whole, whole],
                  [tile, tile, whole], [_sds((T, N_HEADS * HB))] * 3, exchange=exchange)(k, v, q, do, lse, dl)


def _mla_prep_bwd(xp, tabs, dq, dk, dv, g_q, g_kv, w_uq, w_uk, w_uv, tm):
    T = xp.shape[0]
    W = N_HEADS * HB

    def body(xp_ref, ta_ref, tb1_ref, tb2_ref, dq_ref, dk_ref, dv_ref, gq_ref, gkv_ref, wuq_ref, wuk_ref, wuv_ref,
             dxp_ref, dwuq_ref, dwuk_ref, dwuv_ref, dgq_ref, dgkv_ref, dqp):
        i = pl.program_id(0)
        ta, tb1, tb2 = ta_ref[...], tb1_ref[...], tb2_ref[...]
        nq, rq = _rms(xp_ref[:, 0:Q_RANK], Q_RANK)
        nkv, rkv = _rms(xp_ref[:, Q_RANK:Q_RANK + KV_RANK], KV_RANK)
        dkr = jnp.zeros((tm, HB), F32)
        for h in range(N_HEADS):
            sl = slice(h * HB, (h + 1) * HB)
            dqp[:, sl] = _unrope(dq_ref[:, sl], ta, tb1, tb2).astype(BF)
            dkr = dkr + dk_ref[:, sl]
        dkr = pltpu.roll(_unrope(dkr, ta, tb1, tb2), HB - NOPE, 1)
        lane = lax.broadcasted_iota(jnp.int32, (tm, HB), 1)
        dxp_ref[:, Q_RANK + KV_RANK:MLA_IN] = jnp.where(lane < ROPE, dkr, 0.0)
        dqpv = dqp[...]
        dkv, dvv = dk_ref[...].astype(BF), dv_ref[...].astype(BF)
        nqs = (nq * gq_ref[...]).astype(BF)
        nkvs = (nkv * gkv_ref[...]).astype(BF)
        pq, pk, pv = _dot_tn(nqs, dqpv), _dot_tn(nkvs, dkv), _dot_tn(nkvs, dvv)
        dcq, dgq = _rms_bwd(nq, rq, gq_ref[...], _dot_nt(dqpv, wuq_ref[...]), Q_RANK)
        dckv, dgkv = _rms_bwd(nkv, rkv, gkv_ref[...], _dot_nt(dkv, wuk_ref[...]) + _dot_nt(dvv, wuv_ref[...]), KV_RANK)
        dxp_ref[:, 0:Q_RANK] = dcq
        dxp_ref[:, Q_RANK:Q_RANK + KV_RANK] = dckv

        @pl.when(i == 0)
        def _():
            dwuq_ref[...] = pq
            dwuk_ref[...] = pk
            dwuv_ref[...] = pv
            dgq_ref[...] = dgq
            dgkv_ref[...] = dgkv

        @pl.when(i > 0)
        def _():
            dwuq_ref[...] += pq
            dwuk_ref[...] += pk
            dwuv_ref[...] += pv
            dgq_ref[...] += dgq
            dgkv_ref[...] += dgkv

    tab = _rows(tm, HB)
    return _pcall(
        body, "mla_prep_bwd", (T // tm,),
        [_rows(tm, MLA_IN), tab, tab, tab, _rows(tm, W), _rows(tm, W), _rows(tm, W), _full((1, Q_RANK)),
         _full((1, KV_RANK)), _full((Q_RANK, W)), _full((KV_RANK, W)), _full((KV_RANK, W))],
        [_rows(tm, MLA_IN), _full((Q_RANK, W)), _full((KV_RANK, W)), _full((KV_RANK, W)), _full((1, Q_RANK)),
         _full((1, KV_RANK))],
        [_sds((T, MLA_IN)), _sds((Q_RANK, W)), _sds((KV_RANK, W)), _sds((KV_RANK, W)), _sds((1, Q_RANK)),
         _sds((1, KV_RANK))],
        scratch=[pltpu.VMEM((tm, W), BF)],
    )(xp, *tabs, dq, dk, dv, g_q, g_kv, w_uq, w_uk, w_uv)


def _hgrn_bwd(xp, o_raw, s_all, a_all, d_out, lb_logits, g_hn, exchange=None):
    T = xp.shape[0]
    tb = min(_TB, T)
    ncb = tb // CHUNK
    nb = T // tb
    hp = _HP
    W = hp * G_DIM

    def body(hq_ref, hf_ref, hi_ref, hg_ref, o_ref, sall_ref, aall_ref, dout_ref, lbl_ref, ghn_ref,
             dhq_ref, dhf_ref, dhi_ref, dhg_ref, dlbl_ref, dghn_ref, dst_ref, b_s, acc_lb, acc_g):
        t = pl.program_id(1)
        lb_all = _lower_bound(lbl_ref)

        @pl.when(t == 0)
        def _():
            dst_ref[...] = jnp.zeros(dst_ref.shape, F32)
            acc_lb[...] = jnp.zeros(acc_lb.shape, F32)
            acc_g[...] = jnp.zeros(acc_g.shape, F32)

        row = lax.broadcasted_iota(jnp.int32, (CHUNK, CHUNK), 0)
        col = lax.broadcasted_iota(jnp.int32, (CHUNK, CHUNK), 1)
        tri = (col <= row).astype(BF)
        tri_t = (col >= row).astype(BF)
        codes = _intra_codes(SUB_BWD)
        last = lax.broadcasted_iota(jnp.int32, (CHUNK, G_DIM), 0) == CHUNK - 1

        def chunk(cc, carry):
            c = ncb - 1 - cc
            sl = pl.ds(pl.multiple_of(c * CHUNK, CHUNK), CHUNK)
            for h in range(hp):
                ls = slice(h * G_DIM, (h + 1) * G_DIM)
                lb, ghn = lb_all[:, ls], ghn_ref[:, ls]
                hq, hg = hq_ref[sl, ls], hg_ref[sl, ls]
                q, k, f, lf, sig, sq = _gates(hq, hf_ref[sl, ls], lb)
                v = hi_ref[sl, ls]
                b2 = _tri_mm(tri, lf) * LOG2E
                b_s[h] = b2
                st = sall_ref[c, h]
                dstn = dst_ref[h]
                o = o_ref[sl, ls]
                dout = dout_ref[sl, ls]
                n, r = _rms(o, G_DIM)
                sg = _sigmoid(hg)
                dhg_ref[sl, ls] = dout * (n * ghn) * _dsilu(hg, sg)
                do, dg = _rms_bwd(n, r, ghn, dout * (hg * sg), G_DIM)
                acc_g[:, ls] += dg
                eb = jnp.exp2(b2)
                bl = b_s[h, CHUNK - 1:CHUNK, :]
                ebl = jnp.exp2(bl)
                ekd = jnp.exp2(bl - b2)
                kd = k * ekd
                a = aall_ref[c, h]
                dq_i, dk_i = _intra(q, k, b2, b_s.at[h], codes, _dot_nt(do, v))
                dhi_ref[sl, ls] = _dot_tn(a, do) + _dot_nt(kd, dstn)
                dk_state = _dot(v, dstn) * ekd
                dq = dq_i + _dot(do, st) * eb
                dk = dk_i + dk_state
                dbl = jnp.sum(k * dk_state, 0, keepdims=True) + ebl * jnp.sum(dstn * st, 0, keepdims=True)
                db = q * dq - k * dk + jnp.where(last, dbl, 0.0)
                df = _tri_mm(tri_t, db) / f - dk
                dhf_ref[sl, ls] = df * (1.0 - lb) * sig * (1.0 - sig)
                acc_lb[:, ls] += jnp.sum(df * (1.0 - sig), 0, keepdims=True)
                dhq_ref[sl, ls] = dq * _dsilu(hq, sq)
                dst_ref[h] = dstn * ebl + _dot_tn(do, q * eb)
            return carry

        lax.fori_loop(0, ncb, chunk, 0)

        @pl.when(t == nb - 1)
        def _():
            dl0 = acc_lb[...] * lb_all * (1.0 - lb_all)
            dlbl_ref[0:1, :] = dl0
            dlbl_ref[1:2, :] = -dl0
            dghn_ref[...] = acc_g[...]

    col_blk = lambda j: pl.BlockSpec((tb, W), lambda p, t: (nb - 1 - t, j * (G_HEADS // hp) + p))
    head = pl.BlockSpec((tb, W), lambda p, t: (nb - 1 - t, p))
    two = pl.BlockSpec((2, W), lambda p, t: (0, p))
    one = pl.BlockSpec((1, W), lambda p, t: (0, p))
    res = _pcall(
        body, "hgrn_bwd", (G_HEADS // hp, nb),
        [col_blk(0), col_blk(1), col_blk(2), col_blk(3), head,
         pl.BlockSpec((ncb, hp, G_DIM, G_DIM), lambda p, t: (nb - 1 - t, p, 0, 0)),
         pl.BlockSpec((ncb, hp, CHUNK, CHUNK), lambda p, t: (nb - 1 - t, p, 0, 0)), head, two, one],
        [head, head, head, head, two, one],
        [_sds((T, G_W))] * 4 + [_sds((2, G_W)), _sds((1, G_W))],
        scratch=[pltpu.VMEM((hp, G_DIM, G_DIM), F32), pltpu.VMEM((hp, CHUNK, G_DIM), F32),
                 pltpu.VMEM((1, W), F32), pltpu.VMEM((1, W), F32)], exchange=exchange,
    )(xp, xp, xp, xp, o_raw, s_all, a_all, d_out, lb_logits, g_hn)
    return res


def _in_bwd_x(x, dxp_m, dxp_h, dh1, w_in_al, g_pre, tm, exchange=None):
    T = x.shape[0]

    def body(x_ref, dm_ref, d0_ref, d1_ref, d2_ref, d3_ref, dh1_ref, w_ref, g_ref, dx_ref, dg_ref):
        i = pl.program_id(0)
        du = _dot_nt(dm_ref[...], w_ref[:, 0:MLA_IN])
        for j, d_ref in enumerate((d0_ref, d1_ref, d2_ref, d3_ref)):
            du = du + _dot_nt(d_ref[...], w_ref[:, MLA_IN + j * G_W:MLA_IN + (j + 1) * G_W])
        nx, r = _rms(x_ref[...], D)
        dx, dg = _rms_bwd(nx, r, g_ref[...], du, D)
        dx_ref[...] = dh1_ref[...] + dx

        @pl.when(i == 0)
        def _():
            dg_ref[...] = dg

        @pl.when(i > 0)
        def _():
            dg_ref[...] += dg

    return _pcall(body, "in_bwd_x", (T // tm,),
                  [_rows(tm, D), _rows(tm, MLA_IN)] + [_rows(tm, G_W)] * 4 + [_rows(tm, D), _full((D, XP_W)), _full((1, D))],
                  [_rows(tm, D), _full((1, D))], [_sds((T, D)), _sds((1, D))], exchange=exchange,
                  )(x, dxp_m, *dxp_h, dh1, w_in_al, g_pre)


def _aligned_col(c):
    return jnp.where(c < Q_RANK + KV_RANK + ROPE, c, c + (KR_PAD - ROPE))


def _align_w_in(g_in):
    tile = 384
    kr_end = Q_RANK + KV_RANK + ROPE

    def body(g_ref, o_ref, gp):
        gp[...] = jnp.zeros(gp.shape, BF)
        for j in range(N_DEV):
            gp[j, :, 0:IN_SH] = g_ref[j]
        r = lax.broadcasted_iota(jnp.int32, (tile, tile), 0)
        c = lax.broadcasted_iota(jnp.int32, (tile, tile), 1)
        for t in range(XP_W // tile):
            lo, hi = t * tile, (t + 1) * tile
            cols = [a if a < kr_end else a - (KR_PAD - ROPE) for a in (lo, hi - 1)]
            acc = jnp.zeros((D, tile), F32)
            for j in range(cols[0] // IN_SH, cols[-1] // IN_SH + 1):
                sel = (r < IN_SH) & (_aligned_col(j * IN_SH + r) == lo + c)
                acc = acc + jnp.dot(gp[j], sel.astype(BF), preferred_element_type=F32)
            o_ref[:, lo:hi] = acc.astype(BF)

    vm = pl.BlockSpec(memory_space=pltpu.VMEM)
    return pl.pallas_call(
        body, name="align_w_in", in_specs=[vm], out_specs=vm, out_shape=_sds((D, XP_W), BF),
        scratch_shapes=[pltpu.VMEM((N_DEV, D, tile), BF)],
        compiler_params=pltpu.CompilerParams(vmem_limit_bytes=_VMEM_LIMIT))(g_in)


def _in_bwd_w(name, u, dxp_m, dxp_h, tm, half, exchange=None):
    T = u.shape[0]
    nt = T // tm
    nr = D // 2
    win = 640

    def body(u_ref, dm_ref, d0_ref, d1_ref, d2_ref, d3_ref, o_ref, acc):
        i = pl.program_id(0)
        ut = u_ref[...].T
        parts = [(0, MLA_IN, dm_ref)] + [(MLA_IN + j * G_W, G_W, d) for j, d in enumerate((d0_ref, d1_ref, d2_ref, d3_ref))]

        @pl.when(i == 0)
        def _():
            for lo, n, d in parts:
                acc[:, lo:lo + n] = jnp.dot(ut, d[...].astype(BF), preferred_element_type=F32)

        @pl.when(i > 0)
        def _():
            for lo, n, d in parts:
                acc[:, lo:lo + n] += jnp.dot(ut, d[...].astype(BF), preferred_element_type=F32)

        @pl.when(i == nt - 1)
        def _():
            wide = 384
            r = lax.broadcasted_iota(jnp.int32, (win, wide), 0)
            c = lax.broadcasted_iota(jnp.int32, (win, wide), 1)
            kr_end = Q_RANK + KV_RANK + ROPE
            for j in range(N_DEV):
                first = j * IN_SH if j * IN_SH < kr_end else j * IN_SH + (KR_PAD - ROPE)
                lo = min(first // HB * HB, XP_W - win)
                sel = (c < IN_SH) & (_aligned_col(j * IN_SH + c) == lo + r)
                res = jnp.dot(acc[:, lo:lo + win].astype(BF), sel.astype(BF), preferred_element_type=F32)
                o_ref[j] = res[:, 0:IN_SH].astype(BF)

    return _pcall(body, name, (nt,),
                  [pl.BlockSpec((tm, nr), lambda i: (i, half)), _rows(tm, MLA_IN)] + [_rows(tm, G_W)] * 4,
                  [_full((N_DEV, nr, IN_SH))], [_sds((N_DEV, nr, IN_SH), BF)],
                  scratch=[pltpu.VMEM((nr, XP_W), F32)], exchange=exchange)(u, dxp_m, *dxp_h)


def _pad_heads(w, width, real):
    lead = w.shape[:-1]
    w = w.reshape(lead + (N_HEADS, real))
    w = jnp.pad(w, [(0, 0)] * len(lead) + [(0, 0), (0, width - real)])
    return w.reshape(lead + (N_HEADS * width,))


def _unpad_heads(w, width, real):
    lead = w.shape[:-1]
    return w.reshape(lead + (N_HEADS, width))[..., :real].reshape(lead + (N_HEADS * real,))


def _rope_tables(positions):
    half = ROPE // 2
    inv_freq = 1.0 / (ROPE_THETA ** (jnp.arange(0, ROPE, 2, dtype=F32) / ROPE))
    ang = positions.astype(F32)[:, None] * inv_freq
    cos, sin = jnp.cos(ang), jnp.sin(ang)
    T = positions.shape[0]
    z = lambda n: jnp.zeros((T, n), F32)
    ta = jnp.concatenate([jnp.ones((T, NOPE), F32), cos, cos, z(HB - QK_DIM)], 1)
    tb1 = jnp.concatenate([z(NOPE), -sin, z(half), z(HB - QK_DIM)], 1)
    tb2 = jnp.concatenate([z(NOPE), z(half), sin, z(HB - QK_DIM)], 1)
    return ta, tb1, tb2


def kernel(x, positions, attn_pre_norm, w_in, mla_q_norm, mla_w_uq, mla_kv_norm, mla_w_ukv, mla_out_norm, hgrn_lb_logits, hgrn_out_norm, w_out, attn_post_norm, ffn_pre_norm, w_gate, w_up, w_down, ffn_post_norm, loss_target, m_attn_pre_norm, m_w_in, m_mla_q_norm, m_mla_w_uq, m_mla_kv_norm, m_mla_w_ukv, m_mla_out_norm, m_hgrn_lb_logits, m_hgrn_out_norm, m_w_out, m_attn_post_norm, m_ffn_pre_norm, m_w_gate, m_w_up, m_w_down, m_ffn_post_norm, v_attn_pre_norm, v_w_in, v_mla_q_norm, v_mla_w_uq, v_mla_kv_norm, v_mla_w_ukv, v_mla_out_norm, v_hgrn_lb_logits, v_hgrn_out_norm, v_w_out, v_attn_post_norm, v_ffn_pre_norm, v_w_gate, v_w_up, v_w_down, v_ffn_post_norm):
    T = x.shape[1]
    tm = min(_TM, T)
    tq = min(_TQ, T)
    xs, tgt = x[0], loss_target[0]
    uq_sh = (Q_RANK // N_DEV, N_HEADS * QK_DIM)

    b_in, b_uq, b_out, b_gu, b_d = _cast_shards(
        w_in[0], mla_w_uq[0].reshape(uq_sh), w_out[0], w_gate[0].T, w_up[0].T, w_down[0])
    g_in, g_uq = _gather_two_level("ag_first", [b_in, b_uq])
    w_in_al = _align_w_in(g_in)
    w_uq_p = _pad_heads(g_uq.reshape(Q_RANK, N_HEADS * QK_DIM), HB, QK_DIM)
    w_ukv = mla_w_ukv[0].astype(BF)
    w_uk_p = _pad_heads(w_ukv[..., :NOPE].reshape(KV_RANK, N_HEADS * NOPE), HB, NOPE)
    w_uv_p = _pad_heads(w_ukv[..., NOPE:].reshape(KV_RANK, N_HEADS * V_DIM), HB, V_DIM)
    g_on_p = _pad_heads(mla_out_norm, HB, V_DIM)
    tabs = _rope_tables(positions[0])

    xp_m, xp_h, u = _fwd_in(xs, attn_pre_norm, w_in_al, tm)
    q_att, qs_att, k_att, v_att = _mla_prep(xp_m, tabs, mla_q_norm, mla_kv_norm, w_uq_p, w_uk_p, w_uv_p, tm)
    o_hgrn, o_raw, s_all, a_all, g_out, wd = _hgrn_fwd(xp_h, hgrn_lb_logits, hgrn_out_norm, ([GATHER, GATHER], [b_out, b_d]))
    wd = wd.reshape(N_DEV * FF_PAD, D)
    o_pad, lse, wgu = _flash_fwd(qs_att, k_att, v_att, tq, ([GATHER], [b_gu]))
    wgu = wgu.reshape(N_DEV * 2 * FF_PAD, D)
    w_out_full = g_out.reshape(D, D)
    w_out_mla = jnp.pad(w_out_full[:N_HEADS * V_DIM].reshape(N_HEADS, V_DIM, D), ((0, 0), (0, HB - V_DIM), (0, 0)))
    w_out_p = jnp.concatenate([w_out_mla.reshape(N_HEADS * HB, D), w_out_full[N_HEADS * V_DIM:]], 0)
    h1, y1, z, mix = _fwd_out(o_pad, o_hgrn, xs, g_on_p, w_out_p, attn_post_norm, ffn_pre_norm, tm)
    tmf = min(_TMF, T)
    gs, us, ffs, dh2, dy2, d_fpost, loss_row = _ffn_fwd(z, wgu, wd, h1, tgt, ffn_post_norm, tm, _FB // FF_PAD)

    dgu, dh1, dy1, d_fpre, d_post = _ffn_bwd_x(dy2, gs, us, wgu, wd, h1, y1, dh2, ffn_pre_norm, attn_post_norm, tm)
    dwgu, dwd = _ffn_bwd_w(z, ffs, dgu, dy2, tmf)
    do_pad, dl, d_ohg, dw_out_p, d_on_p = _out_bwd(dy1, mix, o_pad, w_out_p, g_on_p, tm)
    dw_out_mla = dw_out_p[:N_HEADS * HB].reshape(N_HEADS, HB, D)[:, :V_DIM].reshape(N_HEADS * V_DIM, D)
    dw_out = jnp.concatenate([dw_out_mla, dw_out_p[N_HEADS * HB:]], 0).reshape(N_DEV, D // N_DEV, D).astype(BF)
    dk_att, dv_att, dq_att, p_gu, p_d, p_out = _flash_bwd(
        q_att, k_att, v_att, do_pad, lse, dl, tq,
        ([SCATTER] * 3, [dwgu.reshape(N_DEV, 2 * FF_PAD, D), dwd.reshape(N_DEV, FF_PAD, D), dw_out]))
    dxp_m, dw_uq_p, dw_uk_p, dw_uv_p, d_gq, d_gkv = _mla_prep_bwd(
        xp_m, tabs, dq_att, dk_att, dv_att, mla_q_norm, mla_kv_norm, w_uq_p, w_uk_p, w_uv_p, tm)
    dw_uq = _unpad_heads(dw_uq_p, HB, QK_DIM).reshape((N_DEV,) + uq_sh).astype(BF)
    dw_ukv = jnp.concatenate([_unpad_heads(dw_uk_p, HB, NOPE).reshape(KV_RANK, N_HEADS, NOPE),
                              _unpad_heads(dw_uv_p, HB, V_DIM).reshape(KV_RANK, N_HEADS, V_DIM)], -1)
    *dxp_h, d_lbl, d_ghn, p_uq, dw_ukv_all = _hgrn_bwd(
        xp_h, o_raw, s_all, a_all, d_ohg, hgrn_lb_logits, hgrn_out_norm,
        ([SCATTER, GATHER], [dw_uq, dw_ukv.reshape(KV_RANK, N_HEADS * HB)]))
    dw_in_a, = _in_bwd_w("in_bwd_w_a", u, dxp_m, dxp_h, tm, 0)
    dw_in_b, p_in_a = _in_bwd_w("in_bwd_w_b", u, dxp_m, dxp_h, tm, 1, ([SCATTER], [dw_in_a]))
    grad_x, d_pre, p_in_b = _in_bwd_x(xs, dxp_m, dxp_h, dh1, w_in_al, attn_pre_norm, tm, ([SCATTER], [dw_in_b]))
    d_on = _unpad_heads(d_on_p, HB, V_DIM)

    ukv2 = lambda a: a.reshape(KV_RANK, N_HEADS * HB)
    vecs = [d_pre, d_gq, d_gkv, d_on, d_lbl, d_ghn, d_post, d_fpre, d_fpost, loss_row]
    small_w = [attn_pre_norm, mla_q_norm, mla_kv_norm, ukv2(mla_w_ukv), mla_out_norm, hgrn_lb_logits, hgrn_out_norm,
               attn_post_norm, ffn_pre_norm, ffn_post_norm]
    small_m = [m_attn_pre_norm, m_mla_q_norm, m_mla_kv_norm, ukv2(m_mla_w_ukv), m_mla_out_norm, m_hgrn_lb_logits,
               m_hgrn_out_norm, m_attn_post_norm, m_ffn_pre_norm, m_ffn_post_norm]
    small_v = [v_attn_pre_norm, v_mla_q_norm, v_mla_kv_norm, ukv2(v_mla_w_ukv), v_mla_out_norm, v_hgrn_lb_logits,
               v_hgrn_out_norm, v_attn_post_norm, v_ffn_pre_norm, v_ffn_post_norm]
    rall = _final_exchange(vecs)
    s_g, s_d, s_m, s_v, loss_all = _small_adam(rall, dw_ukv_all, 3, small_w, small_m, small_v)
    r_in = _shard_adam("adam_w_in", [p_in_a, p_in_b], w_in[0], m_w_in[0], v_w_in[0], 256)
    r_uq = _shard_adam("adam_w_uq", [p_uq], mla_w_uq[0].reshape(uq_sh), m_mla_w_uq[0].reshape(uq_sh),
                       v_mla_w_uq[0].reshape(uq_sh), uq_sh[0])
    r_out = _shard_adam("adam_w_out", [p_out], w_out[0], m_w_out[0], v_w_out[0], D // N_DEV)
    r_g, r_u = _gate_up_adam(p_gu, (w_gate[0].T, w_up[0].T), (m_w_gate[0].T, m_w_up[0].T),
                             (v_w_gate[0].T, v_w_up[0].T))
    r_g, r_u = [a.T for a in r_g], [a.T for a in r_u]
    r_d = _shard_adam("adam_w_down", [p_d], w_down[0], m_w_down[0], v_w_down[0], FF_SH // 2)

    loss = loss_all[0, 0]

    def assemble(big, small):
        b_in, b_uq, b_out, b_g, b_u, b_d = big
        return [small[0], b_in[None], small[1], b_uq.reshape(mla_w_uq.shape), small[2],
                small[3].reshape(mla_w_ukv.shape), small[4], small[5], small[6], b_out[None], small[7], small[8],
                b_g[None], b_u[None], b_d[None], small[9]]

    outs = [loss, grad_x[None]]
    for idx, small in enumerate((s_g, s_d, s_m, s_v)):
        outs += assemble([r[idx] for r in (r_in, r_uq, r_out, r_g, r_u, r_d)], small)
    return tuple(outs)
```

```python
import jax
import jax.numpy as jnp
from jax import lax
from jax.experimental import pallas as pl
from jax.experimental.pallas import tpu as pltpu

BF = jnp.bfloat16
F32 = jnp.float32
MESH = pl.DeviceIdType.MESH

N_DEV = 8
D = 1024
EPS = 1e-6
LOG2E = 1.4426950408889634
ROPE_THETA = 10000.0
N_HEADS = 8
HB = 128
NOPE = 64
ROPE = 32
V_DIM = 64
QK_DIM = NOPE + ROPE
Q_RANK = 384
KV_RANK = 128
KR_PAD = 128
MLA_IN = Q_RANK + KV_RANK + KR_PAD
G_HEADS = 4
G_DIM = 128
G_W = G_HEADS * G_DIM
CHUNK = 64
SUB = 16
XP_W = MLA_IN + 4 * G_W
IN_SH = 324
FF_SH = 352
FF_PAD = 384
MIX_W = N_HEADS * HB + G_W

ADAM_LR = 0.001
ADAM_B1 = 0.9
ADAM_B2 = 0.999
ADAM_EPS = 1e-08
ADAM_WD = 0.01
ADAM_STEP = 10

_TM = 512
_TMF = 1024
_TQ = 512
_AH = 2
_AH_FWD = 4
_FB = 768
_TB = 1024
_TB_BWD = 512
_HP = 4
V7X_VMEM_BYTES = 64 * 1024 * 1024
_VMEM_LIMIT = V7X_VMEM_BYTES - 8 * 1024 * 1024
NEG = -1e30


def _dot(a, b):
    return jnp.dot(a.astype(BF), b.astype(BF), preferred_element_type=F32)


def _dot_nt(a, b):
    return lax.dot_general(a.astype(BF), b.astype(BF), (((1,), (1,)), ((), ())), preferred_element_type=F32)


def _dot_tn(a, b):
    return lax.dot_general(a.astype(BF), b.astype(BF), (((0,), (0,)), ((), ())), preferred_element_type=F32)


def _sigmoid(x):
    return 1.0 / (1.0 + jnp.exp(-x))


def _rms(x, n):
    r = lax.rsqrt(jnp.sum(x * x, -1, keepdims=True) * (1.0 / n) + EPS)
    return x * r, r


def _rms_bwd(nx, r, g, dy, n):
    dg = jnp.sum(dy * nx, 0, keepdims=True)
    dn = dy * g
    dx = r * (dn - nx * (jnp.sum(dn * nx, -1, keepdims=True) * (1.0 / n)))
    return dx, dg


def _adamw(w, g, m, v):
    m2 = ADAM_B1 * m + (1.0 - ADAM_B1) * g
    v2 = ADAM_B2 * v + (1.0 - ADAM_B2) * (g * g)
    m_hat = m2 / (1.0 - ADAM_B1 ** ADAM_STEP)
    v_hat = v2 / (1.0 - ADAM_B2 ** ADAM_STEP)
    delta = -ADAM_LR * (m_hat / (jnp.sqrt(v_hat) + ADAM_EPS) + ADAM_WD * w)
    return delta, m2, v2


def _pcall(body, name, grid, in_specs, out_specs, out_shape, scratch=(), exchange=None):
    scratch = list(scratch)
    extra = ()
    if exchange is not None:
        kinds, extra = exchange
        in_specs, out_specs, out_shape = list(in_specs), list(out_specs), list(out_shape)
        n_in, n_out, n_scr, n_x = len(in_specs), len(out_specs), len(scratch), len(extra)
        inner = body

        def body(*refs):
            ins, rest = refs[:n_in], refs[n_in:]
            x_src, rest = rest[:n_x], rest[n_x:]
            outs, rest = rest[:n_out], rest[n_out:]
            x_dst, rest = rest[:n_x], rest[n_x:]
            ex = _Exchange(kinds, x_src, x_dst, *rest[n_scr:])
            first = pl.program_id(0) == 0
            last = pl.program_id(0) == grid[0] - 1
            for a in range(1, len(grid)):
                first = first & (pl.program_id(a) == 0)
                last = last & (pl.program_id(a) == grid[a] - 1)
            pl.when(first)(ex.start)
            inner(*ins, *outs, *rest[:n_scr])
            pl.when(last)(ex.wait)

        in_specs += [_HBM] * n_x
        out_specs += [_HBM] * n_x
        out_shape += _exchange_shapes(kinds, extra)
        scratch += _exchange_sems(n_x)
    call = pl.pallas_call(
        body, name=name, grid=grid, in_specs=in_specs, out_specs=out_specs, out_shape=out_shape,
        scratch_shapes=scratch,
        compiler_params=pltpu.CompilerParams(
            dimension_semantics=("arbitrary",) * len(grid), vmem_limit_bytes=_VMEM_LIMIT))
    return lambda *operands: call(*operands, *extra)


def _full(shape):
    return pl.BlockSpec(shape, lambda *_: (0,) * len(shape))


def _rows(tm, n):
    return pl.BlockSpec((tm, n), lambda i, *_: (i, 0))


def _sds(shape, dtype=F32):
    return jax.ShapeDtypeStruct(shape, dtype)


def _peer(k, x, y, c):
    px = 1 - x if (k >> 2) & 1 else x
    py = 1 - y if (k >> 1) & 1 else y
    pc = 1 - c if k & 1 else c
    return px, py, pc


GATHER, SCATTER = "gather", "scatter"


class _Exchange:
    def __init__(self, kinds, srcs, dsts, send_sems, recv_sems, loc_sems):
        self.kinds, self.srcs, self.dsts = kinds, srcs, dsts
        self.send_sems, self.recv_sems, self.loc_sems = send_sems, recv_sems, loc_sems
        self.x, self.y, self.c = lax.axis_index("x"), lax.axis_index("y"), lax.axis_index("c")
        self.me = 4 * self.x + 2 * self.y + self.c

    def _src(self, w, slot):
        return self.srcs[w] if self.kinds[w] == GATHER else self.srcs[w].at[slot]

    def _dst(self, w, slot):
        return self.dsts[w].at[slot]

    def _copy(self, w, k, outgoing):
        px, py, pc = _peer(k, self.x, self.y, self.c)
        pid = 4 * px + 2 * py + pc
        return pltpu.make_async_remote_copy(
            src_ref=self._src(w, pid if outgoing else self.me),
            dst_ref=self._dst(w, self.me if outgoing else pid),
            send_sem=self.send_sems.at[w, k - 1], recv_sem=self.recv_sems.at[w, k - 1],
            device_id=(px, py, pc), device_id_type=MESH)

    def _local(self, w):
        return pltpu.make_async_copy(self._src(w, self.me), self._dst(w, self.me), self.loc_sems.at[w])

    def start(self):
        for w in range(len(self.srcs)):
            self._local(w).start()
            for k in range(1, N_DEV):
                self._copy(w, k, True).start()

    def wait(self):
        for w in range(len(self.srcs)):
            self._local(w).wait()
            for k in range(1, N_DEV):
                self._copy(w, k, False).wait_recv()
        for w in range(len(self.srcs)):
            for k in range(1, N_DEV):
                self._copy(w, k, True).wait_send()


def _exchange_sems(n_w):
    return [pltpu.SemaphoreType.DMA((n_w, N_DEV - 1)), pltpu.SemaphoreType.DMA((n_w, N_DEV - 1)),
            pltpu.SemaphoreType.DMA((n_w,))]


def _exchange_shapes(kinds, srcs):
    return [_sds(((N_DEV,) if kd == GATHER else ()) + tuple(s.shape), s.dtype) for kd, s in zip(kinds, srcs)]


_HBM = pl.BlockSpec(memory_space=pl.ANY)


def _cast_shards(w_in, w_uq, w_out, w_gate_t, w_up_t, w_down):
    shapes = [(D, IN_SH), (Q_RANK // N_DEV, N_HEADS * QK_DIM), (D // N_DEV, D), (2 * FF_PAD, D), (FF_PAD, D)]

    def body(win, wuq, wout, wg, wu, wd, sin_, suq, sout, sgu, sd):
        sin_[...] = win[...].astype(BF)
        suq[...] = wuq[...].astype(BF)
        sout[...] = wout[...].astype(BF)
        sgu[...] = jnp.zeros(sgu.shape, BF)
        sgu[0:FF_SH, :] = wg[...].astype(BF)
        sgu[FF_PAD:FF_PAD + FF_SH, :] = wu[...].astype(BF)
        sd[...] = jnp.zeros(sd.shape, BF)
        sd[0:FF_SH, :] = wd[...].astype(BF)

    vm = pl.BlockSpec(memory_space=pltpu.VMEM)
    return pl.pallas_call(
        body, name="cast_shards", in_specs=[vm] * 6, out_specs=[vm] * 5,
        out_shape=[_sds(s, BF) for s in shapes],
        compiler_params=pltpu.CompilerParams(vmem_limit_bytes=_VMEM_LIMIT),
    )(w_in, w_uq, w_out, w_gate_t, w_up_t, w_down)


def _gather_two_level(name, srcs):
    n_w = len(srcs)

    def body(*refs):
        src, dst = refs[:n_w], refs[n_w:2 * n_w]
        send_sems, recv_sems, loc_sems = refs[2 * n_w:]
        x, y, c = lax.axis_index("x"), lax.axis_index("y"), lax.axis_index("c")
        me, sibling = (x, y, c), (x, y, 1 - c)
        chips = [(1 - x, y), (x, 1 - y), (1 - x, 1 - y)]
        slot = lambda p: 4 * p[0] + 2 * p[1] + p[2]

        def copy(w, k, block, to, own=False):
            return pltpu.make_async_remote_copy(
                src_ref=src[w] if own else dst[w].at[slot(block)], dst_ref=dst[w].at[slot(block)],
                send_sem=send_sems.at[w, k], recv_sem=recv_sems.at[w, k], device_id=to, device_id_type=MESH)

        local = [pltpu.make_async_copy(src[w], dst[w].at[slot(me)], loc_sems.at[w]) for w in range(n_w)]
        first, passed = [], []
        for w in range(n_w):
            local[w].start()
            first.append(copy(w, 0, me, sibling, own=True))
            first += [copy(w, 1 + j, me, (*chip, c), own=True) for j, chip in enumerate(chips)]
        for cp in first:
            cp.start()
        for w in range(n_w):
            for j, chip in enumerate(chips):
                copy(w, 1 + j, (*chip, c), me).wait_recv()
                passed.append(copy(w, 4 + j, (*chip, c), sibling))
                passed[-1].start()
        for w in range(n_w):
            copy(w, 0, sibling, me).wait_recv()
            for j, chip in enumerate(chips):
                copy(w, 4 + j, (*chip, 1 - c), me).wait_recv()
        for cp in first + passed:
            cp.wait_send()
        for w in range(n_w):
            local[w].wait()

    return pl.pallas_call(
        body, name=name, in_specs=[_HBM] * n_w, out_specs=[_HBM] * n_w,
        out_shape=_exchange_shapes([GATHER] * n_w, srcs), scratch_shapes=_exchange_sems(n_w))(*srcs)


def _row_offsets(arrays):
    offs, rows = [], 0
    for a in arrays:
        offs.append(rows)
        rows += a.shape[0]
    return offs, -(-rows // 8) * 8


def _final_exchange(vecs):
    n_p = len(vecs)
    offs, rows = _row_offsets(vecs)

    def body(*refs):
        g_refs = refs[:n_p]
        rall, pk, send_sems, recv_sems, loc_sem = refs[n_p:]
        x, y, c = lax.axis_index("x"), lax.axis_index("y"), lax.axis_index("c")
        me = 4 * x + 2 * y + c
        pk[...] = jnp.zeros(pk.shape, F32)
        for p in range(n_p):
            r, n = g_refs[p].shape
            pk[offs[p]:offs[p] + r, 0:n] = g_refs[p][...]

        def remote(k):
            return pltpu.make_async_remote_copy(
                src_ref=pk, dst_ref=rall.at[me], send_sem=send_sems.at[k - 1], recv_sem=recv_sems.at[k - 1],
                device_id=_peer(k, x, y, c), device_id_type=MESH)

        def arrival(k):
            px, py, pc = _peer(k, x, y, c)
            return pltpu.make_async_remote_copy(
                src_ref=pk, dst_ref=rall.at[4 * px + 2 * py + pc], send_sem=send_sems.at[k - 1],
                recv_sem=recv_sems.at[k - 1], device_id=(px, py, pc), device_id_type=MESH)

        local = pltpu.make_async_copy(pk, rall.at[me], loc_sem)
        local.start()
        for k in range(1, N_DEV):
            remote(k).start()
        local.wait()
        for k in range(1, N_DEV):
            arrival(k).wait_recv()
        for k in range(1, N_DEV):
            remote(k).wait_send()

    vm = pl.BlockSpec(memory_space=pltpu.VMEM)
    return pl.pallas_call(
        body, name="final_exchange", in_specs=[vm] * n_p, out_specs=vm, out_shape=_sds((N_DEV, rows, D)),
        scratch_shapes=[pltpu.VMEM((rows, D), F32),
                        pltpu.SemaphoreType.DMA((N_DEV - 1,)), pltpu.SemaphoreType.DMA((N_DEV - 1,)),
                        pltpu.SemaphoreType.DMA],
    )(*vecs)


def _small_adam(rall, big_parts, big, ws, ms, vs):
    n_p = len(ws)
    packed = [w for p, w in enumerate(ws) if p != big] + [jax.ShapeDtypeStruct((1, HB), F32)]
    offs, _ = _row_offsets(packed)
    offs = offs[:big] + [None] + offs[big:]

    def total(ref, sl):
        g = ref[(0,) + sl]
        for j in range(1, N_DEV):
            g = g + ref[(j,) + sl]
        return g

    def body(*refs):
        rall_ref, big_ref = refs[:2]
        w_refs, m_refs, v_refs = refs[2:2 + n_p], refs[2 + n_p:2 + 2 * n_p], refs[2 + 2 * n_p:2 + 3 * n_p]
        outs = refs[2 + 3 * n_p:]
        for p in range(n_p):
            r, n = w_refs[p].shape
            if p == big:
                g = total(big_ref, (slice(0, r), slice(0, n)))
            else:
                g = total(rall_ref, (slice(offs[p], offs[p] + r), slice(0, n)))
            delta, m2, v2 = _adamw(w_refs[p][...], g, m_refs[p][...], v_refs[p][...])
            outs[p][...] = g
            outs[n_p + p][...] = delta
            outs[2 * n_p + p][...] = m2
            outs[3 * n_p + p][...] = v2
        outs[4 * n_p][...] = total(rall_ref, (slice(offs[n_p], offs[n_p] + 1), slice(0, HB)))

    vm = pl.BlockSpec(memory_space=pltpu.VMEM)
    res = pl.pallas_call(
        body, name="small_adam", in_specs=[vm] * (2 + 3 * n_p), out_specs=[vm] * (4 * n_p + 1),
        out_shape=[_sds(w.shape) for w in ws] * 4 + [_sds((1, HB))],
        compiler_params=pltpu.CompilerParams(vmem_limit_bytes=_VMEM_LIMIT),
    )(rall, big_parts, *ws, *ms, *vs)
    return res[:n_p], res[n_p:2 * n_p], res[2 * n_p:3 * n_p], res[3 * n_p:4 * n_p], res[4 * n_p]


def _device_sum(p_ref):
    g = p_ref[0].astype(F32)
    for j in range(1, N_DEV):
        g = g + p_ref[j].astype(F32)
    return g


def _shard_adam(name, parts, w, m, v, tr):
    a0, b0 = w.shape
    n_p = len(parts)
    b = parts[0].shape[2]
    first = [0]
    for p in parts:
        first.append(first[-1] + p.shape[1] // tr)

    def body(*refs):
        p_refs = refs[:n_p]
        w_ref, m_ref, v_ref, g_out, d_out, m_out, v_out = refs[n_p:]
        i = pl.program_id(0)
        g = _device_sum(p_refs[0])
        for k in range(1, n_p):
            g = jnp.where(i >= first[k], _device_sum(p_refs[k]), g)
        g = g[:, 0:b0]
        delta, m2, v2 = _adamw(w_ref[...], g, m_ref[...], v_ref[...])
        g_out[...] = g
        d_out[...] = delta
        m_out[...] = m2
        v_out[...] = v2

    def part_spec(k):
        last = first[k + 1] - first[k] - 1
        return pl.BlockSpec((N_DEV, tr, b), lambda i: (0, jnp.minimum(jnp.maximum(i - first[k], 0), last), 0))

    blk = pl.BlockSpec((tr, b0), lambda i: (i, 0))
    return _pcall(
        body, name, (a0 // tr,), [part_spec(k) for k in range(n_p)] + [blk, blk, blk],
        [blk] * 4, [_sds((a0, b0))] * 4)(*parts, w, m, v)


def _gate_up_adam(parts, ws, ms, vs):
    tc = 256

    def body(p_ref, wg, wu, mg, mu, vg, vu, *outs):
        g = _device_sum(p_ref)
        for k, (w_ref, m_ref, v_ref) in enumerate(((wg, mg, vg), (wu, mu, vu))):
            gk = g[k * FF_PAD:k * FF_PAD + FF_SH]
            delta, m2, v2 = _adamw(w_ref[...], gk, m_ref[...], v_ref[...])
            for o, val in zip(outs[4 * k:4 * k + 4], (gk, delta, m2, v2)):
                o[...] = val

    blk = pl.BlockSpec((FF_SH, tc), lambda i: (0, i))
    res = _pcall(
        body, "adam_w_gate_up", (D // tc,), [pl.BlockSpec((N_DEV, 2 * FF_PAD, tc), lambda i: (0, 0, i))] + [blk] * 6,
        [blk] * 8, [_sds((FF_SH, D))] * 8)(parts, *ws, *ms, *vs)
    return res[:4], res[4:]


def _fwd_in(x, g_pre, w_in_al, tm):
    T = x.shape[0]

    def body(x_ref, g_ref, w_ref, xm_ref, xh_ref, u_ref):
        nx, _ = _rms(x_ref[...], D)
        u = (nx * g_ref[...]).astype(BF)
        u_ref[...] = u
        xm_ref[...] = jnp.dot(u, w_ref[:, 0:MLA_IN], preferred_element_type=F32)
        xh_ref[...] = jnp.dot(u, w_ref[:, MLA_IN:XP_W], preferred_element_type=F32)

    return _pcall(body, "fwd_in", (T // tm,),
                  [_rows(tm, D), _full((1, D)), _full((D, XP_W))],
                  [_rows(tm, MLA_IN), _rows(tm, 4 * G_W), _rows(tm, D)],
                  [_sds((T, MLA_IN)), _sds((T, 4 * G_W)), _sds((T, D), BF)])(x, g_pre, w_in_al)


def _rope(blk, ta, tb1, tb2):
    return blk * ta + pltpu.roll(blk, HB - ROPE // 2, 1) * tb1 + pltpu.roll(blk, ROPE // 2, 1) * tb2


def _unrope(d, ta, tb1, tb2):
    return d * ta + pltpu.roll(d * tb1, ROPE // 2, 1) + pltpu.roll(d * tb2, HB - ROPE // 2, 1)


def _mla_prep(xp, tabs, g_q, g_kv, w_uq, w_uk, w_uv, tm):
    T = xp.shape[0]
    W = N_HEADS * HB

    def body(xp_ref, ta_ref, tb1_ref, tb2_ref, gq_ref, gkv_ref, wuq_ref, wuk_ref, wuv_ref, q_ref, qs_ref, k_ref, v_ref):
        ta, tb1, tb2 = ta_ref[...], tb1_ref[...], tb2_ref[...]
        nq, _ = _rms(xp_ref[:, 0:Q_RANK], Q_RANK)
        nkv, _ = _rms(xp_ref[:, Q_RANK:Q_RANK + KV_RANK], KV_RANK)
        nkv = (nkv * gkv_ref[...]).astype(BF)
        qpre = _dot(nq * gq_ref[...], wuq_ref[...])
        kpre = jnp.dot(nkv, wuk_ref[...], preferred_element_type=F32)
        v = jnp.dot(nkv, wuv_ref[...], preferred_element_type=F32)
        lane = lax.broadcasted_iota(jnp.int32, (tm, W), 1)
        v_ref[...] = jnp.where((lane & (HB - 1)) == V_DIM, 1.0, v).astype(BF)
        kr = _rope(pltpu.roll(xp_ref[:, Q_RANK + KV_RANK:MLA_IN], NOPE, 1), ta, tb1, tb2)
        for h in range(N_HEADS):
            sl = slice(h * HB, (h + 1) * HB)
            qr = _rope(qpre[:, sl], ta, tb1, tb2)
            q_ref[:, sl] = qr.astype(BF)
            qs_ref[:, sl] = (qr * (QK_DIM ** -0.5 * LOG2E)).astype(BF)
            k_ref[:, sl] = (kpre[:, sl] + kr).astype(BF)

    tab = _rows(tm, HB)
    return _pcall(body, "mla_prep", (T // tm,),
                  [_rows(tm, MLA_IN), tab, tab, tab, _full((1, Q_RANK)), _full((1, KV_RANK)),
                   _full((Q_RANK, W)), _full((KV_RANK, W)), _full((KV_RANK, W))],
                  [_rows(tm, W)] * 4, [_sds((T, W), BF)] * 4)(xp, *tabs, g_q, g_kv, w_uq, w_uk, w_uv)


def _flash_fwd(q, k, v, tq, exchange=None):
    T = q.shape[0]
    hp = _AH_FWD
    W = hp * HB

    def body(q_ref, k_ref, v_ref, o_ref, lse_ref):
        i = pl.program_id(1)

        def blk(j, carry, masked):
            st = pl.multiple_of(j * tq, tq)
            out = []
            for h in range(hp):
                ls = slice(h * HB, (h + 1) * HB)
                m, acc = carry[h]
                s = _dot_nt(q_ref[:, ls], k_ref[pl.ds(st, tq), ls])
                if masked:
                    r = lax.broadcasted_iota(jnp.int32, (tq, tq), 0)
                    c = lax.broadcasted_iota(jnp.int32, (tq, tq), 1)
                    s = jnp.where(c <= r, s, NEG)
                m2 = jnp.maximum(m, jnp.max(s, -1, keepdims=True))
                p = jnp.exp2(s - m2)
                out.append((m2, jnp.exp2(m - m2) * acc + _dot(p, v_ref[pl.ds(st, tq), ls])))
            return tuple(out)

        init = tuple((jnp.full((tq, 1), NEG, F32), jnp.zeros((tq, HB), F32)) for _ in range(hp))
        carry = lax.fori_loop(0, i, lambda j, cr: blk(j, cr, False), init)
        res = blk(i, carry, True)
        lane = lax.broadcasted_iota(jnp.int32, (tq, HB), 1)
        for h in range(hp):
            ls = slice(h * HB, (h + 1) * HB)
            m, acc = res[h]
            l = acc[:, V_DIM:V_DIM + 1]
            o_ref[:, ls] = jnp.where(lane < V_DIM, acc / l, 0.0)
            lse_ref[:, ls] = jnp.broadcast_to(m * (1.0 / LOG2E) + jnp.log(l), (tq, HB))

    qs = pl.BlockSpec((tq, W), lambda h, i: (i, h))
    kvs = pl.BlockSpec((T, W), lambda h, i: (0, h))
    return _pcall(body, "flash_fwd", (N_HEADS // hp, T // tq), [qs, kvs, kvs], [qs, qs],
                  [_sds((T, N_HEADS * HB))] * 2, exchange=exchange)(q, k, v)


def _gates(hq, hf, lb):
    sig = _sigmoid(hf)
    f = lb + (1.0 - lb) * sig
    sq = _sigmoid(hq)
    return hq * sq, 1.0 - f, f, jnp.log(f), sig, sq


def _lower_bound(lbl_ref):
    l0, l1 = lbl_ref[0:1, :], lbl_ref[1:2, :]
    mx = jnp.maximum(l0, l1)
    e0, e1 = jnp.exp(l0 - mx), jnp.exp(l1 - mx)
    return e0 / (e0 + e1)


def _split3(x):
    hi = x.astype(BF)
    r1 = x - hi.astype(F32)
    mid = r1.astype(BF)
    lo = (r1 - mid.astype(F32)).astype(BF)
    return hi, mid, lo


def _tri_mm(tri, x):
    hi, mid, lo = _split3(x)
    mm = lambda t: jnp.dot(tri, t, preferred_element_type=F32)
    return mm(hi) + mm(mid) + mm(lo)


def _intra_codes(sub):
    row = lax.broadcasted_iota(jnp.int32, (CHUNK, CHUNK), 0)
    col = lax.broadcasted_iota(jnp.int32, (CHUNK, CHUNK), 1)
    return sub, row, col


def _intra(q, k, b2, b_s, codes, da=None):
    grad = da is not None
    pow2 = (lambda x: jnp.exp2(jnp.minimum(x, 0.0))) if grad else jnp.exp2
    sub, row, col = codes
    a = jnp.zeros((CHUNK, CHUNK), F32)
    dq = jnp.zeros((CHUNK, G_DIM), F32)
    dk = jnp.zeros((CHUNK, G_DIM), F32)
    for i in range(1, CHUNK // sub):
        b0 = b_s[sub * i - 1:sub * i, :]
        eq, ek = pow2(b2 - b0), pow2(b0 - b2)
        mask = ((row // sub) == i) & (col < sub * i)
        if grad:
            dai = jnp.where(mask, da, 0.0)
            dq = dq + _dot(dai, k * ek) * eq
            dk = dk + _dot_tn(dai, q * eq) * ek
        else:
            a = jnp.where(mask, _dot_nt(q * eq, k * ek), a)
    for d in range(sub):
        ksh = pltpu.roll(k, d, 0) if d else k
        bsh = pltpu.roll(b2, d, 0) if d else b2
        e = pow2(b2 - bsh)
        mask = (col == row - d) & ((row & (sub - 1)) >= d)
        if grad:
            g = jnp.sum(jnp.where(mask, da, 0.0), -1, keepdims=True) * e
            dq = dq + g * ksh
            cb = g * q
            dk = dk + (pltpu.roll(cb, CHUNK - d, 0) if d else cb)
        else:
            a = jnp.where(mask, jnp.sum(q * ksh * e, -1, keepdims=True), a)
    return (dq, dk) if grad else a


def _hgrn_fwd(xp, lb_logits, g_hn, exchange=None):
    T = xp.shape[0]
    tb = min(_TB, T)
    ncb = tb // CHUNK
    hp = _HP
    W = hp * G_DIM

    def body(hq_ref, hf_ref, hi_ref, hg_ref, lbl_ref, ghn_ref, out_ref, oraw_ref, sall_ref, aall_ref, ball_ref,
             st_ref, b_s):
        lb_all = _lower_bound(lbl_ref)

        @pl.when(pl.program_id(1) == 0)
        def _():
            st_ref[...] = jnp.zeros(st_ref.shape, F32)

        row = lax.broadcasted_iota(jnp.int32, (CHUNK, CHUNK), 0)
        col = lax.broadcasted_iota(jnp.int32, (CHUNK, CHUNK), 1)
        tri = (col <= row).astype(BF)
        codes = _intra_codes(SUB)

        def chunk(c, carry):
            sl = pl.ds(pl.multiple_of(c * CHUNK, CHUNK), CHUNK)
            for h in range(hp):
                ls = slice(h * G_DIM, (h + 1) * G_DIM)
                q, k, _, lf, _, _ = _gates(hq_ref[sl, ls], hf_ref[sl, ls], lb_all[:, ls])
                v = hi_ref[sl, ls]
                b2 = _tri_mm(tri, lf) * LOG2E
                b_s[h] = b2
                ball_ref[sl, ls] = b2
                st = st_ref[h]
                sall_ref[c, h] = st
                a = _intra(q, k, b2, b_s.at[h], codes)
                aall_ref[c, h] = a
                o = _dot_nt(q * jnp.exp2(b2), st) + _dot(a, v)
                bl = b_s[h, CHUNK - 1:CHUNK, :]
                st_ref[h] = st * jnp.exp2(bl) + _dot_tn(v, k * jnp.exp2(bl - b2))
                oraw_ref[sl, ls] = o
                n, _ = _rms(o, G_DIM)
                hg = hg_ref[sl, ls]
                out_ref[sl, ls] = n * ghn_ref[:, ls] * (hg * _sigmoid(hg))
            return carry

        lax.fori_loop(0, ncb, chunk, 0)

    col_blk = lambda j: pl.BlockSpec((tb, W), lambda p, t: (t, j * (G_HEADS // hp) + p))
    head = pl.BlockSpec((tb, W), lambda p, t: (t, p))
    return _pcall(
        body, "hgrn_fwd", (G_HEADS // hp, T // tb),
        [col_blk(0), col_blk(1), col_blk(2), col_blk(3),
         pl.BlockSpec((2, W), lambda p, t: (0, p)), pl.BlockSpec((1, W), lambda p, t: (0, p))],
        [head, head, pl.BlockSpec((ncb, hp, G_DIM, G_DIM), lambda p, t: (t, p, 0, 0)),
         pl.BlockSpec((ncb, hp, CHUNK, CHUNK), lambda p, t: (t, p, 0, 0)), head],
        [_sds((T, G_W)), _sds((T, G_W)), _sds((T // CHUNK, G_HEADS, G_DIM, G_DIM)),
         _sds((T // CHUNK, G_HEADS, CHUNK, CHUNK)), _sds((T, G_W))],
        scratch=[pltpu.VMEM((hp, G_DIM, G_DIM), F32), pltpu.VMEM((hp, CHUNK, G_DIM), F32)], exchange=exchange,
    )(xp, xp, xp, xp, lb_logits, g_hn)


def _fwd_out(o_pad, o_hgrn, x, g_on, w_out, g_post, g_fpre, tm):
    T = x.shape[0]

    def body(o_ref, oh_ref, x_ref, gon_ref, w_ref, gpost_ref, gfpre_ref, h1_ref, y1_ref, z_ref, mix_ref):
        for h in range(N_HEADS):
            sl = slice(h * HB, (h + 1) * HB)
            n, _ = _rms(o_ref[:, sl], V_DIM)
            mix_ref[:, sl] = (n * gon_ref[:, sl]).astype(BF)
        mix_ref[:, N_HEADS * HB:MIX_W] = oh_ref[...].astype(BF)
        y1 = jnp.dot(mix_ref[...], w_ref[...], preferred_element_type=F32)
        y1_ref[...] = y1
        ny, _ = _rms(y1, D)
        h1 = x_ref[...] + ny * gpost_ref[...]
        h1_ref[...] = h1
        nh, _ = _rms(h1, D)
        z_ref[...] = (nh * gfpre_ref[...]).astype(BF)

    return _pcall(body, "fwd_out", (T // tm,),
                  [_rows(tm, N_HEADS * HB), _rows(tm, G_W), _rows(tm, D), _full((1, N_HEADS * HB)),
                   _full((MIX_W, D)), _full((1, D)), _full((1, D))],
                  [_rows(tm, D), _rows(tm, D), _rows(tm, D), _rows(tm, MIX_W)],
                  [_sds((T, D)), _sds((T, D)), _sds((T, D), BF), _sds((T, MIX_W), BF)],
                  )(o_pad, o_hgrn, x, g_on, w_out, g_post, g_fpre)


def _ffn_fwd(z, wgu, wd, h1, tgt, g_fpost, tm, nd):
    T = z.shape[0]
    fb = nd * FF_PAD
    nf = wd.shape[0] // fb

    def body(z_ref, wgu_ref, wd_ref, h1_ref, t_ref, gp_ref,
             as_ref, bs_ref, ff_ref, dh2_ref, dy2_ref, dgp_ref, loss_ref, acc):
        i, j = pl.program_id(0), pl.program_id(1)
        gu = _dot_nt(z_ref[...], wgu_ref[...])
        piece = lambda n: gu[:, n * FF_PAD:(n + 1) * FF_PAD]
        g = piece(0) if nd == 1 else jnp.concatenate([piece(2 * n) for n in range(nd)], 1)
        u = piece(1) if nd == 1 else jnp.concatenate([piece(2 * n + 1) for n in range(nd)], 1)
        s = _sigmoid(g)
        b = g * s
        ff = (b * u).astype(BF)
        as_ref[...] = (u * _dsilu(g, s)).astype(BF)
        bs_ref[...] = b.astype(BF)
        ff_ref[...] = ff
        part = jnp.dot(ff, wd_ref[...], preferred_element_type=F32)

        @pl.when(j == 0)
        def _():
            acc[...] = part

        @pl.when(j > 0)
        def _():
            acc[...] += part

        @pl.when((i == 0) & (j == 0))
        def _():
            dgp_ref[...] = jnp.zeros(dgp_ref.shape, F32)
            loss_ref[...] = jnp.zeros(loss_ref.shape, F32)

        @pl.when(j == nf - 1)
        def _():
            ny, r = _rms(acc[...], D)
            err = h1_ref[...] + ny * gp_ref[...] - t_ref[...]
            loss_ref[...] += 0.5 * jnp.sum(jnp.sum(err * err, -1, keepdims=True) * (1.0 / D), 0, keepdims=True)
            dh2 = err * (1.0 / D)
            dh2_ref[...] = dh2
            dy2, dgp = _rms_bwd(ny, r, gp_ref[...], dh2, D)
            dy2_ref[...] = dy2.astype(BF)
            dgp_ref[...] += dgp

    tok = lambda n: pl.BlockSpec((tm, n), lambda i, j: (i, 0))
    col = pl.BlockSpec((tm, fb), lambda i, j: (i, j))
    return _pcall(
        body, "ffn_fwd", (T // tm, nf),
        [tok(D), pl.BlockSpec((2 * fb, D), lambda i, j: (j, 0)), pl.BlockSpec((fb, D), lambda i, j: (j, 0)),
         tok(D), tok(D), _full((1, D))],
        [col, col, col, tok(D), tok(D), _full((1, D)), _full((1, HB))],
        [_sds((T, nf * fb), BF)] * 3 + [_sds((T, D)), _sds((T, D), BF), _sds((1, D)), _sds((1, HB))],
        scratch=[pltpu.VMEM((tm, D), F32)],
    )(z, wgu, wd, h1, tgt, g_fpost)


def _dsilu(x, s):
    return s * (1.0 + x * (1.0 - s))


def _ffn_bwd_x(dy2, gs, us, wgu, wd, h1, y1, dh2, g_fpre, g_post, tm):
    T = dy2.shape[0]
    nf = wd.shape[0] // _FB

    def body(dy2_ref, gs_ref, us_ref, wgu_ref, wd_ref, h1_ref, y1_ref, dh2_ref, gf_ref, gp_ref,
             dgu_ref, dh1_ref, dy1_ref, dgf_ref, dgp_ref, acc):
        i, j = pl.program_id(0), pl.program_id(1)
        dff = _dot_nt(dy2_ref[...], wd_ref[...])
        dg = (dff * gs_ref[...].astype(F32)).astype(BF)
        du = (dff * us_ref[...].astype(F32)).astype(BF)
        dgu = jnp.concatenate([dg[:, 0:FF_PAD], du[:, 0:FF_PAD], dg[:, FF_PAD:_FB], du[:, FF_PAD:_FB]], 1)
        dgu_ref[...] = dgu
        part = jnp.dot(dgu, wgu_ref[...], preferred_element_type=F32)

        @pl.when(j == 0)
        def _():
            acc[...] = part

        @pl.when(j > 0)
        def _():
            acc[...] += part

        @pl.when((i == 0) & (j == 0))
        def _():
            dgf_ref[...] = jnp.zeros(dgf_ref.shape, F32)
            dgp_ref[...] = jnp.zeros(dgp_ref.shape, F32)

        @pl.when(j == nf - 1)
        def _():
            nh, rh = _rms(h1_ref[...], D)
            dh, dgf = _rms_bwd(nh, rh, gf_ref[...], acc[...], D)
            dh1 = dh2_ref[...] + dh
            dh1_ref[...] = dh1
            dgf_ref[...] += dgf
            ny, ry = _rms(y1_ref[...], D)
            dy1, dgp = _rms_bwd(ny, ry, gp_ref[...], dh1, D)
            dy1_ref[...] = dy1.astype(BF)
            dgp_ref[...] += dgp

    tok = lambda n: pl.BlockSpec((tm, n), lambda i, j: (i, 0))
    col = pl.BlockSpec((tm, _FB), lambda i, j: (i, j))
    return _pcall(
        body, "ffn_bwd_x", (T // tm, nf),
        [tok(D), col, col, pl.BlockSpec((2 * _FB, D), lambda i, j: (j, 0)), pl.BlockSpec((_FB, D), lambda i, j: (j, 0)),
         tok(D), tok(D), tok(D), _full((1, D)), _full((1, D))],
        [pl.BlockSpec((tm, 2 * _FB), lambda i, j: (i, j)), tok(D), tok(D), _full((1, D)), _full((1, D))],
        [_sds((T, 2 * nf * _FB), BF), _sds((T, D)), _sds((T, D), BF), _sds((1, D)), _sds((1, D))],
        scratch=[pltpu.VMEM((tm, D), F32)],
    )(dy2, gs, us, wgu, wd, h1, y1, dh2, g_fpre, g_post)


def _ffn_bwd_w(z, ffs, dgu, dy2, tm):
    T = z.shape[0]
    nf = ffs.shape[1] // _FB
    nt = T // tm

    def body(z_ref, ff_ref, dgu_ref, dy2_ref, dwgu_ref, dwd_ref, agu, ad):
        i = pl.program_id(1)
        pgu = _dot_tn(dgu_ref[...], z_ref[...])
        pd = _dot_tn(ff_ref[...], dy2_ref[...])

        @pl.when(i == 0)
        def _():
            agu[...] = pgu
            ad[...] = pd

        @pl.when(i > 0)
        def _():
            agu[...] += pgu
            ad[...] += pd

        @pl.when(i == nt - 1)
        def _():
            dwgu_ref[...] = agu[...].astype(BF)
            dwd_ref[...] = ad[...].astype(BF)

    F = nf * _FB
    tok = lambda n: pl.BlockSpec((tm, n), lambda j, i: (i, 0))
    return _pcall(
        body, "ffn_bwd_w", (nf, nt),
        [tok(D), pl.BlockSpec((tm, _FB), lambda j, i: (i, j)), pl.BlockSpec((tm, 2 * _FB), lambda j, i: (i, j)), tok(D)],
        [pl.BlockSpec((2 * _FB, D), lambda j, i: (j, 0)), pl.BlockSpec((_FB, D), lambda j, i: (j, 0))],
        [_sds((2 * F, D), BF), _sds((F, D), BF)],
        scratch=[pltpu.VMEM((2 * _FB, D), F32), pltpu.VMEM((_FB, D), F32)],
    )(z, ffs, dgu, dy2)


def _out_bwd(dy1, mix, o_pad, w_out, g_on, tm):
    T = dy1.shape[0]
    W = N_HEADS * HB

    def body(dy1_ref, mix_ref, o_ref, w_ref, gon_ref, do_ref, dl_ref, dohg_ref, dw_ref, dgon_ref):
        i = pl.program_id(0)
        dy1v = dy1_ref[...]
        dmix = _dot_nt(dy1v, w_ref[...])
        pw = _dot_tn(mix_ref[...], dy1v)

        @pl.when(i == 0)
        def _():
            dw_ref[...] = pw
            dgon_ref[...] = jnp.zeros(dgon_ref.shape, F32)

        @pl.when(i > 0)
        def _():
            dw_ref[...] += pw

        for h in range(N_HEADS):
            sl = slice(h * HB, (h + 1) * HB)
            ov = o_ref[:, sl]
            n, r = _rms(ov, V_DIM)
            do, dg = _rms_bwd(n, r, gon_ref[:, sl], dmix[:, sl], V_DIM)
            dgon_ref[:, sl] += dg
            do_ref[:, sl] = do.astype(BF)
            dl_ref[:, sl] = jnp.broadcast_to(jnp.sum(do * ov, -1, keepdims=True), (tm, HB))
        dohg_ref[...] = dmix[:, W:MIX_W]

    return _pcall(body, "out_bwd", (T // tm,),
                  [_rows(tm, D), _rows(tm, MIX_W), _rows(tm, W), _full((MIX_W, D)), _full((1, W))],
                  [_rows(tm, W), _rows(tm, W), _rows(tm, G_W), _full((MIX_W, D)), _full((1, W))],
                  [_sds((T, W), BF), _sds((T, W)), _sds((T, G_W)), _sds((MIX_W, D)), _sds((1, W))],
                  )(dy1, mix, o_pad, w_out, g_on)


def _flash_bwd(q, k, v, do, lse, dl, tq, exchange=None):
    T = q.shape[0]
    nq = T // tq
    scale = QK_DIM ** -0.5
    hp = _AH
    W = hp * HB

    def body(k_ref, v_ref, q_ref, do_ref, lse_ref, dl_ref, dk_ref, dv_ref, dq_ref):
        j = pl.program_id(1)

        @pl.when(j == 0)
        def _():
            dq_ref[...] = jnp.zeros(dq_ref.shape, F32)

        def blk(i, carry, masked):
            sl = pl.ds(pl.multiple_of(i * tq, tq), tq)
            out = []
            for h in range(hp):
                ls = slice(h * HB, (h + 1) * HB)
                dk, dv = carry[h]
                kv, vv = k_ref[:, ls], v_ref[:, ls]
                qv, dov = q_ref[sl, ls], do_ref[sl, ls]
                s = _dot_nt(qv, kv) * scale
                if masked:
                    r = lax.broadcasted_iota(jnp.int32, (tq, tq), 0)
                    c = lax.broadcasted_iota(jnp.int32, (tq, tq), 1)
                    s = jnp.where(c <= r, s, NEG)
                p = jnp.exp(s - lse_ref[sl, h * HB:h * HB + 1])
                ds = p * (_dot_nt(dov, vv) - dl_ref[sl, h * HB:h * HB + 1]) * scale
                dq_ref[sl, ls] += _dot(ds, kv)
                out.append((dk + _dot_tn(ds, qv), dv + _dot_tn(p, dov)))
            return tuple(out)

        zero = jnp.zeros((tq, HB), F32)
        carry = blk(j, tuple((zero, zero) for _ in range(hp)), True)
        res = lax.fori_loop(j + 1, nq, lambda i, cr: blk(i, cr, False), carry)
        for h in range(hp):
            ls = slice(h * HB, (h + 1) * HB)
            dk_ref[:, ls] = res[h][0]
            dv_ref[:, ls] = res[h][1]

    tile = pl.BlockSpec((tq, W), lambda h, j: (j, h))
    whole = pl.BlockSpec((T, W), lambda h, j: (0, h))
    return _pcall(body, "flash_bwd", (N_HEADS // hp, nq), [tile, tile, whole, whole, whole, whole],
                  [tile, tile, whole], [_sds((T, N_HEADS * HB))] * 3, exchange=exchange)(k, v, q, do, lse, dl)


def _mla_prep_bwd(xp, tabs, dq, dk, dv, g_q, g_kv, w_uq, w_uk, w_uv, tm):
    T = xp.shape[0]
    W = N_HEADS * HB

    def body(xp_ref, ta_ref, tb1_ref, tb2_ref, dq_ref, dk_ref, dv_ref, gq_ref, gkv_ref, wuq_ref, wuk_ref, wuv_ref,
             dxp_ref, dwuq_ref, dwuk_ref, dwuv_ref, dgq_ref, dgkv_ref, dqp):
        i = pl.program_id(0)
        ta, tb1, tb2 = ta_ref[...], tb1_ref[...], tb2_ref[...]
        nq, rq = _rms(xp_ref[:, 0:Q_RANK], Q_RANK)
        nkv, rkv = _rms(xp_ref[:, Q_RANK:Q_RANK + KV_RANK], KV_RANK)
        dkr = jnp.zeros((tm, HB), F32)
        for h in range(N_HEADS):
            sl = slice(h * HB, (h + 1) * HB)
            dqp[:, sl] = _unrope(dq_ref[:, sl], ta, tb1, tb2).astype(BF)
            dkr = dkr + dk_ref[:, sl]
        dkr = pltpu.roll(_unrope(dkr, ta, tb1, tb2), HB - NOPE, 1)
        lane = lax.broadcasted_iota(jnp.int32, (tm, HB), 1)
        dxp_ref[:, Q_RANK + KV_RANK:MLA_IN] = jnp.where(lane < ROPE, dkr, 0.0)
        dqpv = dqp[...]
        dkv, dvv = dk_ref[...].astype(BF), dv_ref[...].astype(BF)
        nqs = (nq * gq_ref[...]).astype(BF)
        nkvs = (nkv * gkv_ref[...]).astype(BF)
        pq, pk, pv = _dot_tn(nqs, dqpv), _dot_tn(nkvs, dkv), _dot_tn(nkvs, dvv)
        dcq, dgq = _rms_bwd(nq, rq, gq_ref[...], _dot_nt(dqpv, wuq_ref[...]), Q_RANK)
        dckv, dgkv = _rms_bwd(nkv, rkv, gkv_ref[...], _dot_nt(dkv, wuk_ref[...]) + _dot_nt(dvv, wuv_ref[...]), KV_RANK)
        dxp_ref[:, 0:Q_RANK] = dcq
        dxp_ref[:, Q_RANK:Q_RANK + KV_RANK] = dckv

        @pl.when(i == 0)
        def _():
            dwuq_ref[...] = pq
            dwuk_ref[...] = pk
            dwuv_ref[...] = pv
            dgq_ref[...] = dgq
            dgkv_ref[...] = dgkv

        @pl.when(i > 0)
        def _():
            dwuq_ref[...] += pq
            dwuk_ref[...] += pk
            dwuv_ref[...] += pv
            dgq_ref[...] += dgq
            dgkv_ref[...] += dgkv

    tab = _rows(tm, HB)
    return _pcall(
        body, "mla_prep_bwd", (T // tm,),
        [_rows(tm, MLA_IN), tab, tab, tab, _rows(tm, W), _rows(tm, W), _rows(tm, W), _full((1, Q_RANK)),
         _full((1, KV_RANK)), _full((Q_RANK, W)), _full((KV_RANK, W)), _full((KV_RANK, W))],
        [_rows(tm, MLA_IN), _full((Q_RANK, W)), _full((KV_RANK, W)), _full((KV_RANK, W)), _full((1, Q_RANK)),
         _full((1, KV_RANK))],
        [_sds((T, MLA_IN)), _sds((Q_RANK, W)), _sds((KV_RANK, W)), _sds((KV_RANK, W)), _sds((1, Q_RANK)),
         _sds((1, KV_RANK))],
        scratch=[pltpu.VMEM((tm, W), BF)],
    )(xp, *tabs, dq, dk, dv, g_q, g_kv, w_uq, w_uk, w_uv)


def _hgrn_bwd(xp, o_raw, s_all, a_all, b_all, d_out, lb_logits, g_hn, exchange=None):
    T = xp.shape[0]
    tb = min(_TB_BWD, T)
    ncb = tb // CHUNK
    nb = T // tb
    hp = _HP
    W = hp * G_DIM

    def body(hq_ref, hf_ref, hi_ref, hg_ref, o_ref, sall_ref, aall_ref, ball_ref, dout_ref, lbl_ref, ghn_ref,
             dhq_ref, dhf_ref, dhi_ref, dhg_ref, dlbl_ref, dghn_ref, dst_ref, b_s, acc_lb, acc_g):
        t = pl.program_id(1)
        lb_all = _lower_bound(lbl_ref)

        @pl.when(t == 0)
        def _():
            dst_ref[...] = jnp.zeros(dst_ref.shape, F32)
            acc_lb[...] = jnp.zeros(acc_lb.shape, F32)
            acc_g[...] = jnp.zeros(acc_g.shape, F32)

        row = lax.broadcasted_iota(jnp.int32, (CHUNK, CHUNK), 0)
        col = lax.broadcasted_iota(jnp.int32, (CHUNK, CHUNK), 1)
        tri_t = (col >= row).astype(BF)
        codes = _intra_codes(SUB)
        last = lax.broadcasted_iota(jnp.int32, (CHUNK, G_DIM), 0) == CHUNK - 1

        def chunk(cc, carry):
            c = ncb - 1 - cc
            sl = pl.ds(pl.multiple_of(c * CHUNK, CHUNK), CHUNK)
            for h in range(hp):
                ls = slice(h * G_DIM, (h + 1) * G_DIM)
                lb, ghn = lb_all[:, ls], ghn_ref[:, ls]
                hq, hg = hq_ref[sl, ls], hg_ref[sl, ls]
                q, k, f, _, sig, sq = _gates(hq, hf_ref[sl, ls], lb)
                v = hi_ref[sl, ls]
                b2 = ball_ref[sl, ls]
                b_s[h] = b2
                st = sall_ref[c, h]
                dstn = dst_ref[h]
                o = o_ref[sl, ls]
                dout = dout_ref[sl, ls]
                n, r = _rms(o, G_DIM)
                sg = _sigmoid(hg)
                dhg_ref[sl, ls] = dout * (n * ghn) * _dsilu(hg, sg)
                do, dg = _rms_bwd(n, r, ghn, dout * (hg * sg), G_DIM)
                acc_g[:, ls] += dg
                eb = jnp.exp2(b2)
                bl = b_s[h, CHUNK - 1:CHUNK, :]
                ebl = jnp.exp2(bl)
                ekd = jnp.exp2(bl - b2)
                kd = k * ekd
                a = aall_ref[c, h]
                dq_i, dk_i = _intra(q, k, b2, b_s.at[h], codes, _dot_nt(do, v))
                dhi_ref[sl, ls] = _dot_tn(a, do) + _dot_nt(kd, dstn)
                dk_state = _dot(v, dstn) * ekd
                dq = dq_i + _dot(do, st) * eb
                dk = dk_i + dk_state
                dbl = jnp.sum(k * dk_state, 0, keepdims=True) + ebl * jnp.sum(dstn * st, 0, keepdims=True)
                db = q * dq - k * dk + jnp.where(last, dbl, 0.0)
                df = _tri_mm(tri_t, db) / f - dk
                dhf_ref[sl, ls] = df * (1.0 - lb) * sig * (1.0 - sig)
                acc_lb[:, ls] += jnp.sum(df * (1.0 - sig), 0, keepdims=True)
                dhq_ref[sl, ls] = dq * _dsilu(hq, sq)
                dst_ref[h] = dstn * ebl + _dot_tn(do, q * eb)
            return carry

        lax.fori_loop(0, ncb, chunk, 0)

        @pl.when(t == nb - 1)
        def _():
            dl0 = acc_lb[...] * lb_all * (1.0 - lb_all)
            dlbl_ref[0:1, :] = dl0
            dlbl_ref[1:2, :] = -dl0
            dghn_ref[...] = acc_g[...]

    col_blk = lambda j: pl.BlockSpec((tb, W), lambda p, t: (nb - 1 - t, j * (G_HEADS // hp) + p))
    head = pl.BlockSpec((tb, W), lambda p, t: (nb - 1 - t, p))
    two = pl.BlockSpec((2, W), lambda p, t: (0, p))
    one = pl.BlockSpec((1, W), lambda p, t: (0, p))
    res = _pcall(
        body, "hgrn_bwd", (G_HEADS // hp, nb),
        [col_blk(0), col_blk(1), col_blk(2), col_blk(3), head,
         pl.BlockSpec((ncb, hp, G_DIM, G_DIM), lambda p, t: (nb - 1 - t, p, 0, 0)),
         pl.BlockSpec((ncb, hp, CHUNK, CHUNK), lambda p, t: (nb - 1 - t, p, 0, 0)), head, head, two, one],
        [head, head, head, head, two, one],
        [_sds((T, G_W))] * 4 + [_sds((2, G_W)), _sds((1, G_W))],
        scratch=[pltpu.VMEM((hp, G_DIM, G_DIM), F32), pltpu.VMEM((hp, CHUNK, G_DIM), F32),
                 pltpu.VMEM((1, W), F32), pltpu.VMEM((1, W), F32)], exchange=exchange,
    )(xp, xp, xp, xp, o_raw, s_all, a_all, b_all, d_out, lb_logits, g_hn)
    return res


def _in_bwd_x(x, dxp_m, dxp_h, dh1, w_in_al, g_pre, tm, exchange=None):
    T = x.shape[0]

    def body(x_ref, dm_ref, d0_ref, d1_ref, d2_ref, d3_ref, dh1_ref, w_ref, g_ref, dx_ref, dg_ref):
        i = pl.program_id(0)
        du = _dot_nt(dm_ref[...], w_ref[:, 0:MLA_IN])
        for j, d_ref in enumerate((d0_ref, d1_ref, d2_ref, d3_ref)):
            du = du + _dot_nt(d_ref[...], w_ref[:, MLA_IN + j * G_W:MLA_IN + (j + 1) * G_W])
        nx, r = _rms(x_ref[...], D)
        dx, dg = _rms_bwd(nx, r, g_ref[...], du, D)
        dx_ref[...] = dh1_ref[...] + dx

        @pl.when(i == 0)
        def _():
            dg_ref[...] = dg

        @pl.when(i > 0)
        def _():
            dg_ref[...] += dg

    return _pcall(body, "in_bwd_x", (T // tm,),
                  [_rows(tm, D), _rows(tm, MLA_IN)] + [_rows(tm, G_W)] * 4 + [_rows(tm, D), _full((D, XP_W)), _full((1, D))],
                  [_rows(tm, D), _full((1, D))], [_sds((T, D)), _sds((1, D))], exchange=exchange,
                  )(x, dxp_m, *dxp_h, dh1, w_in_al, g_pre)


def _aligned_col(c):
    return jnp.where(c < Q_RANK + KV_RANK + ROPE, c, c + (KR_PAD - ROPE))


def _align_w_in(g_in):
    tile = 384
    kr_end = Q_RANK + KV_RANK + ROPE

    def body(g_ref, o_ref, gp):
        gp[...] = jnp.zeros(gp.shape, BF)
        for j in range(N_DEV):
            gp[j, :, 0:IN_SH] = g_ref[j]
        r = lax.broadcasted_iota(jnp.int32, (tile, tile), 0)
        c = lax.broadcasted_iota(jnp.int32, (tile, tile), 1)
        for t in range(XP_W // tile):
            lo, hi = t * tile, (t + 1) * tile
            cols = [a if a < kr_end else a - (KR_PAD - ROPE) for a in (lo, hi - 1)]
            acc = jnp.zeros((D, tile), F32)
            for j in range(cols[0] // IN_SH, cols[-1] // IN_SH + 1):
                sel = (r < IN_SH) & (_aligned_col(j * IN_SH + r) == lo + c)
                acc = acc + jnp.dot(gp[j], sel.astype(BF), preferred_element_type=F32)
            o_ref[:, lo:hi] = acc.astype(BF)

    vm = pl.BlockSpec(memory_space=pltpu.VMEM)
    return pl.pallas_call(
        body, name="align_w_in", in_specs=[vm], out_specs=vm, out_shape=_sds((D, XP_W), BF),
        scratch_shapes=[pltpu.VMEM((N_DEV, D, tile), BF)],
        compiler_params=pltpu.CompilerParams(vmem_limit_bytes=_VMEM_LIMIT))(g_in)


def _in_bwd_w(name, u, dxp_m, dxp_h, tm, half, exchange=None):
    T = u.shape[0]
    nt = T // tm
    nr = D // 2
    win = 640

    def body(u_ref, dm_ref, d0_ref, d1_ref, d2_ref, d3_ref, o_ref, acc):
        i = pl.program_id(0)
        ut = u_ref[...].T
        parts = [(0, MLA_IN, dm_ref)] + [(MLA_IN + j * G_W, G_W, d) for j, d in enumerate((d0_ref, d1_ref, d2_ref, d3_ref))]

        @pl.when(i == 0)
        def _():
            for lo, n, d in parts:
                acc[:, lo:lo + n] = jnp.dot(ut, d[...].astype(BF), preferred_element_type=F32)

        @pl.when(i > 0)
        def _():
            for lo, n, d in parts:
                acc[:, lo:lo + n] += jnp.dot(ut, d[...].astype(BF), preferred_element_type=F32)

        @pl.when(i == nt - 1)
        def _():
            wide = 384
            r = lax.broadcasted_iota(jnp.int32, (win, wide), 0)
            c = lax.broadcasted_iota(jnp.int32, (win, wide), 1)
            kr_end = Q_RANK + KV_RANK + ROPE
            for j in range(N_DEV):
                first = j * IN_SH if j * IN_SH < kr_end else j * IN_SH + (KR_PAD - ROPE)
                lo = min(first // HB * HB, XP_W - win)
                sel = (c < IN_SH) & (_aligned_col(j * IN_SH + c) == lo + r)
                res = jnp.dot(acc[:, lo:lo + win].astype(BF), sel.astype(BF), preferred_element_type=F32)
                o_ref[j] = res[:, 0:IN_SH].astype(BF)

    return _pcall(body, name, (nt,),
                  [pl.BlockSpec((tm, nr), lambda i: (i, half)), _rows(tm, MLA_IN)] + [_rows(tm, G_W)] * 4,
                  [_full((N_DEV, nr, IN_SH))], [_sds((N_DEV, nr, IN_SH), BF)],
                  scratch=[pltpu.VMEM((nr, XP_W), F32)], exchange=exchange)(u, dxp_m, *dxp_h)


def _pad_heads(w, width, real):
    lead = w.shape[:-1]
    w = w.reshape(lead + (N_HEADS, real))
    w = jnp.pad(w, [(0, 0)] * len(lead) + [(0, 0), (0, width - real)])
    return w.reshape(lead + (N_HEADS * width,))


def _unpad_heads(w, width, real):
    lead = w.shape[:-1]
    return w.reshape(lead + (N_HEADS, width))[..., :real].reshape(lead + (N_HEADS * real,))


def _rope_tables(positions):
    half = ROPE // 2
    inv_freq = 1.0 / (ROPE_THETA ** (jnp.arange(0, ROPE, 2, dtype=F32) / ROPE))
    ang = positions.astype(F32)[:, None] * inv_freq
    cos, sin = jnp.cos(ang), jnp.sin(ang)
    T = positions.shape[0]
    z = lambda n: jnp.zeros((T, n), F32)
    ta = jnp.concatenate([jnp.ones((T, NOPE), F32), cos, cos, z(HB - QK_DIM)], 1)
    tb1 = jnp.concatenate([z(NOPE), -sin, z(half), z(HB - QK_DIM)], 1)
    tb2 = jnp.concatenate([z(NOPE), z(half), sin, z(HB - QK_DIM)], 1)
    return ta, tb1, tb2


def kernel(x, positions, attn_pre_norm, w_in, mla_q_norm, mla_w_uq, mla_kv_norm, mla_w_ukv, mla_out_norm, hgrn_lb_logits, hgrn_out_norm, w_out, attn_post_norm, ffn_pre_norm, w_gate, w_up, w_down, ffn_post_norm, loss_target, m_attn_pre_norm, m_w_in, m_mla_q_norm, m_mla_w_uq, m_mla_kv_norm, m_mla_w_ukv, m_mla_out_norm, m_hgrn_lb_logits, m_hgrn_out_norm, m_w_out, m_attn_post_norm, m_ffn_pre_norm, m_w_gate, m_w_up, m_w_down, m_ffn_post_norm, v_attn_pre_norm, v_w_in, v_mla_q_norm, v_mla_w_uq, v_mla_kv_norm, v_mla_w_ukv, v_mla_out_norm, v_hgrn_lb_logits, v_hgrn_out_norm, v_w_out, v_attn_post_norm, v_ffn_pre_norm, v_w_gate, v_w_up, v_w_down, v_ffn_post_norm):
    T = x.shape[1]
    tm = min(_TM, T)
    tq = min(_TQ, T)
    xs, tgt = x[0], loss_target[0]
    uq_sh = (Q_RANK // N_DEV, N_HEADS * QK_DIM)

    b_in, b_uq, b_out, b_gu, b_d = _cast_shards(
        w_in[0], mla_w_uq[0].reshape(uq_sh), w_out[0], w_gate[0].T, w_up[0].T, w_down[0])
    g_in, g_uq = _gather_two_level("ag_first", [b_in, b_uq])
    w_in_al = _align_w_in(g_in)
    w_uq_p = _pad_heads(g_uq.reshape(Q_RANK, N_HEADS * QK_DIM), HB, QK_DIM)
    w_ukv = mla_w_ukv[0].astype(BF)
    w_uk_p = _pad_heads(w_ukv[..., :NOPE].reshape(KV_RANK, N_HEADS * NOPE), HB, NOPE)
    w_uv_p = _pad_heads(w_ukv[..., NOPE:].reshape(KV_RANK, N_HEADS * V_DIM), HB, V_DIM)
    g_on_p = _pad_heads(mla_out_norm, HB, V_DIM)
    tabs = _rope_tables(positions[0])

    xp_m, xp_h, u = _fwd_in(xs, attn_pre_norm, w_in_al, tm)
    q_att, qs_att, k_att, v_att = _mla_prep(xp_m, tabs, mla_q_norm, mla_kv_norm, w_uq_p, w_uk_p, w_uv_p, tm)
    o_hgrn, o_raw, s_all, a_all, b_all, g_out, wd = _hgrn_fwd(xp_h, hgrn_lb_logits, hgrn_out_norm, ([GATHER, GATHER], [b_out, b_d]))
    wd = wd.reshape(N_DEV * FF_PAD, D)
    o_pad, lse, wgu = _flash_fwd(qs_att, k_att, v_att, tq, ([GATHER], [b_gu]))
    wgu = wgu.reshape(N_DEV * 2 * FF_PAD, D)
    w_out_full = g_out.reshape(D, D)
    w_out_mla = jnp.pad(w_out_full[:N_HEADS * V_DIM].reshape(N_HEADS, V_DIM, D), ((0, 0), (0, HB - V_DIM), (0, 0)))
    w_out_p = jnp.concatenate([w_out_mla.reshape(N_HEADS * HB, D), w_out_full[N_HEADS * V_DIM:]], 0)
    h1, y1, z, mix = _fwd_out(o_pad, o_hgrn, xs, g_on_p, w_out_p, attn_post_norm, ffn_pre_norm, tm)
    tmf = min(_TMF, T)
    gs, us, ffs, dh2, dy2, d_fpost, loss_row = _ffn_fwd(z, wgu, wd, h1, tgt, ffn_post_norm, tm, _FB // FF_PAD)

    dgu, dh1, dy1, d_fpre, d_post = _ffn_bwd_x(dy2, gs, us, wgu, wd, h1, y1, dh2, ffn_pre_norm, attn_post_norm, tm)
    dwgu, dwd = _ffn_bwd_w(z, ffs, dgu, dy2, tmf)
    do_pad, dl, d_ohg, dw_out_p, d_on_p = _out_bwd(dy1, mix, o_pad, w_out_p, g_on_p, tm)
    dw_out_mla = dw_out_p[:N_HEADS * HB].reshape(N_HEADS, HB, D)[:, :V_DIM].reshape(N_HEADS * V_DIM, D)
    dw_out = jnp.concatenate([dw_out_mla, dw_out_p[N_HEADS * HB:]], 0).reshape(N_DEV, D // N_DEV, D).astype(BF)
    dk_att, dv_att, dq_att, p_gu, p_d, p_out = _flash_bwd(
        q_att, k_att, v_att, do_pad, lse, dl, tq,
        ([SCATTER] * 3, [dwgu.reshape(N_DEV, 2 * FF_PAD, D), dwd.reshape(N_DEV, FF_PAD, D), dw_out]))
    dxp_m, dw_uq_p, dw_uk_p, dw_uv_p, d_gq, d_gkv = _mla_prep_bwd(
        xp_m, tabs, dq_att, dk_att, dv_att, mla_q_norm, mla_kv_norm, w_uq_p, w_uk_p, w_uv_p, tm)
    dw_uq = _unpad_heads(dw_uq_p, HB, QK_DIM).reshape((N_DEV,) + uq_sh).astype(BF)
    dw_ukv = jnp.concatenate([_unpad_heads(dw_uk_p, HB, NOPE).reshape(KV_RANK, N_HEADS, NOPE),
                              _unpad_heads(dw_uv_p, HB, V_DIM).reshape(KV_RANK, N_HEADS, V_DIM)], -1)
    *dxp_h, d_lbl, d_ghn, p_uq, dw_ukv_all = _hgrn_bwd(
        xp_h, o_raw, s_all, a_all, b_all, d_ohg, hgrn_lb_logits, hgrn_out_norm,
        ([SCATTER, GATHER], [dw_uq, dw_ukv.reshape(KV_RANK, N_HEADS * HB)]))
    dw_in_a, = _in_bwd_w("in_bwd_w_a", u, dxp_m, dxp_h, tm, 0)
    dw_in_b, p_in_a = _in_bwd_w("in_bwd_w_b", u, dxp_m, dxp_h, tm, 1, ([SCATTER], [dw_in_a]))
    grad_x, d_pre, p_in_b = _in_bwd_x(xs, dxp_m, dxp_h, dh1, w_in_al, attn_pre_norm, tm, ([SCATTER], [dw_in_b]))
    d_on = _unpad_heads(d_on_p, HB, V_DIM)

    ukv2 = lambda a: a.reshape(KV_RANK, N_HEADS * HB)
    vecs = [d_pre, d_gq, d_gkv, d_on, d_lbl, d_ghn, d_post, d_fpre, d_fpost, loss_row]
    small_w = [attn_pre_norm, mla_q_norm, mla_kv_norm, ukv2(mla_w_ukv), mla_out_norm, hgrn_lb_logits, hgrn_out_norm,
               attn_post_norm, ffn_pre_norm, ffn_post_norm]
    small_m = [m_attn_pre_norm, m_mla_q_norm, m_mla_kv_norm, ukv2(m_mla_w_ukv), m_mla_out_norm, m_hgrn_lb_logits,
               m_hgrn_out_norm, m_attn_post_norm, m_ffn_pre_norm, m_ffn_post_norm]
    small_v = [v_attn_pre_norm, v_mla_q_norm, v_mla_kv_norm, ukv2(v_mla_w_ukv), v_mla_out_norm, v_hgrn_lb_logits,
               v_hgrn_out_norm, v_attn_post_norm, v_ffn_pre_norm, v_ffn_post_norm]
    rall = _final_exchange(vecs)
    s_g, s_d, s_m, s_v, loss_all = _small_adam(rall, dw_ukv_all, 3, small_w, small_m, small_v)
    r_in = _shard_adam("adam_w_in", [p_in_a, p_in_b], w_in[0], m_w_in[0], v_w_in[0], 256)
    r_uq = _shard_adam("adam_w_uq", [p_uq], mla_w_uq[0].reshape(uq_sh), m_mla_w_uq[0].reshape(uq_sh),
                       v_mla_w_uq[0].reshape(uq_sh), uq_sh[0])
    r_out = _shard_adam("adam_w_out", [p_out], w_out[0], m_w_out[0], v_w_out[0], D // N_DEV)
    r_g, r_u = _gate_up_adam(p_gu, (w_gate[0].T, w_up[0].T), (m_w_gate[0].T, m_w_up[0].T),
                             (v_w_gate[0].T, v_w_up[0].T))
    r_g, r_u = [a.T for a in r_g], [a.T for a in r_u]
    r_d = _shard_adam("adam_w_down", [p_d], w_down[0], m_w_down[0], v_w_down[0], FF_SH // 2)

    loss = loss_all[0, 0]

    def assemble(big, small):
        b_in, b_uq, b_out, b_g, b_u, b_d = big
        return [small[0], b_in[None], small[1], b_uq.reshape(mla_w_uq.shape), small[2],
                small[3].reshape(mla_w_ukv.shape), small[4], small[5], small[6], b_out[None], small[7], small[8],
                b_g[None], b_u[None], b_d[None], small[9]]

    outs = [loss, grad_x[None]]
    for idx, small in enumerate((s_g, s_d, s_m, s_v)):
        outs += assemble([r[idx] for r in (r_in, r_uq, r_out, r_g, r_u, r_d)], small)
    return tuple(outs)
```

```python
import jax
import jax.numpy as jnp
from jax import lax
from jax.experimental import pallas as pl
from jax.experimental.pallas import tpu as pltpu

BF = jnp.bfloat16
F32 = jnp.float32
MESH = pl.DeviceIdType.MESH

N_DEV = 8
D = 1024
EPS = 1e-6
LOG2E = 1.4426950408889634
ROPE_THETA = 10000.0
N_HEADS = 8
HB = 128
NOPE = 64
ROPE = 32
V_DIM = 64
QK_DIM = NOPE + ROPE
Q_RANK = 384
KV_RANK = 128
KR_PAD = 128
MLA_IN = Q_RANK + KV_RANK + KR_PAD
G_HEADS = 4
G_DIM = 128
G_W = G_HEADS * G_DIM
CHUNK = 64
SUB = 16
XP_W = MLA_IN + 4 * G_W
IN_SH = 324
FF_SH = 352
FF_PAD = 384
MIX_W = N_HEADS * HB + G_W

ADAM_LR = 0.001
ADAM_B1 = 0.9
ADAM_B2 = 0.999
ADAM_EPS = 1e-08
ADAM_WD = 0.01
ADAM_STEP = 10

_TM = 512
_TMF = 1024
_TQ = 512
_AH = 2
_AH_FWD = 4
_FB = 768
_TB = 1024
_TB_BWD = 512
_HP = 4
V7X_VMEM_BYTES = 64 * 1024 * 1024
_VMEM_LIMIT = V7X_VMEM_BYTES - 8 * 1024 * 1024
NEG = -1e30


def _dot(a, b):
    return jnp.dot(a.astype(BF), b.astype(BF), preferred_element_type=F32)


def _dot_nt(a, b):
    return lax.dot_general(a.astype(BF), b.astype(BF), (((1,), (1,)), ((), ())), preferred_element_type=F32)


def _dot_tn(a, b):
    return lax.dot_general(a.astype(BF), b.astype(BF), (((0,), (0,)), ((), ())), preferred_element_type=F32)


def _sigmoid(x):
    return 1.0 / (1.0 + jnp.exp(-x))


def _rms(x, n):
    r = lax.rsqrt(jnp.sum(x * x, -1, keepdims=True) * (1.0 / n) + EPS)
    return x * r, r


def _rms_bwd(nx, r, g, dy, n):
    dg = jnp.sum(dy * nx, 0, keepdims=True)
    dn = dy * g
    dx = r * (dn - nx * (jnp.sum(dn * nx, -1, keepdims=True) * (1.0 / n)))
    return dx, dg


def _adamw(w, g, m, v):
    m2 = ADAM_B1 * m + (1.0 - ADAM_B1) * g
    v2 = ADAM_B2 * v + (1.0 - ADAM_B2) * (g * g)
    m_hat = m2 / (1.0 - ADAM_B1 ** ADAM_STEP)
    v_hat = v2 / (1.0 - ADAM_B2 ** ADAM_STEP)
    delta = -ADAM_LR * (m_hat / (jnp.sqrt(v_hat) + ADAM_EPS) + ADAM_WD * w)
    return delta, m2, v2


def _pcall(body, name, grid, in_specs, out_specs, out_shape, scratch=(), exchange=None):
    scratch = list(scratch)
    extra = ()
    if exchange is not None:
        kinds, extra = exchange
        in_specs, out_specs, out_shape = list(in_specs), list(out_specs), list(out_shape)
        n_in, n_out, n_scr, n_x = len(in_specs), len(out_specs), len(scratch), len(extra)
        inner = body

        def body(*refs):
            ins, rest = refs[:n_in], refs[n_in:]
            x_src, rest = rest[:n_x], rest[n_x:]
            outs, rest = rest[:n_out], rest[n_out:]
            x_dst, rest = rest[:n_x], rest[n_x:]
            ex = _Exchange(kinds, x_src, x_dst, *rest[n_scr:])
            first = pl.program_id(0) == 0
            last = pl.program_id(0) == grid[0] - 1
            for a in range(1, len(grid)):
                first = first & (pl.program_id(a) == 0)
                last = last & (pl.program_id(a) == grid[a] - 1)
            pl.when(first)(ex.start)
            inner(*ins, *outs, *rest[:n_scr])
            pl.when(last)(ex.wait)

        in_specs += [_HBM] * n_x
        out_specs += [_HBM] * n_x
        out_shape += _exchange_shapes(kinds, extra)
        scratch += _exchange_sems(n_x)
    call = pl.pallas_call(
        body, name=name, grid=grid, in_specs=in_specs, out_specs=out_specs, out_shape=out_shape,
        scratch_shapes=scratch,
        compiler_params=pltpu.CompilerParams(
            dimension_semantics=("arbitrary",) * len(grid), vmem_limit_bytes=_VMEM_LIMIT))
    return lambda *operands: call(*operands, *extra)


def _full(shape):
    return pl.BlockSpec(shape, lambda *_: (0,) * len(shape))


def _rows(tm, n):
    return pl.BlockSpec((tm, n), lambda i, *_: (i, 0))


def _sds(shape, dtype=F32):
    return jax.ShapeDtypeStruct(shape, dtype)


def _peer(k, x, y, c):
    px = 1 - x if (k >> 2) & 1 else x
    py = 1 - y if (k >> 1) & 1 else y
    pc = 1 - c if k & 1 else c
    return px, py, pc


GATHER, SCATTER = "gather", "scatter"


class _Exchange:
    def __init__(self, kinds, srcs, dsts, send_sems, recv_sems, loc_sems):
        self.kinds, self.srcs, self.dsts = kinds, srcs, dsts
        self.send_sems, self.recv_sems, self.loc_sems = send_sems, recv_sems, loc_sems
        self.x, self.y, self.c = lax.axis_index("x"), lax.axis_index("y"), lax.axis_index("c")
        self.me = 4 * self.x + 2 * self.y + self.c

    def _src(self, w, slot):
        return self.srcs[w] if self.kinds[w] == GATHER else self.srcs[w].at[slot]

    def _dst(self, w, slot):
        return self.dsts[w].at[slot]

    def _copy(self, w, k, outgoing):
        px, py, pc = _peer(k, self.x, self.y, self.c)
        pid = 4 * px + 2 * py + pc
        return pltpu.make_async_remote_copy(
            src_ref=self._src(w, pid if outgoing else self.me),
            dst_ref=self._dst(w, self.me if outgoing else pid),
            send_sem=self.send_sems.at[w, k - 1], recv_sem=self.recv_sems.at[w, k - 1],
            device_id=(px, py, pc), device_id_type=MESH)

    def _local(self, w):
        return pltpu.make_async_copy(self._src(w, self.me), self._dst(w, self.me), self.loc_sems.at[w])

    def start(self):
        for w in range(len(self.srcs)):
            self._local(w).start()
            for k in range(1, N_DEV):
                self._copy(w, k, True).start()

    def wait(self):
        for w in range(len(self.srcs)):
            self._local(w).wait()
            for k in range(1, N_DEV):
                self._copy(w, k, False).wait_recv()
        for w in range(len(self.srcs)):
            for k in range(1, N_DEV):
                self._copy(w, k, True).wait_send()


def _exchange_sems(n_w):
    return [pltpu.SemaphoreType.DMA((n_w, N_DEV - 1)), pltpu.SemaphoreType.DMA((n_w, N_DEV - 1)),
            pltpu.SemaphoreType.DMA((n_w,))]


def _exchange_shapes(kinds, srcs):
    return [_sds(((N_DEV,) if kd == GATHER else ()) + tuple(s.shape), s.dtype) for kd, s in zip(kinds, srcs)]


_HBM = pl.BlockSpec(memory_space=pl.ANY)


def _cast_shards(w_in, w_uq, w_out, w_gate_t, w_up_t, w_down):
    shapes = [(D, IN_SH), (Q_RANK // N_DEV, N_HEADS * QK_DIM), (D // N_DEV, D), (2 * FF_PAD, D), (FF_PAD, D)]

    def body(win, wuq, wout, wg, wu, wd, sin_, suq, sout, sgu, sd):
        sin_[...] = win[...].astype(BF)
        suq[...] = wuq[...].astype(BF)
        sout[...] = wout[...].astype(BF)
        sgu[...] = jnp.zeros(sgu.shape, BF)
        sgu[0:FF_SH, :] = wg[...].astype(BF)
        sgu[FF_PAD:FF_PAD + FF_SH, :] = wu[...].astype(BF)
        sd[...] = jnp.zeros(sd.shape, BF)
        sd[0:FF_SH, :] = wd[...].astype(BF)

    vm = pl.BlockSpec(memory_space=pltpu.VMEM)
    return pl.pallas_call(
        body, name="cast_shards", in_specs=[vm] * 6, out_specs=[vm] * 5,
        out_shape=[_sds(s, BF) for s in shapes],
        compiler_params=pltpu.CompilerParams(vmem_limit_bytes=_VMEM_LIMIT),
    )(w_in, w_uq, w_out, w_gate_t, w_up_t, w_down)


def _gather_two_level(name, srcs):
    n_w = len(srcs)

    def body(*refs):
        src, dst = refs[:n_w], refs[n_w:2 * n_w]
        send_sems, recv_sems, loc_sems = refs[2 * n_w:]
        x, y, c = lax.axis_index("x"), lax.axis_index("y"), lax.axis_index("c")
        me, sibling = (x, y, c), (x, y, 1 - c)
        chips = [(1 - x, y), (x, 1 - y), (1 - x, 1 - y)]
        slot = lambda p: 4 * p[0] + 2 * p[1] + p[2]

        def copy(w, k, block, to, own=False):
            return pltpu.make_async_remote_copy(
                src_ref=src[w] if own else dst[w].at[slot(block)], dst_ref=dst[w].at[slot(block)],
                send_sem=send_sems.at[w, k], recv_sem=recv_sems.at[w, k], device_id=to, device_id_type=MESH)

        local = [pltpu.make_async_copy(src[w], dst[w].at[slot(me)], loc_sems.at[w]) for w in range(n_w)]
        first, passed = [], []
        for w in range(n_w):
            local[w].start()
            first.append(copy(w, 0, me, sibling, own=True))
            first += [copy(w, 1 + j, me, (*chip, c), own=True) for j, chip in enumerate(chips)]
        for cp in first:
            cp.start()
        for w in range(n_w):
            for j, chip in enumerate(chips):
                copy(w, 1 + j, (*chip, c), me).wait_recv()
                passed.append(copy(w, 4 + j, (*chip, c), sibling))
                passed[-1].start()
        for w in range(n_w):
            copy(w, 0, sibling, me).wait_recv()
            for j, chip in enumerate(chips):
                copy(w, 4 + j, (*chip, 1 - c), me).wait_recv()
        for cp in first + passed:
            cp.wait_send()
        for w in range(n_w):
            local[w].wait()

    return pl.pallas_call(
        body, name=name, in_specs=[_HBM] * n_w, out_specs=[_HBM] * n_w,
        out_shape=_exchange_shapes([GATHER] * n_w, srcs), scratch_shapes=_exchange_sems(n_w))(*srcs)


def _row_offsets(arrays):
    offs, rows = [], 0
    for a in arrays:
        offs.append(rows)
        rows += a.shape[0]
    return offs, -(-rows // 8) * 8


def _final_exchange(vecs):
    n_p = len(vecs)
    offs, rows = _row_offsets(vecs)

    def body(*refs):
        g_refs = refs[:n_p]
        rall, pk, send_sems, recv_sems, loc_sem = refs[n_p:]
        x, y, c = lax.axis_index("x"), lax.axis_index("y"), lax.axis_index("c")
        me = 4 * x + 2 * y + c
        pk[...] = jnp.zeros(pk.shape, F32)
        for p in range(n_p):
            r, n = g_refs[p].shape
            pk[offs[p]:offs[p] + r, 0:n] = g_refs[p][...]

        def remote(k):
            return pltpu.make_async_remote_copy(
                src_ref=pk, dst_ref=rall.at[me], send_sem=send_sems.at[k - 1], recv_sem=recv_sems.at[k - 1],
                device_id=_peer(k, x, y, c), device_id_type=MESH)

        def arrival(k):
            px, py, pc = _peer(k, x, y, c)
            return pltpu.make_async_remote_copy(
                src_ref=pk, dst_ref=rall.at[4 * px + 2 * py + pc], send_sem=send_sems.at[k - 1],
                recv_sem=recv_sems.at[k - 1], device_id=(px, py, pc), device_id_type=MESH)

        local = pltpu.make_async_copy(pk, rall.at[me], loc_sem)
        local.start()
        for k in range(1, N_DEV):
            remote(k).start()
        local.wait()
        for k in range(1, N_DEV):
            arrival(k).wait_recv()
        for k in range(1, N_DEV):
            remote(k).wait_send()

    vm = pl.BlockSpec(memory_space=pltpu.VMEM)
    return pl.pallas_call(
        body, name="final_exchange", in_specs=[vm] * n_p, out_specs=vm, out_shape=_sds((N_DEV, rows, D)),
        scratch_shapes=[pltpu.VMEM((rows, D), F32),
                        pltpu.SemaphoreType.DMA((N_DEV - 1,)), pltpu.SemaphoreType.DMA((N_DEV - 1,)),
                        pltpu.SemaphoreType.DMA],
    )(*vecs)


def _small_adam(rall, big_parts, big, ws, ms, vs):
    n_p = len(ws)
    packed = [w for p, w in enumerate(ws) if p != big] + [jax.ShapeDtypeStruct((1, HB), F32)]
    offs, _ = _row_offsets(packed)
    offs = offs[:big] + [None] + offs[big:]

    def total(ref, sl):
        g = ref[(0,) + sl]
        for j in range(1, N_DEV):
            g = g + ref[(j,) + sl]
        return g

    def body(*refs):
        rall_ref, big_ref = refs[:2]
        w_refs, m_refs, v_refs = refs[2:2 + n_p], refs[2 + n_p:2 + 2 * n_p], refs[2 + 2 * n_p:2 + 3 * n_p]
        outs = refs[2 + 3 * n_p:]
        for p in range(n_p):
            r, n = w_refs[p].shape
            if p == big:
                g = total(big_ref, (slice(0, r), slice(0, n)))
            else:
                g = total(rall_ref, (slice(offs[p], offs[p] + r), slice(0, n)))
            delta, m2, v2 = _adamw(w_refs[p][...], g, m_refs[p][...], v_refs[p][...])
            outs[p][...] = g
            outs[n_p + p][...] = delta
            outs[2 * n_p + p][...] = m2
            outs[3 * n_p + p][...] = v2
        outs[4 * n_p][...] = total(rall_ref, (slice(offs[n_p], offs[n_p] + 1), slice(0, HB)))

    vm = pl.BlockSpec(memory_space=pltpu.VMEM)
    res = pl.pallas_call(
        body, name="small_adam", in_specs=[vm] * (2 + 3 * n_p), out_specs=[vm] * (4 * n_p + 1),
        out_shape=[_sds(w.shape) for w in ws] * 4 + [_sds((1, HB))],
        compiler_params=pltpu.CompilerParams(vmem_limit_bytes=_VMEM_LIMIT),
    )(rall, big_parts, *ws, *ms, *vs)
    return res[:n_p], res[n_p:2 * n_p], res[2 * n_p:3 * n_p], res[3 * n_p:4 * n_p], res[4 * n_p]


def _device_sum(p_ref):
    g = p_ref[0].astype(F32)
    for j in range(1, N_DEV):
        g = g + p_ref[j].astype(F32)
    return g


def _shard_adam(name, parts, w, m, v, tr):
    a0, b0 = w.shape
    n_p = len(parts)
    b = parts[0].shape[2]
    first = [0]
    for p in parts:
        first.append(first[-1] + p.shape[1] // tr)

    def body(*refs):
        p_refs = refs[:n_p]
        w_ref, m_ref, v_ref, g_out, d_out, m_out, v_out = refs[n_p:]
        i = pl.program_id(0)
        g = _device_sum(p_refs[0])
        for k in range(1, n_p):
            g = jnp.where(i >= first[k], _device_sum(p_refs[k]), g)
        g = g[:, 0:b0]
        delta, m2, v2 = _adamw(w_ref[...], g, m_ref[...], v_ref[...])
        g_out[...] = g
        d_out[...] = delta
        m_out[...] = m2
        v_out[...] = v2

    def part_spec(k):
        last = first[k + 1] - first[k] - 1
        return pl.BlockSpec((N_DEV, tr, b), lambda i: (0, jnp.minimum(jnp.maximum(i - first[k], 0), last), 0))

    blk = pl.BlockSpec((tr, b0), lambda i: (i, 0))
    return _pcall(
        body, name, (a0 // tr,), [part_spec(k) for k in range(n_p)] + [blk, blk, blk],
        [blk] * 4, [_sds((a0, b0))] * 4)(*parts, w, m, v)


def _gate_up_adam(parts, ws, ms, vs):
    tc = 256

    def body(p_ref, wg, wu, mg, mu, vg, vu, *outs):
        g = _device_sum(p_ref)
        for k, (w_ref, m_ref, v_ref) in enumerate(((wg, mg, vg), (wu, mu, vu))):
            gk = g[k * FF_PAD:k * FF_PAD + FF_SH]
            delta, m2, v2 = _adamw(w_ref[...], gk, m_ref[...], v_ref[...])
            for o, val in zip(outs[4 * k:4 * k + 4], (gk, delta, m2, v2)):
                o[...] = val

    blk = pl.BlockSpec((FF_SH, tc), lambda i: (0, i))
    res = _pcall(
        body, "adam_w_gate_up", (D // tc,), [pl.BlockSpec((N_DEV, 2 * FF_PAD, tc), lambda i: (0, 0, i))] + [blk] * 6,
        [blk] * 8, [_sds((FF_SH, D))] * 8)(parts, *ws, *ms, *vs)
    return res[:4], res[4:]


def _fwd_in(x, g_pre, w_in_al, tm):
    T = x.shape[0]

    def body(x_ref, g_ref, w_ref, xm_ref, xh_ref, u_ref):
        nx, _ = _rms(x_ref[...], D)
        u = (nx * g_ref[...]).astype(BF)
        u_ref[...] = u
        xm_ref[...] = jnp.dot(u, w_ref[:, 0:MLA_IN], preferred_element_type=F32)
        xh_ref[...] = jnp.dot(u, w_ref[:, MLA_IN:XP_W], preferred_element_type=F32)

    return _pcall(body, "fwd_in", (T // tm,),
                  [_rows(tm, D), _full((1, D)), _full((D, XP_W))],
                  [_rows(tm, MLA_IN), _rows(tm, 4 * G_W), _rows(tm, D)],
                  [_sds((T, MLA_IN)), _sds((T, 4 * G_W)), _sds((T, D), BF)])(x, g_pre, w_in_al)


def _rope(blk, ta, tb1, tb2):
    return blk * ta + pltpu.roll(blk, HB - ROPE // 2, 1) * tb1 + pltpu.roll(blk, ROPE // 2, 1) * tb2


def _unrope(d, ta, tb1, tb2):
    return d * ta + pltpu.roll(d * tb1, ROPE // 2, 1) + pltpu.roll(d * tb2, HB - ROPE // 2, 1)


def _mla_prep(xp, tabs, g_q, g_kv, w_uq, w_uk, w_uv, tm):
    T = xp.shape[0]
    W = N_HEADS * HB

    def body(xp_ref, ta_ref, tb1_ref, tb2_ref, gq_ref, gkv_ref, wuq_ref, wuk_ref, wuv_ref, q_ref, qs_ref, k_ref, v_ref):
        ta, tb1, tb2 = ta_ref[...], tb1_ref[...], tb2_ref[...]
        nq, _ = _rms(xp_ref[:, 0:Q_RANK], Q_RANK)
        nkv, _ = _rms(xp_ref[:, Q_RANK:Q_RANK + KV_RANK], KV_RANK)
        nkv = (nkv * gkv_ref[...]).astype(BF)
        qpre = _dot(nq * gq_ref[...], wuq_ref[...])
        kpre = jnp.dot(nkv, wuk_ref[...], preferred_element_type=F32)
        v = jnp.dot(nkv, wuv_ref[...], preferred_element_type=F32)
        lane = lax.broadcasted_iota(jnp.int32, (tm, W), 1)
        v_ref[...] = jnp.where((lane & (HB - 1)) == V_DIM, 1.0, v).astype(BF)
        kr = _rope(pltpu.roll(xp_ref[:, Q_RANK + KV_RANK:MLA_IN], NOPE, 1), ta, tb1, tb2)
        for h in range(N_HEADS):
            sl = slice(h * HB, (h + 1) * HB)
            qr = _rope(qpre[:, sl], ta, tb1, tb2)
            q_ref[:, sl] = qr.astype(BF)
            qs_ref[:, sl] = (qr * (QK_DIM ** -0.5 * LOG2E)).astype(BF)
            k_ref[:, sl] = (kpre[:, sl] + kr).astype(BF)

    tab = _rows(tm, HB)
    return _pcall(body, "mla_prep", (T // tm,),
                  [_rows(tm, MLA_IN), tab, tab, tab, _full((1, Q_RANK)), _full((1, KV_RANK)),
                   _full((Q_RANK, W)), _full((KV_RANK, W)), _full((KV_RANK, W))],
                  [_rows(tm, W)] * 4, [_sds((T, W), BF)] * 4)(xp, *tabs, g_q, g_kv, w_uq, w_uk, w_uv)


def _flash_fwd(q, k, v, tq, exchange=None):
    T = q.shape[0]
    hp = _AH_FWD
    W = hp * HB

    def body(q_ref, k_ref, v_ref, o_ref, lse_ref):
        i = pl.program_id(1)

        def blk(j, carry, masked):
            st = pl.multiple_of(j * tq, tq)
            out = []
            for h in range(hp):
                ls = slice(h * HB, (h + 1) * HB)
                m, acc = carry[h]
                s = _dot_nt(q_ref[:, ls], k_ref[pl.ds(st, tq), ls])
                if masked:
                    r = lax.broadcasted_iota(jnp.int32, (tq, tq), 0)
                    c = lax.broadcasted_iota(jnp.int32, (tq, tq), 1)
                    s = jnp.where(c <= r, s, NEG)
                m2 = jnp.maximum(m, jnp.max(s, -1, keepdims=True))
                p = jnp.exp2(s - m2)
                out.append((m2, jnp.exp2(m - m2) * acc + _dot(p, v_ref[pl.ds(st, tq), ls])))
            return tuple(out)

        init = tuple((jnp.full((tq, 1), NEG, F32), jnp.zeros((tq, HB), F32)) for _ in range(hp))
        carry = lax.fori_loop(0, i, lambda j, cr: blk(j, cr, False), init)
        res = blk(i, carry, True)
        lane = lax.broadcasted_iota(jnp.int32, (tq, HB), 1)
        for h in range(hp):
            ls = slice(h * HB, (h + 1) * HB)
            m, acc = res[h]
            l = acc[:, V_DIM:V_DIM + 1]
            o_ref[:, ls] = jnp.where(lane < V_DIM, acc / l, 0.0).astype(BF)
            lse_ref[:, ls] = jnp.broadcast_to(m * (1.0 / LOG2E) + jnp.log(l), (tq, HB))

    qs = pl.BlockSpec((tq, W), lambda h, i: (i, h))
    kvs = pl.BlockSpec((T, W), lambda h, i: (0, h))
    return _pcall(body, "flash_fwd", (N_HEADS // hp, T // tq), [qs, kvs, kvs], [qs, qs],
                  [_sds((T, N_HEADS * HB), BF), _sds((T, N_HEADS * HB))], exchange=exchange)(q, k, v)


def _gates(hq, hf, lb):
    sig = _sigmoid(hf)
    f = lb + (1.0 - lb) * sig
    sq = _sigmoid(hq)
    return hq * sq, 1.0 - f, f, jnp.log(f), sig, sq


def _lower_bound(lbl_ref):
    l0, l1 = lbl_ref[0:1, :], lbl_ref[1:2, :]
    mx = jnp.maximum(l0, l1)
    e0, e1 = jnp.exp(l0 - mx), jnp.exp(l1 - mx)
    return e0 / (e0 + e1)


def _split3(x):
    hi = x.astype(BF)
    r1 = x - hi.astype(F32)
    mid = r1.astype(BF)
    lo = (r1 - mid.astype(F32)).astype(BF)
    return hi, mid, lo


def _tri_mm(tri, x):
    hi, mid, lo = _split3(x)
    mm = lambda t: jnp.dot(tri, t, preferred_element_type=F32)
    return mm(hi) + mm(mid) + mm(lo)


def _intra_codes(sub):
    row = lax.broadcasted_iota(jnp.int32, (CHUNK, CHUNK), 0)
    col = lax.broadcasted_iota(jnp.int32, (CHUNK, CHUNK), 1)
    return sub, row, col


def _intra(q, k, b2, b_s, codes, da=None):
    grad = da is not None
    pow2 = (lambda x: jnp.exp2(jnp.minimum(x, 0.0))) if grad else jnp.exp2
    sub, row, col = codes
    a = jnp.zeros((CHUNK, CHUNK), F32)
    dq = jnp.zeros((CHUNK, G_DIM), F32)
    dk = jnp.zeros((CHUNK, G_DIM), F32)
    for i in range(1, CHUNK // sub):
        b0 = b_s[sub * i - 1:sub * i, :]
        eq, ek = pow2(b2 - b0), pow2(b0 - b2)
        mask = ((row // sub) == i) & (col < sub * i)
        if grad:
            dai = jnp.where(mask, da, 0.0)
            dq = dq + _dot(dai, k * ek) * eq
            dk = dk + _dot_tn(dai, q * eq) * ek
        else:
            a = jnp.where(mask, _dot_nt(q * eq, k * ek), a)
    for d in range(sub):
        ksh = pltpu.roll(k, d, 0) if d else k
        bsh = pltpu.roll(b2, d, 0) if d else b2
        e = pow2(b2 - bsh)
        mask = (col == row - d) & ((row & (sub - 1)) >= d)
        if grad:
            g = jnp.sum(jnp.where(mask, da, 0.0), -1, keepdims=True) * e
            dq = dq + g * ksh
            cb = g * q
            dk = dk + (pltpu.roll(cb, CHUNK - d, 0) if d else cb)
        else:
            a = jnp.where(mask, jnp.sum(q * ksh * e, -1, keepdims=True), a)
    return (dq, dk) if grad else a


def _hgrn_fwd(xp, lb_logits, g_hn, exchange=None):
    T = xp.shape[0]
    tb = min(_TB, T)
    ncb = tb // CHUNK
    hp = _HP
    W = hp * G_DIM

    def body(hq_ref, hf_ref, hi_ref, hg_ref, lbl_ref, ghn_ref, out_ref, oraw_ref, sall_ref, aall_ref, ball_ref,
             st_ref, b_s):
        lb_all = _lower_bound(lbl_ref)

        @pl.when(pl.program_id(1) == 0)
        def _():
            st_ref[...] = jnp.zeros(st_ref.shape, F32)

        row = lax.broadcasted_iota(jnp.int32, (CHUNK, CHUNK), 0)
        col = lax.broadcasted_iota(jnp.int32, (CHUNK, CHUNK), 1)
        tri = (col <= row).astype(BF)
        codes = _intra_codes(SUB)

        def chunk(c, carry):
            sl = pl.ds(pl.multiple_of(c * CHUNK, CHUNK), CHUNK)
            for h in range(hp):
                ls = slice(h * G_DIM, (h + 1) * G_DIM)
                q, k, _, lf, _, _ = _gates(hq_ref[sl, ls], hf_ref[sl, ls], lb_all[:, ls])
                v = hi_ref[sl, ls]
                b2 = _tri_mm(tri, lf) * LOG2E
                b_s[h] = b2
                ball_ref[sl, ls] = b2
                st = st_ref[h]
                sall_ref[c, h] = st
                a = _intra(q, k, b2, b_s.at[h], codes)
                aall_ref[c, h] = a
                o = _dot_nt(q * jnp.exp2(b2), st) + _dot(a, v)
                bl = b_s[h, CHUNK - 1:CHUNK, :]
                st_ref[h] = st * jnp.exp2(bl) + _dot_tn(v, k * jnp.exp2(bl - b2))
                oraw_ref[sl, ls] = o
                n, _ = _rms(o, G_DIM)
                hg = hg_ref[sl, ls]
                out_ref[sl, ls] = n * ghn_ref[:, ls] * (hg * _sigmoid(hg))
            return carry

        lax.fori_loop(0, ncb, chunk, 0)

    col_blk = lambda j: pl.BlockSpec((tb, W), lambda p, t: (t, j * (G_HEADS // hp) + p))
    head = pl.BlockSpec((tb, W), lambda p, t: (t, p))
    return _pcall(
        body, "hgrn_fwd", (G_HEADS // hp, T // tb),
        [col_blk(0), col_blk(1), col_blk(2), col_blk(3),
         pl.BlockSpec((2, W), lambda p, t: (0, p)), pl.BlockSpec((1, W), lambda p, t: (0, p))],
        [head, head, pl.BlockSpec((ncb, hp, G_DIM, G_DIM), lambda p, t: (t, p, 0, 0)),
         pl.BlockSpec((ncb, hp, CHUNK, CHUNK), lambda p, t: (t, p, 0, 0)), head],
        [_sds((T, G_W)), _sds((T, G_W)), _sds((T // CHUNK, G_HEADS, G_DIM, G_DIM)),
         _sds((T // CHUNK, G_HEADS, CHUNK, CHUNK)), _sds((T, G_W))],
        scratch=[pltpu.VMEM((hp, G_DIM, G_DIM), F32), pltpu.VMEM((hp, CHUNK, G_DIM), F32)], exchange=exchange,
    )(xp, xp, xp, xp, lb_logits, g_hn)


def _fwd_out(o_pad, o_hgrn, x, g_on, w_out, g_post, g_fpre, tm):
    T = x.shape[0]

    def body(o_ref, oh_ref, x_ref, gon_ref, w_ref, gpost_ref, gfpre_ref, h1_ref, y1_ref, z_ref, mix_ref):
        for h in range(N_HEADS):
            sl = slice(h * HB, (h + 1) * HB)
            n, _ = _rms(o_ref[:, sl].astype(F32), V_DIM)
            mix_ref[:, sl] = (n * gon_ref[:, sl]).astype(BF)
        mix_ref[:, N_HEADS * HB:MIX_W] = oh_ref[...].astype(BF)
        y1 = jnp.dot(mix_ref[...], w_ref[...], preferred_element_type=F32)
        y1_ref[...] = y1
        ny, _ = _rms(y1, D)
        h1 = x_ref[...] + ny * gpost_ref[...]
        h1_ref[...] = h1
        nh, _ = _rms(h1, D)
        z_ref[...] = (nh * gfpre_ref[...]).astype(BF)

    return _pcall(body, "fwd_out", (T // tm,),
                  [_rows(tm, N_HEADS * HB), _rows(tm, G_W), _rows(tm, D), _full((1, N_HEADS * HB)),
                   _full((MIX_W, D)), _full((1, D)), _full((1, D))],
                  [_rows(tm, D), _rows(tm, D), _rows(tm, D), _rows(tm, MIX_W)],
                  [_sds((T, D)), _sds((T, D)), _sds((T, D), BF), _sds((T, MIX_W), BF)],
                  )(o_pad, o_hgrn, x, g_on, w_out, g_post, g_fpre)


def _ffn_fwd(z, wgu, wd, h1, tgt, g_fpost, tm, nd):
    T = z.shape[0]
    fb = nd * FF_PAD
    nf = wd.shape[0] // fb

    def body(z_ref, wgu_ref, wd_ref, h1_ref, t_ref, gp_ref,
             as_ref, bs_ref, ff_ref, dh2_ref, dy2_ref, dgp_ref, loss_ref, acc):
        i, j = pl.program_id(0), pl.program_id(1)
        gu = _dot_nt(z_ref[...], wgu_ref[...])
        piece = lambda n: gu[:, n * FF_PAD:(n + 1) * FF_PAD]
        g = piece(0) if nd == 1 else jnp.concatenate([piece(2 * n) for n in range(nd)], 1)
        u = piece(1) if nd == 1 else jnp.concatenate([piece(2 * n + 1) for n in range(nd)], 1)
        s = _sigmoid(g)
        b = g * s
        ff = (b * u).astype(BF)
        as_ref[...] = (u * _dsilu(g, s)).astype(BF)
        bs_ref[...] = b.astype(BF)
        ff_ref[...] = ff
        part = jnp.dot(ff, wd_ref[...], preferred_element_type=F32)

        @pl.when(j == 0)
        def _():
            acc[...] = part

        @pl.when(j > 0)
        def _():
            acc[...] += part

        @pl.when((i == 0) & (j == 0))
        def _():
            dgp_ref[...] = jnp.zeros(dgp_ref.shape, F32)
            loss_ref[...] = jnp.zeros(loss_ref.shape, F32)

        @pl.when(j == nf - 1)
        def _():
            ny, r = _rms(acc[...], D)
            err = h1_ref[...] + ny * gp_ref[...] - t_ref[...]
            loss_ref[...] += 0.5 * jnp.sum(jnp.sum(err * err, -1, keepdims=True) * (1.0 / D), 0, keepdims=True)
            dh2 = err * (1.0 / D)
            dh2_ref[...] = dh2
            dy2, dgp = _rms_bwd(ny, r, gp_ref[...], dh2, D)
            dy2_ref[...] = dy2.astype(BF)
            dgp_ref[...] += dgp

    tok = lambda n: pl.BlockSpec((tm, n), lambda i, j: (i, 0))
    col = pl.BlockSpec((tm, fb), lambda i, j: (i, j))
    return _pcall(
        body, "ffn_fwd", (T // tm, nf),
        [tok(D), pl.BlockSpec((2 * fb, D), lambda i, j: (j, 0)), pl.BlockSpec((fb, D), lambda i, j: (j, 0)),
         tok(D), tok(D), _full((1, D))],
        [col, col, col, tok(D), tok(D), _full((1, D)), _full((1, HB))],
        [_sds((T, nf * fb), BF)] * 3 + [_sds((T, D)), _sds((T, D), BF), _sds((1, D)), _sds((1, HB))],
        scratch=[pltpu.VMEM((tm, D), F32)],
    )(z, wgu, wd, h1, tgt, g_fpost)


def _dsilu(x, s):
    return s * (1.0 + x * (1.0 - s))


def _ffn_bwd_x(dy2, gs, us, wgu, wd, h1, y1, dh2, g_fpre, g_post, tm):
    T = dy2.shape[0]
    nf = wd.shape[0] // _FB

    def body(dy2_ref, gs_ref, us_ref, wgu_ref, wd_ref, h1_ref, y1_ref, dh2_ref, gf_ref, gp_ref,
             dgu_ref, dh1_ref, dy1_ref, dgf_ref, dgp_ref, acc):
        i, j = pl.program_id(0), pl.program_id(1)
        dff = _dot_nt(dy2_ref[...], wd_ref[...])
        dg = (dff * gs_ref[...].astype(F32)).astype(BF)
        du = (dff * us_ref[...].astype(F32)).astype(BF)
        dgu = jnp.concatenate([dg[:, 0:FF_PAD], du[:, 0:FF_PAD], dg[:, FF_PAD:_FB], du[:, FF_PAD:_FB]], 1)
        dgu_ref[...] = dgu
        part = jnp.dot(dgu, wgu_ref[...], preferred_element_type=F32)

        @pl.when(j == 0)
        def _():
            acc[...] = part

        @pl.when(j > 0)
        def _():
            acc[...] += part

        @pl.when((i == 0) & (j == 0))
        def _():
            dgf_ref[...] = jnp.zeros(dgf_ref.shape, F32)
            dgp_ref[...] = jnp.zeros(dgp_ref.shape, F32)

        @pl.when(j == nf - 1)
        def _():
            nh, rh = _rms(h1_ref[...], D)
            dh, dgf = _rms_bwd(nh, rh, gf_ref[...], acc[...], D)
            dh1 = dh2_ref[...] + dh
            dh1_ref[...] = dh1
            dgf_ref[...] += dgf
            ny, ry = _rms(y1_ref[...], D)
            dy1, dgp = _rms_bwd(ny, ry, gp_ref[...], dh1, D)
            dy1_ref[...] = dy1.astype(BF)
            dgp_ref[...] += dgp

    tok = lambda n: pl.BlockSpec((tm, n), lambda i, j: (i, 0))
    col = pl.BlockSpec((tm, _FB), lambda i, j: (i, j))
    return _pcall(
        body, "ffn_bwd_x", (T // tm, nf),
        [tok(D), col, col, pl.BlockSpec((2 * _FB, D), lambda i, j: (j, 0)), pl.BlockSpec((_FB, D), lambda i, j: (j, 0)),
         tok(D), tok(D), tok(D), _full((1, D)), _full((1, D))],
        [pl.BlockSpec((tm, 2 * _FB), lambda i, j: (i, j)), tok(D), tok(D), _full((1, D)), _full((1, D))],
        [_sds((T, 2 * nf * _FB), BF), _sds((T, D)), _sds((T, D), BF), _sds((1, D)), _sds((1, D))],
        scratch=[pltpu.VMEM((tm, D), F32)],
    )(dy2, gs, us, wgu, wd, h1, y1, dh2, g_fpre, g_post)


def _ffn_bwd_w(z, ffs, dgu, dy2, tm):
    T = z.shape[0]
    nf = ffs.shape[1] // _FB
    nt = T // tm

    def body(z_ref, ff_ref, dgu_ref, dy2_ref, dwgu_ref, dwd_ref, agu, ad):
        i = pl.program_id(1)
        pgu = _dot_tn(dgu_ref[...], z_ref[...])
        pd = _dot_tn(ff_ref[...], dy2_ref[...])

        @pl.when(i == 0)
        def _():
            agu[...] = pgu
            ad[...] = pd

        @pl.when(i > 0)
        def _():
            agu[...] += pgu
            ad[...] += pd

        @pl.when(i == nt - 1)
        def _():
            dwgu_ref[...] = agu[...].astype(BF)
            dwd_ref[...] = ad[...].astype(BF)

    F = nf * _FB
    tok = lambda n: pl.BlockSpec((tm, n), lambda j, i: (i, 0))
    return _pcall(
        body, "ffn_bwd_w", (nf, nt),
        [tok(D), pl.BlockSpec((tm, _FB), lambda j, i: (i, j)), pl.BlockSpec((tm, 2 * _FB), lambda j, i: (i, j)), tok(D)],
        [pl.BlockSpec((2 * _FB, D), lambda j, i: (j, 0)), pl.BlockSpec((_FB, D), lambda j, i: (j, 0))],
        [_sds((2 * F, D), BF), _sds((F, D), BF)],
        scratch=[pltpu.VMEM((2 * _FB, D), F32), pltpu.VMEM((_FB, D), F32)],
    )(z, ffs, dgu, dy2)


def _out_bwd(dy1, mix, o_pad, w_out, g_on, tm):
    T = dy1.shape[0]
    W = N_HEADS * HB

    def body(dy1_ref, mix_ref, o_ref, w_ref, gon_ref, do_ref, dl_ref, dohg_ref, dw_ref, dgon_ref):
        i = pl.program_id(0)
        dy1v = dy1_ref[...]
        dmix = _dot_nt(dy1v, w_ref[...])
        pw = _dot_tn(mix_ref[...], dy1v)

        @pl.when(i == 0)
        def _():
            dw_ref[...] = pw
            dgon_ref[...] = jnp.zeros(dgon_ref.shape, F32)

        @pl.when(i > 0)
        def _():
            dw_ref[...] += pw

        for h in range(N_HEADS):
            sl = slice(h * HB, (h + 1) * HB)
            ov = o_ref[:, sl].astype(F32)
            n, r = _rms(ov, V_DIM)
            do, dg = _rms_bwd(n, r, gon_ref[:, sl], dmix[:, sl], V_DIM)
            dgon_ref[:, sl] += dg
            do_ref[:, sl] = do.astype(BF)
            dl_ref[:, sl] = jnp.broadcast_to(jnp.sum(do * ov, -1, keepdims=True), (tm, HB))
        dohg_ref[...] = dmix[:, W:MIX_W]

    return _pcall(body, "out_bwd", (T // tm,),
                  [_rows(tm, D), _rows(tm, MIX_W), _rows(tm, W), _full((MIX_W, D)), _full((1, W))],
                  [_rows(tm, W), _rows(tm, W), _rows(tm, G_W), _full((MIX_W, D)), _full((1, W))],
                  [_sds((T, W), BF), _sds((T, W)), _sds((T, G_W)), _sds((MIX_W, D)), _sds((1, W))],
                  )(dy1, mix, o_pad, w_out, g_on)


def _flash_bwd(q, k, v, do, lse, dl, tq, exchange=None):
    T = q.shape[0]
    nq = T // tq
    scale = QK_DIM ** -0.5
    hp = _AH
    W = hp * HB

    def body(k_ref, v_ref, q_ref, do_ref, lse_ref, dl_ref, dk_ref, dv_ref, dq_ref):
        j = pl.program_id(1)

        @pl.when(j == 0)
        def _():
            dq_ref[...] = jnp.zeros(dq_ref.shape, F32)

        def blk(i, carry, masked):
            sl = pl.ds(pl.multiple_of(i * tq, tq), tq)
            out = []
            for h in range(hp):
                ls = slice(h * HB, (h + 1) * HB)
                dk, dv = carry[h]
                kv, vv = k_ref[:, ls], v_ref[:, ls]
                qv, dov = q_ref[sl, ls], do_ref[sl, ls]
                s = _dot_nt(qv, kv) * scale
                if masked:
                    r = lax.broadcasted_iota(jnp.int32, (tq, tq), 0)
                    c = lax.broadcasted_iota(jnp.int32, (tq, tq), 1)
                    s = jnp.where(c <= r, s, NEG)
                p = jnp.exp(s - lse_ref[sl, h * HB:h * HB + 1])
                ds = p * (_dot_nt(dov, vv) - dl_ref[sl, h * HB:h * HB + 1]) * scale
                dq_ref[sl, ls] += _dot(ds, kv)
                out.append((dk + _dot_tn(ds, qv), dv + _dot_tn(p, dov)))
            return tuple(out)

        zero = jnp.zeros((tq, HB), F32)
        carry = blk(j, tuple((zero, zero) for _ in range(hp)), True)
        res = lax.fori_loop(j + 1, nq, lambda i, cr: blk(i, cr, False), carry)
        for h in range(hp):
            ls = slice(h * HB, (h + 1) * HB)
            dk_ref[:, ls] = res[h][0].astype(BF)
            dv_ref[:, ls] = res[h][1].astype(BF)

    tile = pl.BlockSpec((tq, W), lambda h, j: (j, h))
    whole = pl.BlockSpec((T, W), lambda h, j: (0, h))
    return _pcall(body, "flash_bwd", (N_HEADS // hp, nq), [tile, tile, whole, whole, whole, whole],
                  [tile, tile, whole], [_sds((T, N_HEADS * HB), BF)] * 2 + [_sds((T, N_HEADS * HB))],
                  exchange=exchange)(k, v, q, do, lse, dl)


def _mla_prep_bwd(xp, tabs, dq, dk, dv, g_q, g_kv, w_uq, w_uk, w_uv, tm):
    T = xp.shape[0]
    W = N_HEADS * HB

    def body(xp_ref, ta_ref, tb1_ref, tb2_ref, dq_ref, dk_ref, dv_ref, gq_ref, gkv_ref, wuq_ref, wuk_ref, wuv_ref,
             dxp_ref, dwuq_ref, dwuk_ref, dwuv_ref, dgq_ref, dgkv_ref, dqp):
        i = pl.program_id(0)
        ta, tb1, tb2 = ta_ref[...], tb1_ref[...], tb2_ref[...]
        nq, rq = _rms(xp_ref[:, 0:Q_RANK], Q_RANK)
        nkv, rkv = _rms(xp_ref[:, Q_RANK:Q_RANK + KV_RANK], KV_RANK)
        dkr = jnp.zeros((tm, HB), F32)
        for h in range(N_HEADS):
            sl = slice(h * HB, (h + 1) * HB)
            dqp[:, sl] = _unrope(dq_ref[:, sl], ta, tb1, tb2).astype(BF)
            dkr = dkr + dk_ref[:, sl].astype(F32)
        dkr = pltpu.roll(_unrope(dkr, ta, tb1, tb2), HB - NOPE, 1)
        lane = lax.broadcasted_iota(jnp.int32, (tm, HB), 1)
        dxp_ref[:, Q_RANK + KV_RANK:MLA_IN] = jnp.where(lane < ROPE, dkr, 0.0)
        dqpv = dqp[...]
        dkv, dvv = dk_ref[...], dv_ref[...]
        nqs = (nq * gq_ref[...]).astype(BF)
        nkvs = (nkv * gkv_ref[...]).astype(BF)
        pq, pk, pv = _dot_tn(nqs, dqpv), _dot_tn(nkvs, dkv), _dot_tn(nkvs, dvv)
        dcq, dgq = _rms_bwd(nq, rq, gq_ref[...], _dot_nt(dqpv, wuq_ref[...]), Q_RANK)
        dckv, dgkv = _rms_bwd(nkv, rkv, gkv_ref[...], _dot_nt(dkv, wuk_ref[...]) + _dot_nt(dvv, wuv_ref[...]), KV_RANK)
        dxp_ref[:, 0:Q_RANK] = dcq
        dxp_ref[:, Q_RANK:Q_RANK + KV_RANK] = dckv

        @pl.when(i == 0)
        def _():
            dwuq_ref[...] = pq
            dwuk_ref[...] = pk
            dwuv_ref[...] = pv
            dgq_ref[...] = dgq
            dgkv_ref[...] = dgkv

        @pl.when(i > 0)
        def _():
            dwuq_ref[...] += pq
            dwuk_ref[...] += pk
            dwuv_ref[...] += pv
            dgq_ref[...] += dgq
            dgkv_ref[...] += dgkv

    tab = _rows(tm, HB)
    return _pcall(
        body, "mla_prep_bwd", (T // tm,),
        [_rows(tm, MLA_IN), tab, tab, tab, _rows(tm, W), _rows(tm, W), _rows(tm, W), _full((1, Q_RANK)),
         _full((1, KV_RANK)), _full((Q_RANK, W)), _full((KV_RANK, W)), _full((KV_RANK, W))],
        [_rows(tm, MLA_IN), _full((Q_RANK, W)), _full((KV_RANK, W)), _full((KV_RANK, W)), _full((1, Q_RANK)),
         _full((1, KV_RANK))],
        [_sds((T, MLA_IN)), _sds((Q_RANK, W)), _sds((KV_RANK, W)), _sds((KV_RANK, W)), _sds((1, Q_RANK)),
         _sds((1, KV_RANK))],
        scratch=[pltpu.VMEM((tm, W), BF)],
    )(xp, *tabs, dq, dk, dv, g_q, g_kv, w_uq, w_uk, w_uv)


def _hgrn_bwd(xp, o_raw, s_all, a_all, b_all, d_out, lb_logits, g_hn, exchange=None):
    T = xp.shape[0]
    tb = min(_TB_BWD, T)
    ncb = tb // CHUNK
    nb = T // tb
    hp = _HP
    W = hp * G_DIM

    def body(hq_ref, hf_ref, hi_ref, hg_ref, o_ref, sall_ref, aall_ref, ball_ref, dout_ref, lbl_ref, ghn_ref,
             dhq_ref, dhf_ref, dhi_ref, dhg_ref, dlbl_ref, dghn_ref, dst_ref, b_s, acc_lb, acc_g):
        t = pl.program_id(1)
        lb_all = _lower_bound(lbl_ref)

        @pl.when(t == 0)
        def _():
            dst_ref[...] = jnp.zeros(dst_ref.shape, F32)
            acc_lb[...] = jnp.zeros(acc_lb.shape, F32)
            acc_g[...] = jnp.zeros(acc_g.shape, F32)

        row = lax.broadcasted_iota(jnp.int32, (CHUNK, CHUNK), 0)
        col = lax.broadcasted_iota(jnp.int32, (CHUNK, CHUNK), 1)
        tri_t = (col >= row).astype(BF)
        codes = _intra_codes(SUB)
        last = lax.broadcasted_iota(jnp.int32, (CHUNK, G_DIM), 0) == CHUNK - 1

        def chunk(cc, carry):
            c = ncb - 1 - cc
            sl = pl.ds(pl.multiple_of(c * CHUNK, CHUNK), CHUNK)
            for h in range(hp):
                ls = slice(h * G_DIM, (h + 1) * G_DIM)
                lb, ghn = lb_all[:, ls], ghn_ref[:, ls]
                hq, hg = hq_ref[sl, ls], hg_ref[sl, ls]
                q, k, f, _, sig, sq = _gates(hq, hf_ref[sl, ls], lb)
                v = hi_ref[sl, ls]
                b2 = ball_ref[sl, ls]
                b_s[h] = b2
                st = sall_ref[c, h]
                dstn = dst_ref[h]
                o = o_ref[sl, ls]
                dout = dout_ref[sl, ls]
                n, r = _rms(o, G_DIM)
                sg = _sigmoid(hg)
                dhg_ref[sl, ls] = dout * (n * ghn) * _dsilu(hg, sg)
                do, dg = _rms_bwd(n, r, ghn, dout * (hg * sg), G_DIM)
                acc_g[:, ls] += dg
                eb = jnp.exp2(b2)
                bl = b_s[h, CHUNK - 1:CHUNK, :]
                ebl = jnp.exp2(bl)
                ekd = jnp.exp2(bl - b2)
                kd = k * ekd
                a = aall_ref[c, h]
                dq_i, dk_i = _intra(q, k, b2, b_s.at[h], codes, _dot_nt(do, v))
                dhi_ref[sl, ls] = _dot_tn(a, do) + _dot_nt(kd, dstn)
                dk_state = _dot(v, dstn) * ekd
                dq = dq_i + _dot(do, st) * eb
                dk = dk_i + dk_state
                dbl = jnp.sum(k * dk_state, 0, keepdims=True) + ebl * jnp.sum(dstn * st, 0, keepdims=True)
                db = q * dq - k * dk + jnp.where(last, dbl, 0.0)
                df = _tri_mm(tri_t, db) / f - dk
                dhf_ref[sl, ls] = df * (1.0 - lb) * sig * (1.0 - sig)
                acc_lb[:, ls] += jnp.sum(df * (1.0 - sig), 0, keepdims=True)
                dhq_ref[sl, ls] = dq * _dsilu(hq, sq)
                dst_ref[h] = dstn * ebl + _dot_tn(do, q * eb)
            return carry

        lax.fori_loop(0, ncb, chunk, 0)

        @pl.when(t == nb - 1)
        def _():
            dl0 = acc_lb[...] * lb_all * (1.0 - lb_all)
            dlbl_ref[0:1, :] = dl0
            dlbl_ref[1:2, :] = -dl0
            dghn_ref[...] = acc_g[...]

    col_blk = lambda j: pl.BlockSpec((tb, W), lambda p, t: (nb - 1 - t, j * (G_HEADS // hp) + p))
    head = pl.BlockSpec((tb, W), lambda p, t: (nb - 1 - t, p))
    two = pl.BlockSpec((2, W), lambda p, t: (0, p))
    one = pl.BlockSpec((1, W), lambda p, t: (0, p))
    res = _pcall(
        body, "hgrn_bwd", (G_HEADS // hp, nb),
        [col_blk(0), col_blk(1), col_blk(2), col_blk(3), head,
         pl.BlockSpec((ncb, hp, G_DIM, G_DIM), lambda p, t: (nb - 1 - t, p, 0, 0)),
         pl.BlockSpec((ncb, hp, CHUNK, CHUNK), lambda p, t: (nb - 1 - t, p, 0, 0)), head, head, two, one],
        [head, head, head, head, two, one],
        [_sds((T, G_W))] * 4 + [_sds((2, G_W)), _sds((1, G_W))],
        scratch=[pltpu.VMEM((hp, G_DIM, G_DIM), F32), pltpu.VMEM((hp, CHUNK, G_DIM), F32),
                 pltpu.VMEM((1, W), F32), pltpu.VMEM((1, W), F32)], exchange=exchange,
    )(xp, xp, xp, xp, o_raw, s_all, a_all, b_all, d_out, lb_logits, g_hn)
    return res


def _in_bwd_x(x, dxp_m, dxp_h, dh1, w_in_al, g_pre, tm, exchange=None):
    T = x.shape[0]

    def body(x_ref, dm_ref, d0_ref, d1_ref, d2_ref, d3_ref, dh1_ref, w_ref, g_ref, dx_ref, dg_ref):
        i = pl.program_id(0)
        du = _dot_nt(dm_ref[...], w_ref[:, 0:MLA_IN])
        for j, d_ref in enumerate((d0_ref, d1_ref, d2_ref, d3_ref)):
            du = du + _dot_nt(d_ref[...], w_ref[:, MLA_IN + j * G_W:MLA_IN + (j + 1) * G_W])
        nx, r = _rms(x_ref[...], D)
        dx, dg = _rms_bwd(nx, r, g_ref[...], du, D)
        dx_ref[...] = dh1_ref[...] + dx

        @pl.when(i == 0)
        def _():
            dg_ref[...] = dg

        @pl.when(i > 0)
        def _():
            dg_ref[...] += dg

    return _pcall(body, "in_bwd_x", (T // tm,),
                  [_rows(tm, D), _rows(tm, MLA_IN)] + [_rows(tm, G_W)] * 4 + [_rows(tm, D), _full((D, XP_W)), _full((1, D))],
                  [_rows(tm, D), _full((1, D))], [_sds((T, D)), _sds((1, D))], exchange=exchange,
                  )(x, dxp_m, *dxp_h, dh1, w_in_al, g_pre)


def _aligned_col(c):
    return jnp.where(c < Q_RANK + KV_RANK + ROPE, c, c + (KR_PAD - ROPE))


def _align_w_in(g_in):
    tile = 384
    kr_end = Q_RANK + KV_RANK + ROPE

    def body(g_ref, o_ref, gp):
        gp[...] = jnp.zeros(gp.shape, BF)
        for j in range(N_DEV):
            gp[j, :, 0:IN_SH] = g_ref[j]
        r = lax.broadcasted_iota(jnp.int32, (tile, tile), 0)
        c = lax.broadcasted_iota(jnp.int32, (tile, tile), 1)
        for t in range(XP_W // tile):
            lo, hi = t * tile, (t + 1) * tile
            cols = [a if a < kr_end else a - (KR_PAD - ROPE) for a in (lo, hi - 1)]
            acc = jnp.zeros((D, tile), F32)
            for j in range(cols[0] // IN_SH, cols[-1] // IN_SH + 1):
                sel = (r < IN_SH) & (_aligned_col(j * IN_SH + r) == lo + c)
                acc = acc + jnp.dot(gp[j], sel.astype(BF), preferred_element_type=F32)
            o_ref[:, lo:hi] = acc.astype(BF)

    vm = pl.BlockSpec(memory_space=pltpu.VMEM)
    return pl.pallas_call(
        body, name="align_w_in", in_specs=[vm], out_specs=vm, out_shape=_sds((D, XP_W), BF),
        scratch_shapes=[pltpu.VMEM((N_DEV, D, tile), BF)],
        compiler_params=pltpu.CompilerParams(vmem_limit_bytes=_VMEM_LIMIT))(g_in)


def _in_bwd_w(name, u, dxp_m, dxp_h, tm, half, exchange=None):
    T = u.shape[0]
    nt = T // tm
    nr = D // 2
    win = 640

    def body(u_ref, dm_ref, d0_ref, d1_ref, d2_ref, d3_ref, o_ref, acc):
        i = pl.program_id(0)
        ut = u_ref[...].T
        parts = [(0, MLA_IN, dm_ref)] + [(MLA_IN + j * G_W, G_W, d) for j, d in enumerate((d0_ref, d1_ref, d2_ref, d3_ref))]

        @pl.when(i == 0)
        def _():
            for lo, n, d in parts:
                acc[:, lo:lo + n] = jnp.dot(ut, d[...].astype(BF), preferred_element_type=F32)

        @pl.when(i > 0)
        def _():
            for lo, n, d in parts:
                acc[:, lo:lo + n] += jnp.dot(ut, d[...].astype(BF), preferred_element_type=F32)

        @pl.when(i == nt - 1)
        def _():
            wide = 384
            r = lax.broadcasted_iota(jnp.int32, (win, wide), 0)
            c = lax.broadcasted_iota(jnp.int32, (win, wide), 1)
            kr_end = Q_RANK + KV_RANK + ROPE
            for j in range(N_DEV):
                first = j * IN_SH if j * IN_SH < kr_end else j * IN_SH + (KR_PAD - ROPE)
                lo = min(first // HB * HB, XP_W - win)
                sel = (c < IN_SH) & (_aligned_col(j * IN_SH + c) == lo + r)
                res = jnp.dot(acc[:, lo:lo + win].astype(BF), sel.astype(BF), preferred_element_type=F32)
                o_ref[j] = res[:, 0:IN_SH].astype(BF)

    return _pcall(body, name, (nt,),
                  [pl.BlockSpec((tm, nr), lambda i: (i, half)), _rows(tm, MLA_IN)] + [_rows(tm, G_W)] * 4,
                  [_full((N_DEV, nr, IN_SH))], [_sds((N_DEV, nr, IN_SH), BF)],
                  scratch=[pltpu.VMEM((nr, XP_W), F32)], exchange=exchange)(u, dxp_m, *dxp_h)


def _pad_heads(w, width, real):
    lead = w.shape[:-1]
    w = w.reshape(lead + (N_HEADS, real))
    w = jnp.pad(w, [(0, 0)] * len(lead) + [(0, 0), (0, width - real)])
    return w.reshape(lead + (N_HEADS * width,))


def _unpad_heads(w, width, real):
    lead = w.shape[:-1]
    return w.reshape(lead + (N_HEADS, width))[..., :real].reshape(lead + (N_HEADS * real,))


def _rope_tables(positions):
    half = ROPE // 2
    inv_freq = 1.0 / (ROPE_THETA ** (jnp.arange(0, ROPE, 2, dtype=F32) / ROPE))
    ang = positions.astype(F32)[:, None] * inv_freq
    cos, sin = jnp.cos(ang), jnp.sin(ang)
    T = positions.shape[0]
    z = lambda n: jnp.zeros((T, n), F32)
    ta = jnp.concatenate([jnp.ones((T, NOPE), F32), cos, cos, z(HB - QK_DIM)], 1)
    tb1 = jnp.concatenate([z(NOPE), -sin, z(half), z(HB - QK_DIM)], 1)
    tb2 = jnp.concatenate([z(NOPE), z(half), sin, z(HB - QK_DIM)], 1)
    return ta, tb1, tb2


def kernel(x, positions, attn_pre_norm, w_in, mla_q_norm, mla_w_uq, mla_kv_norm, mla_w_ukv, mla_out_norm, hgrn_lb_logits, hgrn_out_norm, w_out, attn_post_norm, ffn_pre_norm, w_gate, w_up, w_down, ffn_post_norm, loss_target, m_attn_pre_norm, m_w_in, m_mla_q_norm, m_mla_w_uq, m_mla_kv_norm, m_mla_w_ukv, m_mla_out_norm, m_hgrn_lb_logits, m_hgrn_out_norm, m_w_out, m_attn_post_norm, m_ffn_pre_norm, m_w_gate, m_w_up, m_w_down, m_ffn_post_norm, v_attn_pre_norm, v_w_in, v_mla_q_norm, v_mla_w_uq, v_mla_kv_norm, v_mla_w_ukv, v_mla_out_norm, v_hgrn_lb_logits, v_hgrn_out_norm, v_w_out, v_attn_post_norm, v_ffn_pre_norm, v_w_gate, v_w_up, v_w_down, v_ffn_post_norm):
    T = x.shape[1]
    tm = min(_TM, T)
    tq = min(_TQ, T)
    xs, tgt = x[0], loss_target[0]
    uq_sh = (Q_RANK // N_DEV, N_HEADS * QK_DIM)

    b_in, b_uq, b_out, b_gu, b_d = _cast_shards(
        w_in[0], mla_w_uq[0].reshape(uq_sh), w_out[0], w_gate[0].T, w_up[0].T, w_down[0])
    g_in, g_uq = _gather_two_level("ag_first", [b_in, b_uq])
    w_in_al = _align_w_in(g_in)
    w_uq_p = _pad_heads(g_uq.reshape(Q_RANK, N_HEADS * QK_DIM), HB, QK_DIM)
    w_ukv = mla_w_ukv[0].astype(BF)
    w_uk_p = _pad_heads(w_ukv[..., :NOPE].reshape(KV_RANK, N_HEADS * NOPE), HB, NOPE)
    w_uv_p = _pad_heads(w_ukv[..., NOPE:].reshape(KV_RANK, N_HEADS * V_DIM), HB, V_DIM)
    g_on_p = _pad_heads(mla_out_norm, HB, V_DIM)
    tabs = _rope_tables(positions[0])

    xp_m, xp_h, u = _fwd_in(xs, attn_pre_norm, w_in_al, tm)
    q_att, qs_att, k_att, v_att = _mla_prep(xp_m, tabs, mla_q_norm, mla_kv_norm, w_uq_p, w_uk_p, w_uv_p, tm)
    o_hgrn, o_raw, s_all, a_all, b_all, g_out, wd = _hgrn_fwd(xp_h, hgrn_lb_logits, hgrn_out_norm, ([GATHER, GATHER], [b_out, b_d]))
    wd = wd.reshape(N_DEV * FF_PAD, D)
    o_pad, lse, wgu = _flash_fwd(qs_att, k_att, v_att, tq, ([GATHER], [b_gu]))
    wgu = wgu.reshape(N_DEV * 2 * FF_PAD, D)
    w_out_full = g_out.reshape(D, D)
    w_out_mla = jnp.pad(w_out_full[:N_HEADS * V_DIM].reshape(N_HEADS, V_DIM, D), ((0, 0), (0, HB - V_DIM), (0, 0)))
    w_out_p = jnp.concatenate([w_out_mla.reshape(N_HEADS * HB, D), w_out_full[N_HEADS * V_DIM:]], 0)
    h1, y1, z, mix = _fwd_out(o_pad, o_hgrn, xs, g_on_p, w_out_p, attn_post_norm, ffn_pre_norm, tm)
    tmf = min(_TMF, T)
    gs, us, ffs, dh2, dy2, d_fpost, loss_row = _ffn_fwd(z, wgu, wd, h1, tgt, ffn_post_norm, tm, _FB // FF_PAD)

    dgu, dh1, dy1, d_fpre, d_post = _ffn_bwd_x(dy2, gs, us, wgu, wd, h1, y1, dh2, ffn_pre_norm, attn_post_norm, tm)
    dwgu, dwd = _ffn_bwd_w(z, ffs, dgu, dy2, tmf)
    do_pad, dl, d_ohg, dw_out_p, d_on_p = _out_bwd(dy1, mix, o_pad, w_out_p, g_on_p, tm)
    dw_out_mla = dw_out_p[:N_HEADS * HB].reshape(N_HEADS, HB, D)[:, :V_DIM].reshape(N_HEADS * V_DIM, D)
    dw_out = jnp.concatenate([dw_out_mla, dw_out_p[N_HEADS * HB:]], 0).reshape(N_DEV, D // N_DEV, D).astype(BF)
    dk_att, dv_att, dq_att, p_gu, p_d, p_out = _flash_bwd(
        q_att, k_att, v_att, do_pad, lse, dl, tq,
        ([SCATTER] * 3, [dwgu.reshape(N_DEV, 2 * FF_PAD, D), dwd.reshape(N_DEV, FF_PAD, D), dw_out]))
    dxp_m, dw_uq_p, dw_uk_p, dw_uv_p, d_gq, d_gkv = _mla_prep_bwd(
        xp_m, tabs, dq_att, dk_att, dv_att, mla_q_norm, mla_kv_norm, w_uq_p, w_uk_p, w_uv_p, tm)
    dw_uq = _unpad_heads(dw_uq_p, HB, QK_DIM).reshape((N_DEV,) + uq_sh).astype(BF)
    dw_ukv = jnp.concatenate([_unpad_heads(dw_uk_p, HB, NOPE).reshape(KV_RANK, N_HEADS, NOPE),
                              _unpad_heads(dw_uv_p, HB, V_DIM).reshape(KV_RANK, N_HEADS, V_DIM)], -1)
    *dxp_h, d_lbl, d_ghn, p_uq, dw_ukv_all = _hgrn_bwd(
        xp_h, o_raw, s_all, a_all, b_all, d_ohg, hgrn_lb_logits, hgrn_out_norm,
        ([SCATTER, GATHER], [dw_uq, dw_ukv.reshape(KV_RANK, N_HEADS * HB)]))
    dw_in_a, = _in_bwd_w("in_bwd_w_a", u, dxp_m, dxp_h, tm, 0)
    dw_in_b, p_in_a = _in_bwd_w("in_bwd_w_b", u, dxp_m, dxp_h, tm, 1, ([SCATTER], [dw_in_a]))
    grad_x, d_pre, p_in_b = _in_bwd_x(xs, dxp_m, dxp_h, dh1, w_in_al, attn_pre_norm, tm, ([SCATTER], [dw_in_b]))
    d_on = _unpad_heads(d_on_p, HB, V_DIM)

    ukv2 = lambda a: a.reshape(KV_RANK, N_HEADS * HB)
    vecs = [d_pre, d_gq, d_gkv, d_on, d_lbl, d_ghn, d_post, d_fpre, d_fpost, loss_row]
    small_w = [attn_pre_norm, mla_q_norm, mla_kv_norm, ukv2(mla_w_ukv), mla_out_norm, hgrn_lb_logits, hgrn_out_norm,
               attn_post_norm, ffn_pre_norm, ffn_post_norm]
    small_m = [m_attn_pre_norm, m_mla_q_norm, m_mla_kv_norm, ukv2(m_mla_w_ukv), m_mla_out_norm, m_hgrn_lb_logits,
               m_hgrn_out_norm, m_attn_post_norm, m_ffn_pre_norm, m_ffn_post_norm]
    small_v = [v_attn_pre_norm, v_mla_q_norm, v_mla_kv_norm, ukv2(v_mla_w_ukv), v_mla_out_norm, v_hgrn_lb_logits,
               v_hgrn_out_norm, v_attn_post_norm, v_ffn_pre_norm, v_ffn_post_norm]
    rall = _final_exchange(vecs)
    s_g, s_d, s_m, s_v, loss_all = _small_adam(rall, dw_ukv_all, 3, small_w, small_m, small_v)
    r_in = _shard_adam("adam_w_in", [p_in_a, p_in_b], w_in[0], m_w_in[0], v_w_in[0], 256)
    r_uq = _shard_adam("adam_w_uq", [p_uq], mla_w_uq[0].reshape(uq_sh), m_mla_w_uq[0].reshape(uq_sh),
                       v_mla_w_uq[0].reshape(uq_sh), uq_sh[0])
    r_out = _shard_adam("adam_w_out", [p_out], w_out[0], m_w_out[0], v_w_out[0], D // N_DEV)
    r_g, r_u = _gate_up_adam(p_gu, (w_gate[0].T, w_up[0].T), (m_w_gate[0].T, m_w_up[0].T),
                             (v_w_gate[0].T, v_w_up[0].T))
    r_g, r_u = [a.T for a in r_g], [a.T for a in r_u]
    r_d = _shard_adam("adam_w_down", [p_d], w_down[0], m_w_down[0], v_w_down[0], FF_SH // 2)

    loss = loss_all[0, 0]

    def assemble(big, small):
        b_in, b_uq, b_out, b_g, b_u, b_d = big
        return [small[0], b_in[None], small[1], b_uq.reshape(mla_w_uq.shape), small[2],
                small[3].reshape(mla_w_ukv.shape), small[4], small[5], small[6], b_out[None], small[7], small[8],
                b_g[None], b_u[None], b_d[None], small[9]]

    outs = [loss, grad_x[None]]
    for idx, small in enumerate((s_g, s_d, s_m, s_v)):
        outs += assemble([r[idx] for r in (r_in, r_uq, r_out, r_g, r_u, r_d)], small)
    return tuple(outs)
```

```python
import jax
import jax.numpy as jnp
from jax import lax
from jax.experimental import pallas as pl
from jax.experimental.pallas import tpu as pltpu

BF = jnp.bfloat16
F32 = jnp.float32
MESH = pl.DeviceIdType.MESH

N_DEV = 8
D = 1024
EPS = 1e-6
LOG2E = 1.4426950408889634
ROPE_THETA = 10000.0
N_HEADS = 8
HB = 128
NOPE = 64
ROPE = 32
V_DIM = 64
QK_DIM = NOPE + ROPE
Q_RANK = 384
KV_RANK = 128
KR_PAD = 128
MLA_IN = Q_RANK + KV_RANK + KR_PAD
G_HEADS = 4
G_DIM = 128
G_W = G_HEADS * G_DIM
CHUNK = 64
SUB = 16
XP_W = MLA_IN + 4 * G_W
IN_SH = 324
FF_SH = 352
FF_PAD = 384
MIX_W = N_HEADS * HB + G_W

ADAM_LR = 0.001
ADAM_B1 = 0.9
ADAM_B2 = 0.999
ADAM_EPS = 1e-08
ADAM_WD = 0.01
ADAM_STEP = 10

_TM = 512
_TMF = 1024
_TQ = 512
_AH = 2
_AH_FWD = 4
_FB = 768
_TB = 1024
_TB_BWD = 512
_HP = 4
V7X_VMEM_BYTES = 64 * 1024 * 1024
_VMEM_LIMIT = V7X_VMEM_BYTES - 8 * 1024 * 1024
NEG = -1e30


def _dot(a, b):
    return jnp.dot(a.astype(BF), b.astype(BF), preferred_element_type=F32)


def _dot_nt(a, b):
    return lax.dot_general(a.astype(BF), b.astype(BF), (((1,), (1,)), ((), ())), preferred_element_type=F32)


def _dot_tn(a, b):
    return lax.dot_general(a.astype(BF), b.astype(BF), (((0,), (0,)), ((), ())), preferred_element_type=F32)


def _sigmoid(x):
    return 1.0 / (1.0 + jnp.exp(-x))


def _rms(x, n):
    r = lax.rsqrt(jnp.sum(x * x, -1, keepdims=True) * (1.0 / n) + EPS)
    return x * r, r


def _rms_bwd(nx, r, g, dy, n):
    dg = jnp.sum(dy * nx, 0, keepdims=True)
    dn = dy * g
    dx = r * (dn - nx * (jnp.sum(dn * nx, -1, keepdims=True) * (1.0 / n)))
    return dx, dg


def _adamw(w, g, m, v):
    m2 = ADAM_B1 * m + (1.0 - ADAM_B1) * g
    v2 = ADAM_B2 * v + (1.0 - ADAM_B2) * (g * g)
    m_hat = m2 / (1.0 - ADAM_B1 ** ADAM_STEP)
    v_hat = v2 / (1.0 - ADAM_B2 ** ADAM_STEP)
    delta = -ADAM_LR * (m_hat / (jnp.sqrt(v_hat) + ADAM_EPS) + ADAM_WD * w)
    return delta, m2, v2


def _pcall(body, name, grid, in_specs, out_specs, out_shape, scratch=(), exchange=None):
    scratch = list(scratch)
    extra = ()
    if exchange is not None:
        kinds, extra = exchange
        in_specs, out_specs, out_shape = list(in_specs), list(out_specs), list(out_shape)
        n_in, n_out, n_scr, n_x = len(in_specs), len(out_specs), len(scratch), len(extra)
        inner = body

        def body(*refs):
            ins, rest = refs[:n_in], refs[n_in:]
            x_src, rest = rest[:n_x], rest[n_x:]
            outs, rest = rest[:n_out], rest[n_out:]
            x_dst, rest = rest[:n_x], rest[n_x:]
            ex = _Exchange(kinds, x_src, x_dst, *rest[n_scr:])
            first = pl.program_id(0) == 0
            last = pl.program_id(0) == grid[0] - 1
            for a in range(1, len(grid)):
                first = first & (pl.program_id(a) == 0)
                last = last & (pl.program_id(a) == grid[a] - 1)
            pl.when(first)(ex.start)
            inner(*ins, *outs, *rest[:n_scr])
            pl.when(last)(ex.wait)

        in_specs += [_HBM] * n_x
        out_specs += [_HBM] * n_x
        out_shape += _exchange_shapes(kinds, extra)
        scratch += _exchange_sems(n_x)
    call = pl.pallas_call(
        body, name=name, grid=grid, in_specs=in_specs, out_specs=out_specs, out_shape=out_shape,
        scratch_shapes=scratch,
        compiler_params=pltpu.CompilerParams(
            dimension_semantics=("arbitrary",) * len(grid), vmem_limit_bytes=_VMEM_LIMIT))
    return lambda *operands: call(*operands, *extra)


def _full(shape):
    return pl.BlockSpec(shape, lambda *_: (0,) * len(shape))


def _rows(tm, n):
    return pl.BlockSpec((tm, n), lambda i, *_: (i, 0))


def _sds(shape, dtype=F32):
    return jax.ShapeDtypeStruct(shape, dtype)


def _peer(k, x, y, c):
    px = 1 - x if (k >> 2) & 1 else x
    py = 1 - y if (k >> 1) & 1 else y
    pc = 1 - c if k & 1 else c
    return px, py, pc


GATHER, SCATTER = "gather", "scatter"


class _Exchange:
    def __init__(self, kinds, srcs, dsts, send_sems, recv_sems, loc_sems):
        self.kinds, self.srcs, self.dsts = kinds, srcs, dsts
        self.send_sems, self.recv_sems, self.loc_sems = send_sems, recv_sems, loc_sems
        self.x, self.y, self.c = lax.axis_index("x"), lax.axis_index("y"), lax.axis_index("c")
        self.me = 4 * self.x + 2 * self.y + self.c

    def _src(self, w, slot):
        return self.srcs[w] if self.kinds[w] == GATHER else self.srcs[w].at[slot]

    def _dst(self, w, slot):
        return self.dsts[w].at[slot]

    def _copy(self, w, k, outgoing):
        px, py, pc = _peer(k, self.x, self.y, self.c)
        pid = 4 * px + 2 * py + pc
        return pltpu.make_async_remote_copy(
            src_ref=self._src(w, pid if outgoing else self.me),
            dst_ref=self._dst(w, self.me if outgoing else pid),
            send_sem=self.send_sems.at[w, k - 1], recv_sem=self.recv_sems.at[w, k - 1],
            device_id=(px, py, pc), device_id_type=MESH)

    def _local(self, w):
        return pltpu.make_async_copy(self._src(w, self.me), self._dst(w, self.me), self.loc_sems.at[w])

    def start(self):
        for w in range(len(self.srcs)):
            self._local(w).start()
            for k in range(1, N_DEV):
                self._copy(w, k, True).start()

    def wait(self):
        for w in range(len(self.srcs)):
            self._local(w).wait()
            for k in range(1, N_DEV):
                self._copy(w, k, False).wait_recv()
        for w in range(len(self.srcs)):
            for k in range(1, N_DEV):
                self._copy(w, k, True).wait_send()


def _exchange_sems(n_w):
    return [pltpu.SemaphoreType.DMA((n_w, N_DEV - 1)), pltpu.SemaphoreType.DMA((n_w, N_DEV - 1)),
            pltpu.SemaphoreType.DMA((n_w,))]


def _exchange_shapes(kinds, srcs):
    return [_sds(((N_DEV,) if kd == GATHER else ()) + tuple(s.shape), s.dtype) for kd, s in zip(kinds, srcs)]


_HBM = pl.BlockSpec(memory_space=pl.ANY)


def _cast_shards(w_in, w_uq, w_out, w_gate_t, w_up_t, w_down):
    shapes = [(D, IN_SH), (Q_RANK // N_DEV, N_HEADS * QK_DIM), (D // N_DEV, D), (2 * FF_PAD, D), (FF_PAD, D)]

    def body(win, wuq, wout, wg, wu, wd, sin_, suq, sout, sgu, sd):
        sin_[...] = win[...].astype(BF)
        suq[...] = wuq[...].astype(BF)
        sout[...] = wout[...].astype(BF)
        sgu[...] = jnp.zeros(sgu.shape, BF)
        sgu[0:FF_SH, :] = wg[...].astype(BF)
        sgu[FF_PAD:FF_PAD + FF_SH, :] = wu[...].astype(BF)
        sd[...] = jnp.zeros(sd.shape, BF)
        sd[0:FF_SH, :] = wd[...].astype(BF)

    vm = pl.BlockSpec(memory_space=pltpu.VMEM)
    return pl.pallas_call(
        body, name="cast_shards", in_specs=[vm] * 6, out_specs=[vm] * 5,
        out_shape=[_sds(s, BF) for s in shapes],
        compiler_params=pltpu.CompilerParams(vmem_limit_bytes=_VMEM_LIMIT),
    )(w_in, w_uq, w_out, w_gate_t, w_up_t, w_down)


def _gather_two_level(name, srcs):
    n_w = len(srcs)

    def body(*refs):
        src, dst = refs[:n_w], refs[n_w:2 * n_w]
        send_sems, recv_sems, loc_sems = refs[2 * n_w:]
        x, y, c = lax.axis_index("x"), lax.axis_index("y"), lax.axis_index("c")
        me, sibling = (x, y, c), (x, y, 1 - c)
        chips = [(1 - x, y), (x, 1 - y), (1 - x, 1 - y)]
        slot = lambda p: 4 * p[0] + 2 * p[1] + p[2]

        def copy(w, k, block, to, own=False):
            return pltpu.make_async_remote_copy(
                src_ref=src[w] if own else dst[w].at[slot(block)], dst_ref=dst[w].at[slot(block)],
                send_sem=send_sems.at[w, k], recv_sem=recv_sems.at[w, k], device_id=to, device_id_type=MESH)

        local = [pltpu.make_async_copy(src[w], dst[w].at[slot(me)], loc_sems.at[w]) for w in range(n_w)]
        first, passed = [], []
        for w in range(n_w):
            local[w].start()
            first.append(copy(w, 0, me, sibling, own=True))
            first += [copy(w, 1 + j, me, (*chip, c), own=True) for j, chip in enumerate(chips)]
        for cp in first:
            cp.start()
        for w in range(n_w):
            for j, chip in enumerate(chips):
                copy(w, 1 + j, (*chip, c), me).wait_recv()
                passed.append(copy(w, 4 + j, (*chip, c), sibling))
                passed[-1].start()
        for w in range(n_w):
            copy(w, 0, sibling, me).wait_recv()
            for j, chip in enumerate(chips):
                copy(w, 4 + j, (*chip, 1 - c), me).wait_recv()
        for cp in first + passed:
            cp.wait_send()
        for w in range(n_w):
            local[w].wait()

    return pl.pallas_call(
        body, name=name, in_specs=[_HBM] * n_w, out_specs=[_HBM] * n_w,
        out_shape=_exchange_shapes([GATHER] * n_w, srcs), scratch_shapes=_exchange_sems(n_w))(*srcs)


def _row_offsets(arrays):
    offs, rows = [], 0
    for a in arrays:
        offs.append(rows)
        rows += a.shape[0]
    return offs, -(-rows // 8) * 8


def _final_exchange(vecs):
    n_p = len(vecs)
    offs, rows = _row_offsets(vecs)

    def body(*refs):
        g_refs = refs[:n_p]
        rall, pk, send_sems, recv_sems, loc_sem = refs[n_p:]
        x, y, c = lax.axis_index("x"), lax.axis_index("y"), lax.axis_index("c")
        me = 4 * x + 2 * y + c
        pk[...] = jnp.zeros(pk.shape, F32)
        for p in range(n_p):
            r, n = g_refs[p].shape
            pk[offs[p]:offs[p] + r, 0:n] = g_refs[p][...]

        def remote(k):
            return pltpu.make_async_remote_copy(
                src_ref=pk, dst_ref=rall.at[me], send_sem=send_sems.at[k - 1], recv_sem=recv_sems.at[k - 1],
                device_id=_peer(k, x, y, c), device_id_type=MESH)

        def arrival(k):
            px, py, pc = _peer(k, x, y, c)
            return pltpu.make_async_remote_copy(
                src_ref=pk, dst_ref=rall.at[4 * px + 2 * py + pc], send_sem=send_sems.at[k - 1],
                recv_sem=recv_sems.at[k - 1], device_id=(px, py, pc), device_id_type=MESH)

        local = pltpu.make_async_copy(pk, rall.at[me], loc_sem)
        local.start()
        for k in range(1, N_DEV):
            remote(k).start()
        local.wait()
        for k in range(1, N_DEV):
            arrival(k).wait_recv()
        for k in range(1, N_DEV):
            remote(k).wait_send()

    vm = pl.BlockSpec(memory_space=pltpu.VMEM)
    return pl.pallas_call(
        body, name="final_exchange", in_specs=[vm] * n_p, out_specs=vm, out_shape=_sds((N_DEV, rows, D)),
        scratch_shapes=[pltpu.VMEM((rows, D), F32),
                        pltpu.SemaphoreType.DMA((N_DEV - 1,)), pltpu.SemaphoreType.DMA((N_DEV - 1,)),
                        pltpu.SemaphoreType.DMA],
    )(*vecs)


def _small_adam(rall, big_parts, big, ws, ms, vs):
    n_p = len(ws)
    packed = [w for p, w in enumerate(ws) if p != big] + [jax.ShapeDtypeStruct((1, HB), F32)]
    offs, _ = _row_offsets(packed)
    offs = offs[:big] + [None] + offs[big:]

    def total(ref, sl):
        g = ref[(0,) + sl]
        for j in range(1, N_DEV):
            g = g + ref[(j,) + sl]
        return g

    def body(*refs):
        rall_ref, big_ref = refs[:2]
        w_refs, m_refs, v_refs = refs[2:2 + n_p], refs[2 + n_p:2 + 2 * n_p], refs[2 + 2 * n_p:2 + 3 * n_p]
        outs = refs[2 + 3 * n_p:]
        for p in range(n_p):
            r, n = w_refs[p].shape
            if p == big:
                g = total(big_ref, (slice(0, r), slice(0, n)))
            else:
                g = total(rall_ref, (slice(offs[p], offs[p] + r), slice(0, n)))
            delta, m2, v2 = _adamw(w_refs[p][...], g, m_refs[p][...], v_refs[p][...])
            outs[p][...] = g
            outs[n_p + p][...] = delta
            outs[2 * n_p + p][...] = m2
            outs[3 * n_p + p][...] = v2
        outs[4 * n_p][...] = total(rall_ref, (slice(offs[n_p], offs[n_p] + 1), slice(0, HB)))

    vm = pl.BlockSpec(memory_space=pltpu.VMEM)
    res = pl.pallas_call(
        body, name="small_adam", in_specs=[vm] * (2 + 3 * n_p), out_specs=[vm] * (4 * n_p + 1),
        out_shape=[_sds(w.shape) for w in ws] * 4 + [_sds((1, HB))],
        compiler_params=pltpu.CompilerParams(vmem_limit_bytes=_VMEM_LIMIT),
    )(rall, big_parts, *ws, *ms, *vs)
    return res[:n_p], res[n_p:2 * n_p], res[2 * n_p:3 * n_p], res[3 * n_p:4 * n_p], res[4 * n_p]


def _device_sum(p_ref):
    g = p_ref[0].astype(F32)
    for j in range(1, N_DEV):
        g = g + p_ref[j].astype(F32)
    return g


def _shard_adam(name, parts, w, m, v, tr):
    a0, b0 = w.shape
    n_p = len(parts)
    b = parts[0].shape[2]
    first = [0]
    for p in parts:
        first.append(first[-1] + p.shape[1] // tr)

    def body(*refs):
        p_refs = refs[:n_p]
        w_ref, m_ref, v_ref, g_out, d_out, m_out, v_out = refs[n_p:]
        i = pl.program_id(0)
        g = _device_sum(p_refs[0])
        for k in range(1, n_p):
            g = jnp.where(i >= first[k], _device_sum(p_refs[k]), g)
        g = g[:, 0:b0]
        delta, m2, v2 = _adamw(w_ref[...], g, m_ref[...], v_ref[...])
        g_out[...] = g
        d_out[...] = delta
        m_out[...] = m2
        v_out[...] = v2

    def part_spec(k):
        last = first[k + 1] - first[k] - 1
        return pl.BlockSpec((N_DEV, tr, b), lambda i: (0, jnp.minimum(jnp.maximum(i - first[k], 0), last), 0))

    blk = pl.BlockSpec((tr, b0), lambda i: (i, 0))
    return _pcall(
        body, name, (a0 // tr,), [part_spec(k) for k in range(n_p)] + [blk, blk, blk],
        [blk] * 4, [_sds((a0, b0))] * 4)(*parts, w, m, v)


def _gate_up_adam(parts, ws, ms, vs):
    tc = 256

    def body(p_ref, wg, wu, mg, mu, vg, vu, *outs):
        g = _device_sum(p_ref)
        for k, (w_ref, m_ref, v_ref) in enumerate(((wg, mg, vg), (wu, mu, vu))):
            gk = g[k * FF_PAD:k * FF_PAD + FF_SH]
            delta, m2, v2 = _adamw(w_ref[...], gk, m_ref[...], v_ref[...])
            for o, val in zip(outs[4 * k:4 * k + 4], (gk, delta, m2, v2)):
                o[...] = val

    blk = pl.BlockSpec((FF_SH, tc), lambda i: (0, i))
    res = _pcall(
        body, "adam_w_gate_up", (D // tc,), [pl.BlockSpec((N_DEV, 2 * FF_PAD, tc), lambda i: (0, 0, i))] + [blk] * 6,
        [blk] * 8, [_sds((FF_SH, D))] * 8)(parts, *ws, *ms, *vs)
    return res[:4], res[4:]


def _fwd_in(x, g_pre, w_in_al, tm):
    T = x.shape[0]

    def body(x_ref, g_ref, w_ref, xm_ref, xh_ref, u_ref):
        nx, _ = _rms(x_ref[...], D)
        u = (nx * g_ref[...]).astype(BF)
        u_ref[...] = u
        xm_ref[...] = jnp.dot(u, w_ref[:, 0:MLA_IN], preferred_element_type=F32)
        xh_ref[...] = jnp.dot(u, w_ref[:, MLA_IN:XP_W], preferred_element_type=F32)

    return _pcall(body, "fwd_in", (T // tm,),
                  [_rows(tm, D), _full((1, D)), _full((D, XP_W))],
                  [_rows(tm, MLA_IN), _rows(tm, 4 * G_W), _rows(tm, D)],
                  [_sds((T, MLA_IN)), _sds((T, 4 * G_W)), _sds((T, D), BF)])(x, g_pre, w_in_al)


def _rope(blk, ta, tb1, tb2):
    return blk * ta + pltpu.roll(blk, HB - ROPE // 2, 1) * tb1 + pltpu.roll(blk, ROPE // 2, 1) * tb2


def _unrope(d, ta, tb1, tb2):
    return d * ta + pltpu.roll(d * tb1, ROPE // 2, 1) + pltpu.roll(d * tb2, HB - ROPE // 2, 1)


def _mla_prep(xp, tabs, g_q, g_kv, w_uq, w_uk, w_uv, tm):
    T = xp.shape[0]
    W = N_HEADS * HB

    def body(xp_ref, ta_ref, tb1_ref, tb2_ref, gq_ref, gkv_ref, wuq_ref, wuk_ref, wuv_ref, q_ref, qs_ref, k_ref, v_ref):
        ta, tb1, tb2 = ta_ref[...], tb1_ref[...], tb2_ref[...]
        nq, _ = _rms(xp_ref[:, 0:Q_RANK], Q_RANK)
        nkv, _ = _rms(xp_ref[:, Q_RANK:Q_RANK + KV_RANK], KV_RANK)
        nkv = (nkv * gkv_ref[...]).astype(BF)
        qpre = _dot(nq * gq_ref[...], wuq_ref[...])
        kpre = jnp.dot(nkv, wuk_ref[...], preferred_element_type=F32)
        v = jnp.dot(nkv, wuv_ref[...], preferred_element_type=F32)
        lane = lax.broadcasted_iota(jnp.int32, (tm, W), 1)
        v_ref[...] = jnp.where((lane & (HB - 1)) == V_DIM, 1.0, v).astype(BF)
        kr = _rope(pltpu.roll(xp_ref[:, Q_RANK + KV_RANK:MLA_IN], NOPE, 1), ta, tb1, tb2)
        for h in range(N_HEADS):
            sl = slice(h * HB, (h + 1) * HB)
            qr = _rope(qpre[:, sl], ta, tb1, tb2)
            q_ref[:, sl] = qr.astype(BF)
            qs_ref[:, sl] = (qr * (QK_DIM ** -0.5 * LOG2E)).astype(BF)
            k_ref[:, sl] = (kpre[:, sl] + kr).astype(BF)

    tab = _rows(tm, HB)
    return _pcall(body, "mla_prep", (T // tm,),
                  [_rows(tm, MLA_IN), tab, tab, tab, _full((1, Q_RANK)), _full((1, KV_RANK)),
                   _full((Q_RANK, W)), _full((KV_RANK, W)), _full((KV_RANK, W))],
                  [_rows(tm, W)] * 4, [_sds((T, W), BF)] * 4)(xp, *tabs, g_q, g_kv, w_uq, w_uk, w_uv)


def _flash_fwd(q, k, v, tq, exchange=None):
    T = q.shape[0]
    hp = _AH_FWD
    W = hp * HB

    def body(q_ref, k_ref, v_ref, o_ref, lse_ref):
        i = pl.program_id(1)

        def blk(j, carry, masked):
            st = pl.multiple_of(j * tq, tq)
            out = []
            for h in range(hp):
                ls = slice(h * HB, (h + 1) * HB)
                m, acc = carry[h]
                s = _dot_nt(q_ref[:, ls], k_ref[pl.ds(st, tq), ls])
                if masked:
                    r = lax.broadcasted_iota(jnp.int32, (tq, tq), 0)
                    c = lax.broadcasted_iota(jnp.int32, (tq, tq), 1)
                    s = jnp.where(c <= r, s, NEG)
                m2 = jnp.maximum(m, jnp.max(s, -1, keepdims=True))
                p = jnp.exp2(s - m2)
                out.append((m2, jnp.exp2(m - m2) * acc + _dot(p, v_ref[pl.ds(st, tq), ls])))
            return tuple(out)

        init = tuple((jnp.full((tq, 1), NEG, F32), jnp.zeros((tq, HB), F32)) for _ in range(hp))
        carry = lax.fori_loop(0, i, lambda j, cr: blk(j, cr, False), init)
        res = blk(i, carry, True)
        lane = lax.broadcasted_iota(jnp.int32, (tq, HB), 1)
        for h in range(hp):
            ls = slice(h * HB, (h + 1) * HB)
            m, acc = res[h]
            l = acc[:, V_DIM:V_DIM + 1]
            o_ref[:, ls] = jnp.where(lane < V_DIM, acc / l, 0.0).astype(BF)
            lse_ref[:, ls] = jnp.broadcast_to(m * (1.0 / LOG2E) + jnp.log(l), (tq, HB))

    qs = pl.BlockSpec((tq, W), lambda h, i: (i, h))
    kvs = pl.BlockSpec((T, W), lambda h, i: (0, h))
    return _pcall(body, "flash_fwd", (N_HEADS // hp, T // tq), [qs, kvs, kvs], [qs, qs],
                  [_sds((T, N_HEADS * HB), BF), _sds((T, N_HEADS * HB))], exchange=exchange)(q, k, v)


def _gates(hq, hf, lb):
    sig = _sigmoid(hf)
    f = lb + (1.0 - lb) * sig
    sq = _sigmoid(hq)
    return hq * sq, 1.0 - f, f, jnp.log(f), sig, sq


def _lower_bound(lbl_ref):
    l0, l1 = lbl_ref[0:1, :], lbl_ref[1:2, :]
    mx = jnp.maximum(l0, l1)
    e0, e1 = jnp.exp(l0 - mx), jnp.exp(l1 - mx)
    return e0 / (e0 + e1)


def _split3(x):
    hi = x.astype(BF)
    r1 = x - hi.astype(F32)
    mid = r1.astype(BF)
    lo = (r1 - mid.astype(F32)).astype(BF)
    return hi, mid, lo


def _tri_mm(tri, x):
    hi, mid, lo = _split3(x)
    mm = lambda t: jnp.dot(tri, t, preferred_element_type=F32)
    return mm(hi) + mm(mid) + mm(lo)


def _intra_codes(sub):
    row = lax.broadcasted_iota(jnp.int32, (CHUNK, CHUNK), 0)
    col = lax.broadcasted_iota(jnp.int32, (CHUNK, CHUNK), 1)
    return sub, row, col


def _intra(q, k, b2, b_s, codes, da=None):
    grad = da is not None
    pow2 = (lambda x: jnp.exp2(jnp.minimum(x, 0.0))) if grad else jnp.exp2
    sub, row, col = codes
    a = jnp.zeros((CHUNK, CHUNK), F32)
    dq = jnp.zeros((CHUNK, G_DIM), F32)
    dk = jnp.zeros((CHUNK, G_DIM), F32)
    for i in range(1, CHUNK // sub):
        b0 = b_s[sub * i - 1:sub * i, :]
        eq, ek = pow2(b2 - b0), pow2(b0 - b2)
        mask = ((row // sub) == i) & (col < sub * i)
        if grad:
            dai = jnp.where(mask, da, 0.0)
            dq = dq + _dot(dai, k * ek) * eq
            dk = dk + _dot_tn(dai, q * eq) * ek
        else:
            a = jnp.where(mask, _dot_nt(q * eq, k * ek), a)
    for d in range(sub):
        ksh = pltpu.roll(k, d, 0) if d else k
        bsh = pltpu.roll(b2, d, 0) if d else b2
        e = pow2(b2 - bsh)
        mask = (col == row - d) & ((row & (sub - 1)) >= d)
        if grad:
            g = jnp.sum(jnp.where(mask, da, 0.0), -1, keepdims=True) * e
            dq = dq + g * ksh
            cb = g * q
            dk = dk + (pltpu.roll(cb, CHUNK - d, 0) if d else cb)
        else:
            a = jnp.where(mask, jnp.sum(q * ksh * e, -1, keepdims=True), a)
    return (dq, dk) if grad else a


def _hgrn_fwd(xp, lb_logits, g_hn, exchange=None):
    T = xp.shape[0]
    tb = min(_TB, T)
    ncb = tb // CHUNK
    hp = _HP
    W = hp * G_DIM

    def body(hq_ref, hf_ref, hi_ref, hg_ref, lbl_ref, ghn_ref, out_ref, oraw_ref, sall_ref, aall_ref, ball_ref,
             st_ref, b_s):
        lb_all = _lower_bound(lbl_ref)

        @pl.when(pl.program_id(1) == 0)
        def _():
            st_ref[...] = jnp.zeros(st_ref.shape, F32)

        row = lax.broadcasted_iota(jnp.int32, (CHUNK, CHUNK), 0)
        col = lax.broadcasted_iota(jnp.int32, (CHUNK, CHUNK), 1)
        tri = (col <= row).astype(BF)
        codes = _intra_codes(SUB)

        def chunk(c, carry):
            sl = pl.ds(pl.multiple_of(c * CHUNK, CHUNK), CHUNK)
            for h in range(hp):
                ls = slice(h * G_DIM, (h + 1) * G_DIM)
                q, k, _, lf, _, _ = _gates(hq_ref[sl, ls], hf_ref[sl, ls], lb_all[:, ls])
                v = hi_ref[sl, ls]
                b2 = _tri_mm(tri, lf) * LOG2E
                b_s[h] = b2
                ball_ref[sl, ls] = b2
                st = st_ref[h]
                sall_ref[c, h] = st
                a = _intra(q, k, b2, b_s.at[h], codes)
                aall_ref[c, h] = a
                o = _dot_nt(q * jnp.exp2(b2), st) + _dot(a, v)
                bl = b_s[h, CHUNK - 1:CHUNK, :]
                st_ref[h] = st * jnp.exp2(bl) + _dot_tn(v, k * jnp.exp2(bl - b2))
                oraw_ref[sl, ls] = o
                n, _ = _rms(o, G_DIM)
                hg = hg_ref[sl, ls]
                out_ref[sl, ls] = n * ghn_ref[:, ls] * (hg * _sigmoid(hg))
            return carry

        lax.fori_loop(0, ncb, chunk, 0, unroll=4)

    col_blk = lambda j: pl.BlockSpec((tb, W), lambda p, t: (t, j * (G_HEADS // hp) + p))
    head = pl.BlockSpec((tb, W), lambda p, t: (t, p))
    return _pcall(
        body, "hgrn_fwd", (G_HEADS // hp, T // tb),
        [col_blk(0), col_blk(1), col_blk(2), col_blk(3),
         pl.BlockSpec((2, W), lambda p, t: (0, p)), pl.BlockSpec((1, W), lambda p, t: (0, p))],
        [head, head, pl.BlockSpec((ncb, hp, G_DIM, G_DIM), lambda p, t: (t, p, 0, 0)),
         pl.BlockSpec((ncb, hp, CHUNK, CHUNK), lambda p, t: (t, p, 0, 0)), head],
        [_sds((T, G_W)), _sds((T, G_W)), _sds((T // CHUNK, G_HEADS, G_DIM, G_DIM)),
         _sds((T // CHUNK, G_HEADS, CHUNK, CHUNK)), _sds((T, G_W))],
        scratch=[pltpu.VMEM((hp, G_DIM, G_DIM), F32), pltpu.VMEM((hp, CHUNK, G_DIM), F32)], exchange=exchange,
    )(xp, xp, xp, xp, lb_logits, g_hn)


def _fwd_out(o_pad, o_hgrn, x, g_on, w_out, g_post, g_fpre, tm):
    T = x.shape[0]

    def body(o_ref, oh_ref, x_ref, gon_ref, w_ref, gpost_ref, gfpre_ref, h1_ref, y1_ref, z_ref, mix_ref):
        for h in range(N_HEADS):
            sl = slice(h * HB, (h + 1) * HB)
            n, _ = _rms(o_ref[:, sl].astype(F32), V_DIM)
            mix_ref[:, sl] = (n * gon_ref[:, sl]).astype(BF)
        mix_ref[:, N_HEADS * HB:MIX_W] = oh_ref[...].astype(BF)
        y1 = jnp.dot(mix_ref[...], w_ref[...], preferred_element_type=F32)
        y1_ref[...] = y1
        ny, _ = _rms(y1, D)
        h1 = x_ref[...] + ny * gpost_ref[...]
        h1_ref[...] = h1
        nh, _ = _rms(h1, D)
        z_ref[...] = (nh * gfpre_ref[...]).astype(BF)

    return _pcall(body, "fwd_out", (T // tm,),
                  [_rows(tm, N_HEADS * HB), _rows(tm, G_W), _rows(tm, D), _full((1, N_HEADS * HB)),
                   _full((MIX_W, D)), _full((1, D)), _full((1, D))],
                  [_rows(tm, D), _rows(tm, D), _rows(tm, D), _rows(tm, MIX_W)],
                  [_sds((T, D)), _sds((T, D)), _sds((T, D), BF), _sds((T, MIX_W), BF)],
                  )(o_pad, o_hgrn, x, g_on, w_out, g_post, g_fpre)


def _ffn_fwd(z, wgu, wd, h1, tgt, g_fpost, tm, nd):
    T = z.shape[0]
    fb = nd * FF_PAD
    nf = wd.shape[0] // fb

    def body(z_ref, wgu_ref, wd_ref, h1_ref, t_ref, gp_ref,
             as_ref, bs_ref, ff_ref, dh2_ref, dy2_ref, dgp_ref, loss_ref, acc):
        i, j = pl.program_id(0), pl.program_id(1)
        gu = _dot_nt(z_ref[...], wgu_ref[...])
        piece = lambda n: gu[:, n * FF_PAD:(n + 1) * FF_PAD]
        g = piece(0) if nd == 1 else jnp.concatenate([piece(2 * n) for n in range(nd)], 1)
        u = piece(1) if nd == 1 else jnp.concatenate([piece(2 * n + 1) for n in range(nd)], 1)
        s = _sigmoid(g)
        b = g * s
        ff = (b * u).astype(BF)
        as_ref[...] = (u * _dsilu(g, s)).astype(BF)
        bs_ref[...] = b.astype(BF)
        ff_ref[...] = ff
        part = jnp.dot(ff, wd_ref[...], preferred_element_type=F32)

        @pl.when(j == 0)
        def _():
            acc[...] = part

        @pl.when(j > 0)
        def _():
            acc[...] += part

        @pl.when((i == 0) & (j == 0))
        def _():
            dgp_ref[...] = jnp.zeros(dgp_ref.shape, F32)
            loss_ref[...] = jnp.zeros(loss_ref.shape, F32)

        @pl.when(j == nf - 1)
        def _():
            ny, r = _rms(acc[...], D)
            err = h1_ref[...] + ny * gp_ref[...] - t_ref[...]
            loss_ref[...] += 0.5 * jnp.sum(jnp.sum(err * err, -1, keepdims=True) * (1.0 / D), 0, keepdims=True)
            dh2 = err * (1.0 / D)
            dh2_ref[...] = dh2
            dy2, dgp = _rms_bwd(ny, r, gp_ref[...], dh2, D)
            dy2_ref[...] = dy2.astype(BF)
            dgp_ref[...] += dgp

    tok = lambda n: pl.BlockSpec((tm, n), lambda i, j: (i, 0))
    col = pl.BlockSpec((tm, fb), lambda i, j: (i, j))
    return _pcall(
        body, "ffn_fwd", (T // tm, nf),
        [tok(D), pl.BlockSpec((2 * fb, D), lambda i, j: (j, 0)), pl.BlockSpec((fb, D), lambda i, j: (j, 0)),
         tok(D), tok(D), _full((1, D))],
        [col, col, col, tok(D), tok(D), _full((1, D)), _full((1, HB))],
        [_sds((T, nf * fb), BF)] * 3 + [_sds((T, D)), _sds((T, D), BF), _sds((1, D)), _sds((1, HB))],
        scratch=[pltpu.VMEM((tm, D), F32)],
    )(z, wgu, wd, h1, tgt, g_fpost)


def _dsilu(x, s):
    return s * (1.0 + x * (1.0 - s))


def _ffn_bwd_x(dy2, gs, us, wgu, wd, h1, y1, dh2, g_fpre, g_post, tm):
    T = dy2.shape[0]
    nf = wd.shape[0] // _FB

    def body(dy2_ref, gs_ref, us_ref, wgu_ref, wd_ref, h1_ref, y1_ref, dh2_ref, gf_ref, gp_ref,
             dgu_ref, dh1_ref, dy1_ref, dgf_ref, dgp_ref, acc):
        i, j = pl.program_id(0), pl.program_id(1)
        dff = _dot_nt(dy2_ref[...], wd_ref[...])
        dg = (dff * gs_ref[...].astype(F32)).astype(BF)
        du = (dff * us_ref[...].astype(F32)).astype(BF)
        dgu = jnp.concatenate([dg[:, 0:FF_PAD], du[:, 0:FF_PAD], dg[:, FF_PAD:_FB], du[:, FF_PAD:_FB]], 1)
        dgu_ref[...] = dgu
        part = jnp.dot(dgu, wgu_ref[...], preferred_element_type=F32)

        @pl.when(j == 0)
        def _():
            acc[...] = part

        @pl.when(j > 0)
        def _():
            acc[...] += part

        @pl.when((i == 0) & (j == 0))
        def _():
            dgf_ref[...] = jnp.zeros(dgf_ref.shape, F32)
            dgp_ref[...] = jnp.zeros(dgp_ref.shape, F32)

        @pl.when(j == nf - 1)
        def _():
            nh, rh = _rms(h1_ref[...], D)
            dh, dgf = _rms_bwd(nh, rh, gf_ref[...], acc[...], D)
            dh1 = dh2_ref[...] + dh
            dh1_ref[...] = dh1
            dgf_ref[...] += dgf
            ny, ry = _rms(y1_ref[...], D)
            dy1, dgp = _rms_bwd(ny, ry, gp_ref[...], dh1, D)
            dy1_ref[...] = dy1.astype(BF)
            dgp_ref[...] += dgp

    tok = lambda n: pl.BlockSpec((tm, n), lambda i, j: (i, 0))
    col = pl.BlockSpec((tm, _FB), lambda i, j: (i, j))
    return _pcall(
        body, "ffn_bwd_x", (T // tm, nf),
        [tok(D), col, col, pl.BlockSpec((2 * _FB, D), lambda i, j: (j, 0)), pl.BlockSpec((_FB, D), lambda i, j: (j, 0)),
         tok(D), tok(D), tok(D), _full((1, D)), _full((1, D))],
        [pl.BlockSpec((tm, 2 * _FB), lambda i, j: (i, j)), tok(D), tok(D), _full((1, D)), _full((1, D))],
        [_sds((T, 2 * nf * _FB), BF), _sds((T, D)), _sds((T, D), BF), _sds((1, D)), _sds((1, D))],
        scratch=[pltpu.VMEM((tm, D), F32)],
    )(dy2, gs, us, wgu, wd, h1, y1, dh2, g_fpre, g_post)


def _ffn_bwd_w(z, ffs, dgu, dy2, tm):
    T = z.shape[0]
    nf = ffs.shape[1] // _FB
    nt = T // tm

    def body(z_ref, ff_ref, dgu_ref, dy2_ref, dwgu_ref, dwd_ref, agu, ad):
        i = pl.program_id(1)
        pgu = _dot_tn(dgu_ref[...], z_ref[...])
        pd = _dot_tn(ff_ref[...], dy2_ref[...])

        @pl.when(i == 0)
        def _():
            agu[...] = pgu
            ad[...] = pd

        @pl.when(i > 0)
        def _():
            agu[...] += pgu
            ad[...] += pd

        @pl.when(i == nt - 1)
        def _():
            dwgu_ref[...] = agu[...].astype(BF)
            dwd_ref[...] = ad[...].astype(BF)

    F = nf * _FB
    tok = lambda n: pl.BlockSpec((tm, n), lambda j, i: (i, 0))
    return _pcall(
        body, "ffn_bwd_w", (nf, nt),
        [tok(D), pl.BlockSpec((tm, _FB), lambda j, i: (i, j)), pl.BlockSpec((tm, 2 * _FB), lambda j, i: (i, j)), tok(D)],
        [pl.BlockSpec((2 * _FB, D), lambda j, i: (j, 0)), pl.BlockSpec((_FB, D), lambda j, i: (j, 0))],
        [_sds((2 * F, D), BF), _sds((F, D), BF)],
        scratch=[pltpu.VMEM((2 * _FB, D), F32), pltpu.VMEM((_FB, D), F32)],
    )(z, ffs, dgu, dy2)


def _out_bwd(dy1, mix, o_pad, w_out, g_on, tm):
    T = dy1.shape[0]
    W = N_HEADS * HB

    def body(dy1_ref, mix_ref, o_ref, w_ref, gon_ref, do_ref, dl_ref, dohg_ref, dw_ref, dgon_ref):
        i = pl.program_id(0)
        dy1v = dy1_ref[...]
        dmix = _dot_nt(dy1v, w_ref[...])
        pw = _dot_tn(mix_ref[...], dy1v)

        @pl.when(i == 0)
        def _():
            dw_ref[...] = pw
            dgon_ref[...] = jnp.zeros(dgon_ref.shape, F32)

        @pl.when(i > 0)
        def _():
            dw_ref[...] += pw

        for h in range(N_HEADS):
            sl = slice(h * HB, (h + 1) * HB)
            ov = o_ref[:, sl].astype(F32)
            n, r = _rms(ov, V_DIM)
            do, dg = _rms_bwd(n, r, gon_ref[:, sl], dmix[:, sl], V_DIM)
            dgon_ref[:, sl] += dg
            do_ref[:, sl] = do.astype(BF)
            dl_ref[:, sl] = jnp.broadcast_to(jnp.sum(do * ov, -1, keepdims=True), (tm, HB))
        dohg_ref[...] = dmix[:, W:MIX_W]

    return _pcall(body, "out_bwd", (T // tm,),
                  [_rows(tm, D), _rows(tm, MIX_W), _rows(tm, W), _full((MIX_W, D)), _full((1, W))],
                  [_rows(tm, W), _rows(tm, W), _rows(tm, G_W), _full((MIX_W, D)), _full((1, W))],
                  [_sds((T, W), BF), _sds((T, W)), _sds((T, G_W)), _sds((MIX_W, D)), _sds((1, W))],
                  )(dy1, mix, o_pad, w_out, g_on)


def _flash_bwd(q, k, v, do, lse, dl, tq, exchange=None):
    T = q.shape[0]
    nq = T // tq
    scale = QK_DIM ** -0.5
    hp = _AH
    W = hp * HB

    def body(k_ref, v_ref, q_ref, do_ref, lse_ref, dl_ref, dk_ref, dv_ref, dq_ref):
        j = pl.program_id(1)

        @pl.when(j == 0)
        def _():
            dq_ref[...] = jnp.zeros(dq_ref.shape, F32)

        def blk(i, carry, masked):
            sl = pl.ds(pl.multiple_of(i * tq, tq), tq)
            out = []
            for h in range(hp):
                ls = slice(h * HB, (h + 1) * HB)
                dk, dv = carry[h]
                kv, vv = k_ref[:, ls], v_ref[:, ls]
                qv, dov = q_ref[sl, ls], do_ref[sl, ls]
                s = _dot_nt(qv, kv) * scale
                if masked:
                    r = lax.broadcasted_iota(jnp.int32, (tq, tq), 0)
                    c = lax.broadcasted_iota(jnp.int32, (tq, tq), 1)
                    s = jnp.where(c <= r, s, NEG)
                p = jnp.exp(s - lse_ref[sl, h * HB:h * HB + 1])
                ds = p * (_dot_nt(dov, vv) - dl_ref[sl, h * HB:h * HB + 1]) * scale
                dq_ref[sl, ls] += _dot(ds, kv)
                out.append((dk + _dot_tn(ds, qv), dv + _dot_tn(p, dov)))
            return tuple(out)

        zero = jnp.zeros((tq, HB), F32)
        carry = blk(j, tuple((zero, zero) for _ in range(hp)), True)
        res = lax.fori_loop(j + 1, nq, lambda i, cr: blk(i, cr, False), carry)
        for h in range(hp):
            ls = slice(h * HB, (h + 1) * HB)
            dk_ref[:, ls] = res[h][0].astype(BF)
            dv_ref[:, ls] = res[h][1].astype(BF)

    tile = pl.BlockSpec((tq, W), lambda h, j: (j, h))
    whole = pl.BlockSpec((T, W), lambda h, j: (0, h))
    return _pcall(body, "flash_bwd", (N_HEADS // hp, nq), [tile, tile, whole, whole, whole, whole],
                  [tile, tile, whole], [_sds((T, N_HEADS * HB), BF)] * 2 + [_sds((T, N_HEADS * HB))],
                  exchange=exchange)(k, v, q, do, lse, dl)


def _mla_prep_bwd(xp, tabs, dq, dk, dv, g_q, g_kv, w_uq, w_uk, w_uv, tm):
    T = xp.shape[0]
    W = N_HEADS * HB

    def body(xp_ref, ta_ref, tb1_ref, tb2_ref, dq_ref, dk_ref, dv_ref, gq_ref, gkv_ref, wuq_ref, wuk_ref, wuv_ref,
             dxp_ref, dwuq_ref, dwuk_ref, dwuv_ref, dgq_ref, dgkv_ref, dqp):
        i = pl.program_id(0)
        ta, tb1, tb2 = ta_ref[...], tb1_ref[...], tb2_ref[...]
        nq, rq = _rms(xp_ref[:, 0:Q_RANK], Q_RANK)
        nkv, rkv = _rms(xp_ref[:, Q_RANK:Q_RANK + KV_RANK], KV_RANK)
        dkr = jnp.zeros((tm, HB), F32)
        for h in range(N_HEADS):
            sl = slice(h * HB, (h + 1) * HB)
            dqp[:, sl] = _unrope(dq_ref[:, sl], ta, tb1, tb2).astype(BF)
            dkr = dkr + dk_ref[:, sl].astype(F32)
        dkr = pltpu.roll(_unrope(dkr, ta, tb1, tb2), HB - NOPE, 1)
        lane = lax.broadcasted_iota(jnp.int32, (tm, HB), 1)
        dxp_ref[:, Q_RANK + KV_RANK:MLA_IN] = jnp.where(lane < ROPE, dkr, 0.0)
        dqpv = dqp[...]
        dkv, dvv = dk_ref[...], dv_ref[...]
        nqs = (nq * gq_ref[...]).astype(BF)
        nkvs = (nkv * gkv_ref[...]).astype(BF)
        pq, pk, pv = _dot_tn(nqs, dqpv), _dot_tn(nkvs, dkv), _dot_tn(nkvs, dvv)
        dcq, dgq = _rms_bwd(nq, rq, gq_ref[...], _dot_nt(dqpv, wuq_ref[...]), Q_RANK)
        dckv, dgkv = _rms_bwd(nkv, rkv, gkv_ref[...], _dot_nt(dkv, wuk_ref[...]) + _dot_nt(dvv, wuv_ref[...]), KV_RANK)
        dxp_ref[:, 0:Q_RANK] = dcq
        dxp_ref[:, Q_RANK:Q_RANK + KV_RANK] = dckv

        @pl.when(i == 0)
        def _():
            dwuq_ref[...] = pq
            dwuk_ref[...] = pk
            dwuv_ref[...] = pv
            dgq_ref[...] = dgq
            dgkv_ref[...] = dgkv

        @pl.when(i > 0)
        def _():
            dwuq_ref[...] += pq
            dwuk_ref[...] += pk
            dwuv_ref[...] += pv
            dgq_ref[...] += dgq
            dgkv_ref[...] += dgkv

    tab = _rows(tm, HB)
    return _pcall(
        body, "mla_prep_bwd", (T // tm,),
        [_rows(tm, MLA_IN), tab, tab, tab, _rows(tm, W), _rows(tm, W), _rows(tm, W), _full((1, Q_RANK)),
         _full((1, KV_RANK)), _full((Q_RANK, W)), _full((KV_RANK, W)), _full((KV_RANK, W))],
        [_rows(tm, MLA_IN), _full((Q_RANK, W)), _full((KV_RANK, W)), _full((KV_RANK, W)), _full((1, Q_RANK)),
         _full((1, KV_RANK))],
        [_sds((T, MLA_IN)), _sds((Q_RANK, W)), _sds((KV_RANK, W)), _sds((KV_RANK, W)), _sds((1, Q_RANK)),
         _sds((1, KV_RANK))],
        scratch=[pltpu.VMEM((tm, W), BF)],
    )(xp, *tabs, dq, dk, dv, g_q, g_kv, w_uq, w_uk, w_uv)


def _hgrn_bwd(xp, o_raw, s_all, a_all, b_all, d_out, lb_logits, g_hn, exchange=None):
    T = xp.shape[0]
    tb = min(_TB_BWD, T)
    ncb = tb // CHUNK
    nb = T // tb
    hp = _HP
    W = hp * G_DIM

    def body(hq_ref, hf_ref, hi_ref, hg_ref, o_ref, sall_ref, aall_ref, ball_ref, dout_ref, lbl_ref, ghn_ref,
             dhq_ref, dhf_ref, dhi_ref, dhg_ref, dlbl_ref, dghn_ref, dst_ref, b_s, acc_lb, acc_g):
        t = pl.program_id(1)
        lb_all = _lower_bound(lbl_ref)

        @pl.when(t == 0)
        def _():
            dst_ref[...] = jnp.zeros(dst_ref.shape, F32)
            acc_lb[...] = jnp.zeros(acc_lb.shape, F32)
            acc_g[...] = jnp.zeros(acc_g.shape, F32)

        row = lax.broadcasted_iota(jnp.int32, (CHUNK, CHUNK), 0)
        col = lax.broadcasted_iota(jnp.int32, (CHUNK, CHUNK), 1)
        tri_t = (col >= row).astype(BF)
        codes = _intra_codes(SUB)
        last = lax.broadcasted_iota(jnp.int32, (CHUNK, G_DIM), 0) == CHUNK - 1

        def chunk(cc, carry):
            c = ncb - 1 - cc
            sl = pl.ds(pl.multiple_of(c * CHUNK, CHUNK), CHUNK)
            for h in range(hp):
                ls = slice(h * G_DIM, (h + 1) * G_DIM)
                lb, ghn = lb_all[:, ls], ghn_ref[:, ls]
                hq, hg = hq_ref[sl, ls], hg_ref[sl, ls]
                q, k, f, _, sig, sq = _gates(hq, hf_ref[sl, ls], lb)
                v = hi_ref[sl, ls]
                b2 = ball_ref[sl, ls]
                b_s[h] = b2
                st = sall_ref[c, h]
                dstn = dst_ref[h]
                o = o_ref[sl, ls]
                dout = dout_ref[sl, ls]
                n, r = _rms(o, G_DIM)
                sg = _sigmoid(hg)
                dhg_ref[sl, ls] = dout * (n * ghn) * _dsilu(hg, sg)
                do, dg = _rms_bwd(n, r, ghn, dout * (hg * sg), G_DIM)
                acc_g[:, ls] += dg
                eb = jnp.exp2(b2)
                bl = b_s[h, CHUNK - 1:CHUNK, :]
                ebl = jnp.exp2(bl)
                ekd = jnp.exp2(bl - b2)
                kd = k * ekd
                a = aall_ref[c, h]
                dq_i, dk_i = _intra(q, k, b2, b_s.at[h], codes, _dot_nt(do, v))
                dhi_ref[sl, ls] = _dot_tn(a, do) + _dot_nt(kd, dstn)
                dk_state = _dot(v, dstn) * ekd
                dq = dq_i + _dot(do, st) * eb
                dk = dk_i + dk_state
                dbl = jnp.sum(k * dk_state, 0, keepdims=True) + ebl * jnp.sum(dstn * st, 0, keepdims=True)
                db = q * dq - k * dk + jnp.where(last, dbl, 0.0)
                df = _tri_mm(tri_t, db) / f - dk
                dhf_ref[sl, ls] = df * (1.0 - lb) * sig * (1.0 - sig)
                acc_lb[:, ls] += jnp.sum(df * (1.0 - sig), 0, keepdims=True)
                dhq_ref[sl, ls] = dq * _dsilu(hq, sq)
                dst_ref[h] = dstn * ebl + _dot_tn(do, q * eb)
            return carry

        lax.fori_loop(0, ncb, chunk, 0, unroll=2)

        @pl.when(t == nb - 1)
        def _():
            dl0 = acc_lb[...] * lb_all * (1.0 - lb_all)
            dlbl_ref[0:1, :] = dl0
            dlbl_ref[1:2, :] = -dl0
            dghn_ref[...] = acc_g[...]

    col_blk = lambda j: pl.BlockSpec((tb, W), lambda p, t: (nb - 1 - t, j * (G_HEADS // hp) + p))
    head = pl.BlockSpec((tb, W), lambda p, t: (nb - 1 - t, p))
    two = pl.BlockSpec((2, W), lambda p, t: (0, p))
    one = pl.BlockSpec((1, W), lambda p, t: (0, p))
    res = _pcall(
        body, "hgrn_bwd", (G_HEADS // hp, nb),
        [col_blk(0), col_blk(1), col_blk(2), col_blk(3), head,
         pl.BlockSpec((ncb, hp, G_DIM, G_DIM), lambda p, t: (nb - 1 - t, p, 0, 0)),
         pl.BlockSpec((ncb, hp, CHUNK, CHUNK), lambda p, t: (nb - 1 - t, p, 0, 0)), head, head, two, one],
        [head, head, head, head, two, one],
        [_sds((T, G_W))] * 4 + [_sds((2, G_W)), _sds((1, G_W))],
        scratch=[pltpu.VMEM((hp, G_DIM, G_DIM), F32), pltpu.VMEM((hp, CHUNK, G_DIM), F32),
                 pltpu.VMEM((1, W), F32), pltpu.VMEM((1, W), F32)], exchange=exchange,
    )(xp, xp, xp, xp, o_raw, s_all, a_all, b_all, d_out, lb_logits, g_hn)
    return res


def _in_bwd_x(x, dxp_m, dxp_h, dh1, w_in_al, g_pre, tm, exchange=None):
    T = x.shape[0]

    def body(x_ref, dm_ref, d0_ref, d1_ref, d2_ref, d3_ref, dh1_ref, w_ref, g_ref, dx_ref, dg_ref):
        i = pl.program_id(0)
        du = _dot_nt(dm_ref[...], w_ref[:, 0:MLA_IN])
        for j, d_ref in enumerate((d0_ref, d1_ref, d2_ref, d3_ref)):
            du = du + _dot_nt(d_ref[...], w_ref[:, MLA_IN + j * G_W:MLA_IN + (j + 1) * G_W])
        nx, r = _rms(x_ref[...], D)
        dx, dg = _rms_bwd(nx, r, g_ref[...], du, D)
        dx_ref[...] = dh1_ref[...] + dx

        @pl.when(i == 0)
        def _():
            dg_ref[...] = dg

        @pl.when(i > 0)
        def _():
            dg_ref[...] += dg

    return _pcall(body, "in_bwd_x", (T // tm,),
                  [_rows(tm, D), _rows(tm, MLA_IN)] + [_rows(tm, G_W)] * 4 + [_rows(tm, D), _full((D, XP_W)), _full((1, D))],
                  [_rows(tm, D), _full((1, D))], [_sds((T, D)), _sds((1, D))], exchange=exchange,
                  )(x, dxp_m, *dxp_h, dh1, w_in_al, g_pre)


def _aligned_col(c):
    return jnp.where(c < Q_RANK + KV_RANK + ROPE, c, c + (KR_PAD - ROPE))


def _align_w_in(g_in):
    tile = 384
    kr_end = Q_RANK + KV_RANK + ROPE

    def body(g_ref, o_ref, gp):
        gp[...] = jnp.zeros(gp.shape, BF)
        for j in range(N_DEV):
            gp[j, :, 0:IN_SH] = g_ref[j]
        r = lax.broadcasted_iota(jnp.int32, (tile, tile), 0)
        c = lax.broadcasted_iota(jnp.int32, (tile, tile), 1)
        for t in range(XP_W // tile):
            lo, hi = t * tile, (t + 1) * tile
            cols = [a if a < kr_end else a - (KR_PAD - ROPE) for a in (lo, hi - 1)]
            acc = jnp.zeros((D, tile), F32)
            for j in range(cols[0] // IN_SH, cols[-1] // IN_SH + 1):
                sel = (r < IN_SH) & (_aligned_col(j * IN_SH + r) == lo + c)
                acc = acc + jnp.dot(gp[j], sel.astype(BF), preferred_element_type=F32)
            o_ref[:, lo:hi] = acc.astype(BF)

    vm = pl.BlockSpec(memory_space=pltpu.VMEM)
    return pl.pallas_call(
        body, name="align_w_in", in_specs=[vm], out_specs=vm, out_shape=_sds((D, XP_W), BF),
        scratch_shapes=[pltpu.VMEM((N_DEV, D, tile), BF)],
        compiler_params=pltpu.CompilerParams(vmem_limit_bytes=_VMEM_LIMIT))(g_in)


def _in_bwd_w(name, u, dxp_m, dxp_h, tm, half, exchange=None):
    T = u.shape[0]
    nt = T // tm
    nr = D // 2
    win = 640

    def body(u_ref, dm_ref, d0_ref, d1_ref, d2_ref, d3_ref, o_ref, acc):
        i = pl.program_id(0)
        ut = u_ref[...].T
        parts = [(0, MLA_IN, dm_ref)] + [(MLA_IN + j * G_W, G_W, d) for j, d in enumerate((d0_ref, d1_ref, d2_ref, d3_ref))]

        @pl.when(i == 0)
        def _():
            for lo, n, d in parts:
                acc[:, lo:lo + n] = jnp.dot(ut, d[...].astype(BF), preferred_element_type=F32)

        @pl.when(i > 0)
        def _():
            for lo, n, d in parts:
                acc[:, lo:lo + n] += jnp.dot(ut, d[...].astype(BF), preferred_element_type=F32)

        @pl.when(i == nt - 1)
        def _():
            wide = 384
            r = lax.broadcasted_iota(jnp.int32, (win, wide), 0)
            c = lax.broadcasted_iota(jnp.int32, (win, wide), 1)
            kr_end = Q_RANK + KV_RANK + ROPE
            for j in range(N_DEV):
                first = j * IN_SH if j * IN_SH < kr_end else j * IN_SH + (KR_PAD - ROPE)
                lo = min(first // HB * HB, XP_W - win)
                sel = (c < IN_SH) & (_aligned_col(j * IN_SH + c) == lo + r)
                res = jnp.dot(acc[:, lo:lo + win].astype(BF), sel.astype(BF), preferred_element_type=F32)
                o_ref[j] = res[:, 0:IN_SH].astype(BF)

    return _pcall(body, name, (nt,),
                  [pl.BlockSpec((tm, nr), lambda i: (i, half)), _rows(tm, MLA_IN)] + [_rows(tm, G_W)] * 4,
                  [_full((N_DEV, nr, IN_SH))], [_sds((N_DEV, nr, IN_SH), BF)],
                  scratch=[pltpu.VMEM((nr, XP_W), F32)], exchange=exchange)(u, dxp_m, *dxp_h)


def _pad_heads(w, width, real):
    lead = w.shape[:-1]
    w = w.reshape(lead + (N_HEADS, real))
    w = jnp.pad(w, [(0, 0)] * len(lead) + [(0, 0), (0, width - real)])
    return w.reshape(lead + (N_HEADS * width,))


def _unpad_heads(w, width, real):
    lead = w.shape[:-1]
    return w.reshape(lead + (N_HEADS, width))[..., :real].reshape(lead + (N_HEADS * real,))


def _rope_tables(positions):
    half = ROPE // 2
    inv_freq = 1.0 / (ROPE_THETA ** (jnp.arange(0, ROPE, 2, dtype=F32) / ROPE))
    ang = positions.astype(F32)[:, None] * inv_freq
    cos, sin = jnp.cos(ang), jnp.sin(ang)
    T = positions.shape[0]
    z = lambda n: jnp.zeros((T, n), F32)
    ta = jnp.concatenate([jnp.ones((T, NOPE), F32), cos, cos, z(HB - QK_DIM)], 1)
    tb1 = jnp.concatenate([z(NOPE), -sin, z(half), z(HB - QK_DIM)], 1)
    tb2 = jnp.concatenate([z(NOPE), z(half), sin, z(HB - QK_DIM)], 1)
    return ta, tb1, tb2


def kernel(x, positions, attn_pre_norm, w_in, mla_q_norm, mla_w_uq, mla_kv_norm, mla_w_ukv, mla_out_norm, hgrn_lb_logits, hgrn_out_norm, w_out, attn_post_norm, ffn_pre_norm, w_gate, w_up, w_down, ffn_post_norm, loss_target, m_attn_pre_norm, m_w_in, m_mla_q_norm, m_mla_w_uq, m_mla_kv_norm, m_mla_w_ukv, m_mla_out_norm, m_hgrn_lb_logits, m_hgrn_out_norm, m_w_out, m_attn_post_norm, m_ffn_pre_norm, m_w_gate, m_w_up, m_w_down, m_ffn_post_norm, v_attn_pre_norm, v_w_in, v_mla_q_norm, v_mla_w_uq, v_mla_kv_norm, v_mla_w_ukv, v_mla_out_norm, v_hgrn_lb_logits, v_hgrn_out_norm, v_w_out, v_attn_post_norm, v_ffn_pre_norm, v_w_gate, v_w_up, v_w_down, v_ffn_post_norm):
    T = x.shape[1]
    tm = min(_TM, T)
    tq = min(_TQ, T)
    xs, tgt = x[0], loss_target[0]
    uq_sh = (Q_RANK // N_DEV, N_HEADS * QK_DIM)

    b_in, b_uq, b_out, b_gu, b_d = _cast_shards(
        w_in[0], mla_w_uq[0].reshape(uq_sh), w_out[0], w_gate[0].T, w_up[0].T, w_down[0])
    g_in, g_uq = _gather_two_level("ag_first", [b_in, b_uq])
    w_in_al = _align_w_in(g_in)
    w_uq_p = _pad_heads(g_uq.reshape(Q_RANK, N_HEADS * QK_DIM), HB, QK_DIM)
    w_ukv = mla_w_ukv[0].astype(BF)
    w_uk_p = _pad_heads(w_ukv[..., :NOPE].reshape(KV_RANK, N_HEADS * NOPE), HB, NOPE)
    w_uv_p = _pad_heads(w_ukv[..., NOPE:].reshape(KV_RANK, N_HEADS * V_DIM), HB, V_DIM)
    g_on_p = _pad_heads(mla_out_norm, HB, V_DIM)
    tabs = _rope_tables(positions[0])

    xp_m, xp_h, u = _fwd_in(xs, attn_pre_norm, w_in_al, tm)
    q_att, qs_att, k_att, v_att = _mla_prep(xp_m, tabs, mla_q_norm, mla_kv_norm, w_uq_p, w_uk_p, w_uv_p, tm)
    o_hgrn, o_raw, s_all, a_all, b_all, g_out, wd = _hgrn_fwd(xp_h, hgrn_lb_logits, hgrn_out_norm, ([GATHER, GATHER], [b_out, b_d]))
    wd = wd.reshape(N_DEV * FF_PAD, D)
    o_pad, lse, wgu = _flash_fwd(qs_att, k_att, v_att, tq, ([GATHER], [b_gu]))
    wgu = wgu.reshape(N_DEV * 2 * FF_PAD, D)
    w_out_full = g_out.reshape(D, D)
    w_out_mla = jnp.pad(w_out_full[:N_HEADS * V_DIM].reshape(N_HEADS, V_DIM, D), ((0, 0), (0, HB - V_DIM), (0, 0)))
    w_out_p = jnp.concatenate([w_out_mla.reshape(N_HEADS * HB, D), w_out_full[N_HEADS * V_DIM:]], 0)
    h1, y1, z, mix = _fwd_out(o_pad, o_hgrn, xs, g_on_p, w_out_p, attn_post_norm, ffn_pre_norm, tm)
    tmf = min(_TMF, T)
    gs, us, ffs, dh2, dy2, d_fpost, loss_row = _ffn_fwd(z, wgu, wd, h1, tgt, ffn_post_norm, tm, _FB // FF_PAD)

    dgu, dh1, dy1, d_fpre, d_post = _ffn_bwd_x(dy2, gs, us, wgu, wd, h1, y1, dh2, ffn_pre_norm, attn_post_norm, tm)
    dwgu, dwd = _ffn_bwd_w(z, ffs, dgu, dy2, tmf)
    do_pad, dl, d_ohg, dw_out_p, d_on_p = _out_bwd(dy1, mix, o_pad, w_out_p, g_on_p, tm)
    dw_out_mla = dw_out_p[:N_HEADS * HB].reshape(N_HEADS, HB, D)[:, :V_DIM].reshape(N_HEADS * V_DIM, D)
    dw_out = jnp.concatenate([dw_out_mla, dw_out_p[N_HEADS * HB:]], 0).reshape(N_DEV, D // N_DEV, D).astype(BF)
    dk_att, dv_att, dq_att, p_gu, p_d, p_out = _flash_bwd(
        q_att, k_att, v_att, do_pad, lse, dl, tq,
        ([SCATTER] * 3, [dwgu.reshape(N_DEV, 2 * FF_PAD, D), dwd.reshape(N_DEV, FF_PAD, D), dw_out]))
    dxp_m, dw_uq_p, dw_uk_p, dw_uv_p, d_gq, d_gkv = _mla_prep_bwd(
        xp_m, tabs, dq_att, dk_att, dv_att, mla_q_norm, mla_kv_norm, w_uq_p, w_uk_p, w_uv_p, tm)
    dw_uq = _unpad_heads(dw_uq_p, HB, QK_DIM).reshape((N_DEV,) + uq_sh).astype(BF)
    dw_ukv = jnp.concatenate([_unpad_heads(dw_uk_p, HB, NOPE).reshape(KV_RANK, N_HEADS, NOPE),
                              _unpad_heads(dw_uv_p, HB, V_DIM).reshape(KV_RANK, N_HEADS, V_DIM)], -1)
    *dxp_h, d_lbl, d_ghn, p_uq, dw_ukv_all = _hgrn_bwd(
        xp_h, o_raw, s_all, a_all, b_all, d_ohg, hgrn_lb_logits, hgrn_out_norm,
        ([SCATTER, GATHER], [dw_uq, dw_ukv.reshape(KV_RANK, N_HEADS * HB)]))
    dw_in_a, = _in_bwd_w("in_bwd_w_a", u, dxp_m, dxp_h, tm, 0)
    dw_in_b, p_in_a = _in_bwd_w("in_bwd_w_b", u, dxp_m, dxp_h, tm, 1, ([SCATTER], [dw_in_a]))
    grad_x, d_pre, p_in_b = _in_bwd_x(xs, dxp_m, dxp_h, dh1, w_in_al, attn_pre_norm, tm, ([SCATTER], [dw_in_b]))
    d_on = _unpad_heads(d_on_p, HB, V_DIM)

    ukv2 = lambda a: a.reshape(KV_RANK, N_HEADS * HB)
    vecs = [d_pre, d_gq, d_gkv, d_on, d_lbl, d_ghn, d_post, d_fpre, d_fpost, loss_row]
    small_w = [attn_pre_norm, mla_q_norm, mla_kv_norm, ukv2(mla_w_ukv), mla_out_norm, hgrn_lb_logits, hgrn_out_norm,
               attn_post_norm, ffn_pre_norm, ffn_post_norm]
    small_m = [m_attn_pre_norm, m_mla_q_norm, m_mla_kv_norm, ukv2(m_mla_w_ukv), m_mla_out_norm, m_hgrn_lb_logits,
               m_hgrn_out_norm, m_attn_post_norm, m_ffn_pre_norm, m_ffn_post_norm]
    small_v = [v_attn_pre_norm, v_mla_q_norm, v_mla_kv_norm, ukv2(v_mla_w_ukv), v_mla_out_norm, v_hgrn_lb_logits,
               v_hgrn_out_norm, v_attn_post_norm, v_ffn_pre_norm, v_ffn_post_norm]
    rall = _final_exchange(vecs)
    s_g, s_d, s_m, s_v, loss_all = _small_adam(rall, dw_ukv_all, 3, small_w, small_m, small_v)
    r_in = _shard_adam("adam_w_in", [p_in_a, p_in_b], w_in[0], m_w_in[0], v_w_in[0], 256)
    r_uq = _shard_adam("adam_w_uq", [p_uq], mla_w_uq[0].reshape(uq_sh), m_mla_w_uq[0].reshape(uq_sh),
                       v_mla_w_uq[0].reshape(uq_sh), uq_sh[0])
    r_out = _shard_adam("adam_w_out", [p_out], w_out[0], m_w_out[0], v_w_out[0], D // N_DEV)
    r_g, r_u = _gate_up_adam(p_gu, (w_gate[0].T, w_up[0].T), (m_w_gate[0].T, m_w_up[0].T),
                             (v_w_gate[0].T, v_w_up[0].T))
    r_g, r_u = [a.T for a in r_g], [a.T for a in r_u]
    r_d = _shard_adam("adam_w_down", [p_d], w_down[0], m_w_down[0], v_w_down[0], FF_SH // 2)

    loss = loss_all[0, 0]

    def assemble(big, small):
        b_in, b_uq, b_out, b_g, b_u, b_d = big
        return [small[0], b_in[None], small[1], b_uq.reshape(mla_w_uq.shape), small[2],
                small[3].reshape(mla_w_ukv.shape), small[4], small[5], small[6], b_out[None], small[7], small[8],
                b_g[None], b_u[None], b_d[None], small[9]]

    outs = [loss, grad_x[None]]
    for idx, small in enumerate((s_g, s_d, s_m, s_v)):
        outs += assemble([r[idx] for r in (r_in, r_uq, r_out, r_g, r_u, r_d)], small)
    return tuple(outs)
```

```python
import jax
import jax.numpy as jnp
from jax import lax
from jax.experimental import pallas as pl
from jax.experimental.pallas import tpu as pltpu

BF = jnp.bfloat16
F32 = jnp.float32
MESH = pl.DeviceIdType.MESH

N_DEV = 8
D = 1024
EPS = 1e-6
LOG2E = 1.4426950408889634
ROPE_THETA = 10000.0
N_HEADS = 8
HB = 128
NOPE = 64
ROPE = 32
V_DIM = 64
QK_DIM = NOPE + ROPE
Q_RANK = 384
KV_RANK = 128
KR_PAD = 128
MLA_IN = Q_RANK + KV_RANK + KR_PAD
G_HEADS = 4
G_DIM = 128
G_W = G_HEADS * G_DIM
CHUNK = 64
SUB = 16
XP_W = MLA_IN + 4 * G_W
IN_SH = 324
FF_SH = 352
FF_PAD = 384
MIX_W = N_HEADS * HB + G_W

ADAM_LR = 0.001
ADAM_B1 = 0.9
ADAM_B2 = 0.999
ADAM_EPS = 1e-08
ADAM_WD = 0.01
ADAM_STEP = 10

_TM = 512
_TMF = 1024
_TQ = 512
_AH = 2
_AH_FWD = 4
_FB = 768
_TB = 1024
_TB_BWD = 512
_HP = 4
V7X_VMEM_BYTES = 64 * 1024 * 1024
_VMEM_LIMIT = V7X_VMEM_BYTES - 8 * 1024 * 1024
NEG = -1e30


def _dot(a, b):
    return jnp.dot(a.astype(BF), b.astype(BF), preferred_element_type=F32)


def _dot_nt(a, b):
    return lax.dot_general(a.astype(BF), b.astype(BF), (((1,), (1,)), ((), ())), preferred_element_type=F32)


def _dot_tn(a, b):
    return lax.dot_general(a.astype(BF), b.astype(BF), (((0,), (0,)), ((), ())), preferred_element_type=F32)


def _sigmoid(x):
    return 1.0 / (1.0 + jnp.exp(-x))


def _rms(x, n):
    r = lax.rsqrt(jnp.sum(x * x, -1, keepdims=True) * (1.0 / n) + EPS)
    return x * r, r


def _rms_bwd(nx, r, g, dy, n):
    dg = jnp.sum(dy * nx, 0, keepdims=True)
    dn = dy * g
    dx = r * (dn - nx * (jnp.sum(dn * nx, -1, keepdims=True) * (1.0 / n)))
    return dx, dg


def _adamw(w, g, m, v):
    m2 = ADAM_B1 * m + (1.0 - ADAM_B1) * g
    v2 = ADAM_B2 * v + (1.0 - ADAM_B2) * (g * g)
    m_hat = m2 / (1.0 - ADAM_B1 ** ADAM_STEP)
    v_hat = v2 / (1.0 - ADAM_B2 ** ADAM_STEP)
    delta = -ADAM_LR * (m_hat / (jnp.sqrt(v_hat) + ADAM_EPS) + ADAM_WD * w)
    return delta, m2, v2


def _pcall(body, name, grid, in_specs, out_specs, out_shape, scratch=(), exchange=None):
    scratch = list(scratch)
    extra = ()
    if exchange is not None:
        kinds, extra = exchange
        in_specs, out_specs, out_shape = list(in_specs), list(out_specs), list(out_shape)
        n_in, n_out, n_scr, n_x = len(in_specs), len(out_specs), len(scratch), len(extra)
        inner = body

        def body(*refs):
            ins, rest = refs[:n_in], refs[n_in:]
            x_src, rest = rest[:n_x], rest[n_x:]
            outs, rest = rest[:n_out], rest[n_out:]
            x_dst, rest = rest[:n_x], rest[n_x:]
            ex = _Exchange(kinds, x_src, x_dst, *rest[n_scr:])
            first = pl.program_id(0) == 0
            last = pl.program_id(0) == grid[0] - 1
            for a in range(1, len(grid)):
                first = first & (pl.program_id(a) == 0)
                last = last & (pl.program_id(a) == grid[a] - 1)
            pl.when(first)(ex.start)
            inner(*ins, *outs, *rest[:n_scr])
            pl.when(last)(ex.wait)

        in_specs += [_HBM] * n_x
        out_specs += [_HBM] * n_x
        out_shape += _exchange_shapes(kinds, extra)
        scratch += _exchange_sems(n_x)
    call = pl.pallas_call(
        body, name=name, grid=grid, in_specs=in_specs, out_specs=out_specs, out_shape=out_shape,
        scratch_shapes=scratch,
        compiler_params=pltpu.CompilerParams(
            dimension_semantics=("arbitrary",) * len(grid), vmem_limit_bytes=_VMEM_LIMIT))
    return lambda *operands: call(*operands, *extra)


def _full(shape):
    return pl.BlockSpec(shape, lambda *_: (0,) * len(shape))


def _rows(tm, n):
    return pl.BlockSpec((tm, n), lambda i, *_: (i, 0))


def _sds(shape, dtype=F32):
    return jax.ShapeDtypeStruct(shape, dtype)


def _peer(k, x, y, c):
    px = 1 - x if (k >> 2) & 1 else x
    py = 1 - y if (k >> 1) & 1 else y
    pc = 1 - c if k & 1 else c
    return px, py, pc


GATHER, SCATTER = "gather", "scatter"


class _Exchange:
    def __init__(self, kinds, srcs, dsts, send_sems, recv_sems, loc_sems):
        self.kinds, self.srcs, self.dsts = kinds, srcs, dsts
        self.send_sems, self.recv_sems, self.loc_sems = send_sems, recv_sems, loc_sems
        self.x, self.y, self.c = lax.axis_index("x"), lax.axis_index("y"), lax.axis_index("c")
        self.me = 4 * self.x + 2 * self.y + self.c

    def _src(self, w, slot):
        return self.srcs[w] if self.kinds[w] == GATHER else self.srcs[w].at[slot]

    def _dst(self, w, slot):
        return self.dsts[w].at[slot]

    def _copy(self, w, k, outgoing):
        px, py, pc = _peer(k, self.x, self.y, self.c)
        pid = 4 * px + 2 * py + pc
        return pltpu.make_async_remote_copy(
            src_ref=self._src(w, pid if outgoing else self.me),
            dst_ref=self._dst(w, self.me if outgoing else pid),
            send_sem=self.send_sems.at[w, k - 1], recv_sem=self.recv_sems.at[w, k - 1],
            device_id=(px, py, pc), device_id_type=MESH)

    def _local(self, w):
        return pltpu.make_async_copy(self._src(w, self.me), self._dst(w, self.me), self.loc_sems.at[w])

    def start(self):
        for w in range(len(self.srcs)):
            self._local(w).start()
            for k in range(1, N_DEV):
                self._copy(w, k, True).start()

    def wait(self):
        for w in range(len(self.srcs)):
            self._local(w).wait()
            for k in range(1, N_DEV):
                self._copy(w, k, False).wait_recv()
        for w in range(len(self.srcs)):
            for k in range(1, N_DEV):
                self._copy(w, k, True).wait_send()


def _exchange_sems(n_w):
    return [pltpu.SemaphoreType.DMA((n_w, N_DEV - 1)), pltpu.SemaphoreType.DMA((n_w, N_DEV - 1)),
            pltpu.SemaphoreType.DMA((n_w,))]


def _exchange_shapes(kinds, srcs):
    return [_sds(((N_DEV,) if kd == GATHER else ()) + tuple(s.shape), s.dtype) for kd, s in zip(kinds, srcs)]


_HBM = pl.BlockSpec(memory_space=pl.ANY)


def _cast_shards(w_in, w_uq, w_out, w_gate_t, w_up_t, w_down):
    shapes = [(D, IN_SH), (Q_RANK // N_DEV, N_HEADS * QK_DIM), (D // N_DEV, D), (2 * FF_PAD, D), (FF_PAD, D)]

    def body(win, wuq, wout, wg, wu, wd, sin_, suq, sout, sgu, sd):
        sin_[...] = win[...].astype(BF)
        suq[...] = wuq[...].astype(BF)
        sout[...] = wout[...].astype(BF)
        sgu[...] = jnp.zeros(sgu.shape, BF)
        sgu[0:FF_SH, :] = wg[...].astype(BF)
        sgu[FF_PAD:FF_PAD + FF_SH, :] = wu[...].astype(BF)
        sd[...] = jnp.zeros(sd.shape, BF)
        sd[0:FF_SH, :] = wd[...].astype(BF)

    vm = pl.BlockSpec(memory_space=pltpu.VMEM)
    return pl.pallas_call(
        body, name="cast_shards", in_specs=[vm] * 6, out_specs=[vm] * 5,
        out_shape=[_sds(s, BF) for s in shapes],
        compiler_params=pltpu.CompilerParams(vmem_limit_bytes=_VMEM_LIMIT),
    )(w_in, w_uq, w_out, w_gate_t, w_up_t, w_down)


def _gather_two_level(name, srcs):
    n_w = len(srcs)

    def body(*refs):
        src, dst = refs[:n_w], refs[n_w:2 * n_w]
        send_sems, recv_sems, loc_sems = refs[2 * n_w:]
        x, y, c = lax.axis_index("x"), lax.axis_index("y"), lax.axis_index("c")
        me, sibling = (x, y, c), (x, y, 1 - c)
        chips = [(1 - x, y), (x, 1 - y), (1 - x, 1 - y)]
        slot = lambda p: 4 * p[0] + 2 * p[1] + p[2]

        def copy(w, k, block, to, own=False):
            return pltpu.make_async_remote_copy(
                src_ref=src[w] if own else dst[w].at[slot(block)], dst_ref=dst[w].at[slot(block)],
                send_sem=send_sems.at[w, k], recv_sem=recv_sems.at[w, k], device_id=to, device_id_type=MESH)

        local = [pltpu.make_async_copy(src[w], dst[w].at[slot(me)], loc_sems.at[w]) for w in range(n_w)]
        first, passed = [], []
        for w in range(n_w):
            local[w].start()
            first.append(copy(w, 0, me, sibling, own=True))
            first += [copy(w, 1 + j, me, (*chip, c), own=True) for j, chip in enumerate(chips)]
        for cp in first:
            cp.start()
        for w in range(n_w):
            for j, chip in enumerate(chips):
                copy(w, 1 + j, (*chip, c), me).wait_recv()
                passed.append(copy(w, 4 + j, (*chip, c), sibling))
                passed[-1].start()
        for w in range(n_w):
            copy(w, 0, sibling, me).wait_recv()
            for j, chip in enumerate(chips):
                copy(w, 4 + j, (*chip, 1 - c), me).wait_recv()
        for cp in first + passed:
            cp.wait_send()
        for w in range(n_w):
            local[w].wait()

    return pl.pallas_call(
        body, name=name, in_specs=[_HBM] * n_w, out_specs=[_HBM] * n_w,
        out_shape=_exchange_shapes([GATHER] * n_w, srcs), scratch_shapes=_exchange_sems(n_w))(*srcs)


def _row_offsets(arrays):
    offs, rows = [], 0
    for a in arrays:
        offs.append(rows)
        rows += a.shape[0]
    return offs, -(-rows // 8) * 8


def _final_exchange(vecs):
    n_p = len(vecs)
    offs, rows = _row_offsets(vecs)

    def body(*refs):
        g_refs = refs[:n_p]
        rall, pk, send_sems, recv_sems, loc_sem = refs[n_p:]
        x, y, c = lax.axis_index("x"), lax.axis_index("y"), lax.axis_index("c")
        me = 4 * x + 2 * y + c
        pk[...] = jnp.zeros(pk.shape, F32)
        for p in range(n_p):
            r, n = g_refs[p].shape
            pk[offs[p]:offs[p] + r, 0:n] = g_refs[p][...]

        def remote(k):
            return pltpu.make_async_remote_copy(
                src_ref=pk, dst_ref=rall.at[me], send_sem=send_sems.at[k - 1], recv_sem=recv_sems.at[k - 1],
                device_id=_peer(k, x, y, c), device_id_type=MESH)

        def arrival(k):
            px, py, pc = _peer(k, x, y, c)
            return pltpu.make_async_remote_copy(
                src_ref=pk, dst_ref=rall.at[4 * px + 2 * py + pc], send_sem=send_sems.at[k - 1],
                recv_sem=recv_sems.at[k - 1], device_id=(px, py, pc), device_id_type=MESH)

        local = pltpu.make_async_copy(pk, rall.at[me], loc_sem)
        local.start()
        for k in range(1, N_DEV):
            remote(k).start()
        local.wait()
        for k in range(1, N_DEV):
            arrival(k).wait_recv()
        for k in range(1, N_DEV):
            remote(k).wait_send()

    vm = pl.BlockSpec(memory_space=pltpu.VMEM)
    return pl.pallas_call(
        body, name="final_exchange", in_specs=[vm] * n_p, out_specs=vm, out_shape=_sds((N_DEV, rows, D)),
        scratch_shapes=[pltpu.VMEM((rows, D), F32),
                        pltpu.SemaphoreType.DMA((N_DEV - 1,)), pltpu.SemaphoreType.DMA((N_DEV - 1,)),
                        pltpu.SemaphoreType.DMA],
    )(*vecs)


def _small_adam(rall, big_parts, big, ws, ms, vs):
    n_p = len(ws)
    packed = [w for p, w in enumerate(ws) if p != big] + [jax.ShapeDtypeStruct((1, HB), F32)]
    offs, _ = _row_offsets(packed)
    offs = offs[:big] + [None] + offs[big:]

    def total(ref, sl):
        g = ref[(0,) + sl]
        for j in range(1, N_DEV):
            g = g + ref[(j,) + sl]
        return g

    def body(*refs):
        rall_ref, big_ref = refs[:2]
        w_refs, m_refs, v_refs = refs[2:2 + n_p], refs[2 + n_p:2 + 2 * n_p], refs[2 + 2 * n_p:2 + 3 * n_p]
        outs = refs[2 + 3 * n_p:]
        for p in range(n_p):
            r, n = w_refs[p].shape
            if p == big:
                g = total(big_ref, (slice(0, r), slice(0, n)))
            else:
                g = total(rall_ref, (slice(offs[p], offs[p] + r), slice(0, n)))
            delta, m2, v2 = _adamw(w_refs[p][...], g, m_refs[p][...], v_refs[p][...])
            outs[p][...] = g
            outs[n_p + p][...] = delta
            outs[2 * n_p + p][...] = m2
            outs[3 * n_p + p][...] = v2
        outs[4 * n_p][...] = total(rall_ref, (slice(offs[n_p], offs[n_p] + 1), slice(0, HB)))

    vm = pl.BlockSpec(memory_space=pltpu.VMEM)
    res = pl.pallas_call(
        body, name="small_adam", in_specs=[vm] * (2 + 3 * n_p), out_specs=[vm] * (4 * n_p + 1),
        out_shape=[_sds(w.shape) for w in ws] * 4 + [_sds((1, HB))],
        compiler_params=pltpu.CompilerParams(vmem_limit_bytes=_VMEM_LIMIT),
    )(rall, big_parts, *ws, *ms, *vs)
    return res[:n_p], res[n_p:2 * n_p], res[2 * n_p:3 * n_p], res[3 * n_p:4 * n_p], res[4 * n_p]


def _device_sum(p_ref):
    g = p_ref[0].astype(F32)
    for j in range(1, N_DEV):
        g = g + p_ref[j].astype(F32)
    return g


def _shard_adam(name, parts, w, m, v, tr):
    a0, b0 = w.shape
    n_p = len(parts)
    b = parts[0].shape[2]
    first = [0]
    for p in parts:
        first.append(first[-1] + p.shape[1] // tr)

    def body(*refs):
        p_refs = refs[:n_p]
        w_ref, m_ref, v_ref, g_out, d_out, m_out, v_out = refs[n_p:]
        i = pl.program_id(0)
        g = _device_sum(p_refs[0])
        for k in range(1, n_p):
            g = jnp.where(i >= first[k], _device_sum(p_refs[k]), g)
        g = g[:, 0:b0]
        delta, m2, v2 = _adamw(w_ref[...], g, m_ref[...], v_ref[...])
        g_out[...] = g
        d_out[...] = delta
        m_out[...] = m2
        v_out[...] = v2

    def part_spec(k):
        last = first[k + 1] - first[k] - 1
        return pl.BlockSpec((N_DEV, tr, b), lambda i: (0, jnp.minimum(jnp.maximum(i - first[k], 0), last), 0))

    blk = pl.BlockSpec((tr, b0), lambda i: (i, 0))
    return _pcall(
        body, name, (a0 // tr,), [part_spec(k) for k in range(n_p)] + [blk, blk, blk],
        [blk] * 4, [_sds((a0, b0))] * 4)(*parts, w, m, v)


def _gate_up_adam(parts, ws, ms, vs):
    tc = 256

    def body(p_ref, wg, wu, mg, mu, vg, vu, *outs):
        g = _device_sum(p_ref)
        for k, (w_ref, m_ref, v_ref) in enumerate(((wg, mg, vg), (wu, mu, vu))):
            gk = g[k * FF_PAD:k * FF_PAD + FF_SH]
            delta, m2, v2 = _adamw(w_ref[...], gk, m_ref[...], v_ref[...])
            for o, val in zip(outs[4 * k:4 * k + 4], (gk, delta, m2, v2)):
                o[...] = val

    blk = pl.BlockSpec((FF_SH, tc), lambda i: (0, i))
    res = _pcall(
        body, "adam_w_gate_up", (D // tc,), [pl.BlockSpec((N_DEV, 2 * FF_PAD, tc), lambda i: (0, 0, i))] + [blk] * 6,
        [blk] * 8, [_sds((FF_SH, D))] * 8)(parts, *ws, *ms, *vs)
    return res[:4], res[4:]


def _fwd_in(x, g_pre, w_in_al, tm):
    T = x.shape[0]

    def body(x_ref, g_ref, w_ref, xm_ref, xh_ref, u_ref):
        nx, _ = _rms(x_ref[...], D)
        u = (nx * g_ref[...]).astype(BF)
        u_ref[...] = u
        xm_ref[...] = jnp.dot(u, w_ref[:, 0:MLA_IN], preferred_element_type=F32)
        xh_ref[...] = jnp.dot(u, w_ref[:, MLA_IN:XP_W], preferred_element_type=F32)

    return _pcall(body, "fwd_in", (T // tm,),
                  [_rows(tm, D), _full((1, D)), _full((D, XP_W))],
                  [_rows(tm, MLA_IN), _rows(tm, 4 * G_W), _rows(tm, D)],
                  [_sds((T, MLA_IN)), _sds((T, 4 * G_W)), _sds((T, D), BF)])(x, g_pre, w_in_al)


def _rope(blk, ta, tb1, tb2):
    return blk * ta + pltpu.roll(blk, HB - ROPE // 2, 1) * tb1 + pltpu.roll(blk, ROPE // 2, 1) * tb2


def _unrope(d, ta, tb1, tb2):
    return d * ta + pltpu.roll(d * tb1, ROPE // 2, 1) + pltpu.roll(d * tb2, HB - ROPE // 2, 1)


def _mla_prep(xp, tabs, g_q, g_kv, w_uq, w_uk, w_uv, tm):
    T = xp.shape[0]
    W = N_HEADS * HB

    def body(xp_ref, ta_ref, tb1_ref, tb2_ref, gq_ref, gkv_ref, wuq_ref, wuk_ref, wuv_ref, q_ref, qs_ref, k_ref, v_ref):
        ta, tb1, tb2 = ta_ref[...], tb1_ref[...], tb2_ref[...]
        nq, _ = _rms(xp_ref[:, 0:Q_RANK], Q_RANK)
        nkv, _ = _rms(xp_ref[:, Q_RANK:Q_RANK + KV_RANK], KV_RANK)
        nkv = (nkv * gkv_ref[...]).astype(BF)
        qpre = _dot(nq * gq_ref[...], wuq_ref[...])
        kpre = jnp.dot(nkv, wuk_ref[...], preferred_element_type=F32)
        v = jnp.dot(nkv, wuv_ref[...], preferred_element_type=F32)
        lane = lax.broadcasted_iota(jnp.int32, (tm, W), 1)
        v_ref[...] = jnp.where((lane & (HB - 1)) == V_DIM, 1.0, v).astype(BF)
        kr = _rope(pltpu.roll(xp_ref[:, Q_RANK + KV_RANK:MLA_IN], NOPE, 1), ta, tb1, tb2)
        for h in range(N_HEADS):
            sl = slice(h * HB, (h + 1) * HB)
            qr = _rope(qpre[:, sl], ta, tb1, tb2)
            q_ref[:, sl] = qr.astype(BF)
            qs_ref[:, sl] = (qr * (QK_DIM ** -0.5 * LOG2E)).astype(BF)
            k_ref[:, sl] = (kpre[:, sl] + kr).astype(BF)

    tab = _rows(tm, HB)
    return _pcall(body, "mla_prep", (T // tm,),
                  [_rows(tm, MLA_IN), tab, tab, tab, _full((1, Q_RANK)), _full((1, KV_RANK)),
                   _full((Q_RANK, W)), _full((KV_RANK, W)), _full((KV_RANK, W))],
                  [_rows(tm, W)] * 4, [_sds((T, W), BF)] * 4)(xp, *tabs, g_q, g_kv, w_uq, w_uk, w_uv)


def _flash_fwd(q, k, v, tq, exchange=None):
    T = q.shape[0]
    hp = _AH_FWD
    W = hp * HB

    def body(q_ref, k_ref, v_ref, o_ref, lse_ref):
        i = pl.program_id(1)

        def blk(j, carry, masked):
            st = pl.multiple_of(j * tq, tq)
            out = []
            for h in range(hp):
                ls = slice(h * HB, (h + 1) * HB)
                m, acc = carry[h]
                s = _dot_nt(q_ref[:, ls], k_ref[pl.ds(st, tq), ls])
                if masked:
                    r = lax.broadcasted_iota(jnp.int32, (tq, tq), 0)
                    c = lax.broadcasted_iota(jnp.int32, (tq, tq), 1)
                    s = jnp.where(c <= r, s, NEG)
                m2 = jnp.maximum(m, jnp.max(s, -1, keepdims=True))
                p = jnp.exp2(s - m2)
                out.append((m2, jnp.exp2(m - m2) * acc + _dot(p, v_ref[pl.ds(st, tq), ls])))
            return tuple(out)

        init = tuple((jnp.full((tq, 1), NEG, F32), jnp.zeros((tq, HB), F32)) for _ in range(hp))
        carry = lax.fori_loop(0, i, lambda j, cr: blk(j, cr, False), init)
        res = blk(i, carry, True)
        lane = lax.broadcasted_iota(jnp.int32, (tq, HB), 1)
        for h in range(hp):
            ls = slice(h * HB, (h + 1) * HB)
            m, acc = res[h]
            l = acc[:, V_DIM:V_DIM + 1]
            o_ref[:, ls] = jnp.where(lane < V_DIM, acc / l, 0.0).astype(BF)
            lse_ref[:, ls] = jnp.broadcast_to(m * (1.0 / LOG2E) + jnp.log(l), (tq, HB))

    qs = pl.BlockSpec((tq, W), lambda h, i: (i, h))
    kvs = pl.BlockSpec((T, W), lambda h, i: (0, h))
    return _pcall(body, "flash_fwd", (N_HEADS // hp, T // tq), [qs, kvs, kvs], [qs, qs],
                  [_sds((T, N_HEADS * HB), BF), _sds((T, N_HEADS * HB))], exchange=exchange)(q, k, v)


def _gates(hq, hf, lb):
    sig = _sigmoid(hf)
    f = lb + (1.0 - lb) * sig
    sq = _sigmoid(hq)
    return hq * sq, 1.0 - f, f, jnp.log(f), sig, sq


def _lower_bound(lbl_ref):
    l0, l1 = lbl_ref[0:1, :], lbl_ref[1:2, :]
    mx = jnp.maximum(l0, l1)
    e0, e1 = jnp.exp(l0 - mx), jnp.exp(l1 - mx)
    return e0 / (e0 + e1)


def _split3(x):
    hi = x.astype(BF)
    r1 = x - hi.astype(F32)
    mid = r1.astype(BF)
    lo = (r1 - mid.astype(F32)).astype(BF)
    return hi, mid, lo


def _tri_mm(tri, x):
    hi, mid, lo = _split3(x)
    mm = lambda t: jnp.dot(tri, t, preferred_element_type=F32)
    return mm(hi) + mm(mid) + mm(lo)


def _intra_codes(sub):
    row = lax.broadcasted_iota(jnp.int32, (CHUNK, CHUNK), 0)
    col = lax.broadcasted_iota(jnp.int32, (CHUNK, CHUNK), 1)
    return sub, row, col


def _intra(q, k, b2, b_s, codes, da=None):
    grad = da is not None
    pow2 = (lambda x: jnp.exp2(jnp.minimum(x, 0.0))) if grad else jnp.exp2
    sub, row, col = codes
    a = jnp.zeros((CHUNK, CHUNK), F32)
    dq = jnp.zeros((CHUNK, G_DIM), F32)
    dk = jnp.zeros((CHUNK, G_DIM), F32)
    for i in range(1, CHUNK // sub):
        b0 = b_s[sub * i - 1:sub * i, :]
        eq, ek = pow2(b2 - b0), pow2(b0 - b2)
        mask = ((row // sub) == i) & (col < sub * i)
        if grad:
            dai = jnp.where(mask, da, 0.0)
            dq = dq + _dot(dai, k * ek) * eq
            dk = dk + _dot_tn(dai, q * eq) * ek
        else:
            a = jnp.where(mask, _dot_nt(q * eq, k * ek), a)
    for d in range(sub):
        ksh = pltpu.roll(k, d, 0) if d else k
        bsh = pltpu.roll(b2, d, 0) if d else b2
        e = pow2(b2 - bsh)
        mask = (col == row - d) & ((row & (sub - 1)) >= d)
        if grad:
            g = jnp.sum(jnp.where(mask, da, 0.0), -1, keepdims=True) * e
            dq = dq + g * ksh
            cb = g * q
            dk = dk + (pltpu.roll(cb, CHUNK - d, 0) if d else cb)
        else:
            a = jnp.where(mask, jnp.sum(q * ksh * e, -1, keepdims=True), a)
    return (dq, dk) if grad else a


def _hgrn_fwd(xp, lb_logits, g_hn, exchange=None):
    T = xp.shape[0]
    tb = min(_TB, T)
    ncb = tb // CHUNK
    hp = _HP
    W = hp * G_DIM

    def body(hq_ref, hf_ref, hi_ref, hg_ref, lbl_ref, ghn_ref, out_ref, oraw_ref, sall_ref, aall_ref, ball_ref,
             st_ref, b_s):
        lb_all = _lower_bound(lbl_ref)

        @pl.when(pl.program_id(1) == 0)
        def _():
            st_ref[...] = jnp.zeros(st_ref.shape, F32)

        row = lax.broadcasted_iota(jnp.int32, (CHUNK, CHUNK), 0)
        col = lax.broadcasted_iota(jnp.int32, (CHUNK, CHUNK), 1)
        tri = (col <= row).astype(BF)
        codes = _intra_codes(SUB)

        def chunk(c, carry):
            sl = pl.ds(pl.multiple_of(c * CHUNK, CHUNK), CHUNK)
            for h in range(hp):
                ls = slice(h * G_DIM, (h + 1) * G_DIM)
                q, k, _, lf, _, _ = _gates(hq_ref[sl, ls], hf_ref[sl, ls], lb_all[:, ls])
                v = hi_ref[sl, ls]
                b2 = _tri_mm(tri, lf) * LOG2E
                b_s[h] = b2
                ball_ref[sl, ls] = b2
                st = st_ref[h]
                sall_ref[c, h] = st
                a = _intra(q, k, b2, b_s.at[h], codes)
                aall_ref[c, h] = a
                o = _dot_nt(q * jnp.exp2(b2), st) + _dot(a, v)
                bl = b_s[h, CHUNK - 1:CHUNK, :]
                st_ref[h] = st * jnp.exp2(bl) + _dot_tn(v, k * jnp.exp2(bl - b2))
                oraw_ref[sl, ls] = o
                n, _ = _rms(o, G_DIM)
                hg = hg_ref[sl, ls]
                out_ref[sl, ls] = n * ghn_ref[:, ls] * (hg * _sigmoid(hg))
            return carry

        lax.fori_loop(0, ncb, chunk, 0, unroll=4)

    col_blk = lambda j: pl.BlockSpec((tb, W), lambda p, t: (t, j * (G_HEADS // hp) + p))
    head = pl.BlockSpec((tb, W), lambda p, t: (t, p))
    return _pcall(
        body, "hgrn_fwd", (G_HEADS // hp, T // tb),
        [col_blk(0), col_blk(1), col_blk(2), col_blk(3),
         pl.BlockSpec((2, W), lambda p, t: (0, p)), pl.BlockSpec((1, W), lambda p, t: (0, p))],
        [head, head, pl.BlockSpec((ncb, hp, G_DIM, G_DIM), lambda p, t: (t, p, 0, 0)),
         pl.BlockSpec((ncb, hp, CHUNK, CHUNK), lambda p, t: (t, p, 0, 0)), head],
        [_sds((T, G_W)), _sds((T, G_W)), _sds((T // CHUNK, G_HEADS, G_DIM, G_DIM)),
         _sds((T // CHUNK, G_HEADS, CHUNK, CHUNK)), _sds((T, G_W))],
        scratch=[pltpu.VMEM((hp, G_DIM, G_DIM), F32), pltpu.VMEM((hp, CHUNK, G_DIM), F32)], exchange=exchange,
    )(xp, xp, xp, xp, lb_logits, g_hn)


def _fwd_out(o_pad, o_hgrn, x, g_on, w_out, g_post, g_fpre, tm):
    T = x.shape[0]

    def body(o_ref, oh_ref, x_ref, gon_ref, w_ref, gpost_ref, gfpre_ref, h1_ref, y1_ref, z_ref, mix_ref):
        for h in range(N_HEADS):
            sl = slice(h * HB, (h + 1) * HB)
            n, _ = _rms(o_ref[:, sl].astype(F32), V_DIM)
            mix_ref[:, sl] = (n * gon_ref[:, sl]).astype(BF)
        mix_ref[:, N_HEADS * HB:MIX_W] = oh_ref[...].astype(BF)
        y1 = jnp.dot(mix_ref[...], w_ref[...], preferred_element_type=F32)
        y1_ref[...] = y1
        ny, _ = _rms(y1, D)
        h1 = x_ref[...] + ny * gpost_ref[...]
        h1_ref[...] = h1
        nh, _ = _rms(h1, D)
        z_ref[...] = (nh * gfpre_ref[...]).astype(BF)

    return _pcall(body, "fwd_out", (T // tm,),
                  [_rows(tm, N_HEADS * HB), _rows(tm, G_W), _rows(tm, D), _full((1, N_HEADS * HB)),
                   _full((MIX_W, D)), _full((1, D)), _full((1, D))],
                  [_rows(tm, D), _rows(tm, D), _rows(tm, D), _rows(tm, MIX_W)],
                  [_sds((T, D)), _sds((T, D)), _sds((T, D), BF), _sds((T, MIX_W), BF)],
                  )(o_pad, o_hgrn, x, g_on, w_out, g_post, g_fpre)


def _ffn_fwd(z, wgu, wd, h1, tgt, g_fpost, tm, nd):
    T = z.shape[0]
    fb = nd * FF_PAD
    nf = wd.shape[0] // fb

    def body(z_ref, wgu_ref, wd_ref, h1_ref, t_ref, gp_ref,
             as_ref, bs_ref, ff_ref, dh2_ref, dy2_ref, dgp_ref, loss_ref, acc):
        i, j = pl.program_id(0), pl.program_id(1)
        gu = _dot_nt(z_ref[...], wgu_ref[...])
        piece = lambda n: gu[:, n * FF_PAD:(n + 1) * FF_PAD]
        g = piece(0) if nd == 1 else jnp.concatenate([piece(2 * n) for n in range(nd)], 1)
        u = piece(1) if nd == 1 else jnp.concatenate([piece(2 * n + 1) for n in range(nd)], 1)
        s = _sigmoid(g)
        b = g * s
        ff = (b * u).astype(BF)
        as_ref[...] = (u * _dsilu(g, s)).astype(BF)
        bs_ref[...] = b.astype(BF)
        ff_ref[...] = ff
        part = jnp.dot(ff, wd_ref[...], preferred_element_type=F32)

        @pl.when(j == 0)
        def _():
            acc[...] = part

        @pl.when(j > 0)
        def _():
            acc[...] += part

        @pl.when((i == 0) & (j == 0))
        def _():
            dgp_ref[...] = jnp.zeros(dgp_ref.shape, F32)
            loss_ref[...] = jnp.zeros(loss_ref.shape, F32)

        @pl.when(j == nf - 1)
        def _():
            ny, r = _rms(acc[...], D)
            err = h1_ref[...] + ny * gp_ref[...] - t_ref[...]
            loss_ref[...] += 0.5 * jnp.sum(jnp.sum(err * err, -1, keepdims=True) * (1.0 / D), 0, keepdims=True)
            dh2 = err * (1.0 / D)
            dh2_ref[...] = dh2
            dy2, dgp = _rms_bwd(ny, r, gp_ref[...], dh2, D)
            dy2_ref[...] = dy2.astype(BF)
            dgp_ref[...] += dgp

    tok = lambda n: pl.BlockSpec((tm, n), lambda i, j: (i, 0))
    col = pl.BlockSpec((tm, fb), lambda i, j: (i, j))
    return _pcall(
        body, "ffn_fwd", (T // tm, nf),
        [tok(D), pl.BlockSpec((2 * fb, D), lambda i, j: (j, 0)), pl.BlockSpec((fb, D), lambda i, j: (j, 0)),
         tok(D), tok(D), _full((1, D))],
        [col, col, col, tok(D), tok(D), _full((1, D)), _full((1, HB))],
        [_sds((T, nf * fb), BF)] * 3 + [_sds((T, D)), _sds((T, D), BF), _sds((1, D)), _sds((1, HB))],
        scratch=[pltpu.VMEM((tm, D), F32)],
    )(z, wgu, wd, h1, tgt, g_fpost)


def _dsilu(x, s):
    return s * (1.0 + x * (1.0 - s))


def _ffn_bwd_x(dy2, gs, us, wgu, wd, h1, y1, dh2, g_fpre, g_post, tm):
    T = dy2.shape[0]
    nf = wd.shape[0] // _FB

    def body(dy2_ref, gs_ref, us_ref, wgu_ref, wd_ref, h1_ref, y1_ref, dh2_ref, gf_ref, gp_ref,
             dgu_ref, dh1_ref, dy1_ref, dgf_ref, dgp_ref, acc):
        i, j = pl.program_id(0), pl.program_id(1)
        dff = _dot_nt(dy2_ref[...], wd_ref[...])
        dg = (dff * gs_ref[...].astype(F32)).astype(BF)
        du = (dff * us_ref[...].astype(F32)).astype(BF)
        dgu = jnp.concatenate([dg[:, 0:FF_PAD], du[:, 0:FF_PAD], dg[:, FF_PAD:_FB], du[:, FF_PAD:_FB]], 1)
        dgu_ref[...] = dgu
        part = jnp.dot(dgu, wgu_ref[...], preferred_element_type=F32)

        @pl.when(j == 0)
        def _():
            acc[...] = part

        @pl.when(j > 0)
        def _():
            acc[...] += part

        @pl.when((i == 0) & (j == 0))
        def _():
            dgf_ref[...] = jnp.zeros(dgf_ref.shape, F32)
            dgp_ref[...] = jnp.zeros(dgp_ref.shape, F32)

        @pl.when(j == nf - 1)
        def _():
            nh, rh = _rms(h1_ref[...], D)
            dh, dgf = _rms_bwd(nh, rh, gf_ref[...], acc[...], D)
            dh1 = dh2_ref[...] + dh
            dh1_ref[...] = dh1
            dgf_ref[...] += dgf
            ny, ry = _rms(y1_ref[...], D)
            dy1, dgp = _rms_bwd(ny, ry, gp_ref[...], dh1, D)
            dy1_ref[...] = dy1.astype(BF)
            dgp_ref[...] += dgp

    tok = lambda n: pl.BlockSpec((tm, n), lambda i, j: (i, 0))
    col = pl.BlockSpec((tm, _FB), lambda i, j: (i, j))
    return _pcall(
        body, "ffn_bwd_x", (T // tm, nf),
        [tok(D), col, col, pl.BlockSpec((2 * _FB, D), lambda i, j: (j, 0)), pl.BlockSpec((_FB, D), lambda i, j: (j, 0)),
         tok(D), tok(D), tok(D), _full((1, D)), _full((1, D))],
        [pl.BlockSpec((tm, 2 * _FB), lambda i, j: (i, j)), tok(D), tok(D), _full((1, D)), _full((1, D))],
        [_sds((T, 2 * nf * _FB), BF), _sds((T, D)), _sds((T, D), BF), _sds((1, D)), _sds((1, D))],
        scratch=[pltpu.VMEM((tm, D), F32)],
    )(dy2, gs, us, wgu, wd, h1, y1, dh2, g_fpre, g_post)


def _ffn_bwd_w(z, ffs, dgu, dy2, tm):
    T = z.shape[0]
    nf = ffs.shape[1] // _FB
    nt = T // tm

    def body(z_ref, ff_ref, dgu_ref, dy2_ref, dwgu_ref, dwd_ref, agu, ad):
        i = pl.program_id(1)
        pgu = _dot_tn(dgu_ref[...], z_ref[...])
        pd = _dot_tn(ff_ref[...], dy2_ref[...])

        @pl.when(i == 0)
        def _():
            agu[...] = pgu
            ad[...] = pd

        @pl.when(i > 0)
        def _():
            agu[...] += pgu
            ad[...] += pd

        @pl.when(i == nt - 1)
        def _():
            dwgu_ref[...] = agu[...].astype(BF)
            dwd_ref[...] = ad[...].astype(BF)

    F = nf * _FB
    tok = lambda n: pl.BlockSpec((tm, n), lambda j, i: (i, 0))
    return _pcall(
        body, "ffn_bwd_w", (nf, nt),
        [tok(D), pl.BlockSpec((tm, _FB), lambda j, i: (i, j)), pl.BlockSpec((tm, 2 * _FB), lambda j, i: (i, j)), tok(D)],
        [pl.BlockSpec((2 * _FB, D), lambda j, i: (j, 0)), pl.BlockSpec((_FB, D), lambda j, i: (j, 0))],
        [_sds((2 * F, D), BF), _sds((F, D), BF)],
        scratch=[pltpu.VMEM((2 * _FB, D), F32), pltpu.VMEM((_FB, D), F32)],
    )(z, ffs, dgu, dy2)


def _out_bwd(dy1, mix, o_pad, w_out, g_on, tm):
    T = dy1.shape[0]
    W = N_HEADS * HB

    def body(dy1_ref, mix_ref, o_ref, w_ref, gon_ref, do_ref, dl_ref, dohg_ref, dw_ref, dgon_ref):
        i = pl.program_id(0)
        dy1v = dy1_ref[...]
        dmix = _dot_nt(dy1v, w_ref[...])
        pw = _dot_tn(mix_ref[...], dy1v)

        @pl.when(i == 0)
        def _():
            dw_ref[...] = pw
            dgon_ref[...] = jnp.zeros(dgon_ref.shape, F32)

        @pl.when(i > 0)
        def _():
            dw_ref[...] += pw

        for h in range(N_HEADS):
            sl = slice(h * HB, (h + 1) * HB)
            ov = o_ref[:, sl].astype(F32)
            n, r = _rms(ov, V_DIM)
            do, dg = _rms_bwd(n, r, gon_ref[:, sl], dmix[:, sl], V_DIM)
            dgon_ref[:, sl] += dg
            do_ref[:, sl] = do.astype(BF)
            dl_ref[:, sl] = jnp.broadcast_to(jnp.sum(do * ov, -1, keepdims=True), (tm, HB))
        dohg_ref[...] = dmix[:, W:MIX_W]

    return _pcall(body, "out_bwd", (T // tm,),
                  [_rows(tm, D), _rows(tm, MIX_W), _rows(tm, W), _full((MIX_W, D)), _full((1, W))],
                  [_rows(tm, W), _rows(tm, W), _rows(tm, G_W), _full((MIX_W, D)), _full((1, W))],
                  [_sds((T, W), BF), _sds((T, W)), _sds((T, G_W)), _sds((MIX_W, D)), _sds((1, W))],
                  )(dy1, mix, o_pad, w_out, g_on)


def _flash_bwd(q, k, v, do, lse, dl, tq, exchange=None):
    T = q.shape[0]
    nq = T // tq
    scale = QK_DIM ** -0.5
    hp = _AH
    W = hp * HB

    def body(k_ref, v_ref, q_ref, do_ref, lse_ref, dl_ref, dk_ref, dv_ref, dq_ref):
        j = pl.program_id(1)

        @pl.when(j == 0)
        def _():
            dq_ref[...] = jnp.zeros(dq_ref.shape, F32)

        def blk(i, carry, masked):
            sl = pl.ds(pl.multiple_of(i * tq, tq), tq)
            out = []
            for h in range(hp):
                ls = slice(h * HB, (h + 1) * HB)
                dk, dv = carry[h]
                kv, vv = k_ref[:, ls], v_ref[:, ls]
                qv, dov = q_ref[sl, ls], do_ref[sl, ls]
                s = _dot_nt(qv, kv) * scale
                if masked:
                    r = lax.broadcasted_iota(jnp.int32, (tq, tq), 0)
                    c = lax.broadcasted_iota(jnp.int32, (tq, tq), 1)
                    s = jnp.where(c <= r, s, NEG)
                p = jnp.exp(s - lse_ref[sl, h * HB:h * HB + 1])
                ds = p * (_dot_nt(dov, vv) - dl_ref[sl, h * HB:h * HB + 1]) * scale
                dq_ref[sl, ls] += _dot(ds, kv)
                out.append((dk + _dot_tn(ds, qv), dv + _dot_tn(p, dov)))
            return tuple(out)

        zero = jnp.zeros((tq, HB), F32)
        carry = blk(j, tuple((zero, zero) for _ in range(hp)), True)
        res = lax.fori_loop(j + 1, nq, lambda i, cr: blk(i, cr, False), carry)
        for h in range(hp):
            ls = slice(h * HB, (h + 1) * HB)
            dk_ref[:, ls] = res[h][0].astype(BF)
            dv_ref[:, ls] = res[h][1].astype(BF)

    tile = pl.BlockSpec((tq, W), lambda h, j: (j, h))
    whole = pl.BlockSpec((T, W), lambda h, j: (0, h))
    return _pcall(body, "flash_bwd", (N_HEADS // hp, nq), [tile, tile, whole, whole, whole, whole],
                  [tile, tile, whole], [_sds((T, N_HEADS * HB), BF)] * 2 + [_sds((T, N_HEADS * HB))],
                  exchange=exchange)(k, v, q, do, lse, dl)


def _mla_prep_bwd(xp, tabs, dq, dk, dv, g_q, g_kv, w_uq, w_uk, w_uv, tm):
    T = xp.shape[0]
    W = N_HEADS * HB

    def body(xp_ref, ta_ref, tb1_ref, tb2_ref, dq_ref, dk_ref, dv_ref, gq_ref, gkv_ref, wuq_ref, wuk_ref, wuv_ref,
             dxp_ref, dwuq_ref, dwuk_ref, dwuv_ref, dgq_ref, dgkv_ref, dqp):
        i = pl.program_id(0)
        ta, tb1, tb2 = ta_ref[...], tb1_ref[...], tb2_ref[...]
        nq, rq = _rms(xp_ref[:, 0:Q_RANK], Q_RANK)
        nkv, rkv = _rms(xp_ref[:, Q_RANK:Q_RANK + KV_RANK], KV_RANK)
        dkr = jnp.zeros((tm, HB), F32)
        for h in range(N_HEADS):
            sl = slice(h * HB, (h + 1) * HB)
            dqp[:, sl] = _unrope(dq_ref[:, sl], ta, tb1, tb2).astype(BF)
            dkr = dkr + dk_ref[:, sl].astype(F32)
        dkr = pltpu.roll(_unrope(dkr, ta, tb1, tb2), HB - NOPE, 1)
        lane = lax.broadcasted_iota(jnp.int32, (tm, HB), 1)
        dxp_ref[:, Q_RANK + KV_RANK:MLA_IN] = jnp.where(lane < ROPE, dkr, 0.0)
        dqpv = dqp[...]
        dkv, dvv = dk_ref[...], dv_ref[...]
        nqs = (nq * gq_ref[...]).astype(BF)
        nkvs = (nkv * gkv_ref[...]).astype(BF)
        pq, pk, pv = _dot_tn(nqs, dqpv), _dot_tn(nkvs, dkv), _dot_tn(nkvs, dvv)
        dcq, dgq = _rms_bwd(nq, rq, gq_ref[...], _dot_nt(dqpv, wuq_ref[...]), Q_RANK)
        dckv, dgkv = _rms_bwd(nkv, rkv, gkv_ref[...], _dot_nt(dkv, wuk_ref[...]) + _dot_nt(dvv, wuv_ref[...]), KV_RANK)
        dxp_ref[:, 0:Q_RANK] = dcq
        dxp_ref[:, Q_RANK:Q_RANK + KV_RANK] = dckv

        @pl.when(i == 0)
        def _():
            dwuq_ref[...] = pq
            dwuk_ref[...] = pk
            dwuv_ref[...] = pv
            dgq_ref[...] = dgq
            dgkv_ref[...] = dgkv

        @pl.when(i > 0)
        def _():
            dwuq_ref[...] += pq
            dwuk_ref[...] += pk
            dwuv_ref[...] += pv
            dgq_ref[...] += dgq
            dgkv_ref[...] += dgkv

    tab = _rows(tm, HB)
    return _pcall(
        body, "mla_prep_bwd", (T // tm,),
        [_rows(tm, MLA_IN), tab, tab, tab, _rows(tm, W), _rows(tm, W), _rows(tm, W), _full((1, Q_RANK)),
         _full((1, KV_RANK)), _full((Q_RANK, W)), _full((KV_RANK, W)), _full((KV_RANK, W))],
        [_rows(tm, MLA_IN), _full((Q_RANK, W)), _full((KV_RANK, W)), _full((KV_RANK, W)), _full((1, Q_RANK)),
         _full((1, KV_RANK))],
        [_sds((T, MLA_IN)), _sds((Q_RANK, W)), _sds((KV_RANK, W)), _sds((KV_RANK, W)), _sds((1, Q_RANK)),
         _sds((1, KV_RANK))],
        scratch=[pltpu.VMEM((tm, W), BF)],
    )(xp, *tabs, dq, dk, dv, g_q, g_kv, w_uq, w_uk, w_uv)


def _hgrn_bwd(xp, o_raw, s_all, a_all, b_all, d_out, lb_logits, g_hn, exchange=None):
    T = xp.shape[0]
    tb = min(_TB_BWD, T)
    ncb = tb // CHUNK
    nb = T // tb
    hp = _HP
    W = hp * G_DIM

    def body(hq_ref, hf_ref, hi_ref, hg_ref, o_ref, sall_ref, aall_ref, ball_ref, dout_ref, lbl_ref, ghn_ref,
             dhq_ref, dhf_ref, dhi_ref, dhg_ref, dlbl_ref, dghn_ref, dst_ref, b_s, acc_lb, acc_g):
        t = pl.program_id(1)
        lb_all = _lower_bound(lbl_ref)

        @pl.when(t == 0)
        def _():
            dst_ref[...] = jnp.zeros(dst_ref.shape, F32)
            acc_lb[...] = jnp.zeros(acc_lb.shape, F32)
            acc_g[...] = jnp.zeros(acc_g.shape, F32)

        row = lax.broadcasted_iota(jnp.int32, (CHUNK, CHUNK), 0)
        col = lax.broadcasted_iota(jnp.int32, (CHUNK, CHUNK), 1)
        tri_t = (col >= row).astype(BF)
        codes = _intra_codes(SUB)
        last = lax.broadcasted_iota(jnp.int32, (CHUNK, G_DIM), 0) == CHUNK - 1

        def chunk(cc, carry):
            c = ncb - 1 - cc
            sl = pl.ds(pl.multiple_of(c * CHUNK, CHUNK), CHUNK)
            for h in range(hp):
                ls = slice(h * G_DIM, (h + 1) * G_DIM)
                lb, ghn = lb_all[:, ls], ghn_ref[:, ls]
                hq, hg = hq_ref[sl, ls], hg_ref[sl, ls]
                q, k, f, _, sig, sq = _gates(hq, hf_ref[sl, ls], lb)
                v = hi_ref[sl, ls]
                b2 = ball_ref[sl, ls]
                b_s[h] = b2
                st = sall_ref[c, h]
                dstn = dst_ref[h]
                o = o_ref[sl, ls]
                dout = dout_ref[sl, ls]
                n, r = _rms(o, G_DIM)
                sg = _sigmoid(hg)
                dhg_ref[sl, ls] = dout * (n * ghn) * _dsilu(hg, sg)
                do, dg = _rms_bwd(n, r, ghn, dout * (hg * sg), G_DIM)
                acc_g[:, ls] += dg
                eb = jnp.exp2(b2)
                bl = b_s[h, CHUNK - 1:CHUNK, :]
                ebl = jnp.exp2(bl)
                ekd = jnp.exp2(bl - b2)
                kd = k * ekd
                a = aall_ref[c, h]
                dq_i, dk_i = _intra(q, k, b2, b_s.at[h], codes, _dot_nt(do, v))
                dhi_ref[sl, ls] = _dot_tn(a, do) + _dot_nt(kd, dstn)
                dk_state = _dot(v, dstn) * ekd
                dq = dq_i + _dot(do, st) * eb
                dk = dk_i + dk_state
                dbl = jnp.sum(k * dk_state, 0, keepdims=True) + ebl * jnp.sum(dstn * st, 0, keepdims=True)
                db = q * dq - k * dk + jnp.where(last, dbl, 0.0)
                df = _tri_mm(tri_t, db) / f - dk
                dhf_ref[sl, ls] = df * (1.0 - lb) * sig * (1.0 - sig)
                acc_lb[:, ls] += jnp.sum(df * (1.0 - sig), 0, keepdims=True)
                dhq_ref[sl, ls] = dq * _dsilu(hq, sq)
                dst_ref[h] = dstn * ebl + _dot_tn(do, q * eb)
            return carry

        lax.fori_loop(0, ncb, chunk, 0, unroll=2)

        @pl.when(t == nb - 1)
        def _():
            dl0 = acc_lb[...] * lb_all * (1.0 - lb_all)
            dlbl_ref[0:1, :] = dl0
            dlbl_ref[1:2, :] = -dl0
            dghn_ref[...] = acc_g[...]

    col_blk = lambda j: pl.BlockSpec((tb, W), lambda p, t: (nb - 1 - t, j * (G_HEADS // hp) + p))
    head = pl.BlockSpec((tb, W), lambda p, t: (nb - 1 - t, p))
    two = pl.BlockSpec((2, W), lambda p, t: (0, p))
    one = pl.BlockSpec((1, W), lambda p, t: (0, p))
    res = _pcall(
        body, "hgrn_bwd", (G_HEADS // hp, nb),
        [col_blk(0), col_blk(1), col_blk(2), col_blk(3), head,
         pl.BlockSpec((ncb, hp, G_DIM, G_DIM), lambda p, t: (nb - 1 - t, p, 0, 0)),
         pl.BlockSpec((ncb, hp, CHUNK, CHUNK), lambda p, t: (nb - 1 - t, p, 0, 0)), head, head, two, one],
        [head, head, head, head, two, one],
        [_sds((T, G_W))] * 4 + [_sds((2, G_W)), _sds((1, G_W))],
        scratch=[pltpu.VMEM((hp, G_DIM, G_DIM), F32), pltpu.VMEM((hp, CHUNK, G_DIM), F32),
                 pltpu.VMEM((1, W), F32), pltpu.VMEM((1, W), F32)], exchange=exchange,
    )(xp, xp, xp, xp, o_raw, s_all, a_all, b_all, d_out, lb_logits, g_hn)
    return res


def _in_bwd_x(x, dxp_m, dxp_h, dh1, w_in_al, g_pre, tm, exchange=None):
    T = x.shape[0]

    def body(x_ref, dm_ref, d0_ref, d1_ref, d2_ref, d3_ref, dh1_ref, w_ref, g_ref, dx_ref, dg_ref):
        i = pl.program_id(0)
        du = _dot_nt(dm_ref[...], w_ref[:, 0:MLA_IN])
        for j, d_ref in enumerate((d0_ref, d1_ref, d2_ref, d3_ref)):
            du = du + _dot_nt(d_ref[...], w_ref[:, MLA_IN + j * G_W:MLA_IN + (j + 1) * G_W])
        nx, r = _rms(x_ref[...], D)
        dx, dg = _rms_bwd(nx, r, g_ref[...], du, D)
        dx_ref[...] = dh1_ref[...] + dx

        @pl.when(i == 0)
        def _():
            dg_ref[...] = dg

        @pl.when(i > 0)
        def _():
            dg_ref[...] += dg

    return _pcall(body, "in_bwd_x", (T // tm,),
                  [_rows(tm, D), _rows(tm, MLA_IN)] + [_rows(tm, G_W)] * 4 + [_rows(tm, D), _full((D, XP_W)), _full((1, D))],
                  [_rows(tm, D), _full((1, D))], [_sds((T, D)), _sds((1, D))], exchange=exchange,
                  )(x, dxp_m, *dxp_h, dh1, w_in_al, g_pre)


def _aligned_col(c):
    return jnp.where(c < Q_RANK + KV_RANK + ROPE, c, c + (KR_PAD - ROPE))


def _align_w_in(g_in):
    tile = 384
    kr_end = Q_RANK + KV_RANK + ROPE

    def body(g_ref, o_ref, gp):
        gp[...] = jnp.zeros(gp.shape, BF)
        for j in range(N_DEV):
            gp[j, :, 0:IN_SH] = g_ref[j]
        r = lax.broadcasted_iota(jnp.int32, (tile, tile), 0)
        c = lax.broadcasted_iota(jnp.int32, (tile, tile), 1)
        for t in range(XP_W // tile):
            lo, hi = t * tile, (t + 1) * tile
            cols = [a if a < kr_end else a - (KR_PAD - ROPE) for a in (lo, hi - 1)]
            acc = jnp.zeros((D, tile), F32)
            for j in range(cols[0] // IN_SH, cols[-1] // IN_SH + 1):
                sel = (r < IN_SH) & (_aligned_col(j * IN_SH + r) == lo + c)
                acc = acc + jnp.dot(gp[j], sel.astype(BF), preferred_element_type=F32)
            o_ref[:, lo:hi] = acc.astype(BF)

    vm = pl.BlockSpec(memory_space=pltpu.VMEM)
    return pl.pallas_call(
        body, name="align_w_in", in_specs=[vm], out_specs=vm, out_shape=_sds((D, XP_W), BF),
        scratch_shapes=[pltpu.VMEM((N_DEV, D, tile), BF)],
        compiler_params=pltpu.CompilerParams(vmem_limit_bytes=_VMEM_LIMIT))(g_in)


def _in_bwd_w(name, u, dxp_m, dxp_h, tm, half, exchange=None):
    T = u.shape[0]
    nt = T // tm
    nr = D // 2
    win = 640

    def body(u_ref, dm_ref, d0_ref, d1_ref, d2_ref, d3_ref, o_ref, acc):
        i = pl.program_id(0)
        ut = u_ref[...].T
        parts = [(0, MLA_IN, dm_ref)] + [(MLA_IN + j * G_W, G_W, d) for j, d in enumerate((d0_ref, d1_ref, d2_ref, d3_ref))]

        @pl.when(i == 0)
        def _():
            for lo, n, d in parts:
                acc[:, lo:lo + n] = jnp.dot(ut, d[...].astype(BF), preferred_element_type=F32)

        @pl.when(i > 0)
        def _():
            for lo, n, d in parts:
                acc[:, lo:lo + n] += jnp.dot(ut, d[...].astype(BF), preferred_element_type=F32)

        @pl.when(i == nt - 1)
        def _():
            wide = 384
            r = lax.broadcasted_iota(jnp.int32, (win, wide), 0)
            c = lax.broadcasted_iota(jnp.int32, (win, wide), 1)
            kr_end = Q_RANK + KV_RANK + ROPE
            for j in range(N_DEV):
                first = j * IN_SH if j * IN_SH < kr_end else j * IN_SH + (KR_PAD - ROPE)
                lo = min(first // HB * HB, XP_W - win)
                sel = (c < IN_SH) & (_aligned_col(j * IN_SH + c) == lo + r)
                res = jnp.dot(acc[:, lo:lo + win].astype(BF), sel.astype(BF), preferred_element_type=F32)
                o_ref[j] = res[:, 0:IN_SH].astype(BF)

    return _pcall(body, name, (nt,),
                  [pl.BlockSpec((tm, nr), lambda i: (i, half)), _rows(tm, MLA_IN)] + [_rows(tm, G_W)] * 4,
                  [_full((N_DEV, nr, IN_SH))], [_sds((N_DEV, nr, IN_SH), BF)],
                  scratch=[pltpu.VMEM((nr, XP_W), F32)], exchange=exchange)(u, dxp_m, *dxp_h)


def _pad_heads(w, width, real):
    lead = w.shape[:-1]
    w = w.reshape(lead + (N_HEADS, real))
    w = jnp.pad(w, [(0, 0)] * len(lead) + [(0, 0), (0, width - real)])
    return w.reshape(lead + (N_HEADS * width,))


def _unpad_heads(w, width, real):
    lead = w.shape[:-1]
    return w.reshape(lead + (N_HEADS, width))[..., :real].reshape(lead + (N_HEADS * real,))


def _rope_tables(positions):
    half = ROPE // 2
    inv_freq = 1.0 / (ROPE_THETA ** (jnp.arange(0, ROPE, 2, dtype=F32) / ROPE))
    ang = positions.astype(F32)[:, None] * inv_freq
    cos, sin = jnp.cos(ang), jnp.sin(ang)
    T = positions.shape[0]
    z = lambda n: jnp.zeros((T, n), F32)
    ta = jnp.concatenate([jnp.ones((T, NOPE), F32), cos, cos, z(HB - QK_DIM)], 1)
    tb1 = jnp.concatenate([z(NOPE), -sin, z(half), z(HB - QK_DIM)], 1)
    tb2 = jnp.concatenate([z(NOPE), z(half), sin, z(HB - QK_DIM)], 1)
    return ta, tb1, tb2


def kernel(x, positions, attn_pre_norm, w_in, mla_q_norm, mla_w_uq, mla_kv_norm, mla_w_ukv, mla_out_norm, hgrn_lb_logits, hgrn_out_norm, w_out, attn_post_norm, ffn_pre_norm, w_gate, w_up, w_down, ffn_post_norm, loss_target, m_attn_pre_norm, m_w_in, m_mla_q_norm, m_mla_w_uq, m_mla_kv_norm, m_mla_w_ukv, m_mla_out_norm, m_hgrn_lb_logits, m_hgrn_out_norm, m_w_out, m_attn_post_norm, m_ffn_pre_norm, m_w_gate, m_w_up, m_w_down, m_ffn_post_norm, v_attn_pre_norm, v_w_in, v_mla_q_norm, v_mla_w_uq, v_mla_kv_norm, v_mla_w_ukv, v_mla_out_norm, v_hgrn_lb_logits, v_hgrn_out_norm, v_w_out, v_attn_post_norm, v_ffn_pre_norm, v_w_gate, v_w_up, v_w_down, v_ffn_post_norm):
    T = x.shape[1]
    tm = min(_TM, T)
    tq = min(_TQ, T)
    xs, tgt = x[0], loss_target[0]
    uq_sh = (Q_RANK // N_DEV, N_HEADS * QK_DIM)

    b_in, b_uq, b_out, b_gu, b_d = _cast_shards(
        w_in[0], mla_w_uq[0].reshape(uq_sh), w_out[0], w_gate[0].T, w_up[0].T, w_down[0])
    g_in, g_uq = _gather_two_level("ag_first", [b_in, b_uq])
    w_in_al = _align_w_in(g_in)
    w_uq_p = _pad_heads(g_uq.reshape(Q_RANK, N_HEADS * QK_DIM), HB, QK_DIM)
    w_ukv = mla_w_ukv[0].astype(BF)
    w_uk_p = _pad_heads(w_ukv[..., :NOPE].reshape(KV_RANK, N_HEADS * NOPE), HB, NOPE)
    w_uv_p = _pad_heads(w_ukv[..., NOPE:].reshape(KV_RANK, N_HEADS * V_DIM), HB, V_DIM)
    g_on_p = _pad_heads(mla_out_norm, HB, V_DIM)
    tabs = _rope_tables(positions[0])

    xp_m, xp_h, u = _fwd_in(xs, attn_pre_norm, w_in_al, tm)
    q_att, qs_att, k_att, v_att = _mla_prep(xp_m, tabs, mla_q_norm, mla_kv_norm, w_uq_p, w_uk_p, w_uv_p, tm)
    o_hgrn, o_raw, s_all, a_all, b_all, wd = _hgrn_fwd(xp_h, hgrn_lb_logits, hgrn_out_norm, ([GATHER], [b_d]))
    wd = wd.reshape(N_DEV * FF_PAD, D)
    o_pad, lse, wgu, g_out = _flash_fwd(qs_att, k_att, v_att, tq, ([GATHER, GATHER], [b_gu, b_out]))
    wgu = wgu.reshape(N_DEV * 2 * FF_PAD, D)
    w_out_full = g_out.reshape(D, D)
    w_out_mla = jnp.pad(w_out_full[:N_HEADS * V_DIM].reshape(N_HEADS, V_DIM, D), ((0, 0), (0, HB - V_DIM), (0, 0)))
    w_out_p = jnp.concatenate([w_out_mla.reshape(N_HEADS * HB, D), w_out_full[N_HEADS * V_DIM:]], 0)
    h1, y1, z, mix = _fwd_out(o_pad, o_hgrn, xs, g_on_p, w_out_p, attn_post_norm, ffn_pre_norm, tm)
    tmf = min(_TMF, T)
    gs, us, ffs, dh2, dy2, d_fpost, loss_row = _ffn_fwd(z, wgu, wd, h1, tgt, ffn_post_norm, tm, _FB // FF_PAD)

    dgu, dh1, dy1, d_fpre, d_post = _ffn_bwd_x(dy2, gs, us, wgu, wd, h1, y1, dh2, ffn_pre_norm, attn_post_norm, tm)
    dwgu, dwd = _ffn_bwd_w(z, ffs, dgu, dy2, tmf)
    do_pad, dl, d_ohg, dw_out_p, d_on_p = _out_bwd(dy1, mix, o_pad, w_out_p, g_on_p, tm)
    dw_out_mla = dw_out_p[:N_HEADS * HB].reshape(N_HEADS, HB, D)[:, :V_DIM].reshape(N_HEADS * V_DIM, D)
    dw_out = jnp.concatenate([dw_out_mla, dw_out_p[N_HEADS * HB:]], 0).reshape(N_DEV, D // N_DEV, D).astype(BF)
    dk_att, dv_att, dq_att, p_gu, p_d, p_out = _flash_bwd(
        q_att, k_att, v_att, do_pad, lse, dl, tq,
        ([SCATTER] * 3, [dwgu.reshape(N_DEV, 2 * FF_PAD, D), dwd.reshape(N_DEV, FF_PAD, D), dw_out]))
    dxp_m, dw_uq_p, dw_uk_p, dw_uv_p, d_gq, d_gkv = _mla_prep_bwd(
        xp_m, tabs, dq_att, dk_att, dv_att, mla_q_norm, mla_kv_norm, w_uq_p, w_uk_p, w_uv_p, tm)
    dw_uq = _unpad_heads(dw_uq_p, HB, QK_DIM).reshape((N_DEV,) + uq_sh).astype(BF)
    dw_ukv = jnp.concatenate([_unpad_heads(dw_uk_p, HB, NOPE).reshape(KV_RANK, N_HEADS, NOPE),
                              _unpad_heads(dw_uv_p, HB, V_DIM).reshape(KV_RANK, N_HEADS, V_DIM)], -1)
    *dxp_h, d_lbl, d_ghn, p_uq, dw_ukv_all = _hgrn_bwd(
        xp_h, o_raw, s_all, a_all, b_all, d_ohg, hgrn_lb_logits, hgrn_out_norm,
        ([SCATTER, GATHER], [dw_uq, dw_ukv.reshape(KV_RANK, N_HEADS * HB)]))
    dw_in_a, = _in_bwd_w("in_bwd_w_a", u, dxp_m, dxp_h, tm, 0)
    dw_in_b, p_in_a = _in_bwd_w("in_bwd_w_b", u, dxp_m, dxp_h, tm, 1, ([SCATTER], [dw_in_a]))
    grad_x, d_pre, p_in_b = _in_bwd_x(xs, dxp_m, dxp_h, dh1, w_in_al, attn_pre_norm, tm, ([SCATTER], [dw_in_b]))
    d_on = _unpad_heads(d_on_p, HB, V_DIM)

    ukv2 = lambda a: a.reshape(KV_RANK, N_HEADS * HB)
    vecs = [d_pre, d_gq, d_gkv, d_on, d_lbl, d_ghn, d_post, d_fpre, d_fpost, loss_row]
    small_w = [attn_pre_norm, mla_q_norm, mla_kv_norm, ukv2(mla_w_ukv), mla_out_norm, hgrn_lb_logits, hgrn_out_norm,
               attn_post_norm, ffn_pre_norm, ffn_post_norm]
    small_m = [m_attn_pre_norm, m_mla_q_norm, m_mla_kv_norm, ukv2(m_mla_w_ukv), m_mla_out_norm, m_hgrn_lb_logits,
               m_hgrn_out_norm, m_attn_post_norm, m_ffn_pre_norm, m_ffn_post_norm]
    small_v = [v_attn_pre_norm, v_mla_q_norm, v_mla_kv_norm, ukv2(v_mla_w_ukv), v_mla_out_norm, v_hgrn_lb_logits,
               v_hgrn_out_norm, v_attn_post_norm, v_ffn_pre_norm, v_ffn_post_norm]
    rall = _final_exchange(vecs)
    s_g, s_d, s_m, s_v, loss_all = _small_adam(rall, dw_ukv_all, 3, small_w, small_m, small_v)
    r_in = _shard_adam("adam_w_in", [p_in_a, p_in_b], w_in[0], m_w_in[0], v_w_in[0], 256)
    r_uq = _shard_adam("adam_w_uq", [p_uq], mla_w_uq[0].reshape(uq_sh), m_mla_w_uq[0].reshape(uq_sh),
                       v_mla_w_uq[0].reshape(uq_sh), uq_sh[0])
    r_out = _shard_adam("adam_w_out", [p_out], w_out[0], m_w_out[0], v_w_out[0], D // N_DEV)
    r_g, r_u = _gate_up_adam(p_gu, (w_gate[0].T, w_up[0].T), (m_w_gate[0].T, m_w_up[0].T),
                             (v_w_gate[0].T, v_w_up[0].T))
    r_g, r_u = [a.T for a in r_g], [a.T for a in r_u]
    r_d = _shard_adam("adam_w_down", [p_d], w_down[0], m_w_down[0], v_w_down[0], FF_SH // 2)

    loss = loss_all[0, 0]

    def assemble(big, small):
        b_in, b_uq, b_out, b_g, b_u, b_d = big
        return [small[0], b_in[None], small[1], b_uq.reshape(mla_w_uq.shape), small[2],
                small[3].reshape(mla_w_ukv.shape), small[4], small[5], small[6], b_out[None], small[7], small[8],
                b_g[None], b_u[None], b_d[None], small[9]]

    outs = [loss, grad_x[None]]
    for idx, small in enumerate((s_g, s_d, s_m, s_v)):
        outs += assemble([r[idx] for r in (r_in, r_uq, r_out, r_g, r_u, r_d)], small)
    return tuple(outs)
```

```python
import jax
import jax.numpy as jnp
from jax import lax
from jax.experimental import pallas as pl
from jax.experimental.pallas import tpu as pltpu

BF = jnp.bfloat16
F32 = jnp.float32
MESH = pl.DeviceIdType.MESH

N_DEV = 8
D = 1024
EPS = 1e-6
LOG2E = 1.4426950408889634
ROPE_THETA = 10000.0
N_HEADS = 8
HB = 128
NOPE = 64
ROPE = 32
V_DIM = 64
QK_DIM = NOPE + ROPE
Q_RANK = 384
KV_RANK = 128
KR_PAD = 128
MLA_IN = Q_RANK + KV_RANK + KR_PAD
G_HEADS = 4
G_DIM = 128
G_W = G_HEADS * G_DIM
CHUNK = 64
SUB = 16
XP_W = MLA_IN + 4 * G_W
IN_SH = 324
FF_SH = 352
FF_PAD = 384
MIX_W = N_HEADS * HB + G_W

ADAM_LR = 0.001
ADAM_B1 = 0.9
ADAM_B2 = 0.999
ADAM_EPS = 1e-08
ADAM_WD = 0.01
ADAM_STEP = 10

_TM = 512
_TMF = 1024
_TQ = 512
_AH = 2
_AH_FWD = 4
_FB = 768
_TB = 1024
_TB_BWD = 512
_HP = 4
V7X_VMEM_BYTES = 64 * 1024 * 1024
_VMEM_LIMIT = V7X_VMEM_BYTES - 8 * 1024 * 1024
NEG = -1e30


def _dot(a, b):
    return jnp.dot(a.astype(BF), b.astype(BF), preferred_element_type=F32)


def _dot_nt(a, b):
    return lax.dot_general(a.astype(BF), b.astype(BF), (((1,), (1,)), ((), ())), preferred_element_type=F32)


def _dot_tn(a, b):
    return lax.dot_general(a.astype(BF), b.astype(BF), (((0,), (0,)), ((), ())), preferred_element_type=F32)


def _sigmoid(x):
    return 1.0 / (1.0 + jnp.exp(-x))


def _rms(x, n):
    r = lax.rsqrt(jnp.sum(x * x, -1, keepdims=True) * (1.0 / n) + EPS)
    return x * r, r


def _rms_bwd(nx, r, g, dy, n):
    dg = jnp.sum(dy * nx, 0, keepdims=True)
    dn = dy * g
    dx = r * (dn - nx * (jnp.sum(dn * nx, -1, keepdims=True) * (1.0 / n)))
    return dx, dg


def _adamw(w, g, m, v):
    m2 = ADAM_B1 * m + (1.0 - ADAM_B1) * g
    v2 = ADAM_B2 * v + (1.0 - ADAM_B2) * (g * g)
    m_hat = m2 / (1.0 - ADAM_B1 ** ADAM_STEP)
    v_hat = v2 / (1.0 - ADAM_B2 ** ADAM_STEP)
    delta = -ADAM_LR * (m_hat / (jnp.sqrt(v_hat) + ADAM_EPS) + ADAM_WD * w)
    return delta, m2, v2


def _pcall(body, name, grid, in_specs, out_specs, out_shape, scratch=(), exchange=None):
    scratch = list(scratch)
    extra = ()
    if exchange is not None:
        kinds, extra = exchange
        in_specs, out_specs, out_shape = list(in_specs), list(out_specs), list(out_shape)
        n_in, n_out, n_scr, n_x = len(in_specs), len(out_specs), len(scratch), len(extra)
        inner = body

        def body(*refs):
            ins, rest = refs[:n_in], refs[n_in:]
            x_src, rest = rest[:n_x], rest[n_x:]
            outs, rest = rest[:n_out], rest[n_out:]
            x_dst, rest = rest[:n_x], rest[n_x:]
            if all(kind == GATHER for kind in kinds):
                ex = _TwoLevelGather(x_src, x_dst, *rest[n_scr:])
            else:
                ex = _Exchange(kinds, x_src, x_dst, *rest[n_scr:])
            first = pl.program_id(0) == 0
            last = pl.program_id(0) == grid[0] - 1
            for a in range(1, len(grid)):
                first = first & (pl.program_id(a) == 0)
                last = last & (pl.program_id(a) == grid[a] - 1)
            pl.when(first)(ex.start)
            inner(*ins, *outs, *rest[:n_scr])
            pl.when(last)(ex.wait)

        in_specs += [_HBM] * n_x
        out_specs += [_HBM] * n_x
        out_shape += _exchange_shapes(kinds, extra)
        scratch += _exchange_sems(n_x)
    call = pl.pallas_call(
        body, name=name, grid=grid, in_specs=in_specs, out_specs=out_specs, out_shape=out_shape,
        scratch_shapes=scratch,
        compiler_params=pltpu.CompilerParams(
            dimension_semantics=("arbitrary",) * len(grid), vmem_limit_bytes=_VMEM_LIMIT))
    return lambda *operands: call(*operands, *extra)


def _full(shape):
    return pl.BlockSpec(shape, lambda *_: (0,) * len(shape))


def _rows(tm, n):
    return pl.BlockSpec((tm, n), lambda i, *_: (i, 0))


def _sds(shape, dtype=F32):
    return jax.ShapeDtypeStruct(shape, dtype)


def _peer(k, x, y, c):
    px = 1 - x if (k >> 2) & 1 else x
    py = 1 - y if (k >> 1) & 1 else y
    pc = 1 - c if k & 1 else c
    return px, py, pc


GATHER, SCATTER = "gather", "scatter"


class _Exchange:
    def __init__(self, kinds, srcs, dsts, send_sems, recv_sems, loc_sems):
        self.kinds, self.srcs, self.dsts = kinds, srcs, dsts
        self.send_sems, self.recv_sems, self.loc_sems = send_sems, recv_sems, loc_sems
        self.x, self.y, self.c = lax.axis_index("x"), lax.axis_index("y"), lax.axis_index("c")
        self.me = 4 * self.x + 2 * self.y + self.c

    def _src(self, w, slot):
        return self.srcs[w] if self.kinds[w] == GATHER else self.srcs[w].at[slot]

    def _dst(self, w, slot):
        return self.dsts[w].at[slot]

    def _copy(self, w, k, outgoing):
        px, py, pc = _peer(k, self.x, self.y, self.c)
        pid = 4 * px + 2 * py + pc
        return pltpu.make_async_remote_copy(
            src_ref=self._src(w, pid if outgoing else self.me),
            dst_ref=self._dst(w, self.me if outgoing else pid),
            send_sem=self.send_sems.at[w, k - 1], recv_sem=self.recv_sems.at[w, k - 1],
            device_id=(px, py, pc), device_id_type=MESH)

    def _local(self, w):
        return pltpu.make_async_copy(self._src(w, self.me), self._dst(w, self.me), self.loc_sems.at[w])

    def start(self):
        for w in range(len(self.srcs)):
            self._local(w).start()
            for k in range(1, N_DEV):
                self._copy(w, k, True).start()

    def wait(self):
        for w in range(len(self.srcs)):
            self._local(w).wait()
            for k in range(1, N_DEV):
                self._copy(w, k, False).wait_recv()
        for w in range(len(self.srcs)):
            for k in range(1, N_DEV):
                self._copy(w, k, True).wait_send()


def _exchange_sems(n_w):
    return [pltpu.SemaphoreType.DMA((n_w, N_DEV - 1)), pltpu.SemaphoreType.DMA((n_w, N_DEV - 1)),
            pltpu.SemaphoreType.DMA((n_w,))]


def _exchange_shapes(kinds, srcs):
    return [_sds(((N_DEV,) if kd == GATHER else ()) + tuple(s.shape), s.dtype) for kd, s in zip(kinds, srcs)]


_HBM = pl.BlockSpec(memory_space=pl.ANY)


def _cast_shards(w_in, w_uq, w_out, w_gate_t, w_up_t, w_down):
    shapes = [(D, IN_SH), (Q_RANK // N_DEV, N_HEADS * QK_DIM), (D // N_DEV, D), (2 * FF_PAD, D), (FF_PAD, D)]

    def body(win, wuq, wout, wg, wu, wd, sin_, suq, sout, sgu, sd):
        sin_[...] = win[...].astype(BF)
        suq[...] = wuq[...].astype(BF)
        sout[...] = wout[...].astype(BF)
        sgu[...] = jnp.zeros(sgu.shape, BF)
        sgu[0:FF_SH, :] = wg[...].astype(BF)
        sgu[FF_PAD:FF_PAD + FF_SH, :] = wu[...].astype(BF)
        sd[...] = jnp.zeros(sd.shape, BF)
        sd[0:FF_SH, :] = wd[...].astype(BF)

    vm = pl.BlockSpec(memory_space=pltpu.VMEM)
    return pl.pallas_call(
        body, name="cast_shards", in_specs=[vm] * 6, out_specs=[vm] * 5,
        out_shape=[_sds(s, BF) for s in shapes],
        compiler_params=pltpu.CompilerParams(vmem_limit_bytes=_VMEM_LIMIT),
    )(w_in, w_uq, w_out, w_gate_t, w_up_t, w_down)


class _TwoLevelGather:
    def __init__(self, srcs, dsts, send_sems, recv_sems, loc_sems):
        self.srcs, self.dsts = srcs, dsts
        self.send_sems, self.recv_sems, self.loc_sems = send_sems, recv_sems, loc_sems
        x, y, c = lax.axis_index("x"), lax.axis_index("y"), lax.axis_index("c")
        self.c = c
        self.me, self.sibling = (x, y, c), (x, y, 1 - c)
        self.chips = [(1 - x, y), (x, 1 - y), (1 - x, 1 - y)]

    def _copy(self, w, k, block, to, own=False):
        slot = 4 * block[0] + 2 * block[1] + block[2]
        return pltpu.make_async_remote_copy(
            src_ref=self.srcs[w] if own else self.dsts[w].at[slot], dst_ref=self.dsts[w].at[slot],
            send_sem=self.send_sems.at[w, k], recv_sem=self.recv_sems.at[w, k], device_id=to, device_id_type=MESH)

    def _local(self, w):
        slot = 4 * self.me[0] + 2 * self.me[1] + self.me[2]
        return pltpu.make_async_copy(self.srcs[w], self.dsts[w].at[slot], self.loc_sems.at[w])

    def _first(self, w):
        return [self._copy(w, 0, self.me, self.sibling, own=True)] + [
            self._copy(w, 1 + j, self.me, (*chip, self.c), own=True) for j, chip in enumerate(self.chips)]

    def _passed(self, w):
        return [self._copy(w, 4 + j, (*chip, self.c), self.sibling) for j, chip in enumerate(self.chips)]

    def start(self):
        for w in range(len(self.srcs)):
            self._local(w).start()
            for cp in self._first(w):
                cp.start()

    def wait(self):
        n_w = len(self.srcs)
        for w in range(n_w):
            for j, chip in enumerate(self.chips):
                self._copy(w, 1 + j, (*chip, self.c), self.me).wait_recv()
                self._passed(w)[j].start()
        for w in range(n_w):
            self._copy(w, 0, self.sibling, self.me).wait_recv()
            for j, chip in enumerate(self.chips):
                self._copy(w, 4 + j, (*chip, 1 - self.c), self.me).wait_recv()
        for w in range(n_w):
            for cp in self._first(w) + self._passed(w):
                cp.wait_send()
            self._local(w).wait()


def _gather_two_level(name, srcs):
    n_w = len(srcs)

    def body(*refs):
        ex = _TwoLevelGather(refs[:n_w], refs[n_w:2 * n_w], *refs[2 * n_w:])
        ex.start()
        ex.wait()

    return pl.pallas_call(
        body, name=name, in_specs=[_HBM] * n_w, out_specs=[_HBM] * n_w,
        out_shape=_exchange_shapes([GATHER] * n_w, srcs), scratch_shapes=_exchange_sems(n_w))(*srcs)


def _row_offsets(arrays):
    offs, rows = [], 0
    for a in arrays:
        offs.append(rows)
        rows += a.shape[0]
    return offs, -(-rows // 8) * 8


def _final_exchange(vecs):
    n_p = len(vecs)
    offs, rows = _row_offsets(vecs)

    def body(*refs):
        g_refs = refs[:n_p]
        rall, pk, send_sems, recv_sems, loc_sem = refs[n_p:]
        x, y, c = lax.axis_index("x"), lax.axis_index("y"), lax.axis_index("c")
        me = 4 * x + 2 * y + c
        pk[...] = jnp.zeros(pk.shape, F32)
        for p in range(n_p):
            r, n = g_refs[p].shape
            pk[offs[p]:offs[p] + r, 0:n] = g_refs[p][...]

        def remote(k):
            return pltpu.make_async_remote_copy(
                src_ref=pk, dst_ref=rall.at[me], send_sem=send_sems.at[k - 1], recv_sem=recv_sems.at[k - 1],
                device_id=_peer(k, x, y, c), device_id_type=MESH)

        def arrival(k):
            px, py, pc = _peer(k, x, y, c)
            return pltpu.make_async_remote_copy(
                src_ref=pk, dst_ref=rall.at[4 * px + 2 * py + pc], send_sem=send_sems.at[k - 1],
                recv_sem=recv_sems.at[k - 1], device_id=(px, py, pc), device_id_type=MESH)

        local = pltpu.make_async_copy(pk, rall.at[me], loc_sem)
        local.start()
        for k in range(1, N_DEV):
            remote(k).start()
        local.wait()
        for k in range(1, N_DEV):
            arrival(k).wait_recv()
        for k in range(1, N_DEV):
            remote(k).wait_send()

    vm = pl.BlockSpec(memory_space=pltpu.VMEM)
    return pl.pallas_call(
        body, name="final_exchange", in_specs=[vm] * n_p, out_specs=vm, out_shape=_sds((N_DEV, rows, D)),
        scratch_shapes=[pltpu.VMEM((rows, D), F32),
                        pltpu.SemaphoreType.DMA((N_DEV - 1,)), pltpu.SemaphoreType.DMA((N_DEV - 1,)),
                        pltpu.SemaphoreType.DMA],
    )(*vecs)


def _small_adam(rall, big_parts, big, ws, ms, vs):
    n_p = len(ws)
    packed = [w for p, w in enumerate(ws) if p != big] + [jax.ShapeDtypeStruct((1, HB), F32)]
    offs, _ = _row_offsets(packed)
    offs = offs[:big] + [None] + offs[big:]

    def total(ref, sl):
        g = ref[(0,) + sl]
        for j in range(1, N_DEV):
            g = g + ref[(j,) + sl]
        return g

    def body(*refs):
        rall_ref, big_ref = refs[:2]
        w_refs, m_refs, v_refs = refs[2:2 + n_p], refs[2 + n_p:2 + 2 * n_p], refs[2 + 2 * n_p:2 + 3 * n_p]
        outs = refs[2 + 3 * n_p:]
        for p in range(n_p):
            r, n = w_refs[p].shape
            if p == big:
                g = total(big_ref, (slice(0, r), slice(0, n)))
            else:
                g = total(rall_ref, (slice(offs[p], offs[p] + r), slice(0, n)))
            delta, m2, v2 = _adamw(w_refs[p][...], g, m_refs[p][...], v_refs[p][...])
            outs[p][...] = g
            outs[n_p + p][...] = delta
            outs[2 * n_p + p][...] = m2
            outs[3 * n_p + p][...] = v2
        outs[4 * n_p][...] = total(rall_ref, (slice(offs[n_p], offs[n_p] + 1), slice(0, HB)))

    vm = pl.BlockSpec(memory_space=pltpu.VMEM)
    res = pl.pallas_call(
        body, name="small_adam", in_specs=[vm] * (2 + 3 * n_p), out_specs=[vm] * (4 * n_p + 1),
        out_shape=[_sds(w.shape) for w in ws] * 4 + [_sds((1, HB))],
        compiler_params=pltpu.CompilerParams(vmem_limit_bytes=_VMEM_LIMIT),
    )(rall, big_parts, *ws, *ms, *vs)
    return res[:n_p], res[n_p:2 * n_p], res[2 * n_p:3 * n_p], res[3 * n_p:4 * n_p], res[4 * n_p]


def _device_sum(p_ref):
    g = p_ref[0].astype(F32)
    for j in range(1, N_DEV):
        g = g + p_ref[j].astype(F32)
    return g


def _shard_adam(name, parts, w, m, v, tr):
    a0, b0 = w.shape
    n_p = len(parts)
    b = parts[0].shape[2]
    first = [0]
    for p in parts:
        first.append(first[-1] + p.shape[1] // tr)

    def body(*refs):
        p_refs = refs[:n_p]
        w_ref, m_ref, v_ref, g_out, d_out, m_out, v_out = refs[n_p:]
        i = pl.program_id(0)
        g = _device_sum(p_refs[0])
        for k in range(1, n_p):
            g = jnp.where(i >= first[k], _device_sum(p_refs[k]), g)
        g = g[:, 0:b0]
        delta, m2, v2 = _adamw(w_ref[...], g, m_ref[...], v_ref[...])
        g_out[...] = g
        d_out[...] = delta
        m_out[...] = m2
        v_out[...] = v2

    def part_spec(k):
        last = first[k + 1] - first[k] - 1
        return pl.BlockSpec((N_DEV, tr, b), lambda i: (0, jnp.minimum(jnp.maximum(i - first[k], 0), last), 0))

    blk = pl.BlockSpec((tr, b0), lambda i: (i, 0))
    return _pcall(
        body, name, (a0 // tr,), [part_spec(k) for k in range(n_p)] + [blk, blk, blk],
        [blk] * 4, [_sds((a0, b0))] * 4)(*parts, w, m, v)


def _gate_up_adam(parts, ws, ms, vs):
    tc = 256

    def body(p_ref, wg, wu, mg, mu, vg, vu, *outs):
        g = _device_sum(p_ref)
        for k, (w_ref, m_ref, v_ref) in enumerate(((wg, mg, vg), (wu, mu, vu))):
            gk = g[k * FF_PAD:k * FF_PAD + FF_SH]
            delta, m2, v2 = _adamw(w_ref[...], gk, m_ref[...], v_ref[...])
            for o, val in zip(outs[4 * k:4 * k + 4], (gk, delta, m2, v2)):
                o[...] = val

    blk = pl.BlockSpec((FF_SH, tc), lambda i: (0, i))
    res = _pcall(
        body, "adam_w_gate_up", (D // tc,), [pl.BlockSpec((N_DEV, 2 * FF_PAD, tc), lambda i: (0, 0, i))] + [blk] * 6,
        [blk] * 8, [_sds((FF_SH, D))] * 8)(parts, *ws, *ms, *vs)
    return res[:4], res[4:]


def _fwd_in(x, g_pre, w_in_al, tm):
    T = x.shape[0]

    def body(x_ref, g_ref, w_ref, xm_ref, xh_ref, u_ref):
        nx, _ = _rms(x_ref[...], D)
        u = (nx * g_ref[...]).astype(BF)
        u_ref[...] = u
        xm_ref[...] = jnp.dot(u, w_ref[:, 0:MLA_IN], preferred_element_type=F32)
        xh_ref[...] = jnp.dot(u, w_ref[:, MLA_IN:XP_W], preferred_element_type=F32)

    return _pcall(body, "fwd_in", (T // tm,),
                  [_rows(tm, D), _full((1, D)), _full((D, XP_W))],
                  [_rows(tm, MLA_IN), _rows(tm, 4 * G_W), _rows(tm, D)],
                  [_sds((T, MLA_IN)), _sds((T, 4 * G_W)), _sds((T, D), BF)])(x, g_pre, w_in_al)


def _rope(blk, ta, tb1, tb2):
    return blk * ta + pltpu.roll(blk, HB - ROPE // 2, 1) * tb1 + pltpu.roll(blk, ROPE // 2, 1) * tb2


def _unrope(d, ta, tb1, tb2):
    return d * ta + pltpu.roll(d * tb1, ROPE // 2, 1) + pltpu.roll(d * tb2, HB - ROPE // 2, 1)


def _mla_prep(xp, tabs, g_q, g_kv, w_uq, w_uk, w_uv, tm):
    T = xp.shape[0]
    W = N_HEADS * HB

    def body(xp_ref, ta_ref, tb1_ref, tb2_ref, gq_ref, gkv_ref, wuq_ref, wuk_ref, wuv_ref, q_ref, qs_ref, k_ref, v_ref):
        ta, tb1, tb2 = ta_ref[...], tb1_ref[...], tb2_ref[...]
        nq, _ = _rms(xp_ref[:, 0:Q_RANK], Q_RANK)
        nkv, _ = _rms(xp_ref[:, Q_RANK:Q_RANK + KV_RANK], KV_RANK)
        nkv = (nkv * gkv_ref[...]).astype(BF)
        qpre = _dot(nq * gq_ref[...], wuq_ref[...])
        kpre = jnp.dot(nkv, wuk_ref[...], preferred_element_type=F32)
        v = jnp.dot(nkv, wuv_ref[...], preferred_element_type=F32)
        lane = lax.broadcasted_iota(jnp.int32, (tm, W), 1)
        v_ref[...] = jnp.where((lane & (HB - 1)) == V_DIM, 1.0, v).astype(BF)
        kr = _rope(pltpu.roll(xp_ref[:, Q_RANK + KV_RANK:MLA_IN], NOPE, 1), ta, tb1, tb2)
        for h in range(N_HEADS):
            sl = slice(h * HB, (h + 1) * HB)
            qr = _rope(qpre[:, sl], ta, tb1, tb2)
            q_ref[:, sl] = qr.astype(BF)
            qs_ref[:, sl] = (qr * (QK_DIM ** -0.5 * LOG2E)).astype(BF)
            k_ref[:, sl] = (kpre[:, sl] + kr).astype(BF)

    tab = _rows(tm, HB)
    return _pcall(body, "mla_prep", (T // tm,),
                  [_rows(tm, MLA_IN), tab, tab, tab, _full((1, Q_RANK)), _full((1, KV_RANK)),
                   _full((Q_RANK, W)), _full((KV_RANK, W)), _full((KV_RANK, W))],
                  [_rows(tm, W)] * 4, [_sds((T, W), BF)] * 4)(xp, *tabs, g_q, g_kv, w_uq, w_uk, w_uv)


def _flash_fwd(q, k, v, tq, exchange=None):
    T = q.shape[0]
    hp = _AH_FWD
    W = hp * HB

    def body(q_ref, k_ref, v_ref, o_ref, lse_ref):
        i = pl.program_id(1)

        def blk(j, carry, masked):
            st = pl.multiple_of(j * tq, tq)
            out = []
            for h in range(hp):
                ls = slice(h * HB, (h + 1) * HB)
                m, acc = carry[h]
                s = _dot_nt(q_ref[:, ls], k_ref[pl.ds(st, tq), ls])
                if masked:
                    r = lax.broadcasted_iota(jnp.int32, (tq, tq), 0)
                    c = lax.broadcasted_iota(jnp.int32, (tq, tq), 1)
                    s = jnp.where(c <= r, s, NEG)
                m2 = jnp.maximum(m, jnp.max(s, -1, keepdims=True))
                p = jnp.exp2(s - m2)
                out.append((m2, jnp.exp2(m - m2) * acc + _dot(p, v_ref[pl.ds(st, tq), ls])))
            return tuple(out)

        init = tuple((jnp.full((tq, 1), NEG, F32), jnp.zeros((tq, HB), F32)) for _ in range(hp))
        carry = lax.fori_loop(0, i, lambda j, cr: blk(j, cr, False), init)
        res = blk(i, carry, True)
        lane = lax.broadcasted_iota(jnp.int32, (tq, HB), 1)
        for h in range(hp):
            ls = slice(h * HB, (h + 1) * HB)
            m, acc = res[h]
            l = acc[:, V_DIM:V_DIM + 1]
            o_ref[:, ls] = jnp.where(lane < V_DIM, acc / l, 0.0).astype(BF)
            lse_ref[:, ls] = jnp.broadcast_to(m * (1.0 / LOG2E) + jnp.log(l), (tq, HB))

    qs = pl.BlockSpec((tq, W), lambda h, i: (i, h))
    kvs = pl.BlockSpec((T, W), lambda h, i: (0, h))
    return _pcall(body, "flash_fwd", (N_HEADS // hp, T // tq), [qs, kvs, kvs], [qs, qs],
                  [_sds((T, N_HEADS * HB), BF), _sds((T, N_HEADS * HB))], exchange=exchange)(q, k, v)


def _gates(hq, hf, lb):
    sig = _sigmoid(hf)
    f = lb + (1.0 - lb) * sig
    sq = _sigmoid(hq)
    return hq * sq, 1.0 - f, f, jnp.log(f), sig, sq


def _lower_bound(lbl_ref):
    l0, l1 = lbl_ref[0:1, :], lbl_ref[1:2, :]
    mx = jnp.maximum(l0, l1)
    e0, e1 = jnp.exp(l0 - mx), jnp.exp(l1 - mx)
    return e0 / (e0 + e1)


def _split3(x):
    hi = x.astype(BF)
    r1 = x - hi.astype(F32)
    mid = r1.astype(BF)
    lo = (r1 - mid.astype(F32)).astype(BF)
    return hi, mid, lo


def _tri_mm(tri, x):
    hi, mid, lo = _split3(x)
    mm = lambda t: jnp.dot(tri, t, preferred_element_type=F32)
    return mm(hi) + mm(mid) + mm(lo)


def _intra_codes(sub):
    row = lax.broadcasted_iota(jnp.int32, (CHUNK, CHUNK), 0)
    col = lax.broadcasted_iota(jnp.int32, (CHUNK, CHUNK), 1)
    return sub, row, col


def _intra(q, k, b2, b_s, codes, da=None):
    grad = da is not None
    pow2 = (lambda x: jnp.exp2(jnp.minimum(x, 0.0))) if grad else jnp.exp2
    sub, row, col = codes
    a = jnp.zeros((CHUNK, CHUNK), F32)
    dq = jnp.zeros((CHUNK, G_DIM), F32)
    dk = jnp.zeros((CHUNK, G_DIM), F32)
    for i in range(1, CHUNK // sub):
        b0 = b_s[sub * i - 1:sub * i, :]
        eq, ek = pow2(b2 - b0), pow2(b0 - b2)
        mask = ((row // sub) == i) & (col < sub * i)
        if grad:
            dai = jnp.where(mask, da, 0.0)
            dq = dq + _dot(dai, k * ek) * eq
            dk = dk + _dot_tn(dai, q * eq) * ek
        else:
            a = jnp.where(mask, _dot_nt(q * eq, k * ek), a)
    for d in range(sub):
        ksh = pltpu.roll(k, d, 0) if d else k
        bsh = pltpu.roll(b2, d, 0) if d else b2
        e = pow2(b2 - bsh)
        mask = (col == row - d) & ((row & (sub - 1)) >= d)
        if grad:
            g = jnp.sum(jnp.where(mask, da, 0.0), -1, keepdims=True) * e
            dq = dq + g * ksh
            cb = g * q
            dk = dk + (pltpu.roll(cb, CHUNK - d, 0) if d else cb)
        else:
            a = jnp.where(mask, jnp.sum(q * ksh * e, -1, keepdims=True), a)
    return (dq, dk) if grad else a


def _hgrn_fwd(xp, lb_logits, g_hn, exchange=None):
    T = xp.shape[0]
    tb = min(_TB, T)
    ncb = tb // CHUNK
    hp = _HP
    W = hp * G_DIM

    def body(hq_ref, hf_ref, hi_ref, hg_ref, lbl_ref, ghn_ref, out_ref, oraw_ref, sall_ref, aall_ref, ball_ref,
             st_ref, b_s):
        lb_all = _lower_bound(lbl_ref)

        @pl.when(pl.program_id(1) == 0)
        def _():
            st_ref[...] = jnp.zeros(st_ref.shape, F32)

        row = lax.broadcasted_iota(jnp.int32, (CHUNK, CHUNK), 0)
        col = lax.broadcasted_iota(jnp.int32, (CHUNK, CHUNK), 1)
        tri = (col <= row).astype(BF)
        codes = _intra_codes(SUB)

        def chunk(c, carry):
            sl = pl.ds(pl.multiple_of(c * CHUNK, CHUNK), CHUNK)
            for h in range(hp):
                ls = slice(h * G_DIM, (h + 1) * G_DIM)
                q, k, _, lf, _, _ = _gates(hq_ref[sl, ls], hf_ref[sl, ls], lb_all[:, ls])
                v = hi_ref[sl, ls]
                b2 = _tri_mm(tri, lf) * LOG2E
                b_s[h] = b2
                ball_ref[sl, ls] = b2
                st = st_ref[h]
                sall_ref[c, h] = st
                a = _intra(q, k, b2, b_s.at[h], codes)
                aall_ref[c, h] = a
                o = _dot_nt(q * jnp.exp2(b2), st) + _dot(a, v)
                bl = b_s[h, CHUNK - 1:CHUNK, :]
                st_ref[h] = st * jnp.exp2(bl) + _dot_tn(v, k * jnp.exp2(bl - b2))
                oraw_ref[sl, ls] = o
                n, _ = _rms(o, G_DIM)
                hg = hg_ref[sl, ls]
                out_ref[sl, ls] = n * ghn_ref[:, ls] * (hg * _sigmoid(hg))
            return carry

        lax.fori_loop(0, ncb, chunk, 0, unroll=4)

    col_blk = lambda j: pl.BlockSpec((tb, W), lambda p, t: (t, j * (G_HEADS // hp) + p))
    head = pl.BlockSpec((tb, W), lambda p, t: (t, p))
    return _pcall(
        body, "hgrn_fwd", (G_HEADS // hp, T // tb),
        [col_blk(0), col_blk(1), col_blk(2), col_blk(3),
         pl.BlockSpec((2, W), lambda p, t: (0, p)), pl.BlockSpec((1, W), lambda p, t: (0, p))],
        [head, head, pl.BlockSpec((ncb, hp, G_DIM, G_DIM), lambda p, t: (t, p, 0, 0)),
         pl.BlockSpec((ncb, hp, CHUNK, CHUNK), lambda p, t: (t, p, 0, 0)), head],
        [_sds((T, G_W)), _sds((T, G_W)), _sds((T // CHUNK, G_HEADS, G_DIM, G_DIM)),
         _sds((T // CHUNK, G_HEADS, CHUNK, CHUNK)), _sds((T, G_W))],
        scratch=[pltpu.VMEM((hp, G_DIM, G_DIM), F32), pltpu.VMEM((hp, CHUNK, G_DIM), F32)], exchange=exchange,
    )(xp, xp, xp, xp, lb_logits, g_hn)


def _fwd_out(o_pad, o_hgrn, x, g_on, w_out, g_post, g_fpre, tm):
    T = x.shape[0]

    def body(o_ref, oh_ref, x_ref, gon_ref, w_ref, gpost_ref, gfpre_ref, h1_ref, y1_ref, z_ref, mix_ref):
        for h in range(N_HEADS):
            sl = slice(h * HB, (h + 1) * HB)
            n, _ = _rms(o_ref[:, sl].astype(F32), V_DIM)
            mix_ref[:, sl] = (n * gon_ref[:, sl]).astype(BF)
        mix_ref[:, N_HEADS * HB:MIX_W] = oh_ref[...].astype(BF)
        y1 = jnp.dot(mix_ref[...], w_ref[...], preferred_element_type=F32)
        y1_ref[...] = y1
        ny, _ = _rms(y1, D)
        h1 = x_ref[...] + ny * gpost_ref[...]
        h1_ref[...] = h1
        nh, _ = _rms(h1, D)
        z_ref[...] = (nh * gfpre_ref[...]).astype(BF)

    return _pcall(body, "fwd_out", (T // tm,),
                  [_rows(tm, N_HEADS * HB), _rows(tm, G_W), _rows(tm, D), _full((1, N_HEADS * HB)),
                   _full((MIX_W, D)), _full((1, D)), _full((1, D))],
                  [_rows(tm, D), _rows(tm, D), _rows(tm, D), _rows(tm, MIX_W)],
                  [_sds((T, D)), _sds((T, D)), _sds((T, D), BF), _sds((T, MIX_W), BF)],
                  )(o_pad, o_hgrn, x, g_on, w_out, g_post, g_fpre)


def _ffn_fwd(z, wgu, wd, h1, tgt, g_fpost, tm, nd):
    T = z.shape[0]
    fb = nd * FF_PAD
    nf = wd.shape[0] // fb

    def body(z_ref, wgu_ref, wd_ref, h1_ref, t_ref, gp_ref,
             as_ref, bs_ref, ff_ref, dh2_ref, dy2_ref, dgp_ref, loss_ref, acc):
        i, j = pl.program_id(0), pl.program_id(1)
        gu = _dot_nt(z_ref[...], wgu_ref[...])
        piece = lambda n: gu[:, n * FF_PAD:(n + 1) * FF_PAD]
        g = piece(0) if nd == 1 else jnp.concatenate([piece(2 * n) for n in range(nd)], 1)
        u = piece(1) if nd == 1 else jnp.concatenate([piece(2 * n + 1) for n in range(nd)], 1)
        s = _sigmoid(g)
        b = g * s
        ff = (b * u).astype(BF)
        as_ref[...] = (u * _dsilu(g, s)).astype(BF)
        bs_ref[...] = b.astype(BF)
        ff_ref[...] = ff
        part = jnp.dot(ff, wd_ref[...], preferred_element_type=F32)

        @pl.when(j == 0)
        def _():
            acc[...] = part

        @pl.when(j > 0)
        def _():
            acc[...] += part

        @pl.when((i == 0) & (j == 0))
        def _():
            dgp_ref[...] = jnp.zeros(dgp_ref.shape, F32)
            loss_ref[...] = jnp.zeros(loss_ref.shape, F32)

        @pl.when(j == nf - 1)
        def _():
            ny, r = _rms(acc[...], D)
            err = h1_ref[...] + ny * gp_ref[...] - t_ref[...]
            loss_ref[...] += 0.5 * jnp.sum(jnp.sum(err * err, -1, keepdims=True) * (1.0 / D), 0, keepdims=True)
            dh2 = err * (1.0 / D)
            dh2_ref[...] = dh2
            dy2, dgp = _rms_bwd(ny, r, gp_ref[...], dh2, D)
            dy2_ref[...] = dy2.astype(BF)
            dgp_ref[...] += dgp

    tok = lambda n: pl.BlockSpec((tm, n), lambda i, j: (i, 0))
    col = pl.BlockSpec((tm, fb), lambda i, j: (i, j))
    return _pcall(
        body, "ffn_fwd", (T // tm, nf),
        [tok(D), pl.BlockSpec((2 * fb, D), lambda i, j: (j, 0)), pl.BlockSpec((fb, D), lambda i, j: (j, 0)),
         tok(D), tok(D), _full((1, D))],
        [col, col, col, tok(D), tok(D), _full((1, D)), _full((1, HB))],
        [_sds((T, nf * fb), BF)] * 3 + [_sds((T, D)), _sds((T, D), BF), _sds((1, D)), _sds((1, HB))],
        scratch=[pltpu.VMEM((tm, D), F32)],
    )(z, wgu, wd, h1, tgt, g_fpost)


def _dsilu(x, s):
    return s * (1.0 + x * (1.0 - s))


def _ffn_bwd_x(dy2, gs, us, wgu, wd, h1, y1, dh2, g_fpre, g_post, tm):
    T = dy2.shape[0]
    nf = wd.shape[0] // _FB

    def body(dy2_ref, gs_ref, us_ref, wgu_ref, wd_ref, h1_ref, y1_ref, dh2_ref, gf_ref, gp_ref,
             dgu_ref, dh1_ref, dy1_ref, dgf_ref, dgp_ref, acc):
        i, j = pl.program_id(0), pl.program_id(1)
        dff = _dot_nt(dy2_ref[...], wd_ref[...])
        dg = (dff * gs_ref[...].astype(F32)).astype(BF)
        du = (dff * us_ref[...].astype(F32)).astype(BF)
        dgu = jnp.concatenate([dg[:, 0:FF_PAD], du[:, 0:FF_PAD], dg[:, FF_PAD:_FB], du[:, FF_PAD:_FB]], 1)
        dgu_ref[...] = dgu
        part = jnp.dot(dgu, wgu_ref[...], preferred_element_type=F32)

        @pl.when(j == 0)
        def _():
            acc[...] = part

        @pl.when(j > 0)
        def _():
            acc[...] += part

        @pl.when((i == 0) & (j == 0))
        def _():
            dgf_ref[...] = jnp.zeros(dgf_ref.shape, F32)
            dgp_ref[...] = jnp.zeros(dgp_ref.shape, F32)

        @pl.when(j == nf - 1)
        def _():
            nh, rh = _rms(h1_ref[...], D)
            dh, dgf = _rms_bwd(nh, rh, gf_ref[...], acc[...], D)
            dh1 = dh2_ref[...] + dh
            dh1_ref[...] = dh1
            dgf_ref[...] += dgf
            ny, ry = _rms(y1_ref[...], D)
            dy1, dgp = _rms_bwd(ny, ry, gp_ref[...], dh1, D)
            dy1_ref[...] = dy1.astype(BF)
            dgp_ref[...] += dgp

    tok = lambda n: pl.BlockSpec((tm, n), lambda i, j: (i, 0))
    col = pl.BlockSpec((tm, _FB), lambda i, j: (i, j))
    return _pcall(
        body, "ffn_bwd_x", (T // tm, nf),
        [tok(D), col, col, pl.BlockSpec((2 * _FB, D), lambda i, j: (j, 0)), pl.BlockSpec((_FB, D), lambda i, j: (j, 0)),
         tok(D), tok(D), tok(D), _full((1, D)), _full((1, D))],
        [pl.BlockSpec((tm, 2 * _FB), lambda i, j: (i, j)), tok(D), tok(D), _full((1, D)), _full((1, D))],
        [_sds((T, 2 * nf * _FB), BF), _sds((T, D)), _sds((T, D), BF), _sds((1, D)), _sds((1, D))],
        scratch=[pltpu.VMEM((tm, D), F32)],
    )(dy2, gs, us, wgu, wd, h1, y1, dh2, g_fpre, g_post)


def _ffn_bwd_w(z, ffs, dgu, dy2, tm):
    T = z.shape[0]
    nf = ffs.shape[1] // _FB
    nt = T // tm

    def body(z_ref, ff_ref, dgu_ref, dy2_ref, dwgu_ref, dwd_ref, agu, ad):
        i = pl.program_id(1)
        pgu = _dot_tn(dgu_ref[...], z_ref[...])
        pd = _dot_tn(ff_ref[...], dy2_ref[...])

        @pl.when(i == 0)
        def _():
            agu[...] = pgu
            ad[...] = pd

        @pl.when(i > 0)
        def _():
            agu[...] += pgu
            ad[...] += pd

        @pl.when(i == nt - 1)
        def _():
            dwgu_ref[...] = agu[...].astype(BF)
            dwd_ref[...] = ad[...].astype(BF)

    F = nf * _FB
    tok = lambda n: pl.BlockSpec((tm, n), lambda j, i: (i, 0))
    return _pcall(
        body, "ffn_bwd_w", (nf, nt),
        [tok(D), pl.BlockSpec((tm, _FB), lambda j, i: (i, j)), pl.BlockSpec((tm, 2 * _FB), lambda j, i: (i, j)), tok(D)],
        [pl.BlockSpec((2 * _FB, D), lambda j, i: (j, 0)), pl.BlockSpec((_FB, D), lambda j, i: (j, 0))],
        [_sds((2 * F, D), BF), _sds((F, D), BF)],
        scratch=[pltpu.VMEM((2 * _FB, D), F32), pltpu.VMEM((_FB, D), F32)],
    )(z, ffs, dgu, dy2)


def _out_bwd(dy1, mix, o_pad, w_out, g_on, tm):
    T = dy1.shape[0]
    W = N_HEADS * HB

    def body(dy1_ref, mix_ref, o_ref, w_ref, gon_ref, do_ref, dl_ref, dohg_ref, dw_ref, dgon_ref):
        i = pl.program_id(0)
        dy1v = dy1_ref[...]
        dmix = _dot_nt(dy1v, w_ref[...])
        pw = _dot_tn(mix_ref[...], dy1v)

        @pl.when(i == 0)
        def _():
            dw_ref[...] = pw
            dgon_ref[...] = jnp.zeros(dgon_ref.shape, F32)

        @pl.when(i > 0)
        def _():
            dw_ref[...] += pw

        for h in range(N_HEADS):
            sl = slice(h * HB, (h + 1) * HB)
            ov = o_ref[:, sl].astype(F32)
            n, r = _rms(ov, V_DIM)
            do, dg = _rms_bwd(n, r, gon_ref[:, sl], dmix[:, sl], V_DIM)
            dgon_ref[:, sl] += dg
            do_ref[:, sl] = do.astype(BF)
            dl_ref[:, sl] = jnp.broadcast_to(jnp.sum(do * ov, -1, keepdims=True), (tm, HB))
        dohg_ref[...] = dmix[:, W:MIX_W]

    return _pcall(body, "out_bwd", (T // tm,),
                  [_rows(tm, D), _rows(tm, MIX_W), _rows(tm, W), _full((MIX_W, D)), _full((1, W))],
                  [_rows(tm, W), _rows(tm, W), _rows(tm, G_W), _full((MIX_W, D)), _full((1, W))],
                  [_sds((T, W), BF), _sds((T, W)), _sds((T, G_W)), _sds((MIX_W, D)), _sds((1, W))],
                  )(dy1, mix, o_pad, w_out, g_on)


def _flash_bwd(q, k, v, do, lse, dl, tq, exchange=None):
    T = q.shape[0]
    nq = T // tq
    scale = QK_DIM ** -0.5
    hp = _AH
    W = hp * HB

    def body(k_ref, v_ref, q_ref, do_ref, lse_ref, dl_ref, dk_ref, dv_ref, dq_ref):
        j = pl.program_id(1)

        @pl.when(j == 0)
        def _():
            dq_ref[...] = jnp.zeros(dq_ref.shape, F32)

        def blk(i, carry, masked):
            sl = pl.ds(pl.multiple_of(i * tq, tq), tq)
            out = []
            for h in range(hp):
                ls = slice(h * HB, (h + 1) * HB)
                dk, dv = carry[h]
                kv, vv = k_ref[:, ls], v_ref[:, ls]
                qv, dov = q_ref[sl, ls], do_ref[sl, ls]
                s = _dot_nt(qv, kv) * scale
                if masked:
                    r = lax.broadcasted_iota(jnp.int32, (tq, tq), 0)
                    c = lax.broadcasted_iota(jnp.int32, (tq, tq), 1)
                    s = jnp.where(c <= r, s, NEG)
                p = jnp.exp(s - lse_ref[sl, h * HB:h * HB + 1])
                ds = p * (_dot_nt(dov, vv) - dl_ref[sl, h * HB:h * HB + 1]) * scale
                dq_ref[sl, ls] += _dot(ds, kv)
                out.append((dk + _dot_tn(ds, qv), dv + _dot_tn(p, dov)))
            return tuple(out)

        zero = jnp.zeros((tq, HB), F32)
        carry = blk(j, tuple((zero, zero) for _ in range(hp)), True)
        res = lax.fori_loop(j + 1, nq, lambda i, cr: blk(i, cr, False), carry)
        for h in range(hp):
            ls = slice(h * HB, (h + 1) * HB)
            dk_ref[:, ls] = res[h][0].astype(BF)
            dv_ref[:, ls] = res[h][1].astype(BF)

    tile = pl.BlockSpec((tq, W), lambda h, j: (j, h))
    whole = pl.BlockSpec((T, W), lambda h, j: (0, h))
    return _pcall(body, "flash_bwd", (N_HEADS // hp, nq), [tile, tile, whole, whole, whole, whole],
                  [tile, tile, whole], [_sds((T, N_HEADS * HB), BF)] * 2 + [_sds((T, N_HEADS * HB))],
                  exchange=exchange)(k, v, q, do, lse, dl)


def _mla_prep_bwd(xp, tabs, dq, dk, dv, g_q, g_kv, w_uq, w_uk, w_uv, tm):
    T = xp.shape[0]
    W = N_HEADS * HB

    def body(xp_ref, ta_ref, tb1_ref, tb2_ref, dq_ref, dk_ref, dv_ref, gq_ref, gkv_ref, wuq_ref, wuk_ref, wuv_ref,
             dxp_ref, dwuq_ref, dwuk_ref, dwuv_ref, dgq_ref, dgkv_ref, dqp):
        i = pl.program_id(0)
        ta, tb1, tb2 = ta_ref[...], tb1_ref[...], tb2_ref[...]
        nq, rq = _rms(xp_ref[:, 0:Q_RANK], Q_RANK)
        nkv, rkv = _rms(xp_ref[:, Q_RANK:Q_RANK + KV_RANK], KV_RANK)
        dkr = jnp.zeros((tm, HB), F32)
        for h in range(N_HEADS):
            sl = slice(h * HB, (h + 1) * HB)
            dqp[:, sl] = _unrope(dq_ref[:, sl], ta, tb1, tb2).astype(BF)
            dkr = dkr + dk_ref[:, sl].astype(F32)
        dkr = pltpu.roll(_unrope(dkr, ta, tb1, tb2), HB - NOPE, 1)
        lane = lax.broadcasted_iota(jnp.int32, (tm, HB), 1)
        dxp_ref[:, Q_RANK + KV_RANK:MLA_IN] = jnp.where(lane < ROPE, dkr, 0.0)
        dqpv = dqp[...]
        dkv, dvv = dk_ref[...], dv_ref[...]
        nqs = (nq * gq_ref[...]).astype(BF)
        nkvs = (nkv * gkv_ref[...]).astype(BF)
        pq, pk, pv = _dot_tn(nqs, dqpv), _dot_tn(nkvs, dkv), _dot_tn(nkvs, dvv)
        dcq, dgq = _rms_bwd(nq, rq, gq_ref[...], _dot_nt(dqpv, wuq_ref[...]), Q_RANK)
        dckv, dgkv = _rms_bwd(nkv, rkv, gkv_ref[...], _dot_nt(dkv, wuk_ref[...]) + _dot_nt(dvv, wuv_ref[...]), KV_RANK)
        dxp_ref[:, 0:Q_RANK] = dcq
        dxp_ref[:, Q_RANK:Q_RANK + KV_RANK] = dckv

        @pl.when(i == 0)
        def _():
            dwuq_ref[...] = pq
            dwuk_ref[...] = pk
            dwuv_ref[...] = pv
            dgq_ref[...] = dgq
            dgkv_ref[...] = dgkv

        @pl.when(i > 0)
        def _():
            dwuq_ref[...] += pq
            dwuk_ref[...] += pk
            dwuv_ref[...] += pv
            dgq_ref[...] += dgq
            dgkv_ref[...] += dgkv

    tab = _rows(tm, HB)
    return _pcall(
        body, "mla_prep_bwd", (T // tm,),
        [_rows(tm, MLA_IN), tab, tab, tab, _rows(tm, W), _rows(tm, W), _rows(tm, W), _full((1, Q_RANK)),
         _full((1, KV_RANK)), _full((Q_RANK, W)), _full((KV_RANK, W)), _full((KV_RANK, W))],
        [_rows(tm, MLA_IN), _full((Q_RANK, W)), _full((KV_RANK, W)), _full((KV_RANK, W)), _full((1, Q_RANK)),
         _full((1, KV_RANK))],
        [_sds((T, MLA_IN)), _sds((Q_RANK, W)), _sds((KV_RANK, W)), _sds((KV_RANK, W)), _sds((1, Q_RANK)),
         _sds((1, KV_RANK))],
        scratch=[pltpu.VMEM((tm, W), BF)],
    )(xp, *tabs, dq, dk, dv, g_q, g_kv, w_uq, w_uk, w_uv)


def _hgrn_bwd(xp, o_raw, s_all, a_all, b_all, d_out, lb_logits, g_hn, exchange=None):
    T = xp.shape[0]
    tb = min(_TB_BWD, T)
    ncb = tb // CHUNK
    nb = T // tb
    hp = _HP
    W = hp * G_DIM

    def body(hq_ref, hf_ref, hi_ref, hg_ref, o_ref, sall_ref, aall_ref, ball_ref, dout_ref, lbl_ref, ghn_ref,
             dhq_ref, dhf_ref, dhi_ref, dhg_ref, dlbl_ref, dghn_ref, dst_ref, b_s, acc_lb, acc_g):
        t = pl.program_id(1)
        lb_all = _lower_bound(lbl_ref)

        @pl.when(t == 0)
        def _():
            dst_ref[...] = jnp.zeros(dst_ref.shape, F32)
            acc_lb[...] = jnp.zeros(acc_lb.shape, F32)
            acc_g[...] = jnp.zeros(acc_g.shape, F32)

        row = lax.broadcasted_iota(jnp.int32, (CHUNK, CHUNK), 0)
        col = lax.broadcasted_iota(jnp.int32, (CHUNK, CHUNK), 1)
        tri_t = (col >= row).astype(BF)
        codes = _intra_codes(SUB)
        last = lax.broadcasted_iota(jnp.int32, (CHUNK, G_DIM), 0) == CHUNK - 1

        def chunk(cc, carry):
            c = ncb - 1 - cc
            sl = pl.ds(pl.multiple_of(c * CHUNK, CHUNK), CHUNK)
            for h in range(hp):
                ls = slice(h * G_DIM, (h + 1) * G_DIM)
                lb, ghn = lb_all[:, ls], ghn_ref[:, ls]
                hq, hg = hq_ref[sl, ls], hg_ref[sl, ls]
                q, k, f, _, sig, sq = _gates(hq, hf_ref[sl, ls], lb)
                v = hi_ref[sl, ls]
                b2 = ball_ref[sl, ls]
                b_s[h] = b2
                st = sall_ref[c, h]
                dstn = dst_ref[h]
                o = o_ref[sl, ls]
                dout = dout_ref[sl, ls]
                n, r = _rms(o, G_DIM)
                sg = _sigmoid(hg)
                dhg_ref[sl, ls] = dout * (n * ghn) * _dsilu(hg, sg)
                do, dg = _rms_bwd(n, r, ghn, dout * (hg * sg), G_DIM)
                acc_g[:, ls] += dg
                eb = jnp.exp2(b2)
                bl = b_s[h, CHUNK - 1:CHUNK, :]
                ebl = jnp.exp2(bl)
                ekd = jnp.exp2(bl - b2)
                kd = k * ekd
                a = aall_ref[c, h]
                dq_i, dk_i = _intra(q, k, b2, b_s.at[h], codes, _dot_nt(do, v))
                dhi_ref[sl, ls] = _dot_tn(a, do) + _dot_nt(kd, dstn)
                dk_state = _dot(v, dstn) * ekd
                dq = dq_i + _dot(do, st) * eb
                dk = dk_i + dk_state
                dbl = jnp.sum(k * dk_state, 0, keepdims=True) + ebl * jnp.sum(dstn * st, 0, keepdims=True)
                db = q * dq - k * dk + jnp.where(last, dbl, 0.0)
                df = _tri_mm(tri_t, db) / f - dk
                dhf_ref[sl, ls] = df * (1.0 - lb) * sig * (1.0 - sig)
                acc_lb[:, ls] += jnp.sum(df * (1.0 - sig), 0, keepdims=True)
                dhq_ref[sl, ls] = dq * _dsilu(hq, sq)
                dst_ref[h] = dstn * ebl + _dot_tn(do, q * eb)
            return carry

        lax.fori_loop(0, ncb, chunk, 0, unroll=2)

        @pl.when(t == nb - 1)
        def _():
            dl0 = acc_lb[...] * lb_all * (1.0 - lb_all)
            dlbl_ref[0:1, :] = dl0
            dlbl_ref[1:2, :] = -dl0
            dghn_ref[...] = acc_g[...]

    col_blk = lambda j: pl.BlockSpec((tb, W), lambda p, t: (nb - 1 - t, j * (G_HEADS // hp) + p))
    head = pl.BlockSpec((tb, W), lambda p, t: (nb - 1 - t, p))
    two = pl.BlockSpec((2, W), lambda p, t: (0, p))
    one = pl.BlockSpec((1, W), lambda p, t: (0, p))
    res = _pcall(
        body, "hgrn_bwd", (G_HEADS // hp, nb),
        [col_blk(0), col_blk(1), col_blk(2), col_blk(3), head,
         pl.BlockSpec((ncb, hp, G_DIM, G_DIM), lambda p, t: (nb - 1 - t, p, 0, 0)),
         pl.BlockSpec((ncb, hp, CHUNK, CHUNK), lambda p, t: (nb - 1 - t, p, 0, 0)), head, head, two, one],
        [head, head, head, head, two, one],
        [_sds((T, G_W))] * 4 + [_sds((2, G_W)), _sds((1, G_W))],
        scratch=[pltpu.VMEM((hp, G_DIM, G_DIM), F32), pltpu.VMEM((hp, CHUNK, G_DIM), F32),
                 pltpu.VMEM((1, W), F32), pltpu.VMEM((1, W), F32)], exchange=exchange,
    )(xp, xp, xp, xp, o_raw, s_all, a_all, b_all, d_out, lb_logits, g_hn)
    return res


def _in_bwd_x(x, dxp_m, dxp_h, dh1, w_in_al, g_pre, tm, exchange=None):
    T = x.shape[0]

    def body(x_ref, dm_ref, d0_ref, d1_ref, d2_ref, d3_ref, dh1_ref, w_ref, g_ref, dx_ref, dg_ref):
        i = pl.program_id(0)
        du = _dot_nt(dm_ref[...], w_ref[:, 0:MLA_IN])
        for j, d_ref in enumerate((d0_ref, d1_ref, d2_ref, d3_ref)):
            du = du + _dot_nt(d_ref[...], w_ref[:, MLA_IN + j * G_W:MLA_IN + (j + 1) * G_W])
        nx, r = _rms(x_ref[...], D)
        dx, dg = _rms_bwd(nx, r, g_ref[...], du, D)
        dx_ref[...] = dh1_ref[...] + dx

        @pl.when(i == 0)
        def _():
            dg_ref[...] = dg

        @pl.when(i > 0)
        def _():
            dg_ref[...] += dg

    return _pcall(body, "in_bwd_x", (T // tm,),
                  [_rows(tm, D), _rows(tm, MLA_IN)] + [_rows(tm, G_W)] * 4 + [_rows(tm, D), _full((D, XP_W)), _full((1, D))],
                  [_rows(tm, D), _full((1, D))], [_sds((T, D)), _sds((1, D))], exchange=exchange,
                  )(x, dxp_m, *dxp_h, dh1, w_in_al, g_pre)


def _aligned_col(c):
    return jnp.where(c < Q_RANK + KV_RANK + ROPE, c, c + (KR_PAD - ROPE))


def _align_w_in(g_in):
    tile = 384
    kr_end = Q_RANK + KV_RANK + ROPE

    def body(g_ref, o_ref, gp):
        gp[...] = jnp.zeros(gp.shape, BF)
        for j in range(N_DEV):
            gp[j, :, 0:IN_SH] = g_ref[j]
        r = lax.broadcasted_iota(jnp.int32, (tile, tile), 0)
        c = lax.broadcasted_iota(jnp.int32, (tile, tile), 1)
        for t in range(XP_W // tile):
            lo, hi = t * tile, (t + 1) * tile
            cols = [a if a < kr_end else a - (KR_PAD - ROPE) for a in (lo, hi - 1)]
            acc = jnp.zeros((D, tile), F32)
            for j in range(cols[0] // IN_SH, cols[-1] // IN_SH + 1):
                sel = (r < IN_SH) & (_aligned_col(j * IN_SH + r) == lo + c)
                acc = acc + jnp.dot(gp[j], sel.astype(BF), preferred_element_type=F32)
            o_ref[:, lo:hi] = acc.astype(BF)

    vm = pl.BlockSpec(memory_space=pltpu.VMEM)
    return pl.pallas_call(
        body, name="align_w_in", in_specs=[vm], out_specs=vm, out_shape=_sds((D, XP_W), BF),
        scratch_shapes=[pltpu.VMEM((N_DEV, D, tile), BF)],
        compiler_params=pltpu.CompilerParams(vmem_limit_bytes=_VMEM_LIMIT))(g_in)


def _in_bwd_w(name, u, dxp_m, dxp_h, tm, half, exchange=None):
    T = u.shape[0]
    nt = T // tm
    nr = D // 2
    win = 640

    def body(u_ref, dm_ref, d0_ref, d1_ref, d2_ref, d3_ref, o_ref, acc):
        i = pl.program_id(0)
        ut = u_ref[...].T
        parts = [(0, MLA_IN, dm_ref)] + [(MLA_IN + j * G_W, G_W, d) for j, d in enumerate((d0_ref, d1_ref, d2_ref, d3_ref))]

        @pl.when(i == 0)
        def _():
            for lo, n, d in parts:
                acc[:, lo:lo + n] = jnp.dot(ut, d[...].astype(BF), preferred_element_type=F32)

        @pl.when(i > 0)
        def _():
            for lo, n, d in parts:
                acc[:, lo:lo + n] += jnp.dot(ut, d[...].astype(BF), preferred_element_type=F32)

        @pl.when(i == nt - 1)
        def _():
            wide = 384
            r = lax.broadcasted_iota(jnp.int32, (win, wide), 0)
            c = lax.broadcasted_iota(jnp.int32, (win, wide), 1)
            kr_end = Q_RANK + KV_RANK + ROPE
            for j in range(N_DEV):
                first = j * IN_SH if j * IN_SH < kr_end else j * IN_SH + (KR_PAD - ROPE)
                lo = min(first // HB * HB, XP_W - win)
                sel = (c < IN_SH) & (_aligned_col(j * IN_SH + c) == lo + r)
                res = jnp.dot(acc[:, lo:lo + win].astype(BF), sel.astype(BF), preferred_element_type=F32)
                o_ref[j] = res[:, 0:IN_SH].astype(BF)

    return _pcall(body, name, (nt,),
                  [pl.BlockSpec((tm, nr), lambda i: (i, half)), _rows(tm, MLA_IN)] + [_rows(tm, G_W)] * 4,
                  [_full((N_DEV, nr, IN_SH))], [_sds((N_DEV, nr, IN_SH), BF)],
                  scratch=[pltpu.VMEM((nr, XP_W), F32)], exchange=exchange)(u, dxp_m, *dxp_h)


def _pad_heads(w, width, real):
    lead = w.shape[:-1]
    w = w.reshape(lead + (N_HEADS, real))
    w = jnp.pad(w, [(0, 0)] * len(lead) + [(0, 0), (0, width - real)])
    return w.reshape(lead + (N_HEADS * width,))


def _unpad_heads(w, width, real):
    lead = w.shape[:-1]
    return w.reshape(lead + (N_HEADS, width))[..., :real].reshape(lead + (N_HEADS * real,))


def _rope_tables(positions):
    half = ROPE // 2
    inv_freq = 1.0 / (ROPE_THETA ** (jnp.arange(0, ROPE, 2, dtype=F32) / ROPE))
    ang = positions.astype(F32)[:, None] * inv_freq
    cos, sin = jnp.cos(ang), jnp.sin(ang)
    T = positions.shape[0]
    z = lambda n: jnp.zeros((T, n), F32)
    ta = jnp.concatenate([jnp.ones((T, NOPE), F32), cos, cos, z(HB - QK_DIM)], 1)
    tb1 = jnp.concatenate([z(NOPE), -sin, z(half), z(HB - QK_DIM)], 1)
    tb2 = jnp.concatenate([z(NOPE), z(half), sin, z(HB - QK_DIM)], 1)
    return ta, tb1, tb2


def kernel(x, positions, attn_pre_norm, w_in, mla_q_norm, mla_w_uq, mla_kv_norm, mla_w_ukv, mla_out_norm, hgrn_lb_logits, hgrn_out_norm, w_out, attn_post_norm, ffn_pre_norm, w_gate, w_up, w_down, ffn_post_norm, loss_target, m_attn_pre_norm, m_w_in, m_mla_q_norm, m_mla_w_uq, m_mla_kv_norm, m_mla_w_ukv, m_mla_out_norm, m_hgrn_lb_logits, m_hgrn_out_norm, m_w_out, m_attn_post_norm, m_ffn_pre_norm, m_w_gate, m_w_up, m_w_down, m_ffn_post_norm, v_attn_pre_norm, v_w_in, v_mla_q_norm, v_mla_w_uq, v_mla_kv_norm, v_mla_w_ukv, v_mla_out_norm, v_hgrn_lb_logits, v_hgrn_out_norm, v_w_out, v_attn_post_norm, v_ffn_pre_norm, v_w_gate, v_w_up, v_w_down, v_ffn_post_norm):
    T = x.shape[1]
    tm = min(_TM, T)
    tq = min(_TQ, T)
    xs, tgt = x[0], loss_target[0]
    uq_sh = (Q_RANK // N_DEV, N_HEADS * QK_DIM)

    b_in, b_uq, b_out, b_gu, b_d = _cast_shards(
        w_in[0], mla_w_uq[0].reshape(uq_sh), w_out[0], w_gate[0].T, w_up[0].T, w_down[0])
    g_in, g_uq = _gather_two_level("ag_first", [b_in, b_uq])
    w_in_al = _align_w_in(g_in)
    w_uq_p = _pad_heads(g_uq.reshape(Q_RANK, N_HEADS * QK_DIM), HB, QK_DIM)
    w_ukv = mla_w_ukv[0].astype(BF)
    w_uk_p = _pad_heads(w_ukv[..., :NOPE].reshape(KV_RANK, N_HEADS * NOPE), HB, NOPE)
    w_uv_p = _pad_heads(w_ukv[..., NOPE:].reshape(KV_RANK, N_HEADS * V_DIM), HB, V_DIM)
    g_on_p = _pad_heads(mla_out_norm, HB, V_DIM)
    tabs = _rope_tables(positions[0])

    xp_m, xp_h, u = _fwd_in(xs, attn_pre_norm, w_in_al, tm)
    q_att, qs_att, k_att, v_att = _mla_prep(xp_m, tabs, mla_q_norm, mla_kv_norm, w_uq_p, w_uk_p, w_uv_p, tm)
    o_hgrn, o_raw, s_all, a_all, b_all, wd = _hgrn_fwd(xp_h, hgrn_lb_logits, hgrn_out_norm, ([GATHER], [b_d]))
    wd = wd.reshape(N_DEV * FF_PAD, D)
    o_pad, lse, wgu, g_out = _flash_fwd(qs_att, k_att, v_att, tq, ([GATHER, GATHER], [b_gu, b_out]))
    wgu = wgu.reshape(N_DEV * 2 * FF_PAD, D)
    w_out_full = g_out.reshape(D, D)
    w_out_mla = jnp.pad(w_out_full[:N_HEADS * V_DIM].reshape(N_HEADS, V_DIM, D), ((0, 0), (0, HB - V_DIM), (0, 0)))
    w_out_p = jnp.concatenate([w_out_mla.reshape(N_HEADS * HB, D), w_out_full[N_HEADS * V_DIM:]], 0)
    h1, y1, z, mix = _fwd_out(o_pad, o_hgrn, xs, g_on_p, w_out_p, attn_post_norm, ffn_pre_norm, tm)
    tmf = min(_TMF, T)
    gs, us, ffs, dh2, dy2, d_fpost, loss_row = _ffn_fwd(z, wgu, wd, h1, tgt, ffn_post_norm, tm, _FB // FF_PAD)

    dgu, dh1, dy1, d_fpre, d_post = _ffn_bwd_x(dy2, gs, us, wgu, wd, h1, y1, dh2, ffn_pre_norm, attn_post_norm, tm)
    dwgu, dwd = _ffn_bwd_w(z, ffs, dgu, dy2, tmf)
    do_pad, dl, d_ohg, dw_out_p, d_on_p = _out_bwd(dy1, mix, o_pad, w_out_p, g_on_p, tm)
    dw_out_mla = dw_out_p[:N_HEADS * HB].reshape(N_HEADS, HB, D)[:, :V_DIM].reshape(N_HEADS * V_DIM, D)
    dw_out = jnp.concatenate([dw_out_mla, dw_out_p[N_HEADS * HB:]], 0).reshape(N_DEV, D // N_DEV, D).astype(BF)
    dk_att, dv_att, dq_att, p_gu, p_d, p_out = _flash_bwd(
        q_att, k_att, v_att, do_pad, lse, dl, tq,
        ([SCATTER] * 3, [dwgu.reshape(N_DEV, 2 * FF_PAD, D), dwd.reshape(N_DEV, FF_PAD, D), dw_out]))
    dxp_m, dw_uq_p, dw_uk_p, dw_uv_p, d_gq, d_gkv = _mla_prep_bwd(
        xp_m, tabs, dq_att, dk_att, dv_att, mla_q_norm, mla_kv_norm, w_uq_p, w_uk_p, w_uv_p, tm)
    dw_uq = _unpad_heads(dw_uq_p, HB, QK_DIM).reshape((N_DEV,) + uq_sh).astype(BF)
    dw_ukv = jnp.concatenate([_unpad_heads(dw_uk_p, HB, NOPE).reshape(KV_RANK, N_HEADS, NOPE),
                              _unpad_heads(dw_uv_p, HB, V_DIM).reshape(KV_RANK, N_HEADS, V_DIM)], -1)
    *dxp_h, d_lbl, d_ghn, p_uq, dw_ukv_all = _hgrn_bwd(
        xp_h, o_raw, s_all, a_all, b_all, d_ohg, hgrn_lb_logits, hgrn_out_norm,
        ([SCATTER, GATHER], [dw_uq, dw_ukv.reshape(KV_RANK, N_HEADS * HB)]))
    dw_in_a, = _in_bwd_w("in_bwd_w_a", u, dxp_m, dxp_h, tm, 0)
    dw_in_b, p_in_a = _in_bwd_w("in_bwd_w_b", u, dxp_m, dxp_h, tm, 1, ([SCATTER], [dw_in_a]))
    grad_x, d_pre, p_in_b = _in_bwd_x(xs, dxp_m, dxp_h, dh1, w_in_al, attn_pre_norm, tm, ([SCATTER], [dw_in_b]))
    d_on = _unpad_heads(d_on_p, HB, V_DIM)

    ukv2 = lambda a: a.reshape(KV_RANK, N_HEADS * HB)
    vecs = [d_pre, d_gq, d_gkv, d_on, d_lbl, d_ghn, d_post, d_fpre, d_fpost, loss_row]
    small_w = [attn_pre_norm, mla_q_norm, mla_kv_norm, ukv2(mla_w_ukv), mla_out_norm, hgrn_lb_logits, hgrn_out_norm,
               attn_post_norm, ffn_pre_norm, ffn_post_norm]
    small_m = [m_attn_pre_norm, m_mla_q_norm, m_mla_kv_norm, ukv2(m_mla_w_ukv), m_mla_out_norm, m_hgrn_lb_logits,
               m_hgrn_out_norm, m_attn_post_norm, m_ffn_pre_norm, m_ffn_post_norm]
    small_v = [v_attn_pre_norm, v_mla_q_norm, v_mla_kv_norm, ukv2(v_mla_w_ukv), v_mla_out_norm, v_hgrn_lb_logits,
               v_hgrn_out_norm, v_attn_post_norm, v_ffn_pre_norm, v_ffn_post_norm]
    rall = _final_exchange(vecs)
    s_g, s_d, s_m, s_v, loss_all = _small_adam(rall, dw_ukv_all, 3, small_w, small_m, small_v)
    r_in = _shard_adam("adam_w_in", [p_in_a, p_in_b], w_in[0], m_w_in[0], v_w_in[0], 256)
    r_uq = _shard_adam("adam_w_uq", [p_uq], mla_w_uq[0].reshape(uq_sh), m_mla_w_uq[0].reshape(uq_sh),
                       v_mla_w_uq[0].reshape(uq_sh), uq_sh[0])
    r_out = _shard_adam("adam_w_out", [p_out], w_out[0], m_w_out[0], v_w_out[0], D // N_DEV)
    r_g, r_u = _gate_up_adam(p_gu, (w_gate[0].T, w_up[0].T), (m_w_gate[0].T, m_w_up[0].T),
                             (v_w_gate[0].T, v_w_up[0].T))
    r_g, r_u = [a.T for a in r_g], [a.T for a in r_u]
    r_d = _shard_adam("adam_w_down", [p_d], w_down[0], m_w_down[0], v_w_down[0], FF_SH // 2)

    loss = loss_all[0, 0]

    def assemble(big, small):
        b_in, b_uq, b_out, b_g, b_u, b_d = big
        return [small[0], b_in[None], small[1], b_uq.reshape(mla_w_uq.shape), small[2],
                small[3].reshape(mla_w_ukv.shape), small[4], small[5], small[6], b_out[None], small[7], small[8],
                b_g[None], b_u[None], b_d[None], small[9]]

    outs = [loss, grad_x[None]]
    for idx, small in enumerate((s_g, s_d, s_m, s_v)):
        outs += assemble([r[idx] for r in (r_in, r_uq, r_out, r_g, r_u, r_d)], small)
    return tuple(outs)
```

```python
import jax
import jax.numpy as jnp
from jax import lax
from jax.experimental import pallas as pl
from jax.experimental.pallas import tpu as pltpu

BF = jnp.bfloat16
F32 = jnp.float32
MESH = pl.DeviceIdType.MESH

N_DEV = 8
D = 1024
EPS = 1e-6
LOG2E = 1.4426950408889634
ROPE_THETA = 10000.0
N_HEADS = 8
HB = 128
NOPE = 64
ROPE = 32
V_DIM = 64
QK_DIM = NOPE + ROPE
Q_RANK = 384
KV_RANK = 128
KR_PAD = 128
MLA_IN = Q_RANK + KV_RANK + KR_PAD
G_HEADS = 4
G_DIM = 128
G_W = G_HEADS * G_DIM
CHUNK = 64
SUB = 16
XP_W = MLA_IN + 4 * G_W
IN_SH = 324
FF_SH = 352
FF_PAD = 384
MIX_W = N_HEADS * HB + G_W

ADAM_LR = 0.001
ADAM_B1 = 0.9
ADAM_B2 = 0.999
ADAM_EPS = 1e-08
ADAM_WD = 0.01
ADAM_STEP = 10

_TM = 512
_TMF = 1024
_TQ = 512
_AH = 2
_AH_FWD = 4
_FB = 768
_TB = 1024
_TB_BWD = 512
_HP = 4
V7X_VMEM_BYTES = 64 * 1024 * 1024
_VMEM_LIMIT = V7X_VMEM_BYTES - 8 * 1024 * 1024
NEG = -1e30


def _dot(a, b):
    return jnp.dot(a.astype(BF), b.astype(BF), preferred_element_type=F32)


def _dot_nt(a, b):
    return lax.dot_general(a.astype(BF), b.astype(BF), (((1,), (1,)), ((), ())), preferred_element_type=F32)


def _dot_tn(a, b):
    return lax.dot_general(a.astype(BF), b.astype(BF), (((0,), (0,)), ((), ())), preferred_element_type=F32)


def _sigmoid(x):
    return 1.0 / (1.0 + jnp.exp(-x))


def _rms(x, n):
    r = lax.rsqrt(jnp.sum(x * x, -1, keepdims=True) * (1.0 / n) + EPS)
    return x * r, r


def _rms_bwd(nx, r, g, dy, n):
    dg = jnp.sum(dy * nx, 0, keepdims=True)
    dn = dy * g
    dx = r * (dn - nx * (jnp.sum(dn * nx, -1, keepdims=True) * (1.0 / n)))
    return dx, dg


def _adamw(w, g, m, v):
    m2 = ADAM_B1 * m + (1.0 - ADAM_B1) * g
    v2 = ADAM_B2 * v + (1.0 - ADAM_B2) * (g * g)
    m_hat = m2 / (1.0 - ADAM_B1 ** ADAM_STEP)
    v_hat = v2 / (1.0 - ADAM_B2 ** ADAM_STEP)
    delta = -ADAM_LR * (m_hat / (jnp.sqrt(v_hat) + ADAM_EPS) + ADAM_WD * w)
    return delta, m2, v2


def _pcall(body, name, grid, in_specs, out_specs, out_shape, scratch=(), exchange=None):
    scratch = list(scratch)
    extra = ()
    if exchange is not None:
        kinds, extra = exchange
        in_specs, out_specs, out_shape = list(in_specs), list(out_specs), list(out_shape)
        n_in, n_out, n_scr, n_x = len(in_specs), len(out_specs), len(scratch), len(extra)
        inner = body

        def body(*refs):
            ins, rest = refs[:n_in], refs[n_in:]
            x_src, rest = rest[:n_x], rest[n_x:]
            outs, rest = rest[:n_out], rest[n_out:]
            x_dst, rest = rest[:n_x], rest[n_x:]
            ex = _Exchange(kinds, x_src, x_dst, *rest[n_scr:])
            first = pl.program_id(0) == 0
            last = pl.program_id(0) == grid[0] - 1
            for a in range(1, len(grid)):
                first = first & (pl.program_id(a) == 0)
                last = last & (pl.program_id(a) == grid[a] - 1)
            pl.when(first)(ex.start)
            inner(*ins, *outs, *rest[:n_scr])
            pl.when(last)(ex.wait)

        in_specs += [_HBM] * n_x
        out_specs += [_HBM] * n_x
        out_shape += _exchange_shapes(kinds, extra)
        scratch += _exchange_sems(n_x)
    call = pl.pallas_call(
        body, name=name, grid=grid, in_specs=in_specs, out_specs=out_specs, out_shape=out_shape,
        scratch_shapes=scratch,
        compiler_params=pltpu.CompilerParams(
            dimension_semantics=("arbitrary",) * len(grid), vmem_limit_bytes=_VMEM_LIMIT))
    return lambda *operands: call(*operands, *extra)


def _full(shape):
    return pl.BlockSpec(shape, lambda *_: (0,) * len(shape))


def _rows(tm, n):
    return pl.BlockSpec((tm, n), lambda i, *_: (i, 0))


def _sds(shape, dtype=F32):
    return jax.ShapeDtypeStruct(shape, dtype)


def _peer(k, x, y, c):
    px = 1 - x if (k >> 2) & 1 else x
    py = 1 - y if (k >> 1) & 1 else y
    pc = 1 - c if k & 1 else c
    return px, py, pc


GATHER, SCATTER = "gather", "scatter"


class _Exchange:
    def __init__(self, kinds, srcs, dsts, send_sems, recv_sems, loc_sems):
        self.kinds, self.srcs, self.dsts = kinds, srcs, dsts
        self.send_sems, self.recv_sems, self.loc_sems = send_sems, recv_sems, loc_sems
        self.x, self.y, self.c = lax.axis_index("x"), lax.axis_index("y"), lax.axis_index("c")
        self.me = 4 * self.x + 2 * self.y + self.c

    def _src(self, w, slot):
        return self.srcs[w] if self.kinds[w] == GATHER else self.srcs[w].at[slot]

    def _dst(self, w, slot):
        return self.dsts[w].at[slot]

    def _copy(self, w, k, outgoing):
        px, py, pc = _peer(k, self.x, self.y, self.c)
        pid = 4 * px + 2 * py + pc
        return pltpu.make_async_remote_copy(
            src_ref=self._src(w, pid if outgoing else self.me),
            dst_ref=self._dst(w, self.me if outgoing else pid),
            send_sem=self.send_sems.at[w, k - 1], recv_sem=self.recv_sems.at[w, k - 1],
            device_id=(px, py, pc), device_id_type=MESH)

    def _local(self, w):
        return pltpu.make_async_copy(self._src(w, self.me), self._dst(w, self.me), self.loc_sems.at[w])

    def start(self):
        for w in range(len(self.srcs)):
            self._local(w).start()
            for k in range(1, N_DEV):
                self._copy(w, k, True).start()

    def wait(self):
        for w in range(len(self.srcs)):
            self._local(w).wait()
            for k in range(1, N_DEV):
                self._copy(w, k, False).wait_recv()
        for w in range(len(self.srcs)):
            for k in range(1, N_DEV):
                self._copy(w, k, True).wait_send()


def _exchange_sems(n_w):
    return [pltpu.SemaphoreType.DMA((n_w, N_DEV - 1)), pltpu.SemaphoreType.DMA((n_w, N_DEV - 1)),
            pltpu.SemaphoreType.DMA((n_w,))]


def _exchange_shapes(kinds, srcs):
    return [_sds(((N_DEV,) if kd == GATHER else ()) + tuple(s.shape), s.dtype) for kd, s in zip(kinds, srcs)]


_HBM = pl.BlockSpec(memory_space=pl.ANY)


def _cast_shards(w_in, w_uq, w_out, w_gate_t, w_up_t, w_down):
    shapes = [(D, IN_SH), (Q_RANK // N_DEV, N_HEADS * QK_DIM), (D // N_DEV, D), (2 * FF_PAD, D), (FF_PAD, D)]

    def body(win, wuq, wout, wg, wu, wd, sin_, suq, sout, sgu, sd):
        sin_[...] = win[...].astype(BF)
        suq[...] = wuq[...].astype(BF)
        sout[...] = wout[...].astype(BF)
        sgu[...] = jnp.zeros(sgu.shape, BF)
        sgu[0:FF_SH, :] = wg[...].astype(BF)
        sgu[FF_PAD:FF_PAD + FF_SH, :] = wu[...].astype(BF)
        sd[...] = jnp.zeros(sd.shape, BF)
        sd[0:FF_SH, :] = wd[...].astype(BF)

    vm = pl.BlockSpec(memory_space=pltpu.VMEM)
    return pl.pallas_call(
        body, name="cast_shards", in_specs=[vm] * 6, out_specs=[vm] * 5,
        out_shape=[_sds(s, BF) for s in shapes],
        compiler_params=pltpu.CompilerParams(vmem_limit_bytes=_VMEM_LIMIT),
    )(w_in, w_uq, w_out, w_gate_t, w_up_t, w_down)


def _gather_two_level(name, srcs):
    n_w = len(srcs)

    def body(*refs):
        src, dst = refs[:n_w], refs[n_w:2 * n_w]
        send_sems, recv_sems, loc_sems = refs[2 * n_w:]
        x, y, c = lax.axis_index("x"), lax.axis_index("y"), lax.axis_index("c")
        me, sibling = (x, y, c), (x, y, 1 - c)
        chips = [(1 - x, y), (x, 1 - y), (1 - x, 1 - y)]
        slot = lambda p: 4 * p[0] + 2 * p[1] + p[2]

        def copy(w, k, block, to, own=False):
            return pltpu.make_async_remote_copy(
                src_ref=src[w] if own else dst[w].at[slot(block)], dst_ref=dst[w].at[slot(block)],
                send_sem=send_sems.at[w, k], recv_sem=recv_sems.at[w, k], device_id=to, device_id_type=MESH)

        local = [pltpu.make_async_copy(src[w], dst[w].at[slot(me)], loc_sems.at[w]) for w in range(n_w)]
        first, passed = [], []
        for w in range(n_w):
            local[w].start()
            first.append(copy(w, 0, me, sibling, own=True))
            first += [copy(w, 1 + j, me, (*chip, c), own=True) for j, chip in enumerate(chips)]
        for cp in first:
            cp.start()
        for w in range(n_w):
            for j, chip in enumerate(chips):
                copy(w, 1 + j, (*chip, c), me).wait_recv()
                passed.append(copy(w, 4 + j, (*chip, c), sibling))
                passed[-1].start()
        for w in range(n_w):
            copy(w, 0, sibling, me).wait_recv()
            for j, chip in enumerate(chips):
                copy(w, 4 + j, (*chip, 1 - c), me).wait_recv()
        for cp in first + passed:
            cp.wait_send()
        for w in range(n_w):
            local[w].wait()

    return pl.pallas_call(
        body, name=name, in_specs=[_HBM] * n_w, out_specs=[_HBM] * n_w,
        out_shape=_exchange_shapes([GATHER] * n_w, srcs), scratch_shapes=_exchange_sems(n_w))(*srcs)


def _row_offsets(arrays):
    offs, rows = [], 0
    for a in arrays:
        offs.append(rows)
        rows += a.shape[0]
    return offs, -(-rows // 8) * 8


def _final_exchange(vecs):
    n_p = len(vecs)
    offs, rows = _row_offsets(vecs)

    def body(*refs):
        g_refs = refs[:n_p]
        rall, pk, send_sems, recv_sems, loc_sem = refs[n_p:]
        x, y, c = lax.axis_index("x"), lax.axis_index("y"), lax.axis_index("c")
        me = 4 * x + 2 * y + c
        pk[...] = jnp.zeros(pk.shape, F32)
        for p in range(n_p):
            r, n = g_refs[p].shape
            pk[offs[p]:offs[p] + r, 0:n] = g_refs[p][...]

        def remote(k):
            return pltpu.make_async_remote_copy(
                src_ref=pk, dst_ref=rall.at[me], send_sem=send_sems.at[k - 1], recv_sem=recv_sems.at[k - 1],
                device_id=_peer(k, x, y, c), device_id_type=MESH)

        def arrival(k):
            px, py, pc = _peer(k, x, y, c)
            return pltpu.make_async_remote_copy(
                src_ref=pk, dst_ref=rall.at[4 * px + 2 * py + pc], send_sem=send_sems.at[k - 1],
                recv_sem=recv_sems.at[k - 1], device_id=(px, py, pc), device_id_type=MESH)

        local = pltpu.make_async_copy(pk, rall.at[me], loc_sem)
        local.start()
        for k in range(1, N_DEV):
            remote(k).start()
        local.wait()
        for k in range(1, N_DEV):
            arrival(k).wait_recv()
        for k in range(1, N_DEV):
            remote(k).wait_send()

    vm = pl.BlockSpec(memory_space=pltpu.VMEM)
    return pl.pallas_call(
        body, name="final_exchange", in_specs=[vm] * n_p, out_specs=vm, out_shape=_sds((N_DEV, rows, D)),
        scratch_shapes=[pltpu.VMEM((rows, D), F32),
                        pltpu.SemaphoreType.DMA((N_DEV - 1,)), pltpu.SemaphoreType.DMA((N_DEV - 1,)),
                        pltpu.SemaphoreType.DMA],
    )(*vecs)


def _small_adam(rall, big_parts, big, ws, ms, vs):
    n_p = len(ws)
    packed = [w for p, w in enumerate(ws) if p != big] + [jax.ShapeDtypeStruct((1, HB), F32)]
    offs, _ = _row_offsets(packed)
    offs = offs[:big] + [None] + offs[big:]

    def total(ref, sl):
        g = ref[(0,) + sl]
        for j in range(1, N_DEV):
            g = g + ref[(j,) + sl]
        return g

    def body(*refs):
        rall_ref, big_ref = refs[:2]
        w_refs, m_refs, v_refs = refs[2:2 + n_p], refs[2 + n_p:2 + 2 * n_p], refs[2 + 2 * n_p:2 + 3 * n_p]
        outs = refs[2 + 3 * n_p:]
        for p in range(n_p):
            r, n = w_refs[p].shape
            if p == big:
                g = total(big_ref, (slice(0, r), slice(0, n)))
            else:
                g = total(rall_ref, (slice(offs[p], offs[p] + r), slice(0, n)))
            delta, m2, v2 = _adamw(w_refs[p][...], g, m_refs[p][...], v_refs[p][...])
            outs[p][...] = g
            outs[n_p + p][...] = delta
            outs[2 * n_p + p][...] = m2
            outs[3 * n_p + p][...] = v2
        outs[4 * n_p][...] = total(rall_ref, (slice(offs[n_p], offs[n_p] + 1), slice(0, HB)))

    vm = pl.BlockSpec(memory_space=pltpu.VMEM)
    res = pl.pallas_call(
        body, name="small_adam", in_specs=[vm] * (2 + 3 * n_p), out_specs=[vm] * (4 * n_p + 1),
        out_shape=[_sds(w.shape) for w in ws] * 4 + [_sds((1, HB))],
        compiler_params=pltpu.CompilerParams(vmem_limit_bytes=_VMEM_LIMIT),
    )(rall, big_parts, *ws, *ms, *vs)
    return res[:n_p], res[n_p:2 * n_p], res[2 * n_p:3 * n_p], res[3 * n_p:4 * n_p], res[4 * n_p]


def _device_sum(p_ref):
    g = p_ref[0].astype(F32)
    for j in range(1, N_DEV):
        g = g + p_ref[j].astype(F32)
    return g


def _shard_adam(name, parts, w, m, v, tr):
    a0, b0 = w.shape
    n_p = len(parts)
    b = parts[0].shape[2]
    first = [0]
    for p in parts:
        first.append(first[-1] + p.shape[1] // tr)

    def body(*refs):
        p_refs = refs[:n_p]
        w_ref, m_ref, v_ref, g_out, d_out, m_out, v_out = refs[n_p:]
        i = pl.program_id(0)
        g = _device_sum(p_refs[0])
        for k in range(1, n_p):
            g = jnp.where(i >= first[k], _device_sum(p_refs[k]), g)
        g = g[:, 0:b0]
        delta, m2, v2 = _adamw(w_ref[...], g, m_ref[...], v_ref[...])
        g_out[...] = g
        d_out[...] = delta
        m_out[...] = m2
        v_out[...] = v2

    def part_spec(k):
        last = first[k + 1] - first[k] - 1
        return pl.BlockSpec((N_DEV, tr, b), lambda i: (0, jnp.minimum(jnp.maximum(i - first[k], 0), last), 0))

    blk = pl.BlockSpec((tr, b0), lambda i: (i, 0))
    return _pcall(
        body, name, (a0 // tr,), [part_spec(k) for k in range(n_p)] + [blk, blk, blk],
        [blk] * 4, [_sds((a0, b0))] * 4)(*parts, w, m, v)


def _gate_up_adam(parts, ws, ms, vs):
    tc = 256

    def body(p_ref, wg, wu, mg, mu, vg, vu, *outs):
        g = _device_sum(p_ref)
        for k, (w_ref, m_ref, v_ref) in enumerate(((wg, mg, vg), (wu, mu, vu))):
            gk = g[k * FF_PAD:k * FF_PAD + FF_SH]
            delta, m2, v2 = _adamw(w_ref[...], gk, m_ref[...], v_ref[...])
            for o, val in zip(outs[4 * k:4 * k + 4], (gk, delta, m2, v2)):
                o[...] = val

    blk = pl.BlockSpec((FF_SH, tc), lambda i: (0, i))
    res = _pcall(
        body, "adam_w_gate_up", (D // tc,), [pl.BlockSpec((N_DEV, 2 * FF_PAD, tc), lambda i: (0, 0, i))] + [blk] * 6,
        [blk] * 8, [_sds((FF_SH, D))] * 8)(parts, *ws, *ms, *vs)
    return res[:4], res[4:]


def _fwd_in(x, g_pre, w_in_al, tm, exchange=None):
    T = x.shape[0]

    def body(x_ref, g_ref, w_ref, xm_ref, xh_ref, u_ref):
        nx, _ = _rms(x_ref[...], D)
        u = (nx * g_ref[...]).astype(BF)
        u_ref[...] = u
        xm_ref[...] = jnp.dot(u, w_ref[:, 0:MLA_IN], preferred_element_type=F32)
        xh_ref[...] = jnp.dot(u, w_ref[:, MLA_IN:XP_W], preferred_element_type=F32)

    return _pcall(body, "fwd_in", (T // tm,),
                  [_rows(tm, D), _full((1, D)), _full((D, XP_W))],
                  [_rows(tm, MLA_IN), _rows(tm, 4 * G_W), _rows(tm, D)],
                  [_sds((T, MLA_IN)), _sds((T, 4 * G_W)), _sds((T, D), BF)], exchange=exchange)(x, g_pre, w_in_al)


def _rope(blk, ta, tb1, tb2):
    return blk * ta + pltpu.roll(blk, HB - ROPE // 2, 1) * tb1 + pltpu.roll(blk, ROPE // 2, 1) * tb2


def _unrope(d, ta, tb1, tb2):
    return d * ta + pltpu.roll(d * tb1, ROPE // 2, 1) + pltpu.roll(d * tb2, HB - ROPE // 2, 1)


def _mla_prep(xp, tabs, g_q, g_kv, w_uq, w_uk, w_uv, tm):
    T = xp.shape[0]
    W = N_HEADS * HB

    def body(xp_ref, ta_ref, tb1_ref, tb2_ref, gq_ref, gkv_ref, wuq_ref, wuk_ref, wuv_ref, q_ref, qs_ref, k_ref, v_ref):
        ta, tb1, tb2 = ta_ref[...], tb1_ref[...], tb2_ref[...]
        nq, _ = _rms(xp_ref[:, 0:Q_RANK], Q_RANK)
        nkv, _ = _rms(xp_ref[:, Q_RANK:Q_RANK + KV_RANK], KV_RANK)
        nkv = (nkv * gkv_ref[...]).astype(BF)
        qpre = _dot(nq * gq_ref[...], wuq_ref[...])
        kpre = jnp.dot(nkv, wuk_ref[...], preferred_element_type=F32)
        v = jnp.dot(nkv, wuv_ref[...], preferred_element_type=F32)
        lane = lax.broadcasted_iota(jnp.int32, (tm, W), 1)
        v_ref[...] = jnp.where((lane & (HB - 1)) == V_DIM, 1.0, v).astype(BF)
        kr = _rope(pltpu.roll(xp_ref[:, Q_RANK + KV_RANK:MLA_IN], NOPE, 1), ta, tb1, tb2)
        for h in range(N_HEADS):
            sl = slice(h * HB, (h + 1) * HB)
            qr = _rope(qpre[:, sl], ta, tb1, tb2)
            q_ref[:, sl] = qr.astype(BF)
            qs_ref[:, sl] = (qr * (QK_DIM ** -0.5 * LOG2E)).astype(BF)
            k_ref[:, sl] = (kpre[:, sl] + kr).astype(BF)

    tab = _rows(tm, HB)
    return _pcall(body, "mla_prep", (T // tm,),
                  [_rows(tm, MLA_IN), tab, tab, tab, _full((1, Q_RANK)), _full((1, KV_RANK)),
                   _full((Q_RANK, W)), _full((KV_RANK, W)), _full((KV_RANK, W))],
                  [_rows(tm, W)] * 4, [_sds((T, W), BF)] * 4)(xp, *tabs, g_q, g_kv, w_uq, w_uk, w_uv)


def _flash_fwd(q, k, v, tq, exchange=None):
    T = q.shape[0]
    hp = _AH_FWD
    W = hp * HB

    def body(q_ref, k_ref, v_ref, o_ref, lse_ref):
        i = pl.program_id(1)

        def blk(j, carry, masked):
            st = pl.multiple_of(j * tq, tq)
            out = []
            for h in range(hp):
                ls = slice(h * HB, (h + 1) * HB)
                m, acc = carry[h]
                s = _dot_nt(q_ref[:, ls], k_ref[pl.ds(st, tq), ls])
                if masked:
                    r = lax.broadcasted_iota(jnp.int32, (tq, tq), 0)
                    c = lax.broadcasted_iota(jnp.int32, (tq, tq), 1)
                    s = jnp.where(c <= r, s, NEG)
                m2 = jnp.maximum(m, jnp.max(s, -1, keepdims=True))
                p = jnp.exp2(s - m2)
                out.append((m2, jnp.exp2(m - m2) * acc + _dot(p, v_ref[pl.ds(st, tq), ls])))
            return tuple(out)

        init = tuple((jnp.full((tq, 1), NEG, F32), jnp.zeros((tq, HB), F32)) for _ in range(hp))
        carry = lax.fori_loop(0, i, lambda j, cr: blk(j, cr, False), init)
        res = blk(i, carry, True)
        lane = lax.broadcasted_iota(jnp.int32, (tq, HB), 1)
        for h in range(hp):
            ls = slice(h * HB, (h + 1) * HB)
            m, acc = res[h]
            l = acc[:, V_DIM:V_DIM + 1]
            o_ref[:, ls] = jnp.where(lane < V_DIM, acc / l, 0.0).astype(BF)
            lse_ref[:, ls] = jnp.broadcast_to(m * (1.0 / LOG2E) + jnp.log(l), (tq, HB))

    qs = pl.BlockSpec((tq, W), lambda h, i: (i, h))
    kvs = pl.BlockSpec((T, W), lambda h, i: (0, h))
    return _pcall(body, "flash_fwd", (N_HEADS // hp, T // tq), [qs, kvs, kvs], [qs, qs],
                  [_sds((T, N_HEADS * HB), BF), _sds((T, N_HEADS * HB))], exchange=exchange)(q, k, v)


def _gates(hq, hf, lb):
    sig = _sigmoid(hf)
    f = lb + (1.0 - lb) * sig
    sq = _sigmoid(hq)
    return hq * sq, 1.0 - f, f, jnp.log(f), sig, sq


def _lower_bound(lbl_ref):
    l0, l1 = lbl_ref[0:1, :], lbl_ref[1:2, :]
    mx = jnp.maximum(l0, l1)
    e0, e1 = jnp.exp(l0 - mx), jnp.exp(l1 - mx)
    return e0 / (e0 + e1)


def _split3(x):
    hi = x.astype(BF)
    r1 = x - hi.astype(F32)
    mid = r1.astype(BF)
    lo = (r1 - mid.astype(F32)).astype(BF)
    return hi, mid, lo


def _tri_mm(tri, x):
    hi, mid, lo = _split3(x)
    mm = lambda t: jnp.dot(tri, t, preferred_element_type=F32)
    return mm(hi) + mm(mid) + mm(lo)


def _intra_codes(sub):
    row = lax.broadcasted_iota(jnp.int32, (CHUNK, CHUNK), 0)
    col = lax.broadcasted_iota(jnp.int32, (CHUNK, CHUNK), 1)
    return sub, row, col


def _intra(q, k, b2, b_s, codes, da=None):
    grad = da is not None
    pow2 = (lambda x: jnp.exp2(jnp.minimum(x, 0.0))) if grad else jnp.exp2
    sub, row, col = codes
    a = jnp.zeros((CHUNK, CHUNK), F32)
    dq = jnp.zeros((CHUNK, G_DIM), F32)
    dk = jnp.zeros((CHUNK, G_DIM), F32)
    for i in range(1, CHUNK // sub):
        b0 = b_s[sub * i - 1:sub * i, :]
        eq, ek = pow2(b2 - b0), pow2(b0 - b2)
        mask = ((row // sub) == i) & (col < sub * i)
        if grad:
            dai = jnp.where(mask, da, 0.0)
            dq = dq + _dot(dai, k * ek) * eq
            dk = dk + _dot_tn(dai, q * eq) * ek
        else:
            a = jnp.where(mask, _dot_nt(q * eq, k * ek), a)
    for d in range(sub):
        ksh = pltpu.roll(k, d, 0) if d else k
        bsh = pltpu.roll(b2, d, 0) if d else b2
        e = pow2(b2 - bsh)
        mask = (col == row - d) & ((row & (sub - 1)) >= d)
        if grad:
            g = jnp.sum(jnp.where(mask, da, 0.0), -1, keepdims=True) * e
            dq = dq + g * ksh
            cb = g * q
            dk = dk + (pltpu.roll(cb, CHUNK - d, 0) if d else cb)
        else:
            a = jnp.where(mask, jnp.sum(q * ksh * e, -1, keepdims=True), a)
    return (dq, dk) if grad else a


def _hgrn_fwd(xp, lb_logits, g_hn, exchange=None):
    T = xp.shape[0]
    tb = min(_TB, T)
    ncb = tb // CHUNK
    hp = _HP
    W = hp * G_DIM

    def body(hq_ref, hf_ref, hi_ref, hg_ref, lbl_ref, ghn_ref, out_ref, oraw_ref, sall_ref, aall_ref, ball_ref,
             st_ref, b_s):
        lb_all = _lower_bound(lbl_ref)

        @pl.when(pl.program_id(1) == 0)
        def _():
            st_ref[...] = jnp.zeros(st_ref.shape, F32)

        row = lax.broadcasted_iota(jnp.int32, (CHUNK, CHUNK), 0)
        col = lax.broadcasted_iota(jnp.int32, (CHUNK, CHUNK), 1)
        tri = (col <= row).astype(BF)
        codes = _intra_codes(SUB)

        def chunk(c, carry):
            sl = pl.ds(pl.multiple_of(c * CHUNK, CHUNK), CHUNK)
            for h in range(hp):
                ls = slice(h * G_DIM, (h + 1) * G_DIM)
                q, k, _, lf, _, _ = _gates(hq_ref[sl, ls], hf_ref[sl, ls], lb_all[:, ls])
                v = hi_ref[sl, ls]
                b2 = _tri_mm(tri, lf) * LOG2E
                b_s[h] = b2
                ball_ref[sl, ls] = b2
                st = st_ref[h]
                sall_ref[c, h] = st
                a = _intra(q, k, b2, b_s.at[h], codes)
                aall_ref[c, h] = a
                o = _dot_nt(q * jnp.exp2(b2), st) + _dot(a, v)
                bl = b_s[h, CHUNK - 1:CHUNK, :]
                st_ref[h] = st * jnp.exp2(bl) + _dot_tn(v, k * jnp.exp2(bl - b2))
                oraw_ref[sl, ls] = o
                n, _ = _rms(o, G_DIM)
                hg = hg_ref[sl, ls]
                out_ref[sl, ls] = n * ghn_ref[:, ls] * (hg * _sigmoid(hg))
            return carry

        lax.fori_loop(0, ncb, chunk, 0, unroll=4)

    col_blk = lambda j: pl.BlockSpec((tb, W), lambda p, t: (t, j * (G_HEADS // hp) + p))
    head = pl.BlockSpec((tb, W), lambda p, t: (t, p))
    return _pcall(
        body, "hgrn_fwd", (G_HEADS // hp, T // tb),
        [col_blk(0), col_blk(1), col_blk(2), col_blk(3),
         pl.BlockSpec((2, W), lambda p, t: (0, p)), pl.BlockSpec((1, W), lambda p, t: (0, p))],
        [head, head, pl.BlockSpec((ncb, hp, G_DIM, G_DIM), lambda p, t: (t, p, 0, 0)),
         pl.BlockSpec((ncb, hp, CHUNK, CHUNK), lambda p, t: (t, p, 0, 0)), head],
        [_sds((T, G_W)), _sds((T, G_W)), _sds((T // CHUNK, G_HEADS, G_DIM, G_DIM)),
         _sds((T // CHUNK, G_HEADS, CHUNK, CHUNK)), _sds((T, G_W))],
        scratch=[pltpu.VMEM((hp, G_DIM, G_DIM), F32), pltpu.VMEM((hp, CHUNK, G_DIM), F32)], exchange=exchange,
    )(xp, xp, xp, xp, lb_logits, g_hn)


def _fwd_out(o_pad, o_hgrn, x, g_on, w_out, g_post, g_fpre, tm):
    T = x.shape[0]

    def body(o_ref, oh_ref, x_ref, gon_ref, w_ref, gpost_ref, gfpre_ref, h1_ref, y1_ref, z_ref, mix_ref):
        for h in range(N_HEADS):
            sl = slice(h * HB, (h + 1) * HB)
            n, _ = _rms(o_ref[:, sl].astype(F32), V_DIM)
            mix_ref[:, sl] = (n * gon_ref[:, sl]).astype(BF)
        mix_ref[:, N_HEADS * HB:MIX_W] = oh_ref[...].astype(BF)
        y1 = jnp.dot(mix_ref[...], w_ref[...], preferred_element_type=F32)
        y1_ref[...] = y1
        ny, _ = _rms(y1, D)
        h1 = x_ref[...] + ny * gpost_ref[...]
        h1_ref[...] = h1
        nh, _ = _rms(h1, D)
        z_ref[...] = (nh * gfpre_ref[...]).astype(BF)

    return _pcall(body, "fwd_out", (T // tm,),
                  [_rows(tm, N_HEADS * HB), _rows(tm, G_W), _rows(tm, D), _full((1, N_HEADS * HB)),
                   _full((MIX_W, D)), _full((1, D)), _full((1, D))],
                  [_rows(tm, D), _rows(tm, D), _rows(tm, D), _rows(tm, MIX_W)],
                  [_sds((T, D)), _sds((T, D)), _sds((T, D), BF), _sds((T, MIX_W), BF)],
                  )(o_pad, o_hgrn, x, g_on, w_out, g_post, g_fpre)


def _ffn_fwd(z, wgu, wd, h1, tgt, g_fpost, tm, nd):
    T = z.shape[0]
    fb = nd * FF_PAD
    nf = wd.shape[0] // fb

    def body(z_ref, wgu_ref, wd_ref, h1_ref, t_ref, gp_ref,
             as_ref, bs_ref, ff_ref, dh2_ref, dy2_ref, dgp_ref, loss_ref, acc):
        i, j = pl.program_id(0), pl.program_id(1)
        gu = _dot_nt(z_ref[...], wgu_ref[...])
        piece = lambda n: gu[:, n * FF_PAD:(n + 1) * FF_PAD]
        g = piece(0) if nd == 1 else jnp.concatenate([piece(2 * n) for n in range(nd)], 1)
        u = piece(1) if nd == 1 else jnp.concatenate([piece(2 * n + 1) for n in range(nd)], 1)
        s = _sigmoid(g)
        b = g * s
        ff = (b * u).astype(BF)
        as_ref[...] = (u * _dsilu(g, s)).astype(BF)
        bs_ref[...] = b.astype(BF)
        ff_ref[...] = ff
        part = jnp.dot(ff, wd_ref[...], preferred_element_type=F32)

        @pl.when(j == 0)
        def _():
            acc[...] = part

        @pl.when(j > 0)
        def _():
            acc[...] += part

        @pl.when((i == 0) & (j == 0))
        def _():
            dgp_ref[...] = jnp.zeros(dgp_ref.shape, F32)
            loss_ref[...] = jnp.zeros(loss_ref.shape, F32)

        @pl.when(j == nf - 1)
        def _():
            ny, r = _rms(acc[...], D)
            err = h1_ref[...] + ny * gp_ref[...] - t_ref[...]
            loss_ref[...] += 0.5 * jnp.sum(jnp.sum(err * err, -1, keepdims=True) * (1.0 / D), 0, keepdims=True)
            dh2 = err * (1.0 / D)
            dh2_ref[...] = dh2
            dy2, dgp = _rms_bwd(ny, r, gp_ref[...], dh2, D)
            dy2_ref[...] = dy2.astype(BF)
            dgp_ref[...] += dgp

    tok = lambda n: pl.BlockSpec((tm, n), lambda i, j: (i, 0))
    col = pl.BlockSpec((tm, fb), lambda i, j: (i, j))
    return _pcall(
        body, "ffn_fwd", (T // tm, nf),
        [tok(D), pl.BlockSpec((2 * fb, D), lambda i, j: (j, 0)), pl.BlockSpec((fb, D), lambda i, j: (j, 0)),
         tok(D), tok(D), _full((1, D))],
        [col, col, col, tok(D), tok(D), _full((1, D)), _full((1, HB))],
        [_sds((T, nf * fb), BF)] * 3 + [_sds((T, D)), _sds((T, D), BF), _sds((1, D)), _sds((1, HB))],
        scratch=[pltpu.VMEM((tm, D), F32)],
    )(z, wgu, wd, h1, tgt, g_fpost)


def _dsilu(x, s):
    return s * (1.0 + x * (1.0 - s))


def _ffn_bwd_x(dy2, gs, us, wgu, wd, h1, y1, dh2, g_fpre, g_post, tm):
    T = dy2.shape[0]
    nf = wd.shape[0] // _FB

    def body(dy2_ref, gs_ref, us_ref, wgu_ref, wd_ref, h1_ref, y1_ref, dh2_ref, gf_ref, gp_ref,
             dgu_ref, dh1_ref, dy1_ref, dgf_ref, dgp_ref, acc):
        i, j = pl.program_id(0), pl.program_id(1)
        dff = _dot_nt(dy2_ref[...], wd_ref[...])
        dg = (dff * gs_ref[...].astype(F32)).astype(BF)
        du = (dff * us_ref[...].astype(F32)).astype(BF)
        dgu = jnp.concatenate([dg[:, 0:FF_PAD], du[:, 0:FF_PAD], dg[:, FF_PAD:_FB], du[:, FF_PAD:_FB]], 1)
        dgu_ref[...] = dgu
        part = jnp.dot(dgu, wgu_ref[...], preferred_element_type=F32)

        @pl.when(j == 0)
        def _():
            acc[...] = part

        @pl.when(j > 0)
        def _():
            acc[...] += part

        @pl.when((i == 0) & (j == 0))
        def _():
            dgf_ref[...] = jnp.zeros(dgf_ref.shape, F32)
            dgp_ref[...] = jnp.zeros(dgp_ref.shape, F32)

        @pl.when(j == nf - 1)
        def _():
            nh, rh = _rms(h1_ref[...], D)
            dh, dgf = _rms_bwd(nh, rh, gf_ref[...], acc[...], D)
            dh1 = dh2_ref[...] + dh
            dh1_ref[...] = dh1
            dgf_ref[...] += dgf
            ny, ry = _rms(y1_ref[...], D)
            dy1, dgp = _rms_bwd(ny, ry, gp_ref[...], dh1, D)
            dy1_ref[...] = dy1.astype(BF)
            dgp_ref[...] += dgp

    tok = lambda n: pl.BlockSpec((tm, n), lambda i, j: (i, 0))
    col = pl.BlockSpec((tm, _FB), lambda i, j: (i, j))
    return _pcall(
        body, "ffn_bwd_x", (T // tm, nf),
        [tok(D), col, col, pl.BlockSpec((2 * _FB, D), lambda i, j: (j, 0)), pl.BlockSpec((_FB, D), lambda i, j: (j, 0)),
         tok(D), tok(D), tok(D), _full((1, D)), _full((1, D))],
        [pl.BlockSpec((tm, 2 * _FB), lambda i, j: (i, j)), tok(D), tok(D), _full((1, D)), _full((1, D))],
        [_sds((T, 2 * nf * _FB), BF), _sds((T, D)), _sds((T, D), BF), _sds((1, D)), _sds((1, D))],
        scratch=[pltpu.VMEM((tm, D), F32)],
    )(dy2, gs, us, wgu, wd, h1, y1, dh2, g_fpre, g_post)


def _ffn_bwd_w(z, ffs, dgu, dy2, tm):
    T = z.shape[0]
    nf = ffs.shape[1] // _FB
    nt = T // tm

    def body(z_ref, ff_ref, dgu_ref, dy2_ref, dwgu_ref, dwd_ref, agu, ad):
        i = pl.program_id(1)
        pgu = _dot_tn(dgu_ref[...], z_ref[...])
        pd = _dot_tn(ff_ref[...], dy2_ref[...])

        @pl.when(i == 0)
        def _():
            agu[...] = pgu
            ad[...] = pd

        @pl.when(i > 0)
        def _():
            agu[...] += pgu
            ad[...] += pd

        @pl.when(i == nt - 1)
        def _():
            dwgu_ref[...] = agu[...].astype(BF)
            dwd_ref[...] = ad[...].astype(BF)

    F = nf * _FB
    tok = lambda n: pl.BlockSpec((tm, n), lambda j, i: (i, 0))
    return _pcall(
        body, "ffn_bwd_w", (nf, nt),
        [tok(D), pl.BlockSpec((tm, _FB), lambda j, i: (i, j)), pl.BlockSpec((tm, 2 * _FB), lambda j, i: (i, j)), tok(D)],
        [pl.BlockSpec((2 * _FB, D), lambda j, i: (j, 0)), pl.BlockSpec((_FB, D), lambda j, i: (j, 0))],
        [_sds((2 * F, D), BF), _sds((F, D), BF)],
        scratch=[pltpu.VMEM((2 * _FB, D), F32), pltpu.VMEM((_FB, D), F32)],
    )(z, ffs, dgu, dy2)


def _out_bwd(dy1, mix, o_pad, w_out, g_on, tm):
    T = dy1.shape[0]
    W = N_HEADS * HB

    def body(dy1_ref, mix_ref, o_ref, w_ref, gon_ref, do_ref, dl_ref, dohg_ref, dw_ref, dgon_ref):
        i = pl.program_id(0)
        dy1v = dy1_ref[...]
        dmix = _dot_nt(dy1v, w_ref[...])
        pw = _dot_tn(mix_ref[...], dy1v)

        @pl.when(i == 0)
        def _():
            dw_ref[...] = pw
            dgon_ref[...] = jnp.zeros(dgon_ref.shape, F32)

        @pl.when(i > 0)
        def _():
            dw_ref[...] += pw

        for h in range(N_HEADS):
            sl = slice(h * HB, (h + 1) * HB)
            ov = o_ref[:, sl].astype(F32)
            n, r = _rms(ov, V_DIM)
            do, dg = _rms_bwd(n, r, gon_ref[:, sl], dmix[:, sl], V_DIM)
            dgon_ref[:, sl] += dg
            do_ref[:, sl] = do.astype(BF)
            dl_ref[:, sl] = jnp.broadcast_to(jnp.sum(do * ov, -1, keepdims=True), (tm, HB))
        dohg_ref[...] = dmix[:, W:MIX_W]

    return _pcall(body, "out_bwd", (T // tm,),
                  [_rows(tm, D), _rows(tm, MIX_W), _rows(tm, W), _full((MIX_W, D)), _full((1, W))],
                  [_rows(tm, W), _rows(tm, W), _rows(tm, G_W), _full((MIX_W, D)), _full((1, W))],
                  [_sds((T, W), BF), _sds((T, W)), _sds((T, G_W)), _sds((MIX_W, D)), _sds((1, W))],
                  )(dy1, mix, o_pad, w_out, g_on)


def _flash_bwd(q, k, v, do, lse, dl, tq, exchange=None):
    T = q.shape[0]
    nq = T // tq
    scale = QK_DIM ** -0.5
    hp = _AH
    W = hp * HB

    def body(k_ref, v_ref, q_ref, do_ref, lse_ref, dl_ref, dk_ref, dv_ref, dq_ref):
        j = pl.program_id(1)

        @pl.when(j == 0)
        def _():
            dq_ref[...] = jnp.zeros(dq_ref.shape, F32)

        def blk(i, carry, masked):
            sl = pl.ds(pl.multiple_of(i * tq, tq), tq)
            out = []
            for h in range(hp):
                ls = slice(h * HB, (h + 1) * HB)
                dk, dv = carry[h]
                kv, vv = k_ref[:, ls], v_ref[:, ls]
                qv, dov = q_ref[sl, ls], do_ref[sl, ls]
                s = _dot_nt(qv, kv) * scale
                if masked:
                    r = lax.broadcasted_iota(jnp.int32, (tq, tq), 0)
                    c = lax.broadcasted_iota(jnp.int32, (tq, tq), 1)
                    s = jnp.where(c <= r, s, NEG)
                p = jnp.exp(s - lse_ref[sl, h * HB:h * HB + 1])
                ds = p * (_dot_nt(dov, vv) - dl_ref[sl, h * HB:h * HB + 1]) * scale
                dq_ref[sl, ls] += _dot(ds, kv)
                out.append((dk + _dot_tn(ds, qv), dv + _dot_tn(p, dov)))
            return tuple(out)

        zero = jnp.zeros((tq, HB), F32)
        carry = blk(j, tuple((zero, zero) for _ in range(hp)), True)
        res = lax.fori_loop(j + 1, nq, lambda i, cr: blk(i, cr, False), carry)
        for h in range(hp):
            ls = slice(h * HB, (h + 1) * HB)
            dk_ref[:, ls] = res[h][0].astype(BF)
            dv_ref[:, ls] = res[h][1].astype(BF)

    tile = pl.BlockSpec((tq, W), lambda h, j: (j, h))
    whole = pl.BlockSpec((T, W), lambda h, j: (0, h))
    return _pcall(body, "flash_bwd", (N_HEADS // hp, nq), [tile, tile, whole, whole, whole, whole],
                  [tile, tile, whole], [_sds((T, N_HEADS * HB), BF)] * 2 + [_sds((T, N_HEADS * HB))],
                  exchange=exchange)(k, v, q, do, lse, dl)


def _mla_prep_bwd(xp, tabs, dq, dk, dv, g_q, g_kv, w_uq, w_uk, w_uv, tm):
    T = xp.shape[0]
    W = N_HEADS * HB

    def body(xp_ref, ta_ref, tb1_ref, tb2_ref, dq_ref, dk_ref, dv_ref, gq_ref, gkv_ref, wuq_ref, wuk_ref, wuv_ref,
             dxp_ref, dwuq_ref, dwuk_ref, dwuv_ref, dgq_ref, dgkv_ref, dqp):
        i = pl.program_id(0)
        ta, tb1, tb2 = ta_ref[...], tb1_ref[...], tb2_ref[...]
        nq, rq = _rms(xp_ref[:, 0:Q_RANK], Q_RANK)
        nkv, rkv = _rms(xp_ref[:, Q_RANK:Q_RANK + KV_RANK], KV_RANK)
        dkr = jnp.zeros((tm, HB), F32)
        for h in range(N_HEADS):
            sl = slice(h * HB, (h + 1) * HB)
            dqp[:, sl] = _unrope(dq_ref[:, sl], ta, tb1, tb2).astype(BF)
            dkr = dkr + dk_ref[:, sl].astype(F32)
        dkr = pltpu.roll(_unrope(dkr, ta, tb1, tb2), HB - NOPE, 1)
        lane = lax.broadcasted_iota(jnp.int32, (tm, HB), 1)
        dxp_ref[:, Q_RANK + KV_RANK:MLA_IN] = jnp.where(lane < ROPE, dkr, 0.0)
        dqpv = dqp[...]
        dkv, dvv = dk_ref[...], dv_ref[...]
        nqs = (nq * gq_ref[...]).astype(BF)
        nkvs = (nkv * gkv_ref[...]).astype(BF)
        pq, pk, pv = _dot_tn(nqs, dqpv), _dot_tn(nkvs, dkv), _dot_tn(nkvs, dvv)
        dcq, dgq = _rms_bwd(nq, rq, gq_ref[...], _dot_nt(dqpv, wuq_ref[...]), Q_RANK)
        dckv, dgkv = _rms_bwd(nkv, rkv, gkv_ref[...], _dot_nt(dkv, wuk_ref[...]) + _dot_nt(dvv, wuv_ref[...]), KV_RANK)
        dxp_ref[:, 0:Q_RANK] = dcq
        dxp_ref[:, Q_RANK:Q_RANK + KV_RANK] = dckv

        @pl.when(i == 0)
        def _():
            dwuq_ref[...] = pq
            dwuk_ref[...] = pk
            dwuv_ref[...] = pv
            dgq_ref[...] = dgq
            dgkv_ref[...] = dgkv

        @pl.when(i > 0)
        def _():
            dwuq_ref[...] += pq
            dwuk_ref[...] += pk
            dwuv_ref[...] += pv
            dgq_ref[...] += dgq
            dgkv_ref[...] += dgkv

    tab = _rows(tm, HB)
    return _pcall(
        body, "mla_prep_bwd", (T // tm,),
        [_rows(tm, MLA_IN), tab, tab, tab, _rows(tm, W), _rows(tm, W), _rows(tm, W), _full((1, Q_RANK)),
         _full((1, KV_RANK)), _full((Q_RANK, W)), _full((KV_RANK, W)), _full((KV_RANK, W))],
        [_rows(tm, MLA_IN), _full((Q_RANK, W)), _full((KV_RANK, W)), _full((KV_RANK, W)), _full((1, Q_RANK)),
         _full((1, KV_RANK))],
        [_sds((T, MLA_IN)), _sds((Q_RANK, W)), _sds((KV_RANK, W)), _sds((KV_RANK, W)), _sds((1, Q_RANK)),
         _sds((1, KV_RANK))],
        scratch=[pltpu.VMEM((tm, W), BF)],
    )(xp, *tabs, dq, dk, dv, g_q, g_kv, w_uq, w_uk, w_uv)


def _hgrn_bwd(xp, o_raw, s_all, a_all, b_all, d_out, lb_logits, g_hn, exchange=None):
    T = xp.shape[0]
    tb = min(_TB_BWD, T)
    ncb = tb // CHUNK
    nb = T // tb
    hp = _HP
    W = hp * G_DIM

    def body(hq_ref, hf_ref, hi_ref, hg_ref, o_ref, sall_ref, aall_ref, ball_ref, dout_ref, lbl_ref, ghn_ref,
             dhq_ref, dhf_ref, dhi_ref, dhg_ref, dlbl_ref, dghn_ref, dst_ref, b_s, acc_lb, acc_g):
        t = pl.program_id(1)
        lb_all = _lower_bound(lbl_ref)

        @pl.when(t == 0)
        def _():
            dst_ref[...] = jnp.zeros(dst_ref.shape, F32)
            acc_lb[...] = jnp.zeros(acc_lb.shape, F32)
            acc_g[...] = jnp.zeros(acc_g.shape, F32)

        row = lax.broadcasted_iota(jnp.int32, (CHUNK, CHUNK), 0)
        col = lax.broadcasted_iota(jnp.int32, (CHUNK, CHUNK), 1)
        tri_t = (col >= row).astype(BF)
        codes = _intra_codes(SUB)
        last = lax.broadcasted_iota(jnp.int32, (CHUNK, G_DIM), 0) == CHUNK - 1

        def chunk(cc, carry):
            c = ncb - 1 - cc
            sl = pl.ds(pl.multiple_of(c * CHUNK, CHUNK), CHUNK)
            for h in range(hp):
                ls = slice(h * G_DIM, (h + 1) * G_DIM)
                lb, ghn = lb_all[:, ls], ghn_ref[:, ls]
                hq, hg = hq_ref[sl, ls], hg_ref[sl, ls]
                q, k, f, _, sig, sq = _gates(hq, hf_ref[sl, ls], lb)
                v = hi_ref[sl, ls]
                b2 = ball_ref[sl, ls]
                b_s[h] = b2
                st = sall_ref[c, h]
                dstn = dst_ref[h]
                o = o_ref[sl, ls]
                dout = dout_ref[sl, ls]
                n, r = _rms(o, G_DIM)
                sg = _sigmoid(hg)
                dhg_ref[sl, ls] = dout * (n * ghn) * _dsilu(hg, sg)
                do, dg = _rms_bwd(n, r, ghn, dout * (hg * sg), G_DIM)
                acc_g[:, ls] += dg
                eb = jnp.exp2(b2)
                bl = b_s[h, CHUNK - 1:CHUNK, :]
                ebl = jnp.exp2(bl)
                ekd = jnp.exp2(bl - b2)
                kd = k * ekd
                a = aall_ref[c, h]
                dq_i, dk_i = _intra(q, k, b2, b_s.at[h], codes, _dot_nt(do, v))
                dhi_ref[sl, ls] = _dot_tn(a, do) + _dot_nt(kd, dstn)
                dk_state = _dot(v, dstn) * ekd
                dq = dq_i + _dot(do, st) * eb
                dk = dk_i + dk_state
                dbl = jnp.sum(k * dk_state, 0, keepdims=True) + ebl * jnp.sum(dstn * st, 0, keepdims=True)
                db = q * dq - k * dk + jnp.where(last, dbl, 0.0)
                df = _tri_mm(tri_t, db) / f - dk
                dhf_ref[sl, ls] = df * (1.0 - lb) * sig * (1.0 - sig)
                acc_lb[:, ls] += jnp.sum(df * (1.0 - sig), 0, keepdims=True)
                dhq_ref[sl, ls] = dq * _dsilu(hq, sq)
                dst_ref[h] = dstn * ebl + _dot_tn(do, q * eb)
            return carry

        lax.fori_loop(0, ncb, chunk, 0, unroll=2)

        @pl.when(t == nb - 1)
        def _():
            dl0 = acc_lb[...] * lb_all * (1.0 - lb_all)
            dlbl_ref[0:1, :] = dl0
            dlbl_ref[1:2, :] = -dl0
            dghn_ref[...] = acc_g[...]

    col_blk = lambda j: pl.BlockSpec((tb, W), lambda p, t: (nb - 1 - t, j * (G_HEADS // hp) + p))
    head = pl.BlockSpec((tb, W), lambda p, t: (nb - 1 - t, p))
    two = pl.BlockSpec((2, W), lambda p, t: (0, p))
    one = pl.BlockSpec((1, W), lambda p, t: (0, p))
    res = _pcall(
        body, "hgrn_bwd", (G_HEADS // hp, nb),
        [col_blk(0), col_blk(1), col_blk(2), col_blk(3), head,
         pl.BlockSpec((ncb, hp, G_DIM, G_DIM), lambda p, t: (nb - 1 - t, p, 0, 0)),
         pl.BlockSpec((ncb, hp, CHUNK, CHUNK), lambda p, t: (nb - 1 - t, p, 0, 0)), head, head, two, one],
        [head, head, head, head, two, one],
        [_sds((T, G_W))] * 4 + [_sds((2, G_W)), _sds((1, G_W))],
        scratch=[pltpu.VMEM((hp, G_DIM, G_DIM), F32), pltpu.VMEM((hp, CHUNK, G_DIM), F32),
                 pltpu.VMEM((1, W), F32), pltpu.VMEM((1, W), F32)], exchange=exchange,
    )(xp, xp, xp, xp, o_raw, s_all, a_all, b_all, d_out, lb_logits, g_hn)
    return res


def _in_bwd_x(x, dxp_m, dxp_h, dh1, w_in_al, g_pre, tm, exchange=None):
    T = x.shape[0]

    def body(x_ref, dm_ref, d0_ref, d1_ref, d2_ref, d3_ref, dh1_ref, w_ref, g_ref, dx_ref, dg_ref):
        i = pl.program_id(0)
        du = _dot_nt(dm_ref[...], w_ref[:, 0:MLA_IN])
        for j, d_ref in enumerate((d0_ref, d1_ref, d2_ref, d3_ref)):
            du = du + _dot_nt(d_ref[...], w_ref[:, MLA_IN + j * G_W:MLA_IN + (j + 1) * G_W])
        nx, r = _rms(x_ref[...], D)
        dx, dg = _rms_bwd(nx, r, g_ref[...], du, D)
        dx_ref[...] = dh1_ref[...] + dx

        @pl.when(i == 0)
        def _():
            dg_ref[...] = dg

        @pl.when(i > 0)
        def _():
            dg_ref[...] += dg

    return _pcall(body, "in_bwd_x", (T // tm,),
                  [_rows(tm, D), _rows(tm, MLA_IN)] + [_rows(tm, G_W)] * 4 + [_rows(tm, D), _full((D, XP_W)), _full((1, D))],
                  [_rows(tm, D), _full((1, D))], [_sds((T, D)), _sds((1, D))], exchange=exchange,
                  )(x, dxp_m, *dxp_h, dh1, w_in_al, g_pre)


def _aligned_col(c):
    return jnp.where(c < Q_RANK + KV_RANK + ROPE, c, c + (KR_PAD - ROPE))


def _align_w_in(g_in):
    tile = 384
    kr_end = Q_RANK + KV_RANK + ROPE

    def body(g_ref, o_ref, gp):
        gp[...] = jnp.zeros(gp.shape, BF)
        for j in range(N_DEV):
            gp[j, :, 0:IN_SH] = g_ref[j]
        r = lax.broadcasted_iota(jnp.int32, (tile, tile), 0)
        c = lax.broadcasted_iota(jnp.int32, (tile, tile), 1)
        for t in range(XP_W // tile):
            lo, hi = t * tile, (t + 1) * tile
            cols = [a if a < kr_end else a - (KR_PAD - ROPE) for a in (lo, hi - 1)]
            acc = jnp.zeros((D, tile), F32)
            for j in range(cols[0] // IN_SH, cols[-1] // IN_SH + 1):
                sel = (r < IN_SH) & (_aligned_col(j * IN_SH + r) == lo + c)
                acc = acc + jnp.dot(gp[j], sel.astype(BF), preferred_element_type=F32)
            o_ref[:, lo:hi] = acc.astype(BF)

    vm = pl.BlockSpec(memory_space=pltpu.VMEM)
    return pl.pallas_call(
        body, name="align_w_in", in_specs=[vm], out_specs=vm, out_shape=_sds((D, XP_W), BF),
        scratch_shapes=[pltpu.VMEM((N_DEV, D, tile), BF)],
        compiler_params=pltpu.CompilerParams(vmem_limit_bytes=_VMEM_LIMIT))(g_in)


def _in_bwd_w(name, u, dxp_m, dxp_h, tm, half, exchange=None):
    T = u.shape[0]
    nt = T // tm
    nr = D // 2
    win = 640

    def body(u_ref, dm_ref, d0_ref, d1_ref, d2_ref, d3_ref, o_ref, acc):
        i = pl.program_id(0)
        ut = u_ref[...].T
        parts = [(0, MLA_IN, dm_ref)] + [(MLA_IN + j * G_W, G_W, d) for j, d in enumerate((d0_ref, d1_ref, d2_ref, d3_ref))]

        @pl.when(i == 0)
        def _():
            for lo, n, d in parts:
                acc[:, lo:lo + n] = jnp.dot(ut, d[...].astype(BF), preferred_element_type=F32)

        @pl.when(i > 0)
        def _():
            for lo, n, d in parts:
                acc[:, lo:lo + n] += jnp.dot(ut, d[...].astype(BF), preferred_element_type=F32)

        @pl.when(i == nt - 1)
        def _():
            wide = 384
            r = lax.broadcasted_iota(jnp.int32, (win, wide), 0)
            c = lax.broadcasted_iota(jnp.int32, (win, wide), 1)
            kr_end = Q_RANK + KV_RANK + ROPE
            for j in range(N_DEV):
                first = j * IN_SH if j * IN_SH < kr_end else j * IN_SH + (KR_PAD - ROPE)
                lo = min(first // HB * HB, XP_W - win)
                sel = (c < IN_SH) & (_aligned_col(j * IN_SH + c) == lo + r)
                res = jnp.dot(acc[:, lo:lo + win].astype(BF), sel.astype(BF), preferred_element_type=F32)
                o_ref[j] = res[:, 0:IN_SH].astype(BF)

    return _pcall(body, name, (nt,),
                  [pl.BlockSpec((tm, nr), lambda i: (i, half)), _rows(tm, MLA_IN)] + [_rows(tm, G_W)] * 4,
                  [_full((N_DEV, nr, IN_SH))], [_sds((N_DEV, nr, IN_SH), BF)],
                  scratch=[pltpu.VMEM((nr, XP_W), F32)], exchange=exchange)(u, dxp_m, *dxp_h)


def _pad_heads(w, width, real):
    lead = w.shape[:-1]
    w = w.reshape(lead + (N_HEADS, real))
    w = jnp.pad(w, [(0, 0)] * len(lead) + [(0, 0), (0, width - real)])
    return w.reshape(lead + (N_HEADS * width,))


def _unpad_heads(w, width, real):
    lead = w.shape[:-1]
    return w.reshape(lead + (N_HEADS, width))[..., :real].reshape(lead + (N_HEADS * real,))


def _rope_tables(positions):
    half = ROPE // 2
    inv_freq = 1.0 / (ROPE_THETA ** (jnp.arange(0, ROPE, 2, dtype=F32) / ROPE))
    ang = positions.astype(F32)[:, None] * inv_freq
    cos, sin = jnp.cos(ang), jnp.sin(ang)
    T = positions.shape[0]
    z = lambda n: jnp.zeros((T, n), F32)
    ta = jnp.concatenate([jnp.ones((T, NOPE), F32), cos, cos, z(HB - QK_DIM)], 1)
    tb1 = jnp.concatenate([z(NOPE), -sin, z(half), z(HB - QK_DIM)], 1)
    tb2 = jnp.concatenate([z(NOPE), z(half), sin, z(HB - QK_DIM)], 1)
    return ta, tb1, tb2


def kernel(x, positions, attn_pre_norm, w_in, mla_q_norm, mla_w_uq, mla_kv_norm, mla_w_ukv, mla_out_norm, hgrn_lb_logits, hgrn_out_norm, w_out, attn_post_norm, ffn_pre_norm, w_gate, w_up, w_down, ffn_post_norm, loss_target, m_attn_pre_norm, m_w_in, m_mla_q_norm, m_mla_w_uq, m_mla_kv_norm, m_mla_w_ukv, m_mla_out_norm, m_hgrn_lb_logits, m_hgrn_out_norm, m_w_out, m_attn_post_norm, m_ffn_pre_norm, m_w_gate, m_w_up, m_w_down, m_ffn_post_norm, v_attn_pre_norm, v_w_in, v_mla_q_norm, v_mla_w_uq, v_mla_kv_norm, v_mla_w_ukv, v_mla_out_norm, v_hgrn_lb_logits, v_hgrn_out_norm, v_w_out, v_attn_post_norm, v_ffn_pre_norm, v_w_gate, v_w_up, v_w_down, v_ffn_post_norm):
    T = x.shape[1]
    tm = min(_TM, T)
    tq = min(_TQ, T)
    xs, tgt = x[0], loss_target[0]
    uq_sh = (Q_RANK // N_DEV, N_HEADS * QK_DIM)

    b_in, b_uq, b_out, b_gu, b_d = _cast_shards(
        w_in[0], mla_w_uq[0].reshape(uq_sh), w_out[0], w_gate[0].T, w_up[0].T, w_down[0])
    g_in, g_uq = _gather_two_level("ag_first", [b_in, b_uq])
    w_in_al = _align_w_in(g_in)
    w_uq_p = _pad_heads(g_uq.reshape(Q_RANK, N_HEADS * QK_DIM), HB, QK_DIM)
    w_ukv = mla_w_ukv[0].astype(BF)
    w_uk_p = _pad_heads(w_ukv[..., :NOPE].reshape(KV_RANK, N_HEADS * NOPE), HB, NOPE)
    w_uv_p = _pad_heads(w_ukv[..., NOPE:].reshape(KV_RANK, N_HEADS * V_DIM), HB, V_DIM)
    g_on_p = _pad_heads(mla_out_norm, HB, V_DIM)
    tabs = _rope_tables(positions[0])

    xp_m, xp_h, u, g_out = _fwd_in(xs, attn_pre_norm, w_in_al, tm, ([GATHER], [b_out]))
    q_att, qs_att, k_att, v_att = _mla_prep(xp_m, tabs, mla_q_norm, mla_kv_norm, w_uq_p, w_uk_p, w_uv_p, tm)
    o_hgrn, o_raw, s_all, a_all, b_all, wd = _hgrn_fwd(xp_h, hgrn_lb_logits, hgrn_out_norm, ([GATHER], [b_d]))
    wd = wd.reshape(N_DEV * FF_PAD, D)
    o_pad, lse, wgu = _flash_fwd(qs_att, k_att, v_att, tq, ([GATHER], [b_gu]))
    wgu = wgu.reshape(N_DEV * 2 * FF_PAD, D)
    w_out_full = g_out.reshape(D, D)
    w_out_mla = jnp.pad(w_out_full[:N_HEADS * V_DIM].reshape(N_HEADS, V_DIM, D), ((0, 0), (0, HB - V_DIM), (0, 0)))
    w_out_p = jnp.concatenate([w_out_mla.reshape(N_HEADS * HB, D), w_out_full[N_HEADS * V_DIM:]], 0)
    h1, y1, z, mix = _fwd_out(o_pad, o_hgrn, xs, g_on_p, w_out_p, attn_post_norm, ffn_pre_norm, tm)
    tmf = min(_TMF, T)
    gs, us, ffs, dh2, dy2, d_fpost, loss_row = _ffn_fwd(z, wgu, wd, h1, tgt, ffn_post_norm, tm, _FB // FF_PAD)

    dgu, dh1, dy1, d_fpre, d_post = _ffn_bwd_x(dy2, gs, us, wgu, wd, h1, y1, dh2, ffn_pre_norm, attn_post_norm, tm)
    dwgu, dwd = _ffn_bwd_w(z, ffs, dgu, dy2, tmf)
    do_pad, dl, d_ohg, dw_out_p, d_on_p = _out_bwd(dy1, mix, o_pad, w_out_p, g_on_p, tm)
    dw_out_mla = dw_out_p[:N_HEADS * HB].reshape(N_HEADS, HB, D)[:, :V_DIM].reshape(N_HEADS * V_DIM, D)
    dw_out = jnp.concatenate([dw_out_mla, dw_out_p[N_HEADS * HB:]], 0).reshape(N_DEV, D // N_DEV, D).astype(BF)
    dk_att, dv_att, dq_att, p_gu, p_d, p_out = _flash_bwd(
        q_att, k_att, v_att, do_pad, lse, dl, tq,
        ([SCATTER] * 3, [dwgu.reshape(N_DEV, 2 * FF_PAD, D), dwd.reshape(N_DEV, FF_PAD, D), dw_out]))
    dxp_m, dw_uq_p, dw_uk_p, dw_uv_p, d_gq, d_gkv = _mla_prep_bwd(
        xp_m, tabs, dq_att, dk_att, dv_att, mla_q_norm, mla_kv_norm, w_uq_p, w_uk_p, w_uv_p, tm)
    dw_uq = _unpad_heads(dw_uq_p, HB, QK_DIM).reshape((N_DEV,) + uq_sh).astype(BF)
    dw_ukv = jnp.concatenate([_unpad_heads(dw_uk_p, HB, NOPE).reshape(KV_RANK, N_HEADS, NOPE),
                              _unpad_heads(dw_uv_p, HB, V_DIM).reshape(KV_RANK, N_HEADS, V_DIM)], -1)
    *dxp_h, d_lbl, d_ghn, p_uq, dw_ukv_all = _hgrn_bwd(
        xp_h, o_raw, s_all, a_all, b_all, d_ohg, hgrn_lb_logits, hgrn_out_norm,
        ([SCATTER, GATHER], [dw_uq, dw_ukv.reshape(KV_RANK, N_HEADS * HB)]))
    dw_in_a, = _in_bwd_w("in_bwd_w_a", u, dxp_m, dxp_h, tm, 0)
    dw_in_b, p_in_a = _in_bwd_w("in_bwd_w_b", u, dxp_m, dxp_h, tm, 1, ([SCATTER], [dw_in_a]))
    grad_x, d_pre, p_in_b = _in_bwd_x(xs, dxp_m, dxp_h, dh1, w_in_al, attn_pre_norm, tm, ([SCATTER], [dw_in_b]))
    d_on = _unpad_heads(d_on_p, HB, V_DIM)

    ukv2 = lambda a: a.reshape(KV_RANK, N_HEADS * HB)
    vecs = [d_pre, d_gq, d_gkv, d_on, d_lbl, d_ghn, d_post, d_fpre, d_fpost, loss_row]
    small_w = [attn_pre_norm, mla_q_norm, mla_kv_norm, ukv2(mla_w_ukv), mla_out_norm, hgrn_lb_logits, hgrn_out_norm,
               attn_post_norm, ffn_pre_norm, ffn_post_norm]
    small_m = [m_attn_pre_norm, m_mla_q_norm, m_mla_kv_norm, ukv2(m_mla_w_ukv), m_mla_out_norm, m_hgrn_lb_logits,
               m_hgrn_out_norm, m_attn_post_norm, m_ffn_pre_norm, m_ffn_post_norm]
    small_v = [v_attn_pre_norm, v_mla_q_norm, v_mla_kv_norm, ukv2(v_mla_w_ukv), v_mla_out_norm, v_hgrn_lb_logits,
               v_hgrn_out_norm, v_attn_post_norm, v_ffn_pre_norm, v_ffn_post_norm]
    rall = _final_exchange(vecs)
    s_g, s_d, s_m, s_v, loss_all = _small_adam(rall, dw_ukv_all, 3, small_w, small_m, small_v)
    r_in = _shard_adam("adam_w_in", [p_in_a, p_in_b], w_in[0], m_w_in[0], v_w_in[0], 256)
    r_uq = _shard_adam("adam_w_uq", [p_uq], mla_w_uq[0].reshape(uq_sh), m_mla_w_uq[0].reshape(uq_sh),
                       v_mla_w_uq[0].reshape(uq_sh), uq_sh[0])
    r_out = _shard_adam("adam_w_out", [p_out], w_out[0], m_w_out[0], v_w_out[0], D // N_DEV)
    r_g, r_u = _gate_up_adam(p_gu, (w_gate[0].T, w_up[0].T), (m_w_gate[0].T, m_w_up[0].T),
                             (v_w_gate[0].T, v_w_up[0].T))
    r_g, r_u = [a.T for a in r_g], [a.T for a in r_u]
    r_d = _shard_adam("adam_w_down", [p_d], w_down[0], m_w_down[0], v_w_down[0], FF_SH // 2)

    loss = loss_all[0, 0]

    def assemble(big, small):
        b_in, b_uq, b_out, b_g, b_u, b_d = big
        return [small[0], b_in[None], small[1], b_uq.reshape(mla_w_uq.shape), small[2],
                small[3].reshape(mla_w_ukv.shape), small[4], small[5], small[6], b_out[None], small[7], small[8],
                b_g[None], b_u[None], b_d[None], small[9]]

    outs = [loss, grad_x[None]]
    for idx, small in enumerate((s_g, s_d, s_m, s_v)):
        outs += assemble([r[idx] for r in (r_in, r_uq, r_out, r_g, r_u, r_d)], small)
    return tuple(outs)
```

```python
import jax
import jax.numpy as jnp
from jax import lax
from jax.experimental import pallas as pl
from jax.experimental.pallas import tpu as pltpu

BF = jnp.bfloat16
F32 = jnp.float32
MESH = pl.DeviceIdType.MESH

N_DEV = 8
D = 1024
EPS = 1e-6
LOG2E = 1.4426950408889634
ROPE_THETA = 10000.0
N_HEADS = 8
HB = 128
NOPE = 64
ROPE = 32
V_DIM = 64
QK_DIM = NOPE + ROPE
Q_RANK = 384
KV_RANK = 128
KR_PAD = 128
MLA_IN = Q_RANK + KV_RANK + KR_PAD
G_HEADS = 4
G_DIM = 128
G_W = G_HEADS * G_DIM
CHUNK = 64
SUB = 16
XP_W = MLA_IN + 4 * G_W
IN_SH = 324
FF_SH = 352
FF_PAD = 384
MIX_W = N_HEADS * HB + G_W

ADAM_LR = 0.001
ADAM_B1 = 0.9
ADAM_B2 = 0.999
ADAM_EPS = 1e-08
ADAM_WD = 0.01
ADAM_STEP = 10

_TM = 512
_TMF = 1024
_TQ = 512
_AH = 2
_AH_FWD = 4
_FB = 768
_TB = 1024
_TB_BWD = 512
_HP = 4
V7X_VMEM_BYTES = 64 * 1024 * 1024
_VMEM_LIMIT = V7X_VMEM_BYTES - 8 * 1024 * 1024
NEG = -1e30


def _dot(a, b):
    return jnp.dot(a.astype(BF), b.astype(BF), preferred_element_type=F32)


def _dot_nt(a, b):
    return lax.dot_general(a.astype(BF), b.astype(BF), (((1,), (1,)), ((), ())), preferred_element_type=F32)


def _dot_tn(a, b):
    return lax.dot_general(a.astype(BF), b.astype(BF), (((0,), (0,)), ((), ())), preferred_element_type=F32)


def _sigmoid(x):
    return 1.0 / (1.0 + jnp.exp(-x))


def _rms(x, n):
    r = lax.rsqrt(jnp.sum(x * x, -1, keepdims=True) * (1.0 / n) + EPS)
    return x * r, r


def _rms_bwd(nx, r, g, dy, n):
    dg = jnp.sum(dy * nx, 0, keepdims=True)
    dn = dy * g
    dx = r * (dn - nx * (jnp.sum(dn * nx, -1, keepdims=True) * (1.0 / n)))
    return dx, dg


def _adamw(w, g, m, v):
    m2 = ADAM_B1 * m + (1.0 - ADAM_B1) * g
    v2 = ADAM_B2 * v + (1.0 - ADAM_B2) * (g * g)
    m_hat = m2 / (1.0 - ADAM_B1 ** ADAM_STEP)
    v_hat = v2 / (1.0 - ADAM_B2 ** ADAM_STEP)
    delta = -ADAM_LR * (m_hat / (jnp.sqrt(v_hat) + ADAM_EPS) + ADAM_WD * w)
    return delta, m2, v2


def _pcall(body, name, grid, in_specs, out_specs, out_shape, scratch=(), exchange=None):
    scratch = list(scratch)
    extra = ()
    if exchange is not None:
        kinds, extra = exchange
        in_specs, out_specs, out_shape = list(in_specs), list(out_specs), list(out_shape)
        n_in, n_out, n_scr, n_x = len(in_specs), len(out_specs), len(scratch), len(extra)
        inner = body

        def body(*refs):
            ins, rest = refs[:n_in], refs[n_in:]
            x_src, rest = rest[:n_x], rest[n_x:]
            outs, rest = rest[:n_out], rest[n_out:]
            x_dst, rest = rest[:n_x], rest[n_x:]
            ex = _Exchange(kinds, x_src, x_dst, *rest[n_scr:])
            first = pl.program_id(0) == 0
            last = pl.program_id(0) == grid[0] - 1
            for a in range(1, len(grid)):
                first = first & (pl.program_id(a) == 0)
                last = last & (pl.program_id(a) == grid[a] - 1)
            pl.when(first)(ex.start)
            inner(*ins, *outs, *rest[:n_scr])
            pl.when(last)(ex.wait)

        in_specs += [_HBM] * n_x
        out_specs += [_HBM] * n_x
        out_shape += _exchange_shapes(kinds, extra)
        scratch += _exchange_sems(n_x)
    call = pl.pallas_call(
        body, name=name, grid=grid, in_specs=in_specs, out_specs=out_specs, out_shape=out_shape,
        scratch_shapes=scratch,
        compiler_params=pltpu.CompilerParams(
            dimension_semantics=("arbitrary",) * len(grid), vmem_limit_bytes=_VMEM_LIMIT))
    return lambda *operands: call(*operands, *extra)


def _full(shape):
    return pl.BlockSpec(shape, lambda *_: (0,) * len(shape))


def _rows(tm, n):
    return pl.BlockSpec((tm, n), lambda i, *_: (i, 0))


def _sds(shape, dtype=F32):
    return jax.ShapeDtypeStruct(shape, dtype)


def _peer(k, x, y, c):
    px = 1 - x if (k >> 2) & 1 else x
    py = 1 - y if (k >> 1) & 1 else y
    pc = 1 - c if k & 1 else c
    return px, py, pc


GATHER, SCATTER = "gather", "scatter"


class _Exchange:
    def __init__(self, kinds, srcs, dsts, send_sems, recv_sems, loc_sems):
        self.kinds, self.srcs, self.dsts = kinds, srcs, dsts
        self.send_sems, self.recv_sems, self.loc_sems = send_sems, recv_sems, loc_sems
        self.x, self.y, self.c = lax.axis_index("x"), lax.axis_index("y"), lax.axis_index("c")
        self.me = 4 * self.x + 2 * self.y + self.c

    def _src(self, w, slot):
        return self.srcs[w] if self.kinds[w] == GATHER else self.srcs[w].at[slot]

    def _dst(self, w, slot):
        return self.dsts[w].at[slot]

    def _copy(self, w, k, outgoing):
        px, py, pc = _peer(k, self.x, self.y, self.c)
        pid = 4 * px + 2 * py + pc
        return pltpu.make_async_remote_copy(
            src_ref=self._src(w, pid if outgoing else self.me),
            dst_ref=self._dst(w, self.me if outgoing else pid),
            send_sem=self.send_sems.at[w, k - 1], recv_sem=self.recv_sems.at[w, k - 1],
            device_id=(px, py, pc), device_id_type=MESH)

    def _local(self, w):
        return pltpu.make_async_copy(self._src(w, self.me), self._dst(w, self.me), self.loc_sems.at[w])

    def start(self):
        for w in range(len(self.srcs)):
            self._local(w).start()
            for k in range(1, N_DEV):
                self._copy(w, k, True).start()

    def wait(self):
        for w in range(len(self.srcs)):
            self._local(w).wait()
            for k in range(1, N_DEV):
                self._copy(w, k, False).wait_recv()
        for w in range(len(self.srcs)):
            for k in range(1, N_DEV):
                self._copy(w, k, True).wait_send()


def _exchange_sems(n_w):
    return [pltpu.SemaphoreType.DMA((n_w, N_DEV - 1)), pltpu.SemaphoreType.DMA((n_w, N_DEV - 1)),
            pltpu.SemaphoreType.DMA((n_w,))]


def _exchange_shapes(kinds, srcs):
    return [_sds(((N_DEV,) if kd == GATHER else ()) + tuple(s.shape), s.dtype) for kd, s in zip(kinds, srcs)]


_HBM = pl.BlockSpec(memory_space=pl.ANY)


def _cast_shards(w_in, w_uq, w_out, w_gate_t, w_up_t, w_down):
    shapes = [(D, IN_SH), (Q_RANK // N_DEV, N_HEADS * QK_DIM), (D // N_DEV, D), (2 * FF_PAD, D), (FF_PAD, D)]

    def body(win, wuq, wout, wg, wu, wd, sin_, suq, sout, sgu, sd):
        sin_[...] = win[...].astype(BF)
        suq[...] = wuq[...].astype(BF)
        sout[...] = wout[...].astype(BF)
        sgu[...] = jnp.zeros(sgu.shape, BF)
        sgu[0:FF_SH, :] = wg[...].astype(BF)
        sgu[FF_PAD:FF_PAD + FF_SH, :] = wu[...].astype(BF)
        sd[...] = jnp.zeros(sd.shape, BF)
        sd[0:FF_SH, :] = wd[...].astype(BF)

    vm = pl.BlockSpec(memory_space=pltpu.VMEM)
    return pl.pallas_call(
        body, name="cast_shards", in_specs=[vm] * 6, out_specs=[vm] * 5,
        out_shape=[_sds(s, BF) for s in shapes],
        compiler_params=pltpu.CompilerParams(vmem_limit_bytes=_VMEM_LIMIT),
    )(w_in, w_uq, w_out, w_gate_t, w_up_t, w_down)


def _gather_two_level(name, srcs):
    n_w = len(srcs)

    def body(*refs):
        src, dst = refs[:n_w], refs[n_w:2 * n_w]
        send_sems, recv_sems, loc_sems = refs[2 * n_w:]
        x, y, c = lax.axis_index("x"), lax.axis_index("y"), lax.axis_index("c")
        me, sibling = (x, y, c), (x, y, 1 - c)
        chips = [(1 - x, y), (x, 1 - y), (1 - x, 1 - y)]
        slot = lambda p: 4 * p[0] + 2 * p[1] + p[2]

        def copy(w, k, block, to, own=False):
            return pltpu.make_async_remote_copy(
                src_ref=src[w] if own else dst[w].at[slot(block)], dst_ref=dst[w].at[slot(block)],
                send_sem=send_sems.at[w, k], recv_sem=recv_sems.at[w, k], device_id=to, device_id_type=MESH)

        local = [pltpu.make_async_copy(src[w], dst[w].at[slot(me)], loc_sems.at[w]) for w in range(n_w)]
        first, passed = [], []
        for w in range(n_w):
            local[w].start()
            first.append(copy(w, 0, me, sibling, own=True))
            first += [copy(w, 1 + j, me, (*chip, c), own=True) for j, chip in enumerate(chips)]
        for cp in first:
            cp.start()
        for w in range(n_w):
            for j, chip in enumerate(chips):
                copy(w, 1 + j, (*chip, c), me).wait_recv()
                passed.append(copy(w, 4 + j, (*chip, c), sibling))
                passed[-1].start()
        for w in range(n_w):
            copy(w, 0, sibling, me).wait_recv()
            for j, chip in enumerate(chips):
                copy(w, 4 + j, (*chip, 1 - c), me).wait_recv()
        for cp in first + passed:
            cp.wait_send()
        for w in range(n_w):
            local[w].wait()

    return pl.pallas_call(
        body, name=name, in_specs=[_HBM] * n_w, out_specs=[_HBM] * n_w,
        out_shape=_exchange_shapes([GATHER] * n_w, srcs), scratch_shapes=_exchange_sems(n_w))(*srcs)


def _row_offsets(arrays):
    offs, rows = [], 0
    for a in arrays:
        offs.append(rows)
        rows += a.shape[0]
    return offs, -(-rows // 8) * 8


def _final_exchange(vecs):
    n_p = len(vecs)
    offs, rows = _row_offsets(vecs)

    def body(*refs):
        g_refs = refs[:n_p]
        rall, pk, send_sems, recv_sems, loc_sem = refs[n_p:]
        x, y, c = lax.axis_index("x"), lax.axis_index("y"), lax.axis_index("c")
        me = 4 * x + 2 * y + c
        pk[...] = jnp.zeros(pk.shape, F32)
        for p in range(n_p):
            r, n = g_refs[p].shape
            pk[offs[p]:offs[p] + r, 0:n] = g_refs[p][...]

        def remote(k):
            return pltpu.make_async_remote_copy(
                src_ref=pk, dst_ref=rall.at[me], send_sem=send_sems.at[k - 1], recv_sem=recv_sems.at[k - 1],
                device_id=_peer(k, x, y, c), device_id_type=MESH)

        def arrival(k):
            px, py, pc = _peer(k, x, y, c)
            return pltpu.make_async_remote_copy(
                src_ref=pk, dst_ref=rall.at[4 * px + 2 * py + pc], send_sem=send_sems.at[k - 1],
                recv_sem=recv_sems.at[k - 1], device_id=(px, py, pc), device_id_type=MESH)

        local = pltpu.make_async_copy(pk, rall.at[me], loc_sem)
        local.start()
        for k in range(1, N_DEV):
            remote(k).start()
        local.wait()
        for k in range(1, N_DEV):
            arrival(k).wait_recv()
        for k in range(1, N_DEV):
            remote(k).wait_send()

    vm = pl.BlockSpec(memory_space=pltpu.VMEM)
    return pl.pallas_call(
        body, name="final_exchange", in_specs=[vm] * n_p, out_specs=vm, out_shape=_sds((N_DEV, rows, D)),
        scratch_shapes=[pltpu.VMEM((rows, D), F32),
                        pltpu.SemaphoreType.DMA((N_DEV - 1,)), pltpu.SemaphoreType.DMA((N_DEV - 1,)),
                        pltpu.SemaphoreType.DMA],
    )(*vecs)


def _small_adam(rall, big_parts, big, ws, ms, vs):
    n_p = len(ws)
    packed = [w for p, w in enumerate(ws) if p != big] + [jax.ShapeDtypeStruct((1, HB), F32)]
    offs, _ = _row_offsets(packed)
    offs = offs[:big] + [None] + offs[big:]

    def total(ref, sl):
        g = ref[(0,) + sl]
        for j in range(1, N_DEV):
            g = g + ref[(j,) + sl]
        return g

    def body(*refs):
        rall_ref, big_ref = refs[:2]
        w_refs, m_refs, v_refs = refs[2:2 + n_p], refs[2 + n_p:2 + 2 * n_p], refs[2 + 2 * n_p:2 + 3 * n_p]
        outs = refs[2 + 3 * n_p:]
        for p in range(n_p):
            r, n = w_refs[p].shape
            if p == big:
                g = total(big_ref, (slice(0, r), slice(0, n)))
            else:
                g = total(rall_ref, (slice(offs[p], offs[p] + r), slice(0, n)))
            delta, m2, v2 = _adamw(w_refs[p][...], g, m_refs[p][...], v_refs[p][...])
            outs[p][...] = g
            outs[n_p + p][...] = delta
            outs[2 * n_p + p][...] = m2
            outs[3 * n_p + p][...] = v2
        outs[4 * n_p][...] = total(rall_ref, (slice(offs[n_p], offs[n_p] + 1), slice(0, HB)))

    vm = pl.BlockSpec(memory_space=pltpu.VMEM)
    res = pl.pallas_call(
        body, name="small_adam", in_specs=[vm] * (2 + 3 * n_p), out_specs=[vm] * (4 * n_p + 1),
        out_shape=[_sds(w.shape) for w in ws] * 4 + [_sds((1, HB))],
        compiler_params=pltpu.CompilerParams(vmem_limit_bytes=_VMEM_LIMIT),
    )(rall, big_parts, *ws, *ms, *vs)
    return res[:n_p], res[n_p:2 * n_p], res[2 * n_p:3 * n_p], res[3 * n_p:4 * n_p], res[4 * n_p]


def _device_sum(p_ref):
    g = p_ref[0].astype(F32)
    for j in range(1, N_DEV):
        g = g + p_ref[j].astype(F32)
    return g


def _shard_adam(name, parts, w, m, v, tr):
    a0, b0 = w.shape
    n_p = len(parts)
    b = parts[0].shape[2]
    first = [0]
    for p in parts:
        first.append(first[-1] + p.shape[1] // tr)

    def body(*refs):
        p_refs = refs[:n_p]
        w_ref, m_ref, v_ref, g_out, d_out, m_out, v_out = refs[n_p:]
        i = pl.program_id(0)
        g = _device_sum(p_refs[0])
        for k in range(1, n_p):
            g = jnp.where(i >= first[k], _device_sum(p_refs[k]), g)
        g = g[:, 0:b0]
        delta, m2, v2 = _adamw(w_ref[...], g, m_ref[...], v_ref[...])
        g_out[...] = g
        d_out[...] = delta
        m_out[...] = m2
        v_out[...] = v2

    def part_spec(k):
        last = first[k + 1] - first[k] - 1
        return pl.BlockSpec((N_DEV, tr, b), lambda i: (0, jnp.minimum(jnp.maximum(i - first[k], 0), last), 0))

    blk = pl.BlockSpec((tr, b0), lambda i: (i, 0))
    return _pcall(
        body, name, (a0 // tr,), [part_spec(k) for k in range(n_p)] + [blk, blk, blk],
        [blk] * 4, [_sds((a0, b0))] * 4)(*parts, w, m, v)


def _gate_up_adam(parts, ws, ms, vs):
    tc = 256

    def body(p_ref, wg, wu, mg, mu, vg, vu, *outs):
        g = _device_sum(p_ref)
        for k, (w_ref, m_ref, v_ref) in enumerate(((wg, mg, vg), (wu, mu, vu))):
            gk = g[k * FF_PAD:k * FF_PAD + FF_SH]
            delta, m2, v2 = _adamw(w_ref[...], gk, m_ref[...], v_ref[...])
            for o, val in zip(outs[4 * k:4 * k + 4], (gk, delta, m2, v2)):
                o[...] = val

    blk = pl.BlockSpec((FF_SH, tc), lambda i: (0, i))
    res = _pcall(
        body, "adam_w_gate_up", (D // tc,), [pl.BlockSpec((N_DEV, 2 * FF_PAD, tc), lambda i: (0, 0, i))] + [blk] * 6,
        [blk] * 8, [_sds((FF_SH, D))] * 8)(parts, *ws, *ms, *vs)
    return res[:4], res[4:]


def _fwd_in(x, g_pre, w_in_al, tm):
    T = x.shape[0]

    def body(x_ref, g_ref, w_ref, xm_ref, xh_ref, u_ref):
        nx, _ = _rms(x_ref[...], D)
        u = (nx * g_ref[...]).astype(BF)
        u_ref[...] = u
        xm_ref[...] = jnp.dot(u, w_ref[:, 0:MLA_IN], preferred_element_type=F32)
        xh_ref[...] = jnp.dot(u, w_ref[:, MLA_IN:XP_W], preferred_element_type=F32)

    return _pcall(body, "fwd_in", (T // tm,),
                  [_rows(tm, D), _full((1, D)), _full((D, XP_W))],
                  [_rows(tm, MLA_IN), _rows(tm, 4 * G_W), _rows(tm, D)],
                  [_sds((T, MLA_IN)), _sds((T, 4 * G_W)), _sds((T, D), BF)])(x, g_pre, w_in_al)


def _rope(blk, ta, tb1, tb2):
    return blk * ta + pltpu.roll(blk, HB - ROPE // 2, 1) * tb1 + pltpu.roll(blk, ROPE // 2, 1) * tb2


def _unrope(d, ta, tb1, tb2):
    return d * ta + pltpu.roll(d * tb1, ROPE // 2, 1) + pltpu.roll(d * tb2, HB - ROPE // 2, 1)


def _mla_prep(xp, tabs, g_q, g_kv, w_uq, w_uk, w_uv, tm):
    T = xp.shape[0]
    W = N_HEADS * HB

    def body(xp_ref, ta_ref, tb1_ref, tb2_ref, gq_ref, gkv_ref, wuq_ref, wuk_ref, wuv_ref, q_ref, qs_ref, k_ref, v_ref):
        ta, tb1, tb2 = ta_ref[...], tb1_ref[...], tb2_ref[...]
        nq, _ = _rms(xp_ref[:, 0:Q_RANK], Q_RANK)
        nkv, _ = _rms(xp_ref[:, Q_RANK:Q_RANK + KV_RANK], KV_RANK)
        nkv = (nkv * gkv_ref[...]).astype(BF)
        qpre = _dot(nq * gq_ref[...], wuq_ref[...])
        kpre = jnp.dot(nkv, wuk_ref[...], preferred_element_type=F32)
        v = jnp.dot(nkv, wuv_ref[...], preferred_element_type=F32)
        lane = lax.broadcasted_iota(jnp.int32, (tm, W), 1)
        v_ref[...] = jnp.where((lane & (HB - 1)) == V_DIM, 1.0, v).astype(BF)
        kr = _rope(pltpu.roll(xp_ref[:, Q_RANK + KV_RANK:MLA_IN], NOPE, 1), ta, tb1, tb2)
        for h in range(N_HEADS):
            sl = slice(h * HB, (h + 1) * HB)
            qr = _rope(qpre[:, sl], ta, tb1, tb2)
            q_ref[:, sl] = qr.astype(BF)
            qs_ref[:, sl] = (qr * (QK_DIM ** -0.5 * LOG2E)).astype(BF)
            k_ref[:, sl] = (kpre[:, sl] + kr).astype(BF)

    tab = _rows(tm, HB)
    return _pcall(body, "mla_prep", (T // tm,),
                  [_rows(tm, MLA_IN), tab, tab, tab, _full((1, Q_RANK)), _full((1, KV_RANK)),
                   _full((Q_RANK, W)), _full((KV_RANK, W)), _full((KV_RANK, W))],
                  [_rows(tm, W)] * 4, [_sds((T, W), BF)] * 4)(xp, *tabs, g_q, g_kv, w_uq, w_uk, w_uv)


def _flash_fwd(q, k, v, tq, exchange=None):
    T = q.shape[0]
    hp = _AH_FWD
    W = hp * HB

    def body(q_ref, k_ref, v_ref, o_ref, lse_ref):
        i = pl.program_id(1)

        def blk(j, carry, masked):
            st = pl.multiple_of(j * tq, tq)
            out = []
            for h in range(hp):
                ls = slice(h * HB, (h + 1) * HB)
                m, acc = carry[h]
                s = _dot_nt(q_ref[:, ls], k_ref[pl.ds(st, tq), ls])
                if masked:
                    r = lax.broadcasted_iota(jnp.int32, (tq, tq), 0)
                    c = lax.broadcasted_iota(jnp.int32, (tq, tq), 1)
                    s = jnp.where(c <= r, s, NEG)
                m2 = jnp.maximum(m, jnp.max(s, -1, keepdims=True))
                p = jnp.exp2(s - m2)
                out.append((m2, jnp.exp2(m - m2) * acc + _dot(p, v_ref[pl.ds(st, tq), ls])))
            return tuple(out)

        init = tuple((jnp.full((tq, 1), NEG, F32), jnp.zeros((tq, HB), F32)) for _ in range(hp))
        carry = lax.fori_loop(0, i, lambda j, cr: blk(j, cr, False), init)
        res = blk(i, carry, True)
        lane = lax.broadcasted_iota(jnp.int32, (tq, HB), 1)
        for h in range(hp):
            ls = slice(h * HB, (h + 1) * HB)
            m, acc = res[h]
            l = acc[:, V_DIM:V_DIM + 1]
            o_ref[:, ls] = jnp.where(lane < V_DIM, acc / l, 0.0).astype(BF)
            lse_ref[:, ls] = jnp.broadcast_to(m * (1.0 / LOG2E) + jnp.log(l), (tq, HB))

    qs = pl.BlockSpec((tq, W), lambda h, i: (i, h))
    kvs = pl.BlockSpec((T, W), lambda h, i: (0, h))
    return _pcall(body, "flash_fwd", (N_HEADS // hp, T // tq), [qs, kvs, kvs], [qs, qs],
                  [_sds((T, N_HEADS * HB), BF), _sds((T, N_HEADS * HB))], exchange=exchange)(q, k, v)


def _gates(hq, hf, lb):
    sig = _sigmoid(hf)
    f = lb + (1.0 - lb) * sig
    sq = _sigmoid(hq)
    return hq * sq, 1.0 - f, f, jnp.log(f), sig, sq


def _lower_bound(lbl_ref):
    l0, l1 = lbl_ref[0:1, :], lbl_ref[1:2, :]
    mx = jnp.maximum(l0, l1)
    e0, e1 = jnp.exp(l0 - mx), jnp.exp(l1 - mx)
    return e0 / (e0 + e1)


def _split3(x):
    hi = x.astype(BF)
    r1 = x - hi.astype(F32)
    mid = r1.astype(BF)
    lo = (r1 - mid.astype(F32)).astype(BF)
    return hi, mid, lo


def _tri_mm(tri, x):
    hi, mid, lo = _split3(x)
    mm = lambda t: jnp.dot(tri, t, preferred_element_type=F32)
    return mm(hi) + mm(mid) + mm(lo)


def _intra_codes(sub):
    row = lax.broadcasted_iota(jnp.int32, (CHUNK, CHUNK), 0)
    col = lax.broadcasted_iota(jnp.int32, (CHUNK, CHUNK), 1)
    return sub, row, col


def _intra(q, k, b2, b_s, codes, da=None):
    grad = da is not None
    pow2 = (lambda x: jnp.exp2(jnp.minimum(x, 0.0))) if grad else jnp.exp2
    sub, row, col = codes
    a = jnp.zeros((CHUNK, CHUNK), F32)
    dq = jnp.zeros((CHUNK, G_DIM), F32)
    dk = jnp.zeros((CHUNK, G_DIM), F32)
    for i in range(1, CHUNK // sub):
        b0 = b_s[sub * i - 1:sub * i, :]
        eq, ek = pow2(b2 - b0), pow2(b0 - b2)
        mask = ((row // sub) == i) & (col < sub * i)
        if grad:
            dai = jnp.where(mask, da, 0.0)
            dq = dq + _dot(dai, k * ek) * eq
            dk = dk + _dot_tn(dai, q * eq) * ek
        else:
            a = jnp.where(mask, _dot_nt(q * eq, k * ek), a)
    for d in range(sub):
        ksh = pltpu.roll(k, d, 0) if d else k
        bsh = pltpu.roll(b2, d, 0) if d else b2
        e = pow2(b2 - bsh)
        mask = (col == row - d) & ((row & (sub - 1)) >= d)
        if grad:
            g = jnp.sum(jnp.where(mask, da, 0.0), -1, keepdims=True) * e
            dq = dq + g * ksh
            cb = g * q
            dk = dk + (pltpu.roll(cb, CHUNK - d, 0) if d else cb)
        else:
            a = jnp.where(mask, jnp.sum(q * ksh * e, -1, keepdims=True), a)
    return (dq, dk) if grad else a


def _hgrn_fwd(xp, lb_logits, g_hn, exchange=None):
    T = xp.shape[0]
    tb = min(_TB, T)
    ncb = tb // CHUNK
    hp = _HP
    W = hp * G_DIM

    def body(hq_ref, hf_ref, hi_ref, hg_ref, lbl_ref, ghn_ref, out_ref, oraw_ref, sall_ref, aall_ref, ball_ref,
             st_ref, b_s):
        lb_all = _lower_bound(lbl_ref)

        @pl.when(pl.program_id(1) == 0)
        def _():
            st_ref[...] = jnp.zeros(st_ref.shape, F32)

        row = lax.broadcasted_iota(jnp.int32, (CHUNK, CHUNK), 0)
        col = lax.broadcasted_iota(jnp.int32, (CHUNK, CHUNK), 1)
        tri = (col <= row).astype(BF)
        codes = _intra_codes(SUB)

        def chunk(c, carry):
            sl = pl.ds(pl.multiple_of(c * CHUNK, CHUNK), CHUNK)
            for h in range(hp):
                ls = slice(h * G_DIM, (h + 1) * G_DIM)
                q, k, _, lf, _, _ = _gates(hq_ref[sl, ls], hf_ref[sl, ls], lb_all[:, ls])
                v = hi_ref[sl, ls]
                b2 = _tri_mm(tri, lf) * LOG2E
                b_s[h] = b2
                ball_ref[sl, ls] = b2
                st = st_ref[h]
                sall_ref[c, h] = st
                a = _intra(q, k, b2, b_s.at[h], codes)
                aall_ref[c, h] = a
                o = _dot_nt(q * jnp.exp2(b2), st) + _dot(a, v)
                bl = b_s[h, CHUNK - 1:CHUNK, :]
                st_ref[h] = st * jnp.exp2(bl) + _dot_tn(v, k * jnp.exp2(bl - b2))
                oraw_ref[sl, ls] = o
                n, _ = _rms(o, G_DIM)
                hg = hg_ref[sl, ls]
                out_ref[sl, ls] = n * ghn_ref[:, ls] * (hg * _sigmoid(hg))
            return carry

        lax.fori_loop(0, ncb, chunk, 0, unroll=4)

    col_blk = lambda j: pl.BlockSpec((tb, W), lambda p, t: (t, j * (G_HEADS // hp) + p))
    head = pl.BlockSpec((tb, W), lambda p, t: (t, p))
    return _pcall(
        body, "hgrn_fwd", (G_HEADS // hp, T // tb),
        [col_blk(0), col_blk(1), col_blk(2), col_blk(3),
         pl.BlockSpec((2, W), lambda p, t: (0, p)), pl.BlockSpec((1, W), lambda p, t: (0, p))],
        [head, head, pl.BlockSpec((ncb, hp, G_DIM, G_DIM), lambda p, t: (t, p, 0, 0)),
         pl.BlockSpec((ncb, hp, CHUNK, CHUNK), lambda p, t: (t, p, 0, 0)), head],
        [_sds((T, G_W)), _sds((T, G_W)), _sds((T // CHUNK, G_HEADS, G_DIM, G_DIM)),
         _sds((T // CHUNK, G_HEADS, CHUNK, CHUNK)), _sds((T, G_W))],
        scratch=[pltpu.VMEM((hp, G_DIM, G_DIM), F32), pltpu.VMEM((hp, CHUNK, G_DIM), F32)], exchange=exchange,
    )(xp, xp, xp, xp, lb_logits, g_hn)


def _fwd_out(o_pad, o_hgrn, x, g_on, w_out, g_post, g_fpre, tm):
    T = x.shape[0]

    def body(o_ref, oh_ref, x_ref, gon_ref, w_ref, gpost_ref, gfpre_ref, h1_ref, y1_ref, z_ref, mix_ref):
        for h in range(N_HEADS):
            sl = slice(h * HB, (h + 1) * HB)
            n, _ = _rms(o_ref[:, sl].astype(F32), V_DIM)
            mix_ref[:, sl] = (n * gon_ref[:, sl]).astype(BF)
        mix_ref[:, N_HEADS * HB:MIX_W] = oh_ref[...].astype(BF)
        y1 = jnp.dot(mix_ref[...], w_ref[...], preferred_element_type=F32)
        y1_ref[...] = y1
        ny, _ = _rms(y1, D)
        h1 = x_ref[...] + ny * gpost_ref[...]
        h1_ref[...] = h1
        nh, _ = _rms(h1, D)
        z_ref[...] = (nh * gfpre_ref[...]).astype(BF)

    return _pcall(body, "fwd_out", (T // tm,),
                  [_rows(tm, N_HEADS * HB), _rows(tm, G_W), _rows(tm, D), _full((1, N_HEADS * HB)),
                   _full((MIX_W, D)), _full((1, D)), _full((1, D))],
                  [_rows(tm, D), _rows(tm, D), _rows(tm, D), _rows(tm, MIX_W)],
                  [_sds((T, D)), _sds((T, D)), _sds((T, D), BF), _sds((T, MIX_W), BF)],
                  )(o_pad, o_hgrn, x, g_on, w_out, g_post, g_fpre)


def _ffn_fwd(z, wgu, wd, h1, tgt, g_fpost, tm, nd):
    T = z.shape[0]
    fb = nd * FF_PAD
    nf = wd.shape[0] // fb

    def body(z_ref, wgu_ref, wd_ref, h1_ref, t_ref, gp_ref,
             as_ref, bs_ref, ff_ref, dh2_ref, dy2_ref, dgp_ref, loss_ref, acc):
        i, j = pl.program_id(0), pl.program_id(1)
        gu = _dot_nt(z_ref[...], wgu_ref[...])
        piece = lambda n: gu[:, n * FF_PAD:(n + 1) * FF_PAD]
        g = piece(0) if nd == 1 else jnp.concatenate([piece(2 * n) for n in range(nd)], 1)
        u = piece(1) if nd == 1 else jnp.concatenate([piece(2 * n + 1) for n in range(nd)], 1)
        s = _sigmoid(g)
        b = g * s
        ff = (b * u).astype(BF)
        as_ref[...] = (u * _dsilu(g, s)).astype(BF)
        bs_ref[...] = b.astype(BF)
        ff_ref[...] = ff
        part = jnp.dot(ff, wd_ref[...], preferred_element_type=F32)

        @pl.when(j == 0)
        def _():
            acc[...] = part

        @pl.when(j > 0)
        def _():
            acc[...] += part

        @pl.when((i == 0) & (j == 0))
        def _():
            dgp_ref[...] = jnp.zeros(dgp_ref.shape, F32)
            loss_ref[...] = jnp.zeros(loss_ref.shape, F32)

        @pl.when(j == nf - 1)
        def _():
            ny, r = _rms(acc[...], D)
            err = h1_ref[...] + ny * gp_ref[...] - t_ref[...]
            loss_ref[...] += 0.5 * jnp.sum(jnp.sum(err * err, -1, keepdims=True) * (1.0 / D), 0, keepdims=True)
            dh2 = err * (1.0 / D)
            dh2_ref[...] = dh2
            dy2, dgp = _rms_bwd(ny, r, gp_ref[...], dh2, D)
            dy2_ref[...] = dy2.astype(BF)
            dgp_ref[...] += dgp

    tok = lambda n: pl.BlockSpec((tm, n), lambda i, j: (i, 0))
    col = pl.BlockSpec((tm, fb), lambda i, j: (i, j))
    return _pcall(
        body, "ffn_fwd", (T // tm, nf),
        [tok(D), pl.BlockSpec((2 * fb, D), lambda i, j: (j, 0)), pl.BlockSpec((fb, D), lambda i, j: (j, 0)),
         tok(D), tok(D), _full((1, D))],
        [col, col, col, tok(D), tok(D), _full((1, D)), _full((1, HB))],
        [_sds((T, nf * fb), BF)] * 3 + [_sds((T, D)), _sds((T, D), BF), _sds((1, D)), _sds((1, HB))],
        scratch=[pltpu.VMEM((tm, D), F32)],
    )(z, wgu, wd, h1, tgt, g_fpost)


def _dsilu(x, s):
    return s * (1.0 + x * (1.0 - s))


def _ffn_bwd_x(dy2, gs, us, wgu, wd, h1, y1, dh2, g_fpre, g_post, tm):
    T = dy2.shape[0]
    nf = wd.shape[0] // _FB

    def body(dy2_ref, gs_ref, us_ref, wgu_ref, wd_ref, h1_ref, y1_ref, dh2_ref, gf_ref, gp_ref,
             dgu_ref, dh1_ref, dy1_ref, dgf_ref, dgp_ref, acc):
        i, j = pl.program_id(0), pl.program_id(1)
        dff = _dot_nt(dy2_ref[...], wd_ref[...])
        dg = (dff * gs_ref[...].astype(F32)).astype(BF)
        du = (dff * us_ref[...].astype(F32)).astype(BF)
        dgu = jnp.concatenate([dg[:, 0:FF_PAD], du[:, 0:FF_PAD], dg[:, FF_PAD:_FB], du[:, FF_PAD:_FB]], 1)
        dgu_ref[...] = dgu
        part = jnp.dot(dgu, wgu_ref[...], preferred_element_type=F32)

        @pl.when(j == 0)
        def _():
            acc[...] = part

        @pl.when(j > 0)
        def _():
            acc[...] += part

        @pl.when((i == 0) & (j == 0))
        def _():
            dgf_ref[...] = jnp.zeros(dgf_ref.shape, F32)
            dgp_ref[...] = jnp.zeros(dgp_ref.shape, F32)

        @pl.when(j == nf - 1)
        def _():
            nh, rh = _rms(h1_ref[...], D)
            dh, dgf = _rms_bwd(nh, rh, gf_ref[...], acc[...], D)
            dh1 = dh2_ref[...] + dh
            dh1_ref[...] = dh1
            dgf_ref[...] += dgf
            ny, ry = _rms(y1_ref[...], D)
            dy1, dgp = _rms_bwd(ny, ry, gp_ref[...], dh1, D)
            dy1_ref[...] = dy1.astype(BF)
            dgp_ref[...] += dgp

    tok = lambda n: pl.BlockSpec((tm, n), lambda i, j: (i, 0))
    col = pl.BlockSpec((tm, _FB), lambda i, j: (i, j))
    return _pcall(
        body, "ffn_bwd_x", (T // tm, nf),
        [tok(D), col, col, pl.BlockSpec((2 * _FB, D), lambda i, j: (j, 0)), pl.BlockSpec((_FB, D), lambda i, j: (j, 0)),
         tok(D), tok(D), tok(D), _full((1, D)), _full((1, D))],
        [pl.BlockSpec((tm, 2 * _FB), lambda i, j: (i, j)), tok(D), tok(D), _full((1, D)), _full((1, D))],
        [_sds((T, 2 * nf * _FB), BF), _sds((T, D)), _sds((T, D), BF), _sds((1, D)), _sds((1, D))],
        scratch=[pltpu.VMEM((tm, D), F32)],
    )(dy2, gs, us, wgu, wd, h1, y1, dh2, g_fpre, g_post)


def _ffn_bwd_w(z, ffs, dgu, dy2, tm):
    T = z.shape[0]
    nf = ffs.shape[1] // _FB
    nt = T // tm

    def body(z_ref, ff_ref, dgu_ref, dy2_ref, dwgu_ref, dwd_ref, agu, ad):
        i = pl.program_id(1)
        pgu = _dot_tn(dgu_ref[...], z_ref[...])
        pd = _dot_tn(ff_ref[...], dy2_ref[...])

        @pl.when(i == 0)
        def _():
            agu[...] = pgu
            ad[...] = pd

        @pl.when(i > 0)
        def _():
            agu[...] += pgu
            ad[...] += pd

        @pl.when(i == nt - 1)
        def _():
            dwgu_ref[...] = agu[...].astype(BF)
            dwd_ref[...] = ad[...].astype(BF)

    F = nf * _FB
    tok = lambda n: pl.BlockSpec((tm, n), lambda j, i: (i, 0))
    return _pcall(
        body, "ffn_bwd_w", (nf, nt),
        [tok(D), pl.BlockSpec((tm, _FB), lambda j, i: (i, j)), pl.BlockSpec((tm, 2 * _FB), lambda j, i: (i, j)), tok(D)],
        [pl.BlockSpec((2 * _FB, D), lambda j, i: (j, 0)), pl.BlockSpec((_FB, D), lambda j, i: (j, 0))],
        [_sds((2 * F, D), BF), _sds((F, D), BF)],
        scratch=[pltpu.VMEM((2 * _FB, D), F32), pltpu.VMEM((_FB, D), F32)],
    )(z, ffs, dgu, dy2)


def _out_bwd(dy1, mix, o_pad, w_out, g_on, tm):
    T = dy1.shape[0]
    W = N_HEADS * HB

    def body(dy1_ref, mix_ref, o_ref, w_ref, gon_ref, do_ref, dl_ref, dohg_ref, dw_ref, dgon_ref):
        i = pl.program_id(0)
        dy1v = dy1_ref[...]
        dmix = _dot_nt(dy1v, w_ref[...])
        pw = _dot_tn(mix_ref[...], dy1v)

        @pl.when(i == 0)
        def _():
            dw_ref[...] = pw
            dgon_ref[...] = jnp.zeros(dgon_ref.shape, F32)

        @pl.when(i > 0)
        def _():
            dw_ref[...] += pw

        for h in range(N_HEADS):
            sl = slice(h * HB, (h + 1) * HB)
            ov = o_ref[:, sl].astype(F32)
            n, r = _rms(ov, V_DIM)
            do, dg = _rms_bwd(n, r, gon_ref[:, sl], dmix[:, sl], V_DIM)
            dgon_ref[:, sl] += dg
            do_ref[:, sl] = do.astype(BF)
            dl_ref[:, sl] = jnp.broadcast_to(jnp.sum(do * ov, -1, keepdims=True), (tm, HB))
        dohg_ref[...] = dmix[:, W:MIX_W]

    return _pcall(body, "out_bwd", (T // tm,),
                  [_rows(tm, D), _rows(tm, MIX_W), _rows(tm, W), _full((MIX_W, D)), _full((1, W))],
                  [_rows(tm, W), _rows(tm, W), _rows(tm, G_W), _full((MIX_W, D)), _full((1, W))],
                  [_sds((T, W), BF), _sds((T, W)), _sds((T, G_W)), _sds((MIX_W, D)), _sds((1, W))],
                  )(dy1, mix, o_pad, w_out, g_on)


def _flash_bwd(q, k, v, do, lse, dl, tq, exchange=None):
    T = q.shape[0]
    nq = T // tq
    scale = QK_DIM ** -0.5
    hp = _AH
    W = hp * HB

    def body(k_ref, v_ref, q_ref, do_ref, lse_ref, dl_ref, dk_ref, dv_ref, dq_ref):
        j = pl.program_id(1)

        @pl.when(j == 0)
        def _():
            dq_ref[...] = jnp.zeros(dq_ref.shape, F32)

        def blk(i, carry, masked):
            sl = pl.ds(pl.multiple_of(i * tq, tq), tq)
            out = []
            for h in range(hp):
                ls = slice(h * HB, (h + 1) * HB)
                dk, dv = carry[h]
                kv, vv = k_ref[:, ls], v_ref[:, ls]
                qv, dov = q_ref[sl, ls], do_ref[sl, ls]
                s = _dot_nt(qv, kv) * scale
                if masked:
                    r = lax.broadcasted_iota(jnp.int32, (tq, tq), 0)
                    c = lax.broadcasted_iota(jnp.int32, (tq, tq), 1)
                    s = jnp.where(c <= r, s, NEG)
                p = jnp.exp(s - lse_ref[sl, h * HB:h * HB + 1])
                ds = p * (_dot_nt(dov, vv) - dl_ref[sl, h * HB:h * HB + 1]) * scale
                dq_ref[sl, ls] += _dot(ds, kv)
                out.append((dk + _dot_tn(ds, qv), dv + _dot_tn(p, dov)))
            return tuple(out)

        zero = jnp.zeros((tq, HB), F32)
        carry = blk(j, tuple((zero, zero) for _ in range(hp)), True)
        res = lax.fori_loop(j + 1, nq, lambda i, cr: blk(i, cr, False), carry)
        for h in range(hp):
            ls = slice(h * HB, (h + 1) * HB)
            dk_ref[:, ls] = res[h][0].astype(BF)
            dv_ref[:, ls] = res[h][1].astype(BF)

    tile = pl.BlockSpec((tq, W), lambda h, j: (j, h))
    whole = pl.BlockSpec((T, W), lambda h, j: (0, h))
    return _pcall(body, "flash_bwd", (N_HEADS // hp, nq), [tile, tile, whole, whole, whole, whole],
                  [tile, tile, whole], [_sds((T, N_HEADS * HB), BF)] * 2 + [_sds((T, N_HEADS * HB))],
                  exchange=exchange)(k, v, q, do, lse, dl)


def _mla_prep_bwd(xp, tabs, dq, dk, dv, g_q, g_kv, w_uq, w_uk, w_uv, tm):
    T = xp.shape[0]
    W = N_HEADS * HB

    def body(xp_ref, ta_ref, tb1_ref, tb2_ref, dq_ref, dk_ref, dv_ref, gq_ref, gkv_ref, wuq_ref, wuk_ref, wuv_ref,
             dxp_ref, dwuq_ref, dwuk_ref, dwuv_ref, dgq_ref, dgkv_ref, dqp):
        i = pl.program_id(0)
        ta, tb1, tb2 = ta_ref[...], tb1_ref[...], tb2_ref[...]
        nq, rq = _rms(xp_ref[:, 0:Q_RANK], Q_RANK)
        nkv, rkv = _rms(xp_ref[:, Q_RANK:Q_RANK + KV_RANK], KV_RANK)
        dkr = jnp.zeros((tm, HB), F32)
        for h in range(N_HEADS):
            sl = slice(h * HB, (h + 1) * HB)
            dqp[:, sl] = _unrope(dq_ref[:, sl], ta, tb1, tb2).astype(BF)
            dkr = dkr + dk_ref[:, sl].astype(F32)
        dkr = pltpu.roll(_unrope(dkr, ta, tb1, tb2), HB - NOPE, 1)
        lane = lax.broadcasted_iota(jnp.int32, (tm, HB), 1)
        dxp_ref[:, Q_RANK + KV_RANK:MLA_IN] = jnp.where(lane < ROPE, dkr, 0.0)
        dqpv = dqp[...]
        dkv, dvv = dk_ref[...], dv_ref[...]
        nqs = (nq * gq_ref[...]).astype(BF)
        nkvs = (nkv * gkv_ref[...]).astype(BF)
        pq, pk, pv = _dot_tn(nqs, dqpv), _dot_tn(nkvs, dkv), _dot_tn(nkvs, dvv)
        dcq, dgq = _rms_bwd(nq, rq, gq_ref[...], _dot_nt(dqpv, wuq_ref[...]), Q_RANK)
        dckv, dgkv = _rms_bwd(nkv, rkv, gkv_ref[...], _dot_nt(dkv, wuk_ref[...]) + _dot_nt(dvv, wuv_ref[...]), KV_RANK)
        dxp_ref[:, 0:Q_RANK] = dcq
        dxp_ref[:, Q_RANK:Q_RANK + KV_RANK] = dckv

        @pl.when(i == 0)
        def _():
            dwuq_ref[...] = pq
            dwuk_ref[...] = pk
            dwuv_ref[...] = pv
            dgq_ref[...] = dgq
            dgkv_ref[...] = dgkv

        @pl.when(i > 0)
        def _():
            dwuq_ref[...] += pq
            dwuk_ref[...] += pk
            dwuv_ref[...] += pv
            dgq_ref[...] += dgq
            dgkv_ref[...] += dgkv

    tab = _rows(tm, HB)
    return _pcall(
        body, "mla_prep_bwd", (T // tm,),
        [_rows(tm, MLA_IN), tab, tab, tab, _rows(tm, W), _rows(tm, W), _rows(tm, W), _full((1, Q_RANK)),
         _full((1, KV_RANK)), _full((Q_RANK, W)), _full((KV_RANK, W)), _full((KV_RANK, W))],
        [_rows(tm, MLA_IN), _full((Q_RANK, W)), _full((KV_RANK, W)), _full((KV_RANK, W)), _full((1, Q_RANK)),
         _full((1, KV_RANK))],
        [_sds((T, MLA_IN)), _sds((Q_RANK, W)), _sds((KV_RANK, W)), _sds((KV_RANK, W)), _sds((1, Q_RANK)),
         _sds((1, KV_RANK))],
        scratch=[pltpu.VMEM((tm, W), BF)],
    )(xp, *tabs, dq, dk, dv, g_q, g_kv, w_uq, w_uk, w_uv)


def _hgrn_bwd(xp, o_raw, s_all, a_all, b_all, d_out, lb_logits, g_hn, exchange=None):
    T = xp.shape[0]
    tb = min(_TB_BWD, T)
    ncb = tb // CHUNK
    nb = T // tb
    hp = _HP
    W = hp * G_DIM

    def body(hq_ref, hf_ref, hi_ref, hg_ref, o_ref, sall_ref, aall_ref, ball_ref, dout_ref, lbl_ref, ghn_ref,
             dhq_ref, dhf_ref, dhi_ref, dhg_ref, dlbl_ref, dghn_ref, dst_ref, b_s, acc_lb, acc_g):
        t = pl.program_id(1)
        lb_all = _lower_bound(lbl_ref)

        @pl.when(t == 0)
        def _():
            dst_ref[...] = jnp.zeros(dst_ref.shape, F32)
            acc_lb[...] = jnp.zeros(acc_lb.shape, F32)
            acc_g[...] = jnp.zeros(acc_g.shape, F32)

        row = lax.broadcasted_iota(jnp.int32, (CHUNK, CHUNK), 0)
        col = lax.broadcasted_iota(jnp.int32, (CHUNK, CHUNK), 1)
        tri_t = (col >= row).astype(BF)
        codes = _intra_codes(SUB)
        last = lax.broadcasted_iota(jnp.int32, (CHUNK, G_DIM), 0) == CHUNK - 1

        def chunk(cc, carry):
            c = ncb - 1 - cc
            sl = pl.ds(pl.multiple_of(c * CHUNK, CHUNK), CHUNK)
            for h in range(hp):
                ls = slice(h * G_DIM, (h + 1) * G_DIM)
                lb, ghn = lb_all[:, ls], ghn_ref[:, ls]
                hq, hg = hq_ref[sl, ls], hg_ref[sl, ls]
                q, k, f, _, sig, sq = _gates(hq, hf_ref[sl, ls], lb)
                v = hi_ref[sl, ls]
                b2 = ball_ref[sl, ls]
                b_s[h] = b2
                st = sall_ref[c, h]
                dstn = dst_ref[h]
                o = o_ref[sl, ls]
                dout = dout_ref[sl, ls]
                n, r = _rms(o, G_DIM)
                sg = _sigmoid(hg)
                dhg_ref[sl, ls] = dout * (n * ghn) * _dsilu(hg, sg)
                do, dg = _rms_bwd(n, r, ghn, dout * (hg * sg), G_DIM)
                acc_g[:, ls] += dg
                eb = jnp.exp2(b2)
                bl = b_s[h, CHUNK - 1:CHUNK, :]
                ebl = jnp.exp2(bl)
                ekd = jnp.exp2(bl - b2)
                kd = k * ekd
                a = aall_ref[c, h]
                dq_i, dk_i = _intra(q, k, b2, b_s.at[h], codes, _dot_nt(do, v))
                dhi_ref[sl, ls] = _dot_tn(a, do) + _dot_nt(kd, dstn)
                dk_state = _dot(v, dstn) * ekd
                dq = dq_i + _dot(do, st) * eb
                dk = dk_i + dk_state
                dbl = jnp.sum(k * dk_state, 0, keepdims=True) + ebl * jnp.sum(dstn * st, 0, keepdims=True)
                db = q * dq - k * dk + jnp.where(last, dbl, 0.0)
                df = _tri_mm(tri_t, db) / f - dk
                dhf_ref[sl, ls] = df * (1.0 - lb) * sig * (1.0 - sig)
                acc_lb[:, ls] += jnp.sum(df * (1.0 - sig), 0, keepdims=True)
                dhq_ref[sl, ls] = dq * _dsilu(hq, sq)
                dst_ref[h] = dstn * ebl + _dot_tn(do, q * eb)
            return carry

        lax.fori_loop(0, ncb, chunk, 0, unroll=4)

        @pl.when(t == nb - 1)
        def _():
            dl0 = acc_lb[...] * lb_all * (1.0 - lb_all)
            dlbl_ref[0:1, :] = dl0
            dlbl_ref[1:2, :] = -dl0
            dghn_ref[...] = acc_g[...]

    col_blk = lambda j: pl.BlockSpec((tb, W), lambda p, t: (nb - 1 - t, j * (G_HEADS // hp) + p))
    head = pl.BlockSpec((tb, W), lambda p, t: (nb - 1 - t, p))
    two = pl.BlockSpec((2, W), lambda p, t: (0, p))
    one = pl.BlockSpec((1, W), lambda p, t: (0, p))
    res = _pcall(
        body, "hgrn_bwd", (G_HEADS // hp, nb),
        [col_blk(0), col_blk(1), col_blk(2), col_blk(3), head,
         pl.BlockSpec((ncb, hp, G_DIM, G_DIM), lambda p, t: (nb - 1 - t, p, 0, 0)),
         pl.BlockSpec((ncb, hp, CHUNK, CHUNK), lambda p, t: (nb - 1 - t, p, 0, 0)), head, head, two, one],
        [head, head, head, head, two, one],
        [_sds((T, G_W))] * 4 + [_sds((2, G_W)), _sds((1, G_W))],
        scratch=[pltpu.VMEM((hp, G_DIM, G_DIM), F32), pltpu.VMEM((hp, CHUNK, G_DIM), F32),
                 pltpu.VMEM((1, W), F32), pltpu.VMEM((1, W), F32)], exchange=exchange,
    )(xp, xp, xp, xp, o_raw, s_all, a_all, b_all, d_out, lb_logits, g_hn)
    return res


def _in_bwd_x(x, dxp_m, dxp_h, dh1, w_in_al, g_pre, tm, exchange=None):
    T = x.shape[0]

    def body(x_ref, dm_ref, d0_ref, d1_ref, d2_ref, d3_ref, dh1_ref, w_ref, g_ref, dx_ref, dg_ref):
        i = pl.program_id(0)
        du = _dot_nt(dm_ref[...], w_ref[:, 0:MLA_IN])
        for j, d_ref in enumerate((d0_ref, d1_ref, d2_ref, d3_ref)):
            du = du + _dot_nt(d_ref[...], w_ref[:, MLA_IN + j * G_W:MLA_IN + (j + 1) * G_W])
        nx, r = _rms(x_ref[...], D)
        dx, dg = _rms_bwd(nx, r, g_ref[...], du, D)
        dx_ref[...] = dh1_ref[...] + dx

        @pl.when(i == 0)
        def _():
            dg_ref[...] = dg

        @pl.when(i > 0)
        def _():
            dg_ref[...] += dg

    return _pcall(body, "in_bwd_x", (T // tm,),
                  [_rows(tm, D), _rows(tm, MLA_IN)] + [_rows(tm, G_W)] * 4 + [_rows(tm, D), _full((D, XP_W)), _full((1, D))],
                  [_rows(tm, D), _full((1, D))], [_sds((T, D)), _sds((1, D))], exchange=exchange,
                  )(x, dxp_m, *dxp_h, dh1, w_in_al, g_pre)


def _aligned_col(c):
    return jnp.where(c < Q_RANK + KV_RANK + ROPE, c, c + (KR_PAD - ROPE))


def _align_w_in(g_in):
    tile = 384
    kr_end = Q_RANK + KV_RANK + ROPE

    def body(g_ref, o_ref, gp):
        gp[...] = jnp.zeros(gp.shape, BF)
        for j in range(N_DEV):
            gp[j, :, 0:IN_SH] = g_ref[j]
        r = lax.broadcasted_iota(jnp.int32, (tile, tile), 0)
        c = lax.broadcasted_iota(jnp.int32, (tile, tile), 1)
        for t in range(XP_W // tile):
            lo, hi = t * tile, (t + 1) * tile
            cols = [a if a < kr_end else a - (KR_PAD - ROPE) for a in (lo, hi - 1)]
            acc = jnp.zeros((D, tile), F32)
            for j in range(cols[0] // IN_SH, cols[-1] // IN_SH + 1):
                sel = (r < IN_SH) & (_aligned_col(j * IN_SH + r) == lo + c)
                acc = acc + jnp.dot(gp[j], sel.astype(BF), preferred_element_type=F32)
            o_ref[:, lo:hi] = acc.astype(BF)

    vm = pl.BlockSpec(memory_space=pltpu.VMEM)
    return pl.pallas_call(
        body, name="align_w_in", in_specs=[vm], out_specs=vm, out_shape=_sds((D, XP_W), BF),
        scratch_shapes=[pltpu.VMEM((N_DEV, D, tile), BF)],
        compiler_params=pltpu.CompilerParams(vmem_limit_bytes=_VMEM_LIMIT))(g_in)


def _in_bwd_w(name, u, dxp_m, dxp_h, tm, half, exchange=None):
    T = u.shape[0]
    nt = T // tm
    nr = D // 2
    win = 640

    def body(u_ref, dm_ref, d0_ref, d1_ref, d2_ref, d3_ref, o_ref, acc):
        i = pl.program_id(0)
        ut = u_ref[...].T
        parts = [(0, MLA_IN, dm_ref)] + [(MLA_IN + j * G_W, G_W, d) for j, d in enumerate((d0_ref, d1_ref, d2_ref, d3_ref))]

        @pl.when(i == 0)
        def _():
            for lo, n, d in parts:
                acc[:, lo:lo + n] = jnp.dot(ut, d[...].astype(BF), preferred_element_type=F32)

        @pl.when(i > 0)
        def _():
            for lo, n, d in parts:
                acc[:, lo:lo + n] += jnp.dot(ut, d[...].astype(BF), preferred_element_type=F32)

        @pl.when(i == nt - 1)
        def _():
            wide = 384
            r = lax.broadcasted_iota(jnp.int32, (win, wide), 0)
            c = lax.broadcasted_iota(jnp.int32, (win, wide), 1)
            kr_end = Q_RANK + KV_RANK + ROPE
            for j in range(N_DEV):
                first = j * IN_SH if j * IN_SH < kr_end else j * IN_SH + (KR_PAD - ROPE)
                lo = min(first // HB * HB, XP_W - win)
                sel = (c < IN_SH) & (_aligned_col(j * IN_SH + c) == lo + r)
                res = jnp.dot(acc[:, lo:lo + win].astype(BF), sel.astype(BF), preferred_element_type=F32)
                o_ref[j] = res[:, 0:IN_SH].astype(BF)

    return _pcall(body, name, (nt,),
                  [pl.BlockSpec((tm, nr), lambda i: (i, half)), _rows(tm, MLA_IN)] + [_rows(tm, G_W)] * 4,
                  [_full((N_DEV, nr, IN_SH))], [_sds((N_DEV, nr, IN_SH), BF)],
                  scratch=[pltpu.VMEM((nr, XP_W), F32)], exchange=exchange)(u, dxp_m, *dxp_h)


def _pad_heads(w, width, real):
    lead = w.shape[:-1]
    w = w.reshape(lead + (N_HEADS, real))
    w = jnp.pad(w, [(0, 0)] * len(lead) + [(0, 0), (0, width - real)])
    return w.reshape(lead + (N_HEADS * width,))


def _unpad_heads(w, width, real):
    lead = w.shape[:-1]
    return w.reshape(lead + (N_HEADS, width))[..., :real].reshape(lead + (N_HEADS * real,))


def _rope_tables(positions):
    half = ROPE // 2
    inv_freq = 1.0 / (ROPE_THETA ** (jnp.arange(0, ROPE, 2, dtype=F32) / ROPE))
    ang = positions.astype(F32)[:, None] * inv_freq
    cos, sin = jnp.cos(ang), jnp.sin(ang)
    T = positions.shape[0]
    z = lambda n: jnp.zeros((T, n), F32)
    ta = jnp.concatenate([jnp.ones((T, NOPE), F32), cos, cos, z(HB - QK_DIM)], 1)
    tb1 = jnp.concatenate([z(NOPE), -sin, z(half), z(HB - QK_DIM)], 1)
    tb2 = jnp.concatenate([z(NOPE), z(half), sin, z(HB - QK_DIM)], 1)
    return ta, tb1, tb2


def kernel(x, positions, attn_pre_norm, w_in, mla_q_norm, mla_w_uq, mla_kv_norm, mla_w_ukv, mla_out_norm, hgrn_lb_logits, hgrn_out_norm, w_out, attn_post_norm, ffn_pre_norm, w_gate, w_up, w_down, ffn_post_norm, loss_target, m_attn_pre_norm, m_w_in, m_mla_q_norm, m_mla_w_uq, m_mla_kv_norm, m_mla_w_ukv, m_mla_out_norm, m_hgrn_lb_logits, m_hgrn_out_norm, m_w_out, m_attn_post_norm, m_ffn_pre_norm, m_w_gate, m_w_up, m_w_down, m_ffn_post_norm, v_attn_pre_norm, v_w_in, v_mla_q_norm, v_mla_w_uq, v_mla_kv_norm, v_mla_w_ukv, v_mla_out_norm, v_hgrn_lb_logits, v_hgrn_out_norm, v_w_out, v_attn_post_norm, v_ffn_pre_norm, v_w_gate, v_w_up, v_w_down, v_ffn_post_norm):
    T = x.shape[1]
    tm = min(_TM, T)
    tq = min(_TQ, T)
    xs, tgt = x[0], loss_target[0]
    uq_sh = (Q_RANK // N_DEV, N_HEADS * QK_DIM)

    b_in, b_uq, b_out, b_gu, b_d = _cast_shards(
        w_in[0], mla_w_uq[0].reshape(uq_sh), w_out[0], w_gate[0].T, w_up[0].T, w_down[0])
    g_in, g_uq = _gather_two_level("ag_first", [b_in, b_uq])
    w_in_al = _align_w_in(g_in)
    w_uq_p = _pad_heads(g_uq.reshape(Q_RANK, N_HEADS * QK_DIM), HB, QK_DIM)
    w_ukv = mla_w_ukv[0].astype(BF)
    w_uk_p = _pad_heads(w_ukv[..., :NOPE].reshape(KV_RANK, N_HEADS * NOPE), HB, NOPE)
    w_uv_p = _pad_heads(w_ukv[..., NOPE:].reshape(KV_RANK, N_HEADS * V_DIM), HB, V_DIM)
    g_on_p = _pad_heads(mla_out_norm, HB, V_DIM)
    tabs = _rope_tables(positions[0])

    xp_m, xp_h, u = _fwd_in(xs, attn_pre_norm, w_in_al, tm)
    q_att, qs_att, k_att, v_att = _mla_prep(xp_m, tabs, mla_q_norm, mla_kv_norm, w_uq_p, w_uk_p, w_uv_p, tm)
    o_hgrn, o_raw, s_all, a_all, b_all, wd = _hgrn_fwd(xp_h, hgrn_lb_logits, hgrn_out_norm, ([GATHER], [b_d]))
    wd = wd.reshape(N_DEV * FF_PAD, D)
    o_pad, lse, wgu, g_out = _flash_fwd(qs_att, k_att, v_att, tq, ([GATHER, GATHER], [b_gu, b_out]))
    wgu = wgu.reshape(N_DEV * 2 * FF_PAD, D)
    w_out_full = g_out.reshape(D, D)
    w_out_mla = jnp.pad(w_out_full[:N_HEADS * V_DIM].reshape(N_HEADS, V_DIM, D), ((0, 0), (0, HB - V_DIM), (0, 0)))
    w_out_p = jnp.concatenate([w_out_mla.reshape(N_HEADS * HB, D), w_out_full[N_HEADS * V_DIM:]], 0)
    h1, y1, z, mix = _fwd_out(o_pad, o_hgrn, xs, g_on_p, w_out_p, attn_post_norm, ffn_pre_norm, tm)
    tmf = min(_TMF, T)
    gs, us, ffs, dh2, dy2, d_fpost, loss_row = _ffn_fwd(z, wgu, wd, h1, tgt, ffn_post_norm, tm, _FB // FF_PAD)

    dgu, dh1, dy1, d_fpre, d_post = _ffn_bwd_x(dy2, gs, us, wgu, wd, h1, y1, dh2, ffn_pre_norm, attn_post_norm, tm)
    dwgu, dwd = _ffn_bwd_w(z, ffs, dgu, dy2, tmf)
    do_pad, dl, d_ohg, dw_out_p, d_on_p = _out_bwd(dy1, mix, o_pad, w_out_p, g_on_p, tm)
    dw_out_mla = dw_out_p[:N_HEADS * HB].reshape(N_HEADS, HB, D)[:, :V_DIM].reshape(N_HEADS * V_DIM, D)
    dw_out = jnp.concatenate([dw_out_mla, dw_out_p[N_HEADS * HB:]], 0).reshape(N_DEV, D // N_DEV, D).astype(BF)
    dk_att, dv_att, dq_att, p_gu, p_d, p_out = _flash_bwd(
        q_att, k_att, v_att, do_pad, lse, dl, tq,
        ([SCATTER] * 3, [dwgu.reshape(N_DEV, 2 * FF_PAD, D), dwd.reshape(N_DEV, FF_PAD, D), dw_out]))
    dxp_m, dw_uq_p, dw_uk_p, dw_uv_p, d_gq, d_gkv = _mla_prep_bwd(
        xp_m, tabs, dq_att, dk_att, dv_att, mla_q_norm, mla_kv_norm, w_uq_p, w_uk_p, w_uv_p, tm)
    dw_uq = _unpad_heads(dw_uq_p, HB, QK_DIM).reshape((N_DEV,) + uq_sh).astype(BF)
    dw_ukv = jnp.concatenate([_unpad_heads(dw_uk_p, HB, NOPE).reshape(KV_RANK, N_HEADS, NOPE),
                              _unpad_heads(dw_uv_p, HB, V_DIM).reshape(KV_RANK, N_HEADS, V_DIM)], -1)
    *dxp_h, d_lbl, d_ghn, p_uq, dw_ukv_all = _hgrn_bwd(
        xp_h, o_raw, s_all, a_all, b_all, d_ohg, hgrn_lb_logits, hgrn_out_norm,
        ([SCATTER, GATHER], [dw_uq, dw_ukv.reshape(KV_RANK, N_HEADS * HB)]))
    dw_in_a, = _in_bwd_w("in_bwd_w_a", u, dxp_m, dxp_h, tm, 0)
    dw_in_b, p_in_a = _in_bwd_w("in_bwd_w_b", u, dxp_m, dxp_h, tm, 1, ([SCATTER], [dw_in_a]))
    grad_x, d_pre, p_in_b = _in_bwd_x(xs, dxp_m, dxp_h, dh1, w_in_al, attn_pre_norm, tm, ([SCATTER], [dw_in_b]))
    d_on = _unpad_heads(d_on_p, HB, V_DIM)

    ukv2 = lambda a: a.reshape(KV_RANK, N_HEADS * HB)
    vecs = [d_pre, d_gq, d_gkv, d_on, d_lbl, d_ghn, d_post, d_fpre, d_fpost, loss_row]
    small_w = [attn_pre_norm, mla_q_norm, mla_kv_norm, ukv2(mla_w_ukv), mla_out_norm, hgrn_lb_logits, hgrn_out_norm,
               attn_post_norm, ffn_pre_norm, ffn_post_norm]
    small_m = [m_attn_pre_norm, m_mla_q_norm, m_mla_kv_norm, ukv2(m_mla_w_ukv), m_mla_out_norm, m_hgrn_lb_logits,
               m_hgrn_out_norm, m_attn_post_norm, m_ffn_pre_norm, m_ffn_post_norm]
    small_v = [v_attn_pre_norm, v_mla_q_norm, v_mla_kv_norm, ukv2(v_mla_w_ukv), v_mla_out_norm, v_hgrn_lb_logits,
               v_hgrn_out_norm, v_attn_post_norm, v_ffn_pre_norm, v_ffn_post_norm]
    rall = _final_exchange(vecs)
    s_g, s_d, s_m, s_v, loss_all = _small_adam(rall, dw_ukv_all, 3, small_w, small_m, small_v)
    r_in = _shard_adam("adam_w_in", [p_in_a, p_in_b], w_in[0], m_w_in[0], v_w_in[0], 256)
    r_uq = _shard_adam("adam_w_uq", [p_uq], mla_w_uq[0].reshape(uq_sh), m_mla_w_uq[0].reshape(uq_sh),
                       v_mla_w_uq[0].reshape(uq_sh), uq_sh[0])
    r_out = _shard_adam("adam_w_out", [p_out], w_out[0], m_w_out[0], v_w_out[0], D // N_DEV)
    r_g, r_u = _gate_up_adam(p_gu, (w_gate[0].T, w_up[0].T), (m_w_gate[0].T, m_w_up[0].T),
                             (v_w_gate[0].T, v_w_up[0].T))
    r_g, r_u = [a.T for a in r_g], [a.T for a in r_u]
    r_d = _shard_adam("adam_w_down", [p_d], w_down[0], m_w_down[0], v_w_down[0], FF_SH // 2)

    loss = loss_all[0, 0]

    def assemble(big, small):
        b_in, b_uq, b_out, b_g, b_u, b_d = big
        return [small[0], b_in[None], small[1], b_uq.reshape(mla_w_uq.shape), small[2],
                small[3].reshape(mla_w_ukv.shape), small[4], small[5], small[6], b_out[None], small[7], small[8],
                b_g[None], b_u[None], b_d[None], small[9]]

    outs = [loss, grad_x[None]]
    for idx, small in enumerate((s_g, s_d, s_m, s_v)):
        outs += assemble([r[idx] for r in (r_in, r_uq, r_out, r_g, r_u, r_d)], small)
    return tuple(outs)
```

```python
import jax
import jax.numpy as jnp
from jax import lax
from jax.experimental import pallas as pl
from jax.experimental.pallas import tpu as pltpu

BF = jnp.bfloat16
F32 = jnp.float32
MESH = pl.DeviceIdType.MESH

N_DEV = 8
D = 1024
EPS = 1e-6
LOG2E = 1.4426950408889634
ROPE_THETA = 10000.0
N_HEADS = 8
HB = 128
NOPE = 64
ROPE = 32
V_DIM = 64
QK_DIM = NOPE + ROPE
Q_RANK = 384
KV_RANK = 128
KR_PAD = 128
MLA_IN = Q_RANK + KV_RANK + KR_PAD
G_HEADS = 4
G_DIM = 128
G_W = G_HEADS * G_DIM
CHUNK = 64
SUB = 16
XP_W = MLA_IN + 4 * G_W
IN_SH = 324
FF_SH = 352
FF_PAD = 384
MIX_W = N_HEADS * HB + G_W

ADAM_LR = 0.001
ADAM_B1 = 0.9
ADAM_B2 = 0.999
ADAM_EPS = 1e-08
ADAM_WD = 0.01
ADAM_STEP = 10

_TM = 512
_TMF = 1024
_IN_ROWS_A = 384
_TQ = 512
_AH = 2
_AH_FWD = 4
_FB = 768
_TB = 1024
_TB_BWD = 512
_HP = 4
V7X_VMEM_BYTES = 64 * 1024 * 1024
_VMEM_LIMIT = V7X_VMEM_BYTES - 8 * 1024 * 1024
NEG = -1e30


def _dot(a, b):
    return jnp.dot(a.astype(BF), b.astype(BF), preferred_element_type=F32)


def _dot_nt(a, b):
    return lax.dot_general(a.astype(BF), b.astype(BF), (((1,), (1,)), ((), ())), preferred_element_type=F32)


def _dot_tn(a, b):
    return lax.dot_general(a.astype(BF), b.astype(BF), (((0,), (0,)), ((), ())), preferred_element_type=F32)


def _sigmoid(x):
    return 1.0 / (1.0 + jnp.exp(-x))


def _rms(x, n):
    r = lax.rsqrt(jnp.sum(x * x, -1, keepdims=True) * (1.0 / n) + EPS)
    return x * r, r


def _rms_bwd(nx, r, g, dy, n):
    dg = jnp.sum(dy * nx, 0, keepdims=True)
    dn = dy * g
    dx = r * (dn - nx * (jnp.sum(dn * nx, -1, keepdims=True) * (1.0 / n)))
    return dx, dg


def _adamw(w, g, m, v):
    m2 = ADAM_B1 * m + (1.0 - ADAM_B1) * g
    v2 = ADAM_B2 * v + (1.0 - ADAM_B2) * (g * g)
    m_hat = m2 / (1.0 - ADAM_B1 ** ADAM_STEP)
    v_hat = v2 / (1.0 - ADAM_B2 ** ADAM_STEP)
    delta = -ADAM_LR * (m_hat / (jnp.sqrt(v_hat) + ADAM_EPS) + ADAM_WD * w)
    return delta, m2, v2


def _pcall(body, name, grid, in_specs, out_specs, out_shape, scratch=(), exchange=None):
    scratch = list(scratch)
    extra = ()
    if exchange is not None:
        kinds, extra = exchange
        in_specs, out_specs, out_shape = list(in_specs), list(out_specs), list(out_shape)
        n_in, n_out, n_scr, n_x = len(in_specs), len(out_specs), len(scratch), len(extra)
        inner = body

        def body(*refs):
            ins, rest = refs[:n_in], refs[n_in:]
            x_src, rest = rest[:n_x], rest[n_x:]
            outs, rest = rest[:n_out], rest[n_out:]
            x_dst, rest = rest[:n_x], rest[n_x:]
            ex = _Exchange(kinds, x_src, x_dst, *rest[n_scr:])
            first = pl.program_id(0) == 0
            last = pl.program_id(0) == grid[0] - 1
            for a in range(1, len(grid)):
                first = first & (pl.program_id(a) == 0)
                last = last & (pl.program_id(a) == grid[a] - 1)
            pl.when(first)(ex.start)
            inner(*ins, *outs, *rest[:n_scr])
            pl.when(last)(ex.wait)

        in_specs += [_HBM] * n_x
        out_specs += [_HBM] * n_x
        out_shape += _exchange_shapes(kinds, extra)
        scratch += _exchange_sems(n_x)
    call = pl.pallas_call(
        body, name=name, grid=grid, in_specs=in_specs, out_specs=out_specs, out_shape=out_shape,
        scratch_shapes=scratch,
        compiler_params=pltpu.CompilerParams(
            dimension_semantics=("arbitrary",) * len(grid), vmem_limit_bytes=_VMEM_LIMIT))
    return lambda *operands: call(*operands, *extra)


def _full(shape):
    return pl.BlockSpec(shape, lambda *_: (0,) * len(shape))


def _rows(tm, n):
    return pl.BlockSpec((tm, n), lambda i, *_: (i, 0))


def _sds(shape, dtype=F32):
    return jax.ShapeDtypeStruct(shape, dtype)


def _peer(k, x, y, c):
    px = 1 - x if (k >> 2) & 1 else x
    py = 1 - y if (k >> 1) & 1 else y
    pc = 1 - c if k & 1 else c
    return px, py, pc


GATHER, SCATTER = "gather", "scatter"


class _Exchange:
    def __init__(self, kinds, srcs, dsts, send_sems, recv_sems, loc_sems):
        self.kinds, self.srcs, self.dsts = kinds, srcs, dsts
        self.send_sems, self.recv_sems, self.loc_sems = send_sems, recv_sems, loc_sems
        self.x, self.y, self.c = lax.axis_index("x"), lax.axis_index("y"), lax.axis_index("c")
        self.me = 4 * self.x + 2 * self.y + self.c

    def _src(self, w, slot):
        return self.srcs[w] if self.kinds[w] == GATHER else self.srcs[w].at[slot]

    def _dst(self, w, slot):
        return self.dsts[w].at[slot]

    def _copy(self, w, k, outgoing):
        px, py, pc = _peer(k, self.x, self.y, self.c)
        pid = 4 * px + 2 * py + pc
        return pltpu.make_async_remote_copy(
            src_ref=self._src(w, pid if outgoing else self.me),
            dst_ref=self._dst(w, self.me if outgoing else pid),
            send_sem=self.send_sems.at[w, k - 1], recv_sem=self.recv_sems.at[w, k - 1],
            device_id=(px, py, pc), device_id_type=MESH)

    def _local(self, w):
        return pltpu.make_async_copy(self._src(w, self.me), self._dst(w, self.me), self.loc_sems.at[w])

    def start(self):
        for w in range(len(self.srcs)):
            self._local(w).start()
            for k in range(1, N_DEV):
                self._copy(w, k, True).start()

    def wait(self):
        for w in range(len(self.srcs)):
            self._local(w).wait()
            for k in range(1, N_DEV):
                self._copy(w, k, False).wait_recv()
        for w in range(len(self.srcs)):
            for k in range(1, N_DEV):
                self._copy(w, k, True).wait_send()


def _exchange_sems(n_w):
    return [pltpu.SemaphoreType.DMA((n_w, N_DEV - 1)), pltpu.SemaphoreType.DMA((n_w, N_DEV - 1)),
            pltpu.SemaphoreType.DMA((n_w,))]


def _exchange_shapes(kinds, srcs):
    return [_sds(((N_DEV,) if kd == GATHER else ()) + tuple(s.shape), s.dtype) for kd, s in zip(kinds, srcs)]


_HBM = pl.BlockSpec(memory_space=pl.ANY)


def _cast_shards(w_in, w_uq, w_out, w_gate_t, w_up_t, w_down):
    shapes = [(D, IN_SH), (Q_RANK // N_DEV, N_HEADS * QK_DIM), (D // N_DEV, D), (2 * FF_PAD, D), (FF_PAD, D)]

    def body(win, wuq, wout, wg, wu, wd, sin_, suq, sout, sgu, sd):
        sin_[...] = win[...].astype(BF)
        suq[...] = wuq[...].astype(BF)
        sout[...] = wout[...].astype(BF)
        sgu[...] = jnp.zeros(sgu.shape, BF)
        sgu[0:FF_SH, :] = wg[...].astype(BF)
        sgu[FF_PAD:FF_PAD + FF_SH, :] = wu[...].astype(BF)
        sd[...] = jnp.zeros(sd.shape, BF)
        sd[0:FF_SH, :] = wd[...].astype(BF)

    vm = pl.BlockSpec(memory_space=pltpu.VMEM)
    return pl.pallas_call(
        body, name="cast_shards", in_specs=[vm] * 6, out_specs=[vm] * 5,
        out_shape=[_sds(s, BF) for s in shapes],
        compiler_params=pltpu.CompilerParams(vmem_limit_bytes=_VMEM_LIMIT),
    )(w_in, w_uq, w_out, w_gate_t, w_up_t, w_down)


def _gather_two_level(name, srcs):
    n_w = len(srcs)

    def body(*refs):
        src, dst = refs[:n_w], refs[n_w:2 * n_w]
        send_sems, recv_sems, loc_sems = refs[2 * n_w:]
        x, y, c = lax.axis_index("x"), lax.axis_index("y"), lax.axis_index("c")
        me, sibling = (x, y, c), (x, y, 1 - c)
        chips = [(1 - x, y), (x, 1 - y), (1 - x, 1 - y)]
        slot = lambda p: 4 * p[0] + 2 * p[1] + p[2]

        def copy(w, k, block, to, own=False):
            return pltpu.make_async_remote_copy(
                src_ref=src[w] if own else dst[w].at[slot(block)], dst_ref=dst[w].at[slot(block)],
                send_sem=send_sems.at[w, k], recv_sem=recv_sems.at[w, k], device_id=to, device_id_type=MESH)

        local = [pltpu.make_async_copy(src[w], dst[w].at[slot(me)], loc_sems.at[w]) for w in range(n_w)]
        first, passed = [], []
        for w in range(n_w):
            local[w].start()
            first.append(copy(w, 0, me, sibling, own=True))
            first += [copy(w, 1 + j, me, (*chip, c), own=True) for j, chip in enumerate(chips)]
        for cp in first:
            cp.start()
        for w in range(n_w):
            for j, chip in enumerate(chips):
                copy(w, 1 + j, (*chip, c), me).wait_recv()
                passed.append(copy(w, 4 + j, (*chip, c), sibling))
                passed[-1].start()
        for w in range(n_w):
            copy(w, 0, sibling, me).wait_recv()
            for j, chip in enumerate(chips):
                copy(w, 4 + j, (*chip, 1 - c), me).wait_recv()
        for cp in first + passed:
            cp.wait_send()
        for w in range(n_w):
            local[w].wait()

    return pl.pallas_call(
        body, name=name, in_specs=[_HBM] * n_w, out_specs=[_HBM] * n_w,
        out_shape=_exchange_shapes([GATHER] * n_w, srcs), scratch_shapes=_exchange_sems(n_w))(*srcs)


def _row_offsets(arrays):
    offs, rows = [], 0
    for a in arrays:
        offs.append(rows)
        rows += a.shape[0]
    return offs, -(-rows // 8) * 8


def _final_exchange(vecs):
    n_p = len(vecs)
    offs, rows = _row_offsets(vecs)

    def body(*refs):
        g_refs = refs[:n_p]
        rall, pk, send_sems, recv_sems, loc_sem = refs[n_p:]
        x, y, c = lax.axis_index("x"), lax.axis_index("y"), lax.axis_index("c")
        me = 4 * x + 2 * y + c
        pk[...] = jnp.zeros(pk.shape, F32)
        for p in range(n_p):
            r, n = g_refs[p].shape
            pk[offs[p]:offs[p] + r, 0:n] = g_refs[p][...]

        def remote(k):
            return pltpu.make_async_remote_copy(
                src_ref=pk, dst_ref=rall.at[me], send_sem=send_sems.at[k - 1], recv_sem=recv_sems.at[k - 1],
                device_id=_peer(k, x, y, c), device_id_type=MESH)

        def arrival(k):
            px, py, pc = _peer(k, x, y, c)
            return pltpu.make_async_remote_copy(
                src_ref=pk, dst_ref=rall.at[4 * px + 2 * py + pc], send_sem=send_sems.at[k - 1],
                recv_sem=recv_sems.at[k - 1], device_id=(px, py, pc), device_id_type=MESH)

        local = pltpu.make_async_copy(pk, rall.at[me], loc_sem)
        local.start()
        for k in range(1, N_DEV):
            remote(k).start()
        local.wait()
        for k in range(1, N_DEV):
            arrival(k).wait_recv()
        for k in range(1, N_DEV):
            remote(k).wait_send()

    vm = pl.BlockSpec(memory_space=pltpu.VMEM)
    return pl.pallas_call(
        body, name="final_exchange", in_specs=[vm] * n_p, out_specs=vm, out_shape=_sds((N_DEV, rows, D)),
        scratch_shapes=[pltpu.VMEM((rows, D), F32),
                        pltpu.SemaphoreType.DMA((N_DEV - 1,)), pltpu.SemaphoreType.DMA((N_DEV - 1,)),
                        pltpu.SemaphoreType.DMA],
    )(*vecs)


def _small_adam(rall, big_parts, big, ws, ms, vs):
    n_p = len(ws)
    packed = [w for p, w in enumerate(ws) if p != big] + [jax.ShapeDtypeStruct((1, HB), F32)]
    offs, _ = _row_offsets(packed)
    offs = offs[:big] + [None] + offs[big:]

    def total(ref, sl):
        g = ref[(0,) + sl]
        for j in range(1, N_DEV):
            g = g + ref[(j,) + sl]
        return g

    def body(*refs):
        rall_ref, big_ref = refs[:2]
        w_refs, m_refs, v_refs = refs[2:2 + n_p], refs[2 + n_p:2 + 2 * n_p], refs[2 + 2 * n_p:2 + 3 * n_p]
        outs = refs[2 + 3 * n_p:]
        for p in range(n_p):
            r, n = w_refs[p].shape
            if p == big:
                g = total(big_ref, (slice(0, r), slice(0, n)))
            else:
                g = total(rall_ref, (slice(offs[p], offs[p] + r), slice(0, n)))
            delta, m2, v2 = _adamw(w_refs[p][...], g, m_refs[p][...], v_refs[p][...])
            outs[p][...] = g
            outs[n_p + p][...] = delta
            outs[2 * n_p + p][...] = m2
            outs[3 * n_p + p][...] = v2
        outs[4 * n_p][...] = total(rall_ref, (slice(offs[n_p], offs[n_p] + 1), slice(0, HB)))

    vm = pl.BlockSpec(memory_space=pltpu.VMEM)
    res = pl.pallas_call(
        body, name="small_adam", in_specs=[vm] * (2 + 3 * n_p), out_specs=[vm] * (4 * n_p + 1),
        out_shape=[_sds(w.shape) for w in ws] * 4 + [_sds((1, HB))],
        compiler_params=pltpu.CompilerParams(vmem_limit_bytes=_VMEM_LIMIT),
    )(rall, big_parts, *ws, *ms, *vs)
    return res[:n_p], res[n_p:2 * n_p], res[2 * n_p:3 * n_p], res[3 * n_p:4 * n_p], res[4 * n_p]


def _device_sum(p_ref):
    g = p_ref[0].astype(F32)
    for j in range(1, N_DEV):
        g = g + p_ref[j].astype(F32)
    return g


def _shard_adam(name, parts, w, m, v, tr):
    a0, b0 = w.shape
    n_p = len(parts)
    b = parts[0].shape[2]
    first = [0]
    for p in parts:
        first.append(first[-1] + p.shape[1] // tr)

    def body(*refs):
        p_refs = refs[:n_p]
        w_ref, m_ref, v_ref, g_out, d_out, m_out, v_out = refs[n_p:]
        i = pl.program_id(0)
        g = _device_sum(p_refs[0])
        for k in range(1, n_p):
            g = jnp.where(i >= first[k], _device_sum(p_refs[k]), g)
        g = g[:, 0:b0]
        delta, m2, v2 = _adamw(w_ref[...], g, m_ref[...], v_ref[...])
        g_out[...] = g
        d_out[...] = delta
        m_out[...] = m2
        v_out[...] = v2

    def part_spec(k):
        last = first[k + 1] - first[k] - 1
        return pl.BlockSpec((N_DEV, tr, b), lambda i: (0, jnp.minimum(jnp.maximum(i - first[k], 0), last), 0))

    blk = pl.BlockSpec((tr, b0), lambda i: (i, 0))
    return _pcall(
        body, name, (a0 // tr,), [part_spec(k) for k in range(n_p)] + [blk, blk, blk],
        [blk] * 4, [_sds((a0, b0))] * 4)(*parts, w, m, v)


def _gate_up_adam(parts, ws, ms, vs):
    tc = 256

    def body(p_ref, wg, wu, mg, mu, vg, vu, *outs):
        g = _device_sum(p_ref)
        for k, (w_ref, m_ref, v_ref) in enumerate(((wg, mg, vg), (wu, mu, vu))):
            gk = g[k * FF_PAD:k * FF_PAD + FF_SH]
            delta, m2, v2 = _adamw(w_ref[...], gk, m_ref[...], v_ref[...])
            for o, val in zip(outs[4 * k:4 * k + 4], (gk, delta, m2, v2)):
                o[...] = val

    blk = pl.BlockSpec((FF_SH, tc), lambda i: (0, i))
    res = _pcall(
        body, "adam_w_gate_up", (D // tc,), [pl.BlockSpec((N_DEV, 2 * FF_PAD, tc), lambda i: (0, 0, i))] + [blk] * 6,
        [blk] * 8, [_sds((FF_SH, D))] * 8)(parts, *ws, *ms, *vs)
    return res[:4], res[4:]


def _fwd_in(x, g_pre, w_in_al, tm):
    T = x.shape[0]

    def body(x_ref, g_ref, w_ref, xm_ref, xh_ref, u_ref):
        nx, _ = _rms(x_ref[...], D)
        u = (nx * g_ref[...]).astype(BF)
        u_ref[...] = u
        xm_ref[...] = jnp.dot(u, w_ref[:, 0:MLA_IN], preferred_element_type=F32)
        xh_ref[...] = jnp.dot(u, w_ref[:, MLA_IN:XP_W], preferred_element_type=F32)

    return _pcall(body, "fwd_in", (T // tm,),
                  [_rows(tm, D), _full((1, D)), _full((D, XP_W))],
                  [_rows(tm, MLA_IN), _rows(tm, 4 * G_W), _rows(tm, D)],
                  [_sds((T, MLA_IN)), _sds((T, 4 * G_W)), _sds((T, D), BF)])(x, g_pre, w_in_al)


def _rope(blk, ta, tb1, tb2):
    return blk * ta + pltpu.roll(blk, HB - ROPE // 2, 1) * tb1 + pltpu.roll(blk, ROPE // 2, 1) * tb2


def _unrope(d, ta, tb1, tb2):
    return d * ta + pltpu.roll(d * tb1, ROPE // 2, 1) + pltpu.roll(d * tb2, HB - ROPE // 2, 1)


def _mla_prep(xp, tabs, g_q, g_kv, w_uq, w_uk, w_uv, tm):
    T = xp.shape[0]
    W = N_HEADS * HB

    def body(xp_ref, ta_ref, tb1_ref, tb2_ref, gq_ref, gkv_ref, wuq_ref, wuk_ref, wuv_ref, q_ref, qs_ref, k_ref, v_ref):
        ta, tb1, tb2 = ta_ref[...], tb1_ref[...], tb2_ref[...]
        nq, _ = _rms(xp_ref[:, 0:Q_RANK], Q_RANK)
        nkv, _ = _rms(xp_ref[:, Q_RANK:Q_RANK + KV_RANK], KV_RANK)
        nkv = (nkv * gkv_ref[...]).astype(BF)
        qpre = _dot(nq * gq_ref[...], wuq_ref[...])
        kpre = jnp.dot(nkv, wuk_ref[...], preferred_element_type=F32)
        v = jnp.dot(nkv, wuv_ref[...], preferred_element_type=F32)
        lane = lax.broadcasted_iota(jnp.int32, (tm, W), 1)
        v_ref[...] = jnp.where((lane & (HB - 1)) == V_DIM, 1.0, v).astype(BF)
        kr = _rope(pltpu.roll(xp_ref[:, Q_RANK + KV_RANK:MLA_IN], NOPE, 1), ta, tb1, tb2)
        for h in range(N_HEADS):
            sl = slice(h * HB, (h + 1) * HB)
            qr = _rope(qpre[:, sl], ta, tb1, tb2)
            q_ref[:, sl] = qr.astype(BF)
            qs_ref[:, sl] = (qr * (QK_DIM ** -0.5 * LOG2E)).astype(BF)
            k_ref[:, sl] = (kpre[:, sl] + kr).astype(BF)

    tab = _rows(tm, HB)
    return _pcall(body, "mla_prep", (T // tm,),
                  [_rows(tm, MLA_IN), tab, tab, tab, _full((1, Q_RANK)), _full((1, KV_RANK)),
                   _full((Q_RANK, W)), _full((KV_RANK, W)), _full((KV_RANK, W))],
                  [_rows(tm, W)] * 4, [_sds((T, W), BF)] * 4)(xp, *tabs, g_q, g_kv, w_uq, w_uk, w_uv)


def _flash_fwd(q, k, v, tq, exchange=None):
    T = q.shape[0]
    hp = _AH_FWD
    W = hp * HB

    def body(q_ref, k_ref, v_ref, o_ref, lse_ref):
        i = pl.program_id(1)

        def blk(j, carry, masked):
            st = pl.multiple_of(j * tq, tq)
            out = []
            for h in range(hp):
                ls = slice(h * HB, (h + 1) * HB)
                m, acc = carry[h]
                s = _dot_nt(q_ref[:, ls], k_ref[pl.ds(st, tq), ls])
                if masked:
                    r = lax.broadcasted_iota(jnp.int32, (tq, tq), 0)
                    c = lax.broadcasted_iota(jnp.int32, (tq, tq), 1)
                    s = jnp.where(c <= r, s, NEG)
                m2 = jnp.maximum(m, jnp.max(s, -1, keepdims=True))
                p = jnp.exp2(s - m2)
                out.append((m2, jnp.exp2(m - m2) * acc + _dot(p, v_ref[pl.ds(st, tq), ls])))
            return tuple(out)

        init = tuple((jnp.full((tq, 1), NEG, F32), jnp.zeros((tq, HB), F32)) for _ in range(hp))
        carry = lax.fori_loop(0, i, lambda j, cr: blk(j, cr, False), init)
        res = blk(i, carry, True)
        lane = lax.broadcasted_iota(jnp.int32, (tq, HB), 1)
        for h in range(hp):
            ls = slice(h * HB, (h + 1) * HB)
            m, acc = res[h]
            l = acc[:, V_DIM:V_DIM + 1]
            o_ref[:, ls] = jnp.where(lane < V_DIM, acc / l, 0.0).astype(BF)
            lse_ref[:, ls] = jnp.broadcast_to(m * (1.0 / LOG2E) + jnp.log(l), (tq, HB))

    qs = pl.BlockSpec((tq, W), lambda h, i: (i, h))
    kvs = pl.BlockSpec((T, W), lambda h, i: (0, h))
    return _pcall(body, "flash_fwd", (N_HEADS // hp, T // tq), [qs, kvs, kvs], [qs, qs],
                  [_sds((T, N_HEADS * HB), BF), _sds((T, N_HEADS * HB))], exchange=exchange)(q, k, v)


def _gates(hq, hf, lb):
    sig = _sigmoid(hf)
    f = lb + (1.0 - lb) * sig
    sq = _sigmoid(hq)
    return hq * sq, 1.0 - f, f, jnp.log(f), sig, sq


def _lower_bound(lbl_ref):
    l0, l1 = lbl_ref[0:1, :], lbl_ref[1:2, :]
    mx = jnp.maximum(l0, l1)
    e0, e1 = jnp.exp(l0 - mx), jnp.exp(l1 - mx)
    return e0 / (e0 + e1)


def _split3(x):
    hi = x.astype(BF)
    r1 = x - hi.astype(F32)
    mid = r1.astype(BF)
    lo = (r1 - mid.astype(F32)).astype(BF)
    return hi, mid, lo


def _tri_mm(tri, x):
    hi, mid, lo = _split3(x)
    mm = lambda t: jnp.dot(tri, t, preferred_element_type=F32)
    return mm(hi) + mm(mid) + mm(lo)


def _intra_codes(sub):
    row = lax.broadcasted_iota(jnp.int32, (CHUNK, CHUNK), 0)
    col = lax.broadcasted_iota(jnp.int32, (CHUNK, CHUNK), 1)
    return sub, row, col


def _intra(q, k, b2, b_s, codes, da=None):
    grad = da is not None
    pow2 = (lambda x: jnp.exp2(jnp.minimum(x, 0.0))) if grad else jnp.exp2
    sub, row, col = codes
    a = jnp.zeros((CHUNK, CHUNK), F32)
    dq = jnp.zeros((CHUNK, G_DIM), F32)
    dk = jnp.zeros((CHUNK, G_DIM), F32)
    for i in range(1, CHUNK // sub):
        b0 = b_s[sub * i - 1:sub * i, :]
        eq, ek = pow2(b2 - b0), pow2(b0 - b2)
        mask = ((row // sub) == i) & (col < sub * i)
        if grad:
            dai = jnp.where(mask, da, 0.0)
            dq = dq + _dot(dai, k * ek) * eq
            dk = dk + _dot_tn(dai, q * eq) * ek
        else:
            a = jnp.where(mask, _dot_nt(q * eq, k * ek), a)
    for d in range(sub):
        ksh = pltpu.roll(k, d, 0) if d else k
        bsh = pltpu.roll(b2, d, 0) if d else b2
        e = pow2(b2 - bsh)
        mask = (col == row - d) & ((row & (sub - 1)) >= d)
        if grad:
            g = jnp.sum(jnp.where(mask, da, 0.0), -1, keepdims=True) * e
            dq = dq + g * ksh
            cb = g * q
            dk = dk + (pltpu.roll(cb, CHUNK - d, 0) if d else cb)
        else:
            a = jnp.where(mask, jnp.sum(q * ksh * e, -1, keepdims=True), a)
    return (dq, dk) if grad else a


def _hgrn_fwd(xp, lb_logits, g_hn, exchange=None):
    T = xp.shape[0]
    tb = min(_TB, T)
    ncb = tb // CHUNK
    hp = _HP
    W = hp * G_DIM

    def body(hq_ref, hf_ref, hi_ref, hg_ref, lbl_ref, ghn_ref, out_ref, oraw_ref, sall_ref, aall_ref, ball_ref,
             st_ref, b_s):
        lb_all = _lower_bound(lbl_ref)

        @pl.when(pl.program_id(1) == 0)
        def _():
            st_ref[...] = jnp.zeros(st_ref.shape, F32)

        row = lax.broadcasted_iota(jnp.int32, (CHUNK, CHUNK), 0)
        col = lax.broadcasted_iota(jnp.int32, (CHUNK, CHUNK), 1)
        tri = (col <= row).astype(BF)
        codes = _intra_codes(SUB)

        def chunk(c, carry):
            sl = pl.ds(pl.multiple_of(c * CHUNK, CHUNK), CHUNK)
            for h in range(hp):
                ls = slice(h * G_DIM, (h + 1) * G_DIM)
                q, k, _, lf, _, _ = _gates(hq_ref[sl, ls], hf_ref[sl, ls], lb_all[:, ls])
                v = hi_ref[sl, ls]
                b2 = _tri_mm(tri, lf) * LOG2E
                b_s[h] = b2
                ball_ref[sl, ls] = b2
                st = st_ref[h]
                sall_ref[c, h] = st
                a = _intra(q, k, b2, b_s.at[h], codes)
                aall_ref[c, h] = a
                o = _dot_nt(q * jnp.exp2(b2), st) + _dot(a, v)
                bl = b_s[h, CHUNK - 1:CHUNK, :]
                st_ref[h] = st * jnp.exp2(bl) + _dot_tn(v, k * jnp.exp2(bl - b2))
                oraw_ref[sl, ls] = o
                n, _ = _rms(o, G_DIM)
                hg = hg_ref[sl, ls]
                out_ref[sl, ls] = n * ghn_ref[:, ls] * (hg * _sigmoid(hg))
            return carry

        lax.fori_loop(0, ncb, chunk, 0, unroll=4)

    col_blk = lambda j: pl.BlockSpec((tb, W), lambda p, t: (t, j * (G_HEADS // hp) + p))
    head = pl.BlockSpec((tb, W), lambda p, t: (t, p))
    return _pcall(
        body, "hgrn_fwd", (G_HEADS // hp, T // tb),
        [col_blk(0), col_blk(1), col_blk(2), col_blk(3),
         pl.BlockSpec((2, W), lambda p, t: (0, p)), pl.BlockSpec((1, W), lambda p, t: (0, p))],
        [head, head, pl.BlockSpec((ncb, hp, G_DIM, G_DIM), lambda p, t: (t, p, 0, 0)),
         pl.BlockSpec((ncb, hp, CHUNK, CHUNK), lambda p, t: (t, p, 0, 0)), head],
        [_sds((T, G_W)), _sds((T, G_W)), _sds((T // CHUNK, G_HEADS, G_DIM, G_DIM)),
         _sds((T // CHUNK, G_HEADS, CHUNK, CHUNK)), _sds((T, G_W))],
        scratch=[pltpu.VMEM((hp, G_DIM, G_DIM), F32), pltpu.VMEM((hp, CHUNK, G_DIM), F32)], exchange=exchange,
    )(xp, xp, xp, xp, lb_logits, g_hn)


def _fwd_out(o_pad, o_hgrn, x, g_on, w_out, g_post, g_fpre, tm):
    T = x.shape[0]

    def body(o_ref, oh_ref, x_ref, gon_ref, w_ref, gpost_ref, gfpre_ref, h1_ref, y1_ref, z_ref, mix_ref):
        for h in range(N_HEADS):
            sl = slice(h * HB, (h + 1) * HB)
            n, _ = _rms(o_ref[:, sl].astype(F32), V_DIM)
            mix_ref[:, sl] = (n * gon_ref[:, sl]).astype(BF)
        mix_ref[:, N_HEADS * HB:MIX_W] = oh_ref[...].astype(BF)
        y1 = jnp.dot(mix_ref[...], w_ref[...], preferred_element_type=F32)
        y1_ref[...] = y1
        ny, _ = _rms(y1, D)
        h1 = x_ref[...] + ny * gpost_ref[...]
        h1_ref[...] = h1
        nh, _ = _rms(h1, D)
        z_ref[...] = (nh * gfpre_ref[...]).astype(BF)

    return _pcall(body, "fwd_out", (T // tm,),
                  [_rows(tm, N_HEADS * HB), _rows(tm, G_W), _rows(tm, D), _full((1, N_HEADS * HB)),
                   _full((MIX_W, D)), _full((1, D)), _full((1, D))],
                  [_rows(tm, D), _rows(tm, D), _rows(tm, D), _rows(tm, MIX_W)],
                  [_sds((T, D)), _sds((T, D)), _sds((T, D), BF), _sds((T, MIX_W), BF)],
                  )(o_pad, o_hgrn, x, g_on, w_out, g_post, g_fpre)


def _ffn_fwd(z, wgu, wd, h1, tgt, g_fpost, tm, nd):
    T = z.shape[0]
    fb = nd * FF_PAD
    nf = wd.shape[0] // fb

    def body(z_ref, wgu_ref, wd_ref, h1_ref, t_ref, gp_ref,
             as_ref, bs_ref, ff_ref, dh2_ref, dy2_ref, dgp_ref, loss_ref, acc):
        i, j = pl.program_id(0), pl.program_id(1)
        gu = _dot_nt(z_ref[...], wgu_ref[...])
        piece = lambda n: gu[:, n * FF_PAD:(n + 1) * FF_PAD]
        g = piece(0) if nd == 1 else jnp.concatenate([piece(2 * n) for n in range(nd)], 1)
        u = piece(1) if nd == 1 else jnp.concatenate([piece(2 * n + 1) for n in range(nd)], 1)
        s = _sigmoid(g)
        b = g * s
        ff = (b * u).astype(BF)
        as_ref[...] = (u * _dsilu(g, s)).astype(BF)
        bs_ref[...] = b.astype(BF)
        ff_ref[...] = ff
        part = jnp.dot(ff, wd_ref[...], preferred_element_type=F32)

        @pl.when(j == 0)
        def _():
            acc[...] = part

        @pl.when(j > 0)
        def _():
            acc[...] += part

        @pl.when((i == 0) & (j == 0))
        def _():
            dgp_ref[...] = jnp.zeros(dgp_ref.shape, F32)
            loss_ref[...] = jnp.zeros(loss_ref.shape, F32)

        @pl.when(j == nf - 1)
        def _():
            ny, r = _rms(acc[...], D)
            err = h1_ref[...] + ny * gp_ref[...] - t_ref[...]
            loss_ref[...] += 0.5 * jnp.sum(jnp.sum(err * err, -1, keepdims=True) * (1.0 / D), 0, keepdims=True)
            dh2 = err * (1.0 / D)
            dh2_ref[...] = dh2
            dy2, dgp = _rms_bwd(ny, r, gp_ref[...], dh2, D)
            dy2_ref[...] = dy2.astype(BF)
            dgp_ref[...] += dgp

    tok = lambda n: pl.BlockSpec((tm, n), lambda i, j: (i, 0))
    col = pl.BlockSpec((tm, fb), lambda i, j: (i, j))
    return _pcall(
        body, "ffn_fwd", (T // tm, nf),
        [tok(D), pl.BlockSpec((2 * fb, D), lambda i, j: (j, 0)), pl.BlockSpec((fb, D), lambda i, j: (j, 0)),
         tok(D), tok(D), _full((1, D))],
        [col, col, col, tok(D), tok(D), _full((1, D)), _full((1, HB))],
        [_sds((T, nf * fb), BF)] * 3 + [_sds((T, D)), _sds((T, D), BF), _sds((1, D)), _sds((1, HB))],
        scratch=[pltpu.VMEM((tm, D), F32)],
    )(z, wgu, wd, h1, tgt, g_fpost)


def _dsilu(x, s):
    return s * (1.0 + x * (1.0 - s))


def _ffn_bwd_x(dy2, gs, us, wgu, wd, h1, y1, dh2, g_fpre, g_post, tm):
    T = dy2.shape[0]
    nf = wd.shape[0] // _FB

    def body(dy2_ref, gs_ref, us_ref, wgu_ref, wd_ref, h1_ref, y1_ref, dh2_ref, gf_ref, gp_ref,
             dgu_ref, dh1_ref, dy1_ref, dgf_ref, dgp_ref, acc):
        i, j = pl.program_id(0), pl.program_id(1)
        dff = _dot_nt(dy2_ref[...], wd_ref[...])
        dg = (dff * gs_ref[...].astype(F32)).astype(BF)
        du = (dff * us_ref[...].astype(F32)).astype(BF)
        dgu = jnp.concatenate([dg[:, 0:FF_PAD], du[:, 0:FF_PAD], dg[:, FF_PAD:_FB], du[:, FF_PAD:_FB]], 1)
        dgu_ref[...] = dgu
        part = jnp.dot(dgu, wgu_ref[...], preferred_element_type=F32)

        @pl.when(j == 0)
        def _():
            acc[...] = part

        @pl.when(j > 0)
        def _():
            acc[...] += part

        @pl.when((i == 0) & (j == 0))
        def _():
            dgf_ref[...] = jnp.zeros(dgf_ref.shape, F32)
            dgp_ref[...] = jnp.zeros(dgp_ref.shape, F32)

        @pl.when(j == nf - 1)
        def _():
            nh, rh = _rms(h1_ref[...], D)
            dh, dgf = _rms_bwd(nh, rh, gf_ref[...], acc[...], D)
            dh1 = dh2_ref[...] + dh
            dh1_ref[...] = dh1
            dgf_ref[...] += dgf
            ny, ry = _rms(y1_ref[...], D)
            dy1, dgp = _rms_bwd(ny, ry, gp_ref[...], dh1, D)
            dy1_ref[...] = dy1.astype(BF)
            dgp_ref[...] += dgp

    tok = lambda n: pl.BlockSpec((tm, n), lambda i, j: (i, 0))
    col = pl.BlockSpec((tm, _FB), lambda i, j: (i, j))
    return _pcall(
        body, "ffn_bwd_x", (T // tm, nf),
        [tok(D), col, col, pl.BlockSpec((2 * _FB, D), lambda i, j: (j, 0)), pl.BlockSpec((_FB, D), lambda i, j: (j, 0)),
         tok(D), tok(D), tok(D), _full((1, D)), _full((1, D))],
        [pl.BlockSpec((tm, 2 * _FB), lambda i, j: (i, j)), tok(D), tok(D), _full((1, D)), _full((1, D))],
        [_sds((T, 2 * nf * _FB), BF), _sds((T, D)), _sds((T, D), BF), _sds((1, D)), _sds((1, D))],
        scratch=[pltpu.VMEM((tm, D), F32)],
    )(dy2, gs, us, wgu, wd, h1, y1, dh2, g_fpre, g_post)


def _ffn_bwd_w(z, ffs, dgu, dy2, tm):
    T = z.shape[0]
    nf = ffs.shape[1] // _FB
    nt = T // tm

    def body(z_ref, ff_ref, dgu_ref, dy2_ref, dwgu_ref, dwd_ref, agu, ad):
        i = pl.program_id(1)
        pgu = _dot_tn(dgu_ref[...], z_ref[...])
        pd = _dot_tn(ff_ref[...], dy2_ref[...])

        @pl.when(i == 0)
        def _():
            agu[...] = pgu
            ad[...] = pd

        @pl.when(i > 0)
        def _():
            agu[...] += pgu
            ad[...] += pd

        @pl.when(i == nt - 1)
        def _():
            dwgu_ref[...] = agu[...].astype(BF)
            dwd_ref[...] = ad[...].astype(BF)

    F = nf * _FB
    tok = lambda n: pl.BlockSpec((tm, n), lambda j, i: (i, 0))
    return _pcall(
        body, "ffn_bwd_w", (nf, nt),
        [tok(D), pl.BlockSpec((tm, _FB), lambda j, i: (i, j)), pl.BlockSpec((tm, 2 * _FB), lambda j, i: (i, j)), tok(D)],
        [pl.BlockSpec((2 * _FB, D), lambda j, i: (j, 0)), pl.BlockSpec((_FB, D), lambda j, i: (j, 0))],
        [_sds((2 * F, D), BF), _sds((F, D), BF)],
        scratch=[pltpu.VMEM((2 * _FB, D), F32), pltpu.VMEM((_FB, D), F32)],
    )(z, ffs, dgu, dy2)


def _out_bwd(dy1, mix, o_pad, w_out, g_on, tm):
    T = dy1.shape[0]
    W = N_HEADS * HB

    def body(dy1_ref, mix_ref, o_ref, w_ref, gon_ref, do_ref, dl_ref, dohg_ref, dw_ref, dgon_ref):
        i = pl.program_id(0)
        dy1v = dy1_ref[...]
        dmix = _dot_nt(dy1v, w_ref[...])
        pw = _dot_tn(mix_ref[...], dy1v)

        @pl.when(i == 0)
        def _():
            dw_ref[...] = pw
            dgon_ref[...] = jnp.zeros(dgon_ref.shape, F32)

        @pl.when(i > 0)
        def _():
            dw_ref[...] += pw

        for h in range(N_HEADS):
            sl = slice(h * HB, (h + 1) * HB)
            ov = o_ref[:, sl].astype(F32)
            n, r = _rms(ov, V_DIM)
            do, dg = _rms_bwd(n, r, gon_ref[:, sl], dmix[:, sl], V_DIM)
            dgon_ref[:, sl] += dg
            do_ref[:, sl] = do.astype(BF)
            dl_ref[:, sl] = jnp.broadcast_to(jnp.sum(do * ov, -1, keepdims=True), (tm, HB))
        dohg_ref[...] = dmix[:, W:MIX_W]

    return _pcall(body, "out_bwd", (T // tm,),
                  [_rows(tm, D), _rows(tm, MIX_W), _rows(tm, W), _full((MIX_W, D)), _full((1, W))],
                  [_rows(tm, W), _rows(tm, W), _rows(tm, G_W), _full((MIX_W, D)), _full((1, W))],
                  [_sds((T, W), BF), _sds((T, W)), _sds((T, G_W)), _sds((MIX_W, D)), _sds((1, W))],
                  )(dy1, mix, o_pad, w_out, g_on)


def _flash_bwd(q, k, v, do, lse, dl, tq, exchange=None):
    T = q.shape[0]
    nq = T // tq
    scale = QK_DIM ** -0.5
    hp = _AH
    W = hp * HB

    def body(k_ref, v_ref, q_ref, do_ref, lse_ref, dl_ref, dk_ref, dv_ref, dq_ref):
        j = pl.program_id(1)

        @pl.when(j == 0)
        def _():
            dq_ref[...] = jnp.zeros(dq_ref.shape, F32)

        def blk(i, carry, masked):
            sl = pl.ds(pl.multiple_of(i * tq, tq), tq)
            out = []
            for h in range(hp):
                ls = slice(h * HB, (h + 1) * HB)
                dk, dv = carry[h]
                kv, vv = k_ref[:, ls], v_ref[:, ls]
                qv, dov = q_ref[sl, ls], do_ref[sl, ls]
                s = _dot_nt(qv, kv) * scale
                if masked:
                    r = lax.broadcasted_iota(jnp.int32, (tq, tq), 0)
                    c = lax.broadcasted_iota(jnp.int32, (tq, tq), 1)
                    s = jnp.where(c <= r, s, NEG)
                p = jnp.exp(s - lse_ref[sl, h * HB:h * HB + 1])
                ds = p * (_dot_nt(dov, vv) - dl_ref[sl, h * HB:h * HB + 1]) * scale
                dq_ref[sl, ls] += _dot(ds, kv)
                out.append((dk + _dot_tn(ds, qv), dv + _dot_tn(p, dov)))
            return tuple(out)

        zero = jnp.zeros((tq, HB), F32)
        carry = blk(j, tuple((zero, zero) for _ in range(hp)), True)
        res = lax.fori_loop(j + 1, nq, lambda i, cr: blk(i, cr, False), carry)
        for h in range(hp):
            ls = slice(h * HB, (h + 1) * HB)
            dk_ref[:, ls] = res[h][0].astype(BF)
            dv_ref[:, ls] = res[h][1].astype(BF)

    tile = pl.BlockSpec((tq, W), lambda h, j: (j, h))
    whole = pl.BlockSpec((T, W), lambda h, j: (0, h))
    return _pcall(body, "flash_bwd", (N_HEADS // hp, nq), [tile, tile, whole, whole, whole, whole],
                  [tile, tile, whole], [_sds((T, N_HEADS * HB), BF)] * 2 + [_sds((T, N_HEADS * HB))],
                  exchange=exchange)(k, v, q, do, lse, dl)


def _mla_prep_bwd(xp, tabs, dq, dk, dv, g_q, g_kv, w_uq, w_uk, w_uv, tm):
    T = xp.shape[0]
    W = N_HEADS * HB

    def body(xp_ref, ta_ref, tb1_ref, tb2_ref, dq_ref, dk_ref, dv_ref, gq_ref, gkv_ref, wuq_ref, wuk_ref, wuv_ref,
             dxp_ref, dwuq_ref, dwuk_ref, dwuv_ref, dgq_ref, dgkv_ref, dqp):
        i = pl.program_id(0)
        ta, tb1, tb2 = ta_ref[...], tb1_ref[...], tb2_ref[...]
        nq, rq = _rms(xp_ref[:, 0:Q_RANK], Q_RANK)
        nkv, rkv = _rms(xp_ref[:, Q_RANK:Q_RANK + KV_RANK], KV_RANK)
        dkr = jnp.zeros((tm, HB), F32)
        for h in range(N_HEADS):
            sl = slice(h * HB, (h + 1) * HB)
            dqp[:, sl] = _unrope(dq_ref[:, sl], ta, tb1, tb2).astype(BF)
            dkr = dkr + dk_ref[:, sl].astype(F32)
        dkr = pltpu.roll(_unrope(dkr, ta, tb1, tb2), HB - NOPE, 1)
        lane = lax.broadcasted_iota(jnp.int32, (tm, HB), 1)
        dxp_ref[:, Q_RANK + KV_RANK:MLA_IN] = jnp.where(lane < ROPE, dkr, 0.0)
        dqpv = dqp[...]
        dkv, dvv = dk_ref[...], dv_ref[...]
        nqs = (nq * gq_ref[...]).astype(BF)
        nkvs = (nkv * gkv_ref[...]).astype(BF)
        pq, pk, pv = _dot_tn(nqs, dqpv), _dot_tn(nkvs, dkv), _dot_tn(nkvs, dvv)
        dcq, dgq = _rms_bwd(nq, rq, gq_ref[...], _dot_nt(dqpv, wuq_ref[...]), Q_RANK)
        dckv, dgkv = _rms_bwd(nkv, rkv, gkv_ref[...], _dot_nt(dkv, wuk_ref[...]) + _dot_nt(dvv, wuv_ref[...]), KV_RANK)
        dxp_ref[:, 0:Q_RANK] = dcq
        dxp_ref[:, Q_RANK:Q_RANK + KV_RANK] = dckv

        @pl.when(i == 0)
        def _():
            dwuq_ref[...] = pq
            dwuk_ref[...] = pk
            dwuv_ref[...] = pv
            dgq_ref[...] = dgq
            dgkv_ref[...] = dgkv

        @pl.when(i > 0)
        def _():
            dwuq_ref[...] += pq
            dwuk_ref[...] += pk
            dwuv_ref[...] += pv
            dgq_ref[...] += dgq
            dgkv_ref[...] += dgkv

    tab = _rows(tm, HB)
    return _pcall(
        body, "mla_prep_bwd", (T // tm,),
        [_rows(tm, MLA_IN), tab, tab, tab, _rows(tm, W), _rows(tm, W), _rows(tm, W), _full((1, Q_RANK)),
         _full((1, KV_RANK)), _full((Q_RANK, W)), _full((KV_RANK, W)), _full((KV_RANK, W))],
        [_rows(tm, MLA_IN), _full((Q_RANK, W)), _full((KV_RANK, W)), _full((KV_RANK, W)), _full((1, Q_RANK)),
         _full((1, KV_RANK))],
        [_sds((T, MLA_IN)), _sds((Q_RANK, W)), _sds((KV_RANK, W)), _sds((KV_RANK, W)), _sds((1, Q_RANK)),
         _sds((1, KV_RANK))],
        scratch=[pltpu.VMEM((tm, W), BF)],
    )(xp, *tabs, dq, dk, dv, g_q, g_kv, w_uq, w_uk, w_uv)


def _hgrn_bwd(xp, o_raw, s_all, a_all, b_all, d_out, lb_logits, g_hn, exchange=None):
    T = xp.shape[0]
    tb = min(_TB_BWD, T)
    ncb = tb // CHUNK
    nb = T // tb
    hp = _HP
    W = hp * G_DIM

    def body(hq_ref, hf_ref, hi_ref, hg_ref, o_ref, sall_ref, aall_ref, ball_ref, dout_ref, lbl_ref, ghn_ref,
             dhq_ref, dhf_ref, dhi_ref, dhg_ref, dlbl_ref, dghn_ref, dst_ref, b_s, acc_lb, acc_g):
        t = pl.program_id(1)
        lb_all = _lower_bound(lbl_ref)

        @pl.when(t == 0)
        def _():
            dst_ref[...] = jnp.zeros(dst_ref.shape, F32)
            acc_lb[...] = jnp.zeros(acc_lb.shape, F32)
            acc_g[...] = jnp.zeros(acc_g.shape, F32)

        row = lax.broadcasted_iota(jnp.int32, (CHUNK, CHUNK), 0)
        col = lax.broadcasted_iota(jnp.int32, (CHUNK, CHUNK), 1)
        tri_t = (col >= row).astype(BF)
        codes = _intra_codes(SUB)
        last = lax.broadcasted_iota(jnp.int32, (CHUNK, G_DIM), 0) == CHUNK - 1

        def chunk(cc, carry):
            c = ncb - 1 - cc
            sl = pl.ds(pl.multiple_of(c * CHUNK, CHUNK), CHUNK)
            for h in range(hp):
                ls = slice(h * G_DIM, (h + 1) * G_DIM)
                lb, ghn = lb_all[:, ls], ghn_ref[:, ls]
                hq, hg = hq_ref[sl, ls], hg_ref[sl, ls]
                q, k, f, _, sig, sq = _gates(hq, hf_ref[sl, ls], lb)
                v = hi_ref[sl, ls]
                b2 = ball_ref[sl, ls]
                b_s[h] = b2
                st = sall_ref[c, h]
                dstn = dst_ref[h]
                o = o_ref[sl, ls]
                dout = dout_ref[sl, ls]
                n, r = _rms(o, G_DIM)
                sg = _sigmoid(hg)
                dhg_ref[sl, ls] = dout * (n * ghn) * _dsilu(hg, sg)
                do, dg = _rms_bwd(n, r, ghn, dout * (hg * sg), G_DIM)
                acc_g[:, ls] += dg
                eb = jnp.exp2(b2)
                bl = b_s[h, CHUNK - 1:CHUNK, :]
                ebl = jnp.exp2(bl)
                ekd = jnp.exp2(bl - b2)
                kd = k * ekd
                a = aall_ref[c, h]
                dq_i, dk_i = _intra(q, k, b2, b_s.at[h], codes, _dot_nt(do, v))
                dhi_ref[sl, ls] = _dot_tn(a, do) + _dot_nt(kd, dstn)
                dk_state = _dot(v, dstn) * ekd
                dq = dq_i + _dot(do, st) * eb
                dk = dk_i + dk_state
                dbl = jnp.sum(k * dk_state, 0, keepdims=True) + ebl * jnp.sum(dstn * st, 0, keepdims=True)
                db = q * dq - k * dk + jnp.where(last, dbl, 0.0)
                df = _tri_mm(tri_t, db) / f - dk
                dhf_ref[sl, ls] = df * (1.0 - lb) * sig * (1.0 - sig)
                acc_lb[:, ls] += jnp.sum(df * (1.0 - sig), 0, keepdims=True)
                dhq_ref[sl, ls] = dq * _dsilu(hq, sq)
                dst_ref[h] = dstn * ebl + _dot_tn(do, q * eb)
            return carry

        lax.fori_loop(0, ncb, chunk, 0, unroll=4)

        @pl.when(t == nb - 1)
        def _():
            dl0 = acc_lb[...] * lb_all * (1.0 - lb_all)
            dlbl_ref[0:1, :] = dl0
            dlbl_ref[1:2, :] = -dl0
            dghn_ref[...] = acc_g[...]

    col_blk = lambda j: pl.BlockSpec((tb, W), lambda p, t: (nb - 1 - t, j * (G_HEADS // hp) + p))
    head = pl.BlockSpec((tb, W), lambda p, t: (nb - 1 - t, p))
    two = pl.BlockSpec((2, W), lambda p, t: (0, p))
    one = pl.BlockSpec((1, W), lambda p, t: (0, p))
    res = _pcall(
        body, "hgrn_bwd", (G_HEADS // hp, nb),
        [col_blk(0), col_blk(1), col_blk(2), col_blk(3), head,
         pl.BlockSpec((ncb, hp, G_DIM, G_DIM), lambda p, t: (nb - 1 - t, p, 0, 0)),
         pl.BlockSpec((ncb, hp, CHUNK, CHUNK), lambda p, t: (nb - 1 - t, p, 0, 0)), head, head, two, one],
        [head, head, head, head, two, one],
        [_sds((T, G_W))] * 4 + [_sds((2, G_W)), _sds((1, G_W))],
        scratch=[pltpu.VMEM((hp, G_DIM, G_DIM), F32), pltpu.VMEM((hp, CHUNK, G_DIM), F32),
                 pltpu.VMEM((1, W), F32), pltpu.VMEM((1, W), F32)], exchange=exchange,
    )(xp, xp, xp, xp, o_raw, s_all, a_all, b_all, d_out, lb_logits, g_hn)
    return res


def _in_bwd_x(x, dxp_m, dxp_h, dh1, w_in_al, g_pre, tm, exchange=None):
    T = x.shape[0]

    def body(x_ref, dm_ref, d0_ref, d1_ref, d2_ref, d3_ref, dh1_ref, w_ref, g_ref, dx_ref, dg_ref):
        i = pl.program_id(0)
        du = _dot_nt(dm_ref[...], w_ref[:, 0:MLA_IN])
        for j, d_ref in enumerate((d0_ref, d1_ref, d2_ref, d3_ref)):
            du = du + _dot_nt(d_ref[...], w_ref[:, MLA_IN + j * G_W:MLA_IN + (j + 1) * G_W])
        nx, r = _rms(x_ref[...], D)
        dx, dg = _rms_bwd(nx, r, g_ref[...], du, D)
        dx_ref[...] = dh1_ref[...] + dx

        @pl.when(i == 0)
        def _():
            dg_ref[...] = dg

        @pl.when(i > 0)
        def _():
            dg_ref[...] += dg

    return _pcall(body, "in_bwd_x", (T // tm,),
                  [_rows(tm, D), _rows(tm, MLA_IN)] + [_rows(tm, G_W)] * 4 + [_rows(tm, D), _full((D, XP_W)), _full((1, D))],
                  [_rows(tm, D), _full((1, D))], [_sds((T, D)), _sds((1, D))], exchange=exchange,
                  )(x, dxp_m, *dxp_h, dh1, w_in_al, g_pre)


def _aligned_col(c):
    return jnp.where(c < Q_RANK + KV_RANK + ROPE, c, c + (KR_PAD - ROPE))


def _align_w_in(g_in):
    tile = 384
    kr_end = Q_RANK + KV_RANK + ROPE

    def body(g_ref, o_ref, gp):
        gp[...] = jnp.zeros(gp.shape, BF)
        for j in range(N_DEV):
            gp[j, :, 0:IN_SH] = g_ref[j]
        r = lax.broadcasted_iota(jnp.int32, (tile, tile), 0)
        c = lax.broadcasted_iota(jnp.int32, (tile, tile), 1)
        for t in range(XP_W // tile):
            lo, hi = t * tile, (t + 1) * tile
            cols = [a if a < kr_end else a - (KR_PAD - ROPE) for a in (lo, hi - 1)]
            acc = jnp.zeros((D, tile), F32)
            for j in range(cols[0] // IN_SH, cols[-1] // IN_SH + 1):
                sel = (r < IN_SH) & (_aligned_col(j * IN_SH + r) == lo + c)
                acc = acc + jnp.dot(gp[j], sel.astype(BF), preferred_element_type=F32)
            o_ref[:, lo:hi] = acc.astype(BF)

    vm = pl.BlockSpec(memory_space=pltpu.VMEM)
    return pl.pallas_call(
        body, name="align_w_in", in_specs=[vm], out_specs=vm, out_shape=_sds((D, XP_W), BF),
        scratch_shapes=[pltpu.VMEM((N_DEV, D, tile), BF)],
        compiler_params=pltpu.CompilerParams(vmem_limit_bytes=_VMEM_LIMIT))(g_in)


def _in_bwd_w(name, u, dxp_m, dxp_h, tm, row0, nr, exchange=None):
    T = u.shape[0]
    nt = T // tm
    win = 640

    def body(u_ref, dm_ref, d0_ref, d1_ref, d2_ref, d3_ref, o_ref, acc):
        i = pl.program_id(0)
        ut = u_ref[:, row0:row0 + nr].T
        parts = [(0, MLA_IN, dm_ref)] + [(MLA_IN + j * G_W, G_W, d) for j, d in enumerate((d0_ref, d1_ref, d2_ref, d3_ref))]

        @pl.when(i == 0)
        def _():
            for lo, n, d in parts:
                acc[:, lo:lo + n] = jnp.dot(ut, d[...].astype(BF), preferred_element_type=F32)

        @pl.when(i > 0)
        def _():
            for lo, n, d in parts:
                acc[:, lo:lo + n] += jnp.dot(ut, d[...].astype(BF), preferred_element_type=F32)

        @pl.when(i == nt - 1)
        def _():
            wide = 384
            r = lax.broadcasted_iota(jnp.int32, (win, wide), 0)
            c = lax.broadcasted_iota(jnp.int32, (win, wide), 1)
            kr_end = Q_RANK + KV_RANK + ROPE
            for j in range(N_DEV):
                first = j * IN_SH if j * IN_SH < kr_end else j * IN_SH + (KR_PAD - ROPE)
                lo = min(first // HB * HB, XP_W - win)
                sel = (c < IN_SH) & (_aligned_col(j * IN_SH + c) == lo + r)
                res = jnp.dot(acc[:, lo:lo + win].astype(BF), sel.astype(BF), preferred_element_type=F32)
                o_ref[j] = res[:, 0:IN_SH].astype(BF)

    return _pcall(body, name, (nt,),
                  [_rows(tm, D), _rows(tm, MLA_IN)] + [_rows(tm, G_W)] * 4,
                  [_full((N_DEV, nr, IN_SH))], [_sds((N_DEV, nr, IN_SH), BF)],
                  scratch=[pltpu.VMEM((nr, XP_W), F32)], exchange=exchange)(u, dxp_m, *dxp_h)


def _pad_heads(w, width, real):
    lead = w.shape[:-1]
    w = w.reshape(lead + (N_HEADS, real))
    w = jnp.pad(w, [(0, 0)] * len(lead) + [(0, 0), (0, width - real)])
    return w.reshape(lead + (N_HEADS * width,))


def _unpad_heads(w, width, real):
    lead = w.shape[:-1]
    return w.reshape(lead + (N_HEADS, width))[..., :real].reshape(lead + (N_HEADS * real,))


def _rope_tables(positions):
    half = ROPE // 2
    inv_freq = 1.0 / (ROPE_THETA ** (jnp.arange(0, ROPE, 2, dtype=F32) / ROPE))
    ang = positions.astype(F32)[:, None] * inv_freq
    cos, sin = jnp.cos(ang), jnp.sin(ang)
    T = positions.shape[0]
    z = lambda n: jnp.zeros((T, n), F32)
    ta = jnp.concatenate([jnp.ones((T, NOPE), F32), cos, cos, z(HB - QK_DIM)], 1)
    tb1 = jnp.concatenate([z(NOPE), -sin, z(half), z(HB - QK_DIM)], 1)
    tb2 = jnp.concatenate([z(NOPE), z(half), sin, z(HB - QK_DIM)], 1)
    return ta, tb1, tb2


def kernel(x, positions, attn_pre_norm, w_in, mla_q_norm, mla_w_uq, mla_kv_norm, mla_w_ukv, mla_out_norm, hgrn_lb_logits, hgrn_out_norm, w_out, attn_post_norm, ffn_pre_norm, w_gate, w_up, w_down, ffn_post_norm, loss_target, m_attn_pre_norm, m_w_in, m_mla_q_norm, m_mla_w_uq, m_mla_kv_norm, m_mla_w_ukv, m_mla_out_norm, m_hgrn_lb_logits, m_hgrn_out_norm, m_w_out, m_attn_post_norm, m_ffn_pre_norm, m_w_gate, m_w_up, m_w_down, m_ffn_post_norm, v_attn_pre_norm, v_w_in, v_mla_q_norm, v_mla_w_uq, v_mla_kv_norm, v_mla_w_ukv, v_mla_out_norm, v_hgrn_lb_logits, v_hgrn_out_norm, v_w_out, v_attn_post_norm, v_ffn_pre_norm, v_w_gate, v_w_up, v_w_down, v_ffn_post_norm):
    T = x.shape[1]
    tm = min(_TM, T)
    tq = min(_TQ, T)
    xs, tgt = x[0], loss_target[0]
    uq_sh = (Q_RANK // N_DEV, N_HEADS * QK_DIM)

    b_in, b_uq, b_out, b_gu, b_d = _cast_shards(
        w_in[0], mla_w_uq[0].reshape(uq_sh), w_out[0], w_gate[0].T, w_up[0].T, w_down[0])
    g_in, g_uq = _gather_two_level("ag_first", [b_in, b_uq])
    w_in_al = _align_w_in(g_in)
    w_uq_p = _pad_heads(g_uq.reshape(Q_RANK, N_HEADS * QK_DIM), HB, QK_DIM)
    w_ukv = mla_w_ukv[0].astype(BF)
    w_uk_p = _pad_heads(w_ukv[..., :NOPE].reshape(KV_RANK, N_HEADS * NOPE), HB, NOPE)
    w_uv_p = _pad_heads(w_ukv[..., NOPE:].reshape(KV_RANK, N_HEADS * V_DIM), HB, V_DIM)
    g_on_p = _pad_heads(mla_out_norm, HB, V_DIM)
    tabs = _rope_tables(positions[0])

    xp_m, xp_h, u = _fwd_in(xs, attn_pre_norm, w_in_al, tm)
    q_att, qs_att, k_att, v_att = _mla_prep(xp_m, tabs, mla_q_norm, mla_kv_norm, w_uq_p, w_uk_p, w_uv_p, tm)
    o_hgrn, o_raw, s_all, a_all, b_all, wd = _hgrn_fwd(xp_h, hgrn_lb_logits, hgrn_out_norm, ([GATHER], [b_d]))
    wd = wd.reshape(N_DEV * FF_PAD, D)
    o_pad, lse, wgu, g_out = _flash_fwd(qs_att, k_att, v_att, tq, ([GATHER, GATHER], [b_gu, b_out]))
    wgu = wgu.reshape(N_DEV * 2 * FF_PAD, D)
    w_out_full = g_out.reshape(D, D)
    w_out_mla = jnp.pad(w_out_full[:N_HEADS * V_DIM].reshape(N_HEADS, V_DIM, D), ((0, 0), (0, HB - V_DIM), (0, 0)))
    w_out_p = jnp.concatenate([w_out_mla.reshape(N_HEADS * HB, D), w_out_full[N_HEADS * V_DIM:]], 0)
    h1, y1, z, mix = _fwd_out(o_pad, o_hgrn, xs, g_on_p, w_out_p, attn_post_norm, ffn_pre_norm, tm)
    tmf = min(_TMF, T)
    gs, us, ffs, dh2, dy2, d_fpost, loss_row = _ffn_fwd(z, wgu, wd, h1, tgt, ffn_post_norm, tm, _FB // FF_PAD)

    dgu, dh1, dy1, d_fpre, d_post = _ffn_bwd_x(dy2, gs, us, wgu, wd, h1, y1, dh2, ffn_pre_norm, attn_post_norm, tm)
    dwgu, dwd = _ffn_bwd_w(z, ffs, dgu, dy2, tmf)
    do_pad, dl, d_ohg, dw_out_p, d_on_p = _out_bwd(dy1, mix, o_pad, w_out_p, g_on_p, tm)
    dw_out_mla = dw_out_p[:N_HEADS * HB].reshape(N_HEADS, HB, D)[:, :V_DIM].reshape(N_HEADS * V_DIM, D)
    dw_out = jnp.concatenate([dw_out_mla, dw_out_p[N_HEADS * HB:]], 0).reshape(N_DEV, D // N_DEV, D).astype(BF)
    dk_att, dv_att, dq_att, p_gu, p_d, p_out = _flash_bwd(
        q_att, k_att, v_att, do_pad, lse, dl, tq,
        ([SCATTER] * 3, [dwgu.reshape(N_DEV, 2 * FF_PAD, D), dwd.reshape(N_DEV, FF_PAD, D), dw_out]))
    dxp_m, dw_uq_p, dw_uk_p, dw_uv_p, d_gq, d_gkv = _mla_prep_bwd(
        xp_m, tabs, dq_att, dk_att, dv_att, mla_q_norm, mla_kv_norm, w_uq_p, w_uk_p, w_uv_p, tm)
    dw_uq = _unpad_heads(dw_uq_p, HB, QK_DIM).reshape((N_DEV,) + uq_sh).astype(BF)
    dw_ukv = jnp.concatenate([_unpad_heads(dw_uk_p, HB, NOPE).reshape(KV_RANK, N_HEADS, NOPE),
                              _unpad_heads(dw_uv_p, HB, V_DIM).reshape(KV_RANK, N_HEADS, V_DIM)], -1)
    *dxp_h, d_lbl, d_ghn, p_uq, dw_ukv_all = _hgrn_bwd(
        xp_h, o_raw, s_all, a_all, b_all, d_ohg, hgrn_lb_logits, hgrn_out_norm,
        ([SCATTER, GATHER], [dw_uq, dw_ukv.reshape(KV_RANK, N_HEADS * HB)]))
    dw_in_a, = _in_bwd_w("in_bwd_w_a", u, dxp_m, dxp_h, tm, 0, _IN_ROWS_A)
    dw_in_b, p_in_a = _in_bwd_w("in_bwd_w_b", u, dxp_m, dxp_h, tm, _IN_ROWS_A, D - _IN_ROWS_A, ([SCATTER], [dw_in_a]))
    grad_x, d_pre, p_in_b = _in_bwd_x(xs, dxp_m, dxp_h, dh1, w_in_al, attn_pre_norm, tm, ([SCATTER], [dw_in_b]))
    d_on = _unpad_heads(d_on_p, HB, V_DIM)

    ukv2 = lambda a: a.reshape(KV_RANK, N_HEADS * HB)
    vecs = [d_pre, d_gq, d_gkv, d_on, d_lbl, d_ghn, d_post, d_fpre, d_fpost, loss_row]
    small_w = [attn_pre_norm, mla_q_norm, mla_kv_norm, ukv2(mla_w_ukv), mla_out_norm, hgrn_lb_logits, hgrn_out_norm,
               attn_post_norm, ffn_pre_norm, ffn_post_norm]
    small_m = [m_attn_pre_norm, m_mla_q_norm, m_mla_kv_norm, ukv2(m_mla_w_ukv), m_mla_out_norm, m_hgrn_lb_logits,
               m_hgrn_out_norm, m_attn_post_norm, m_ffn_pre_norm, m_ffn_post_norm]
    small_v = [v_attn_pre_norm, v_mla_q_norm, v_mla_kv_norm, ukv2(v_mla_w_ukv), v_mla_out_norm, v_hgrn_lb_logits,
               v_hgrn_out_norm, v_attn_post_norm, v_ffn_pre_norm, v_ffn_post_norm]
    rall = _final_exchange(vecs)
    s_g, s_d, s_m, s_v, loss_all = _small_adam(rall, dw_ukv_all, 3, small_w, small_m, small_v)
    r_in = _shard_adam("adam_w_in", [p_in_a, p_in_b], w_in[0], m_w_in[0], v_w_in[0], 128)
    r_uq = _shard_adam("adam_w_uq", [p_uq], mla_w_uq[0].reshape(uq_sh), m_mla_w_uq[0].reshape(uq_sh),
                       v_mla_w_uq[0].reshape(uq_sh), uq_sh[0])
    r_out = _shard_adam("adam_w_out", [p_out], w_out[0], m_w_out[0], v_w_out[0], D // N_DEV)
    r_g, r_u = _gate_up_adam(p_gu, (w_gate[0].T, w_up[0].T), (m_w_gate[0].T, m_w_up[0].T),
                             (v_w_gate[0].T, v_w_up[0].T))
    r_g, r_u = [a.T for a in r_g], [a.T for a in r_u]
    r_d = _shard_adam("adam_w_down", [p_d], w_down[0], m_w_down[0], v_w_down[0], FF_SH // 2)

    loss = loss_all[0, 0]

    def assemble(big, small):
        b_in, b_uq, b_out, b_g, b_u, b_d = big
        return [small[0], b_in[None], small[1], b_uq.reshape(mla_w_uq.shape), small[2],
                small[3].reshape(mla_w_ukv.shape), small[4], small[5], small[6], b_out[None], small[7], small[8],
                b_g[None], b_u[None], b_d[None], small[9]]

    outs = [loss, grad_x[None]]
    for idx, small in enumerate((s_g, s_d, s_m, s_v)):
        outs += assemble([r[idx] for r in (r_in, r_uq, r_out, r_g, r_u, r_d)], small)
    return tuple(outs)
```

```python
import jax
import jax.numpy as jnp
from jax import lax
from jax.experimental import pallas as pl
from jax.experimental.pallas import tpu as pltpu

BF = jnp.bfloat16
F32 = jnp.float32
MESH = pl.DeviceIdType.MESH

N_DEV = 8
D = 1024
EPS = 1e-6
LOG2E = 1.4426950408889634
ROPE_THETA = 10000.0
N_HEADS = 8
HB = 128
NOPE = 64
ROPE = 32
V_DIM = 64
QK_DIM = NOPE + ROPE
Q_RANK = 384
KV_RANK = 128
KR_PAD = 128
MLA_IN = Q_RANK + KV_RANK + KR_PAD
G_HEADS = 4
G_DIM = 128
G_W = G_HEADS * G_DIM
CHUNK = 64
SUB = 16
XP_W = MLA_IN + 4 * G_W
IN_SH = 324
FF_SH = 352
FF_PAD = 384
MIX_W = N_HEADS * HB + G_W

ADAM_LR = 0.001
ADAM_B1 = 0.9
ADAM_B2 = 0.999
ADAM_EPS = 1e-08
ADAM_WD = 0.01
ADAM_STEP = 10

_TM = 512
_TMF = 1024
_IN_ROWS_A = 384
_TQ = 512
_AH = 2
_AH_FWD = 4
_FB = 768
_TB = 1024
_TB_BWD = 512
_HP = 4
V7X_VMEM_BYTES = 64 * 1024 * 1024
_VMEM_LIMIT = V7X_VMEM_BYTES - 8 * 1024 * 1024
NEG = -1e30


def _dot(a, b):
    return jnp.dot(a.astype(BF), b.astype(BF), preferred_element_type=F32)


def _dot_nt(a, b):
    return lax.dot_general(a.astype(BF), b.astype(BF), (((1,), (1,)), ((), ())), preferred_element_type=F32)


def _dot_tn(a, b):
    return lax.dot_general(a.astype(BF), b.astype(BF), (((0,), (0,)), ((), ())), preferred_element_type=F32)


def _sigmoid(x):
    return 1.0 / (1.0 + jnp.exp(-x))


def _rms(x, n):
    r = lax.rsqrt(jnp.sum(x * x, -1, keepdims=True) * (1.0 / n) + EPS)
    return x * r, r


def _rms_bwd(nx, r, g, dy, n):
    dg = jnp.sum(dy * nx, 0, keepdims=True)
    dn = dy * g
    dx = r * (dn - nx * (jnp.sum(dn * nx, -1, keepdims=True) * (1.0 / n)))
    return dx, dg


def _adamw(w, g, m, v):
    m2 = ADAM_B1 * m + (1.0 - ADAM_B1) * g
    v2 = ADAM_B2 * v + (1.0 - ADAM_B2) * (g * g)
    m_hat = m2 / (1.0 - ADAM_B1 ** ADAM_STEP)
    v_hat = v2 / (1.0 - ADAM_B2 ** ADAM_STEP)
    delta = -ADAM_LR * (m_hat / (jnp.sqrt(v_hat) + ADAM_EPS) + ADAM_WD * w)
    return delta, m2, v2


def _pcall(body, name, grid, in_specs, out_specs, out_shape, scratch=(), exchange=None):
    scratch = list(scratch)
    extra = ()
    if exchange is not None:
        kinds, extra = exchange
        in_specs, out_specs, out_shape = list(in_specs), list(out_specs), list(out_shape)
        n_in, n_out, n_scr, n_x = len(in_specs), len(out_specs), len(scratch), len(extra)
        inner = body

        def body(*refs):
            ins, rest = refs[:n_in], refs[n_in:]
            x_src, rest = rest[:n_x], rest[n_x:]
            outs, rest = rest[:n_out], rest[n_out:]
            x_dst, rest = rest[:n_x], rest[n_x:]
            ex = _Exchange(kinds, x_src, x_dst, *rest[n_scr:])
            first = pl.program_id(0) == 0
            last = pl.program_id(0) == grid[0] - 1
            for a in range(1, len(grid)):
                first = first & (pl.program_id(a) == 0)
                last = last & (pl.program_id(a) == grid[a] - 1)
            pl.when(first)(ex.start)
            inner(*ins, *outs, *rest[:n_scr])
            pl.when(last)(ex.wait)

        in_specs += [_HBM] * n_x
        out_specs += [_HBM] * n_x
        out_shape += _exchange_shapes(kinds, extra)
        scratch += _exchange_sems(n_x)
    call = pl.pallas_call(
        body, name=name, grid=grid, in_specs=in_specs, out_specs=out_specs, out_shape=out_shape,
        scratch_shapes=scratch,
        compiler_params=pltpu.CompilerParams(
            dimension_semantics=("arbitrary",) * len(grid), vmem_limit_bytes=_VMEM_LIMIT))
    return lambda *operands: call(*operands, *extra)


def _full(shape):
    return pl.BlockSpec(shape, lambda *_: (0,) * len(shape))


def _rows(tm, n):
    return pl.BlockSpec((tm, n), lambda i, *_: (i, 0))


def _sds(shape, dtype=F32):
    return jax.ShapeDtypeStruct(shape, dtype)


def _peer(k, x, y, c):
    px = 1 - x if (k >> 2) & 1 else x
    py = 1 - y if (k >> 1) & 1 else y
    pc = 1 - c if k & 1 else c
    return px, py, pc


GATHER, SCATTER = "gather", "scatter"


class _Exchange:
    def __init__(self, kinds, srcs, dsts, send_sems, recv_sems, loc_sems):
        self.kinds, self.srcs, self.dsts = kinds, srcs, dsts
        self.send_sems, self.recv_sems, self.loc_sems = send_sems, recv_sems, loc_sems
        self.x, self.y, self.c = lax.axis_index("x"), lax.axis_index("y"), lax.axis_index("c")
        self.me = 4 * self.x + 2 * self.y + self.c

    def _src(self, w, slot):
        return self.srcs[w] if self.kinds[w] == GATHER else self.srcs[w].at[slot]

    def _dst(self, w, slot):
        return self.dsts[w].at[slot]

    def _copy(self, w, k, outgoing):
        px, py, pc = _peer(k, self.x, self.y, self.c)
        pid = 4 * px + 2 * py + pc
        return pltpu.make_async_remote_copy(
            src_ref=self._src(w, pid if outgoing else self.me),
            dst_ref=self._dst(w, self.me if outgoing else pid),
            send_sem=self.send_sems.at[w, k - 1], recv_sem=self.recv_sems.at[w, k - 1],
            device_id=(px, py, pc), device_id_type=MESH)

    def _local(self, w):
        return pltpu.make_async_copy(self._src(w, self.me), self._dst(w, self.me), self.loc_sems.at[w])

    def start(self):
        for w in range(len(self.srcs)):
            self._local(w).start()
            for k in range(1, N_DEV):
                self._copy(w, k, True).start()

    def wait(self):
        for w in range(len(self.srcs)):
            self._local(w).wait()
            for k in range(1, N_DEV):
                self._copy(w, k, False).wait_recv()
        for w in range(len(self.srcs)):
            for k in range(1, N_DEV):
                self._copy(w, k, True).wait_send()


def _exchange_sems(n_w):
    return [pltpu.SemaphoreType.DMA((n_w, N_DEV - 1)), pltpu.SemaphoreType.DMA((n_w, N_DEV - 1)),
            pltpu.SemaphoreType.DMA((n_w,))]


def _exchange_shapes(kinds, srcs):
    return [_sds(((N_DEV,) if kd == GATHER else ()) + tuple(s.shape), s.dtype) for kd, s in zip(kinds, srcs)]


_HBM = pl.BlockSpec(memory_space=pl.ANY)


def _cast_shards(w_in, w_uq, w_out, w_gate_t, w_up_t, w_down):
    shapes = [(D, IN_SH), (Q_RANK // N_DEV, N_HEADS * QK_DIM), (D // N_DEV, D), (2 * FF_PAD, D), (FF_PAD, D)]

    def body(win, wuq, wout, wg, wu, wd, sin_, suq, sout, sgu, sd):
        sin_[...] = win[...].astype(BF)
        suq[...] = wuq[...].astype(BF)
        sout[...] = wout[...].astype(BF)
        sgu[...] = jnp.zeros(sgu.shape, BF)
        sgu[0:FF_SH, :] = wg[...].astype(BF)
        sgu[FF_PAD:FF_PAD + FF_SH, :] = wu[...].astype(BF)
        sd[...] = jnp.zeros(sd.shape, BF)
        sd[0:FF_SH, :] = wd[...].astype(BF)

    vm = pl.BlockSpec(memory_space=pltpu.VMEM)
    return pl.pallas_call(
        body, name="cast_shards", in_specs=[vm] * 6, out_specs=[vm] * 5,
        out_shape=[_sds(s, BF) for s in shapes],
        compiler_params=pltpu.CompilerParams(vmem_limit_bytes=_VMEM_LIMIT),
    )(w_in, w_uq, w_out, w_gate_t, w_up_t, w_down)


def _gather_two_level(name, srcs):
    n_w = len(srcs)

    def body(*refs):
        src, dst = refs[:n_w], refs[n_w:2 * n_w]
        send_sems, recv_sems, loc_sems = refs[2 * n_w:]
        x, y, c = lax.axis_index("x"), lax.axis_index("y"), lax.axis_index("c")
        me, sibling = (x, y, c), (x, y, 1 - c)
        chips = [(1 - x, y), (x, 1 - y), (1 - x, 1 - y)]
        slot = lambda p: 4 * p[0] + 2 * p[1] + p[2]

        def copy(w, k, block, to, own=False):
            return pltpu.make_async_remote_copy(
                src_ref=src[w] if own else dst[w].at[slot(block)], dst_ref=dst[w].at[slot(block)],
                send_sem=send_sems.at[w, k], recv_sem=recv_sems.at[w, k], device_id=to, device_id_type=MESH)

        local = [pltpu.make_async_copy(src[w], dst[w].at[slot(me)], loc_sems.at[w]) for w in range(n_w)]
        first, passed = [], []
        for w in range(n_w):
            local[w].start()
            first.append(copy(w, 0, me, sibling, own=True))
            first += [copy(w, 1 + j, me, (*chip, c), own=True) for j, chip in enumerate(chips)]
        for cp in first:
            cp.start()
        for w in range(n_w):
            for j, chip in enumerate(chips):
                copy(w, 1 + j, (*chip, c), me).wait_recv()
                passed.append(copy(w, 4 + j, (*chip, c), sibling))
                passed[-1].start()
        for w in range(n_w):
            copy(w, 0, sibling, me).wait_recv()
            for j, chip in enumerate(chips):
                copy(w, 4 + j, (*chip, 1 - c), me).wait_recv()
        for cp in first + passed:
            cp.wait_send()
        for w in range(n_w):
            local[w].wait()

    return pl.pallas_call(
        body, name=name, in_specs=[_HBM] * n_w, out_specs=[_HBM] * n_w,
        out_shape=_exchange_shapes([GATHER] * n_w, srcs), scratch_shapes=_exchange_sems(n_w))(*srcs)


def _row_offsets(arrays):
    offs, rows = [], 0
    for a in arrays:
        offs.append(rows)
        rows += a.shape[0]
    return offs, -(-rows // 8) * 8


def _small_adam(rall, big_parts, big, ws, ms, vs):
    n_p = len(ws)
    packed = [w for p, w in enumerate(ws) if p != big] + [jax.ShapeDtypeStruct((1, HB), F32)]
    offs, _ = _row_offsets(packed)
    offs = offs[:big] + [None] + offs[big:]

    def total(ref, sl):
        g = ref[(0,) + sl]
        for j in range(1, N_DEV):
            g = g + ref[(j,) + sl]
        return g

    def body(*refs):
        rall_ref, big_ref = refs[:2]
        w_refs, m_refs, v_refs = refs[2:2 + n_p], refs[2 + n_p:2 + 2 * n_p], refs[2 + 2 * n_p:2 + 3 * n_p]
        outs = refs[2 + 3 * n_p:]
        for p in range(n_p):
            r, n = w_refs[p].shape
            if p == big:
                g = total(big_ref, (slice(0, r), slice(0, n)))
            else:
                g = total(rall_ref, (slice(offs[p], offs[p] + r), slice(0, n)))
            delta, m2, v2 = _adamw(w_refs[p][...], g, m_refs[p][...], v_refs[p][...])
            outs[p][...] = g
            outs[n_p + p][...] = delta
            outs[2 * n_p + p][...] = m2
            outs[3 * n_p + p][...] = v2
        outs[4 * n_p][...] = total(rall_ref, (slice(offs[n_p], offs[n_p] + 1), slice(0, HB)))

    vm = pl.BlockSpec(memory_space=pltpu.VMEM)
    res = pl.pallas_call(
        body, name="small_adam", in_specs=[vm] * (2 + 3 * n_p), out_specs=[vm] * (4 * n_p + 1),
        out_shape=[_sds(w.shape) for w in ws] * 4 + [_sds((1, HB))],
        compiler_params=pltpu.CompilerParams(vmem_limit_bytes=_VMEM_LIMIT),
    )(rall, big_parts, *ws, *ms, *vs)
    return res[:n_p], res[n_p:2 * n_p], res[2 * n_p:3 * n_p], res[3 * n_p:4 * n_p], res[4 * n_p]


def _device_sum(p_ref):
    g = p_ref[0].astype(F32)
    for j in range(1, N_DEV):
        g = g + p_ref[j].astype(F32)
    return g


def _shard_adam(name, parts, w, m, v, tr):
    a0, b0 = w.shape
    n_p = len(parts)
    b = parts[0].shape[2]
    first = [0]
    for p in parts:
        first.append(first[-1] + p.shape[1] // tr)

    def body(*refs):
        p_refs = refs[:n_p]
        w_ref, m_ref, v_ref, g_out, d_out, m_out, v_out = refs[n_p:]
        i = pl.program_id(0)
        g = _device_sum(p_refs[0])
        for k in range(1, n_p):
            g = jnp.where(i >= first[k], _device_sum(p_refs[k]), g)
        g = g[:, 0:b0]
        delta, m2, v2 = _adamw(w_ref[...], g, m_ref[...], v_ref[...])
        g_out[...] = g
        d_out[...] = delta
        m_out[...] = m2
        v_out[...] = v2

    def part_spec(k):
        last = first[k + 1] - first[k] - 1
        return pl.BlockSpec((N_DEV, tr, b), lambda i: (0, jnp.minimum(jnp.maximum(i - first[k], 0), last), 0))

    blk = pl.BlockSpec((tr, b0), lambda i: (i, 0))
    return _pcall(
        body, name, (a0 // tr,), [part_spec(k) for k in range(n_p)] + [blk, blk, blk],
        [blk] * 4, [_sds((a0, b0))] * 4)(*parts, w, m, v)


def _gate_up_adam(parts, ws, ms, vs):
    tc = 256

    def body(p_ref, wg, wu, mg, mu, vg, vu, *outs):
        g = _device_sum(p_ref)
        for k, (w_ref, m_ref, v_ref) in enumerate(((wg, mg, vg), (wu, mu, vu))):
            gk = g[k * FF_PAD:k * FF_PAD + FF_SH]
            delta, m2, v2 = _adamw(w_ref[...], gk, m_ref[...], v_ref[...])
            for o, val in zip(outs[4 * k:4 * k + 4], (gk, delta, m2, v2)):
                o[...] = val

    blk = pl.BlockSpec((FF_SH, tc), lambda i: (0, i))
    res = _pcall(
        body, "adam_w_gate_up", (D // tc,), [pl.BlockSpec((N_DEV, 2 * FF_PAD, tc), lambda i: (0, 0, i))] + [blk] * 6,
        [blk] * 8, [_sds((FF_SH, D))] * 8)(parts, *ws, *ms, *vs)
    return res[:4], res[4:]


def _fwd_in(x, g_pre, w_in_al, tm):
    T = x.shape[0]

    def body(x_ref, g_ref, w_ref, xm_ref, xh_ref, u_ref):
        nx, _ = _rms(x_ref[...], D)
        u = (nx * g_ref[...]).astype(BF)
        u_ref[...] = u
        xm_ref[...] = jnp.dot(u, w_ref[:, 0:MLA_IN], preferred_element_type=F32)
        xh_ref[...] = jnp.dot(u, w_ref[:, MLA_IN:XP_W], preferred_element_type=F32)

    return _pcall(body, "fwd_in", (T // tm,),
                  [_rows(tm, D), _full((1, D)), _full((D, XP_W))],
                  [_rows(tm, MLA_IN), _rows(tm, 4 * G_W), _rows(tm, D)],
                  [_sds((T, MLA_IN)), _sds((T, 4 * G_W)), _sds((T, D), BF)])(x, g_pre, w_in_al)


def _rope(blk, ta, tb1, tb2):
    return blk * ta + pltpu.roll(blk, HB - ROPE // 2, 1) * tb1 + pltpu.roll(blk, ROPE // 2, 1) * tb2


def _unrope(d, ta, tb1, tb2):
    return d * ta + pltpu.roll(d * tb1, ROPE // 2, 1) + pltpu.roll(d * tb2, HB - ROPE // 2, 1)


def _mla_prep(xp, tabs, g_q, g_kv, w_uq, w_uk, w_uv, tm):
    T = xp.shape[0]
    W = N_HEADS * HB

    def body(xp_ref, ta_ref, tb1_ref, tb2_ref, gq_ref, gkv_ref, wuq_ref, wuk_ref, wuv_ref, q_ref, qs_ref, k_ref, v_ref):
        ta, tb1, tb2 = ta_ref[...], tb1_ref[...], tb2_ref[...]
        nq, _ = _rms(xp_ref[:, 0:Q_RANK], Q_RANK)
        nkv, _ = _rms(xp_ref[:, Q_RANK:Q_RANK + KV_RANK], KV_RANK)
        nkv = (nkv * gkv_ref[...]).astype(BF)
        qpre = _dot(nq * gq_ref[...], wuq_ref[...])
        kpre = jnp.dot(nkv, wuk_ref[...], preferred_element_type=F32)
        v = jnp.dot(nkv, wuv_ref[...], preferred_element_type=F32)
        lane = lax.broadcasted_iota(jnp.int32, (tm, W), 1)
        v_ref[...] = jnp.where((lane & (HB - 1)) == V_DIM, 1.0, v).astype(BF)
        kr = _rope(pltpu.roll(xp_ref[:, Q_RANK + KV_RANK:MLA_IN], NOPE, 1), ta, tb1, tb2)
        for h in range(N_HEADS):
            sl = slice(h * HB, (h + 1) * HB)
            qr = _rope(qpre[:, sl], ta, tb1, tb2)
            q_ref[:, sl] = qr.astype(BF)
            qs_ref[:, sl] = (qr * (QK_DIM ** -0.5 * LOG2E)).astype(BF)
            k_ref[:, sl] = (kpre[:, sl] + kr).astype(BF)

    tab = _rows(tm, HB)
    return _pcall(body, "mla_prep", (T // tm,),
                  [_rows(tm, MLA_IN), tab, tab, tab, _full((1, Q_RANK)), _full((1, KV_RANK)),
                   _full((Q_RANK, W)), _full((KV_RANK, W)), _full((KV_RANK, W))],
                  [_rows(tm, W)] * 4, [_sds((T, W), BF)] * 4)(xp, *tabs, g_q, g_kv, w_uq, w_uk, w_uv)


def _flash_fwd(q, k, v, tq, exchange=None):
    T = q.shape[0]
    hp = _AH_FWD
    W = hp * HB

    def body(q_ref, k_ref, v_ref, o_ref, lse_ref):
        i = pl.program_id(1)

        def blk(j, carry, masked):
            st = pl.multiple_of(j * tq, tq)
            out = []
            for h in range(hp):
                ls = slice(h * HB, (h + 1) * HB)
                m, acc = carry[h]
                s = _dot_nt(q_ref[:, ls], k_ref[pl.ds(st, tq), ls])
                if masked:
                    r = lax.broadcasted_iota(jnp.int32, (tq, tq), 0)
                    c = lax.broadcasted_iota(jnp.int32, (tq, tq), 1)
                    s = jnp.where(c <= r, s, NEG)
                m2 = jnp.maximum(m, jnp.max(s, -1, keepdims=True))
                p = jnp.exp2(s - m2)
                out.append((m2, jnp.exp2(m - m2) * acc + _dot(p, v_ref[pl.ds(st, tq), ls])))
            return tuple(out)

        init = tuple((jnp.full((tq, 1), NEG, F32), jnp.zeros((tq, HB), F32)) for _ in range(hp))
        carry = lax.fori_loop(0, i, lambda j, cr: blk(j, cr, False), init)
        res = blk(i, carry, True)
        lane = lax.broadcasted_iota(jnp.int32, (tq, HB), 1)
        for h in range(hp):
            ls = slice(h * HB, (h + 1) * HB)
            m, acc = res[h]
            l = acc[:, V_DIM:V_DIM + 1]
            o_ref[:, ls] = jnp.where(lane < V_DIM, acc / l, 0.0).astype(BF)
            lse_ref[:, ls] = jnp.broadcast_to(m * (1.0 / LOG2E) + jnp.log(l), (tq, HB))

    qs = pl.BlockSpec((tq, W), lambda h, i: (i, h))
    kvs = pl.BlockSpec((T, W), lambda h, i: (0, h))
    return _pcall(body, "flash_fwd", (N_HEADS // hp, T // tq), [qs, kvs, kvs], [qs, qs],
                  [_sds((T, N_HEADS * HB), BF), _sds((T, N_HEADS * HB))], exchange=exchange)(q, k, v)


def _gates(hq, hf, lb):
    sig = _sigmoid(hf)
    f = lb + (1.0 - lb) * sig
    sq = _sigmoid(hq)
    return hq * sq, 1.0 - f, f, jnp.log(f), sig, sq


def _lower_bound(lbl_ref):
    l0, l1 = lbl_ref[0:1, :], lbl_ref[1:2, :]
    mx = jnp.maximum(l0, l1)
    e0, e1 = jnp.exp(l0 - mx), jnp.exp(l1 - mx)
    return e0 / (e0 + e1)


def _split3(x):
    hi = x.astype(BF)
    r1 = x - hi.astype(F32)
    mid = r1.astype(BF)
    lo = (r1 - mid.astype(F32)).astype(BF)
    return hi, mid, lo


def _tri_mm(tri, x):
    hi, mid, lo = _split3(x)
    mm = lambda t: jnp.dot(tri, t, preferred_element_type=F32)
    return mm(hi) + mm(mid) + mm(lo)


def _intra_codes(sub):
    row = lax.broadcasted_iota(jnp.int32, (CHUNK, CHUNK), 0)
    col = lax.broadcasted_iota(jnp.int32, (CHUNK, CHUNK), 1)
    return sub, row, col


def _intra(q, k, b2, b_s, codes, da=None):
    grad = da is not None
    pow2 = (lambda x: jnp.exp2(jnp.minimum(x, 0.0))) if grad else jnp.exp2
    sub, row, col = codes
    a = jnp.zeros((CHUNK, CHUNK), F32)
    dq = jnp.zeros((CHUNK, G_DIM), F32)
    dk = jnp.zeros((CHUNK, G_DIM), F32)
    for i in range(1, CHUNK // sub):
        b0 = b_s[sub * i - 1:sub * i, :]
        eq, ek = pow2(b2 - b0), pow2(b0 - b2)
        mask = ((row // sub) == i) & (col < sub * i)
        if grad:
            dai = jnp.where(mask, da, 0.0)
            dq = dq + _dot(dai, k * ek) * eq
            dk = dk + _dot_tn(dai, q * eq) * ek
        else:
            a = jnp.where(mask, _dot_nt(q * eq, k * ek), a)
    for d in range(sub):
        ksh = pltpu.roll(k, d, 0) if d else k
        bsh = pltpu.roll(b2, d, 0) if d else b2
        e = pow2(b2 - bsh)
        mask = (col == row - d) & ((row & (sub - 1)) >= d)
        if grad:
            g = jnp.sum(jnp.where(mask, da, 0.0), -1, keepdims=True) * e
            dq = dq + g * ksh
            cb = g * q
            dk = dk + (pltpu.roll(cb, CHUNK - d, 0) if d else cb)
        else:
            a = jnp.where(mask, jnp.sum(q * ksh * e, -1, keepdims=True), a)
    return (dq, dk) if grad else a


def _hgrn_fwd(xp, lb_logits, g_hn, exchange=None):
    T = xp.shape[0]
    tb = min(_TB, T)
    ncb = tb // CHUNK
    hp = _HP
    W = hp * G_DIM

    def body(hq_ref, hf_ref, hi_ref, hg_ref, lbl_ref, ghn_ref, out_ref, oraw_ref, sall_ref, aall_ref, ball_ref,
             st_ref, b_s):
        lb_all = _lower_bound(lbl_ref)

        @pl.when(pl.program_id(1) == 0)
        def _():
            st_ref[...] = jnp.zeros(st_ref.shape, F32)

        row = lax.broadcasted_iota(jnp.int32, (CHUNK, CHUNK), 0)
        col = lax.broadcasted_iota(jnp.int32, (CHUNK, CHUNK), 1)
        tri = (col <= row).astype(BF)
        codes = _intra_codes(SUB)

        def chunk(c, carry):
            sl = pl.ds(pl.multiple_of(c * CHUNK, CHUNK), CHUNK)
            for h in range(hp):
                ls = slice(h * G_DIM, (h + 1) * G_DIM)
                q, k, _, lf, _, _ = _gates(hq_ref[sl, ls], hf_ref[sl, ls], lb_all[:, ls])
                v = hi_ref[sl, ls]
                b2 = _tri_mm(tri, lf) * LOG2E
                b_s[h] = b2
                ball_ref[sl, ls] = b2
                st = st_ref[h]
                sall_ref[c, h] = st
                a = _intra(q, k, b2, b_s.at[h], codes)
                aall_ref[c, h] = a
                o = _dot_nt(q * jnp.exp2(b2), st) + _dot(a, v)
                bl = b_s[h, CHUNK - 1:CHUNK, :]
                st_ref[h] = st * jnp.exp2(bl) + _dot_tn(v, k * jnp.exp2(bl - b2))
                oraw_ref[sl, ls] = o
                n, _ = _rms(o, G_DIM)
                hg = hg_ref[sl, ls]
                out_ref[sl, ls] = n * ghn_ref[:, ls] * (hg * _sigmoid(hg))
            return carry

        lax.fori_loop(0, ncb, chunk, 0, unroll=4)

    col_blk = lambda j: pl.BlockSpec((tb, W), lambda p, t: (t, j * (G_HEADS // hp) + p))
    head = pl.BlockSpec((tb, W), lambda p, t: (t, p))
    return _pcall(
        body, "hgrn_fwd", (G_HEADS // hp, T // tb),
        [col_blk(0), col_blk(1), col_blk(2), col_blk(3),
         pl.BlockSpec((2, W), lambda p, t: (0, p)), pl.BlockSpec((1, W), lambda p, t: (0, p))],
        [head, head, pl.BlockSpec((ncb, hp, G_DIM, G_DIM), lambda p, t: (t, p, 0, 0)),
         pl.BlockSpec((ncb, hp, CHUNK, CHUNK), lambda p, t: (t, p, 0, 0)), head],
        [_sds((T, G_W)), _sds((T, G_W)), _sds((T // CHUNK, G_HEADS, G_DIM, G_DIM)),
         _sds((T // CHUNK, G_HEADS, CHUNK, CHUNK)), _sds((T, G_W))],
        scratch=[pltpu.VMEM((hp, G_DIM, G_DIM), F32), pltpu.VMEM((hp, CHUNK, G_DIM), F32)], exchange=exchange,
    )(xp, xp, xp, xp, lb_logits, g_hn)


def _fwd_out(o_pad, o_hgrn, x, g_on, w_out, g_post, g_fpre, tm):
    T = x.shape[0]

    def body(o_ref, oh_ref, x_ref, gon_ref, w_ref, gpost_ref, gfpre_ref, h1_ref, y1_ref, z_ref, mix_ref):
        for h in range(N_HEADS):
            sl = slice(h * HB, (h + 1) * HB)
            n, _ = _rms(o_ref[:, sl].astype(F32), V_DIM)
            mix_ref[:, sl] = (n * gon_ref[:, sl]).astype(BF)
        mix_ref[:, N_HEADS * HB:MIX_W] = oh_ref[...].astype(BF)
        y1 = jnp.dot(mix_ref[...], w_ref[...], preferred_element_type=F32)
        y1_ref[...] = y1
        ny, _ = _rms(y1, D)
        h1 = x_ref[...] + ny * gpost_ref[...]
        h1_ref[...] = h1
        nh, _ = _rms(h1, D)
        z_ref[...] = (nh * gfpre_ref[...]).astype(BF)

    return _pcall(body, "fwd_out", (T // tm,),
                  [_rows(tm, N_HEADS * HB), _rows(tm, G_W), _rows(tm, D), _full((1, N_HEADS * HB)),
                   _full((MIX_W, D)), _full((1, D)), _full((1, D))],
                  [_rows(tm, D), _rows(tm, D), _rows(tm, D), _rows(tm, MIX_W)],
                  [_sds((T, D)), _sds((T, D)), _sds((T, D), BF), _sds((T, MIX_W), BF)],
                  )(o_pad, o_hgrn, x, g_on, w_out, g_post, g_fpre)


def _ffn_fwd(z, wgu, wd, h1, tgt, g_fpost, tm, nd):
    T = z.shape[0]
    fb = nd * FF_PAD
    nf = wd.shape[0] // fb

    def body(z_ref, wgu_ref, wd_ref, h1_ref, t_ref, gp_ref,
             as_ref, bs_ref, ff_ref, dh2_ref, dy2_ref, dgp_ref, loss_ref, acc):
        i, j = pl.program_id(0), pl.program_id(1)
        gu = _dot_nt(z_ref[...], wgu_ref[...])
        piece = lambda n: gu[:, n * FF_PAD:(n + 1) * FF_PAD]
        g = piece(0) if nd == 1 else jnp.concatenate([piece(2 * n) for n in range(nd)], 1)
        u = piece(1) if nd == 1 else jnp.concatenate([piece(2 * n + 1) for n in range(nd)], 1)
        s = _sigmoid(g)
        b = g * s
        ff = (b * u).astype(BF)
        as_ref[...] = (u * _dsilu(g, s)).astype(BF)
        bs_ref[...] = b.astype(BF)
        ff_ref[...] = ff
        part = jnp.dot(ff, wd_ref[...], preferred_element_type=F32)

        @pl.when(j == 0)
        def _():
            acc[...] = part

        @pl.when(j > 0)
        def _():
            acc[...] += part

        @pl.when((i == 0) & (j == 0))
        def _():
            dgp_ref[...] = jnp.zeros(dgp_ref.shape, F32)
            loss_ref[...] = jnp.zeros(loss_ref.shape, F32)

        @pl.when(j == nf - 1)
        def _():
            ny, r = _rms(acc[...], D)
            err = h1_ref[...] + ny * gp_ref[...] - t_ref[...]
            loss_ref[...] += 0.5 * jnp.sum(jnp.sum(err * err, -1, keepdims=True) * (1.0 / D), 0, keepdims=True)
            dh2 = err * (1.0 / D)
            dh2_ref[...] = dh2
            dy2, dgp = _rms_bwd(ny, r, gp_ref[...], dh2, D)
            dy2_ref[...] = dy2.astype(BF)
            dgp_ref[...] += dgp

    tok = lambda n: pl.BlockSpec((tm, n), lambda i, j: (i, 0))
    col = pl.BlockSpec((tm, fb), lambda i, j: (i, j))
    return _pcall(
        body, "ffn_fwd", (T // tm, nf),
        [tok(D), pl.BlockSpec((2 * fb, D), lambda i, j: (j, 0)), pl.BlockSpec((fb, D), lambda i, j: (j, 0)),
         tok(D), tok(D), _full((1, D))],
        [col, col, col, tok(D), tok(D), _full((1, D)), _full((1, HB))],
        [_sds((T, nf * fb), BF)] * 3 + [_sds((T, D)), _sds((T, D), BF), _sds((1, D)), _sds((1, HB))],
        scratch=[pltpu.VMEM((tm, D), F32)],
    )(z, wgu, wd, h1, tgt, g_fpost)


def _dsilu(x, s):
    return s * (1.0 + x * (1.0 - s))


def _ffn_bwd_x(dy2, gs, us, wgu, wd, h1, y1, dh2, g_fpre, g_post, tm):
    T = dy2.shape[0]
    nf = wd.shape[0] // _FB

    def body(dy2_ref, gs_ref, us_ref, wgu_ref, wd_ref, h1_ref, y1_ref, dh2_ref, gf_ref, gp_ref,
             dgu_ref, dh1_ref, dy1_ref, dgf_ref, dgp_ref, acc):
        i, j = pl.program_id(0), pl.program_id(1)
        dff = _dot_nt(dy2_ref[...], wd_ref[...])
        dg = (dff * gs_ref[...].astype(F32)).astype(BF)
        du = (dff * us_ref[...].astype(F32)).astype(BF)
        dgu = jnp.concatenate([dg[:, 0:FF_PAD], du[:, 0:FF_PAD], dg[:, FF_PAD:_FB], du[:, FF_PAD:_FB]], 1)
        dgu_ref[...] = dgu
        part = jnp.dot(dgu, wgu_ref[...], preferred_element_type=F32)

        @pl.when(j == 0)
        def _():
            acc[...] = part

        @pl.when(j > 0)
        def _():
            acc[...] += part

        @pl.when((i == 0) & (j == 0))
        def _():
            dgf_ref[...] = jnp.zeros(dgf_ref.shape, F32)
            dgp_ref[...] = jnp.zeros(dgp_ref.shape, F32)

        @pl.when(j == nf - 1)
        def _():
            nh, rh = _rms(h1_ref[...], D)
            dh, dgf = _rms_bwd(nh, rh, gf_ref[...], acc[...], D)
            dh1 = dh2_ref[...] + dh
            dh1_ref[...] = dh1
            dgf_ref[...] += dgf
            ny, ry = _rms(y1_ref[...], D)
            dy1, dgp = _rms_bwd(ny, ry, gp_ref[...], dh1, D)
            dy1_ref[...] = dy1.astype(BF)
            dgp_ref[...] += dgp

    tok = lambda n: pl.BlockSpec((tm, n), lambda i, j: (i, 0))
    col = pl.BlockSpec((tm, _FB), lambda i, j: (i, j))
    return _pcall(
        body, "ffn_bwd_x", (T // tm, nf),
        [tok(D), col, col, pl.BlockSpec((2 * _FB, D), lambda i, j: (j, 0)), pl.BlockSpec((_FB, D), lambda i, j: (j, 0)),
         tok(D), tok(D), tok(D), _full((1, D)), _full((1, D))],
        [pl.BlockSpec((tm, 2 * _FB), lambda i, j: (i, j)), tok(D), tok(D), _full((1, D)), _full((1, D))],
        [_sds((T, 2 * nf * _FB), BF), _sds((T, D)), _sds((T, D), BF), _sds((1, D)), _sds((1, D))],
        scratch=[pltpu.VMEM((tm, D), F32)],
    )(dy2, gs, us, wgu, wd, h1, y1, dh2, g_fpre, g_post)


def _ffn_bwd_w(z, ffs, dgu, dy2, tm):
    T = z.shape[0]
    nf = ffs.shape[1] // _FB
    nt = T // tm

    def body(z_ref, ff_ref, dgu_ref, dy2_ref, dwgu_ref, dwd_ref, agu, ad):
        i = pl.program_id(1)
        pgu = _dot_tn(dgu_ref[...], z_ref[...])
        pd = _dot_tn(ff_ref[...], dy2_ref[...])

        @pl.when(i == 0)
        def _():
            agu[...] = pgu
            ad[...] = pd

        @pl.when(i > 0)
        def _():
            agu[...] += pgu
            ad[...] += pd

        @pl.when(i == nt - 1)
        def _():
            dwgu_ref[...] = agu[...].astype(BF)
            dwd_ref[...] = ad[...].astype(BF)

    F = nf * _FB
    tok = lambda n: pl.BlockSpec((tm, n), lambda j, i: (i, 0))
    return _pcall(
        body, "ffn_bwd_w", (nf, nt),
        [tok(D), pl.BlockSpec((tm, _FB), lambda j, i: (i, j)), pl.BlockSpec((tm, 2 * _FB), lambda j, i: (i, j)), tok(D)],
        [pl.BlockSpec((2 * _FB, D), lambda j, i: (j, 0)), pl.BlockSpec((_FB, D), lambda j, i: (j, 0))],
        [_sds((2 * F, D), BF), _sds((F, D), BF)],
        scratch=[pltpu.VMEM((2 * _FB, D), F32), pltpu.VMEM((_FB, D), F32)],
    )(z, ffs, dgu, dy2)


def _out_bwd(dy1, mix, o_pad, w_out, g_on, tm):
    T = dy1.shape[0]
    W = N_HEADS * HB

    def body(dy1_ref, mix_ref, o_ref, w_ref, gon_ref, do_ref, dl_ref, dohg_ref, dw_ref, dgon_ref):
        i = pl.program_id(0)
        dy1v = dy1_ref[...]
        dmix = _dot_nt(dy1v, w_ref[...])
        pw = _dot_tn(mix_ref[...], dy1v)

        @pl.when(i == 0)
        def _():
            dw_ref[...] = pw
            dgon_ref[...] = jnp.zeros(dgon_ref.shape, F32)

        @pl.when(i > 0)
        def _():
            dw_ref[...] += pw

        for h in range(N_HEADS):
            sl = slice(h * HB, (h + 1) * HB)
            ov = o_ref[:, sl].astype(F32)
            n, r = _rms(ov, V_DIM)
            do, dg = _rms_bwd(n, r, gon_ref[:, sl], dmix[:, sl], V_DIM)
            dgon_ref[:, sl] += dg
            do_ref[:, sl] = do.astype(BF)
            dl_ref[:, sl] = jnp.broadcast_to(jnp.sum(do * ov, -1, keepdims=True), (tm, HB))
        dohg_ref[...] = dmix[:, W:MIX_W]

    return _pcall(body, "out_bwd", (T // tm,),
                  [_rows(tm, D), _rows(tm, MIX_W), _rows(tm, W), _full((MIX_W, D)), _full((1, W))],
                  [_rows(tm, W), _rows(tm, W), _rows(tm, G_W), _full((MIX_W, D)), _full((1, W))],
                  [_sds((T, W), BF), _sds((T, W)), _sds((T, G_W)), _sds((MIX_W, D)), _sds((1, W))],
                  )(dy1, mix, o_pad, w_out, g_on)


def _flash_bwd(q, k, v, do, lse, dl, tq, exchange=None):
    T = q.shape[0]
    nq = T // tq
    scale = QK_DIM ** -0.5
    hp = _AH
    W = hp * HB

    def body(k_ref, v_ref, q_ref, do_ref, lse_ref, dl_ref, dk_ref, dv_ref, dq_ref):
        j = pl.program_id(1)

        @pl.when(j == 0)
        def _():
            dq_ref[...] = jnp.zeros(dq_ref.shape, F32)

        def blk(i, carry, masked):
            sl = pl.ds(pl.multiple_of(i * tq, tq), tq)
            out = []
            for h in range(hp):
                ls = slice(h * HB, (h + 1) * HB)
                dk, dv = carry[h]
                kv, vv = k_ref[:, ls], v_ref[:, ls]
                qv, dov = q_ref[sl, ls], do_ref[sl, ls]
                s = _dot_nt(qv, kv) * scale
                if masked:
                    r = lax.broadcasted_iota(jnp.int32, (tq, tq), 0)
                    c = lax.broadcasted_iota(jnp.int32, (tq, tq), 1)
                    s = jnp.where(c <= r, s, NEG)
                p = jnp.exp(s - lse_ref[sl, h * HB:h * HB + 1])
                ds = p * (_dot_nt(dov, vv) - dl_ref[sl, h * HB:h * HB + 1]) * scale
                dq_ref[sl, ls] += _dot(ds, kv)
                out.append((dk + _dot_tn(ds, qv), dv + _dot_tn(p, dov)))
            return tuple(out)

        zero = jnp.zeros((tq, HB), F32)
        carry = blk(j, tuple((zero, zero) for _ in range(hp)), True)
        res = lax.fori_loop(j + 1, nq, lambda i, cr: blk(i, cr, False), carry)
        for h in range(hp):
            ls = slice(h * HB, (h + 1) * HB)
            dk_ref[:, ls] = res[h][0].astype(BF)
            dv_ref[:, ls] = res[h][1].astype(BF)

    tile = pl.BlockSpec((tq, W), lambda h, j: (j, h))
    whole = pl.BlockSpec((T, W), lambda h, j: (0, h))
    return _pcall(body, "flash_bwd", (N_HEADS // hp, nq), [tile, tile, whole, whole, whole, whole],
                  [tile, tile, whole], [_sds((T, N_HEADS * HB), BF)] * 2 + [_sds((T, N_HEADS * HB))],
                  exchange=exchange)(k, v, q, do, lse, dl)


def _mla_prep_bwd(xp, tabs, dq, dk, dv, g_q, g_kv, w_uq, w_uk, w_uv, tm):
    T = xp.shape[0]
    W = N_HEADS * HB

    def body(xp_ref, ta_ref, tb1_ref, tb2_ref, dq_ref, dk_ref, dv_ref, gq_ref, gkv_ref, wuq_ref, wuk_ref, wuv_ref,
             dxp_ref, dwuq_ref, dwuk_ref, dwuv_ref, dgq_ref, dgkv_ref, dqp):
        i = pl.program_id(0)
        ta, tb1, tb2 = ta_ref[...], tb1_ref[...], tb2_ref[...]
        nq, rq = _rms(xp_ref[:, 0:Q_RANK], Q_RANK)
        nkv, rkv = _rms(xp_ref[:, Q_RANK:Q_RANK + KV_RANK], KV_RANK)
        dkr = jnp.zeros((tm, HB), F32)
        for h in range(N_HEADS):
            sl = slice(h * HB, (h + 1) * HB)
            dqp[:, sl] = _unrope(dq_ref[:, sl], ta, tb1, tb2).astype(BF)
            dkr = dkr + dk_ref[:, sl].astype(F32)
        dkr = pltpu.roll(_unrope(dkr, ta, tb1, tb2), HB - NOPE, 1)
        lane = lax.broadcasted_iota(jnp.int32, (tm, HB), 1)
        dxp_ref[:, Q_RANK + KV_RANK:MLA_IN] = jnp.where(lane < ROPE, dkr, 0.0)
        dqpv = dqp[...]
        dkv, dvv = dk_ref[...], dv_ref[...]
        nqs = (nq * gq_ref[...]).astype(BF)
        nkvs = (nkv * gkv_ref[...]).astype(BF)
        pq, pk, pv = _dot_tn(nqs, dqpv), _dot_tn(nkvs, dkv), _dot_tn(nkvs, dvv)
        dcq, dgq = _rms_bwd(nq, rq, gq_ref[...], _dot_nt(dqpv, wuq_ref[...]), Q_RANK)
        dckv, dgkv = _rms_bwd(nkv, rkv, gkv_ref[...], _dot_nt(dkv, wuk_ref[...]) + _dot_nt(dvv, wuv_ref[...]), KV_RANK)
        dxp_ref[:, 0:Q_RANK] = dcq
        dxp_ref[:, Q_RANK:Q_RANK + KV_RANK] = dckv

        @pl.when(i == 0)
        def _():
            dwuq_ref[...] = pq
            dwuk_ref[...] = pk
            dwuv_ref[...] = pv
            dgq_ref[...] = dgq
            dgkv_ref[...] = dgkv

        @pl.when(i > 0)
        def _():
            dwuq_ref[...] += pq
            dwuk_ref[...] += pk
            dwuv_ref[...] += pv
            dgq_ref[...] += dgq
            dgkv_ref[...] += dgkv

    tab = _rows(tm, HB)
    return _pcall(
        body, "mla_prep_bwd", (T // tm,),
        [_rows(tm, MLA_IN), tab, tab, tab, _rows(tm, W), _rows(tm, W), _rows(tm, W), _full((1, Q_RANK)),
         _full((1, KV_RANK)), _full((Q_RANK, W)), _full((KV_RANK, W)), _full((KV_RANK, W))],
        [_rows(tm, MLA_IN), _full((Q_RANK, W)), _full((KV_RANK, W)), _full((KV_RANK, W)), _full((1, Q_RANK)),
         _full((1, KV_RANK))],
        [_sds((T, MLA_IN)), _sds((Q_RANK, W)), _sds((KV_RANK, W)), _sds((KV_RANK, W)), _sds((1, Q_RANK)),
         _sds((1, KV_RANK))],
        scratch=[pltpu.VMEM((tm, W), BF)],
    )(xp, *tabs, dq, dk, dv, g_q, g_kv, w_uq, w_uk, w_uv)


def _hgrn_bwd(xp, o_raw, s_all, a_all, b_all, d_out, lb_logits, g_hn, exchange=None):
    T = xp.shape[0]
    tb = min(_TB_BWD, T)
    ncb = tb // CHUNK
    nb = T // tb
    hp = _HP
    W = hp * G_DIM

    def body(hq_ref, hf_ref, hi_ref, hg_ref, o_ref, sall_ref, aall_ref, ball_ref, dout_ref, lbl_ref, ghn_ref,
             dhq_ref, dhf_ref, dhi_ref, dhg_ref, dlbl_ref, dghn_ref, dst_ref, b_s, acc_lb, acc_g):
        t = pl.program_id(1)
        lb_all = _lower_bound(lbl_ref)

        @pl.when(t == 0)
        def _():
            dst_ref[...] = jnp.zeros(dst_ref.shape, F32)
            acc_lb[...] = jnp.zeros(acc_lb.shape, F32)
            acc_g[...] = jnp.zeros(acc_g.shape, F32)

        row = lax.broadcasted_iota(jnp.int32, (CHUNK, CHUNK), 0)
        col = lax.broadcasted_iota(jnp.int32, (CHUNK, CHUNK), 1)
        tri_t = (col >= row).astype(BF)
        codes = _intra_codes(SUB)
        last = lax.broadcasted_iota(jnp.int32, (CHUNK, G_DIM), 0) == CHUNK - 1

        def chunk(cc, carry):
            c = ncb - 1 - cc
            sl = pl.ds(pl.multiple_of(c * CHUNK, CHUNK), CHUNK)
            for h in range(hp):
                ls = slice(h * G_DIM, (h + 1) * G_DIM)
                lb, ghn = lb_all[:, ls], ghn_ref[:, ls]
                hq, hg = hq_ref[sl, ls], hg_ref[sl, ls]
                q, k, f, _, sig, sq = _gates(hq, hf_ref[sl, ls], lb)
                v = hi_ref[sl, ls]
                b2 = ball_ref[sl, ls]
                b_s[h] = b2
                st = sall_ref[c, h]
                dstn = dst_ref[h]
                o = o_ref[sl, ls]
                dout = dout_ref[sl, ls]
                n, r = _rms(o, G_DIM)
                sg = _sigmoid(hg)
                dhg_ref[sl, ls] = dout * (n * ghn) * _dsilu(hg, sg)
                do, dg = _rms_bwd(n, r, ghn, dout * (hg * sg), G_DIM)
                acc_g[:, ls] += dg
                eb = jnp.exp2(b2)
                bl = b_s[h, CHUNK - 1:CHUNK, :]
                ebl = jnp.exp2(bl)
                ekd = jnp.exp2(bl - b2)
                kd = k * ekd
                a = aall_ref[c, h]
                dq_i, dk_i = _intra(q, k, b2, b_s.at[h], codes, _dot_nt(do, v))
                dhi_ref[sl, ls] = _dot_tn(a, do) + _dot_nt(kd, dstn)
                dk_state = _dot(v, dstn) * ekd
                dq = dq_i + _dot(do, st) * eb
                dk = dk_i + dk_state
                dbl = jnp.sum(k * dk_state, 0, keepdims=True) + ebl * jnp.sum(dstn * st, 0, keepdims=True)
                db = q * dq - k * dk + jnp.where(last, dbl, 0.0)
                df = _tri_mm(tri_t, db) / f - dk
                dhf_ref[sl, ls] = df * (1.0 - lb) * sig * (1.0 - sig)
                acc_lb[:, ls] += jnp.sum(df * (1.0 - sig), 0, keepdims=True)
                dhq_ref[sl, ls] = dq * _dsilu(hq, sq)
                dst_ref[h] = dstn * ebl + _dot_tn(do, q * eb)
            return carry

        lax.fori_loop(0, ncb, chunk, 0, unroll=4)

        @pl.when(t == nb - 1)
        def _():
            dl0 = acc_lb[...] * lb_all * (1.0 - lb_all)
            dlbl_ref[0:1, :] = dl0
            dlbl_ref[1:2, :] = -dl0
            dghn_ref[...] = acc_g[...]

    col_blk = lambda j: pl.BlockSpec((tb, W), lambda p, t: (nb - 1 - t, j * (G_HEADS // hp) + p))
    head = pl.BlockSpec((tb, W), lambda p, t: (nb - 1 - t, p))
    two = pl.BlockSpec((2, W), lambda p, t: (0, p))
    one = pl.BlockSpec((1, W), lambda p, t: (0, p))
    res = _pcall(
        body, "hgrn_bwd", (G_HEADS // hp, nb),
        [col_blk(0), col_blk(1), col_blk(2), col_blk(3), head,
         pl.BlockSpec((ncb, hp, G_DIM, G_DIM), lambda p, t: (nb - 1 - t, p, 0, 0)),
         pl.BlockSpec((ncb, hp, CHUNK, CHUNK), lambda p, t: (nb - 1 - t, p, 0, 0)), head, head, two, one],
        [head, head, head, head, two, one],
        [_sds((T, G_W))] * 4 + [_sds((2, G_W)), _sds((1, G_W))],
        scratch=[pltpu.VMEM((hp, G_DIM, G_DIM), F32), pltpu.VMEM((hp, CHUNK, G_DIM), F32),
                 pltpu.VMEM((1, W), F32), pltpu.VMEM((1, W), F32)], exchange=exchange,
    )(xp, xp, xp, xp, o_raw, s_all, a_all, b_all, d_out, lb_logits, g_hn)
    return res


def _in_bwd_x(x, dxp_m, dxp_h, dh1, w_in_al, g_pre, vecs, tm, exchange=None):
    T = x.shape[0]
    nt = T // tm
    n_v = len(vecs)
    offs, rows = _row_offsets([g_pre] + list(vecs))

    def body(*refs):
        x_ref, dm_ref, d0_ref, d1_ref, d2_ref, d3_ref, dh1_ref, w_ref, g_ref = refs[:9]
        v_refs = refs[9:9 + n_v]
        dx_ref, dg_ref, rall, pk, send_sems, recv_sems, loc_sem = refs[9 + n_v:]
        i = pl.program_id(0)
        du = _dot_nt(dm_ref[...], w_ref[:, 0:MLA_IN])
        for j, d_ref in enumerate((d0_ref, d1_ref, d2_ref, d3_ref)):
            du = du + _dot_nt(d_ref[...], w_ref[:, MLA_IN + j * G_W:MLA_IN + (j + 1) * G_W])
        nx, r = _rms(x_ref[...], D)
        dx, dg = _rms_bwd(nx, r, g_ref[...], du, D)
        dx_ref[...] = dh1_ref[...] + dx

        @pl.when(i == 0)
        def _():
            dg_ref[...] = dg

        @pl.when(i > 0)
        def _():
            dg_ref[...] += dg

        @pl.when(i == nt - 1)
        def _():
            mx, my, mc = lax.axis_index("x"), lax.axis_index("y"), lax.axis_index("c")
            me = 4 * mx + 2 * my + mc
            pk[...] = jnp.zeros(pk.shape, F32)
            for p, v_ref in enumerate((dg_ref,) + tuple(v_refs)):
                vr, vn = v_ref.shape
                pk[offs[p]:offs[p] + vr, 0:vn] = v_ref[...]

            def copy(k, outgoing):
                px, py, pc = _peer(k, mx, my, mc)
                return pltpu.make_async_remote_copy(
                    src_ref=pk, dst_ref=rall.at[me if outgoing else 4 * px + 2 * py + pc],
                    send_sem=send_sems.at[k - 1], recv_sem=recv_sems.at[k - 1],
                    device_id=(px, py, pc), device_id_type=MESH)

            local = pltpu.make_async_copy(pk, rall.at[me], loc_sem)
            local.start()
            for k in range(1, N_DEV):
                copy(k, True).start()
            local.wait()
            for k in range(1, N_DEV):
                copy(k, False).wait_recv()
            for k in range(1, N_DEV):
                copy(k, True).wait_send()

    return _pcall(body, "in_bwd_x", (nt,),
                  [_rows(tm, D), _rows(tm, MLA_IN)] + [_rows(tm, G_W)] * 4 + [_rows(tm, D), _full((D, XP_W)), _full((1, D))]
                  + [_full(v.shape) for v in vecs],
                  [_rows(tm, D), _full((1, D)), _HBM], [_sds((T, D)), _sds((1, D)), _sds((N_DEV, rows, D))],
                  scratch=[pltpu.VMEM((rows, D), F32), pltpu.SemaphoreType.DMA((N_DEV - 1,)),
                           pltpu.SemaphoreType.DMA((N_DEV - 1,)), pltpu.SemaphoreType.DMA],
                  exchange=exchange)(x, dxp_m, *dxp_h, dh1, w_in_al, g_pre, *vecs)


def _aligned_col(c):
    return jnp.where(c < Q_RANK + KV_RANK + ROPE, c, c + (KR_PAD - ROPE))


def _align_w_in(g_in):
    tile = 384
    kr_end = Q_RANK + KV_RANK + ROPE

    def body(g_ref, o_ref, gp):
        gp[...] = jnp.zeros(gp.shape, BF)
        for j in range(N_DEV):
            gp[j, :, 0:IN_SH] = g_ref[j]
        r = lax.broadcasted_iota(jnp.int32, (tile, tile), 0)
        c = lax.broadcasted_iota(jnp.int32, (tile, tile), 1)
        for t in range(XP_W // tile):
            lo, hi = t * tile, (t + 1) * tile
            cols = [a if a < kr_end else a - (KR_PAD - ROPE) for a in (lo, hi - 1)]
            acc = jnp.zeros((D, tile), F32)
            for j in range(cols[0] // IN_SH, cols[-1] // IN_SH + 1):
                sel = (r < IN_SH) & (_aligned_col(j * IN_SH + r) == lo + c)
                acc = acc + jnp.dot(gp[j], sel.astype(BF), preferred_element_type=F32)
            o_ref[:, lo:hi] = acc.astype(BF)

    vm = pl.BlockSpec(memory_space=pltpu.VMEM)
    return pl.pallas_call(
        body, name="align_w_in", in_specs=[vm], out_specs=vm, out_shape=_sds((D, XP_W), BF),
        scratch_shapes=[pltpu.VMEM((N_DEV, D, tile), BF)],
        compiler_params=pltpu.CompilerParams(vmem_limit_bytes=_VMEM_LIMIT))(g_in)


def _in_bwd_w(name, u, dxp_m, dxp_h, tm, row0, nr, exchange=None):
    T = u.shape[0]
    nt = T // tm
    win = 640

    def body(u_ref, dm_ref, d0_ref, d1_ref, d2_ref, d3_ref, o_ref, acc):
        i = pl.program_id(0)
        ut = u_ref[:, row0:row0 + nr].T
        parts = [(0, MLA_IN, dm_ref)] + [(MLA_IN + j * G_W, G_W, d) for j, d in enumerate((d0_ref, d1_ref, d2_ref, d3_ref))]

        @pl.when(i == 0)
        def _():
            for lo, n, d in parts:
                acc[:, lo:lo + n] = jnp.dot(ut, d[...].astype(BF), preferred_element_type=F32)

        @pl.when(i > 0)
        def _():
            for lo, n, d in parts:
                acc[:, lo:lo + n] += jnp.dot(ut, d[...].astype(BF), preferred_element_type=F32)

        @pl.when(i == nt - 1)
        def _():
            wide = 384
            r = lax.broadcasted_iota(jnp.int32, (win, wide), 0)
            c = lax.broadcasted_iota(jnp.int32, (win, wide), 1)
            kr_end = Q_RANK + KV_RANK + ROPE
            for j in range(N_DEV):
                first = j * IN_SH if j * IN_SH < kr_end else j * IN_SH + (KR_PAD - ROPE)
                lo = min(first // HB * HB, XP_W - win)
                sel = (c < IN_SH) & (_aligned_col(j * IN_SH + c) == lo + r)
                res = jnp.dot(acc[:, lo:lo + win].astype(BF), sel.astype(BF), preferred_element_type=F32)
                o_ref[j] = res[:, 0:IN_SH].astype(BF)

    return _pcall(body, name, (nt,),
                  [_rows(tm, D), _rows(tm, MLA_IN)] + [_rows(tm, G_W)] * 4,
                  [_full((N_DEV, nr, IN_SH))], [_sds((N_DEV, nr, IN_SH), BF)],
                  scratch=[pltpu.VMEM((nr, XP_W), F32)], exchange=exchange)(u, dxp_m, *dxp_h)


def _pad_heads(w, width, real):
    lead = w.shape[:-1]
    w = w.reshape(lead + (N_HEADS, real))
    w = jnp.pad(w, [(0, 0)] * len(lead) + [(0, 0), (0, width - real)])
    return w.reshape(lead + (N_HEADS * width,))


def _unpad_heads(w, width, real):
    lead = w.shape[:-1]
    return w.reshape(lead + (N_HEADS, width))[..., :real].reshape(lead + (N_HEADS * real,))


def _rope_tables(positions):
    half = ROPE // 2
    inv_freq = 1.0 / (ROPE_THETA ** (jnp.arange(0, ROPE, 2, dtype=F32) / ROPE))
    ang = positions.astype(F32)[:, None] * inv_freq
    cos, sin = jnp.cos(ang), jnp.sin(ang)
    T = positions.shape[0]
    z = lambda n: jnp.zeros((T, n), F32)
    ta = jnp.concatenate([jnp.ones((T, NOPE), F32), cos, cos, z(HB - QK_DIM)], 1)
    tb1 = jnp.concatenate([z(NOPE), -sin, z(half), z(HB - QK_DIM)], 1)
    tb2 = jnp.concatenate([z(NOPE), z(half), sin, z(HB - QK_DIM)], 1)
    return ta, tb1, tb2


def kernel(x, positions, attn_pre_norm, w_in, mla_q_norm, mla_w_uq, mla_kv_norm, mla_w_ukv, mla_out_norm, hgrn_lb_logits, hgrn_out_norm, w_out, attn_post_norm, ffn_pre_norm, w_gate, w_up, w_down, ffn_post_norm, loss_target, m_attn_pre_norm, m_w_in, m_mla_q_norm, m_mla_w_uq, m_mla_kv_norm, m_mla_w_ukv, m_mla_out_norm, m_hgrn_lb_logits, m_hgrn_out_norm, m_w_out, m_attn_post_norm, m_ffn_pre_norm, m_w_gate, m_w_up, m_w_down, m_ffn_post_norm, v_attn_pre_norm, v_w_in, v_mla_q_norm, v_mla_w_uq, v_mla_kv_norm, v_mla_w_ukv, v_mla_out_norm, v_hgrn_lb_logits, v_hgrn_out_norm, v_w_out, v_attn_post_norm, v_ffn_pre_norm, v_w_gate, v_w_up, v_w_down, v_ffn_post_norm):
    T = x.shape[1]
    tm = min(_TM, T)
    tq = min(_TQ, T)
    xs, tgt = x[0], loss_target[0]
    uq_sh = (Q_RANK // N_DEV, N_HEADS * QK_DIM)

    b_in, b_uq, b_out, b_gu, b_d = _cast_shards(
        w_in[0], mla_w_uq[0].reshape(uq_sh), w_out[0], w_gate[0].T, w_up[0].T, w_down[0])
    g_in, g_uq = _gather_two_level("ag_first", [b_in, b_uq])
    w_in_al = _align_w_in(g_in)
    w_uq_p = _pad_heads(g_uq.reshape(Q_RANK, N_HEADS * QK_DIM), HB, QK_DIM)
    w_ukv = mla_w_ukv[0].astype(BF)
    w_uk_p = _pad_heads(w_ukv[..., :NOPE].reshape(KV_RANK, N_HEADS * NOPE), HB, NOPE)
    w_uv_p = _pad_heads(w_ukv[..., NOPE:].reshape(KV_RANK, N_HEADS * V_DIM), HB, V_DIM)
    g_on_p = _pad_heads(mla_out_norm, HB, V_DIM)
    tabs = _rope_tables(positions[0])

    xp_m, xp_h, u = _fwd_in(xs, attn_pre_norm, w_in_al, tm)
    q_att, qs_att, k_att, v_att = _mla_prep(xp_m, tabs, mla_q_norm, mla_kv_norm, w_uq_p, w_uk_p, w_uv_p, tm)
    o_hgrn, o_raw, s_all, a_all, b_all, wd = _hgrn_fwd(xp_h, hgrn_lb_logits, hgrn_out_norm, ([GATHER], [b_d]))
    wd = wd.reshape(N_DEV * FF_PAD, D)
    o_pad, lse, wgu, g_out = _flash_fwd(qs_att, k_att, v_att, tq, ([GATHER, GATHER], [b_gu, b_out]))
    wgu = wgu.reshape(N_DEV * 2 * FF_PAD, D)
    w_out_full = g_out.reshape(D, D)
    w_out_mla = jnp.pad(w_out_full[:N_HEADS * V_DIM].reshape(N_HEADS, V_DIM, D), ((0, 0), (0, HB - V_DIM), (0, 0)))
    w_out_p = jnp.concatenate([w_out_mla.reshape(N_HEADS * HB, D), w_out_full[N_HEADS * V_DIM:]], 0)
    h1, y1, z, mix = _fwd_out(o_pad, o_hgrn, xs, g_on_p, w_out_p, attn_post_norm, ffn_pre_norm, tm)
    tmf = min(_TMF, T)
    gs, us, ffs, dh2, dy2, d_fpost, loss_row = _ffn_fwd(z, wgu, wd, h1, tgt, ffn_post_norm, tm, _FB // FF_PAD)

    dgu, dh1, dy1, d_fpre, d_post = _ffn_bwd_x(dy2, gs, us, wgu, wd, h1, y1, dh2, ffn_pre_norm, attn_post_norm, tm)
    dwgu, dwd = _ffn_bwd_w(z, ffs, dgu, dy2, tmf)
    do_pad, dl, d_ohg, dw_out_p, d_on_p = _out_bwd(dy1, mix, o_pad, w_out_p, g_on_p, tm)
    dw_out_mla = dw_out_p[:N_HEADS * HB].reshape(N_HEADS, HB, D)[:, :V_DIM].reshape(N_HEADS * V_DIM, D)
    dw_out = jnp.concatenate([dw_out_mla, dw_out_p[N_HEADS * HB:]], 0).reshape(N_DEV, D // N_DEV, D).astype(BF)
    dk_att, dv_att, dq_att, p_gu, p_d, p_out = _flash_bwd(
        q_att, k_att, v_att, do_pad, lse, dl, tq,
        ([SCATTER] * 3, [dwgu.reshape(N_DEV, 2 * FF_PAD, D), dwd.reshape(N_DEV, FF_PAD, D), dw_out]))
    dxp_m, dw_uq_p, dw_uk_p, dw_uv_p, d_gq, d_gkv = _mla_prep_bwd(
        xp_m, tabs, dq_att, dk_att, dv_att, mla_q_norm, mla_kv_norm, w_uq_p, w_uk_p, w_uv_p, tm)
    dw_uq = _unpad_heads(dw_uq_p, HB, QK_DIM).reshape((N_DEV,) + uq_sh).astype(BF)
    dw_ukv = jnp.concatenate([_unpad_heads(dw_uk_p, HB, NOPE).reshape(KV_RANK, N_HEADS, NOPE),
                              _unpad_heads(dw_uv_p, HB, V_DIM).reshape(KV_RANK, N_HEADS, V_DIM)], -1)
    *dxp_h, d_lbl, d_ghn, p_uq, dw_ukv_all = _hgrn_bwd(
        xp_h, o_raw, s_all, a_all, b_all, d_ohg, hgrn_lb_logits, hgrn_out_norm,
        ([SCATTER, GATHER], [dw_uq, dw_ukv.reshape(KV_RANK, N_HEADS * HB)]))
    dw_in_a, = _in_bwd_w("in_bwd_w_a", u, dxp_m, dxp_h, tm, 0, _IN_ROWS_A)
    dw_in_b, p_in_a = _in_bwd_w("in_bwd_w_b", u, dxp_m, dxp_h, tm, _IN_ROWS_A, D - _IN_ROWS_A, ([SCATTER], [dw_in_a]))
    d_on = _unpad_heads(d_on_p, HB, V_DIM)
    vecs = [d_gq, d_gkv, d_on, d_lbl, d_ghn, d_post, d_fpre, d_fpost, loss_row]
    grad_x, _, rall, p_in_b = _in_bwd_x(xs, dxp_m, dxp_h, dh1, w_in_al, attn_pre_norm, vecs, tm, ([SCATTER], [dw_in_b]))

    ukv2 = lambda a: a.reshape(KV_RANK, N_HEADS * HB)
    small_w = [attn_pre_norm, mla_q_norm, mla_kv_norm, ukv2(mla_w_ukv), mla_out_norm, hgrn_lb_logits, hgrn_out_norm,
               attn_post_norm, ffn_pre_norm, ffn_post_norm]
    small_m = [m_attn_pre_norm, m_mla_q_norm, m_mla_kv_norm, ukv2(m_mla_w_ukv), m_mla_out_norm, m_hgrn_lb_logits,
               m_hgrn_out_norm, m_attn_post_norm, m_ffn_pre_norm, m_ffn_post_norm]
    small_v = [v_attn_pre_norm, v_mla_q_norm, v_mla_kv_norm, ukv2(v_mla_w_ukv), v_mla_out_norm, v_hgrn_lb_logits,
               v_hgrn_out_norm, v_attn_post_norm, v_ffn_pre_norm, v_ffn_post_norm]
    s_g, s_d, s_m, s_v, loss_all = _small_adam(rall, dw_ukv_all, 3, small_w, small_m, small_v)
    r_in = _shard_adam("adam_w_in", [p_in_a, p_in_b], w_in[0], m_w_in[0], v_w_in[0], 128)
    r_uq = _shard_adam("adam_w_uq", [p_uq], mla_w_uq[0].reshape(uq_sh), m_mla_w_uq[0].reshape(uq_sh),
                       v_mla_w_uq[0].reshape(uq_sh), uq_sh[0])
    r_out = _shard_adam("adam_w_out", [p_out], w_out[0], m_w_out[0], v_w_out[0], D // N_DEV)
    r_g, r_u = _gate_up_adam(p_gu, (w_gate[0].T, w_up[0].T), (m_w_gate[0].T, m_w_up[0].T),
                             (v_w_gate[0].T, v_w_up[0].T))
    r_g, r_u = [a.T for a in r_g], [a.T for a in r_u]
    r_d = _shard_adam("adam_w_down", [p_d], w_down[0], m_w_down[0], v_w_down[0], FF_SH // 2)

    loss = loss_all[0, 0]

    def assemble(big, small):
        b_in, b_uq, b_out, b_g, b_u, b_d = big
        return [small[0], b_in[None], small[1], b_uq.reshape(mla_w_uq.shape), small[2],
                small[3].reshape(mla_w_ukv.shape), small[4], small[5], small[6], b_out[None], small[7], small[8],
                b_g[None], b_u[None], b_d[None], small[9]]

    outs = [loss, grad_x[None]]
    for idx, small in enumerate((s_g, s_d, s_m, s_v)):
        outs += assemble([r[idx] for r in (r_in, r_uq, r_out, r_g, r_u, r_d)], small)
    return tuple(outs)
```

```python
import jax
import jax.numpy as jnp
from jax import lax
from jax.experimental import pallas as pl
from jax.experimental.pallas import tpu as pltpu

BF = jnp.bfloat16
F32 = jnp.float32
MESH = pl.DeviceIdType.MESH

N_DEV = 8
D = 1024
EPS = 1e-6
LOG2E = 1.4426950408889634
ROPE_THETA = 10000.0
N_HEADS = 8
HB = 128
NOPE = 64
ROPE = 32
V_DIM = 64
QK_DIM = NOPE + ROPE
Q_RANK = 384
KV_RANK = 128
KR_PAD = 128
MLA_IN = Q_RANK + KV_RANK + KR_PAD
G_HEADS = 4
G_DIM = 128
G_W = G_HEADS * G_DIM
CHUNK = 64
SUB = 16
XP_W = MLA_IN + 4 * G_W
IN_SH = 324
FF_SH = 352
FF_PAD = 384
MIX_W = N_HEADS * HB + G_W

ADAM_LR = 0.001
ADAM_B1 = 0.9
ADAM_B2 = 0.999
ADAM_EPS = 1e-08
ADAM_WD = 0.01
ADAM_STEP = 10

_TM = 512
_TMF = 1024
_IN_ROWS_A = 384
_TQ = 512
_AH = 2
_AH_FWD = 4
_FB = 768
_TB = 1024
_TB_BWD = 512
_HP = 4
V7X_VMEM_BYTES = 64 * 1024 * 1024
_VMEM_LIMIT = V7X_VMEM_BYTES - 8 * 1024 * 1024
NEG = -1e30


def _dot(a, b):
    return jnp.dot(a.astype(BF), b.astype(BF), preferred_element_type=F32)


def _dot_nt(a, b):
    return lax.dot_general(a.astype(BF), b.astype(BF), (((1,), (1,)), ((), ())), preferred_element_type=F32)


def _dot_tn(a, b):
    return lax.dot_general(a.astype(BF), b.astype(BF), (((0,), (0,)), ((), ())), preferred_element_type=F32)


def _sigmoid(x):
    return 1.0 / (1.0 + jnp.exp(-x))


def _rms(x, n):
    r = lax.rsqrt(jnp.sum(x * x, -1, keepdims=True) * (1.0 / n) + EPS)
    return x * r, r


def _rms_bwd(nx, r, g, dy, n):
    dg = jnp.sum(dy * nx, 0, keepdims=True)
    dn = dy * g
    dx = r * (dn - nx * (jnp.sum(dn * nx, -1, keepdims=True) * (1.0 / n)))
    return dx, dg


def _adamw(w, g, m, v):
    m2 = ADAM_B1 * m + (1.0 - ADAM_B1) * g
    v2 = ADAM_B2 * v + (1.0 - ADAM_B2) * (g * g)
    m_hat = m2 / (1.0 - ADAM_B1 ** ADAM_STEP)
    v_hat = v2 / (1.0 - ADAM_B2 ** ADAM_STEP)
    delta = -ADAM_LR * (m_hat / (jnp.sqrt(v_hat) + ADAM_EPS) + ADAM_WD * w)
    return delta, m2, v2


def _pcall(body, name, grid, in_specs, out_specs, out_shape, scratch=(), exchange=None):
    scratch = list(scratch)
    extra = ()
    if exchange is not None:
        kinds, extra = exchange
        in_specs, out_specs, out_shape = list(in_specs), list(out_specs), list(out_shape)
        n_in, n_out, n_scr, n_x = len(in_specs), len(out_specs), len(scratch), len(extra)
        inner = body

        def body(*refs):
            ins, rest = refs[:n_in], refs[n_in:]
            x_src, rest = rest[:n_x], rest[n_x:]
            outs, rest = rest[:n_out], rest[n_out:]
            x_dst, rest = rest[:n_x], rest[n_x:]
            ex = _Exchange(kinds, x_src, x_dst, *rest[n_scr:])
            first = pl.program_id(0) == 0
            last = pl.program_id(0) == grid[0] - 1
            for a in range(1, len(grid)):
                first = first & (pl.program_id(a) == 0)
                last = last & (pl.program_id(a) == grid[a] - 1)
            pl.when(first)(ex.start)
            inner(*ins, *outs, *rest[:n_scr])
            pl.when(last)(ex.wait)

        in_specs += [_HBM] * n_x
        out_specs += [_HBM] * n_x
        out_shape += _exchange_shapes(kinds, extra)
        scratch += _exchange_sems(n_x)
    call = pl.pallas_call(
        body, name=name, grid=grid, in_specs=in_specs, out_specs=out_specs, out_shape=out_shape,
        scratch_shapes=scratch,
        compiler_params=pltpu.CompilerParams(
            dimension_semantics=("arbitrary",) * len(grid), vmem_limit_bytes=_VMEM_LIMIT))
    return lambda *operands: call(*operands, *extra)


def _full(shape):
    return pl.BlockSpec(shape, lambda *_: (0,) * len(shape))


def _rows(tm, n):
    return pl.BlockSpec((tm, n), lambda i, *_: (i, 0))


def _sds(shape, dtype=F32):
    return jax.ShapeDtypeStruct(shape, dtype)


def _peer(k, x, y, c):
    px = 1 - x if (k >> 2) & 1 else x
    py = 1 - y if (k >> 1) & 1 else y
    pc = 1 - c if k & 1 else c
    return px, py, pc


GATHER, SCATTER = "gather", "scatter"


class _Exchange:
    def __init__(self, kinds, srcs, dsts, send_sems, recv_sems, loc_sems):
        self.kinds, self.srcs, self.dsts = kinds, srcs, dsts
        self.send_sems, self.recv_sems, self.loc_sems = send_sems, recv_sems, loc_sems
        self.x, self.y, self.c = lax.axis_index("x"), lax.axis_index("y"), lax.axis_index("c")
        self.me = 4 * self.x + 2 * self.y + self.c

    def _src(self, w, slot):
        return self.srcs[w] if self.kinds[w] == GATHER else self.srcs[w].at[slot]

    def _dst(self, w, slot):
        return self.dsts[w].at[slot]

    def _copy(self, w, k, outgoing):
        px, py, pc = _peer(k, self.x, self.y, self.c)
        pid = 4 * px + 2 * py + pc
        return pltpu.make_async_remote_copy(
            src_ref=self._src(w, pid if outgoing else self.me),
            dst_ref=self._dst(w, self.me if outgoing else pid),
            send_sem=self.send_sems.at[w, k - 1], recv_sem=self.recv_sems.at[w, k - 1],
            device_id=(px, py, pc), device_id_type=MESH)

    def _local(self, w):
        return pltpu.make_async_copy(self._src(w, self.me), self._dst(w, self.me), self.loc_sems.at[w])

    def start(self):
        for w in range(len(self.srcs)):
            self._local(w).start()
            for k in range(1, N_DEV):
                self._copy(w, k, True).start()

    def wait(self):
        for w in range(len(self.srcs)):
            self._local(w).wait()
            for k in range(1, N_DEV):
                self._copy(w, k, False).wait_recv()
        for w in range(len(self.srcs)):
            for k in range(1, N_DEV):
                self._copy(w, k, True).wait_send()


def _exchange_sems(n_w):
    return [pltpu.SemaphoreType.DMA((n_w, N_DEV - 1)), pltpu.SemaphoreType.DMA((n_w, N_DEV - 1)),
            pltpu.SemaphoreType.DMA((n_w,))]


def _exchange_shapes(kinds, srcs):
    return [_sds(((N_DEV,) if kd == GATHER else ()) + tuple(s.shape), s.dtype) for kd, s in zip(kinds, srcs)]


_HBM = pl.BlockSpec(memory_space=pl.ANY)


def _cast_shards(w_in, w_uq, w_out, w_gate_t, w_up_t, w_down):
    shapes = [(D, IN_SH), (Q_RANK // N_DEV, N_HEADS * QK_DIM), (D // N_DEV, D), (2 * FF_PAD, D), (FF_PAD, D)]

    def body(win, wuq, wout, wg, wu, wd, sin_, suq, sout, sgu, sd):
        sin_[...] = win[...].astype(BF)
        suq[...] = wuq[...].astype(BF)
        sout[...] = wout[...].astype(BF)
        sgu[...] = jnp.zeros(sgu.shape, BF)
        sgu[0:FF_SH, :] = wg[...].astype(BF)
        sgu[FF_PAD:FF_PAD + FF_SH, :] = wu[...].astype(BF)
        sd[...] = jnp.zeros(sd.shape, BF)
        sd[0:FF_SH, :] = wd[...].astype(BF)

    vm = pl.BlockSpec(memory_space=pltpu.VMEM)
    return pl.pallas_call(
        body, name="cast_shards", in_specs=[vm] * 6, out_specs=[vm] * 5,
        out_shape=[_sds(s, BF) for s in shapes],
        compiler_params=pltpu.CompilerParams(vmem_limit_bytes=_VMEM_LIMIT),
    )(w_in, w_uq, w_out, w_gate_t, w_up_t, w_down)


def _gather_two_level(name, srcs):
    n_w = len(srcs)

    def body(*refs):
        src, dst = refs[:n_w], refs[n_w:2 * n_w]
        send_sems, recv_sems, loc_sems = refs[2 * n_w:]
        x, y, c = lax.axis_index("x"), lax.axis_index("y"), lax.axis_index("c")
        me, sibling = (x, y, c), (x, y, 1 - c)
        chips = [(1 - x, y), (x, 1 - y), (1 - x, 1 - y)]
        slot = lambda p: 4 * p[0] + 2 * p[1] + p[2]

        def copy(w, k, block, to, own=False):
            return pltpu.make_async_remote_copy(
                src_ref=src[w] if own else dst[w].at[slot(block)], dst_ref=dst[w].at[slot(block)],
                send_sem=send_sems.at[w, k], recv_sem=recv_sems.at[w, k], device_id=to, device_id_type=MESH)

        local = [pltpu.make_async_copy(src[w], dst[w].at[slot(me)], loc_sems.at[w]) for w in range(n_w)]
        first, passed = [], []
        for w in range(n_w):
            local[w].start()
            first.append(copy(w, 0, me, sibling, own=True))
            first += [copy(w, 1 + j, me, (*chip, c), own=True) for j, chip in enumerate(chips)]
        for cp in first:
            cp.start()
        for w in range(n_w):
            for j, chip in enumerate(chips):
                copy(w, 1 + j, (*chip, c), me).wait_recv()
                passed.append(copy(w, 4 + j, (*chip, c), sibling))
                passed[-1].start()
        for w in range(n_w):
            copy(w, 0, sibling, me).wait_recv()
            for j, chip in enumerate(chips):
                copy(w, 4 + j, (*chip, 1 - c), me).wait_recv()
        for cp in first + passed:
            cp.wait_send()
        for w in range(n_w):
            local[w].wait()

    return pl.pallas_call(
        body, name=name, in_specs=[_HBM] * n_w, out_specs=[_HBM] * n_w,
        out_shape=_exchange_shapes([GATHER] * n_w, srcs), scratch_shapes=_exchange_sems(n_w))(*srcs)


def _row_offsets(arrays):
    offs, rows = [], 0
    for a in arrays:
        offs.append(rows)
        rows += a.shape[0]
    return offs, -(-rows // 8) * 8


def _small_adam(rall, big_parts, big, ws, ms, vs):
    n_p = len(ws)
    packed = [w for p, w in enumerate(ws) if p != big] + [jax.ShapeDtypeStruct((1, HB), F32)]
    offs, _ = _row_offsets(packed)
    offs = offs[:big] + [None] + offs[big:]

    def total(ref, sl):
        g = ref[(0,) + sl]
        for j in range(1, N_DEV):
            g = g + ref[(j,) + sl]
        return g

    def body(*refs):
        rall_ref, big_ref = refs[:2]
        w_refs, m_refs, v_refs = refs[2:2 + n_p], refs[2 + n_p:2 + 2 * n_p], refs[2 + 2 * n_p:2 + 3 * n_p]
        outs = refs[2 + 3 * n_p:]
        for p in range(n_p):
            r, n = w_refs[p].shape
            if p == big:
                g = total(big_ref, (slice(0, r), slice(0, n)))
            else:
                g = total(rall_ref, (slice(offs[p], offs[p] + r), slice(0, n)))
            delta, m2, v2 = _adamw(w_refs[p][...], g, m_refs[p][...], v_refs[p][...])
            outs[p][...] = g
            outs[n_p + p][...] = delta
            outs[2 * n_p + p][...] = m2
            outs[3 * n_p + p][...] = v2
        outs[4 * n_p][...] = total(rall_ref, (slice(offs[n_p], offs[n_p] + 1), slice(0, HB)))

    vm = pl.BlockSpec(memory_space=pltpu.VMEM)
    res = pl.pallas_call(
        body, name="small_adam", in_specs=[vm] * (2 + 3 * n_p), out_specs=[vm] * (4 * n_p + 1),
        out_shape=[_sds(w.shape) for w in ws] * 4 + [_sds((1, HB))],
        compiler_params=pltpu.CompilerParams(vmem_limit_bytes=_VMEM_LIMIT),
    )(rall, big_parts, *ws, *ms, *vs)
    return res[:n_p], res[n_p:2 * n_p], res[2 * n_p:3 * n_p], res[3 * n_p:4 * n_p], res[4 * n_p]


def _device_sum(p_ref):
    g = p_ref[0].astype(F32)
    for j in range(1, N_DEV):
        g = g + p_ref[j].astype(F32)
    return g


def _shard_adam(name, parts, w, m, v, tr):
    a0, b0 = w.shape
    n_p = len(parts)
    b = parts[0].shape[2]
    first = [0]
    for p in parts:
        first.append(first[-1] + p.shape[1] // tr)

    def body(*refs):
        p_refs = refs[:n_p]
        w_ref, m_ref, v_ref, g_out, d_out, m_out, v_out = refs[n_p:]
        i = pl.program_id(0)
        g = _device_sum(p_refs[0])
        for k in range(1, n_p):
            g = jnp.where(i >= first[k], _device_sum(p_refs[k]), g)
        g = g[:, 0:b0]
        delta, m2, v2 = _adamw(w_ref[...], g, m_ref[...], v_ref[...])
        g_out[...] = g
        d_out[...] = delta
        m_out[...] = m2
        v_out[...] = v2

    def part_spec(k):
        last = first[k + 1] - first[k] - 1
        return pl.BlockSpec((N_DEV, tr, b), lambda i: (0, jnp.minimum(jnp.maximum(i - first[k], 0), last), 0))

    blk = pl.BlockSpec((tr, b0), lambda i: (i, 0))
    return _pcall(
        body, name, (a0 // tr,), [part_spec(k) for k in range(n_p)] + [blk, blk, blk],
        [blk] * 4, [_sds((a0, b0))] * 4)(*parts, w, m, v)


def _gate_up_adam(parts, ws, ms, vs):
    tc = 256

    def body(p_ref, wg, wu, mg, mu, vg, vu, *outs):
        g = _device_sum(p_ref)
        for k, (w_ref, m_ref, v_ref) in enumerate(((wg, mg, vg), (wu, mu, vu))):
            gk = g[k * FF_PAD:k * FF_PAD + FF_SH]
            delta, m2, v2 = _adamw(w_ref[...], gk, m_ref[...], v_ref[...])
            for o, val in zip(outs[4 * k:4 * k + 4], (gk, delta, m2, v2)):
                o[...] = val

    blk = pl.BlockSpec((FF_SH, tc), lambda i: (0, i))
    res = _pcall(
        body, "adam_w_gate_up", (D // tc,), [pl.BlockSpec((N_DEV, 2 * FF_PAD, tc), lambda i: (0, 0, i))] + [blk] * 6,
        [blk] * 8, [_sds((FF_SH, D))] * 8)(parts, *ws, *ms, *vs)
    return res[:4], res[4:]


def _fwd_in(x, g_pre, w_in_al, tm):
    T = x.shape[0]

    def body(x_ref, g_ref, w_ref, xm_ref, xh_ref, u_ref):
        nx, _ = _rms(x_ref[...], D)
        u = (nx * g_ref[...]).astype(BF)
        u_ref[...] = u
        xm_ref[...] = jnp.dot(u, w_ref[:, 0:MLA_IN], preferred_element_type=F32)
        xh_ref[...] = jnp.dot(u, w_ref[:, MLA_IN:XP_W], preferred_element_type=F32)

    return _pcall(body, "fwd_in", (T // tm,),
                  [_rows(tm, D), _full((1, D)), _full((D, XP_W))],
                  [_rows(tm, MLA_IN), _rows(tm, 4 * G_W), _rows(tm, D)],
                  [_sds((T, MLA_IN)), _sds((T, 4 * G_W)), _sds((T, D), BF)])(x, g_pre, w_in_al)


def _rope(blk, ta, tb1, tb2):
    return blk * ta + pltpu.roll(blk, HB - ROPE // 2, 1) * tb1 + pltpu.roll(blk, ROPE // 2, 1) * tb2


def _unrope(d, ta, tb1, tb2):
    return d * ta + pltpu.roll(d * tb1, ROPE // 2, 1) + pltpu.roll(d * tb2, HB - ROPE // 2, 1)


def _mla_prep(xp, tabs, g_q, g_kv, w_uq, w_uk, w_uv, tm):
    T = xp.shape[0]
    W = N_HEADS * HB

    def body(xp_ref, ta_ref, tb1_ref, tb2_ref, gq_ref, gkv_ref, wuq_ref, wuk_ref, wuv_ref, q_ref, qs_ref, k_ref, v_ref):
        ta, tb1, tb2 = ta_ref[...], tb1_ref[...], tb2_ref[...]
        nq, _ = _rms(xp_ref[:, 0:Q_RANK], Q_RANK)
        nkv, _ = _rms(xp_ref[:, Q_RANK:Q_RANK + KV_RANK], KV_RANK)
        nkv = (nkv * gkv_ref[...]).astype(BF)
        qpre = _dot(nq * gq_ref[...], wuq_ref[...])
        kpre = jnp.dot(nkv, wuk_ref[...], preferred_element_type=F32)
        v = jnp.dot(nkv, wuv_ref[...], preferred_element_type=F32)
        lane = lax.broadcasted_iota(jnp.int32, (tm, W), 1)
        v_ref[...] = jnp.where((lane & (HB - 1)) == V_DIM, 1.0, v).astype(BF)
        kr = _rope(pltpu.roll(xp_ref[:, Q_RANK + KV_RANK:MLA_IN], NOPE, 1), ta, tb1, tb2)
        for h in range(N_HEADS):
            sl = slice(h * HB, (h + 1) * HB)
            qr = _rope(qpre[:, sl], ta, tb1, tb2)
            q_ref[:, sl] = qr.astype(BF)
            qs_ref[:, sl] = (qr * (QK_DIM ** -0.5 * LOG2E)).astype(BF)
            k_ref[:, sl] = (kpre[:, sl] + kr).astype(BF)

    tab = _rows(tm, HB)
    return _pcall(body, "mla_prep", (T // tm,),
                  [_rows(tm, MLA_IN), tab, tab, tab, _full((1, Q_RANK)), _full((1, KV_RANK)),
                   _full((Q_RANK, W)), _full((KV_RANK, W)), _full((KV_RANK, W))],
                  [_rows(tm, W)] * 4, [_sds((T, W), BF)] * 4)(xp, *tabs, g_q, g_kv, w_uq, w_uk, w_uv)


def _flash_fwd(q, k, v, tq, exchange=None):
    T = q.shape[0]
    hp = _AH_FWD
    W = hp * HB

    def body(q_ref, k_ref, v_ref, o_ref, lse_ref):
        i = pl.program_id(1)

        def blk(j, carry, masked):
            st = pl.multiple_of(j * tq, tq)
            out = []
            for h in range(hp):
                ls = slice(h * HB, (h + 1) * HB)
                m, acc = carry[h]
                s = _dot_nt(q_ref[:, ls], k_ref[pl.ds(st, tq), ls])
                if masked:
                    r = lax.broadcasted_iota(jnp.int32, (tq, tq), 0)
                    c = lax.broadcasted_iota(jnp.int32, (tq, tq), 1)
                    s = jnp.where(c <= r, s, NEG)
                m2 = jnp.maximum(m, jnp.max(s, -1, keepdims=True))
                p = jnp.exp2(s - m2)
                out.append((m2, jnp.exp2(m - m2) * acc + _dot(p, v_ref[pl.ds(st, tq), ls])))
            return tuple(out)

        init = tuple((jnp.full((tq, 1), NEG, F32), jnp.zeros((tq, HB), F32)) for _ in range(hp))
        carry = lax.fori_loop(0, i, lambda j, cr: blk(j, cr, False), init)
        res = blk(i, carry, True)
        lane = lax.broadcasted_iota(jnp.int32, (tq, HB), 1)
        for h in range(hp):
            ls = slice(h * HB, (h + 1) * HB)
            m, acc = res[h]
            l = acc[:, V_DIM:V_DIM + 1]
            o_ref[:, ls] = jnp.where(lane < V_DIM, acc / l, 0.0).astype(BF)
            lse_ref[:, ls] = jnp.broadcast_to(m * (1.0 / LOG2E) + jnp.log(l), (tq, HB))

    qs = pl.BlockSpec((tq, W), lambda h, i: (i, h))
    kvs = pl.BlockSpec((T, W), lambda h, i: (0, h))
    return _pcall(body, "flash_fwd", (N_HEADS // hp, T // tq), [qs, kvs, kvs], [qs, qs],
                  [_sds((T, N_HEADS * HB), BF), _sds((T, N_HEADS * HB))], exchange=exchange)(q, k, v)


def _gates(hq, hf, lb):
    sig = _sigmoid(hf)
    f = lb + (1.0 - lb) * sig
    sq = _sigmoid(hq)
    return hq * sq, 1.0 - f, f, jnp.log(f), sig, sq


def _lower_bound(lbl_ref):
    l0, l1 = lbl_ref[0:1, :], lbl_ref[1:2, :]
    mx = jnp.maximum(l0, l1)
    e0, e1 = jnp.exp(l0 - mx), jnp.exp(l1 - mx)
    return e0 / (e0 + e1)


def _split3(x):
    hi = x.astype(BF)
    r1 = x - hi.astype(F32)
    mid = r1.astype(BF)
    lo = (r1 - mid.astype(F32)).astype(BF)
    return hi, mid, lo


def _tri_mm(tri, x):
    hi, mid, lo = _split3(x)
    mm = lambda t: jnp.dot(tri, t, preferred_element_type=F32)
    return mm(hi) + mm(mid) + mm(lo)


def _intra_codes(sub):
    row = lax.broadcasted_iota(jnp.int32, (CHUNK, CHUNK), 0)
    col = lax.broadcasted_iota(jnp.int32, (CHUNK, CHUNK), 1)
    return sub, row, col


def _intra(q, k, b2, b_s, codes, da=None):
    grad = da is not None
    pow2 = (lambda x: jnp.exp2(jnp.minimum(x, 0.0))) if grad else jnp.exp2
    sub, row, col = codes
    a = jnp.zeros((CHUNK, CHUNK), F32)
    dq = jnp.zeros((CHUNK, G_DIM), F32)
    dk = jnp.zeros((CHUNK, G_DIM), F32)
    for i in range(1, CHUNK // sub):
        b0 = b_s[sub * i - 1:sub * i, :]
        eq, ek = pow2(b2 - b0), pow2(b0 - b2)
        mask = ((row // sub) == i) & (col < sub * i)
        if grad:
            dai = jnp.where(mask, da, 0.0)
            dq = dq + _dot(dai, k * ek) * eq
            dk = dk + _dot_tn(dai, q * eq) * ek
        else:
            a = jnp.where(mask, _dot_nt(q * eq, k * ek), a)
    for d in range(sub):
        ksh = pltpu.roll(k, d, 0) if d else k
        bsh = pltpu.roll(b2, d, 0) if d else b2
        e = pow2(b2 - bsh)
        mask = (col == row - d) & ((row & (sub - 1)) >= d)
        if grad:
            g = jnp.sum(jnp.where(mask, da, 0.0), -1, keepdims=True) * e
            dq = dq + g * ksh
            cb = g * q
            dk = dk + (pltpu.roll(cb, CHUNK - d, 0) if d else cb)
        else:
            a = jnp.where(mask, jnp.sum(q * ksh * e, -1, keepdims=True), a)
    return (dq, dk) if grad else a


def _hgrn_fwd(xp, lb_logits, g_hn, exchange=None):
    T = xp.shape[0]
    tb = min(_TB, T)
    ncb = tb // CHUNK
    hp = _HP
    W = hp * G_DIM

    def body(hq_ref, hf_ref, hi_ref, hg_ref, lbl_ref, ghn_ref, out_ref, oraw_ref, sall_ref, aall_ref, ball_ref,
             st_ref, b_s):
        lb_all = _lower_bound(lbl_ref)

        @pl.when(pl.program_id(1) == 0)
        def _():
            st_ref[...] = jnp.zeros(st_ref.shape, F32)

        row = lax.broadcasted_iota(jnp.int32, (CHUNK, CHUNK), 0)
        col = lax.broadcasted_iota(jnp.int32, (CHUNK, CHUNK), 1)
        tri = (col <= row).astype(BF)
        codes = _intra_codes(SUB)

        def chunk(c, carry):
            sl = pl.ds(pl.multiple_of(c * CHUNK, CHUNK), CHUNK)
            for h in range(hp):
                ls = slice(h * G_DIM, (h + 1) * G_DIM)
                q, k, _, lf, _, _ = _gates(hq_ref[sl, ls], hf_ref[sl, ls], lb_all[:, ls])
                v = hi_ref[sl, ls]
                b2 = _tri_mm(tri, lf) * LOG2E
                b_s[h] = b2
                ball_ref[sl, ls] = b2
                st = st_ref[h]
                sall_ref[c, h] = st
                a = _intra(q, k, b2, b_s.at[h], codes)
                aall_ref[c, h] = a
                o = _dot_nt(q * jnp.exp2(b2), st) + _dot(a, v)
                bl = b_s[h, CHUNK - 1:CHUNK, :]
                st_ref[h] = st * jnp.exp2(bl) + _dot_tn(v, k * jnp.exp2(bl - b2))
                oraw_ref[sl, ls] = o
                n, _ = _rms(o, G_DIM)
                hg = hg_ref[sl, ls]
                out_ref[sl, ls] = n * ghn_ref[:, ls] * (hg * _sigmoid(hg))
            return carry

        lax.fori_loop(0, ncb, chunk, 0, unroll=4)

    col_blk = lambda j: pl.BlockSpec((tb, W), lambda p, t: (t, j * (G_HEADS // hp) + p))
    head = pl.BlockSpec((tb, W), lambda p, t: (t, p))
    return _pcall(
        body, "hgrn_fwd", (G_HEADS // hp, T // tb),
        [col_blk(0), col_blk(1), col_blk(2), col_blk(3),
         pl.BlockSpec((2, W), lambda p, t: (0, p)), pl.BlockSpec((1, W), lambda p, t: (0, p))],
        [head, head, pl.BlockSpec((ncb, hp, G_DIM, G_DIM), lambda p, t: (t, p, 0, 0)),
         pl.BlockSpec((ncb, hp, CHUNK, CHUNK), lambda p, t: (t, p, 0, 0)), head],
        [_sds((T, G_W)), _sds((T, G_W)), _sds((T // CHUNK, G_HEADS, G_DIM, G_DIM)),
         _sds((T // CHUNK, G_HEADS, CHUNK, CHUNK)), _sds((T, G_W))],
        scratch=[pltpu.VMEM((hp, G_DIM, G_DIM), F32), pltpu.VMEM((hp, CHUNK, G_DIM), F32)], exchange=exchange,
    )(xp, xp, xp, xp, lb_logits, g_hn)


def _fwd_out(o_pad, o_hgrn, x, g_on, w_out, g_post, g_fpre, tm):
    T = x.shape[0]

    def body(o_ref, oh_ref, x_ref, gon_ref, w_ref, gpost_ref, gfpre_ref, h1_ref, y1_ref, z_ref, mix_ref):
        for h in range(N_HEADS):
            sl = slice(h * HB, (h + 1) * HB)
            n, _ = _rms(o_ref[:, sl].astype(F32), V_DIM)
            mix_ref[:, sl] = (n * gon_ref[:, sl]).astype(BF)
        mix_ref[:, N_HEADS * HB:MIX_W] = oh_ref[...].astype(BF)
        y1 = jnp.dot(mix_ref[...], w_ref[...], preferred_element_type=F32)
        y1_ref[...] = y1
        ny, _ = _rms(y1, D)
        h1 = x_ref[...] + ny * gpost_ref[...]
        h1_ref[...] = h1
        nh, _ = _rms(h1, D)
        z_ref[...] = (nh * gfpre_ref[...]).astype(BF)

    return _pcall(body, "fwd_out", (T // tm,),
                  [_rows(tm, N_HEADS * HB), _rows(tm, G_W), _rows(tm, D), _full((1, N_HEADS * HB)),
                   _full((MIX_W, D)), _full((1, D)), _full((1, D))],
                  [_rows(tm, D), _rows(tm, D), _rows(tm, D), _rows(tm, MIX_W)],
                  [_sds((T, D)), _sds((T, D)), _sds((T, D), BF), _sds((T, MIX_W), BF)],
                  )(o_pad, o_hgrn, x, g_on, w_out, g_post, g_fpre)


def _ffn_fwd(z, wgu, wd, h1, tgt, g_fpost, tm, nd):
    T = z.shape[0]
    fb = nd * FF_PAD
    nf = wd.shape[0] // fb

    def body(z_ref, wgu_ref, wd_ref, h1_ref, t_ref, gp_ref,
             as_ref, bs_ref, ff_ref, dh2_ref, dy2_ref, dgp_ref, loss_ref, acc):
        i, j = pl.program_id(0), pl.program_id(1)
        gu = _dot_nt(z_ref[...], wgu_ref[...])
        piece = lambda n: gu[:, n * FF_PAD:(n + 1) * FF_PAD]
        g = piece(0) if nd == 1 else jnp.concatenate([piece(2 * n) for n in range(nd)], 1)
        u = piece(1) if nd == 1 else jnp.concatenate([piece(2 * n + 1) for n in range(nd)], 1)
        s = _sigmoid(g)
        b = g * s
        ff = (b * u).astype(BF)
        as_ref[...] = (u * _dsilu(g, s)).astype(BF)
        bs_ref[...] = b.astype(BF)
        ff_ref[...] = ff
        part = jnp.dot(ff, wd_ref[...], preferred_element_type=F32)

        @pl.when(j == 0)
        def _():
            acc[...] = part

        @pl.when(j > 0)
        def _():
            acc[...] += part

        @pl.when((i == 0) & (j == 0))
        def _():
            dgp_ref[...] = jnp.zeros(dgp_ref.shape, F32)
            loss_ref[...] = jnp.zeros(loss_ref.shape, F32)

        @pl.when(j == nf - 1)
        def _():
            ny, r = _rms(acc[...], D)
            err = h1_ref[...] + ny * gp_ref[...] - t_ref[...]
            loss_ref[...] += 0.5 * jnp.sum(jnp.sum(err * err, -1, keepdims=True) * (1.0 / D), 0, keepdims=True)
            dh2 = err * (1.0 / D)
            dh2_ref[...] = dh2
            dy2, dgp = _rms_bwd(ny, r, gp_ref[...], dh2, D)
            dy2_ref[...] = dy2.astype(BF)
            dgp_ref[...] += dgp

    tok = lambda n: pl.BlockSpec((tm, n), lambda i, j: (i, 0), pipeline_mode=pl.Buffered(1))
    col = pl.BlockSpec((tm, fb), lambda i, j: (i, j))
    return _pcall(
        body, "ffn_fwd", (T // tm, nf),
        [tok(D), pl.BlockSpec((2 * fb, D), lambda i, j: (j, 0)), pl.BlockSpec((fb, D), lambda i, j: (j, 0)),
         tok(D), tok(D), _full((1, D))],
        [col, col, col, tok(D), tok(D), _full((1, D)), _full((1, HB))],
        [_sds((T, nf * fb), BF)] * 3 + [_sds((T, D)), _sds((T, D), BF), _sds((1, D)), _sds((1, HB))],
        scratch=[pltpu.VMEM((tm, D), F32)],
    )(z, wgu, wd, h1, tgt, g_fpost)


def _dsilu(x, s):
    return s * (1.0 + x * (1.0 - s))


def _ffn_bwd_x(dy2, gs, us, wgu, wd, h1, y1, dh2, g_fpre, g_post, tm):
    T = dy2.shape[0]
    nf = wd.shape[0] // _FB

    def body(dy2_ref, gs_ref, us_ref, wgu_ref, wd_ref, h1_ref, y1_ref, dh2_ref, gf_ref, gp_ref,
             dgu_ref, dh1_ref, dy1_ref, dgf_ref, dgp_ref, acc):
        i, j = pl.program_id(0), pl.program_id(1)
        dff = _dot_nt(dy2_ref[...], wd_ref[...])
        dg = (dff * gs_ref[...].astype(F32)).astype(BF)
        du = (dff * us_ref[...].astype(F32)).astype(BF)
        dgu = jnp.concatenate([dg[:, 0:FF_PAD], du[:, 0:FF_PAD], dg[:, FF_PAD:_FB], du[:, FF_PAD:_FB]], 1)
        dgu_ref[...] = dgu
        part = jnp.dot(dgu, wgu_ref[...], preferred_element_type=F32)

        @pl.when(j == 0)
        def _():
            acc[...] = part

        @pl.when(j > 0)
        def _():
            acc[...] += part

        @pl.when((i == 0) & (j == 0))
        def _():
            dgf_ref[...] = jnp.zeros(dgf_ref.shape, F32)
            dgp_ref[...] = jnp.zeros(dgp_ref.shape, F32)

        @pl.when(j == nf - 1)
        def _():
            nh, rh = _rms(h1_ref[...], D)
            dh, dgf = _rms_bwd(nh, rh, gf_ref[...], acc[...], D)
            dh1 = dh2_ref[...] + dh
            dh1_ref[...] = dh1
            dgf_ref[...] += dgf
            ny, ry = _rms(y1_ref[...], D)
            dy1, dgp = _rms_bwd(ny, ry, gp_ref[...], dh1, D)
            dy1_ref[...] = dy1.astype(BF)
            dgp_ref[...] += dgp

    tok = lambda n: pl.BlockSpec((tm, n), lambda i, j: (i, 0))
    col = pl.BlockSpec((tm, _FB), lambda i, j: (i, j))
    return _pcall(
        body, "ffn_bwd_x", (T // tm, nf),
        [tok(D), col, col, pl.BlockSpec((2 * _FB, D), lambda i, j: (j, 0)), pl.BlockSpec((_FB, D), lambda i, j: (j, 0)),
         tok(D), tok(D), tok(D), _full((1, D)), _full((1, D))],
        [pl.BlockSpec((tm, 2 * _FB), lambda i, j: (i, j)), tok(D), tok(D), _full((1, D)), _full((1, D))],
        [_sds((T, 2 * nf * _FB), BF), _sds((T, D)), _sds((T, D), BF), _sds((1, D)), _sds((1, D))],
        scratch=[pltpu.VMEM((tm, D), F32)],
    )(dy2, gs, us, wgu, wd, h1, y1, dh2, g_fpre, g_post)


def _ffn_bwd_w(z, ffs, dgu, dy2, tm):
    T = z.shape[0]
    nf = ffs.shape[1] // _FB
    nt = T // tm

    def body(z_ref, ff_ref, dgu_ref, dy2_ref, dwgu_ref, dwd_ref, agu, ad):
        i = pl.program_id(1)
        pgu = _dot_tn(dgu_ref[...], z_ref[...])
        pd = _dot_tn(ff_ref[...], dy2_ref[...])

        @pl.when(i == 0)
        def _():
            agu[...] = pgu
            ad[...] = pd

        @pl.when(i > 0)
        def _():
            agu[...] += pgu
            ad[...] += pd

        @pl.when(i == nt - 1)
        def _():
            dwgu_ref[...] = agu[...].astype(BF)
            dwd_ref[...] = ad[...].astype(BF)

    F = nf * _FB
    tok = lambda n: pl.BlockSpec((tm, n), lambda j, i: (i, 0))
    return _pcall(
        body, "ffn_bwd_w", (nf, nt),
        [tok(D), pl.BlockSpec((tm, _FB), lambda j, i: (i, j)), pl.BlockSpec((tm, 2 * _FB), lambda j, i: (i, j)), tok(D)],
        [pl.BlockSpec((2 * _FB, D), lambda j, i: (j, 0)), pl.BlockSpec((_FB, D), lambda j, i: (j, 0))],
        [_sds((2 * F, D), BF), _sds((F, D), BF)],
        scratch=[pltpu.VMEM((2 * _FB, D), F32), pltpu.VMEM((_FB, D), F32)],
    )(z, ffs, dgu, dy2)


def _out_bwd(dy1, mix, o_pad, w_out, g_on, tm):
    T = dy1.shape[0]
    W = N_HEADS * HB

    def body(dy1_ref, mix_ref, o_ref, w_ref, gon_ref, do_ref, dl_ref, dohg_ref, dw_ref, dgon_ref):
        i = pl.program_id(0)
        dy1v = dy1_ref[...]
        dmix = _dot_nt(dy1v, w_ref[...])
        pw = _dot_tn(mix_ref[...], dy1v)

        @pl.when(i == 0)
        def _():
            dw_ref[...] = pw
            dgon_ref[...] = jnp.zeros(dgon_ref.shape, F32)

        @pl.when(i > 0)
        def _():
            dw_ref[...] += pw

        for h in range(N_HEADS):
            sl = slice(h * HB, (h + 1) * HB)
            ov = o_ref[:, sl].astype(F32)
            n, r = _rms(ov, V_DIM)
            do, dg = _rms_bwd(n, r, gon_ref[:, sl], dmix[:, sl], V_DIM)
            dgon_ref[:, sl] += dg
            do_ref[:, sl] = do.astype(BF)
            dl_ref[:, sl] = jnp.broadcast_to(jnp.sum(do * ov, -1, keepdims=True), (tm, HB))
        dohg_ref[...] = dmix[:, W:MIX_W]

    return _pcall(body, "out_bwd", (T // tm,),
                  [_rows(tm, D), _rows(tm, MIX_W), _rows(tm, W), _full((MIX_W, D)), _full((1, W))],
                  [_rows(tm, W), _rows(tm, W), _rows(tm, G_W), _full((MIX_W, D)), _full((1, W))],
                  [_sds((T, W), BF), _sds((T, W)), _sds((T, G_W)), _sds((MIX_W, D)), _sds((1, W))],
                  )(dy1, mix, o_pad, w_out, g_on)


def _flash_bwd(q, k, v, do, lse, dl, tq, exchange=None):
    T = q.shape[0]
    nq = T // tq
    scale = QK_DIM ** -0.5
    hp = _AH
    W = hp * HB

    def body(k_ref, v_ref, q_ref, do_ref, lse_ref, dl_ref, dk_ref, dv_ref, dq_ref):
        j = pl.program_id(1)

        @pl.when(j == 0)
        def _():
            dq_ref[...] = jnp.zeros(dq_ref.shape, F32)

        def blk(i, carry, masked):
            sl = pl.ds(pl.multiple_of(i * tq, tq), tq)
            out = []
            for h in range(hp):
                ls = slice(h * HB, (h + 1) * HB)
                dk, dv = carry[h]
                kv, vv = k_ref[:, ls], v_ref[:, ls]
                qv, dov = q_ref[sl, ls], do_ref[sl, ls]
                s = _dot_nt(qv, kv) * scale
                if masked:
                    r = lax.broadcasted_iota(jnp.int32, (tq, tq), 0)
                    c = lax.broadcasted_iota(jnp.int32, (tq, tq), 1)
                    s = jnp.where(c <= r, s, NEG)
                p = jnp.exp(s - lse_ref[sl, h * HB:h * HB + 1])
                ds = p * (_dot_nt(dov, vv) - dl_ref[sl, h * HB:h * HB + 1]) * scale
                dq_ref[sl, ls] += _dot(ds, kv)
                out.append((dk + _dot_tn(ds, qv), dv + _dot_tn(p, dov)))
            return tuple(out)

        zero = jnp.zeros((tq, HB), F32)
        carry = blk(j, tuple((zero, zero) for _ in range(hp)), True)
        res = lax.fori_loop(j + 1, nq, lambda i, cr: blk(i, cr, False), carry)
        for h in range(hp):
            ls = slice(h * HB, (h + 1) * HB)
            dk_ref[:, ls] = res[h][0].astype(BF)
            dv_ref[:, ls] = res[h][1].astype(BF)

    tile = pl.BlockSpec((tq, W), lambda h, j: (j, h))
    whole = pl.BlockSpec((T, W), lambda h, j: (0, h))
    return _pcall(body, "flash_bwd", (N_HEADS // hp, nq), [tile, tile, whole, whole, whole, whole],
                  [tile, tile, whole], [_sds((T, N_HEADS * HB), BF)] * 2 + [_sds((T, N_HEADS * HB))],
                  exchange=exchange)(k, v, q, do, lse, dl)


def _mla_prep_bwd(xp, tabs, dq, dk, dv, g_q, g_kv, w_uq, w_uk, w_uv, tm):
    T = xp.shape[0]
    W = N_HEADS * HB

    def body(xp_ref, ta_ref, tb1_ref, tb2_ref, dq_ref, dk_ref, dv_ref, gq_ref, gkv_ref, wuq_ref, wuk_ref, wuv_ref,
             dxp_ref, dwuq_ref, dwuk_ref, dwuv_ref, dgq_ref, dgkv_ref, dqp):
        i = pl.program_id(0)
        ta, tb1, tb2 = ta_ref[...], tb1_ref[...], tb2_ref[...]
        nq, rq = _rms(xp_ref[:, 0:Q_RANK], Q_RANK)
        nkv, rkv = _rms(xp_ref[:, Q_RANK:Q_RANK + KV_RANK], KV_RANK)
        dkr = jnp.zeros((tm, HB), F32)
        for h in range(N_HEADS):
            sl = slice(h * HB, (h + 1) * HB)
            dqp[:, sl] = _unrope(dq_ref[:, sl], ta, tb1, tb2).astype(BF)
            dkr = dkr + dk_ref[:, sl].astype(F32)
        dkr = pltpu.roll(_unrope(dkr, ta, tb1, tb2), HB - NOPE, 1)
        lane = lax.broadcasted_iota(jnp.int32, (tm, HB), 1)
        dxp_ref[:, Q_RANK + KV_RANK:MLA_IN] = jnp.where(lane < ROPE, dkr, 0.0)
        dqpv = dqp[...]
        dkv, dvv = dk_ref[...], dv_ref[...]
        nqs = (nq * gq_ref[...]).astype(BF)
        nkvs = (nkv * gkv_ref[...]).astype(BF)
        pq, pk, pv = _dot_tn(nqs, dqpv), _dot_tn(nkvs, dkv), _dot_tn(nkvs, dvv)
        dcq, dgq = _rms_bwd(nq, rq, gq_ref[...], _dot_nt(dqpv, wuq_ref[...]), Q_RANK)
        dckv, dgkv = _rms_bwd(nkv, rkv, gkv_ref[...], _dot_nt(dkv, wuk_ref[...]) + _dot_nt(dvv, wuv_ref[...]), KV_RANK)
        dxp_ref[:, 0:Q_RANK] = dcq
        dxp_ref[:, Q_RANK:Q_RANK + KV_RANK] = dckv

        @pl.when(i == 0)
        def _():
            dwuq_ref[...] = pq
            dwuk_ref[...] = pk
            dwuv_ref[...] = pv
            dgq_ref[...] = dgq
            dgkv_ref[...] = dgkv

        @pl.when(i > 0)
        def _():
            dwuq_ref[...] += pq
            dwuk_ref[...] += pk
            dwuv_ref[...] += pv
            dgq_ref[...] += dgq
            dgkv_ref[...] += dgkv

    tab = _rows(tm, HB)
    return _pcall(
        body, "mla_prep_bwd", (T // tm,),
        [_rows(tm, MLA_IN), tab, tab, tab, _rows(tm, W), _rows(tm, W), _rows(tm, W), _full((1, Q_RANK)),
         _full((1, KV_RANK)), _full((Q_RANK, W)), _full((KV_RANK, W)), _full((KV_RANK, W))],
        [_rows(tm, MLA_IN), _full((Q_RANK, W)), _full((KV_RANK, W)), _full((KV_RANK, W)), _full((1, Q_RANK)),
         _full((1, KV_RANK))],
        [_sds((T, MLA_IN)), _sds((Q_RANK, W)), _sds((KV_RANK, W)), _sds((KV_RANK, W)), _sds((1, Q_RANK)),
         _sds((1, KV_RANK))],
        scratch=[pltpu.VMEM((tm, W), BF)],
    )(xp, *tabs, dq, dk, dv, g_q, g_kv, w_uq, w_uk, w_uv)


def _hgrn_bwd(xp, o_raw, s_all, a_all, b_all, d_out, lb_logits, g_hn, exchange=None):
    T = xp.shape[0]
    tb = min(_TB_BWD, T)
    ncb = tb // CHUNK
    nb = T // tb
    hp = _HP
    W = hp * G_DIM

    def body(hq_ref, hf_ref, hi_ref, hg_ref, o_ref, sall_ref, aall_ref, ball_ref, dout_ref, lbl_ref, ghn_ref,
             dhq_ref, dhf_ref, dhi_ref, dhg_ref, dlbl_ref, dghn_ref, dst_ref, b_s, acc_lb, acc_g):
        t = pl.program_id(1)
        lb_all = _lower_bound(lbl_ref)

        @pl.when(t == 0)
        def _():
            dst_ref[...] = jnp.zeros(dst_ref.shape, F32)
            acc_lb[...] = jnp.zeros(acc_lb.shape, F32)
            acc_g[...] = jnp.zeros(acc_g.shape, F32)

        row = lax.broadcasted_iota(jnp.int32, (CHUNK, CHUNK), 0)
        col = lax.broadcasted_iota(jnp.int32, (CHUNK, CHUNK), 1)
        tri_t = (col >= row).astype(BF)
        codes = _intra_codes(SUB)
        last = lax.broadcasted_iota(jnp.int32, (CHUNK, G_DIM), 0) == CHUNK - 1

        def chunk(cc, carry):
            c = ncb - 1 - cc
            sl = pl.ds(pl.multiple_of(c * CHUNK, CHUNK), CHUNK)
            for h in range(hp):
                ls = slice(h * G_DIM, (h + 1) * G_DIM)
                lb, ghn = lb_all[:, ls], ghn_ref[:, ls]
                hq, hg = hq_ref[sl, ls], hg_ref[sl, ls]
                q, k, f, _, sig, sq = _gates(hq, hf_ref[sl, ls], lb)
                v = hi_ref[sl, ls]
                b2 = ball_ref[sl, ls]
                b_s[h] = b2
                st = sall_ref[c, h]
                dstn = dst_ref[h]
                o = o_ref[sl, ls]
                dout = dout_ref[sl, ls]
                n, r = _rms(o, G_DIM)
                sg = _sigmoid(hg)
                dhg_ref[sl, ls] = dout * (n * ghn) * _dsilu(hg, sg)
                do, dg = _rms_bwd(n, r, ghn, dout * (hg * sg), G_DIM)
                acc_g[:, ls] += dg
                eb = jnp.exp2(b2)
                bl = b_s[h, CHUNK - 1:CHUNK, :]
                ebl = jnp.exp2(bl)
                ekd = jnp.exp2(bl - b2)
                kd = k * ekd
                a = aall_ref[c, h]
                dq_i, dk_i = _intra(q, k, b2, b_s.at[h], codes, _dot_nt(do, v))
                dhi_ref[sl, ls] = _dot_tn(a, do) + _dot_nt(kd, dstn)
                dk_state = _dot(v, dstn) * ekd
                dq = dq_i + _dot(do, st) * eb
                dk = dk_i + dk_state
                dbl = jnp.sum(k * dk_state, 0, keepdims=True) + ebl * jnp.sum(dstn * st, 0, keepdims=True)
                db = q * dq - k * dk + jnp.where(last, dbl, 0.0)
                df = _tri_mm(tri_t, db) / f - dk
                dhf_ref[sl, ls] = df * (1.0 - lb) * sig * (1.0 - sig)
                acc_lb[:, ls] += jnp.sum(df * (1.0 - sig), 0, keepdims=True)
                dhq_ref[sl, ls] = dq * _dsilu(hq, sq)
                dst_ref[h] = dstn * ebl + _dot_tn(do, q * eb)
            return carry

        lax.fori_loop(0, ncb, chunk, 0, unroll=4)

        @pl.when(t == nb - 1)
        def _():
            dl0 = acc_lb[...] * lb_all * (1.0 - lb_all)
            dlbl_ref[0:1, :] = dl0
            dlbl_ref[1:2, :] = -dl0
            dghn_ref[...] = acc_g[...]

    col_blk = lambda j: pl.BlockSpec((tb, W), lambda p, t: (nb - 1 - t, j * (G_HEADS // hp) + p))
    head = pl.BlockSpec((tb, W), lambda p, t: (nb - 1 - t, p))
    two = pl.BlockSpec((2, W), lambda p, t: (0, p))
    one = pl.BlockSpec((1, W), lambda p, t: (0, p))
    res = _pcall(
        body, "hgrn_bwd", (G_HEADS // hp, nb),
        [col_blk(0), col_blk(1), col_blk(2), col_blk(3), head,
         pl.BlockSpec((ncb, hp, G_DIM, G_DIM), lambda p, t: (nb - 1 - t, p, 0, 0)),
         pl.BlockSpec((ncb, hp, CHUNK, CHUNK), lambda p, t: (nb - 1 - t, p, 0, 0)), head, head, two, one],
        [head, head, head, head, two, one],
        [_sds((T, G_W))] * 4 + [_sds((2, G_W)), _sds((1, G_W))],
        scratch=[pltpu.VMEM((hp, G_DIM, G_DIM), F32), pltpu.VMEM((hp, CHUNK, G_DIM), F32),
                 pltpu.VMEM((1, W), F32), pltpu.VMEM((1, W), F32)], exchange=exchange,
    )(xp, xp, xp, xp, o_raw, s_all, a_all, b_all, d_out, lb_logits, g_hn)
    return res


def _in_bwd_x(x, dxp_m, dxp_h, dh1, w_in_al, g_pre, vecs, tm, exchange=None):
    T = x.shape[0]
    nt = T // tm
    n_v = len(vecs)
    offs, rows = _row_offsets([g_pre] + list(vecs))

    def body(*refs):
        x_ref, dm_ref, d0_ref, d1_ref, d2_ref, d3_ref, dh1_ref, w_ref, g_ref = refs[:9]
        v_refs = refs[9:9 + n_v]
        dx_ref, dg_ref, rall, pk, send_sems, recv_sems, loc_sem = refs[9 + n_v:]
        i = pl.program_id(0)
        du = _dot_nt(dm_ref[...], w_ref[:, 0:MLA_IN])
        for j, d_ref in enumerate((d0_ref, d1_ref, d2_ref, d3_ref)):
            du = du + _dot_nt(d_ref[...], w_ref[:, MLA_IN + j * G_W:MLA_IN + (j + 1) * G_W])
        nx, r = _rms(x_ref[...], D)
        dx, dg = _rms_bwd(nx, r, g_ref[...], du, D)
        dx_ref[...] = dh1_ref[...] + dx

        @pl.when(i == 0)
        def _():
            dg_ref[...] = dg

        @pl.when(i > 0)
        def _():
            dg_ref[...] += dg

        @pl.when(i == nt - 1)
        def _():
            mx, my, mc = lax.axis_index("x"), lax.axis_index("y"), lax.axis_index("c")
            me = 4 * mx + 2 * my + mc
            pk[...] = jnp.zeros(pk.shape, F32)
            for p, v_ref in enumerate((dg_ref,) + tuple(v_refs)):
                vr, vn = v_ref.shape
                pk[offs[p]:offs[p] + vr, 0:vn] = v_ref[...]

            def copy(k, outgoing):
                px, py, pc = _peer(k, mx, my, mc)
                return pltpu.make_async_remote_copy(
                    src_ref=pk, dst_ref=rall.at[me if outgoing else 4 * px + 2 * py + pc],
                    send_sem=send_sems.at[k - 1], recv_sem=recv_sems.at[k - 1],
                    device_id=(px, py, pc), device_id_type=MESH)

            local = pltpu.make_async_copy(pk, rall.at[me], loc_sem)
            local.start()
            for k in range(1, N_DEV):
                copy(k, True).start()
            local.wait()
            for k in range(1, N_DEV):
                copy(k, False).wait_recv()
            for k in range(1, N_DEV):
                copy(k, True).wait_send()

    return _pcall(body, "in_bwd_x", (nt,),
                  [_rows(tm, D), _rows(tm, MLA_IN)] + [_rows(tm, G_W)] * 4 + [_rows(tm, D), _full((D, XP_W)), _full((1, D))]
                  + [_full(v.shape) for v in vecs],
                  [_rows(tm, D), _full((1, D)), _HBM], [_sds((T, D)), _sds((1, D)), _sds((N_DEV, rows, D))],
                  scratch=[pltpu.VMEM((rows, D), F32), pltpu.SemaphoreType.DMA((N_DEV - 1,)),
                           pltpu.SemaphoreType.DMA((N_DEV - 1,)), pltpu.SemaphoreType.DMA],
                  exchange=exchange)(x, dxp_m, *dxp_h, dh1, w_in_al, g_pre, *vecs)


def _aligned_col(c):
    return jnp.where(c < Q_RANK + KV_RANK + ROPE, c, c + (KR_PAD - ROPE))


def _align_w_in(g_in):
    tile = 384
    kr_end = Q_RANK + KV_RANK + ROPE

    def body(g_ref, o_ref, gp):
        gp[...] = jnp.zeros(gp.shape, BF)
        for j in range(N_DEV):
            gp[j, :, 0:IN_SH] = g_ref[j]
        r = lax.broadcasted_iota(jnp.int32, (tile, tile), 0)
        c = lax.broadcasted_iota(jnp.int32, (tile, tile), 1)
        for t in range(XP_W // tile):
            lo, hi = t * tile, (t + 1) * tile
            cols = [a if a < kr_end else a - (KR_PAD - ROPE) for a in (lo, hi - 1)]
            acc = jnp.zeros((D, tile), F32)
            for j in range(cols[0] // IN_SH, cols[-1] // IN_SH + 1):
                sel = (r < IN_SH) & (_aligned_col(j * IN_SH + r) == lo + c)
                acc = acc + jnp.dot(gp[j], sel.astype(BF), preferred_element_type=F32)
            o_ref[:, lo:hi] = acc.astype(BF)

    vm = pl.BlockSpec(memory_space=pltpu.VMEM)
    return pl.pallas_call(
        body, name="align_w_in", in_specs=[vm], out_specs=vm, out_shape=_sds((D, XP_W), BF),
        scratch_shapes=[pltpu.VMEM((N_DEV, D, tile), BF)],
        compiler_params=pltpu.CompilerParams(vmem_limit_bytes=_VMEM_LIMIT))(g_in)


def _in_bwd_w(name, u, dxp_m, dxp_h, tm, row0, nr, exchange=None):
    T = u.shape[0]
    nt = T // tm
    win = 640

    def body(u_ref, dm_ref, d0_ref, d1_ref, d2_ref, d3_ref, o_ref, acc):
        i = pl.program_id(0)
        ut = u_ref[:, row0:row0 + nr].T
        parts = [(0, MLA_IN, dm_ref)] + [(MLA_IN + j * G_W, G_W, d) for j, d in enumerate((d0_ref, d1_ref, d2_ref, d3_ref))]

        @pl.when(i == 0)
        def _():
            for lo, n, d in parts:
                acc[:, lo:lo + n] = jnp.dot(ut, d[...].astype(BF), preferred_element_type=F32)

        @pl.when(i > 0)
        def _():
            for lo, n, d in parts:
                acc[:, lo:lo + n] += jnp.dot(ut, d[...].astype(BF), preferred_element_type=F32)

        @pl.when(i == nt - 1)
        def _():
            wide = 384
            r = lax.broadcasted_iota(jnp.int32, (win, wide), 0)
            c = lax.broadcasted_iota(jnp.int32, (win, wide), 1)
            kr_end = Q_RANK + KV_RANK + ROPE
            for j in range(N_DEV):
                first = j * IN_SH if j * IN_SH < kr_end else j * IN_SH + (KR_PAD - ROPE)
                lo = min(first // HB * HB, XP_W - win)
                sel = (c < IN_SH) & (_aligned_col(j * IN_SH + c) == lo + r)
                res = jnp.dot(acc[:, lo:lo + win].astype(BF), sel.astype(BF), preferred_element_type=F32)
                o_ref[j] = res[:, 0:IN_SH].astype(BF)

    return _pcall(body, name, (nt,),
                  [_rows(tm, D), _rows(tm, MLA_IN)] + [_rows(tm, G_W)] * 4,
                  [_full((N_DEV, nr, IN_SH))], [_sds((N_DEV, nr, IN_SH), BF)],
                  scratch=[pltpu.VMEM((nr, XP_W), F32)], exchange=exchange)(u, dxp_m, *dxp_h)


def _pad_heads(w, width, real):
    lead = w.shape[:-1]
    w = w.reshape(lead + (N_HEADS, real))
    w = jnp.pad(w, [(0, 0)] * len(lead) + [(0, 0), (0, width - real)])
    return w.reshape(lead + (N_HEADS * width,))


def _unpad_heads(w, width, real):
    lead = w.shape[:-1]
    return w.reshape(lead + (N_HEADS, width))[..., :real].reshape(lead + (N_HEADS * real,))


def _rope_tables(positions):
    half = ROPE // 2
    inv_freq = 1.0 / (ROPE_THETA ** (jnp.arange(0, ROPE, 2, dtype=F32) / ROPE))
    ang = positions.astype(F32)[:, None] * inv_freq
    cos, sin = jnp.cos(ang), jnp.sin(ang)
    T = positions.shape[0]
    z = lambda n: jnp.zeros((T, n), F32)
    ta = jnp.concatenate([jnp.ones((T, NOPE), F32), cos, cos, z(HB - QK_DIM)], 1)
    tb1 = jnp.concatenate([z(NOPE), -sin, z(half), z(HB - QK_DIM)], 1)
    tb2 = jnp.concatenate([z(NOPE), z(half), sin, z(HB - QK_DIM)], 1)
    return ta, tb1, tb2


def kernel(x, positions, attn_pre_norm, w_in, mla_q_norm, mla_w_uq, mla_kv_norm, mla_w_ukv, mla_out_norm, hgrn_lb_logits, hgrn_out_norm, w_out, attn_post_norm, ffn_pre_norm, w_gate, w_up, w_down, ffn_post_norm, loss_target, m_attn_pre_norm, m_w_in, m_mla_q_norm, m_mla_w_uq, m_mla_kv_norm, m_mla_w_ukv, m_mla_out_norm, m_hgrn_lb_logits, m_hgrn_out_norm, m_w_out, m_attn_post_norm, m_ffn_pre_norm, m_w_gate, m_w_up, m_w_down, m_ffn_post_norm, v_attn_pre_norm, v_w_in, v_mla_q_norm, v_mla_w_uq, v_mla_kv_norm, v_mla_w_ukv, v_mla_out_norm, v_hgrn_lb_logits, v_hgrn_out_norm, v_w_out, v_attn_post_norm, v_ffn_pre_norm, v_w_gate, v_w_up, v_w_down, v_ffn_post_norm):
    T = x.shape[1]
    tm = min(_TM, T)
    tq = min(_TQ, T)
    xs, tgt = x[0], loss_target[0]
    uq_sh = (Q_RANK // N_DEV, N_HEADS * QK_DIM)

    b_in, b_uq, b_out, b_gu, b_d = _cast_shards(
        w_in[0], mla_w_uq[0].reshape(uq_sh), w_out[0], w_gate[0].T, w_up[0].T, w_down[0])
    g_in, g_uq = _gather_two_level("ag_first", [b_in, b_uq])
    w_in_al = _align_w_in(g_in)
    w_uq_p = _pad_heads(g_uq.reshape(Q_RANK, N_HEADS * QK_DIM), HB, QK_DIM)
    w_ukv = mla_w_ukv[0].astype(BF)
    w_uk_p = _pad_heads(w_ukv[..., :NOPE].reshape(KV_RANK, N_HEADS * NOPE), HB, NOPE)
    w_uv_p = _pad_heads(w_ukv[..., NOPE:].reshape(KV_RANK, N_HEADS * V_DIM), HB, V_DIM)
    g_on_p = _pad_heads(mla_out_norm, HB, V_DIM)
    tabs = _rope_tables(positions[0])

    xp_m, xp_h, u = _fwd_in(xs, attn_pre_norm, w_in_al, tm)
    q_att, qs_att, k_att, v_att = _mla_prep(xp_m, tabs, mla_q_norm, mla_kv_norm, w_uq_p, w_uk_p, w_uv_p, tm)
    o_hgrn, o_raw, s_all, a_all, b_all, wd = _hgrn_fwd(xp_h, hgrn_lb_logits, hgrn_out_norm, ([GATHER], [b_d]))
    wd = wd.reshape(N_DEV * FF_PAD, D)
    o_pad, lse, wgu, g_out = _flash_fwd(qs_att, k_att, v_att, tq, ([GATHER, GATHER], [b_gu, b_out]))
    wgu = wgu.reshape(N_DEV * 2 * FF_PAD, D)
    w_out_full = g_out.reshape(D, D)
    w_out_mla = jnp.pad(w_out_full[:N_HEADS * V_DIM].reshape(N_HEADS, V_DIM, D), ((0, 0), (0, HB - V_DIM), (0, 0)))
    w_out_p = jnp.concatenate([w_out_mla.reshape(N_HEADS * HB, D), w_out_full[N_HEADS * V_DIM:]], 0)
    h1, y1, z, mix = _fwd_out(o_pad, o_hgrn, xs, g_on_p, w_out_p, attn_post_norm, ffn_pre_norm, tm)
    tmf = min(_TMF, T)
    gs, us, ffs, dh2, dy2, d_fpost, loss_row = _ffn_fwd(z, wgu, wd, h1, tgt, ffn_post_norm, tmf, _FB // FF_PAD)

    dgu, dh1, dy1, d_fpre, d_post = _ffn_bwd_x(dy2, gs, us, wgu, wd, h1, y1, dh2, ffn_pre_norm, attn_post_norm, tm)
    dwgu, dwd = _ffn_bwd_w(z, ffs, dgu, dy2, tmf)
    do_pad, dl, d_ohg, dw_out_p, d_on_p = _out_bwd(dy1, mix, o_pad, w_out_p, g_on_p, tm)
    dw_out_mla = dw_out_p[:N_HEADS * HB].reshape(N_HEADS, HB, D)[:, :V_DIM].reshape(N_HEADS * V_DIM, D)
    dw_out = jnp.concatenate([dw_out_mla, dw_out_p[N_HEADS * HB:]], 0).reshape(N_DEV, D // N_DEV, D).astype(BF)
    dk_att, dv_att, dq_att, p_gu, p_d, p_out = _flash_bwd(
        q_att, k_att, v_att, do_pad, lse, dl, tq,
        ([SCATTER] * 3, [dwgu.reshape(N_DEV, 2 * FF_PAD, D), dwd.reshape(N_DEV, FF_PAD, D), dw_out]))
    dxp_m, dw_uq_p, dw_uk_p, dw_uv_p, d_gq, d_gkv = _mla_prep_bwd(
        xp_m, tabs, dq_att, dk_att, dv_att, mla_q_norm, mla_kv_norm, w_uq_p, w_uk_p, w_uv_p, tm)
    dw_uq = _unpad_heads(dw_uq_p, HB, QK_DIM).reshape((N_DEV,) + uq_sh).astype(BF)
    dw_ukv = jnp.concatenate([_unpad_heads(dw_uk_p, HB, NOPE).reshape(KV_RANK, N_HEADS, NOPE),
                              _unpad_heads(dw_uv_p, HB, V_DIM).reshape(KV_RANK, N_HEADS, V_DIM)], -1)
    *dxp_h, d_lbl, d_ghn, p_uq, dw_ukv_all = _hgrn_bwd(
        xp_h, o_raw, s_all, a_all, b_all, d_ohg, hgrn_lb_logits, hgrn_out_norm,
        ([SCATTER, GATHER], [dw_uq, dw_ukv.reshape(KV_RANK, N_HEADS * HB)]))
    dw_in_a, = _in_bwd_w("in_bwd_w_a", u, dxp_m, dxp_h, tm, 0, _IN_ROWS_A)
    dw_in_b, p_in_a = _in_bwd_w("in_bwd_w_b", u, dxp_m, dxp_h, tm, _IN_ROWS_A, D - _IN_ROWS_A, ([SCATTER], [dw_in_a]))
    d_on = _unpad_heads(d_on_p, HB, V_DIM)
    vecs = [d_gq, d_gkv, d_on, d_lbl, d_ghn, d_post, d_fpre, d_fpost, loss_row]
    grad_x, _, rall, p_in_b = _in_bwd_x(xs, dxp_m, dxp_h, dh1, w_in_al, attn_pre_norm, vecs, tm, ([SCATTER], [dw_in_b]))

    ukv2 = lambda a: a.reshape(KV_RANK, N_HEADS * HB)
    small_w = [attn_pre_norm, mla_q_norm, mla_kv_norm, ukv2(mla_w_ukv), mla_out_norm, hgrn_lb_logits, hgrn_out_norm,
               attn_post_norm, ffn_pre_norm, ffn_post_norm]
    small_m = [m_attn_pre_norm, m_mla_q_norm, m_mla_kv_norm, ukv2(m_mla_w_ukv), m_mla_out_norm, m_hgrn_lb_logits,
               m_hgrn_out_norm, m_attn_post_norm, m_ffn_pre_norm, m_ffn_post_norm]
    small_v = [v_attn_pre_norm, v_mla_q_norm, v_mla_kv_norm, ukv2(v_mla_w_ukv), v_mla_out_norm, v_hgrn_lb_logits,
               v_hgrn_out_norm, v_attn_post_norm, v_ffn_pre_norm, v_ffn_post_norm]
    s_g, s_d, s_m, s_v, loss_all = _small_adam(rall, dw_ukv_all, 3, small_w, small_m, small_v)
    r_in = _shard_adam("adam_w_in", [p_in_a, p_in_b], w_in[0], m_w_in[0], v_w_in[0], 128)
    r_uq = _shard_adam("adam_w_uq", [p_uq], mla_w_uq[0].reshape(uq_sh), m_mla_w_uq[0].reshape(uq_sh),
                       v_mla_w_uq[0].reshape(uq_sh), uq_sh[0])
    r_out = _shard_adam("adam_w_out", [p_out], w_out[0], m_w_out[0], v_w_out[0], D // N_DEV)
    r_g, r_u = _gate_up_adam(p_gu, (w_gate[0].T, w_up[0].T), (m_w_gate[0].T, m_w_up[0].T),
                             (v_w_gate[0].T, v_w_up[0].T))
    r_g, r_u = [a.T for a in r_g], [a.T for a in r_u]
    r_d = _shard_adam("adam_w_down", [p_d], w_down[0], m_w_down[0], v_w_down[0], FF_SH // 2)

    loss = loss_all[0, 0]

    def assemble(big, small):
        b_in, b_uq, b_out, b_g, b_u, b_d = big
        return [small[0], b_in[None], small[1], b_uq.reshape(mla_w_uq.shape), small[2],
                small[3].reshape(mla_w_ukv.shape), small[4], small[5], small[6], b_out[None], small[7], small[8],
                b_g[None], b_u[None], b_d[None], small[9]]

    outs = [loss, grad_x[None]]
    for idx, small in enumerate((s_g, s_d, s_m, s_v)):
        outs += assemble([r[idx] for r in (r_in, r_uq, r_out, r_g, r_u, r_d)], small)
    return tuple(outs)
```

```python
import jax
import jax.numpy as jnp
from jax import lax
from jax.experimental import pallas as pl
from jax.experimental.pallas import tpu as pltpu

BF = jnp.bfloat16
F32 = jnp.float32
MESH = pl.DeviceIdType.MESH

N_DEV = 8
D = 1024
EPS = 1e-6
LOG2E = 1.4426950408889634
ROPE_THETA = 10000.0
N_HEADS = 8
HB = 128
NOPE = 64
ROPE = 32
V_DIM = 64
QK_DIM = NOPE + ROPE
Q_RANK = 384
KV_RANK = 128
KR_PAD = 128
MLA_IN = Q_RANK + KV_RANK + KR_PAD
G_HEADS = 4
G_DIM = 128
G_W = G_HEADS * G_DIM
CHUNK = 64
SUB = 16
XP_W = MLA_IN + 4 * G_W
IN_SH = 324
FF_SH = 352
FF_PAD = 384
MIX_W = N_HEADS * HB + G_W

ADAM_LR = 0.001
ADAM_B1 = 0.9
ADAM_B2 = 0.999
ADAM_EPS = 1e-08
ADAM_WD = 0.01
ADAM_STEP = 10

_TM = 512
_TMF = 1024
_IN_ROWS_A = 384
_TQ = 512
_AH = 2
_AH_FWD = 4
_FB = 768
_TB = 1024
_TB_BWD = 512
_HP = 4
V7X_VMEM_BYTES = 64 * 1024 * 1024
_VMEM_LIMIT = V7X_VMEM_BYTES - 8 * 1024 * 1024
NEG = -1e30


def _dot(a, b):
    return jnp.dot(a.astype(BF), b.astype(BF), preferred_element_type=F32)


def _dot_nt(a, b):
    return lax.dot_general(a.astype(BF), b.astype(BF), (((1,), (1,)), ((), ())), preferred_element_type=F32)


def _dot_tn(a, b):
    return lax.dot_general(a.astype(BF), b.astype(BF), (((0,), (0,)), ((), ())), preferred_element_type=F32)


def _sigmoid(x):
    return 1.0 / (1.0 + jnp.exp(-x))


def _rms(x, n):
    r = lax.rsqrt(jnp.sum(x * x, -1, keepdims=True) * (1.0 / n) + EPS)
    return x * r, r


def _rms_bwd(nx, r, g, dy, n):
    dg = jnp.sum(dy * nx, 0, keepdims=True)
    dn = dy * g
    dx = r * (dn - nx * (jnp.sum(dn * nx, -1, keepdims=True) * (1.0 / n)))
    return dx, dg


def _adamw(w, g, m, v):
    m2 = ADAM_B1 * m + (1.0 - ADAM_B1) * g
    v2 = ADAM_B2 * v + (1.0 - ADAM_B2) * (g * g)
    m_hat = m2 / (1.0 - ADAM_B1 ** ADAM_STEP)
    v_hat = v2 / (1.0 - ADAM_B2 ** ADAM_STEP)
    delta = -ADAM_LR * (m_hat / (jnp.sqrt(v_hat) + ADAM_EPS) + ADAM_WD * w)
    return delta, m2, v2


def _pcall(body, name, grid, in_specs, out_specs, out_shape, scratch=(), exchange=None):
    scratch = list(scratch)
    extra = ()
    if exchange is not None:
        kinds, extra = exchange
        in_specs, out_specs, out_shape = list(in_specs), list(out_specs), list(out_shape)
        n_in, n_out, n_scr, n_x = len(in_specs), len(out_specs), len(scratch), len(extra)
        inner = body

        def body(*refs):
            ins, rest = refs[:n_in], refs[n_in:]
            x_src, rest = rest[:n_x], rest[n_x:]
            outs, rest = rest[:n_out], rest[n_out:]
            x_dst, rest = rest[:n_x], rest[n_x:]
            step, total = pl.program_id(0), grid[0]
            for a in range(1, len(grid)):
                step, total = step * grid[a] + pl.program_id(a), total * grid[a]
            if all(kind == GATHER for kind in kinds) and total >= 4:
                ex = _TwoLevelGather(x_src, x_dst, *rest[n_scr:])
                pl.when(step == 0)(ex.start)
                inner(*ins, *outs, *rest[:n_scr])
                pl.when(step == (2 * total) // 3)(ex.forward)
                pl.when(step == total - 1)(ex.finish)
            else:
                ex = _Exchange(kinds, x_src, x_dst, *rest[n_scr:])
                pl.when(step == 0)(ex.start)
                inner(*ins, *outs, *rest[:n_scr])
                pl.when(step == total - 1)(ex.wait)

        in_specs += [_HBM] * n_x
        out_specs += [_HBM] * n_x
        out_shape += _exchange_shapes(kinds, extra)
        scratch += _exchange_sems(n_x)
    call = pl.pallas_call(
        body, name=name, grid=grid, in_specs=in_specs, out_specs=out_specs, out_shape=out_shape,
        scratch_shapes=scratch,
        compiler_params=pltpu.CompilerParams(
            dimension_semantics=("arbitrary",) * len(grid), vmem_limit_bytes=_VMEM_LIMIT))
    return lambda *operands: call(*operands, *extra)


def _full(shape):
    return pl.BlockSpec(shape, lambda *_: (0,) * len(shape))


def _rows(tm, n):
    return pl.BlockSpec((tm, n), lambda i, *_: (i, 0))


def _sds(shape, dtype=F32):
    return jax.ShapeDtypeStruct(shape, dtype)


def _peer(k, x, y, c):
    px = 1 - x if (k >> 2) & 1 else x
    py = 1 - y if (k >> 1) & 1 else y
    pc = 1 - c if k & 1 else c
    return px, py, pc


GATHER, SCATTER = "gather", "scatter"


class _Exchange:
    def __init__(self, kinds, srcs, dsts, send_sems, recv_sems, loc_sems):
        self.kinds, self.srcs, self.dsts = kinds, srcs, dsts
        self.send_sems, self.recv_sems, self.loc_sems = send_sems, recv_sems, loc_sems
        self.x, self.y, self.c = lax.axis_index("x"), lax.axis_index("y"), lax.axis_index("c")
        self.me = 4 * self.x + 2 * self.y + self.c

    def _src(self, w, slot):
        return self.srcs[w] if self.kinds[w] == GATHER else self.srcs[w].at[slot]

    def _dst(self, w, slot):
        return self.dsts[w].at[slot]

    def _copy(self, w, k, outgoing):
        px, py, pc = _peer(k, self.x, self.y, self.c)
        pid = 4 * px + 2 * py + pc
        return pltpu.make_async_remote_copy(
            src_ref=self._src(w, pid if outgoing else self.me),
            dst_ref=self._dst(w, self.me if outgoing else pid),
            send_sem=self.send_sems.at[w, k - 1], recv_sem=self.recv_sems.at[w, k - 1],
            device_id=(px, py, pc), device_id_type=MESH)

    def _local(self, w):
        return pltpu.make_async_copy(self._src(w, self.me), self._dst(w, self.me), self.loc_sems.at[w])

    def start(self):
        for w in range(len(self.srcs)):
            self._local(w).start()
            for k in range(1, N_DEV):
                self._copy(w, k, True).start()

    def wait(self):
        for w in range(len(self.srcs)):
            self._local(w).wait()
            for k in range(1, N_DEV):
                self._copy(w, k, False).wait_recv()
        for w in range(len(self.srcs)):
            for k in range(1, N_DEV):
                self._copy(w, k, True).wait_send()


class _TwoLevelGather:
    def __init__(self, srcs, dsts, send_sems, recv_sems, loc_sems):
        self.srcs, self.dsts = srcs, dsts
        self.send_sems, self.recv_sems, self.loc_sems = send_sems, recv_sems, loc_sems
        x, y, c = lax.axis_index("x"), lax.axis_index("y"), lax.axis_index("c")
        self.c = c
        self.me, self.sibling = (x, y, c), (x, y, 1 - c)
        self.chips = [(1 - x, y), (x, 1 - y), (1 - x, 1 - y)]

    def _copy(self, w, k, block, to, own=False):
        slot = 4 * block[0] + 2 * block[1] + block[2]
        return pltpu.make_async_remote_copy(
            src_ref=self.srcs[w] if own else self.dsts[w].at[slot], dst_ref=self.dsts[w].at[slot],
            send_sem=self.send_sems.at[w, k], recv_sem=self.recv_sems.at[w, k], device_id=to, device_id_type=MESH)

    def _local(self, w):
        slot = 4 * self.me[0] + 2 * self.me[1] + self.me[2]
        return pltpu.make_async_copy(self.srcs[w], self.dsts[w].at[slot], self.loc_sems.at[w])

    def _first(self, w):
        return [self._copy(w, 0, self.me, self.sibling, own=True)] + [
            self._copy(w, 1 + j, self.me, (*chip, self.c), own=True) for j, chip in enumerate(self.chips)]

    def _passed(self, w):
        return [self._copy(w, 4 + j, (*chip, self.c), self.sibling) for j, chip in enumerate(self.chips)]

    def start(self):
        for w in range(len(self.srcs)):
            self._local(w).start()
            for cp in self._first(w):
                cp.start()

    def forward(self):
        for w in range(len(self.srcs)):
            for j, chip in enumerate(self.chips):
                self._copy(w, 1 + j, (*chip, self.c), self.me).wait_recv()
                self._passed(w)[j].start()

    def finish(self):
        for w in range(len(self.srcs)):
            self._copy(w, 0, self.sibling, self.me).wait_recv()
            for j, chip in enumerate(self.chips):
                self._copy(w, 4 + j, (*chip, 1 - self.c), self.me).wait_recv()
        for w in range(len(self.srcs)):
            for cp in self._first(w) + self._passed(w):
                cp.wait_send()
            self._local(w).wait()


def _exchange_sems(n_w):
    return [pltpu.SemaphoreType.DMA((n_w, N_DEV - 1)), pltpu.SemaphoreType.DMA((n_w, N_DEV - 1)),
            pltpu.SemaphoreType.DMA((n_w,))]


def _exchange_shapes(kinds, srcs):
    return [_sds(((N_DEV,) if kd == GATHER else ()) + tuple(s.shape), s.dtype) for kd, s in zip(kinds, srcs)]


_HBM = pl.BlockSpec(memory_space=pl.ANY)


def _cast_shards(w_in, w_uq, w_out, w_gate_t, w_up_t, w_down):
    shapes = [(D, IN_SH), (Q_RANK // N_DEV, N_HEADS * QK_DIM), (D // N_DEV, D), (2 * FF_PAD, D), (FF_PAD, D)]

    def body(win, wuq, wout, wg, wu, wd, sin_, suq, sout, sgu, sd):
        sin_[...] = win[...].astype(BF)
        suq[...] = wuq[...].astype(BF)
        sout[...] = wout[...].astype(BF)
        sgu[...] = jnp.zeros(sgu.shape, BF)
        sgu[0:FF_SH, :] = wg[...].astype(BF)
        sgu[FF_PAD:FF_PAD + FF_SH, :] = wu[...].astype(BF)
        sd[...] = jnp.zeros(sd.shape, BF)
        sd[0:FF_SH, :] = wd[...].astype(BF)

    vm = pl.BlockSpec(memory_space=pltpu.VMEM)
    return pl.pallas_call(
        body, name="cast_shards", in_specs=[vm] * 6, out_specs=[vm] * 5,
        out_shape=[_sds(s, BF) for s in shapes],
        compiler_params=pltpu.CompilerParams(vmem_limit_bytes=_VMEM_LIMIT),
    )(w_in, w_uq, w_out, w_gate_t, w_up_t, w_down)


def _gather_two_level(name, srcs):
    n_w = len(srcs)

    def body(*refs):
        src, dst = refs[:n_w], refs[n_w:2 * n_w]
        send_sems, recv_sems, loc_sems = refs[2 * n_w:]
        x, y, c = lax.axis_index("x"), lax.axis_index("y"), lax.axis_index("c")
        me, sibling = (x, y, c), (x, y, 1 - c)
        chips = [(1 - x, y), (x, 1 - y), (1 - x, 1 - y)]
        slot = lambda p: 4 * p[0] + 2 * p[1] + p[2]

        def copy(w, k, block, to, own=False):
            return pltpu.make_async_remote_copy(
                src_ref=src[w] if own else dst[w].at[slot(block)], dst_ref=dst[w].at[slot(block)],
                send_sem=send_sems.at[w, k], recv_sem=recv_sems.at[w, k], device_id=to, device_id_type=MESH)

        local = [pltpu.make_async_copy(src[w], dst[w].at[slot(me)], loc_sems.at[w]) for w in range(n_w)]
        first, passed = [], []
        for w in range(n_w):
            local[w].start()
            first.append(copy(w, 0, me, sibling, own=True))
            first += [copy(w, 1 + j, me, (*chip, c), own=True) for j, chip in enumerate(chips)]
        for cp in first:
            cp.start()
        for w in range(n_w):
            for j, chip in enumerate(chips):
                copy(w, 1 + j, (*chip, c), me).wait_recv()
                passed.append(copy(w, 4 + j, (*chip, c), sibling))
                passed[-1].start()
        for w in range(n_w):
            copy(w, 0, sibling, me).wait_recv()
            for j, chip in enumerate(chips):
                copy(w, 4 + j, (*chip, 1 - c), me).wait_recv()
        for cp in first + passed:
            cp.wait_send()
        for w in range(n_w):
            local[w].wait()

    return pl.pallas_call(
        body, name=name, in_specs=[_HBM] * n_w, out_specs=[_HBM] * n_w,
        out_shape=_exchange_shapes([GATHER] * n_w, srcs), scratch_shapes=_exchange_sems(n_w))(*srcs)


def _row_offsets(arrays):
    offs, rows = [], 0
    for a in arrays:
        offs.append(rows)
        rows += a.shape[0]
    return offs, -(-rows // 8) * 8


def _small_adam(rall, big_parts, big, ws, ms, vs):
    n_p = len(ws)
    packed = [w for p, w in enumerate(ws) if p != big] + [jax.ShapeDtypeStruct((1, HB), F32)]
    offs, _ = _row_offsets(packed)
    offs = offs[:big] + [None] + offs[big:]

    def total(ref, sl):
        g = ref[(0,) + sl]
        for j in range(1, N_DEV):
            g = g + ref[(j,) + sl]
        return g

    def body(*refs):
        rall_ref, big_ref = refs[:2]
        w_refs, m_refs, v_refs = refs[2:2 + n_p], refs[2 + n_p:2 + 2 * n_p], refs[2 + 2 * n_p:2 + 3 * n_p]
        outs = refs[2 + 3 * n_p:]
        for p in range(n_p):
            r, n = w_refs[p].shape
            if p == big:
                g = total(big_ref, (slice(0, r), slice(0, n)))
            else:
                g = total(rall_ref, (slice(offs[p], offs[p] + r), slice(0, n)))
            delta, m2, v2 = _adamw(w_refs[p][...], g, m_refs[p][...], v_refs[p][...])
            outs[p][...] = g
            outs[n_p + p][...] = delta
            outs[2 * n_p + p][...] = m2
            outs[3 * n_p + p][...] = v2
        outs[4 * n_p][...] = total(rall_ref, (slice(offs[n_p], offs[n_p] + 1), slice(0, HB)))

    vm = pl.BlockSpec(memory_space=pltpu.VMEM)
    res = pl.pallas_call(
        body, name="small_adam", in_specs=[vm] * (2 + 3 * n_p), out_specs=[vm] * (4 * n_p + 1),
        out_shape=[_sds(w.shape) for w in ws] * 4 + [_sds((1, HB))],
        compiler_params=pltpu.CompilerParams(vmem_limit_bytes=_VMEM_LIMIT),
    )(rall, big_parts, *ws, *ms, *vs)
    return res[:n_p], res[n_p:2 * n_p], res[2 * n_p:3 * n_p], res[3 * n_p:4 * n_p], res[4 * n_p]


def _device_sum(p_ref):
    g = p_ref[0].astype(F32)
    for j in range(1, N_DEV):
        g = g + p_ref[j].astype(F32)
    return g


def _shard_adam(name, parts, w, m, v, tr):
    a0, b0 = w.shape
    n_p = len(parts)
    b = parts[0].shape[2]
    first = [0]
    for p in parts:
        first.append(first[-1] + p.shape[1] // tr)

    def body(*refs):
        p_refs = refs[:n_p]
        w_ref, m_ref, v_ref, g_out, d_out, m_out, v_out = refs[n_p:]
        i = pl.program_id(0)
        g = _device_sum(p_refs[0])
        for k in range(1, n_p):
            g = jnp.where(i >= first[k], _device_sum(p_refs[k]), g)
        g = g[:, 0:b0]
        delta, m2, v2 = _adamw(w_ref[...], g, m_ref[...], v_ref[...])
        g_out[...] = g
        d_out[...] = delta
        m_out[...] = m2
        v_out[...] = v2

    def part_spec(k):
        last = first[k + 1] - first[k] - 1
        return pl.BlockSpec((N_DEV, tr, b), lambda i: (0, jnp.minimum(jnp.maximum(i - first[k], 0), last), 0))

    blk = pl.BlockSpec((tr, b0), lambda i: (i, 0))
    return _pcall(
        body, name, (a0 // tr,), [part_spec(k) for k in range(n_p)] + [blk, blk, blk],
        [blk] * 4, [_sds((a0, b0))] * 4)(*parts, w, m, v)


def _gate_up_adam(parts, ws, ms, vs):
    tc = 256

    def body(p_ref, wg, wu, mg, mu, vg, vu, *outs):
        g = _device_sum(p_ref)
        for k, (w_ref, m_ref, v_ref) in enumerate(((wg, mg, vg), (wu, mu, vu))):
            gk = g[k * FF_PAD:k * FF_PAD + FF_SH]
            delta, m2, v2 = _adamw(w_ref[...], gk, m_ref[...], v_ref[...])
            for o, val in zip(outs[4 * k:4 * k + 4], (gk, delta, m2, v2)):
                o[...] = val

    blk = pl.BlockSpec((FF_SH, tc), lambda i: (0, i))
    res = _pcall(
        body, "adam_w_gate_up", (D // tc,), [pl.BlockSpec((N_DEV, 2 * FF_PAD, tc), lambda i: (0, 0, i))] + [blk] * 6,
        [blk] * 8, [_sds((FF_SH, D))] * 8)(parts, *ws, *ms, *vs)
    return res[:4], res[4:]


def _fwd_in(x, g_pre, w_in_al, tm):
    T = x.shape[0]

    def body(x_ref, g_ref, w_ref, xm_ref, xh_ref, u_ref):
        nx, _ = _rms(x_ref[...], D)
        u = (nx * g_ref[...]).astype(BF)
        u_ref[...] = u
        xm_ref[...] = jnp.dot(u, w_ref[:, 0:MLA_IN], preferred_element_type=F32)
        xh_ref[...] = jnp.dot(u, w_ref[:, MLA_IN:XP_W], preferred_element_type=F32)

    return _pcall(body, "fwd_in", (T // tm,),
                  [_rows(tm, D), _full((1, D)), _full((D, XP_W))],
                  [_rows(tm, MLA_IN), _rows(tm, 4 * G_W), _rows(tm, D)],
                  [_sds((T, MLA_IN)), _sds((T, 4 * G_W)), _sds((T, D), BF)])(x, g_pre, w_in_al)


def _rope(blk, ta, tb1, tb2):
    return blk * ta + pltpu.roll(blk, HB - ROPE // 2, 1) * tb1 + pltpu.roll(blk, ROPE // 2, 1) * tb2


def _unrope(d, ta, tb1, tb2):
    return d * ta + pltpu.roll(d * tb1, ROPE // 2, 1) + pltpu.roll(d * tb2, HB - ROPE // 2, 1)


def _mla_prep(xp, tabs, g_q, g_kv, w_uq, w_uk, w_uv, tm):
    T = xp.shape[0]
    W = N_HEADS * HB

    def body(xp_ref, ta_ref, tb1_ref, tb2_ref, gq_ref, gkv_ref, wuq_ref, wuk_ref, wuv_ref, q_ref, qs_ref, k_ref, v_ref):
        ta, tb1, tb2 = ta_ref[...], tb1_ref[...], tb2_ref[...]
        nq, _ = _rms(xp_ref[:, 0:Q_RANK], Q_RANK)
        nkv, _ = _rms(xp_ref[:, Q_RANK:Q_RANK + KV_RANK], KV_RANK)
        nkv = (nkv * gkv_ref[...]).astype(BF)
        qpre = _dot(nq * gq_ref[...], wuq_ref[...])
        kpre = jnp.dot(nkv, wuk_ref[...], preferred_element_type=F32)
        v = jnp.dot(nkv, wuv_ref[...], preferred_element_type=F32)
        lane = lax.broadcasted_iota(jnp.int32, (tm, W), 1)
        v_ref[...] = jnp.where((lane & (HB - 1)) == V_DIM, 1.0, v).astype(BF)
        kr = _rope(pltpu.roll(xp_ref[:, Q_RANK + KV_RANK:MLA_IN], NOPE, 1), ta, tb1, tb2)
        for h in range(N_HEADS):
            sl = slice(h * HB, (h + 1) * HB)
            qr = _rope(qpre[:, sl], ta, tb1, tb2)
            q_ref[:, sl] = qr.astype(BF)
            qs_ref[:, sl] = (qr * (QK_DIM ** -0.5 * LOG2E)).astype(BF)
            k_ref[:, sl] = (kpre[:, sl] + kr).astype(BF)

    tab = _rows(tm, HB)
    return _pcall(body, "mla_prep", (T // tm,),
                  [_rows(tm, MLA_IN), tab, tab, tab, _full((1, Q_RANK)), _full((1, KV_RANK)),
                   _full((Q_RANK, W)), _full((KV_RANK, W)), _full((KV_RANK, W))],
                  [_rows(tm, W)] * 4, [_sds((T, W), BF)] * 4)(xp, *tabs, g_q, g_kv, w_uq, w_uk, w_uv)


def _flash_fwd(q, k, v, tq, exchange=None):
    T = q.shape[0]
    hp = _AH_FWD
    W = hp * HB

    def body(q_ref, k_ref, v_ref, o_ref, lse_ref):
        i = pl.program_id(1)

        def blk(j, carry, masked):
            st = pl.multiple_of(j * tq, tq)
            out = []
            for h in range(hp):
                ls = slice(h * HB, (h + 1) * HB)
                m, acc = carry[h]
                s = _dot_nt(q_ref[:, ls], k_ref[pl.ds(st, tq), ls])
                if masked:
                    r = lax.broadcasted_iota(jnp.int32, (tq, tq), 0)
                    c = lax.broadcasted_iota(jnp.int32, (tq, tq), 1)
                    s = jnp.where(c <= r, s, NEG)
                m2 = jnp.maximum(m, jnp.max(s, -1, keepdims=True))
                p = jnp.exp2(s - m2)
                out.append((m2, jnp.exp2(m - m2) * acc + _dot(p, v_ref[pl.ds(st, tq), ls])))
            return tuple(out)

        init = tuple((jnp.full((tq, 1), NEG, F32), jnp.zeros((tq, HB), F32)) for _ in range(hp))
        carry = lax.fori_loop(0, i, lambda j, cr: blk(j, cr, False), init)
        res = blk(i, carry, True)
        lane = lax.broadcasted_iota(jnp.int32, (tq, HB), 1)
        for h in range(hp):
            ls = slice(h * HB, (h + 1) * HB)
            m, acc = res[h]
            l = acc[:, V_DIM:V_DIM + 1]
            o_ref[:, ls] = jnp.where(lane < V_DIM, acc / l, 0.0).astype(BF)
            lse_ref[:, ls] = jnp.broadcast_to(m * (1.0 / LOG2E) + jnp.log(l), (tq, HB))

    qs = pl.BlockSpec((tq, W), lambda h, i: (i, h))
    kvs = pl.BlockSpec((T, W), lambda h, i: (0, h))
    return _pcall(body, "flash_fwd", (N_HEADS // hp, T // tq), [qs, kvs, kvs], [qs, qs],
                  [_sds((T, N_HEADS * HB), BF), _sds((T, N_HEADS * HB))], exchange=exchange)(q, k, v)


def _gates(hq, hf, lb):
    sig = _sigmoid(hf)
    f = lb + (1.0 - lb) * sig
    sq = _sigmoid(hq)
    return hq * sq, 1.0 - f, f, jnp.log(f), sig, sq


def _lower_bound(lbl_ref):
    l0, l1 = lbl_ref[0:1, :], lbl_ref[1:2, :]
    mx = jnp.maximum(l0, l1)
    e0, e1 = jnp.exp(l0 - mx), jnp.exp(l1 - mx)
    return e0 / (e0 + e1)


def _split3(x):
    hi = x.astype(BF)
    r1 = x - hi.astype(F32)
    mid = r1.astype(BF)
    lo = (r1 - mid.astype(F32)).astype(BF)
    return hi, mid, lo


def _tri_mm(tri, x):
    hi, mid, lo = _split3(x)
    mm = lambda t: jnp.dot(tri, t, preferred_element_type=F32)
    return mm(hi) + mm(mid) + mm(lo)


def _intra_codes(sub):
    row = lax.broadcasted_iota(jnp.int32, (CHUNK, CHUNK), 0)
    col = lax.broadcasted_iota(jnp.int32, (CHUNK, CHUNK), 1)
    return sub, row, col


def _intra(q, k, b2, b_s, codes, da=None):
    grad = da is not None
    pow2 = (lambda x: jnp.exp2(jnp.minimum(x, 0.0))) if grad else jnp.exp2
    sub, row, col = codes
    a = jnp.zeros((CHUNK, CHUNK), F32)
    dq = jnp.zeros((CHUNK, G_DIM), F32)
    dk = jnp.zeros((CHUNK, G_DIM), F32)
    for i in range(1, CHUNK // sub):
        b0 = b_s[sub * i - 1:sub * i, :]
        eq, ek = pow2(b2 - b0), pow2(b0 - b2)
        mask = ((row // sub) == i) & (col < sub * i)
        if grad:
            dai = jnp.where(mask, da, 0.0)
            dq = dq + _dot(dai, k * ek) * eq
            dk = dk + _dot_tn(dai, q * eq) * ek
        else:
            a = jnp.where(mask, _dot_nt(q * eq, k * ek), a)
    for d in range(sub):
        ksh = pltpu.roll(k, d, 0) if d else k
        bsh = pltpu.roll(b2, d, 0) if d else b2
        e = pow2(b2 - bsh)
        mask = (col == row - d) & ((row & (sub - 1)) >= d)
        if grad:
            g = jnp.sum(jnp.where(mask, da, 0.0), -1, keepdims=True) * e
            dq = dq + g * ksh
            cb = g * q
            dk = dk + (pltpu.roll(cb, CHUNK - d, 0) if d else cb)
        else:
            a = jnp.where(mask, jnp.sum(q * ksh * e, -1, keepdims=True), a)
    return (dq, dk) if grad else a


def _hgrn_fwd(xp, lb_logits, g_hn, exchange=None):
    T = xp.shape[0]
    tb = min(_TB, T)
    ncb = tb // CHUNK
    hp = _HP
    W = hp * G_DIM

    def body(hq_ref, hf_ref, hi_ref, hg_ref, lbl_ref, ghn_ref, out_ref, oraw_ref, sall_ref, aall_ref, ball_ref,
             st_ref, b_s):
        lb_all = _lower_bound(lbl_ref)

        @pl.when(pl.program_id(1) == 0)
        def _():
            st_ref[...] = jnp.zeros(st_ref.shape, F32)

        row = lax.broadcasted_iota(jnp.int32, (CHUNK, CHUNK), 0)
        col = lax.broadcasted_iota(jnp.int32, (CHUNK, CHUNK), 1)
        tri = (col <= row).astype(BF)
        codes = _intra_codes(SUB)

        def chunk(c, carry):
            sl = pl.ds(pl.multiple_of(c * CHUNK, CHUNK), CHUNK)
            for h in range(hp):
                ls = slice(h * G_DIM, (h + 1) * G_DIM)
                q, k, _, lf, _, _ = _gates(hq_ref[sl, ls], hf_ref[sl, ls], lb_all[:, ls])
                v = hi_ref[sl, ls]
                b2 = _tri_mm(tri, lf) * LOG2E
                b_s[h] = b2
                ball_ref[sl, ls] = b2
                st = st_ref[h]
                sall_ref[c, h] = st
                a = _intra(q, k, b2, b_s.at[h], codes)
                aall_ref[c, h] = a
                o = _dot_nt(q * jnp.exp2(b2), st) + _dot(a, v)
                bl = b_s[h, CHUNK - 1:CHUNK, :]
                st_ref[h] = st * jnp.exp2(bl) + _dot_tn(v, k * jnp.exp2(bl - b2))
                oraw_ref[sl, ls] = o
                n, _ = _rms(o, G_DIM)
                hg = hg_ref[sl, ls]
                out_ref[sl, ls] = n * ghn_ref[:, ls] * (hg * _sigmoid(hg))
            return carry

        lax.fori_loop(0, ncb, chunk, 0, unroll=4)

    col_blk = lambda j: pl.BlockSpec((tb, W), lambda p, t: (t, j * (G_HEADS // hp) + p))
    head = pl.BlockSpec((tb, W), lambda p, t: (t, p))
    return _pcall(
        body, "hgrn_fwd", (G_HEADS // hp, T // tb),
        [col_blk(0), col_blk(1), col_blk(2), col_blk(3),
         pl.BlockSpec((2, W), lambda p, t: (0, p)), pl.BlockSpec((1, W), lambda p, t: (0, p))],
        [head, head, pl.BlockSpec((ncb, hp, G_DIM, G_DIM), lambda p, t: (t, p, 0, 0)),
         pl.BlockSpec((ncb, hp, CHUNK, CHUNK), lambda p, t: (t, p, 0, 0)), head],
        [_sds((T, G_W)), _sds((T, G_W)), _sds((T // CHUNK, G_HEADS, G_DIM, G_DIM)),
         _sds((T // CHUNK, G_HEADS, CHUNK, CHUNK)), _sds((T, G_W))],
        scratch=[pltpu.VMEM((hp, G_DIM, G_DIM), F32), pltpu.VMEM((hp, CHUNK, G_DIM), F32)], exchange=exchange,
    )(xp, xp, xp, xp, lb_logits, g_hn)


def _fwd_out(o_pad, o_hgrn, x, g_on, w_out, g_post, g_fpre, tm):
    T = x.shape[0]

    def body(o_ref, oh_ref, x_ref, gon_ref, w_ref, gpost_ref, gfpre_ref, h1_ref, y1_ref, z_ref, mix_ref):
        for h in range(N_HEADS):
            sl = slice(h * HB, (h + 1) * HB)
            n, _ = _rms(o_ref[:, sl].astype(F32), V_DIM)
            mix_ref[:, sl] = (n * gon_ref[:, sl]).astype(BF)
        mix_ref[:, N_HEADS * HB:MIX_W] = oh_ref[...].astype(BF)
        y1 = jnp.dot(mix_ref[...], w_ref[...], preferred_element_type=F32)
        y1_ref[...] = y1
        ny, _ = _rms(y1, D)
        h1 = x_ref[...] + ny * gpost_ref[...]
        h1_ref[...] = h1
        nh, _ = _rms(h1, D)
        z_ref[...] = (nh * gfpre_ref[...]).astype(BF)

    return _pcall(body, "fwd_out", (T // tm,),
                  [_rows(tm, N_HEADS * HB), _rows(tm, G_W), _rows(tm, D), _full((1, N_HEADS * HB)),
                   _full((MIX_W, D)), _full((1, D)), _full((1, D))],
                  [_rows(tm, D), _rows(tm, D), _rows(tm, D), _rows(tm, MIX_W)],
                  [_sds((T, D)), _sds((T, D)), _sds((T, D), BF), _sds((T, MIX_W), BF)],
                  )(o_pad, o_hgrn, x, g_on, w_out, g_post, g_fpre)


def _ffn_fwd(z, wgu, wd, h1, tgt, g_fpost, tm, nd):
    T = z.shape[0]
    fb = nd * FF_PAD
    nf = wd.shape[0] // fb

    def body(z_ref, wgu_ref, wd_ref, h1_ref, t_ref, gp_ref,
             as_ref, bs_ref, ff_ref, dh2_ref, dy2_ref, dgp_ref, loss_ref, acc):
        i, j = pl.program_id(0), pl.program_id(1)
        gu = _dot_nt(z_ref[...], wgu_ref[...])
        piece = lambda n: gu[:, n * FF_PAD:(n + 1) * FF_PAD]
        g = piece(0) if nd == 1 else jnp.concatenate([piece(2 * n) for n in range(nd)], 1)
        u = piece(1) if nd == 1 else jnp.concatenate([piece(2 * n + 1) for n in range(nd)], 1)
        s = _sigmoid(g)
        b = g * s
        ff = (b * u).astype(BF)
        as_ref[...] = (u * _dsilu(g, s)).astype(BF)
        bs_ref[...] = b.astype(BF)
        ff_ref[...] = ff
        part = jnp.dot(ff, wd_ref[...], preferred_element_type=F32)

        @pl.when(j == 0)
        def _():
            acc[...] = part

        @pl.when(j > 0)
        def _():
            acc[...] += part

        @pl.when((i == 0) & (j == 0))
        def _():
            dgp_ref[...] = jnp.zeros(dgp_ref.shape, F32)
            loss_ref[...] = jnp.zeros(loss_ref.shape, F32)

        @pl.when(j == nf - 1)
        def _():
            ny, r = _rms(acc[...], D)
            err = h1_ref[...] + ny * gp_ref[...] - t_ref[...]
            loss_ref[...] += 0.5 * jnp.sum(jnp.sum(err * err, -1, keepdims=True) * (1.0 / D), 0, keepdims=True)
            dh2 = err * (1.0 / D)
            dh2_ref[...] = dh2
            dy2, dgp = _rms_bwd(ny, r, gp_ref[...], dh2, D)
            dy2_ref[...] = dy2.astype(BF)
            dgp_ref[...] += dgp

    tok = lambda n: pl.BlockSpec((tm, n), lambda i, j: (i, 0))
    col = pl.BlockSpec((tm, fb), lambda i, j: (i, j))
    return _pcall(
        body, "ffn_fwd", (T // tm, nf),
        [tok(D), pl.BlockSpec((2 * fb, D), lambda i, j: (j, 0)), pl.BlockSpec((fb, D), lambda i, j: (j, 0)),
         tok(D), tok(D), _full((1, D))],
        [col, col, col, tok(D), tok(D), _full((1, D)), _full((1, HB))],
        [_sds((T, nf * fb), BF)] * 3 + [_sds((T, D)), _sds((T, D), BF), _sds((1, D)), _sds((1, HB))],
        scratch=[pltpu.VMEM((tm, D), F32)],
    )(z, wgu, wd, h1, tgt, g_fpost)


def _dsilu(x, s):
    return s * (1.0 + x * (1.0 - s))


def _ffn_bwd_x(dy2, gs, us, wgu, wd, h1, y1, dh2, g_fpre, g_post, tm):
    T = dy2.shape[0]
    nf = wd.shape[0] // _FB

    def body(dy2_ref, gs_ref, us_ref, wgu_ref, wd_ref, h1_ref, y1_ref, dh2_ref, gf_ref, gp_ref,
             dgu_ref, dh1_ref, dy1_ref, dgf_ref, dgp_ref, acc):
        i, j = pl.program_id(0), pl.program_id(1)
        dff = _dot_nt(dy2_ref[...], wd_ref[...])
        dg = (dff * gs_ref[...].astype(F32)).astype(BF)
        du = (dff * us_ref[...].astype(F32)).astype(BF)
        dgu = jnp.concatenate([dg[:, 0:FF_PAD], du[:, 0:FF_PAD], dg[:, FF_PAD:_FB], du[:, FF_PAD:_FB]], 1)
        dgu_ref[...] = dgu
        part = jnp.dot(dgu, wgu_ref[...], preferred_element_type=F32)

        @pl.when(j == 0)
        def _():
            acc[...] = part

        @pl.when(j > 0)
        def _():
            acc[...] += part

        @pl.when((i == 0) & (j == 0))
        def _():
            dgf_ref[...] = jnp.zeros(dgf_ref.shape, F32)
            dgp_ref[...] = jnp.zeros(dgp_ref.shape, F32)

        @pl.when(j == nf - 1)
        def _():
            nh, rh = _rms(h1_ref[...], D)
            dh, dgf = _rms_bwd(nh, rh, gf_ref[...], acc[...], D)
            dh1 = dh2_ref[...] + dh
            dh1_ref[...] = dh1
            dgf_ref[...] += dgf
            ny, ry = _rms(y1_ref[...], D)
            dy1, dgp = _rms_bwd(ny, ry, gp_ref[...], dh1, D)
            dy1_ref[...] = dy1.astype(BF)
            dgp_ref[...] += dgp

    tok = lambda n: pl.BlockSpec((tm, n), lambda i, j: (i, 0))
    col = pl.BlockSpec((tm, _FB), lambda i, j: (i, j))
    return _pcall(
        body, "ffn_bwd_x", (T // tm, nf),
        [tok(D), col, col, pl.BlockSpec((2 * _FB, D), lambda i, j: (j, 0)), pl.BlockSpec((_FB, D), lambda i, j: (j, 0)),
         tok(D), tok(D), tok(D), _full((1, D)), _full((1, D))],
        [pl.BlockSpec((tm, 2 * _FB), lambda i, j: (i, j)), tok(D), tok(D), _full((1, D)), _full((1, D))],
        [_sds((T, 2 * nf * _FB), BF), _sds((T, D)), _sds((T, D), BF), _sds((1, D)), _sds((1, D))],
        scratch=[pltpu.VMEM((tm, D), F32)],
    )(dy2, gs, us, wgu, wd, h1, y1, dh2, g_fpre, g_post)


def _ffn_bwd_w(z, ffs, dgu, dy2, tm):
    T = z.shape[0]
    nf = ffs.shape[1] // _FB
    nt = T // tm

    def body(z_ref, ff_ref, dgu_ref, dy2_ref, dwgu_ref, dwd_ref, agu, ad):
        i = pl.program_id(1)
        pgu = _dot_tn(dgu_ref[...], z_ref[...])
        pd = _dot_tn(ff_ref[...], dy2_ref[...])

        @pl.when(i == 0)
        def _():
            agu[...] = pgu
            ad[...] = pd

        @pl.when(i > 0)
        def _():
            agu[...] += pgu
            ad[...] += pd

        @pl.when(i == nt - 1)
        def _():
            dwgu_ref[...] = agu[...].astype(BF)
            dwd_ref[...] = ad[...].astype(BF)

    F = nf * _FB
    tok = lambda n: pl.BlockSpec((tm, n), lambda j, i: (i, 0))
    return _pcall(
        body, "ffn_bwd_w", (nf, nt),
        [tok(D), pl.BlockSpec((tm, _FB), lambda j, i: (i, j)), pl.BlockSpec((tm, 2 * _FB), lambda j, i: (i, j)), tok(D)],
        [pl.BlockSpec((2 * _FB, D), lambda j, i: (j, 0)), pl.BlockSpec((_FB, D), lambda j, i: (j, 0))],
        [_sds((2 * F, D), BF), _sds((F, D), BF)],
        scratch=[pltpu.VMEM((2 * _FB, D), F32), pltpu.VMEM((_FB, D), F32)],
    )(z, ffs, dgu, dy2)


def _out_bwd(dy1, mix, o_pad, w_out, g_on, tm):
    T = dy1.shape[0]
    W = N_HEADS * HB

    def body(dy1_ref, mix_ref, o_ref, w_ref, gon_ref, do_ref, dl_ref, dohg_ref, dw_ref, dgon_ref):
        i = pl.program_id(0)
        dy1v = dy1_ref[...]
        dmix = _dot_nt(dy1v, w_ref[...])
        pw = _dot_tn(mix_ref[...], dy1v)

        @pl.when(i == 0)
        def _():
            dw_ref[...] = pw
            dgon_ref[...] = jnp.zeros(dgon_ref.shape, F32)

        @pl.when(i > 0)
        def _():
            dw_ref[...] += pw

        for h in range(N_HEADS):
            sl = slice(h * HB, (h + 1) * HB)
            ov = o_ref[:, sl].astype(F32)
            n, r = _rms(ov, V_DIM)
            do, dg = _rms_bwd(n, r, gon_ref[:, sl], dmix[:, sl], V_DIM)
            dgon_ref[:, sl] += dg
            do_ref[:, sl] = do.astype(BF)
            dl_ref[:, sl] = jnp.broadcast_to(jnp.sum(do * ov, -1, keepdims=True), (tm, HB))
        dohg_ref[...] = dmix[:, W:MIX_W]

    return _pcall(body, "out_bwd", (T // tm,),
                  [_rows(tm, D), _rows(tm, MIX_W), _rows(tm, W), _full((MIX_W, D)), _full((1, W))],
                  [_rows(tm, W), _rows(tm, W), _rows(tm, G_W), _full((MIX_W, D)), _full((1, W))],
                  [_sds((T, W), BF), _sds((T, W)), _sds((T, G_W)), _sds((MIX_W, D)), _sds((1, W))],
                  )(dy1, mix, o_pad, w_out, g_on)


def _flash_bwd(q, k, v, do, lse, dl, tq, exchange=None):
    T = q.shape[0]
    nq = T // tq
    scale = QK_DIM ** -0.5
    hp = _AH
    W = hp * HB

    def body(k_ref, v_ref, q_ref, do_ref, lse_ref, dl_ref, dk_ref, dv_ref, dq_ref):
        j = pl.program_id(1)

        @pl.when(j == 0)
        def _():
            dq_ref[...] = jnp.zeros(dq_ref.shape, F32)

        def blk(i, carry, masked):
            sl = pl.ds(pl.multiple_of(i * tq, tq), tq)
            out = []
            for h in range(hp):
                ls = slice(h * HB, (h + 1) * HB)
                dk, dv = carry[h]
                kv, vv = k_ref[:, ls], v_ref[:, ls]
                qv, dov = q_ref[sl, ls], do_ref[sl, ls]
                s = _dot_nt(qv, kv) * scale
                if masked:
                    r = lax.broadcasted_iota(jnp.int32, (tq, tq), 0)
                    c = lax.broadcasted_iota(jnp.int32, (tq, tq), 1)
                    s = jnp.where(c <= r, s, NEG)
                p = jnp.exp(s - lse_ref[sl, h * HB:h * HB + 1])
                ds = p * (_dot_nt(dov, vv) - dl_ref[sl, h * HB:h * HB + 1]) * scale
                dq_ref[sl, ls] += _dot(ds, kv)
                out.append((dk + _dot_tn(ds, qv), dv + _dot_tn(p, dov)))
            return tuple(out)

        zero = jnp.zeros((tq, HB), F32)
        carry = blk(j, tuple((zero, zero) for _ in range(hp)), True)
        res = lax.fori_loop(j + 1, nq, lambda i, cr: blk(i, cr, False), carry)
        for h in range(hp):
            ls = slice(h * HB, (h + 1) * HB)
            dk_ref[:, ls] = res[h][0].astype(BF)
            dv_ref[:, ls] = res[h][1].astype(BF)

    tile = pl.BlockSpec((tq, W), lambda h, j: (j, h))
    whole = pl.BlockSpec((T, W), lambda h, j: (0, h))
    return _pcall(body, "flash_bwd", (N_HEADS // hp, nq), [tile, tile, whole, whole, whole, whole],
                  [tile, tile, whole], [_sds((T, N_HEADS * HB), BF)] * 2 + [_sds((T, N_HEADS * HB))],
                  exchange=exchange)(k, v, q, do, lse, dl)


def _mla_prep_bwd(xp, tabs, dq, dk, dv, g_q, g_kv, w_uq, w_uk, w_uv, tm):
    T = xp.shape[0]
    W = N_HEADS * HB

    def body(xp_ref, ta_ref, tb1_ref, tb2_ref, dq_ref, dk_ref, dv_ref, gq_ref, gkv_ref, wuq_ref, wuk_ref, wuv_ref,
             dxp_ref, dwuq_ref, dwuk_ref, dwuv_ref, dgq_ref, dgkv_ref, dqp):
        i = pl.program_id(0)
        ta, tb1, tb2 = ta_ref[...], tb1_ref[...], tb2_ref[...]
        nq, rq = _rms(xp_ref[:, 0:Q_RANK], Q_RANK)
        nkv, rkv = _rms(xp_ref[:, Q_RANK:Q_RANK + KV_RANK], KV_RANK)
        dkr = jnp.zeros((tm, HB), F32)
        for h in range(N_HEADS):
            sl = slice(h * HB, (h + 1) * HB)
            dqp[:, sl] = _unrope(dq_ref[:, sl], ta, tb1, tb2).astype(BF)
            dkr = dkr + dk_ref[:, sl].astype(F32)
        dkr = pltpu.roll(_unrope(dkr, ta, tb1, tb2), HB - NOPE, 1)
        lane = lax.broadcasted_iota(jnp.int32, (tm, HB), 1)
        dxp_ref[:, Q_RANK + KV_RANK:MLA_IN] = jnp.where(lane < ROPE, dkr, 0.0)
        dqpv = dqp[...]
        dkv, dvv = dk_ref[...], dv_ref[...]
        nqs = (nq * gq_ref[...]).astype(BF)
        nkvs = (nkv * gkv_ref[...]).astype(BF)
        pq, pk, pv = _dot_tn(nqs, dqpv), _dot_tn(nkvs, dkv), _dot_tn(nkvs, dvv)
        dcq, dgq = _rms_bwd(nq, rq, gq_ref[...], _dot_nt(dqpv, wuq_ref[...]), Q_RANK)
        dckv, dgkv = _rms_bwd(nkv, rkv, gkv_ref[...], _dot_nt(dkv, wuk_ref[...]) + _dot_nt(dvv, wuv_ref[...]), KV_RANK)
        dxp_ref[:, 0:Q_RANK] = dcq
        dxp_ref[:, Q_RANK:Q_RANK + KV_RANK] = dckv

        @pl.when(i == 0)
        def _():
            dwuq_ref[...] = pq
            dwuk_ref[...] = pk
            dwuv_ref[...] = pv
            dgq_ref[...] = dgq
            dgkv_ref[...] = dgkv

        @pl.when(i > 0)
        def _():
            dwuq_ref[...] += pq
            dwuk_ref[...] += pk
            dwuv_ref[...] += pv
            dgq_ref[...] += dgq
            dgkv_ref[...] += dgkv

    tab = _rows(tm, HB)
    return _pcall(
        body, "mla_prep_bwd", (T // tm,),
        [_rows(tm, MLA_IN), tab, tab, tab, _rows(tm, W), _rows(tm, W), _rows(tm, W), _full((1, Q_RANK)),
         _full((1, KV_RANK)), _full((Q_RANK, W)), _full((KV_RANK, W)), _full((KV_RANK, W))],
        [_rows(tm, MLA_IN), _full((Q_RANK, W)), _full((KV_RANK, W)), _full((KV_RANK, W)), _full((1, Q_RANK)),
         _full((1, KV_RANK))],
        [_sds((T, MLA_IN)), _sds((Q_RANK, W)), _sds((KV_RANK, W)), _sds((KV_RANK, W)), _sds((1, Q_RANK)),
         _sds((1, KV_RANK))],
        scratch=[pltpu.VMEM((tm, W), BF)],
    )(xp, *tabs, dq, dk, dv, g_q, g_kv, w_uq, w_uk, w_uv)


def _hgrn_bwd(xp, o_raw, s_all, a_all, b_all, d_out, lb_logits, g_hn, exchange=None):
    T = xp.shape[0]
    tb = min(_TB_BWD, T)
    ncb = tb // CHUNK
    nb = T // tb
    hp = _HP
    W = hp * G_DIM

    def body(hq_ref, hf_ref, hi_ref, hg_ref, o_ref, sall_ref, aall_ref, ball_ref, dout_ref, lbl_ref, ghn_ref,
             dhq_ref, dhf_ref, dhi_ref, dhg_ref, dlbl_ref, dghn_ref, dst_ref, b_s, acc_lb, acc_g):
        t = pl.program_id(1)
        lb_all = _lower_bound(lbl_ref)

        @pl.when(t == 0)
        def _():
            dst_ref[...] = jnp.zeros(dst_ref.shape, F32)
            acc_lb[...] = jnp.zeros(acc_lb.shape, F32)
            acc_g[...] = jnp.zeros(acc_g.shape, F32)

        row = lax.broadcasted_iota(jnp.int32, (CHUNK, CHUNK), 0)
        col = lax.broadcasted_iota(jnp.int32, (CHUNK, CHUNK), 1)
        tri_t = (col >= row).astype(BF)
        codes = _intra_codes(SUB)
        last = lax.broadcasted_iota(jnp.int32, (CHUNK, G_DIM), 0) == CHUNK - 1

        def chunk(cc, carry):
            c = ncb - 1 - cc
            sl = pl.ds(pl.multiple_of(c * CHUNK, CHUNK), CHUNK)
            for h in range(hp):
                ls = slice(h * G_DIM, (h + 1) * G_DIM)
                lb, ghn = lb_all[:, ls], ghn_ref[:, ls]
                hq, hg = hq_ref[sl, ls], hg_ref[sl, ls]
                q, k, f, _, sig, sq = _gates(hq, hf_ref[sl, ls], lb)
                v = hi_ref[sl, ls]
                b2 = ball_ref[sl, ls]
                b_s[h] = b2
                st = sall_ref[c, h]
                dstn = dst_ref[h]
                o = o_ref[sl, ls]
                dout = dout_ref[sl, ls]
                n, r = _rms(o, G_DIM)
                sg = _sigmoid(hg)
                dhg_ref[sl, ls] = dout * (n * ghn) * _dsilu(hg, sg)
                do, dg = _rms_bwd(n, r, ghn, dout * (hg * sg), G_DIM)
                acc_g[:, ls] += dg
                eb = jnp.exp2(b2)
                bl = b_s[h, CHUNK - 1:CHUNK, :]
                ebl = jnp.exp2(bl)
                ekd = jnp.exp2(bl - b2)
                kd = k * ekd
                a = aall_ref[c, h]
                dq_i, dk_i = _intra(q, k, b2, b_s.at[h], codes, _dot_nt(do, v))
                dhi_ref[sl, ls] = _dot_tn(a, do) + _dot_nt(kd, dstn)
                dk_state = _dot(v, dstn) * ekd
                dq = dq_i + _dot(do, st) * eb
                dk = dk_i + dk_state
                dbl = jnp.sum(k * dk_state, 0, keepdims=True) + ebl * jnp.sum(dstn * st, 0, keepdims=True)
                db = q * dq - k * dk + jnp.where(last, dbl, 0.0)
                df = _tri_mm(tri_t, db) / f - dk
                dhf_ref[sl, ls] = df * (1.0 - lb) * sig * (1.0 - sig)
                acc_lb[:, ls] += jnp.sum(df * (1.0 - sig), 0, keepdims=True)
                dhq_ref[sl, ls] = dq * _dsilu(hq, sq)
                dst_ref[h] = dstn * ebl + _dot_tn(do, q * eb)
            return carry

        lax.fori_loop(0, ncb, chunk, 0, unroll=4)

        @pl.when(t == nb - 1)
        def _():
            dl0 = acc_lb[...] * lb_all * (1.0 - lb_all)
            dlbl_ref[0:1, :] = dl0
            dlbl_ref[1:2, :] = -dl0
            dghn_ref[...] = acc_g[...]

    col_blk = lambda j: pl.BlockSpec((tb, W), lambda p, t: (nb - 1 - t, j * (G_HEADS // hp) + p))
    head = pl.BlockSpec((tb, W), lambda p, t: (nb - 1 - t, p))
    two = pl.BlockSpec((2, W), lambda p, t: (0, p))
    one = pl.BlockSpec((1, W), lambda p, t: (0, p))
    res = _pcall(
        body, "hgrn_bwd", (G_HEADS // hp, nb),
        [col_blk(0), col_blk(1), col_blk(2), col_blk(3), head,
         pl.BlockSpec((ncb, hp, G_DIM, G_DIM), lambda p, t: (nb - 1 - t, p, 0, 0)),
         pl.BlockSpec((ncb, hp, CHUNK, CHUNK), lambda p, t: (nb - 1 - t, p, 0, 0)), head, head, two, one],
        [head, head, head, head, two, one],
        [_sds((T, G_W))] * 4 + [_sds((2, G_W)), _sds((1, G_W))],
        scratch=[pltpu.VMEM((hp, G_DIM, G_DIM), F32), pltpu.VMEM((hp, CHUNK, G_DIM), F32),
                 pltpu.VMEM((1, W), F32), pltpu.VMEM((1, W), F32)], exchange=exchange,
    )(xp, xp, xp, xp, o_raw, s_all, a_all, b_all, d_out, lb_logits, g_hn)
    return res


def _in_bwd_x(x, dxp_m, dxp_h, dh1, w_in_al, g_pre, vecs, tm, exchange=None):
    T = x.shape[0]
    nt = T // tm
    n_v = len(vecs)
    offs, rows = _row_offsets([g_pre] + list(vecs))

    def body(*refs):
        x_ref, dm_ref, d0_ref, d1_ref, d2_ref, d3_ref, dh1_ref, w_ref, g_ref = refs[:9]
        v_refs = refs[9:9 + n_v]
        dx_ref, dg_ref, rall, pk, send_sems, recv_sems, loc_sem = refs[9 + n_v:]
        i = pl.program_id(0)
        du = _dot_nt(dm_ref[...], w_ref[:, 0:MLA_IN])
        for j, d_ref in enumerate((d0_ref, d1_ref, d2_ref, d3_ref)):
            du = du + _dot_nt(d_ref[...], w_ref[:, MLA_IN + j * G_W:MLA_IN + (j + 1) * G_W])
        nx, r = _rms(x_ref[...], D)
        dx, dg = _rms_bwd(nx, r, g_ref[...], du, D)
        dx_ref[...] = dh1_ref[...] + dx

        @pl.when(i == 0)
        def _():
            dg_ref[...] = dg

        @pl.when(i > 0)
        def _():
            dg_ref[...] += dg

        @pl.when(i == nt - 1)
        def _():
            mx, my, mc = lax.axis_index("x"), lax.axis_index("y"), lax.axis_index("c")
            me = 4 * mx + 2 * my + mc
            pk[...] = jnp.zeros(pk.shape, F32)
            for p, v_ref in enumerate((dg_ref,) + tuple(v_refs)):
                vr, vn = v_ref.shape
                pk[offs[p]:offs[p] + vr, 0:vn] = v_ref[...]

            def copy(k, outgoing):
                px, py, pc = _peer(k, mx, my, mc)
                return pltpu.make_async_remote_copy(
                    src_ref=pk, dst_ref=rall.at[me if outgoing else 4 * px + 2 * py + pc],
                    send_sem=send_sems.at[k - 1], recv_sem=recv_sems.at[k - 1],
                    device_id=(px, py, pc), device_id_type=MESH)

            local = pltpu.make_async_copy(pk, rall.at[me], loc_sem)
            local.start()
            for k in range(1, N_DEV):
                copy(k, True).start()
            local.wait()
            for k in range(1, N_DEV):
                copy(k, False).wait_recv()
            for k in range(1, N_DEV):
                copy(k, True).wait_send()

    return _pcall(body, "in_bwd_x", (nt,),
                  [_rows(tm, D), _rows(tm, MLA_IN)] + [_rows(tm, G_W)] * 4 + [_rows(tm, D), _full((D, XP_W)), _full((1, D))]
                  + [_full(v.shape) for v in vecs],
                  [_rows(tm, D), _full((1, D)), _HBM], [_sds((T, D)), _sds((1, D)), _sds((N_DEV, rows, D))],
                  scratch=[pltpu.VMEM((rows, D), F32), pltpu.SemaphoreType.DMA((N_DEV - 1,)),
                           pltpu.SemaphoreType.DMA((N_DEV - 1,)), pltpu.SemaphoreType.DMA],
                  exchange=exchange)(x, dxp_m, *dxp_h, dh1, w_in_al, g_pre, *vecs)


def _aligned_col(c):
    return jnp.where(c < Q_RANK + KV_RANK + ROPE, c, c + (KR_PAD - ROPE))


def _align_w_in(g_in):
    tile = 384
    kr_end = Q_RANK + KV_RANK + ROPE

    def body(g_ref, o_ref, gp):
        gp[...] = jnp.zeros(gp.shape, BF)
        for j in range(N_DEV):
            gp[j, :, 0:IN_SH] = g_ref[j]
        r = lax.broadcasted_iota(jnp.int32, (tile, tile), 0)
        c = lax.broadcasted_iota(jnp.int32, (tile, tile), 1)
        for t in range(XP_W // tile):
            lo, hi = t * tile, (t + 1) * tile
            cols = [a if a < kr_end else a - (KR_PAD - ROPE) for a in (lo, hi - 1)]
            acc = jnp.zeros((D, tile), F32)
            for j in range(cols[0] // IN_SH, cols[-1] // IN_SH + 1):
                sel = (r < IN_SH) & (_aligned_col(j * IN_SH + r) == lo + c)
                acc = acc + jnp.dot(gp[j], sel.astype(BF), preferred_element_type=F32)
            o_ref[:, lo:hi] = acc.astype(BF)

    vm = pl.BlockSpec(memory_space=pltpu.VMEM)
    return pl.pallas_call(
        body, name="align_w_in", in_specs=[vm], out_specs=vm, out_shape=_sds((D, XP_W), BF),
        scratch_shapes=[pltpu.VMEM((N_DEV, D, tile), BF)],
        compiler_params=pltpu.CompilerParams(vmem_limit_bytes=_VMEM_LIMIT))(g_in)


def _in_bwd_w(name, u, dxp_m, dxp_h, tm, row0, nr, exchange=None):
    T = u.shape[0]
    nt = T // tm
    win = 640

    def body(u_ref, dm_ref, d0_ref, d1_ref, d2_ref, d3_ref, o_ref, acc):
        i = pl.program_id(0)
        ut = u_ref[:, row0:row0 + nr].T
        parts = [(0, MLA_IN, dm_ref)] + [(MLA_IN + j * G_W, G_W, d) for j, d in enumerate((d0_ref, d1_ref, d2_ref, d3_ref))]

        @pl.when(i == 0)
        def _():
            for lo, n, d in parts:
                acc[:, lo:lo + n] = jnp.dot(ut, d[...].astype(BF), preferred_element_type=F32)

        @pl.when(i > 0)
        def _():
            for lo, n, d in parts:
                acc[:, lo:lo + n] += jnp.dot(ut, d[...].astype(BF), preferred_element_type=F32)

        @pl.when(i == nt - 1)
        def _():
            wide = 384
            r = lax.broadcasted_iota(jnp.int32, (win, wide), 0)
            c = lax.broadcasted_iota(jnp.int32, (win, wide), 1)
            kr_end = Q_RANK + KV_RANK + ROPE
            for j in range(N_DEV):
                first = j * IN_SH if j * IN_SH < kr_end else j * IN_SH + (KR_PAD - ROPE)
                lo = min(first // HB * HB, XP_W - win)
                sel = (c < IN_SH) & (_aligned_col(j * IN_SH + c) == lo + r)
                res = jnp.dot(acc[:, lo:lo + win].astype(BF), sel.astype(BF), preferred_element_type=F32)
                o_ref[j] = res[:, 0:IN_SH].astype(BF)

    return _pcall(body, name, (nt,),
                  [_rows(tm, D), _rows(tm, MLA_IN)] + [_rows(tm, G_W)] * 4,
                  [_full((N_DEV, nr, IN_SH))], [_sds((N_DEV, nr, IN_SH), BF)],
                  scratch=[pltpu.VMEM((nr, XP_W), F32)], exchange=exchange)(u, dxp_m, *dxp_h)


def _pad_heads(w, width, real):
    lead = w.shape[:-1]
    w = w.reshape(lead + (N_HEADS, real))
    w = jnp.pad(w, [(0, 0)] * len(lead) + [(0, 0), (0, width - real)])
    return w.reshape(lead + (N_HEADS * width,))


def _unpad_heads(w, width, real):
    lead = w.shape[:-1]
    return w.reshape(lead + (N_HEADS, width))[..., :real].reshape(lead + (N_HEADS * real,))


def _rope_tables(positions):
    half = ROPE // 2
    inv_freq = 1.0 / (ROPE_THETA ** (jnp.arange(0, ROPE, 2, dtype=F32) / ROPE))
    ang = positions.astype(F32)[:, None] * inv_freq
    cos, sin = jnp.cos(ang), jnp.sin(ang)
    T = positions.shape[0]
    z = lambda n: jnp.zeros((T, n), F32)
    ta = jnp.concatenate([jnp.ones((T, NOPE), F32), cos, cos, z(HB - QK_DIM)], 1)
    tb1 = jnp.concatenate([z(NOPE), -sin, z(half), z(HB - QK_DIM)], 1)
    tb2 = jnp.concatenate([z(NOPE), z(half), sin, z(HB - QK_DIM)], 1)
    return ta, tb1, tb2


def kernel(x, positions, attn_pre_norm, w_in, mla_q_norm, mla_w_uq, mla_kv_norm, mla_w_ukv, mla_out_norm, hgrn_lb_logits, hgrn_out_norm, w_out, attn_post_norm, ffn_pre_norm, w_gate, w_up, w_down, ffn_post_norm, loss_target, m_attn_pre_norm, m_w_in, m_mla_q_norm, m_mla_w_uq, m_mla_kv_norm, m_mla_w_ukv, m_mla_out_norm, m_hgrn_lb_logits, m_hgrn_out_norm, m_w_out, m_attn_post_norm, m_ffn_pre_norm, m_w_gate, m_w_up, m_w_down, m_ffn_post_norm, v_attn_pre_norm, v_w_in, v_mla_q_norm, v_mla_w_uq, v_mla_kv_norm, v_mla_w_ukv, v_mla_out_norm, v_hgrn_lb_logits, v_hgrn_out_norm, v_w_out, v_attn_post_norm, v_ffn_pre_norm, v_w_gate, v_w_up, v_w_down, v_ffn_post_norm):
    T = x.shape[1]
    tm = min(_TM, T)
    tq = min(_TQ, T)
    xs, tgt = x[0], loss_target[0]
    uq_sh = (Q_RANK // N_DEV, N_HEADS * QK_DIM)

    b_in, b_uq, b_out, b_gu, b_d = _cast_shards(
        w_in[0], mla_w_uq[0].reshape(uq_sh), w_out[0], w_gate[0].T, w_up[0].T, w_down[0])
    g_in, g_uq = _gather_two_level("ag_first", [b_in, b_uq])
    w_in_al = _align_w_in(g_in)
    w_uq_p = _pad_heads(g_uq.reshape(Q_RANK, N_HEADS * QK_DIM), HB, QK_DIM)
    w_ukv = mla_w_ukv[0].astype(BF)
    w_uk_p = _pad_heads(w_ukv[..., :NOPE].reshape(KV_RANK, N_HEADS * NOPE), HB, NOPE)
    w_uv_p = _pad_heads(w_ukv[..., NOPE:].reshape(KV_RANK, N_HEADS * V_DIM), HB, V_DIM)
    g_on_p = _pad_heads(mla_out_norm, HB, V_DIM)
    tabs = _rope_tables(positions[0])

    xp_m, xp_h, u = _fwd_in(xs, attn_pre_norm, w_in_al, tm)
    q_att, qs_att, k_att, v_att = _mla_prep(xp_m, tabs, mla_q_norm, mla_kv_norm, w_uq_p, w_uk_p, w_uv_p, tm)
    o_hgrn, o_raw, s_all, a_all, b_all, wd = _hgrn_fwd(xp_h, hgrn_lb_logits, hgrn_out_norm, ([GATHER], [b_d]))
    wd = wd.reshape(N_DEV * FF_PAD, D)
    o_pad, lse, wgu, g_out = _flash_fwd(qs_att, k_att, v_att, tq, ([GATHER, GATHER], [b_gu, b_out]))
    wgu = wgu.reshape(N_DEV * 2 * FF_PAD, D)
    w_out_full = g_out.reshape(D, D)
    w_out_mla = jnp.pad(w_out_full[:N_HEADS * V_DIM].reshape(N_HEADS, V_DIM, D), ((0, 0), (0, HB - V_DIM), (0, 0)))
    w_out_p = jnp.concatenate([w_out_mla.reshape(N_HEADS * HB, D), w_out_full[N_HEADS * V_DIM:]], 0)
    h1, y1, z, mix = _fwd_out(o_pad, o_hgrn, xs, g_on_p, w_out_p, attn_post_norm, ffn_pre_norm, tm)
    tmf = min(_TMF, T)
    gs, us, ffs, dh2, dy2, d_fpost, loss_row = _ffn_fwd(z, wgu, wd, h1, tgt, ffn_post_norm, tm, _FB // FF_PAD)

    dgu, dh1, dy1, d_fpre, d_post = _ffn_bwd_x(dy2, gs, us, wgu, wd, h1, y1, dh2, ffn_pre_norm, attn_post_norm, tm)
    dwgu, dwd = _ffn_bwd_w(z, ffs, dgu, dy2, tmf)
    do_pad, dl, d_ohg, dw_out_p, d_on_p = _out_bwd(dy1, mix, o_pad, w_out_p, g_on_p, tm)
    dw_out_mla = dw_out_p[:N_HEADS * HB].reshape(N_HEADS, HB, D)[:, :V_DIM].reshape(N_HEADS * V_DIM, D)
    dw_out = jnp.concatenate([dw_out_mla, dw_out_p[N_HEADS * HB:]], 0).reshape(N_DEV, D // N_DEV, D).astype(BF)
    dk_att, dv_att, dq_att, p_gu, p_d, p_out = _flash_bwd(
        q_att, k_att, v_att, do_pad, lse, dl, tq,
        ([SCATTER] * 3, [dwgu.reshape(N_DEV, 2 * FF_PAD, D), dwd.reshape(N_DEV, FF_PAD, D), dw_out]))
    dxp_m, dw_uq_p, dw_uk_p, dw_uv_p, d_gq, d_gkv = _mla_prep_bwd(
        xp_m, tabs, dq_att, dk_att, dv_att, mla_q_norm, mla_kv_norm, w_uq_p, w_uk_p, w_uv_p, tm)
    dw_uq = _unpad_heads(dw_uq_p, HB, QK_DIM).reshape((N_DEV,) + uq_sh).astype(BF)
    dw_ukv = jnp.concatenate([_unpad_heads(dw_uk_p, HB, NOPE).reshape(KV_RANK, N_HEADS, NOPE),
                              _unpad_heads(dw_uv_p, HB, V_DIM).reshape(KV_RANK, N_HEADS, V_DIM)], -1)
    *dxp_h, d_lbl, d_ghn, p_uq, dw_ukv_all = _hgrn_bwd(
        xp_h, o_raw, s_all, a_all, b_all, d_ohg, hgrn_lb_logits, hgrn_out_norm,
        ([SCATTER, GATHER], [dw_uq, dw_ukv.reshape(KV_RANK, N_HEADS * HB)]))
    dw_in_a, = _in_bwd_w("in_bwd_w_a", u, dxp_m, dxp_h, tm, 0, _IN_ROWS_A)
    dw_in_b, p_in_a = _in_bwd_w("in_bwd_w_b", u, dxp_m, dxp_h, tm, _IN_ROWS_A, D - _IN_ROWS_A, ([SCATTER], [dw_in_a]))
    d_on = _unpad_heads(d_on_p, HB, V_DIM)
    vecs = [d_gq, d_gkv, d_on, d_lbl, d_ghn, d_post, d_fpre, d_fpost, loss_row]
    grad_x, _, rall, p_in_b = _in_bwd_x(xs, dxp_m, dxp_h, dh1, w_in_al, attn_pre_norm, vecs, tm, ([SCATTER], [dw_in_b]))

    ukv2 = lambda a: a.reshape(KV_RANK, N_HEADS * HB)
    small_w = [attn_pre_norm, mla_q_norm, mla_kv_norm, ukv2(mla_w_ukv), mla_out_norm, hgrn_lb_logits, hgrn_out_norm,
               attn_post_norm, ffn_pre_norm, ffn_post_norm]
    small_m = [m_attn_pre_norm, m_mla_q_norm, m_mla_kv_norm, ukv2(m_mla_w_ukv), m_mla_out_norm, m_hgrn_lb_logits,
               m_hgrn_out_norm, m_attn_post_norm, m_ffn_pre_norm, m_ffn_post_norm]
    small_v = [v_attn_pre_norm, v_mla_q_norm, v_mla_kv_norm, ukv2(v_mla_w_ukv), v_mla_out_norm, v_hgrn_lb_logits,
               v_hgrn_out_norm, v_attn_post_norm, v_ffn_pre_norm, v_ffn_post_norm]
    s_g, s_d, s_m, s_v, loss_all = _small_adam(rall, dw_ukv_all, 3, small_w, small_m, small_v)
    r_in = _shard_adam("adam_w_in", [p_in_a, p_in_b], w_in[0], m_w_in[0], v_w_in[0], 128)
    r_uq = _shard_adam("adam_w_uq", [p_uq], mla_w_uq[0].reshape(uq_sh), m_mla_w_uq[0].reshape(uq_sh),
                       v_mla_w_uq[0].reshape(uq_sh), uq_sh[0])
    r_out = _shard_adam("adam_w_out", [p_out], w_out[0], m_w_out[0], v_w_out[0], D // N_DEV)
    r_g, r_u = _gate_up_adam(p_gu, (w_gate[0].T, w_up[0].T), (m_w_gate[0].T, m_w_up[0].T),
                             (v_w_gate[0].T, v_w_up[0].T))
    r_g, r_u = [a.T for a in r_g], [a.T for a in r_u]
    r_d = _shard_adam("adam_w_down", [p_d], w_down[0], m_w_down[0], v_w_down[0], FF_SH // 2)

    loss = loss_all[0, 0]

    def assemble(big, small):
        b_in, b_uq, b_out, b_g, b_u, b_d = big
        return [small[0], b_in[None], small[1], b_uq.reshape(mla_w_uq.shape), small[2],
                small[3].reshape(mla_w_ukv.shape), small[4], small[5], small[6], b_out[None], small[7], small[8],
                b_g[None], b_u[None], b_d[None], small[9]]

    outs = [loss, grad_x[None]]
    for idx, small in enumerate((s_g, s_d, s_m, s_v)):
        outs += assemble([r[idx] for r in (r_in, r_uq, r_out, r_g, r_u, r_d)], small)
    return tuple(outs)
```

```python
import jax
import jax.numpy as jnp
from jax import lax
from jax.experimental import pallas as pl
from jax.experimental.pallas import tpu as pltpu

BF = jnp.bfloat16
F32 = jnp.float32
MESH = pl.DeviceIdType.MESH

N_DEV = 8
D = 1024
EPS = 1e-6
LOG2E = 1.4426950408889634
ROPE_THETA = 10000.0
N_HEADS = 8
HB = 128
NOPE = 64
ROPE = 32
V_DIM = 64
QK_DIM = NOPE + ROPE
Q_RANK = 384
KV_RANK = 128
KR_PAD = 128
MLA_IN = Q_RANK + KV_RANK + KR_PAD
G_HEADS = 4
G_DIM = 128
G_W = G_HEADS * G_DIM
CHUNK = 64
SUB = 16
XP_W = MLA_IN + 4 * G_W
IN_SH = 324
FF_SH = 352
FF_PAD = 384
MIX_W = N_HEADS * HB + G_W

ADAM_LR = 0.001
ADAM_B1 = 0.9
ADAM_B2 = 0.999
ADAM_EPS = 1e-08
ADAM_WD = 0.01
ADAM_STEP = 10

_TM = 512
_TMF = 1024
_IN_ROWS_A = 384
_TQ = 512
_AH = 2
_AH_FWD = 4
_FB = 768
_TB = 1024
_TB_BWD = 512
_HP = 4
V7X_VMEM_BYTES = 64 * 1024 * 1024
_VMEM_LIMIT = V7X_VMEM_BYTES - 8 * 1024 * 1024
NEG = -1e30


def _dot(a, b):
    return jnp.dot(a.astype(BF), b.astype(BF), preferred_element_type=F32)


def _dot_nt(a, b):
    return lax.dot_general(a.astype(BF), b.astype(BF), (((1,), (1,)), ((), ())), preferred_element_type=F32)


def _dot_tn(a, b):
    return lax.dot_general(a.astype(BF), b.astype(BF), (((0,), (0,)), ((), ())), preferred_element_type=F32)


def _sigmoid(x):
    return 1.0 / (1.0 + jnp.exp(-x))


def _rms(x, n):
    r = lax.rsqrt(jnp.sum(x * x, -1, keepdims=True) * (1.0 / n) + EPS)
    return x * r, r


def _rms_bwd(nx, r, g, dy, n):
    dg = jnp.sum(dy * nx, 0, keepdims=True)
    dn = dy * g
    dx = r * (dn - nx * (jnp.sum(dn * nx, -1, keepdims=True) * (1.0 / n)))
    return dx, dg


def _adamw(w, g, m, v):
    m2 = ADAM_B1 * m + (1.0 - ADAM_B1) * g
    v2 = ADAM_B2 * v + (1.0 - ADAM_B2) * (g * g)
    m_hat = m2 / (1.0 - ADAM_B1 ** ADAM_STEP)
    v_hat = v2 / (1.0 - ADAM_B2 ** ADAM_STEP)
    delta = -ADAM_LR * (m_hat / (jnp.sqrt(v_hat) + ADAM_EPS) + ADAM_WD * w)
    return delta, m2, v2


def _pcall(body, name, grid, in_specs, out_specs, out_shape, scratch=(), exchange=None):
    scratch = list(scratch)
    extra = ()
    if exchange is not None:
        kinds, extra = exchange
        in_specs, out_specs, out_shape = list(in_specs), list(out_specs), list(out_shape)
        n_in, n_out, n_scr, n_x = len(in_specs), len(out_specs), len(scratch), len(extra)
        inner = body

        def body(*refs):
            ins, rest = refs[:n_in], refs[n_in:]
            x_src, rest = rest[:n_x], rest[n_x:]
            outs, rest = rest[:n_out], rest[n_out:]
            x_dst, rest = rest[:n_x], rest[n_x:]
            ex = _Exchange(kinds, x_src, x_dst, *rest[n_scr:])
            first = pl.program_id(0) == 0
            last = pl.program_id(0) == grid[0] - 1
            for a in range(1, len(grid)):
                first = first & (pl.program_id(a) == 0)
                last = last & (pl.program_id(a) == grid[a] - 1)
            pl.when(first)(ex.start)
            inner(*ins, *outs, *rest[:n_scr])
            pl.when(last)(ex.wait)

        in_specs += [_HBM] * n_x
        out_specs += [_HBM] * n_x
        out_shape += _exchange_shapes(kinds, extra)
        scratch += _exchange_sems(n_x)
    call = pl.pallas_call(
        body, name=name, grid=grid, in_specs=in_specs, out_specs=out_specs, out_shape=out_shape,
        scratch_shapes=scratch,
        compiler_params=pltpu.CompilerParams(
            dimension_semantics=("arbitrary",) * len(grid), vmem_limit_bytes=_VMEM_LIMIT))
    return lambda *operands: call(*operands, *extra)


def _full(shape):
    return pl.BlockSpec(shape, lambda *_: (0,) * len(shape))


def _rows(tm, n):
    return pl.BlockSpec((tm, n), lambda i, *_: (i, 0))


def _sds(shape, dtype=F32):
    return jax.ShapeDtypeStruct(shape, dtype)


def _peer(k, x, y, c):
    px = 1 - x if (k >> 2) & 1 else x
    py = 1 - y if (k >> 1) & 1 else y
    pc = 1 - c if k & 1 else c
    return px, py, pc


GATHER, SCATTER = "gather", "scatter"


class _Exchange:
    def __init__(self, kinds, srcs, dsts, send_sems, recv_sems, loc_sems):
        self.kinds, self.srcs, self.dsts = kinds, srcs, dsts
        self.send_sems, self.recv_sems, self.loc_sems = send_sems, recv_sems, loc_sems
        self.x, self.y, self.c = lax.axis_index("x"), lax.axis_index("y"), lax.axis_index("c")
        self.me = 4 * self.x + 2 * self.y + self.c

    def _src(self, w, slot):
        return self.srcs[w] if self.kinds[w] == GATHER else self.srcs[w].at[slot]

    def _dst(self, w, slot):
        return self.dsts[w].at[slot]

    def _copy(self, w, k, outgoing):
        px, py, pc = _peer(k, self.x, self.y, self.c)
        pid = 4 * px + 2 * py + pc
        return pltpu.make_async_remote_copy(
            src_ref=self._src(w, pid if outgoing else self.me),
            dst_ref=self._dst(w, self.me if outgoing else pid),
            send_sem=self.send_sems.at[w, k - 1], recv_sem=self.recv_sems.at[w, k - 1],
            device_id=(px, py, pc), device_id_type=MESH)

    def _local(self, w):
        return pltpu.make_async_copy(self._src(w, self.me), self._dst(w, self.me), self.loc_sems.at[w])

    def start(self):
        for w in range(len(self.srcs)):
            self._local(w).start()
            for k in range(1, N_DEV):
                self._copy(w, k, True).start()

    def wait(self):
        for w in range(len(self.srcs)):
            self._local(w).wait()
            for k in range(1, N_DEV):
                self._copy(w, k, False).wait_recv()
        for w in range(len(self.srcs)):
            for k in range(1, N_DEV):
                self._copy(w, k, True).wait_send()


def _exchange_sems(n_w):
    return [pltpu.SemaphoreType.DMA((n_w, N_DEV - 1)), pltpu.SemaphoreType.DMA((n_w, N_DEV - 1)),
            pltpu.SemaphoreType.DMA((n_w,))]


def _exchange_shapes(kinds, srcs):
    return [_sds(((N_DEV,) if kd == GATHER else ()) + tuple(s.shape), s.dtype) for kd, s in zip(kinds, srcs)]


_HBM = pl.BlockSpec(memory_space=pl.ANY)


def _cast_shards(w_in, w_uq, w_out, w_gate_t, w_up_t, w_down):
    shapes = [(D, IN_SH), (Q_RANK // N_DEV, N_HEADS * QK_DIM), (D // N_DEV, D), (2 * FF_PAD, D), (FF_PAD, D)]

    def body(win, wuq, wout, wg, wu, wd, sin_, suq, sout, sgu, sd):
        sin_[...] = win[...].astype(BF)
        suq[...] = wuq[...].astype(BF)
        sout[...] = wout[...].astype(BF)
        sgu[...] = jnp.zeros(sgu.shape, BF)
        sgu[0:FF_SH, :] = wg[...].astype(BF)
        sgu[FF_PAD:FF_PAD + FF_SH, :] = wu[...].astype(BF)
        sd[...] = jnp.zeros(sd.shape, BF)
        sd[0:FF_SH, :] = wd[...].astype(BF)

    vm = pl.BlockSpec(memory_space=pltpu.VMEM)
    return pl.pallas_call(
        body, name="cast_shards", in_specs=[vm] * 6, out_specs=[vm] * 5,
        out_shape=[_sds(s, BF) for s in shapes],
        compiler_params=pltpu.CompilerParams(vmem_limit_bytes=_VMEM_LIMIT),
    )(w_in, w_uq, w_out, w_gate_t, w_up_t, w_down)


def _gather_two_level(name, srcs):
    n_w = len(srcs)

    def body(*refs):
        src, dst = refs[:n_w], refs[n_w:2 * n_w]
        send_sems, recv_sems, loc_sems = refs[2 * n_w:]
        x, y, c = lax.axis_index("x"), lax.axis_index("y"), lax.axis_index("c")
        me, sibling = (x, y, c), (x, y, 1 - c)
        chips = [(1 - x, y), (x, 1 - y), (1 - x, 1 - y)]
        slot = lambda p: 4 * p[0] + 2 * p[1] + p[2]

        def copy(w, k, block, to, own=False):
            return pltpu.make_async_remote_copy(
                src_ref=src[w] if own else dst[w].at[slot(block)], dst_ref=dst[w].at[slot(block)],
                send_sem=send_sems.at[w, k], recv_sem=recv_sems.at[w, k], device_id=to, device_id_type=MESH)

        local = [pltpu.make_async_copy(src[w], dst[w].at[slot(me)], loc_sems.at[w]) for w in range(n_w)]
        first, passed = [], []
        for w in range(n_w):
            local[w].start()
            first.append(copy(w, 0, me, sibling, own=True))
            first += [copy(w, 1 + j, me, (*chip, c), own=True) for j, chip in enumerate(chips)]
        for cp in first:
            cp.start()
        for w in range(n_w):
            for j, chip in enumerate(chips):
                copy(w, 1 + j, (*chip, c), me).wait_recv()
                passed.append(copy(w, 4 + j, (*chip, c), sibling))
                passed[-1].start()
        for w in range(n_w):
            copy(w, 0, sibling, me).wait_recv()
            for j, chip in enumerate(chips):
                copy(w, 4 + j, (*chip, 1 - c), me).wait_recv()
        for cp in first + passed:
            cp.wait_send()
        for w in range(n_w):
            local[w].wait()

    return pl.pallas_call(
        body, name=name, in_specs=[_HBM] * n_w, out_specs=[_HBM] * n_w,
        out_shape=_exchange_shapes([GATHER] * n_w, srcs), scratch_shapes=_exchange_sems(n_w))(*srcs)


def _row_offsets(arrays):
    offs, rows = [], 0
    for a in arrays:
        offs.append(rows)
        rows += a.shape[0]
    return offs, -(-rows // 8) * 8


def _small_adam(rall, big_parts, big, ws, ms, vs):
    n_p = len(ws)
    packed = [w for p, w in enumerate(ws) if p != big] + [jax.ShapeDtypeStruct((1, HB), F32)]
    offs, _ = _row_offsets(packed)
    offs = offs[:big] + [None] + offs[big:]

    def total(ref, sl):
        g = ref[(0,) + sl]
        for j in range(1, N_DEV):
            g = g + ref[(j,) + sl]
        return g

    def body(*refs):
        rall_ref, big_ref = refs[:2]
        w_refs, m_refs, v_refs = refs[2:2 + n_p], refs[2 + n_p:2 + 2 * n_p], refs[2 + 2 * n_p:2 + 3 * n_p]
        outs = refs[2 + 3 * n_p:]
        for p in range(n_p):
            r, n = w_refs[p].shape
            if p == big:
                g = total(big_ref, (slice(0, r), slice(0, n)))
            else:
                g = total(rall_ref, (slice(offs[p], offs[p] + r), slice(0, n)))
            delta, m2, v2 = _adamw(w_refs[p][...], g, m_refs[p][...], v_refs[p][...])
            outs[p][...] = g
            outs[n_p + p][...] = delta
            outs[2 * n_p + p][...] = m2
            outs[3 * n_p + p][...] = v2
        outs[4 * n_p][...] = total(rall_ref, (slice(offs[n_p], offs[n_p] + 1), slice(0, HB)))

    vm = pl.BlockSpec(memory_space=pltpu.VMEM)
    res = pl.pallas_call(
        body, name="small_adam", in_specs=[vm] * (2 + 3 * n_p), out_specs=[vm] * (4 * n_p + 1),
        out_shape=[_sds(w.shape) for w in ws] * 4 + [_sds((1, HB))],
        compiler_params=pltpu.CompilerParams(vmem_limit_bytes=_VMEM_LIMIT),
    )(rall, big_parts, *ws, *ms, *vs)
    return res[:n_p], res[n_p:2 * n_p], res[2 * n_p:3 * n_p], res[3 * n_p:4 * n_p], res[4 * n_p]


def _device_sum(p_ref):
    g = p_ref[0].astype(F32)
    for j in range(1, N_DEV):
        g = g + p_ref[j].astype(F32)
    return g


def _shard_adam(name, parts, w, m, v, tr):
    a0, b0 = w.shape
    n_p = len(parts)
    b = parts[0].shape[2]
    first = [0]
    for p in parts:
        first.append(first[-1] + p.shape[1] // tr)

    def body(*refs):
        p_refs = refs[:n_p]
        w_ref, m_ref, v_ref, g_out, d_out, m_out, v_out = refs[n_p:]
        i = pl.program_id(0)
        g = _device_sum(p_refs[0])
        for k in range(1, n_p):
            g = jnp.where(i >= first[k], _device_sum(p_refs[k]), g)
        g = g[:, 0:b0]
        delta, m2, v2 = _adamw(w_ref[...], g, m_ref[...], v_ref[...])
        g_out[...] = g
        d_out[...] = delta
        m_out[...] = m2
        v_out[...] = v2

    def part_spec(k):
        last = first[k + 1] - first[k] - 1
        return pl.BlockSpec((N_DEV, tr, b), lambda i: (0, jnp.minimum(jnp.maximum(i - first[k], 0), last), 0))

    blk = pl.BlockSpec((tr, b0), lambda i: (i, 0))
    return _pcall(
        body, name, (a0 // tr,), [part_spec(k) for k in range(n_p)] + [blk, blk, blk],
        [blk] * 4, [_sds((a0, b0))] * 4)(*parts, w, m, v)


def _gate_up_adam(parts, ws, ms, vs):
    tc = 256

    def body(p_ref, wg, wu, mg, mu, vg, vu, *outs):
        g = _device_sum(p_ref)
        for k, (w_ref, m_ref, v_ref) in enumerate(((wg, mg, vg), (wu, mu, vu))):
            gk = g[k * FF_PAD:k * FF_PAD + FF_SH]
            delta, m2, v2 = _adamw(w_ref[...], gk, m_ref[...], v_ref[...])
            for o, val in zip(outs[4 * k:4 * k + 4], (gk, delta, m2, v2)):
                o[...] = val

    blk = pl.BlockSpec((FF_SH, tc), lambda i: (0, i))
    res = _pcall(
        body, "adam_w_gate_up", (D // tc,), [pl.BlockSpec((N_DEV, 2 * FF_PAD, tc), lambda i: (0, 0, i))] + [blk] * 6,
        [blk] * 8, [_sds((FF_SH, D))] * 8)(parts, *ws, *ms, *vs)
    return res[:4], res[4:]


def _fwd_in(x, g_pre, w_in_al, tm):
    T = x.shape[0]

    def body(x_ref, g_ref, w_ref, xm_ref, xh_ref, u_ref):
        nx, _ = _rms(x_ref[...], D)
        u = (nx * g_ref[...]).astype(BF)
        u_ref[...] = u
        xm_ref[...] = jnp.dot(u, w_ref[:, 0:MLA_IN], preferred_element_type=F32)
        xh_ref[...] = jnp.dot(u, w_ref[:, MLA_IN:XP_W], preferred_element_type=F32)

    return _pcall(body, "fwd_in", (T // tm,),
                  [_rows(tm, D), _full((1, D)), _full((D, XP_W))],
                  [_rows(tm, MLA_IN), _rows(tm, 4 * G_W), _rows(tm, D)],
                  [_sds((T, MLA_IN)), _sds((T, 4 * G_W)), _sds((T, D), BF)])(x, g_pre, w_in_al)


def _rope(blk, ta, tb1, tb2):
    return blk * ta + pltpu.roll(blk, HB - ROPE // 2, 1) * tb1 + pltpu.roll(blk, ROPE // 2, 1) * tb2


def _unrope(d, ta, tb1, tb2):
    return d * ta + pltpu.roll(d * tb1, ROPE // 2, 1) + pltpu.roll(d * tb2, HB - ROPE // 2, 1)


def _mla_prep(xp, tabs, g_q, g_kv, w_uq, w_uk, w_uv, tm):
    T = xp.shape[0]
    W = N_HEADS * HB

    def body(xp_ref, ta_ref, tb1_ref, tb2_ref, gq_ref, gkv_ref, wuq_ref, wuk_ref, wuv_ref, q_ref, qs_ref, k_ref, v_ref):
        ta, tb1, tb2 = ta_ref[...], tb1_ref[...], tb2_ref[...]
        nq, _ = _rms(xp_ref[:, 0:Q_RANK], Q_RANK)
        nkv, _ = _rms(xp_ref[:, Q_RANK:Q_RANK + KV_RANK], KV_RANK)
        nkv = (nkv * gkv_ref[...]).astype(BF)
        qpre = _dot(nq * gq_ref[...], wuq_ref[...])
        kpre = jnp.dot(nkv, wuk_ref[...], preferred_element_type=F32)
        v = jnp.dot(nkv, wuv_ref[...], preferred_element_type=F32)
        lane = lax.broadcasted_iota(jnp.int32, (tm, W), 1)
        v_ref[...] = jnp.where((lane & (HB - 1)) == V_DIM, 1.0, v).astype(BF)
        kr = _rope(pltpu.roll(xp_ref[:, Q_RANK + KV_RANK:MLA_IN], NOPE, 1), ta, tb1, tb2)
        for h in range(N_HEADS):
            sl = slice(h * HB, (h + 1) * HB)
            qr = _rope(qpre[:, sl], ta, tb1, tb2)
            q_ref[:, sl] = qr.astype(BF)
            qs_ref[:, sl] = (qr * (QK_DIM ** -0.5 * LOG2E)).astype(BF)
            k_ref[:, sl] = (kpre[:, sl] + kr).astype(BF)

    tab = _rows(tm, HB)
    return _pcall(body, "mla_prep", (T // tm,),
                  [_rows(tm, MLA_IN), tab, tab, tab, _full((1, Q_RANK)), _full((1, KV_RANK)),
                   _full((Q_RANK, W)), _full((KV_RANK, W)), _full((KV_RANK, W))],
                  [_rows(tm, W)] * 4, [_sds((T, W), BF)] * 4)(xp, *tabs, g_q, g_kv, w_uq, w_uk, w_uv)


def _flash_fwd(q, k, v, tq, exchange=None):
    T = q.shape[0]
    hp = _AH_FWD
    W = hp * HB

    def body(q_ref, k_ref, v_ref, o_ref, lse_ref):
        i = pl.program_id(1)

        def blk(j, carry, masked):
            st = pl.multiple_of(j * tq, tq)
            out = []
            for h in range(hp):
                ls = slice(h * HB, (h + 1) * HB)
                m, acc = carry[h]
                s = _dot_nt(q_ref[:, ls], k_ref[pl.ds(st, tq), ls])
                if masked:
                    r = lax.broadcasted_iota(jnp.int32, (tq, tq), 0)
                    c = lax.broadcasted_iota(jnp.int32, (tq, tq), 1)
                    s = jnp.where(c <= r, s, NEG)
                m2 = jnp.maximum(m, jnp.max(s, -1, keepdims=True))
                p = jnp.exp2(s - m2)
                out.append((m2, jnp.exp2(m - m2) * acc + _dot(p, v_ref[pl.ds(st, tq), ls])))
            return tuple(out)

        init = tuple((jnp.full((tq, 1), NEG, F32), jnp.zeros((tq, HB), F32)) for _ in range(hp))
        carry = lax.fori_loop(0, i, lambda j, cr: blk(j, cr, False), init)
        res = blk(i, carry, True)
        lane = lax.broadcasted_iota(jnp.int32, (tq, HB), 1)
        for h in range(hp):
            ls = slice(h * HB, (h + 1) * HB)
            m, acc = res[h]
            l = acc[:, V_DIM:V_DIM + 1]
            o_ref[:, ls] = jnp.where(lane < V_DIM, acc / l, 0.0).astype(BF)
            lse_ref[:, ls] = jnp.broadcast_to(m * (1.0 / LOG2E) + jnp.log(l), (tq, HB))

    qs = pl.BlockSpec((tq, W), lambda h, i: (i, h))
    kvs = pl.BlockSpec((T, W), lambda h, i: (0, h))
    return _pcall(body, "flash_fwd", (N_HEADS // hp, T // tq), [qs, kvs, kvs], [qs, qs],
                  [_sds((T, N_HEADS * HB), BF), _sds((T, N_HEADS * HB))], exchange=exchange)(q, k, v)


def _gates(hq, hf, lb):
    sig = _sigmoid(hf)
    f = lb + (1.0 - lb) * sig
    sq = _sigmoid(hq)
    return hq * sq, 1.0 - f, f, jnp.log(f), sig, sq


def _lower_bound(lbl_ref):
    l0, l1 = lbl_ref[0:1, :], lbl_ref[1:2, :]
    mx = jnp.maximum(l0, l1)
    e0, e1 = jnp.exp(l0 - mx), jnp.exp(l1 - mx)
    return e0 / (e0 + e1)


def _split3(x):
    hi = x.astype(BF)
    r1 = x - hi.astype(F32)
    mid = r1.astype(BF)
    lo = (r1 - mid.astype(F32)).astype(BF)
    return hi, mid, lo


def _tri_mm(tri, x):
    hi, mid, lo = _split3(x)
    mm = lambda t: jnp.dot(tri, t, preferred_element_type=F32)
    return mm(hi) + mm(mid) + mm(lo)


def _intra_codes(sub):
    row = lax.broadcasted_iota(jnp.int32, (CHUNK, CHUNK), 0)
    col = lax.broadcasted_iota(jnp.int32, (CHUNK, CHUNK), 1)
    return sub, row, col


def _intra(q, k, b2, b_s, codes, da=None):
    grad = da is not None
    pow2 = (lambda x: jnp.exp2(jnp.minimum(x, 0.0))) if grad else jnp.exp2
    sub, row, col = codes
    a = jnp.zeros((CHUNK, CHUNK), F32)
    dq = jnp.zeros((CHUNK, G_DIM), F32)
    dk = jnp.zeros((CHUNK, G_DIM), F32)
    for i in range(1, CHUNK // sub):
        b0 = b_s[sub * i - 1:sub * i, :]
        eq, ek = pow2(b2 - b0), pow2(b0 - b2)
        mask = ((row // sub) == i) & (col < sub * i)
        if grad:
            dai = jnp.where(mask, da, 0.0)
            dq = dq + _dot(dai, k * ek) * eq
            dk = dk + _dot_tn(dai, q * eq) * ek
        else:
            a = jnp.where(mask, _dot_nt(q * eq, k * ek), a)
    for d in range(sub):
        ksh = pltpu.roll(k, d, 0) if d else k
        bsh = pltpu.roll(b2, d, 0) if d else b2
        e = pow2(b2 - bsh)
        mask = (col == row - d) & ((row & (sub - 1)) >= d)
        if grad:
            g = jnp.sum(jnp.where(mask, da, 0.0), -1, keepdims=True) * e
            dq = dq + g * ksh
            cb = g * q
            dk = dk + (pltpu.roll(cb, CHUNK - d, 0) if d else cb)
        else:
            a = jnp.where(mask, jnp.sum(q * ksh * e, -1, keepdims=True), a)
    return (dq, dk) if grad else a


def _hgrn_fwd(xp, lb_logits, g_hn, exchange=None):
    T = xp.shape[0]
    tb = min(_TB, T)
    ncb = tb // CHUNK
    hp = _HP
    W = hp * G_DIM

    def body(hq_ref, hf_ref, hi_ref, hg_ref, lbl_ref, ghn_ref, out_ref, oraw_ref, sall_ref, aall_ref, ball_ref,
             st_ref, b_s):
        lb_all = _lower_bound(lbl_ref)

        @pl.when(pl.program_id(1) == 0)
        def _():
            st_ref[...] = jnp.zeros(st_ref.shape, F32)

        row = lax.broadcasted_iota(jnp.int32, (CHUNK, CHUNK), 0)
        col = lax.broadcasted_iota(jnp.int32, (CHUNK, CHUNK), 1)
        tri = (col <= row).astype(BF)
        codes = _intra_codes(SUB)

        def chunk(c, carry):
            sl = pl.ds(pl.multiple_of(c * CHUNK, CHUNK), CHUNK)
            for h in range(hp):
                ls = slice(h * G_DIM, (h + 1) * G_DIM)
                q, k, _, lf, _, _ = _gates(hq_ref[sl, ls], hf_ref[sl, ls], lb_all[:, ls])
                v = hi_ref[sl, ls]
                b2 = _tri_mm(tri, lf) * LOG2E
                b_s[h] = b2
                ball_ref[sl, ls] = b2
                st = st_ref[h]
                sall_ref[c, h] = st
                a = _intra(q, k, b2, b_s.at[h], codes)
                aall_ref[c, h] = a
                o = _dot_nt(q * jnp.exp2(b2), st) + _dot(a, v)
                bl = b_s[h, CHUNK - 1:CHUNK, :]
                st_ref[h] = st * jnp.exp2(bl) + _dot_tn(v, k * jnp.exp2(bl - b2))
                oraw_ref[sl, ls] = o
                n, _ = _rms(o, G_DIM)
                hg = hg_ref[sl, ls]
                out_ref[sl, ls] = n * ghn_ref[:, ls] * (hg * _sigmoid(hg))
            return carry

        lax.fori_loop(0, ncb, chunk, 0, unroll=4)

    col_blk = lambda j: pl.BlockSpec((tb, W), lambda p, t: (t, j * (G_HEADS // hp) + p))
    head = pl.BlockSpec((tb, W), lambda p, t: (t, p))
    return _pcall(
        body, "hgrn_fwd", (G_HEADS // hp, T // tb),
        [col_blk(0), col_blk(1), col_blk(2), col_blk(3),
         pl.BlockSpec((2, W), lambda p, t: (0, p)), pl.BlockSpec((1, W), lambda p, t: (0, p))],
        [head, head, pl.BlockSpec((ncb, hp, G_DIM, G_DIM), lambda p, t: (t, p, 0, 0)),
         pl.BlockSpec((ncb, hp, CHUNK, CHUNK), lambda p, t: (t, p, 0, 0)), head],
        [_sds((T, G_W)), _sds((T, G_W)), _sds((T // CHUNK, G_HEADS, G_DIM, G_DIM)),
         _sds((T // CHUNK, G_HEADS, CHUNK, CHUNK)), _sds((T, G_W))],
        scratch=[pltpu.VMEM((hp, G_DIM, G_DIM), F32), pltpu.VMEM((hp, CHUNK, G_DIM), F32)], exchange=exchange,
    )(xp, xp, xp, xp, lb_logits, g_hn)


def _fwd_out(o_pad, o_hgrn, x, g_on, w_out, g_post, g_fpre, tm):
    T = x.shape[0]

    def body(o_ref, oh_ref, x_ref, gon_ref, w_ref, gpost_ref, gfpre_ref, h1_ref, y1_ref, z_ref, mix_ref):
        for h in range(N_HEADS):
            sl = slice(h * HB, (h + 1) * HB)
            n, _ = _rms(o_ref[:, sl].astype(F32), V_DIM)
            mix_ref[:, sl] = (n * gon_ref[:, sl]).astype(BF)
        mix_ref[:, N_HEADS * HB:MIX_W] = oh_ref[...].astype(BF)
        y1 = jnp.dot(mix_ref[...], w_ref[...], preferred_element_type=F32)
        y1_ref[...] = y1
        ny, _ = _rms(y1, D)
        h1 = x_ref[...] + ny * gpost_ref[...]
        h1_ref[...] = h1
        nh, _ = _rms(h1, D)
        z_ref[...] = (nh * gfpre_ref[...]).astype(BF)

    return _pcall(body, "fwd_out", (T // tm,),
                  [_rows(tm, N_HEADS * HB), _rows(tm, G_W), _rows(tm, D), _full((1, N_HEADS * HB)),
                   _full((MIX_W, D)), _full((1, D)), _full((1, D))],
                  [_rows(tm, D), _rows(tm, D), _rows(tm, D), _rows(tm, MIX_W)],
                  [_sds((T, D)), _sds((T, D)), _sds((T, D), BF), _sds((T, MIX_W), BF)],
                  )(o_pad, o_hgrn, x, g_on, w_out, g_post, g_fpre)


def _ffn_fwd(z, wgu, wd, h1, tgt, g_fpost, tm, nd):
    T = z.shape[0]
    fb = nd * FF_PAD
    nf = wd.shape[0] // fb

    def body(z_ref, wgu_ref, wd_ref, h1_ref, t_ref, gp_ref,
             as_ref, bs_ref, ff_ref, dh2_ref, dy2_ref, dgp_ref, loss_ref, acc):
        i, j = pl.program_id(0), pl.program_id(1)
        gu = _dot_nt(z_ref[...], wgu_ref[...])
        piece = lambda n: gu[:, n * FF_PAD:(n + 1) * FF_PAD]
        g = piece(0) if nd == 1 else jnp.concatenate([piece(2 * n) for n in range(nd)], 1)
        u = piece(1) if nd == 1 else jnp.concatenate([piece(2 * n + 1) for n in range(nd)], 1)
        s = _sigmoid(g)
        b = g * s
        ff = (b * u).astype(BF)
        as_ref[...] = (u * _dsilu(g, s)).astype(BF)
        bs_ref[...] = b.astype(BF)
        ff_ref[...] = ff
        part = jnp.dot(ff, wd_ref[...], preferred_element_type=F32)

        @pl.when(j == 0)
        def _():
            acc[...] = part

        @pl.when(j > 0)
        def _():
            acc[...] += part

        @pl.when((i == 0) & (j == 0))
        def _():
            dgp_ref[...] = jnp.zeros(dgp_ref.shape, F32)
            loss_ref[...] = jnp.zeros(loss_ref.shape, F32)

        @pl.when(j == nf - 1)
        def _():
            ny, r = _rms(acc[...], D)
            err = h1_ref[...] + ny * gp_ref[...] - t_ref[...]
            loss_ref[...] += 0.5 * jnp.sum(jnp.sum(err * err, -1, keepdims=True) * (1.0 / D), 0, keepdims=True)
            dh2 = err * (1.0 / D)
            dh2_ref[...] = dh2
            dy2, dgp = _rms_bwd(ny, r, gp_ref[...], dh2, D)
            dy2_ref[...] = dy2.astype(BF)
            dgp_ref[...] += dgp

    tok = lambda n: pl.BlockSpec((tm, n), lambda i, j: (i, 0))
    col = pl.BlockSpec((tm, fb), lambda i, j: (i, j))
    return _pcall(
        body, "ffn_fwd", (T // tm, nf),
        [tok(D), pl.BlockSpec((2 * fb, D), lambda i, j: (j, 0)), pl.BlockSpec((fb, D), lambda i, j: (j, 0)),
         tok(D), tok(D), _full((1, D))],
        [col, col, col, tok(D), tok(D), _full((1, D)), _full((1, HB))],
        [_sds((T, nf * fb), BF)] * 3 + [_sds((T, D)), _sds((T, D), BF), _sds((1, D)), _sds((1, HB))],
        scratch=[pltpu.VMEM((tm, D), F32)],
    )(z, wgu, wd, h1, tgt, g_fpost)


def _dsilu(x, s):
    return s * (1.0 + x * (1.0 - s))


def _ffn_bwd_x(dy2, gs, us, wgu, wd, h1, y1, dh2, g_fpre, g_post, tm):
    T = dy2.shape[0]
    nf = wd.shape[0] // _FB

    def body(dy2_ref, gs_ref, us_ref, wgu_ref, wd_ref, h1_ref, y1_ref, dh2_ref, gf_ref, gp_ref,
             dgu_ref, dh1_ref, dy1_ref, dgf_ref, dgp_ref, acc):
        i, j = pl.program_id(0), pl.program_id(1)
        dff = _dot_nt(dy2_ref[...], wd_ref[...])
        dg = (dff * gs_ref[...].astype(F32)).astype(BF)
        du = (dff * us_ref[...].astype(F32)).astype(BF)
        dgu = jnp.concatenate([dg[:, 0:FF_PAD], du[:, 0:FF_PAD], dg[:, FF_PAD:_FB], du[:, FF_PAD:_FB]], 1)
        dgu_ref[...] = dgu
        part = jnp.dot(dgu, wgu_ref[...], preferred_element_type=F32)

        @pl.when(j == 0)
        def _():
            acc[...] = part

        @pl.when(j > 0)
        def _():
            acc[...] += part

        @pl.when((i == 0) & (j == 0))
        def _():
            dgf_ref[...] = jnp.zeros(dgf_ref.shape, F32)
            dgp_ref[...] = jnp.zeros(dgp_ref.shape, F32)

        @pl.when(j == nf - 1)
        def _():
            nh, rh = _rms(h1_ref[...], D)
            dh, dgf = _rms_bwd(nh, rh, gf_ref[...], acc[...], D)
            dh1 = dh2_ref[...] + dh
            dh1_ref[...] = dh1
            dgf_ref[...] += dgf
            ny, ry = _rms(y1_ref[...], D)
            dy1, dgp = _rms_bwd(ny, ry, gp_ref[...], dh1, D)
            dy1_ref[...] = dy1.astype(BF)
            dgp_ref[...] += dgp

    tok = lambda n: pl.BlockSpec((tm, n), lambda i, j: (i, 0))
    col = pl.BlockSpec((tm, _FB), lambda i, j: (i, j))
    return _pcall(
        body, "ffn_bwd_x", (T // tm, nf),
        [tok(D), col, col, pl.BlockSpec((2 * _FB, D), lambda i, j: (j, 0)), pl.BlockSpec((_FB, D), lambda i, j: (j, 0)),
         tok(D), tok(D), tok(D), _full((1, D)), _full((1, D))],
        [pl.BlockSpec((tm, 2 * _FB), lambda i, j: (i, j)), tok(D), tok(D), _full((1, D)), _full((1, D))],
        [_sds((T, 2 * nf * _FB), BF), _sds((T, D)), _sds((T, D), BF), _sds((1, D)), _sds((1, D))],
        scratch=[pltpu.VMEM((tm, D), F32)],
    )(dy2, gs, us, wgu, wd, h1, y1, dh2, g_fpre, g_post)


def _ffn_bwd_w(z, ffs, dgu, dy2, tm):
    T = z.shape[0]
    nf = ffs.shape[1] // _FB
    nt = T // tm

    def body(z_ref, ff_ref, dgu_ref, dy2_ref, dwgu_ref, dwd_ref, agu, ad):
        i = pl.program_id(1)
        pgu = _dot_tn(dgu_ref[...], z_ref[...])
        pd = _dot_tn(ff_ref[...], dy2_ref[...])

        @pl.when(i == 0)
        def _():
            agu[...] = pgu
            ad[...] = pd

        @pl.when(i > 0)
        def _():
            agu[...] += pgu
            ad[...] += pd

        @pl.when(i == nt - 1)
        def _():
            dwgu_ref[...] = agu[...].astype(BF)
            dwd_ref[...] = ad[...].astype(BF)

    F = nf * _FB
    tok = lambda n: pl.BlockSpec((tm, n), lambda j, i: (i, 0))
    return _pcall(
        body, "ffn_bwd_w", (nf, nt),
        [tok(D), pl.BlockSpec((tm, _FB), lambda j, i: (i, j)), pl.BlockSpec((tm, 2 * _FB), lambda j, i: (i, j)), tok(D)],
        [pl.BlockSpec((2 * _FB, D), lambda j, i: (j, 0)), pl.BlockSpec((_FB, D), lambda j, i: (j, 0))],
        [_sds((2 * F, D), BF), _sds((F, D), BF)],
        scratch=[pltpu.VMEM((2 * _FB, D), F32), pltpu.VMEM((_FB, D), F32)],
    )(z, ffs, dgu, dy2)


def _out_bwd(dy1, mix, o_pad, w_out, g_on, tm):
    T = dy1.shape[0]
    W = N_HEADS * HB

    def body(dy1_ref, mix_ref, o_ref, w_ref, gon_ref, do_ref, dl_ref, dohg_ref, dw_ref, dgon_ref):
        i = pl.program_id(0)
        dy1v = dy1_ref[...]
        dmix = _dot_nt(dy1v, w_ref[...])
        pw = _dot_tn(mix_ref[...], dy1v)

        @pl.when(i == 0)
        def _():
            dw_ref[...] = pw
            dgon_ref[...] = jnp.zeros(dgon_ref.shape, F32)

        @pl.when(i > 0)
        def _():
            dw_ref[...] += pw

        for h in range(N_HEADS):
            sl = slice(h * HB, (h + 1) * HB)
            ov = o_ref[:, sl].astype(F32)
            n, r = _rms(ov, V_DIM)
            do, dg = _rms_bwd(n, r, gon_ref[:, sl], dmix[:, sl], V_DIM)
            dgon_ref[:, sl] += dg
            do_ref[:, sl] = do.astype(BF)
            dl_ref[:, sl] = jnp.broadcast_to(jnp.sum(do * ov, -1, keepdims=True), (tm, HB))
        dohg_ref[...] = dmix[:, W:MIX_W]

    return _pcall(body, "out_bwd", (T // tm,),
                  [_rows(tm, D), _rows(tm, MIX_W), _rows(tm, W), _full((MIX_W, D)), _full((1, W))],
                  [_rows(tm, W), _rows(tm, W), _rows(tm, G_W), _full((MIX_W, D)), _full((1, W))],
                  [_sds((T, W), BF), _sds((T, W)), _sds((T, G_W)), _sds((MIX_W, D)), _sds((1, W))],
                  )(dy1, mix, o_pad, w_out, g_on)


def _flash_bwd(q, k, v, do, lse, dl, tq, exchange=None):
    T = q.shape[0]
    nq = T // tq
    scale = QK_DIM ** -0.5
    hp = _AH
    W = hp * HB

    def body(k_ref, v_ref, q_ref, do_ref, lse_ref, dl_ref, dk_ref, dv_ref, dq_ref):
        j = pl.program_id(1)

        @pl.when(j == 0)
        def _():
            dq_ref[...] = jnp.zeros(dq_ref.shape, F32)

        def blk(i, carry):
            sl = pl.ds(pl.multiple_of(i * tq, tq), tq)
            if carry is None:
                keep = (lax.broadcasted_iota(jnp.int32, (tq, tq), 1) <= lax.broadcasted_iota(jnp.int32, (tq, tq), 0))
            out = []
            for h in range(hp):
                ls = slice(h * HB, (h + 1) * HB)
                kv, vv = k_ref[:, ls], v_ref[:, ls]
                qv, dov = q_ref[sl, ls], do_ref[sl, ls]
                s = _dot_nt(qv, kv) * scale
                if carry is None:
                    s = jnp.where(keep, s, NEG)
                p = jnp.exp(s - lse_ref[sl, h * HB:h * HB + 1])
                ds = p * (_dot_nt(dov, vv) - dl_ref[sl, h * HB:h * HB + 1]) * scale
                dq_ref[sl, ls] += _dot(ds, kv)
                dk, dv = _dot_tn(ds, qv), _dot_tn(p, dov)
                out.append((dk, dv) if carry is None else (carry[h][0] + dk, carry[h][1] + dv))
            return tuple(out)

        res = lax.fori_loop(j + 1, nq, blk, blk(j, None))
        for h in range(hp):
            ls = slice(h * HB, (h + 1) * HB)
            dk_ref[:, ls] = res[h][0].astype(BF)
            dv_ref[:, ls] = res[h][1].astype(BF)

    tile = pl.BlockSpec((tq, W), lambda h, j: (j, h))
    whole = pl.BlockSpec((T, W), lambda h, j: (0, h))
    return _pcall(body, "flash_bwd", (N_HEADS // hp, nq), [tile, tile, whole, whole, whole, whole],
                  [tile, tile, whole], [_sds((T, N_HEADS * HB), BF)] * 2 + [_sds((T, N_HEADS * HB))],
                  exchange=exchange)(k, v, q, do, lse, dl)


def _mla_prep_bwd(xp, tabs, dq, dk, dv, g_q, g_kv, w_uq, w_uk, w_uv, tm):
    T = xp.shape[0]
    W = N_HEADS * HB

    def body(xp_ref, ta_ref, tb1_ref, tb2_ref, dq_ref, dk_ref, dv_ref, gq_ref, gkv_ref, wuq_ref, wuk_ref, wuv_ref,
             dxp_ref, dwuq_ref, dwuk_ref, dwuv_ref, dgq_ref, dgkv_ref, dqp):
        i = pl.program_id(0)
        ta, tb1, tb2 = ta_ref[...], tb1_ref[...], tb2_ref[...]
        nq, rq = _rms(xp_ref[:, 0:Q_RANK], Q_RANK)
        nkv, rkv = _rms(xp_ref[:, Q_RANK:Q_RANK + KV_RANK], KV_RANK)
        dkr = jnp.zeros((tm, HB), F32)
        for h in range(N_HEADS):
            sl = slice(h * HB, (h + 1) * HB)
            dqp[:, sl] = _unrope(dq_ref[:, sl], ta, tb1, tb2).astype(BF)
            dkr = dkr + dk_ref[:, sl].astype(F32)
        dkr = pltpu.roll(_unrope(dkr, ta, tb1, tb2), HB - NOPE, 1)
        lane = lax.broadcasted_iota(jnp.int32, (tm, HB), 1)
        dxp_ref[:, Q_RANK + KV_RANK:MLA_IN] = jnp.where(lane < ROPE, dkr, 0.0)
        dqpv = dqp[...]
        dkv, dvv = dk_ref[...], dv_ref[...]
        nqs = (nq * gq_ref[...]).astype(BF)
        nkvs = (nkv * gkv_ref[...]).astype(BF)
        pq, pk, pv = _dot_tn(nqs, dqpv), _dot_tn(nkvs, dkv), _dot_tn(nkvs, dvv)
        dcq, dgq = _rms_bwd(nq, rq, gq_ref[...], _dot_nt(dqpv, wuq_ref[...]), Q_RANK)
        dckv, dgkv = _rms_bwd(nkv, rkv, gkv_ref[...], _dot_nt(dkv, wuk_ref[...]) + _dot_nt(dvv, wuv_ref[...]), KV_RANK)
        dxp_ref[:, 0:Q_RANK] = dcq
        dxp_ref[:, Q_RANK:Q_RANK + KV_RANK] = dckv

        @pl.when(i == 0)
        def _():
            dwuq_ref[...] = pq
            dwuk_ref[...] = pk
            dwuv_ref[...] = pv
            dgq_ref[...] = dgq
            dgkv_ref[...] = dgkv

        @pl.when(i > 0)
        def _():
            dwuq_ref[...] += pq
            dwuk_ref[...] += pk
            dwuv_ref[...] += pv
            dgq_ref[...] += dgq
            dgkv_ref[...] += dgkv

    tab = _rows(tm, HB)
    return _pcall(
        body, "mla_prep_bwd", (T // tm,),
        [_rows(tm, MLA_IN), tab, tab, tab, _rows(tm, W), _rows(tm, W), _rows(tm, W), _full((1, Q_RANK)),
         _full((1, KV_RANK)), _full((Q_RANK, W)), _full((KV_RANK, W)), _full((KV_RANK, W))],
        [_rows(tm, MLA_IN), _full((Q_RANK, W)), _full((KV_RANK, W)), _full((KV_RANK, W)), _full((1, Q_RANK)),
         _full((1, KV_RANK))],
        [_sds((T, MLA_IN)), _sds((Q_RANK, W)), _sds((KV_RANK, W)), _sds((KV_RANK, W)), _sds((1, Q_RANK)),
         _sds((1, KV_RANK))],
        scratch=[pltpu.VMEM((tm, W), BF)],
    )(xp, *tabs, dq, dk, dv, g_q, g_kv, w_uq, w_uk, w_uv)


def _hgrn_bwd(xp, o_raw, s_all, a_all, b_all, d_out, lb_logits, g_hn, exchange=None):
    T = xp.shape[0]
    tb = min(_TB_BWD, T)
    ncb = tb // CHUNK
    nb = T // tb
    hp = _HP
    W = hp * G_DIM

    def body(hq_ref, hf_ref, hi_ref, hg_ref, o_ref, sall_ref, aall_ref, ball_ref, dout_ref, lbl_ref, ghn_ref,
             dhq_ref, dhf_ref, dhi_ref, dhg_ref, dlbl_ref, dghn_ref, dst_ref, b_s, acc_lb, acc_g):
        t = pl.program_id(1)
        lb_all = _lower_bound(lbl_ref)

        @pl.when(t == 0)
        def _():
            dst_ref[...] = jnp.zeros(dst_ref.shape, F32)
            acc_lb[...] = jnp.zeros(acc_lb.shape, F32)
            acc_g[...] = jnp.zeros(acc_g.shape, F32)

        row = lax.broadcasted_iota(jnp.int32, (CHUNK, CHUNK), 0)
        col = lax.broadcasted_iota(jnp.int32, (CHUNK, CHUNK), 1)
        tri_t = (col >= row).astype(BF)
        codes = _intra_codes(SUB)
        last = lax.broadcasted_iota(jnp.int32, (CHUNK, G_DIM), 0) == CHUNK - 1

        def chunk(cc, carry):
            c = ncb - 1 - cc
            sl = pl.ds(pl.multiple_of(c * CHUNK, CHUNK), CHUNK)
            for h in range(hp):
                ls = slice(h * G_DIM, (h + 1) * G_DIM)
                lb, ghn = lb_all[:, ls], ghn_ref[:, ls]
                hq, hg = hq_ref[sl, ls], hg_ref[sl, ls]
                q, k, f, _, sig, sq = _gates(hq, hf_ref[sl, ls], lb)
                v = hi_ref[sl, ls]
                b2 = ball_ref[sl, ls]
                b_s[h] = b2
                st = sall_ref[c, h]
                dstn = dst_ref[h]
                o = o_ref[sl, ls]
                dout = dout_ref[sl, ls]
                n, r = _rms(o, G_DIM)
                sg = _sigmoid(hg)
                dhg_ref[sl, ls] = dout * (n * ghn) * _dsilu(hg, sg)
                do, dg = _rms_bwd(n, r, ghn, dout * (hg * sg), G_DIM)
                acc_g[:, ls] += dg
                eb = jnp.exp2(b2)
                bl = b_s[h, CHUNK - 1:CHUNK, :]
                ebl = jnp.exp2(bl)
                ekd = jnp.exp2(bl - b2)
                kd = k * ekd
                a = aall_ref[c, h]
                dq_i, dk_i = _intra(q, k, b2, b_s.at[h], codes, _dot_nt(do, v))
                dhi_ref[sl, ls] = _dot_tn(a, do) + _dot_nt(kd, dstn)
                dk_state = _dot(v, dstn) * ekd
                dq = dq_i + _dot(do, st) * eb
                dk = dk_i + dk_state
                dbl = jnp.sum(k * dk_state, 0, keepdims=True) + ebl * jnp.sum(dstn * st, 0, keepdims=True)
                db = q * dq - k * dk + jnp.where(last, dbl, 0.0)
                df = _tri_mm(tri_t, db) / f - dk
                dhf_ref[sl, ls] = df * (1.0 - lb) * sig * (1.0 - sig)
                acc_lb[:, ls] += jnp.sum(df * (1.0 - sig), 0, keepdims=True)
                dhq_ref[sl, ls] = dq * _dsilu(hq, sq)
                dst_ref[h] = dstn * ebl + _dot_tn(do, q * eb)
            return carry

        lax.fori_loop(0, ncb, chunk, 0, unroll=4)

        @pl.when(t == nb - 1)
        def _():
            dl0 = acc_lb[...] * lb_all * (1.0 - lb_all)
            dlbl_ref[0:1, :] = dl0
            dlbl_ref[1:2, :] = -dl0
            dghn_ref[...] = acc_g[...]

    col_blk = lambda j: pl.BlockSpec((tb, W), lambda p, t: (nb - 1 - t, j * (G_HEADS // hp) + p))
    head = pl.BlockSpec((tb, W), lambda p, t: (nb - 1 - t, p))
    two = pl.BlockSpec((2, W), lambda p, t: (0, p))
    one = pl.BlockSpec((1, W), lambda p, t: (0, p))
    res = _pcall(
        body, "hgrn_bwd", (G_HEADS // hp, nb),
        [col_blk(0), col_blk(1), col_blk(2), col_blk(3), head,
         pl.BlockSpec((ncb, hp, G_DIM, G_DIM), lambda p, t: (nb - 1 - t, p, 0, 0)),
         pl.BlockSpec((ncb, hp, CHUNK, CHUNK), lambda p, t: (nb - 1 - t, p, 0, 0)), head, head, two, one],
        [head, head, head, head, two, one],
        [_sds((T, G_W))] * 4 + [_sds((2, G_W)), _sds((1, G_W))],
        scratch=[pltpu.VMEM((hp, G_DIM, G_DIM), F32), pltpu.VMEM((hp, CHUNK, G_DIM), F32),
                 pltpu.VMEM((1, W), F32), pltpu.VMEM((1, W), F32)], exchange=exchange,
    )(xp, xp, xp, xp, o_raw, s_all, a_all, b_all, d_out, lb_logits, g_hn)
    return res


def _in_bwd_x(x, dxp_m, dxp_h, dh1, w_in_al, g_pre, vecs, tm, exchange=None):
    T = x.shape[0]
    nt = T // tm
    n_v = len(vecs)
    offs, rows = _row_offsets([g_pre] + list(vecs))

    def body(*refs):
        x_ref, dm_ref, d0_ref, d1_ref, d2_ref, d3_ref, dh1_ref, w_ref, g_ref = refs[:9]
        v_refs = refs[9:9 + n_v]
        dx_ref, dg_ref, rall, pk, send_sems, recv_sems, loc_sem = refs[9 + n_v:]
        i = pl.program_id(0)
        du = _dot_nt(dm_ref[...], w_ref[:, 0:MLA_IN])
        for j, d_ref in enumerate((d0_ref, d1_ref, d2_ref, d3_ref)):
            du = du + _dot_nt(d_ref[...], w_ref[:, MLA_IN + j * G_W:MLA_IN + (j + 1) * G_W])
        nx, r = _rms(x_ref[...], D)
        dx, dg = _rms_bwd(nx, r, g_ref[...], du, D)
        dx_ref[...] = dh1_ref[...] + dx

        @pl.when(i == 0)
        def _():
            dg_ref[...] = dg

        @pl.when(i > 0)
        def _():
            dg_ref[...] += dg

        @pl.when(i == nt - 1)
        def _():
            mx, my, mc = lax.axis_index("x"), lax.axis_index("y"), lax.axis_index("c")
            me = 4 * mx + 2 * my + mc
            pk[...] = jnp.zeros(pk.shape, F32)
            for p, v_ref in enumerate((dg_ref,) + tuple(v_refs)):
                vr, vn = v_ref.shape
                pk[offs[p]:offs[p] + vr, 0:vn] = v_ref[...]

            def copy(k, outgoing):
                px, py, pc = _peer(k, mx, my, mc)
                return pltpu.make_async_remote_copy(
                    src_ref=pk, dst_ref=rall.at[me if outgoing else 4 * px + 2 * py + pc],
                    send_sem=send_sems.at[k - 1], recv_sem=recv_sems.at[k - 1],
                    device_id=(px, py, pc), device_id_type=MESH)

            local = pltpu.make_async_copy(pk, rall.at[me], loc_sem)
            local.start()
            for k in range(1, N_DEV):
                copy(k, True).start()
            local.wait()
            for k in range(1, N_DEV):
                copy(k, False).wait_recv()
            for k in range(1, N_DEV):
                copy(k, True).wait_send()

    return _pcall(body, "in_bwd_x", (nt,),
                  [_rows(tm, D), _rows(tm, MLA_IN)] + [_rows(tm, G_W)] * 4 + [_rows(tm, D), _full((D, XP_W)), _full((1, D))]
                  + [_full(v.shape) for v in vecs],
                  [_rows(tm, D), _full((1, D)), _HBM], [_sds((T, D)), _sds((1, D)), _sds((N_DEV, rows, D))],
                  scratch=[pltpu.VMEM((rows, D), F32), pltpu.SemaphoreType.DMA((N_DEV - 1,)),
                           pltpu.SemaphoreType.DMA((N_DEV - 1,)), pltpu.SemaphoreType.DMA],
                  exchange=exchange)(x, dxp_m, *dxp_h, dh1, w_in_al, g_pre, *vecs)


def _aligned_col(c):
    return jnp.where(c < Q_RANK + KV_RANK + ROPE, c, c + (KR_PAD - ROPE))


def _align_w_in(g_in):
    tile = 384
    kr_end = Q_RANK + KV_RANK + ROPE

    def body(g_ref, o_ref, gp):
        gp[...] = jnp.zeros(gp.shape, BF)
        for j in range(N_DEV):
            gp[j, :, 0:IN_SH] = g_ref[j]
        r = lax.broadcasted_iota(jnp.int32, (tile, tile), 0)
        c = lax.broadcasted_iota(jnp.int32, (tile, tile), 1)
        for t in range(XP_W // tile):
            lo, hi = t * tile, (t + 1) * tile
            cols = [a if a < kr_end else a - (KR_PAD - ROPE) for a in (lo, hi - 1)]
            acc = jnp.zeros((D, tile), F32)
            for j in range(cols[0] // IN_SH, cols[-1] // IN_SH + 1):
                sel = (r < IN_SH) & (_aligned_col(j * IN_SH + r) == lo + c)
                acc = acc + jnp.dot(gp[j], sel.astype(BF), preferred_element_type=F32)
            o_ref[:, lo:hi] = acc.astype(BF)

    vm = pl.BlockSpec(memory_space=pltpu.VMEM)
    return pl.pallas_call(
        body, name="align_w_in", in_specs=[vm], out_specs=vm, out_shape=_sds((D, XP_W), BF),
        scratch_shapes=[pltpu.VMEM((N_DEV, D, tile), BF)],
        compiler_params=pltpu.CompilerParams(vmem_limit_bytes=_VMEM_LIMIT))(g_in)


def _in_bwd_w(name, u, dxp_m, dxp_h, tm, row0, nr, exchange=None):
    T = u.shape[0]
    nt = T // tm
    win = 640

    def body(u_ref, dm_ref, d0_ref, d1_ref, d2_ref, d3_ref, o_ref, acc):
        i = pl.program_id(0)
        ut = u_ref[:, row0:row0 + nr].T
        parts = [(0, MLA_IN, dm_ref)] + [(MLA_IN + j * G_W, G_W, d) for j, d in enumerate((d0_ref, d1_ref, d2_ref, d3_ref))]

        @pl.when(i == 0)
        def _():
            for lo, n, d in parts:
                acc[:, lo:lo + n] = jnp.dot(ut, d[...].astype(BF), preferred_element_type=F32)

        @pl.when(i > 0)
        def _():
            for lo, n, d in parts:
                acc[:, lo:lo + n] += jnp.dot(ut, d[...].astype(BF), preferred_element_type=F32)

        @pl.when(i == nt - 1)
        def _():
            wide = 384
            r = lax.broadcasted_iota(jnp.int32, (win, wide), 0)
            c = lax.broadcasted_iota(jnp.int32, (win, wide), 1)
            kr_end = Q_RANK + KV_RANK + ROPE
            for j in range(N_DEV):
                first = j * IN_SH if j * IN_SH < kr_end else j * IN_SH + (KR_PAD - ROPE)
                lo = min(first // HB * HB, XP_W - win)
                sel = (c < IN_SH) & (_aligned_col(j * IN_SH + c) == lo + r)
                res = jnp.dot(acc[:, lo:lo + win].astype(BF), sel.astype(BF), preferred_element_type=F32)
                o_ref[j] = res[:, 0:IN_SH].astype(BF)

    return _pcall(body, name, (nt,),
                  [_rows(tm, D), _rows(tm, MLA_IN)] + [_rows(tm, G_W)] * 4,
                  [_full((N_DEV, nr, IN_SH))], [_sds((N_DEV, nr, IN_SH), BF)],
                  scratch=[pltpu.VMEM((nr, XP_W), F32)], exchange=exchange)(u, dxp_m, *dxp_h)


def _pad_heads(w, width, real):
    lead = w.shape[:-1]
    w = w.reshape(lead + (N_HEADS, real))
    w = jnp.pad(w, [(0, 0)] * len(lead) + [(0, 0), (0, width - real)])
    return w.reshape(lead + (N_HEADS * width,))


def _unpad_heads(w, width, real):
    lead = w.shape[:-1]
    return w.reshape(lead + (N_HEADS, width))[..., :real].reshape(lead + (N_HEADS * real,))


def _rope_tables(positions):
    half = ROPE // 2
    inv_freq = 1.0 / (ROPE_THETA ** (jnp.arange(0, ROPE, 2, dtype=F32) / ROPE))
    ang = positions.astype(F32)[:, None] * inv_freq
    cos, sin = jnp.cos(ang), jnp.sin(ang)
    T = positions.shape[0]
    z = lambda n: jnp.zeros((T, n), F32)
    ta = jnp.concatenate([jnp.ones((T, NOPE), F32), cos, cos, z(HB - QK_DIM)], 1)
    tb1 = jnp.concatenate([z(NOPE), -sin, z(half), z(HB - QK_DIM)], 1)
    tb2 = jnp.concatenate([z(NOPE), z(half), sin, z(HB - QK_DIM)], 1)
    return ta, tb1, tb2


def kernel(x, positions, attn_pre_norm, w_in, mla_q_norm, mla_w_uq, mla_kv_norm, mla_w_ukv, mla_out_norm, hgrn_lb_logits, hgrn_out_norm, w_out, attn_post_norm, ffn_pre_norm, w_gate, w_up, w_down, ffn_post_norm, loss_target, m_attn_pre_norm, m_w_in, m_mla_q_norm, m_mla_w_uq, m_mla_kv_norm, m_mla_w_ukv, m_mla_out_norm, m_hgrn_lb_logits, m_hgrn_out_norm, m_w_out, m_attn_post_norm, m_ffn_pre_norm, m_w_gate, m_w_up, m_w_down, m_ffn_post_norm, v_attn_pre_norm, v_w_in, v_mla_q_norm, v_mla_w_uq, v_mla_kv_norm, v_mla_w_ukv, v_mla_out_norm, v_hgrn_lb_logits, v_hgrn_out_norm, v_w_out, v_attn_post_norm, v_ffn_pre_norm, v_w_gate, v_w_up, v_w_down, v_ffn_post_norm):
    T = x.shape[1]
    tm = min(_TM, T)
    tq = min(_TQ, T)
    xs, tgt = x[0], loss_target[0]
    uq_sh = (Q_RANK // N_DEV, N_HEADS * QK_DIM)

    b_in, b_uq, b_out, b_gu, b_d = _cast_shards(
        w_in[0], mla_w_uq[0].reshape(uq_sh), w_out[0], w_gate[0].T, w_up[0].T, w_down[0])
    g_in, g_uq = _gather_two_level("ag_first", [b_in, b_uq])
    w_in_al = _align_w_in(g_in)
    w_uq_p = _pad_heads(g_uq.reshape(Q_RANK, N_HEADS * QK_DIM), HB, QK_DIM)
    w_ukv = mla_w_ukv[0].astype(BF)
    w_uk_p = _pad_heads(w_ukv[..., :NOPE].reshape(KV_RANK, N_HEADS * NOPE), HB, NOPE)
    w_uv_p = _pad_heads(w_ukv[..., NOPE:].reshape(KV_RANK, N_HEADS * V_DIM), HB, V_DIM)
    g_on_p = _pad_heads(mla_out_norm, HB, V_DIM)
    tabs = _rope_tables(positions[0])

    xp_m, xp_h, u = _fwd_in(xs, attn_pre_norm, w_in_al, tm)
    q_att, qs_att, k_att, v_att = _mla_prep(xp_m, tabs, mla_q_norm, mla_kv_norm, w_uq_p, w_uk_p, w_uv_p, tm)
    o_hgrn, o_raw, s_all, a_all, b_all, wd = _hgrn_fwd(xp_h, hgrn_lb_logits, hgrn_out_norm, ([GATHER], [b_d]))
    wd = wd.reshape(N_DEV * FF_PAD, D)
    o_pad, lse, wgu, g_out = _flash_fwd(qs_att, k_att, v_att, tq, ([GATHER, GATHER], [b_gu, b_out]))
    wgu = wgu.reshape(N_DEV * 2 * FF_PAD, D)
    w_out_full = g_out.reshape(D, D)
    w_out_mla = jnp.pad(w_out_full[:N_HEADS * V_DIM].reshape(N_HEADS, V_DIM, D), ((0, 0), (0, HB - V_DIM), (0, 0)))
    w_out_p = jnp.concatenate([w_out_mla.reshape(N_HEADS * HB, D), w_out_full[N_HEADS * V_DIM:]], 0)
    h1, y1, z, mix = _fwd_out(o_pad, o_hgrn, xs, g_on_p, w_out_p, attn_post_norm, ffn_pre_norm, tm)
    tmf = min(_TMF, T)
    gs, us, ffs, dh2, dy2, d_fpost, loss_row = _ffn_fwd(z, wgu, wd, h1, tgt, ffn_post_norm, tm, _FB // FF_PAD)

    dgu, dh1, dy1, d_fpre, d_post = _ffn_bwd_x(dy2, gs, us, wgu, wd, h1, y1, dh2, ffn_pre_norm, attn_post_norm, tm)
    dwgu, dwd = _ffn_bwd_w(z, ffs, dgu, dy2, tmf)
    do_pad, dl, d_ohg, dw_out_p, d_on_p = _out_bwd(dy1, mix, o_pad, w_out_p, g_on_p, tm)
    dw_out_mla = dw_out_p[:N_HEADS * HB].reshape(N_HEADS, HB, D)[:, :V_DIM].reshape(N_HEADS * V_DIM, D)
    dw_out = jnp.concatenate([dw_out_mla, dw_out_p[N_HEADS * HB:]], 0).reshape(N_DEV, D // N_DEV, D).astype(BF)
    dk_att, dv_att, dq_att, p_gu, p_d, p_out = _flash_bwd(
        q_att, k_att, v_att, do_pad, lse, dl, tq,
        ([SCATTER] * 3, [dwgu.reshape(N_DEV, 2 * FF_PAD, D), dwd.reshape(N_DEV, FF_PAD, D), dw_out]))
    dxp_m, dw_uq_p, dw_uk_p, dw_uv_p, d_gq, d_gkv = _mla_prep_bwd(
        xp_m, tabs, dq_att, dk_att, dv_att, mla_q_norm, mla_kv_norm, w_uq_p, w_uk_p, w_uv_p, tm)
    dw_uq = _unpad_heads(dw_uq_p, HB, QK_DIM).reshape((N_DEV,) + uq_sh).astype(BF)
    dw_ukv = jnp.concatenate([_unpad_heads(dw_uk_p, HB, NOPE).reshape(KV_RANK, N_HEADS, NOPE),
                              _unpad_heads(dw_uv_p, HB, V_DIM).reshape(KV_RANK, N_HEADS, V_DIM)], -1)
    *dxp_h, d_lbl, d_ghn, p_uq, dw_ukv_all = _hgrn_bwd(
        xp_h, o_raw, s_all, a_all, b_all, d_ohg, hgrn_lb_logits, hgrn_out_norm,
        ([SCATTER, GATHER], [dw_uq, dw_ukv.reshape(KV_RANK, N_HEADS * HB)]))
    dw_in_a, = _in_bwd_w("in_bwd_w_a", u, dxp_m, dxp_h, tm, 0, _IN_ROWS_A)
    dw_in_b, p_in_a = _in_bwd_w("in_bwd_w_b", u, dxp_m, dxp_h, tm, _IN_ROWS_A, D - _IN_ROWS_A, ([SCATTER], [dw_in_a]))
    d_on = _unpad_heads(d_on_p, HB, V_DIM)
    vecs = [d_gq, d_gkv, d_on, d_lbl, d_ghn, d_post, d_fpre, d_fpost, loss_row]
    grad_x, _, rall, p_in_b = _in_bwd_x(xs, dxp_m, dxp_h, dh1, w_in_al, attn_pre_norm, vecs, tm, ([SCATTER], [dw_in_b]))

    ukv2 = lambda a: a.reshape(KV_RANK, N_HEADS * HB)
    small_w = [attn_pre_norm, mla_q_norm, mla_kv_norm, ukv2(mla_w_ukv), mla_out_norm, hgrn_lb_logits, hgrn_out_norm,
               attn_post_norm, ffn_pre_norm, ffn_post_norm]
    small_m = [m_attn_pre_norm, m_mla_q_norm, m_mla_kv_norm, ukv2(m_mla_w_ukv), m_mla_out_norm, m_hgrn_lb_logits,
               m_hgrn_out_norm, m_attn_post_norm, m_ffn_pre_norm, m_ffn_post_norm]
    small_v = [v_attn_pre_norm, v_mla_q_norm, v_mla_kv_norm, ukv2(v_mla_w_ukv), v_mla_out_norm, v_hgrn_lb_logits,
               v_hgrn_out_norm, v_attn_post_norm, v_ffn_pre_norm, v_ffn_post_norm]
    s_g, s_d, s_m, s_v, loss_all = _small_adam(rall, dw_ukv_all, 3, small_w, small_m, small_v)
    r_in = _shard_adam("adam_w_in", [p_in_a, p_in_b], w_in[0], m_w_in[0], v_w_in[0], 128)
    r_uq = _shard_adam("adam_w_uq", [p_uq], mla_w_uq[0].reshape(uq_sh), m_mla_w_uq[0].reshape(uq_sh),
                       v_mla_w_uq[0].reshape(uq_sh), uq_sh[0])
    r_out = _shard_adam("adam_w_out", [p_out], w_out[0], m_w_out[0], v_w_out[0], D // N_DEV)
    r_g, r_u = _gate_up_adam(p_gu, (w_gate[0].T, w_up[0].T), (m_w_gate[0].T, m_w_up[0].T),
                             (v_w_gate[0].T, v_w_up[0].T))
    r_g, r_u = [a.T for a in r_g], [a.T for a in r_u]
    r_d = _shard_adam("adam_w_down", [p_d], w_down[0], m_w_down[0], v_w_down[0], FF_SH // 2)

    loss = loss_all[0, 0]

    def assemble(big, small):
        b_in, b_uq, b_out, b_g, b_u, b_d = big
        return [small[0], b_in[None], small[1], b_uq.reshape(mla_w_uq.shape), small[2],
                small[3].reshape(mla_w_ukv.shape), small[4], small[5], small[6], b_out[None], small[7], small[8],
                b_g[None], b_u[None], b_d[None], small[9]]

    outs = [loss, grad_x[None]]
    for idx, small in enumerate((s_g, s_d, s_m, s_v)):
        outs += assemble([r[idx] for r in (r_in, r_uq, r_out, r_g, r_u, r_d)], small)
    return tuple(outs)
```
